```python
import math
import jax, jax.numpy as jnp
from jax import lax
import numpy as np

D_MODEL = 1024
BATCH = 8
SEQ = 2048
DEPTH = 4

RMS_EPS = 1e-6
NEG_INF = -1e30

MLA_HEADS = 16
MLA_Q_LORA = 512
MLA_KV_LORA = 256
MLA_NOPE = 64
MLA_ROPE = 32
MLA_V = 64
MLA_ROPE_THETA = 10000.0
ATTN_BLOCK = 128

DIL_PATTERN = ((128, 1), (512, 4), (2048, 16))
DIL_GROUPS = 3
DIL_HEADS = 4
DIL_HEAD_DIM = 64
ROPE_THETA = 500000.0
PARTIAL_ROT = DIL_HEAD_DIM // 4

MLA_COLS = MLA_Q_LORA + MLA_KV_LORA + MLA_ROPE
DIL_COLS = DIL_GROUPS * 3 * DIL_HEADS * DIL_HEAD_DIM
GATE_COLS = 2 * D_MODEL
IN_COLS = MLA_COLS + DIL_COLS + GATE_COLS
MLA_OUT = MLA_HEADS * MLA_V
DIL_OUT = DIL_HEADS * DIL_HEAD_DIM

N_GROUPS = 8
EXPERTS_PER_GROUP = 8
N_EXPERTS = N_GROUPS * EXPERTS_PER_GROUP
TOP_K = 2
EXPERT_FF = 256
EXPERT_BLOCK = 128

PLE_DIM = 256

kernel_name = 'hybrid_mla_dilated_hmoe_block'


def rms_norm(x, g):
    xf = x.astype(jnp.float32)
    y = xf * lax.rsqrt(jnp.mean(xf * xf, axis=-1, keepdims=True) + RMS_EPS)
    return (y * g.astype(jnp.float32)).astype(x.dtype)


def apply_rope(x, pos, theta, rot_dim):
    half = rot_dim // 2
    inv = jnp.float32(theta) ** (-jnp.arange(half, dtype=jnp.float32) * 2.0 / rot_dim)
    ang = pos.astype(jnp.float32)[..., None] * inv
    cos = jnp.cos(ang)[:, :, None, :].astype(x.dtype)
    sin = jnp.sin(ang)[:, :, None, :].astype(x.dtype)
    x1 = x[..., :half]
    x2 = x[..., half:rot_dim]
    return jnp.concatenate([x1 * cos - x2 * sin, x1 * sin + x2 * cos, x[..., rot_dim:]], axis=-1)


def causal_block_attention(q, k, v, scale):
    B, S, H, Dq = q.shape
    Dv = v.shape[-1]
    nb = S // ATTN_BLOCK
    qb = q.reshape(B, nb, ATTN_BLOCK, H, Dq).transpose(1, 0, 2, 3, 4)
    kpos = jnp.arange(S)

    def one_block(args):
        qi, b = args
        s = jnp.einsum('bqhd,bkhd->bhqk', qi, k).astype(jnp.float32) * scale
        qpos = b * ATTN_BLOCK + jnp.arange(ATTN_BLOCK)
        s = jnp.where(kpos[None, :] <= qpos[:, None], s, NEG_INF)
        pr = jax.nn.softmax(s, axis=-1).astype(v.dtype)
        return jnp.einsum('bhqk,bkhd->bqhd', pr, v)

    out = lax.map(one_block, (qb, jnp.arange(nb)))
    return out.transpose(1, 0, 2, 3, 4).reshape(B, S, H, Dv)


def dilated_window_attention(q, k, v, dilation, window):
    B, S, H, Dh = q.shape
    nback = window // dilation
    L = S // dilation
    nb = -(-L // nback)
    Lp = nb * nback

    def to_blocks(t):
        t = t.reshape(B, L, dilation, H, Dh).transpose(0, 2, 1, 3, 4)
        t = jnp.pad(t, ((0, 0), (0, 0), (0, Lp - L), (0, 0), (0, 0)))
        return t.reshape(B, dilation, nb, nback, H, Dh)

    def with_prev(t):
        prev = jnp.pad(t[:, :, :-1], ((0, 0), (0, 0), (1, 0), (0, 0), (0, 0), (0, 0)))
        return jnp.concatenate([prev, t], axis=3)

    qb = to_blocks(q)
    kk = with_prev(to_blocks(k))
    vv = with_prev(to_blocks(v))
    s = jnp.einsum('brnqhd,brnkhd->brnhqk', qb, kk).astype(jnp.float32) * (Dh ** -0.5)
    qi = jnp.arange(nback)[:, None]
    kj = jnp.arange(2 * nback)[None, :]
    band = (kj >= qi) & (kj <= qi + nback)
    has_prev = jnp.arange(nb)[:, None, None] > 0
    mask = band[None] & (has_prev | (kj >= nback)[None])
    s = jnp.where(mask[None, None, :, None], s, NEG_INF)
    m = jnp.max(s, axis=-1, keepdims=True)
    e = jnp.exp(s - m)
    den = jnp.sum(e, axis=-1, keepdims=True)
    o = jnp.einsum('brnhqk,brnkhd->brnqhd', (e / den).astype(v.dtype), vv)
    lse = (m + jnp.log(den))[..., 0]
    o = o.reshape(B, dilation, Lp, H, Dh)[:, :, :L].transpose(0, 2, 1, 3, 4).reshape(B, S, H, Dh)
    lse = lse.transpose(0, 1, 2, 4, 3).reshape(B, dilation, Lp, H)[:, :, :L]
    lse = lse.transpose(0, 2, 1, 3).reshape(B, S, H)
    return o, lse


def routed_expert_ffn(hf, expert_idx, gate_w, w1, w3, w2):
    T, D = hf.shape
    A = T * TOP_K
    flat_e = expert_idx.reshape(A)
    flat_tok = jnp.arange(A, dtype=jnp.int32) // TOP_K
    flat_w = gate_w.reshape(A)
    order = jnp.argsort(flat_e)
    e_s = flat_e[order]
    tok_s = flat_tok[order]
    w_s = flat_w[order]
    counts = jnp.bincount(flat_e, length=N_EXPERTS)
    starts = jnp.cumsum(counts) - counts
    pcounts = (counts + EXPERT_BLOCK - 1) // EXPERT_BLOCK * EXPERT_BLOCK
    pends = jnp.cumsum(pcounts)
    pstarts = pends - pcounts
    dest = pstarts[e_s] + (jnp.arange(A, dtype=jnp.int32) - starts[e_s])
    n_blocks = -(-A // EXPERT_BLOCK) + N_EXPERTS
    n_rows = n_blocks * EXPERT_BLOCK
    row_tok = jnp.zeros((n_rows,), jnp.int32).at[dest].set(tok_s)
    xs = hf[row_tok].reshape(n_blocks, EXPERT_BLOCK, D)
    block_e = jnp.minimum(jnp.searchsorted(pends, jnp.arange(n_blocks) * EXPERT_BLOCK, side='right'), N_EXPERTS - 1)

    def one_block(args):
        xb, e = args
        return (jax.nn.silu(xb @ w1[e]) * (xb @ w3[e])) @ w2[e]

    ys = lax.map(one_block, (xs, block_e)).reshape(n_rows, D)
    contrib = ys[dest] * w_s[:, None].astype(hf.dtype)
    return jnp.zeros_like(hf).at[tok_s].add(contrib)


def hier_moe(h, w_grp, b_grp, w_exp, b_exp, w1, w3, w2):
    B, S, D = h.shape
    T = B * S
    hf = h.reshape(T, D)
    g_logits = (hf @ w_grp).astype(jnp.float32) + b_grp.astype(jnp.float32)
    g_prob = jax.nn.softmax(g_logits, axis=-1)
    g_sel = jnp.argmax(g_logits, axis=-1).astype(jnp.int32)
    e_logits = (hf @ w_exp).astype(jnp.float32).reshape(T, N_GROUPS, EXPERTS_PER_GROUP) + b_exp.astype(jnp.float32)
    idx = jnp.broadcast_to(g_sel[:, None, None], (T, 1, EXPERTS_PER_GROUP))
    e_in = jnp.take_along_axis(e_logits, idx, axis=1)[:, 0]
    top_v, top_i = lax.top_k(e_in, TOP_K)
    p_g = jnp.take_along_axis(g_prob, g_sel[:, None], axis=1)
    gate_w = jax.nn.softmax(top_v, axis=-1) * p_g
    expert_idx = g_sel[:, None] * EXPERTS_PER_GROUP + top_i.astype(jnp.int32)
    return routed_expert_ffn(hf, expert_idx, gate_w, w1, w3, w2).reshape(B, S, D)


def setup_inputs(seed: int = 0) -> dict:
    key = jax.random.key(seed)
    ks = jax.random.split(key, 32)
    f32 = jnp.float32

    def w(k, shape, fan_in):
        return jax.random.normal(k, shape, f32) * (fan_in ** -0.5)

    def gain(k, shape):
        return 1.0 + 0.02 * jax.random.normal(k, shape, f32)

    return {
        'x': jax.random.normal(ks[0], (BATCH, SEQ, D_MODEL), f32),
        'p': jax.random.normal(ks[1], (DEPTH, BATCH, SEQ, PLE_DIM), f32),
        'positions': jnp.broadcast_to(jnp.arange(SEQ, dtype=jnp.int32), (BATCH, SEQ)),
        'g_mix': gain(ks[2], (DEPTH, D_MODEL)),
        'w_in': w(ks[3], (DEPTH, D_MODEL, IN_COLS), D_MODEL),
        'g_q_lat': gain(ks[4], (DEPTH, MLA_Q_LORA)),
        'w_q_up': w(ks[5], (DEPTH, MLA_Q_LORA, MLA_HEADS * (MLA_NOPE + MLA_ROPE)), MLA_Q_LORA),
        'g_kv_lat': gain(ks[6], (DEPTH, MLA_KV_LORA)),
        'w_kv_up': w(ks[7], (DEPTH, MLA_KV_LORA, MLA_HEADS * (MLA_NOPE + MLA_V)), MLA_KV_LORA),
        'w_branch_a': w(ks[8], (DEPTH, MLA_OUT, D_MODEL), MLA_OUT),
        'w_branch_b': w(ks[9], (DEPTH, DIL_OUT, D_MODEL), DIL_OUT),
        'w_out': w(ks[10], (DEPTH, D_MODEL, D_MODEL), D_MODEL),
        'g_ffn': gain(ks[11], (DEPTH, D_MODEL)),
        'w_router_grp': w(ks[12], (DEPTH, D_MODEL, N_GROUPS), D_MODEL),
        'b_router_grp': 0.01 * jax.random.normal(ks[13], (DEPTH, N_GROUPS), f32),
        'w_router_exp': w(ks[14], (DEPTH, D_MODEL, N_EXPERTS), D_MODEL),
        'b_router_exp': 0.01 * jax.random.normal(ks[15], (DEPTH, N_GROUPS, EXPERTS_PER_GROUP), f32),
        'w_exp_gate': w(ks[16], (DEPTH, N_EXPERTS, D_MODEL, EXPERT_FF), D_MODEL),
        'w_exp_up': w(ks[17], (DEPTH, N_EXPERTS, D_MODEL, EXPERT_FF), D_MODEL),
        'w_exp_down': w(ks[18], (DEPTH, N_EXPERTS, EXPERT_FF, D_MODEL), EXPERT_FF),
        'g_ple': gain(ks[19], (DEPTH, D_MODEL)),
        'w_ple_gate': w(ks[20], (DEPTH, D_MODEL, D_MODEL), D_MODEL),
        'w_ple_proj': w(ks[21], (DEPTH, PLE_DIM, D_MODEL), PLE_DIM),
        'g_final': gain(ks[22], (D_MODEL,)),
    }


def reference(x, p, positions, g_mix, w_in, g_q_lat, w_q_up, g_kv_lat, w_kv_up, w_branch_a, w_branch_b, w_out, g_ffn, w_router_grp, b_router_grp, w_router_exp, b_router_exp, w_exp_gate, w_exp_up, w_exp_down, g_ple, w_ple_gate, w_ple_proj, g_final):
    B, S, D = x.shape
    splits = [MLA_Q_LORA, MLA_Q_LORA + MLA_KV_LORA, MLA_COLS, MLA_COLS + DIL_COLS]
    for i in range(DEPTH):
        h = rms_norm(x, g_mix[i])
        proj = h @ w_in[i]
        c_q, c_kv, k_pe, dil, gates = jnp.split(proj, splits, axis=-1)

        q = (rms_norm(c_q, g_q_lat[i]) @ w_q_up[i]).reshape(B, S, MLA_HEADS, MLA_NOPE + MLA_ROPE)
        q = jnp.concatenate([q[..., :MLA_NOPE], apply_rope(q[..., MLA_NOPE:], positions, MLA_ROPE_THETA, MLA_ROPE)], axis=-1)
        kv = (rms_norm(c_kv, g_kv_lat[i]) @ w_kv_up[i]).reshape(B, S, MLA_HEADS, MLA_NOPE + MLA_V)
        k_rot = apply_rope(k_pe[:, :, None, :], positions, MLA_ROPE_THETA, MLA_ROPE)
        k = jnp.concatenate([kv[..., :MLA_NOPE], jnp.broadcast_to(k_rot, (B, S, MLA_HEADS, MLA_ROPE))], axis=-1)
        v = kv[..., MLA_NOPE:]
        o_a = causal_block_attention(q, k, v, (MLA_NOPE + MLA_ROPE) ** -0.5).reshape(B, S, MLA_OUT)

        dil = dil.reshape(B, S, DIL_GROUPS, 3, DIL_HEADS, DIL_HEAD_DIM)
        outs = []
        lses = []
        for gi, (win, rate) in enumerate(DIL_PATTERN):
            qg = apply_rope(dil[:, :, gi, 0], positions, ROPE_THETA, PARTIAL_ROT)
            kg = apply_rope(dil[:, :, gi, 1], positions, ROPE_THETA, PARTIAL_ROT)
            o_g, lse_g = dilated_window_attention(qg, kg, dil[:, :, gi, 2], rate, win)
            outs.append(o_g)
            lses.append(lse_g)
        wts = jax.nn.softmax(jnp.stack(lses, axis=0), axis=0)
        o_b = jnp.sum(wts[..., None].astype(x.dtype) * jnp.stack(outs, axis=0), axis=0).reshape(B, S, DIL_OUT)

        g_a, g_b = jnp.split(gates, 2, axis=-1)
        merged = jax.nn.sigmoid(g_a) * (o_a @ w_branch_a[i]) + jax.nn.sigmoid(g_b) * (o_b @ w_branch_b[i])
        x = x + merged @ w_out[i]

        h2 = rms_norm(x, g_ffn[i])
        x = x + hier_moe(h2, w_router_grp[i], b_router_grp[i], w_router_exp[i], b_router_exp[i], w_exp_gate[i], w_exp_up[i], w_exp_down[i])

        e = p[i] @ w_ple_proj[i]
        x = x + jax.nn.sigmoid(rms_norm(x, g_ple[i]) @ w_ple_gate[i]) * e
    return rms_norm(x, g_final)
```

```python
import functools
import math

import jax
import jax.numpy as jnp
from jax import lax
from jax.experimental import pallas as pl
from jax.experimental.pallas import tpu as pltpu

F32 = jnp.float32
BF16 = jnp.bfloat16

D_MODEL = 1024
DEPTH = 4
RMS_EPS = 1e-6
NEG = -1e30
LOG2E = math.log2(math.e)

MLA_HEADS = 16
MLA_Q_LORA = 512
MLA_KV_LORA = 256
MLA_NOPE = 64
MLA_ROPE = 32
MLA_V = 64
MLA_ROPE_THETA = 10000.0
MLA_SLOT = 128

DIL_PATTERN = ((128, 1), (512, 4), (2048, 16))
DIL_GROUPS = 3
DIL_HEADS = 4
DIL_HEAD_DIM = 64
DIL_W = DIL_HEADS * DIL_HEAD_DIM
DIL_BLK = 128
ROPE_THETA = 500000.0
PARTIAL_ROT = DIL_HEAD_DIM // 4

N_GROUPS = 8
EXPERTS_PER_GROUP = 8
N_EXPERTS = 64
TOP_K = 2
EXPERT_FF = 256
PLE_DIM = 256

IN_TN = 768
MAIN_COLS = 3072
IN_COLS_PAD = MAIN_COLS + DIL_GROUPS * 3 * DIL_W
N_MAIN_TILES = MAIN_COLS // IN_TN

FFN_BM = 256
VMEM_LIMIT = 48 * 1024 * 1024


def _cparams(sem):
    return pltpu.CompilerParams(dimension_semantics=sem, vmem_limit_bytes=VMEM_LIMIT)


def _rms(x, g):
    return x * lax.rsqrt(jnp.mean(x * x, axis=-1, keepdims=True) + RMS_EPS) * g


def _rope128(x, c, s_up, s_dn, half):
    return x * c + pltpu.roll(x, half, 1) * s_up + pltpu.roll(x, 128 - half, 1) * s_dn


def _in_kernel(x_ref, g_ref, w_ref, tab_ref, main_ref, d0_ref, d1_ref, d2_ref, xn_ref, acc_ref):
    j = pl.program_id(1)

    @pl.when(j == 0)
    def _():
        xn_ref[...] = _rms(x_ref[...], g_ref[...]).astype(BF16)

    acc = jnp.dot(xn_ref[...], w_ref[...], preferred_element_type=F32)

    @pl.when(j < N_MAIN_TILES)
    def _():
        main_ref[...] = acc.astype(BF16)

    def dil_tile(out_ref, dil):
        for c in range(4):
            t0 = 0 if c < 2 else 3
            x = acc[:, c * 128:(c + 1) * 128]
            acc_ref[c] = _rope128(
                x, tab_ref[:, t0 * 128:(t0 + 1) * 128], tab_ref[:, (t0 + 1) * 128:(t0 + 2) * 128],
                tab_ref[:, (t0 + 2) * 128:(t0 + 3) * 128], PARTIAL_ROT // 2)
        for c in range(4, 6):
            acc_ref[c] = acc[:, c * 128:(c + 1) * 128]
        rows = acc_ref.shape[1] // dil
        for r in range(dil):
            for c in range(6):
                out_ref[0, r, :, c * 128:(c + 1) * 128] = acc_ref[c, pl.ds(r, rows, stride=dil), :].astype(BF16)

    for gi, (_, dil) in enumerate(DIL_PATTERN):
        pl.when(j == N_MAIN_TILES + gi)(functools.partial(dil_tile, (d0_ref, d1_ref, d2_ref)[gi], dil))


def _in_proj(x2d, g, w, tab, B, S, tm):
    T = x2d.shape[0]
    nt = S // tm
    dil_shapes = [jax.ShapeDtypeStruct((B, d, S // d, 3 * DIL_W), BF16) for _, d in DIL_PATTERN]
    dil_specs = [pl.BlockSpec((1, d, tm // d, 3 * DIL_W), lambda i, j, nt=nt: (i // nt, 0, i % nt, 0))
                 for _, d in DIL_PATTERN]
    return pl.pallas_call(
        _in_kernel,
        grid=(T // tm, IN_COLS_PAD // IN_TN),
        in_specs=[
            pl.BlockSpec((tm, D_MODEL), lambda i, j: (i, 0)),
            pl.BlockSpec((1, D_MODEL), lambda i, j: (0, 0)),
            pl.BlockSpec((D_MODEL, IN_TN), lambda i, j: (0, j)),
            pl.BlockSpec((tm, 6 * 128), lambda i, j: (i, 0)),
        ],
        out_specs=[pl.BlockSpec((tm, IN_TN), lambda i, j: (i, jnp.minimum(j, N_MAIN_TILES - 1)))] + dil_specs,
        out_shape=[jax.ShapeDtypeStruct((T, MAIN_COLS), BF16)] + dil_shapes,
        scratch_shapes=[pltpu.VMEM((tm, D_MODEL), BF16), pltpu.VMEM((IN_TN // 128, tm, 128), F32)],
        compiler_params=_cparams(("parallel", "arbitrary")),
        name="in_proj",
    )(x2d, g, w, tab)


def _qkv_kernel(cq_ref, ckv_ref, kpe_ref, gq_ref, gkv_ref, wq_ref, wk_ref, wv_ref, tab_ref,
                q_ref, k_ref, v_ref):
    qn = _rms(cq_ref[...].astype(F32), gq_ref[...]).astype(BF16)
    kvn = _rms(ckv_ref[...].astype(F32), gkv_ref[...]).astype(BF16)
    qacc = jnp.dot(qn, wq_ref[...], preferred_element_type=F32)
    kacc = jnp.dot(kvn, wk_ref[...], preferred_element_type=F32)
    v_ref[...] = jnp.dot(kvn, wv_ref[...], preferred_element_type=F32).astype(BF16)
    half = MLA_ROPE // 2
    tq = [tab_ref[:, t * 128:(t + 1) * 128] for t in range(3)]
    tk = [tab_ref[:, t * 128:(t + 1) * 128] for t in range(3, 6)]
    krot = _rope128(kpe_ref[:, 0:128].astype(F32), tk[0], tk[1], tk[2], half)
    for h in range(MLA_HEADS):
        sl = slice(h * MLA_SLOT, (h + 1) * MLA_SLOT)
        q_ref[:, sl] = _rope128(qacc[:, sl], tq[0], tq[1], tq[2], half).astype(BF16)
        k_ref[:, sl] = (kacc[:, sl] + krot).astype(BF16)


def _mla_qkv(main, gq, gkv, wq, wk, wv, tab, tm):
    T = main.shape[0]
    const = lambda i: (0, 0)
    return pl.pallas_call(
        _qkv_kernel,
        grid=(T // tm,),
        in_specs=[
            pl.BlockSpec((tm, MLA_Q_LORA), lambda i: (i, 2048 // MLA_Q_LORA)),
            pl.BlockSpec((tm, MLA_KV_LORA), lambda i: (i, 2560 // MLA_KV_LORA)),
            pl.BlockSpec((tm, 256), lambda i: (i, 2816 // 256)),
            pl.BlockSpec((1, MLA_Q_LORA), const),
            pl.BlockSpec((1, MLA_KV_LORA), const),
            pl.BlockSpec((MLA_Q_LORA, MLA_HEADS * MLA_SLOT), const),
            pl.BlockSpec((MLA_KV_LORA, MLA_HEADS * MLA_SLOT), const),
            pl.BlockSpec((MLA_KV_LORA, MLA_HEADS * MLA_V), const),
            pl.BlockSpec((tm, 6 * 128), lambda i: (i, 0)),
        ],
        out_specs=[
            pl.BlockSpec((tm, MLA_HEADS * MLA_SLOT), lambda i: (i, 0)),
            pl.BlockSpec((tm, MLA_HEADS * MLA_SLOT), lambda i: (i, 0)),
            pl.BlockSpec((tm, MLA_HEADS * MLA_V), lambda i: (i, 0)),
        ],
        out_shape=[
            jax.ShapeDtypeStruct((T, MLA_HEADS * MLA_SLOT), BF16),
            jax.ShapeDtypeStruct((T, MLA_HEADS * MLA_SLOT), BF16),
            jax.ShapeDtypeStruct((T, MLA_HEADS * MLA_V), BF16),
        ],
        compiler_params=_cparams(("parallel",)),
        name="mla_qkv",
    )(main, main, main, gq, gkv, wq, wk, wv, tab)


def _mla_attn_kernel(q_ref, k_ref, v_ref, o_ref, m_ref, l_ref, acc_ref, *, tq):
    i = pl.program_id(2)
    q = q_ref[0]
    lane = lax.broadcasted_iota(jnp.int32, q.shape, 1)
    zero = jnp.zeros_like(q)
    qh = [jnp.where(lane < MLA_SLOT, q, zero), jnp.where(lane >= MLA_SLOT, q, zero)]
    m_ref[...] = jnp.full(m_ref.shape, NEG, F32)
    l_ref[...] = jnp.zeros(l_ref.shape, F32)
    acc_ref[...] = jnp.zeros(acc_ref.shape, F32)
    row = lax.broadcasted_iota(jnp.int32, (tq, tq), 0)
    col = lax.broadcasted_iota(jnp.int32, (tq, tq), 1)

    def step(j, masked):
        start = pl.multiple_of(j * tq, tq)
        kb = k_ref[0, pl.ds(start, tq), :]
        vb = v_ref[0, pl.ds(start, tq), :]
        for h in range(2):
            s = lax.dot_general(qh[h], kb, (((1,), (1,)), ((), ())), preferred_element_type=F32)
            if masked:
                s = jnp.where(col <= row, s, NEG)
            m_prev = m_ref[h]
            m_new = jnp.maximum(m_prev, jnp.max(s, axis=-1, keepdims=True))
            alpha = jnp.exp2(m_prev - m_new)
            p = jnp.exp2(s - m_new)
            l_ref[h] = alpha * l_ref[h] + jnp.sum(p, axis=-1, keepdims=True)
            acc_ref[h] = alpha * acc_ref[h] + jnp.dot(p.astype(BF16), vb, preferred_element_type=F32)
            m_ref[h] = m_new

    def body(j, c):
        step(j, False)
        return c

    lax.fori_loop(0, i, body, 0)
    step(i, True)
    lane_o = lax.broadcasted_iota(jnp.int32, (tq, 2 * MLA_V), 1)
    o = jnp.where(lane_o < MLA_V, acc_ref[0] / l_ref[0], acc_ref[1] / l_ref[1])
    o_ref[0] = o.astype(BF16)


def _mla_attn(q, k, v, B, S, tq):
    q = q.reshape(B, S, MLA_HEADS * MLA_SLOT)
    k = k.reshape(B, S, MLA_HEADS * MLA_SLOT)
    v = v.reshape(B, S, MLA_HEADS * MLA_V)
    return pl.pallas_call(
        functools.partial(_mla_attn_kernel, tq=tq),
        grid=(B, MLA_HEADS // 2, S // tq),
        in_specs=[
            pl.BlockSpec((1, tq, 2 * MLA_SLOT), lambda b, h, i: (b, i, h)),
            pl.BlockSpec((1, S, 2 * MLA_SLOT), lambda b, h, i: (b, 0, h)),
            pl.BlockSpec((1, S, 2 * MLA_V), lambda b, h, i: (b, 0, h)),
        ],
        out_specs=pl.BlockSpec((1, tq, 2 * MLA_V), lambda b, h, i: (b, i, h)),
        out_shape=jax.ShapeDtypeStruct((B, S, MLA_HEADS * MLA_V), BF16),
        scratch_shapes=[pltpu.VMEM((2, tq, 1), F32), pltpu.VMEM((2, tq, 1), F32),
                        pltpu.VMEM((2, tq, 2 * MLA_V), F32)],
        compiler_params=_cparams(("parallel", "parallel", "arbitrary")),
        name="mla_attn",
    )(q, k, v)


def _dil_kernel(q_ref, k_ref, v_ref, o_ref, lse_ref, *, has_prev):
    n = pl.program_id(2)
    q = q_ref[0, 0]
    cur = pl.multiple_of(n * DIL_BLK, DIL_BLK)
    kc = k_ref[0, 0, pl.ds(cur, DIL_BLK), :]
    vc = v_ref[0, 0, pl.ds(cur, DIL_BLK), :]
    row = lax.broadcasted_iota(jnp.int32, (DIL_BLK, DIL_BLK), 0)
    col = lax.broadcasted_iota(jnp.int32, (DIL_BLK, DIL_BLK), 1)
    mask_c = col <= row
    if has_prev:
        prev = pl.multiple_of(jnp.maximum(n - 1, 0) * DIL_BLK, DIL_BLK)
        kp = k_ref[0, 0, pl.ds(prev, DIL_BLK), :]
        vp = v_ref[0, 0, pl.ds(prev, DIL_BLK), :]
        mask_p = jnp.logical_and(col >= row, n > 0)
    lane = lax.broadcasted_iota(jnp.int32, q.shape, 1)
    zero = jnp.zeros_like(q)
    o = jnp.zeros(q.shape, F32)
    lse = jnp.zeros(q.shape, F32)
    dn = (((1,), (1,)), ((), ()))
    for h in range(DIL_HEADS):
        hm = jnp.logical_and(lane >= h * DIL_HEAD_DIM, lane < (h + 1) * DIL_HEAD_DIM)
        qm = jnp.where(hm, q, zero)
        sc = jnp.where(mask_c, lax.dot_general(qm, kc, dn, preferred_element_type=F32), NEG)
        m = jnp.max(sc, axis=-1, keepdims=True)
        if has_prev:
            sp = jnp.where(mask_p, lax.dot_general(qm, kp, dn, preferred_element_type=F32), NEG)
            m = jnp.maximum(m, jnp.max(sp, axis=-1, keepdims=True))
        ec = jnp.exp2(sc - m)
        den = jnp.sum(ec, axis=-1, keepdims=True)
        oh = jnp.dot(ec.astype(BF16), vc, preferred_element_type=F32)
        if has_prev:
            ep = jnp.exp2(sp - m)
            den = den + jnp.sum(ep, axis=-1, keepdims=True)
            oh = oh + jnp.dot(ep.astype(BF16), vp, preferred_element_type=F32)
        o = jnp.where(hm, oh / den, o)
        lse = jnp.where(hm, m + jnp.log2(den), lse)
    o_ref[0, 0] = o.astype(BF16)
    lse_ref[0, 0] = lse


def _dil_attn(qkv, B, dil, L):
    nb = L // DIL_BLK
    return pl.pallas_call(
        functools.partial(_dil_kernel, has_prev=nb > 1),
        grid=(B, dil, nb),
        in_specs=[
            pl.BlockSpec((1, 1, DIL_BLK, DIL_W), lambda b, r, n: (b, r, n, 0)),
            pl.BlockSpec((1, 1, L, DIL_W), lambda b, r, n: (b, r, 0, 1)),
            pl.BlockSpec((1, 1, L, DIL_W), lambda b, r, n: (b, r, 0, 2)),
        ],
        out_specs=[
            pl.BlockSpec((1, 1, DIL_BLK, DIL_W), lambda b, r, n: (b, r, n, 0)),
            pl.BlockSpec((1, 1, DIL_BLK, DIL_W), lambda b, r, n: (b, r, n, 0)),
        ],
        out_shape=[
            jax.ShapeDtypeStruct((B, dil, L, DIL_W), BF16),
            jax.ShapeDtypeStruct((B, dil, L, DIL_W), F32),
        ],
        compiler_params=_cparams(("parallel", "parallel", "arbitrary")),
        name=f"dil_attn_d{dil}",
    )(qkv, qkv, qkv)


def _post_kernel(x_ref, oa_ref, og0_ref, og1_ref, og2_ref, ls0_ref, ls1_ref, ls2_ref, gates_ref,
                 wa_ref, wb_ref, wo_ref, gffn_ref, wr_ref, br_ref,
                 x1_ref, h2_ref, lg_ref, ob_ref, *, tm):
    ls =[r[0] for r in (ls0_ref, ls1_ref, ls2_ref)]
    og = [r[0] for r in (og0_ref, og1_ref, og2_ref)]
    for gi, (_, dil) in enumerate(DIL_PATTERN):
        rows = tm // dil
        for r in range(dil):
            for c in range(2):
                sl = slice(c * 128, (c + 1) * 128)
                ob_ref[2 * gi + c, pl.ds(r, rows, stride=dil), :] = og[gi][r][:, sl].astype(F32)
                ob_ref[6 + 2 * gi + c, pl.ds(r, rows, stride=dil), :] = ls[gi][r][:, sl]

    def tok_major(k):
        return jnp.concatenate([ob_ref[2 * k], ob_ref[2 * k + 1]], axis=-1)

    l0, l1, l2 = tok_major(3), tok_major(4), tok_major(5)
    mx = jnp.maximum(jnp.maximum(l0, l1), l2)
    w0, w1, w2 = jnp.exp2(l0 - mx), jnp.exp2(l1 - mx), jnp.exp2(l2 - mx)
    ob = (w0 * tok_major(0) + w1 * tok_major(1) + w2 * tok_major(2)) / (w0 + w1 + w2)
    ya = jnp.dot(oa_ref[...], wa_ref[...], preferred_element_type=F32)
    yb = jnp.dot(ob.astype(BF16), wb_ref[...], preferred_element_type=F32)
    merged = (jax.nn.sigmoid(gates_ref[:, 0:D_MODEL].astype(F32)) * ya
              + jax.nn.sigmoid(gates_ref[:, D_MODEL:2 * D_MODEL].astype(F32)) * yb)
    x1 = x_ref[...] + jnp.dot(merged.astype(BF16), wo_ref[...], preferred_element_type=F32)
    x1_ref[...] = x1
    h2 = _rms(x1, gffn_ref[...])
    h2_ref[...] = h2
    lg_ref[...] = jnp.dot(h2.astype(BF16), wr_ref[...], preferred_element_type=F32) + br_ref[...]


def _post(x2d, oa, ogs, lss, main, wa, wb, wo, gffn, wr, br, B, S, tm):
    T = x2d.shape[0]
    nt = S // tm
    const = lambda i: (0, 0)
    res_specs = [pl.BlockSpec((1, d, tm // d, DIL_W), lambda i, nt=nt: (i // nt, 0, i % nt, 0))
                 for _, d in DIL_PATTERN]
    return pl.pallas_call(
        functools.partial(_post_kernel, tm=tm),
        grid=(T // tm,),
        in_specs=[pl.BlockSpec((tm, D_MODEL), lambda i: (i, 0)),
                  pl.BlockSpec((tm, D_MODEL), lambda i: (i, 0))]
                 + res_specs + res_specs
                 + [pl.BlockSpec((tm, 2 * D_MODEL), lambda i: (i, 0)),
                    pl.BlockSpec((D_MODEL, D_MODEL), const),
                    pl.BlockSpec((DIL_W, D_MODEL), const),
                    pl.BlockSpec((D_MODEL, D_MODEL), const),
                    pl.BlockSpec((1, D_MODEL), const),
                    pl.BlockSpec((D_MODEL, 128), const),
                    pl.BlockSpec((1, 128), const)],
        out_specs=[pl.BlockSpec((tm, D_MODEL), lambda i: (i, 0)),
                   pl.BlockSpec((tm, D_MODEL), lambda i: (i, 0)),
                   pl.BlockSpec((tm, 128), lambda i: (i, 0))],
        out_shape=[jax.ShapeDtypeStruct((T, D_MODEL), F32),
                   jax.ShapeDtypeStruct((T, D_MODEL), F32),
                   jax.ShapeDtypeStruct((T, 128), F32)],
        scratch_shapes=[pltpu.VMEM((12, tm, 128), F32)],
        compiler_params=_cparams(("parallel",)),
        name="post_attn",
    )(x2d, oa, *ogs, *lss, main, wa, wb, wo, gffn, wr, br)


def _route_kernel(lg_ref, ri_ref, rw_ref, cnt_ref, carry_ref, *, tm):
    i = pl.program_id(0)

    @pl.when(i == 0)
    def _():
        carry_ref[...] = jnp.zeros(carry_ref.shape, F32)

    lg = lg_ref[...]
    lane = lax.broadcasted_iota(jnp.int32, lg.shape, 1)
    lane_f = lane.astype(F32)
    ninf = jnp.float32(-jnp.inf)

    def first_max(vals):
        vmax = jnp.max(vals, axis=-1, keepdims=True)
        idx = jnp.min(jnp.where(vals == vmax, lane_f, 128.0), axis=-1, keepdims=True)
        return vmax, idx.astype(jnp.int32)

    gl = jnp.where(lane < N_GROUPS, lg, ninf)
    gmax, g_sel = first_max(gl)
    p_g = 1.0 / jnp.sum(jnp.exp(gl - gmax), axis=-1, keepdims=True)
    lo = N_GROUPS + g_sel * EXPERTS_PER_GROUP
    el = jnp.where(jnp.logical_and(lane >= lo, lane < lo + EXPERTS_PER_GROUP), lg, ninf)
    v0, i0 = first_max(el)
    v1, i1 = first_max(jnp.where(lane == i0, ninf, el))
    t = jnp.exp(v1 - v0)
    w0 = p_g / (1.0 + t)
    w1 = p_g * t / (1.0 + t)
    e0 = i0 - N_GROUPS
    e1 = i1 - N_GROUPS
    hit0 = lane == e0
    hit1 = lane == e1
    oh = jnp.where(jnp.logical_or(hit0, hit1), 1.0, 0.0).astype(F32)
    r_i = lax.broadcasted_iota(jnp.int32, (tm, tm), 0)
    c_i = lax.broadcasted_iota(jnp.int32, (tm, tm), 1)
    lower = jnp.where(c_i < r_i, 1.0, 0.0).astype(BF16)
    excl = jnp.dot(lower, oh.astype(BF16), preferred_element_type=F32) + carry_ref[...]
    r0 = jnp.sum(jnp.where(hit0, excl, 0.0), axis=-1, keepdims=True).astype(jnp.int32)
    r1 = jnp.sum(jnp.where(hit1, excl, 0.0), axis=-1, keepdims=True).astype(jnp.int32)
    carry_ref[...] = carry_ref[...] + jnp.sum(oh, axis=0, keepdims=True)
    zi = jnp.zeros(lg.shape, jnp.int32)
    ri_ref[...] = jnp.where(lane == 0, e0, jnp.where(lane == 1, e1, jnp.where(lane == 2, r0, jnp.where(lane == 3, r1, zi))))
    rw_ref[...] = jnp.where(lane == 0, w0, jnp.where(lane == 1, w1, jnp.zeros(lg.shape, F32)))
    cnt_ref[...] = carry_ref[...]


def _route(lg, tm):
    T = lg.shape[0]
    return pl.pallas_call(
        functools.partial(_route_kernel, tm=tm),
        grid=(T // tm,),
        in_specs=[pl.BlockSpec((tm, 128), lambda i: (i, 0))],
        out_specs=[pl.BlockSpec((tm, 128), lambda i: (i, 0)),
                   pl.BlockSpec((tm, 128), lambda i: (i, 0)),
                   pl.BlockSpec((1, 128), lambda i: (0, 0))],
        out_shape=[jax.ShapeDtypeStruct((T, 128), jnp.int32),
                   jax.ShapeDtypeStruct((T, 128), F32),
                   jax.ShapeDtypeStruct((1, 128), F32)],
        scratch_shapes=[pltpu.VMEM((1, 128), F32)],
        compiler_params=_cparams(("arbitrary",)),
        name="route",
    )(lg)


def _row_gather(idx_ref, n_rows, src_hbm, buf, sem):
    def issue(r, c):
        pltpu.make_async_copy(src_hbm.at[pl.ds(idx_ref[0, 0, r], 1), :], buf.at[pl.ds(r, 1), :], sem).start()
        return c

    lax.fori_loop(0, n_rows, issue, 0)
    pltpu.make_async_copy(src_hbm.at[pl.ds(0, n_rows), :], buf, sem).wait()


def _ffn_kernel(be_ref, nvb_ref, rt_ref, h_hbm, w1_ref, w3_ref, w2_ref, y_ref, xbuf, sem):
    b = pl.program_id(0)

    @pl.when(b < nvb_ref[0])
    def _():
        _row_gather(rt_ref, FFN_BM, h_hbm, xbuf, sem)
        xb = xbuf[...].astype(BF16)
        h1 = jnp.dot(xb, w1_ref[0].astype(BF16), preferred_element_type=F32)
        h3 = jnp.dot(xb, w3_ref[0].astype(BF16), preferred_element_type=F32)
        a = (jax.nn.silu(h1) * h3).astype(BF16)
        y_ref[...] = jnp.dot(a, w2_ref[0].astype(BF16), preferred_element_type=F32)

    @pl.when(b >= nvb_ref[0])
    def _():
        y_ref[...] = jnp.zeros(y_ref.shape, F32)


def _expert_ffn(block_e, nvb, row_tok, h2, w1, w3, w2):
    nb = block_e.shape[0]
    return pl.pallas_call(
        _ffn_kernel,
        grid_spec=pltpu.PrefetchScalarGridSpec(
            num_scalar_prefetch=2,
            grid=(nb,),
            in_specs=[
                pl.BlockSpec((1, 1, FFN_BM), lambda b, be, nv: (b, 0, 0), memory_space=pltpu.SMEM),
                pl.BlockSpec(memory_space=pl.ANY),
                pl.BlockSpec((1, D_MODEL, EXPERT_FF), lambda b, be, nv: (be[b], 0, 0)),
                pl.BlockSpec((1, D_MODEL, EXPERT_FF), lambda b, be, nv: (be[b], 0, 0)),
                pl.BlockSpec((1, EXPERT_FF, D_MODEL), lambda b, be, nv: (be[b], 0, 0)),
            ],
            out_specs=pl.BlockSpec((FFN_BM, D_MODEL), lambda b, be, nv: (b, 0)),
            scratch_shapes=[pltpu.VMEM((FFN_BM, D_MODEL), F32), pltpu.SemaphoreType.DMA],
        ),
        out_shape=jax.ShapeDtypeStruct((nb * FFN_BM, D_MODEL), F32),
        compiler_params=_cparams(("arbitrary",)),
        name="expert_ffn",
    )(block_e, nvb, row_tok.reshape(nb, 1, FFN_BM), h2, w1, w3, w2)


def _comb_kernel(dest_ref, ys_hbm, x1_ref, rw_ref, p_ref, gple_ref, wpg_ref, wpp_ref, o_ref, ybuf, sem, *, tm):
    _row_gather(dest_ref, 2 * tm, ys_hbm, ybuf, sem)
    w = rw_ref[...]
    x2 = x1_ref[...] + (ybuf[0:tm, :] * w[:, 0:1] + ybuf[tm:2 * tm, :] * w[:, 1:2])
    e = jnp.dot(p_ref[...].astype(BF16), wpp_ref[...], preferred_element_type=F32)
    gate = jax.nn.sigmoid(jnp.dot(_rms(x2, gple_ref[...]).astype(BF16), wpg_ref[...], preferred_element_type=F32))
    o_ref[...] = x2 + gate * e


def _combine_ple(dest_t, ys, x1, rw, p2d, gple, wpg, wpp, tm):
    T = x1.shape[0]
    const = lambda i: (0, 0)
    return pl.pallas_call(
        functools.partial(_comb_kernel, tm=tm),
        grid=(T // tm,),
        in_specs=[
            pl.BlockSpec((1, 1, 2 * tm), lambda i: (i, 0, 0), memory_space=pltpu.SMEM),
            pl.BlockSpec(memory_space=pl.ANY),
            pl.BlockSpec((tm, D_MODEL), lambda i: (i, 0)),
            pl.BlockSpec((tm, 128), lambda i: (i, 0)),
            pl.BlockSpec((tm, PLE_DIM), lambda i: (i, 0)),
            pl.BlockSpec((1, D_MODEL), const),
            pl.BlockSpec((D_MODEL, D_MODEL), const),
            pl.BlockSpec((PLE_DIM, D_MODEL), const),
        ],
        out_specs=pl.BlockSpec((tm, D_MODEL), lambda i: (i, 0)),
        out_shape=jax.ShapeDtypeStruct((T, D_MODEL), F32),
        scratch_shapes=[pltpu.VMEM((2 * tm, D_MODEL), F32), pltpu.SemaphoreType.DMA],
        compiler_params=_cparams(("arbitrary",)),
        name="combine_ple",
    )(dest_t, ys, x1, rw, p2d, gple, wpg, wpp)


def _final_kernel(x_ref, g_ref, o_ref):
    o_ref[...] = _rms(x_ref[...], g_ref[...])


def _final_norm(x2d, g, tm):
    T = x2d.shape[0]
    return pl.pallas_call(
        _final_kernel,
        grid=(T // tm,),
        in_specs=[pl.BlockSpec((tm, D_MODEL), lambda i: (i, 0)), pl.BlockSpec((1, D_MODEL), lambda i: (0, 0))],
        out_specs=pl.BlockSpec((tm, D_MODEL), lambda i: (i, 0)),
        out_shape=jax.ShapeDtypeStruct((T, D_MODEL), F32),
        compiler_params=_cparams(("parallel",)),
        name="final_norm",
    )(x2d, g)


def _rope_tables(pos, theta, rot_dim, period, offset, scale):
    half = rot_dim // 2
    inv = jnp.float32(theta) ** (-jnp.arange(half, dtype=F32) * 2.0 / rot_dim)
    ang = pos.astype(F32)[:, None] * inv
    cos, sin = jnp.cos(ang), jnp.sin(ang)
    T = pos.shape[0]
    c = jnp.ones((T, period), F32).at[:, offset:offset + half].set(cos).at[:, offset + half:offset + rot_dim].set(cos)
    s_up = jnp.zeros((T, period), F32).at[:, offset + half:offset + rot_dim].set(sin)
    s_dn = jnp.zeros((T, period), F32).at[:, offset:offset + half].set(-sin)
    rep = 128 // period
    return jnp.concatenate([jnp.tile(t, (1, rep)) for t in (c, s_up, s_dn)], axis=1) * jnp.float32(scale)


def _prep_w_in(w):
    c_q = w[:, 0:512]
    c_kv = w[:, 512:768]
    k_pe = w[:, 768:800]
    dil = w[:, 800:800 + 2304]
    gates = w[:, 3104:5152]
    kslot = jnp.zeros((D_MODEL, 256), w.dtype).at[:, MLA_NOPE:MLA_NOPE + MLA_ROPE].set(k_pe)
    return jnp.concatenate([gates, c_q, c_kv, kslot, dil], axis=1).astype(BF16)


def _prep_w_q(w):
    w = w.reshape(MLA_Q_LORA, MLA_HEADS, MLA_NOPE + MLA_ROPE)
    w = jnp.pad(w, ((0, 0), (0, 0), (0, MLA_SLOT - MLA_NOPE - MLA_ROPE)))
    return w.reshape(MLA_Q_LORA, MLA_HEADS * MLA_SLOT).astype(BF16)


def _prep_w_kv(w):
    w = w.reshape(MLA_KV_LORA, MLA_HEADS, MLA_NOPE + MLA_V)
    wk = jnp.pad(w[:, :, :MLA_NOPE], ((0, 0), (0, 0), (0, MLA_SLOT - MLA_NOPE)))
    wv = w[:, :, MLA_NOPE:]
    return (wk.reshape(MLA_KV_LORA, MLA_HEADS * MLA_SLOT).astype(BF16),
            wv.reshape(MLA_KV_LORA, MLA_HEADS * MLA_V).astype(BF16))


def _dispatch_plan(ri, cnt, T):
    e = ri[:, 0:2]
    rank = ri[:, 2:4]
    counts = cnt[0, :N_EXPERTS].astype(jnp.int32)
    pcounts = (counts + FFN_BM - 1) // FFN_BM * FFN_BM
    pends = jnp.cumsum(pcounts)
    pstarts = pends - pcounts
    dest = pstarts[e] + rank
    n_blocks = (T * TOP_K) // FFN_BM + N_EXPERTS
    row_tok = jnp.zeros((n_blocks * FFN_BM,), jnp.int32).at[dest.reshape(-1)].set(
        jnp.arange(T * TOP_K, dtype=jnp.int32) // TOP_K)
    block_e = jnp.minimum(jnp.searchsorted(pends, jnp.arange(n_blocks, dtype=jnp.int32) * FFN_BM, side='right'),
                          N_EXPERTS - 1).astype(jnp.int32)
    nvb = (pends[-1] // FFN_BM).astype(jnp.int32).reshape(1)
    return dest, row_tok, block_e, nvb


def kernel(x, p, positions, g_mix, w_in, g_q_lat, w_q_up, g_kv_lat, w_kv_up, w_branch_a, w_branch_b, w_out, g_ffn, w_router_grp, b_router_grp, w_router_exp, b_router_exp, w_exp_gate, w_exp_up, w_exp_down, g_ple, w_ple_gate, w_ple_proj, g_final):
    B, S, D = x.shape
    T = B * S
    TM_IN, TM_QKV, TQ, TM_POST, TM_ROUTE, TM_COMB = 512, 512, 256, 256, 512, 256

    pos = positions.reshape(T)
    dil_scale = DIL_HEAD_DIM ** -0.5 * LOG2E
    mla_scale = (MLA_NOPE + MLA_ROPE) ** -0.5 * LOG2E
    tab_dil = jnp.concatenate([_rope_tables(pos, ROPE_THETA, PARTIAL_ROT, DIL_HEAD_DIM, 0, dil_scale),
                               _rope_tables(pos, ROPE_THETA, PARTIAL_ROT, DIL_HEAD_DIM, 0, 1.0)], axis=1)
    tab_mla = jnp.concatenate([_rope_tables(pos, MLA_ROPE_THETA, MLA_ROPE, MLA_SLOT, MLA_NOPE, mla_scale),
                               _rope_tables(pos, MLA_ROPE_THETA, MLA_ROPE, MLA_SLOT, MLA_NOPE, 1.0)], axis=1)

    xc = x.reshape(T, D)
    for i in range(DEPTH):
        w_in_p = _prep_w_in(w_in[i])
        wq = _prep_w_q(w_q_up[i])
        wk, wv = _prep_w_kv(w_kv_up[i])
        wr = jnp.zeros((D, 128), F32).at[:, :N_GROUPS].set(w_router_grp[i]).at[:, N_GROUPS:N_GROUPS + N_EXPERTS].set(
            w_router_exp[i]).astype(BF16)
        br = jnp.zeros((1, 128), F32).at[0, :N_GROUPS].set(b_router_grp[i]).at[0, N_GROUPS:N_GROUPS + N_EXPERTS].set(
            b_router_exp[i].reshape(N_EXPERTS))

        main, d0, d1, d2 = _in_proj(xc, g_mix[i].reshape(1, D), w_in_p, tab_dil, B, S, TM_IN)
        q, k, v = _mla_qkv(main, g_q_lat[i].reshape(1, -1), g_kv_lat[i].reshape(1, -1), wq, wk, wv, tab_mla, TM_QKV)
        o_a = _mla_attn(q, k, v, B, S, TQ).reshape(T, MLA_HEADS * MLA_V)
        ogs, lss = [], []
        for (_, dil), qkv in zip(DIL_PATTERN, (d0, d1, d2)):
            og, ls = _dil_attn(qkv, B, dil, S // dil)
            ogs.append(og)
            lss.append(ls)
        x1, h2, lg = _post(xc, o_a, ogs, lss, main, w_branch_a[i].astype(BF16), w_branch_b[i].astype(BF16),
                           w_out[i].astype(BF16), g_ffn[i].reshape(1, D), wr, br, B, S, TM_POST)
        ri, rw, cnt = _route(lg, TM_ROUTE)
        dest, row_tok, block_e, nvb = _dispatch_plan(ri, cnt, T)
        ys = _expert_ffn(block_e, nvb, row_tok, h2, w_exp_gate[i], w_exp_up[i], w_exp_down[i])
        dest_t = dest.reshape(T // TM_COMB, TM_COMB, TOP_K).transpose(0, 2, 1).reshape(T // TM_COMB, 1, TOP_K * TM_COMB)
        xc = _combine_ple(dest_t, ys, x1, rw, p[i].reshape(T, PLE_DIM), g_ple[i].reshape(1, D),
                          w_ple_gate[i].astype(BF16), w_ple_proj[i].astype(BF16), TM_COMB)
    return _final_norm(xc, g_final.reshape(1, D), 512).reshape(B, S, D)
```

```python
import functools
import math

import jax
import jax.numpy as jnp
from jax import lax
from jax.experimental import pallas as pl
from jax.experimental.pallas import tpu as pltpu

F32 = jnp.float32
BF16 = jnp.bfloat16

D_MODEL = 1024
DEPTH = 4
RMS_EPS = 1e-6
NEG = -1e30
LOG2E = math.log2(math.e)

MLA_HEADS = 16
MLA_Q_LORA = 512
MLA_KV_LORA = 256
MLA_NOPE = 64
MLA_ROPE = 32
MLA_V = 64
MLA_ROPE_THETA = 10000.0
MLA_SLOT = 128

DIL_PATTERN = ((128, 1), (512, 4), (2048, 16))
DIL_GROUPS = 3
DIL_HEADS = 4
DIL_HEAD_DIM = 64
DIL_W = DIL_HEADS * DIL_HEAD_DIM
DIL_BLK = 128
ROPE_THETA = 500000.0
PARTIAL_ROT = DIL_HEAD_DIM // 4

N_GROUPS = 8
EXPERTS_PER_GROUP = 8
N_EXPERTS = 64
TOP_K = 2
EXPERT_FF = 256
PLE_DIM = 256

IN_TN = 768
MAIN_COLS = 3072
IN_COLS_PAD = MAIN_COLS + DIL_GROUPS * 3 * DIL_W
N_MAIN_TILES = MAIN_COLS // IN_TN

FFN_BM = 256
VMEM_LIMIT = 48 * 1024 * 1024


def _cparams(sem, bounds_checks=True):
    return pltpu.CompilerParams(dimension_semantics=sem, vmem_limit_bytes=VMEM_LIMIT,
                                disable_bounds_checks=not bounds_checks)


def _rms(x, g):
    return x * lax.rsqrt(jnp.mean(x * x, axis=-1, keepdims=True) + RMS_EPS) * g


def _rope128(x, c, s_up, s_dn, half):
    return x * c + pltpu.roll(x, half, 1) * s_up + pltpu.roll(x, 128 - half, 1) * s_dn


def _in_kernel(x_ref, g_ref, w_ref, tab_ref, main_ref, d0_ref, d1_ref, d2_ref, xn_ref, acc_ref):
    j = pl.program_id(1)

    @pl.when(j == 0)
    def _():
        xn_ref[...] = _rms(x_ref[...], g_ref[...]).astype(BF16)

    acc = jnp.dot(xn_ref[...], w_ref[...], preferred_element_type=F32)

    @pl.when(j < N_MAIN_TILES)
    def _():
        main_ref[...] = acc.astype(BF16)

    def dil_tile(out_ref, dil):
        for c in range(4):
            t0 = 0 if c < 2 else 3
            x = acc[:, c * 128:(c + 1) * 128]
            acc_ref[c] = _rope128(
                x, tab_ref[:, t0 * 128:(t0 + 1) * 128], tab_ref[:, (t0 + 1) * 128:(t0 + 2) * 128],
                tab_ref[:, (t0 + 2) * 128:(t0 + 3) * 128], PARTIAL_ROT // 2)
        for c in range(4, 6):
            acc_ref[c] = acc[:, c * 128:(c + 1) * 128]
        rows = acc_ref.shape[1] // dil
        for r in range(dil):
            for c in range(6):
                out_ref[0, r, :, c * 128:(c + 1) * 128] = acc_ref[c, pl.ds(r, rows, stride=dil), :].astype(BF16)

    for gi, (_, dil) in enumerate(DIL_PATTERN):
        pl.when(j == N_MAIN_TILES + gi)(functools.partial(dil_tile, (d0_ref, d1_ref, d2_ref)[gi], dil))


def _in_proj(x2d, g, w, tab, B, S, tm):
    T = x2d.shape[0]
    nt = S // tm
    dil_shapes = [jax.ShapeDtypeStruct((B, d, S // d, 3 * DIL_W), BF16) for _, d in DIL_PATTERN]
    dil_specs = [pl.BlockSpec((1, d, tm // d, 3 * DIL_W), lambda i, j, nt=nt: (i // nt, 0, i % nt, 0))
                 for _, d in DIL_PATTERN]
    return pl.pallas_call(
        _in_kernel,
        grid=(T // tm, IN_COLS_PAD // IN_TN),
        in_specs=[
            pl.BlockSpec((tm, D_MODEL), lambda i, j: (i, 0)),
            pl.BlockSpec((1, D_MODEL), lambda i, j: (0, 0)),
            pl.BlockSpec((D_MODEL, IN_TN), lambda i, j: (0, j)),
            pl.BlockSpec((tm, 6 * 128), lambda i, j: (i, 0)),
        ],
        out_specs=[pl.BlockSpec((tm, IN_TN), lambda i, j: (i, jnp.minimum(j, N_MAIN_TILES - 1)))] + dil_specs,
        out_shape=[jax.ShapeDtypeStruct((T, MAIN_COLS), BF16)] + dil_shapes,
        scratch_shapes=[pltpu.VMEM((tm, D_MODEL), BF16), pltpu.VMEM((IN_TN // 128, tm, 128), F32)],
        compiler_params=_cparams(("parallel", "arbitrary")),
        name="in_proj",
    )(x2d, g, w, tab)


def _qkv_kernel(cq_ref, ckv_ref, kpe_ref, gq_ref, gkv_ref, wq_ref, wk_ref, wv_ref, tab_ref,
                q_ref, k_ref, v_ref):
    qn = _rms(cq_ref[...].astype(F32), gq_ref[...]).astype(BF16)
    kvn = _rms(ckv_ref[...].astype(F32), gkv_ref[...]).astype(BF16)
    qacc = jnp.dot(qn, wq_ref[...], preferred_element_type=F32)
    kacc = jnp.dot(kvn, wk_ref[...], preferred_element_type=F32)
    vacc = jnp.dot(kvn, wv_ref[...], preferred_element_type=F32)
    lane_v = lax.broadcasted_iota(jnp.int32, vacc.shape, 1)
    v_ref[...] = jnp.where(lane_v % MLA_SLOT < MLA_V, vacc, 1.0).astype(BF16)
    half = MLA_ROPE // 2
    tq = [tab_ref[:, t * 128:(t + 1) * 128] for t in range(3)]
    tk = [tab_ref[:, t * 128:(t + 1) * 128] for t in range(3, 6)]
    krot = _rope128(kpe_ref[:, 0:128].astype(F32), tk[0], tk[1], tk[2], half)
    for h in range(MLA_HEADS):
        sl = slice(h * MLA_SLOT, (h + 1) * MLA_SLOT)
        q_ref[:, sl] = _rope128(qacc[:, sl], tq[0], tq[1], tq[2], half).astype(BF16)
        k_ref[:, sl] = (kacc[:, sl] + krot).astype(BF16)


def _mla_qkv(main, gq, gkv, wq, wk, wv, tab, tm):
    T = main.shape[0]
    const = lambda i: (0, 0)
    return pl.pallas_call(
        _qkv_kernel,
        grid=(T // tm,),
        in_specs=[
            pl.BlockSpec((tm, MLA_Q_LORA), lambda i: (i, 2048 // MLA_Q_LORA)),
            pl.BlockSpec((tm, MLA_KV_LORA), lambda i: (i, 2560 // MLA_KV_LORA)),
            pl.BlockSpec((tm, 256), lambda i: (i, 2816 // 256)),
            pl.BlockSpec((1, MLA_Q_LORA), const),
            pl.BlockSpec((1, MLA_KV_LORA), const),
            pl.BlockSpec((MLA_Q_LORA, MLA_HEADS * MLA_SLOT), const),
            pl.BlockSpec((MLA_KV_LORA, MLA_HEADS * MLA_SLOT), const),
            pl.BlockSpec((MLA_KV_LORA, MLA_HEADS * MLA_SLOT), const),
            pl.BlockSpec((tm, 6 * 128), lambda i: (i, 0)),
        ],
        out_specs=[
            pl.BlockSpec((tm, MLA_HEADS * MLA_SLOT), lambda i: (i, 0)),
            pl.BlockSpec((tm, MLA_HEADS * MLA_SLOT), lambda i: (i, 0)),
            pl.BlockSpec((tm, MLA_HEADS * MLA_SLOT), lambda i: (i, 0)),
        ],
        out_shape=[
            jax.ShapeDtypeStruct((T, MLA_HEADS * MLA_SLOT), BF16),
            jax.ShapeDtypeStruct((T, MLA_HEADS * MLA_SLOT), BF16),
            jax.ShapeDtypeStruct((T, MLA_HEADS * MLA_SLOT), BF16),
        ],
        compiler_params=_cparams(("parallel",)),
        name="mla_qkv",
    )(main, main, main, gq, gkv, wq, wk, wv, tab)


def _mla_attn_kernel(q_ref, k_ref, v_ref, o_ref, m_ref, acc_ref, *, tq):
    i = pl.program_id(2)
    q = q_ref[0]
    lane = lax.broadcasted_iota(jnp.int32, q.shape, 1)
    zero = jnp.zeros_like(q)
    qh = [jnp.where(lane < MLA_SLOT, q, zero), jnp.where(lane >= MLA_SLOT, q, zero)]
    m_ref[...] = jnp.full(m_ref.shape, NEG, F32)
    acc_ref[...] = jnp.zeros(acc_ref.shape, F32)
    hq = tq // 2
    dn = (((1,), (1,)), ((), ()))

    def step(start, width, r0, mask):
        kb = k_ref[0, pl.ds(start, width), :]
        vb = v_ref[0, pl.ds(start, width), :]
        for h in range(2):
            s = lax.dot_general(qh[h][r0:], kb, dn, preferred_element_type=F32)
            if mask is not None:
                s = jnp.where(mask, s, NEG)
            m_prev = m_ref[h, r0:, :]
            m_new = jnp.maximum(m_prev, jnp.max(s, axis=-1, keepdims=True))
            alpha = jnp.exp2(m_prev - m_new)
            p = jnp.exp2(s - jnp.concatenate([m_new] * (width // 128), axis=-1))
            pv = jnp.dot(p.astype(BF16), vb[:, h * MLA_SLOT:(h + 1) * MLA_SLOT], preferred_element_type=F32)
            acc_ref[h, r0:, :] = alpha * acc_ref[h, r0:, :] + pv
            m_ref[h, r0:, :] = m_new

    def body(j, c):
        step(pl.multiple_of(j * tq, tq), tq, 0, None)
        return c

    lax.fori_loop(0, i, body, 0)
    d0 = pl.multiple_of(i * tq, tq)
    def causal(rows):
        return lax.broadcasted_iota(jnp.int32, (rows, hq), 1) <= lax.broadcasted_iota(jnp.int32, (rows, hq), 0)

    step(d0, hq, 0, causal(tq))
    step(pl.multiple_of(d0 + hq, hq), hq, hq, causal(hq))
    outs = []
    for h in range(2):
        a = acc_ref[h]
        outs.append(a / pltpu.roll(a, MLA_V, 1))
    lane_o = lax.broadcasted_iota(jnp.int32, (tq, 2 * MLA_V), 1)
    o_ref[0] = jnp.where(lane_o < MLA_V, outs[0], pltpu.roll(outs[1], MLA_V, 1)).astype(BF16)


def _mla_attn(q, k, v, B, S, tq):
    q = q.reshape(B, S, MLA_HEADS * MLA_SLOT)
    k = k.reshape(B, S, MLA_HEADS * MLA_SLOT)
    v = v.reshape(B, S, MLA_HEADS * MLA_SLOT)
    return pl.pallas_call(
        functools.partial(_mla_attn_kernel, tq=tq),
        grid=(B, MLA_HEADS // 2, S // tq),
        in_specs=[
            pl.BlockSpec((1, tq, 2 * MLA_SLOT), lambda b, h, i: (b, i, h)),
            pl.BlockSpec((1, S, 2 * MLA_SLOT), lambda b, h, i: (b, 0, h)),
            pl.BlockSpec((1, S, 2 * MLA_SLOT), lambda b, h, i: (b, 0, h)),
        ],
        out_specs=pl.BlockSpec((1, tq, 2 * MLA_V), lambda b, h, i: (b, i, h)),
        out_shape=jax.ShapeDtypeStruct((B, S, MLA_HEADS * MLA_V), BF16),
        scratch_shapes=[pltpu.VMEM((2, tq, 128), F32), pltpu.VMEM((2, tq, MLA_SLOT), F32)],
        compiler_params=_cparams(("parallel", "parallel", "arbitrary")),
        name="mla_attn",
    )(q, k, v)


def _dil_kernel(qkv_ref, o_ref, lse_ref, bias_ref, *, nb):
    dil = qkv_ref.shape[1]
    width = 2 * DIL_BLK if nb > 1 else DIL_BLK
    row = lax.broadcasted_iota(jnp.int32, (DIL_BLK, width), 0)
    col = lax.broadcasted_iota(jnp.int32, (DIL_BLK, width), 1)
    bias_ref[0] = jnp.where(col <= row, 0.0, NEG)
    if nb > 1:
        later = jnp.logical_or(jnp.logical_and(col >= DIL_BLK, col - DIL_BLK <= row),
                               jnp.logical_and(col < DIL_BLK, col >= row))
        bias_ref[1] = jnp.where(later, 0.0, NEG)
    head_of_lane = lax.broadcasted_iota(jnp.int32, (DIL_BLK, DIL_W), 1) // DIL_HEAD_DIM
    dn = (((1,), (1,)), ((), ()))

    def by_head(parts):
        out = parts[DIL_HEADS - 1]
        for h in range(DIL_HEADS - 2, -1, -1):
            out = jnp.where(head_of_lane == h, parts[h], out)
        return out

    def unit(u):
        r = u // nb
        n = u % nb
        q0 = pl.multiple_of(n * DIL_BLK, DIL_BLK)
        q = qkv_ref[0, r, pl.ds(q0, DIL_BLK), 0:DIL_W]
        if nb > 1:
            w0 = pl.multiple_of(jnp.maximum(n - 1, 0) * DIL_BLK, DIL_BLK)
            bias = bias_ref[jnp.minimum(n, 1)]
        else:
            w0 = 0
            bias = bias_ref[0]
        kw = qkv_ref[0, r, pl.ds(w0, width), DIL_W:2 * DIL_W]
        vw = qkv_ref[0, r, pl.ds(w0, width), 2 * DIL_W:3 * DIL_W]
        zero = jnp.zeros_like(q)
        qs = jnp.concatenate([jnp.where(head_of_lane == h, q, zero) for h in range(DIL_HEADS)], axis=0)
        s = lax.dot_general(qs, kw, dn, preferred_element_type=F32)
        s = (s.reshape(DIL_HEADS, DIL_BLK, width) + bias[None]).reshape(DIL_HEADS * DIL_BLK, width)
        m = jnp.max(s, axis=-1, keepdims=True)
        e = jnp.exp2(s - m)
        den = jnp.sum(e, axis=-1, keepdims=True)
        pv = jnp.dot(e.astype(BF16), vw, preferred_element_type=F32)
        lse = m + jnp.log2(den)
        blk = lambda t, h: t[h * DIL_BLK:(h + 1) * DIL_BLK]
        o = by_head([blk(pv, h) for h in range(DIL_HEADS)]) / by_head(
            [jnp.broadcast_to(blk(den, h), (DIL_BLK, DIL_W)) for h in range(DIL_HEADS)])
        o_ref[0, r, pl.ds(q0, DIL_BLK), :] = o.astype(BF16)
        lse_ref[0, r, pl.ds(q0, DIL_BLK), :] = by_head(
            [jnp.broadcast_to(blk(lse, h), (DIL_BLK, DIL_W)) for h in range(DIL_HEADS)])

    def body(t, carry):
        unit(2 * t)
        unit(2 * t + 1)
        return carry

    lax.fori_loop(0, dil * nb // 2, body, 0)


def _dil_attn(qkv, B, dil, L):
    nb = L // DIL_BLK
    return pl.pallas_call(
        functools.partial(_dil_kernel, nb=nb),
        grid=(B,),
        in_specs=[pl.BlockSpec((1, dil, L, 3 * DIL_W), lambda b: (b, 0, 0, 0))],
        out_specs=[
            pl.BlockSpec((1, dil, L, DIL_W), lambda b: (b, 0, 0, 0)),
            pl.BlockSpec((1, dil, L, DIL_W), lambda b: (b, 0, 0, 0)),
        ],
        out_shape=[
            jax.ShapeDtypeStruct((B, dil, L, DIL_W), BF16),
            jax.ShapeDtypeStruct((B, dil, L, DIL_W), F32),
        ],
        scratch_shapes=[pltpu.VMEM((2, DIL_BLK, 2 * DIL_BLK if nb > 1 else DIL_BLK), F32)],
        compiler_params=_cparams(("parallel",)),
        name=f"dil_attn_d{dil}",
    )(qkv)


def _post_kernel(x_ref, oa_ref, og0_ref, og1_ref, og2_ref, ls0_ref, ls1_ref, ls2_ref, gates_ref,
                 wa_ref, wb_ref, wo_ref, gffn_ref, wr_ref, br_ref,
                 x1_ref, h2_ref, lg_ref, ob_ref, *, tm):
    ls =[r[0] for r in (ls0_ref, ls1_ref, ls2_ref)]
    og = [r[0] for r in (og0_ref, og1_ref, og2_ref)]
    for gi, (_, dil) in enumerate(DIL_PATTERN):
        rows = tm // dil
        for r in range(dil):
            for c in range(2):
                sl = slice(c * 128, (c + 1) * 128)
                ob_ref[2 * gi + c, pl.ds(r, rows, stride=dil), :] = og[gi][r][:, sl].astype(F32)
                ob_ref[6 + 2 * gi + c, pl.ds(r, rows, stride=dil), :] = ls[gi][r][:, sl]

    def tok_major(k):
        return jnp.concatenate([ob_ref[2 * k], ob_ref[2 * k + 1]], axis=-1)

    l0, l1, l2 = tok_major(3), tok_major(4), tok_major(5)
    mx = jnp.maximum(jnp.maximum(l0, l1), l2)
    w0, w1, w2 = jnp.exp2(l0 - mx), jnp.exp2(l1 - mx), jnp.exp2(l2 - mx)
    ob = (w0 * tok_major(0) + w1 * tok_major(1) + w2 * tok_major(2)) / (w0 + w1 + w2)
    ya = jnp.dot(oa_ref[...], wa_ref[...], preferred_element_type=F32)
    yb = jnp.dot(ob.astype(BF16), wb_ref[...], preferred_element_type=F32)
    merged = (jax.nn.sigmoid(gates_ref[:, 0:D_MODEL].astype(F32)) * ya
              + jax.nn.sigmoid(gates_ref[:, D_MODEL:2 * D_MODEL].astype(F32)) * yb)
    x1 = x_ref[...] + jnp.dot(merged.astype(BF16), wo_ref[...], preferred_element_type=F32)
    x1_ref[...] = x1
    h2 = _rms(x1, gffn_ref[...])
    h2_ref[...] = h2
    lg_ref[...] = jnp.dot(h2.astype(BF16), wr_ref[...], preferred_element_type=F32) + br_ref[...]


def _post(x2d, oa, ogs, lss, main, wa, wb, wo, gffn, wr, br, B, S, tm):
    T = x2d.shape[0]
    nt = S // tm
    const = lambda i: (0, 0)
    res_specs = [pl.BlockSpec((1, d, tm // d, DIL_W), lambda i, nt=nt: (i // nt, 0, i % nt, 0))
                 for _, d in DIL_PATTERN]
    return pl.pallas_call(
        functools.partial(_post_kernel, tm=tm),
        grid=(T // tm,),
        in_specs=[pl.BlockSpec((tm, D_MODEL), lambda i: (i, 0)),
                  pl.BlockSpec((tm, D_MODEL), lambda i: (i, 0))]
                 + res_specs + res_specs
                 + [pl.BlockSpec((tm, 2 * D_MODEL), lambda i: (i, 0)),
                    pl.BlockSpec((D_MODEL, D_MODEL), const),
                    pl.BlockSpec((DIL_W, D_MODEL), const),
                    pl.BlockSpec((D_MODEL, D_MODEL), const),
                    pl.BlockSpec((1, D_MODEL), const),
                    pl.BlockSpec((D_MODEL, 128), const),
                    pl.BlockSpec((1, 128), const)],
        out_specs=[pl.BlockSpec((tm, D_MODEL), lambda i: (i, 0)),
                   pl.BlockSpec((tm, D_MODEL), lambda i: (i, 0)),
                   pl.BlockSpec((tm, 128), lambda i: (i, 0))],
        out_shape=[jax.ShapeDtypeStruct((T, D_MODEL), F32),
                   jax.ShapeDtypeStruct((T, D_MODEL), F32),
                   jax.ShapeDtypeStruct((T, 128), F32)],
        scratch_shapes=[pltpu.VMEM((12, tm, 128), F32)],
        compiler_params=_cparams(("parallel",)),
        name="post_attn",
    )(x2d, oa, *ogs, *lss, main, wa, wb, wo, gffn, wr, br)


def _route_kernel(lg_ref, ri_ref, rw_ref, cnt_ref, carry_ref, *, tm):
    i = pl.program_id(0)

    @pl.when(i == 0)
    def _():
        carry_ref[...] = jnp.zeros(carry_ref.shape, F32)

    lg = lg_ref[...]
    lane = lax.broadcasted_iota(jnp.int32, lg.shape, 1)
    lane_f = lane.astype(F32)
    ninf = jnp.float32(-jnp.inf)

    def first_max(vals):
        vmax = jnp.max(vals, axis=-1, keepdims=True)
        idx = jnp.min(jnp.where(vals == vmax, lane_f, 128.0), axis=-1, keepdims=True)
        return vmax, idx.astype(jnp.int32)

    gl = jnp.where(lane < N_GROUPS, lg, ninf)
    gmax, g_sel = first_max(gl)
    p_g = 1.0 / jnp.sum(jnp.exp(gl - gmax), axis=-1, keepdims=True)
    lo = N_GROUPS + g_sel * EXPERTS_PER_GROUP
    el = jnp.where(jnp.logical_and(lane >= lo, lane < lo + EXPERTS_PER_GROUP), lg, ninf)
    v0, i0 = first_max(el)
    v1, i1 = first_max(jnp.where(lane == i0, ninf, el))
    t = jnp.exp(v1 - v0)
    w0 = p_g / (1.0 + t)
    w1 = p_g * t / (1.0 + t)
    e0 = i0 - N_GROUPS
    e1 = i1 - N_GROUPS
    hit0 = lane == e0
    hit1 = lane == e1
    oh = jnp.where(jnp.logical_or(hit0, hit1), 1.0, 0.0).astype(F32)
    r_i = lax.broadcasted_iota(jnp.int32, (tm, tm), 0)
    c_i = lax.broadcasted_iota(jnp.int32, (tm, tm), 1)
    lower = jnp.where(c_i < r_i, 1.0, 0.0).astype(BF16)
    excl = jnp.dot(lower, oh.astype(BF16), preferred_element_type=F32) + carry_ref[...]
    r0 = jnp.sum(jnp.where(hit0, excl, 0.0), axis=-1, keepdims=True).astype(jnp.int32)
    r1 = jnp.sum(jnp.where(hit1, excl, 0.0), axis=-1, keepdims=True).astype(jnp.int32)
    carry_ref[...] = carry_ref[...] + jnp.sum(oh, axis=0, keepdims=True)
    zi = jnp.zeros(lg.shape, jnp.int32)
    ri_ref[...] = jnp.where(lane == 0, e0, jnp.where(lane == 1, e1, jnp.where(lane == 2, r0, jnp.where(lane == 3, r1, zi))))
    rw_ref[...] = jnp.where(lane == 0, w0, jnp.where(lane == 1, w1, jnp.zeros(lg.shape, F32)))
    cnt_ref[...] = carry_ref[...]


def _route(lg, tm):
    T = lg.shape[0]
    return pl.pallas_call(
        functools.partial(_route_kernel, tm=tm),
        grid=(T // tm,),
        in_specs=[pl.BlockSpec((tm, 128), lambda i: (i, 0))],
        out_specs=[pl.BlockSpec((tm, 128), lambda i: (i, 0)),
                   pl.BlockSpec((tm, 128), lambda i: (i, 0)),
                   pl.BlockSpec((1, 128), lambda i: (0, 0))],
        out_shape=[jax.ShapeDtypeStruct((T, 128), jnp.int32),
                   jax.ShapeDtypeStruct((T, 128), F32),
                   jax.ShapeDtypeStruct((1, 128), F32)],
        scratch_shapes=[pltpu.VMEM((1, 128), F32)],
        compiler_params=_cparams(("arbitrary",)),
        name="route",
    )(lg)


def _issue_rows(idx_ref, n_rows, src_hbm, buf, sem, unrolled=True):
    def start(r):
        pltpu.make_async_copy(src_hbm.at[pl.ds(idx_ref[0, 0, r], 1), :], buf.at[pl.ds(r, 1), :], sem).start()

    if unrolled:
        for r in range(n_rows):
            start(r)
    else:
        lax.fori_loop(0, n_rows, lambda r, c: (start(r), c)[1], 0)


def _wait_rows(n_rows, src_hbm, buf, sem):
    pltpu.make_async_copy(src_hbm.at[pl.ds(0, n_rows), :], buf, sem).wait()


def _ffn_kernel(be_ref, nvb_ref, rt_cur_ref, rt_nxt_ref, h_hbm, w1_ref, w3_ref, w2_ref, y_ref, xbuf, sem):
    b = pl.program_id(0)
    nvb = nvb_ref[0]
    slot = b % 2

    @pl.when(jnp.logical_and(b == 0, nvb > 0))
    def _():
        _issue_rows(rt_cur_ref, FFN_BM, h_hbm, xbuf.at[0], sem.at[0], unrolled=False)

    @pl.when(b + 1 < nvb)
    def _():
        _issue_rows(rt_nxt_ref, FFN_BM, h_hbm, xbuf.at[1 - slot], sem.at[1 - slot])

    @pl.when(b < nvb)
    def _():
        _wait_rows(FFN_BM, h_hbm, xbuf.at[slot], sem.at[slot])
        xb = xbuf[slot].astype(BF16)
        h1 = jnp.dot(xb, w1_ref[0].astype(BF16), preferred_element_type=F32)
        h3 = jnp.dot(xb, w3_ref[0].astype(BF16), preferred_element_type=F32)
        a = (jax.nn.silu(h1) * h3).astype(BF16)
        y_ref[...] = jnp.dot(a, w2_ref[0].astype(BF16), preferred_element_type=F32)

    @pl.when(b >= nvb)
    def _():
        y_ref[...] = jnp.zeros(y_ref.shape, F32)


def _expert_ffn(block_e, nvb, row_tok, h2, w1, w3, w2):
    nb = block_e.shape[0]
    row_tok = row_tok.reshape(nb, 1, FFN_BM)
    return pl.pallas_call(
        _ffn_kernel,
        grid_spec=pltpu.PrefetchScalarGridSpec(
            num_scalar_prefetch=2,
            grid=(nb,),
            in_specs=[
                pl.BlockSpec((1, 1, FFN_BM), lambda b, be, nv: (b, 0, 0), memory_space=pltpu.SMEM),
                pl.BlockSpec((1, 1, FFN_BM), lambda b, be, nv: (jnp.minimum(b + 1, nb - 1), 0, 0),
                             memory_space=pltpu.SMEM),
                pl.BlockSpec(memory_space=pl.ANY),
                pl.BlockSpec((1, D_MODEL, EXPERT_FF), lambda b, be, nv: (be[b], 0, 0)),
                pl.BlockSpec((1, D_MODEL, EXPERT_FF), lambda b, be, nv: (be[b], 0, 0)),
                pl.BlockSpec((1, EXPERT_FF, D_MODEL), lambda b, be, nv: (be[b], 0, 0)),
            ],
            out_specs=pl.BlockSpec((FFN_BM, D_MODEL), lambda b, be, nv: (b, 0)),
            scratch_shapes=[pltpu.VMEM((2, FFN_BM, D_MODEL), F32), pltpu.SemaphoreType.DMA((2,))],
        ),
        out_shape=jax.ShapeDtypeStruct((nb * FFN_BM, D_MODEL), F32),
        compiler_params=_cparams(("arbitrary",), bounds_checks=False),
        name="expert_ffn",
    )(block_e, nvb, row_tok, row_tok, h2, w1, w3, w2)


def _comb_kernel(d_cur_ref, d_nxt_ref, ys_hbm, x1_ref, rw_ref, p_ref, gple_ref, wpg_ref, wpp_ref, o_ref,
                 ybuf, sem, *, tm):
    i = pl.program_id(0)
    slot = i % 2

    @pl.when(i == 0)
    def _():
        _issue_rows(d_cur_ref, 2 * tm, ys_hbm, ybuf.at[0], sem.at[0], unrolled=False)

    @pl.when(i + 1 < pl.num_programs(0))
    def _():
        _issue_rows(d_nxt_ref, 2 * tm, ys_hbm, ybuf.at[1 - slot], sem.at[1 - slot])

    _wait_rows(2 * tm, ys_hbm, ybuf.at[slot], sem.at[slot])
    w = rw_ref[...]
    x2 = x1_ref[...] + (ybuf[slot, 0:tm, :] * w[:, 0:1] + ybuf[slot, tm:2 * tm, :] * w[:, 1:2])
    e = jnp.dot(p_ref[...].astype(BF16), wpp_ref[...], preferred_element_type=F32)
    gate = jax.nn.sigmoid(jnp.dot(_rms(x2, gple_ref[...]).astype(BF16), wpg_ref[...], preferred_element_type=F32))
    o_ref[...] = x2 + gate * e


def _combine_ple(dest_t, ys, x1, rw, p2d, gple, wpg, wpp, tm):
    T = x1.shape[0]
    nt = T // tm
    const = lambda i: (0, 0)
    return pl.pallas_call(
        functools.partial(_comb_kernel, tm=tm),
        grid=(nt,),
        in_specs=[
            pl.BlockSpec((1, 1, 2 * tm), lambda i: (i, 0, 0), memory_space=pltpu.SMEM),
            pl.BlockSpec((1, 1, 2 * tm), lambda i: (jnp.minimum(i + 1, nt - 1), 0, 0), memory_space=pltpu.SMEM),
            pl.BlockSpec(memory_space=pl.ANY),
            pl.BlockSpec((tm, D_MODEL), lambda i: (i, 0)),
            pl.BlockSpec((tm, 128), lambda i: (i, 0)),
            pl.BlockSpec((tm, PLE_DIM), lambda i: (i, 0)),
            pl.BlockSpec((1, D_MODEL), const),
            pl.BlockSpec((D_MODEL, D_MODEL), const),
            pl.BlockSpec((PLE_DIM, D_MODEL), const),
        ],
        out_specs=pl.BlockSpec((tm, D_MODEL), lambda i: (i, 0)),
        out_shape=jax.ShapeDtypeStruct((T, D_MODEL), F32),
        scratch_shapes=[pltpu.VMEM((2, 2 * tm, D_MODEL), F32), pltpu.SemaphoreType.DMA((2,))],
        compiler_params=_cparams(("arbitrary",)),
        name="combine_ple",
    )(dest_t, dest_t, ys, x1, rw, p2d, gple, wpg, wpp)


def _final_kernel(x_ref, g_ref, o_ref):
    o_ref[...] = _rms(x_ref[...], g_ref[...])


def _final_norm(x2d, g, tm):
    T = x2d.shape[0]
    return pl.pallas_call(
        _final_kernel,
        grid=(T // tm,),
        in_specs=[pl.BlockSpec((tm, D_MODEL), lambda i: (i, 0)), pl.BlockSpec((1, D_MODEL), lambda i: (0, 0))],
        out_specs=pl.BlockSpec((tm, D_MODEL), lambda i: (i, 0)),
        out_shape=jax.ShapeDtypeStruct((T, D_MODEL), F32),
        compiler_params=_cparams(("parallel",)),
        name="final_norm",
    )(x2d, g)


def _rope_tables(pos, theta, rot_dim, period, offset, scale):
    half = rot_dim // 2
    inv = jnp.float32(theta) ** (-jnp.arange(half, dtype=F32) * 2.0 / rot_dim)
    ang = pos.astype(F32)[:, None] * inv
    cos, sin = jnp.cos(ang), jnp.sin(ang)
    T = pos.shape[0]
    c = jnp.ones((T, period), F32).at[:, offset:offset + half].set(cos).at[:, offset + half:offset + rot_dim].set(cos)
    s_up = jnp.zeros((T, period), F32).at[:, offset + half:offset + rot_dim].set(sin)
    s_dn = jnp.zeros((T, period), F32).at[:, offset:offset + half].set(-sin)
    rep = 128 // period
    return jnp.concatenate([jnp.tile(t, (1, rep)) for t in (c, s_up, s_dn)], axis=1) * jnp.float32(scale)


def _prep_w_in(w):
    c_q = w[:, 0:512]
    c_kv = w[:, 512:768]
    k_pe = w[:, 768:800]
    dil = w[:, 800:800 + 2304]
    gates = w[:, 3104:5152]
    kslot = jnp.zeros((D_MODEL, 256), w.dtype).at[:, MLA_NOPE:MLA_NOPE + MLA_ROPE].set(k_pe)
    return jnp.concatenate([gates, c_q, c_kv, kslot, dil], axis=1).astype(BF16)


def _prep_w_q(w):
    w = w.reshape(MLA_Q_LORA, MLA_HEADS, MLA_NOPE + MLA_ROPE)
    w = jnp.pad(w, ((0, 0), (0, 0), (0, MLA_SLOT - MLA_NOPE - MLA_ROPE)))
    return w.reshape(MLA_Q_LORA, MLA_HEADS * MLA_SLOT).astype(BF16)


def _prep_w_kv(w):
    w = w.reshape(MLA_KV_LORA, MLA_HEADS, MLA_NOPE + MLA_V)
    wk = jnp.pad(w[:, :, :MLA_NOPE], ((0, 0), (0, 0), (0, MLA_SLOT - MLA_NOPE)))
    wv = jnp.pad(w[:, :, MLA_NOPE:], ((0, 0), (0, 0), (0, MLA_SLOT - MLA_V)))
    return (wk.reshape(MLA_KV_LORA, MLA_HEADS * MLA_SLOT).astype(BF16),
            wv.reshape(MLA_KV_LORA, MLA_HEADS * MLA_SLOT).astype(BF16))


def _dispatch_plan(ri, cnt, T):
    e = ri[:, 0:2]
    rank = ri[:, 2:4]
    counts = cnt[0, :N_EXPERTS].astype(jnp.int32)
    pcounts = (counts + FFN_BM - 1) // FFN_BM * FFN_BM
    pends = jnp.cumsum(pcounts)
    pstarts = pends - pcounts
    dest = pstarts[e] + rank
    n_blocks = (T * TOP_K) // FFN_BM + N_EXPERTS
    row_tok = jnp.zeros((n_blocks * FFN_BM,), jnp.int32).at[dest.reshape(-1)].set(
        jnp.arange(T * TOP_K, dtype=jnp.int32) // TOP_K)
    block_e = jnp.minimum(jnp.searchsorted(pends, jnp.arange(n_blocks, dtype=jnp.int32) * FFN_BM, side='right'),
                          N_EXPERTS - 1).astype(jnp.int32)
    nvb = (pends[-1] // FFN_BM).astype(jnp.int32).reshape(1)
    return dest, row_tok, block_e, nvb


def kernel(x, p, positions, g_mix, w_in, g_q_lat, w_q_up, g_kv_lat, w_kv_up, w_branch_a, w_branch_b, w_out, g_ffn, w_router_grp, b_router_grp, w_router_exp, b_router_exp, w_exp_gate, w_exp_up, w_exp_down, g_ple, w_ple_gate, w_ple_proj, g_final):
    B, S, D = x.shape
    T = B * S
    TM_IN, TM_QKV, TQ, TM_POST, TM_ROUTE, TM_COMB = 512, 512, 512, 256, 512, 256

    pos = positions.reshape(T)
    dil_scale = DIL_HEAD_DIM ** -0.5 * LOG2E
    mla_scale = (MLA_NOPE + MLA_ROPE) ** -0.5 * LOG2E
    tab_dil = jnp.concatenate([_rope_tables(pos, ROPE_THETA, PARTIAL_ROT, DIL_HEAD_DIM, 0, dil_scale),
                               _rope_tables(pos, ROPE_THETA, PARTIAL_ROT, DIL_HEAD_DIM, 0, 1.0)], axis=1)
    tab_mla = jnp.concatenate([_rope_tables(pos, MLA_ROPE_THETA, MLA_ROPE, MLA_SLOT, MLA_NOPE, mla_scale),
                               _rope_tables(pos, MLA_ROPE_THETA, MLA_ROPE, MLA_SLOT, MLA_NOPE, 1.0)], axis=1)

    xc = x.reshape(T, D)
    for i in range(DEPTH):
        w_in_p = _prep_w_in(w_in[i])
        wq = _prep_w_q(w_q_up[i])
        wk, wv = _prep_w_kv(w_kv_up[i])
        wr = jnp.zeros((D, 128), F32).at[:, :N_GROUPS].set(w_router_grp[i]).at[:, N_GROUPS:N_GROUPS + N_EXPERTS].set(
            w_router_exp[i]).astype(BF16)
        br = jnp.zeros((1, 128), F32).at[0, :N_GROUPS].set(b_router_grp[i]).at[0, N_GROUPS:N_GROUPS + N_EXPERTS].set(
            b_router_exp[i].reshape(N_EXPERTS))

        main, d0, d1, d2 = _in_proj(xc, g_mix[i].reshape(1, D), w_in_p, tab_dil, B, S, TM_IN)
        q, k, v = _mla_qkv(main, g_q_lat[i].reshape(1, -1), g_kv_lat[i].reshape(1, -1), wq, wk, wv, tab_mla, TM_QKV)
        o_a = _mla_attn(q, k, v, B, S, TQ).reshape(T, MLA_HEADS * MLA_V)
        ogs, lss = [], []
        for (_, dil), qkv in zip(DIL_PATTERN, (d0, d1, d2)):
            og, ls = _dil_attn(qkv, B, dil, S // dil)
            ogs.append(og)
            lss.append(ls)
        x1, h2, lg = _post(xc, o_a, ogs, lss, main, w_branch_a[i].astype(BF16), w_branch_b[i].astype(BF16),
                           w_out[i].astype(BF16), g_ffn[i].reshape(1, D), wr, br, B, S, TM_POST)
        ri, rw, cnt = _route(lg, TM_ROUTE)
        dest, row_tok, block_e, nvb = _dispatch_plan(ri, cnt, T)
        ys = _expert_ffn(block_e, nvb, row_tok, h2, w_exp_gate[i], w_exp_up[i], w_exp_down[i])
        dest_t = dest.reshape(T // TM_COMB, TM_COMB, TOP_K).transpose(0, 2, 1).reshape(T // TM_COMB, 1, TOP_K * TM_COMB)
        xc = _combine_ple(dest_t, ys, x1, rw, p[i].reshape(T, PLE_DIM), g_ple[i].reshape(1, D),
                          w_ple_gate[i].astype(BF16), w_ple_proj[i].astype(BF16), TM_COMB)
    return _final_norm(xc, g_final.reshape(1, D), 512).reshape(B, S, D)
```

```python
import functools
import math

import jax
import jax.numpy as jnp
from jax import lax
from jax.experimental import pallas as pl
from jax.experimental.pallas import tpu as pltpu

F32 = jnp.float32
BF16 = jnp.bfloat16

D_MODEL = 1024
DEPTH = 4
RMS_EPS = 1e-6
NEG = -1e30
LOG2E = math.log2(math.e)

MLA_HEADS = 16
MLA_Q_LORA = 512
MLA_KV_LORA = 256
MLA_NOPE = 64
MLA_ROPE = 32
MLA_V = 64
MLA_ROPE_THETA = 10000.0
MLA_SLOT = 128
MLA_HPS = 2

DIL_PATTERN = ((128, 1), (512, 4), (2048, 16))
DIL_GROUPS = 3
DIL_HEADS = 4
DIL_HEAD_DIM = 64
DIL_W = DIL_HEADS * DIL_HEAD_DIM
DIL_BLK = 128
ROPE_THETA = 500000.0
PARTIAL_ROT = DIL_HEAD_DIM // 4

N_GROUPS = 8
EXPERTS_PER_GROUP = 8
N_EXPERTS = 64
TOP_K = 2
EXPERT_FF = 256
PLE_DIM = 256

IN_TN = 768
MAIN_COLS = 3072
IN_COLS_PAD = MAIN_COLS + DIL_GROUPS * 3 * DIL_W
N_MAIN_TILES = MAIN_COLS // IN_TN

FFN_BM = 256
ROW_TILE = D_MODEL // 128
VMEM_LIMIT = 48 * 1024 * 1024


def _cparams(sem, bounds_checks=True):
    return pltpu.CompilerParams(dimension_semantics=sem, vmem_limit_bytes=VMEM_LIMIT,
                                disable_bounds_checks=not bounds_checks)


def _rms(x, g):
    return x * lax.rsqrt(jnp.mean(x * x, axis=-1, keepdims=True) + RMS_EPS) * g


def _rope128(x, c, s_up, s_dn, half):
    return x * c + pltpu.roll(x, half, 1) * s_up + pltpu.roll(x, 128 - half, 1) * s_dn


def _in_kernel(x_ref, g_ref, w_ref, tab_ref, main_ref, d0_ref, d1_ref, d2_ref, xn_ref, acc_ref):
    j = pl.program_id(1)

    @pl.when(j == 0)
    def _():
        xn_ref[...] = _rms(x_ref[...], g_ref[...]).astype(BF16)

    acc = jnp.dot(xn_ref[...], w_ref[...], preferred_element_type=F32)

    @pl.when(j < N_MAIN_TILES)
    def _():
        main_ref[...] = acc.astype(BF16)

    def dil_tile(out_ref, dil):
        for c in range(4):
            t0 = 0 if c < 2 else 3
            x = acc[:, c * 128:(c + 1) * 128]
            acc_ref[c] = _rope128(
                x, tab_ref[:, t0 * 128:(t0 + 1) * 128], tab_ref[:, (t0 + 1) * 128:(t0 + 2) * 128],
                tab_ref[:, (t0 + 2) * 128:(t0 + 3) * 128], PARTIAL_ROT // 2)
        for c in range(4, 6):
            acc_ref[c] = acc[:, c * 128:(c + 1) * 128]
        rows = acc_ref.shape[1] // dil
        for r in range(dil):
            for c in range(6):
                out_ref[0, r, :, c * 128:(c + 1) * 128] = acc_ref[c, pl.ds(r, rows, stride=dil), :].astype(BF16)

    for gi, (_, dil) in enumerate(DIL_PATTERN):
        pl.when(j == N_MAIN_TILES + gi)(functools.partial(dil_tile, (d0_ref, d1_ref, d2_ref)[gi], dil))


def _in_proj(x2d, g, w, tab, B, S, tm):
    T = x2d.shape[0]
    nt = S // tm
    dil_shapes = [jax.ShapeDtypeStruct((B, d, S // d, 3 * DIL_W), BF16) for _, d in DIL_PATTERN]
    dil_specs = [pl.BlockSpec((1, d, tm // d, 3 * DIL_W), lambda i, j, nt=nt: (i // nt, 0, i % nt, 0))
                 for _, d in DIL_PATTERN]
    return pl.pallas_call(
        _in_kernel,
        grid=(T // tm, IN_COLS_PAD // IN_TN),
        in_specs=[
            pl.BlockSpec((tm, D_MODEL), lambda i, j: (i, 0)),
            pl.BlockSpec((1, D_MODEL), lambda i, j: (0, 0)),
            pl.BlockSpec((D_MODEL, IN_TN), lambda i, j: (0, j)),
            pl.BlockSpec((tm, 6 * 128), lambda i, j: (i, 0)),
        ],
        out_specs=[pl.BlockSpec((tm, IN_TN), lambda i, j: (i, jnp.minimum(j, N_MAIN_TILES - 1)))] + dil_specs,
        out_shape=[jax.ShapeDtypeStruct((T, MAIN_COLS), BF16)] + dil_shapes,
        scratch_shapes=[pltpu.VMEM((tm, D_MODEL), BF16), pltpu.VMEM((IN_TN // 128, tm, 128), F32)],
        compiler_params=_cparams(("parallel", "arbitrary")),
        name="in_proj",
    )(x2d, g, w, tab)


def _qkv_kernel(cq_ref, ckv_ref, kpe_ref, gq_ref, gkv_ref, wq_ref, wk_ref, wv_ref, tab_ref,
                q_ref, k_ref, v_ref):
    qn = _rms(cq_ref[...].astype(F32), gq_ref[...]).astype(BF16)
    kvn = _rms(ckv_ref[...].astype(F32), gkv_ref[...]).astype(BF16)
    qacc = jnp.dot(qn, wq_ref[...], preferred_element_type=F32)
    kacc = jnp.dot(kvn, wk_ref[...], preferred_element_type=F32)
    vacc = jnp.dot(kvn, wv_ref[...], preferred_element_type=F32)
    lane_v = lax.broadcasted_iota(jnp.int32, vacc.shape, 1)
    v_ref[...] = jnp.where(lane_v % MLA_SLOT < MLA_V, vacc, 1.0).astype(BF16)
    half = MLA_ROPE // 2
    tq = [tab_ref[:, t * 128:(t + 1) * 128] for t in range(3)]
    tk = [tab_ref[:, t * 128:(t + 1) * 128] for t in range(3, 6)]
    krot = _rope128(kpe_ref[:, 0:128].astype(F32), tk[0], tk[1], tk[2], half)
    for h in range(MLA_HEADS):
        sl = slice(h * MLA_SLOT, (h + 1) * MLA_SLOT)
        q_ref[:, sl] = _rope128(qacc[:, sl], tq[0], tq[1], tq[2], half).astype(BF16)
        k_ref[:, sl] = (kacc[:, sl] + krot).astype(BF16)


def _mla_qkv(main, gq, gkv, wq, wk, wv, tab, tm):
    T = main.shape[0]
    const = lambda i: (0, 0)
    return pl.pallas_call(
        _qkv_kernel,
        grid=(T // tm,),
        in_specs=[
            pl.BlockSpec((tm, MLA_Q_LORA), lambda i: (i, 2048 // MLA_Q_LORA)),
            pl.BlockSpec((tm, MLA_KV_LORA), lambda i: (i, 2560 // MLA_KV_LORA)),
            pl.BlockSpec((tm, 256), lambda i: (i, 2816 // 256)),
            pl.BlockSpec((1, MLA_Q_LORA), const),
            pl.BlockSpec((1, MLA_KV_LORA), const),
            pl.BlockSpec((MLA_Q_LORA, MLA_HEADS * MLA_SLOT), const),
            pl.BlockSpec((MLA_KV_LORA, MLA_HEADS * MLA_SLOT), const),
            pl.BlockSpec((MLA_KV_LORA, MLA_HEADS * MLA_SLOT), const),
            pl.BlockSpec((tm, 6 * 128), lambda i: (i, 0)),
        ],
        out_specs=[
            pl.BlockSpec((tm, MLA_HEADS * MLA_SLOT), lambda i: (i, 0)),
            pl.BlockSpec((tm, MLA_HEADS * MLA_SLOT), lambda i: (i, 0)),
            pl.BlockSpec((tm, MLA_HEADS * MLA_SLOT), lambda i: (i, 0)),
        ],
        out_shape=[
            jax.ShapeDtypeStruct((T, MLA_HEADS * MLA_SLOT), BF16),
            jax.ShapeDtypeStruct((T, MLA_HEADS * MLA_SLOT), BF16),
            jax.ShapeDtypeStruct((T, MLA_HEADS * MLA_SLOT), BF16),
        ],
        compiler_params=_cparams(("parallel",)),
        name="mla_qkv",
    )(main, main, main, gq, gkv, wq, wk, wv, tab)


def _mla_attn_kernel(q_ref, k_ref, v_ref, o_ref, m_ref, acc_ref, *, tq):
    i = pl.program_id(2)
    pair_w = 2 * MLA_SLOT
    lane = lax.broadcasted_iota(jnp.int32, (tq, pair_w), 1)
    qh = []
    for pp in range(MLA_HPS // 2):
        qp = q_ref[0, :, pp * pair_w:(pp + 1) * pair_w]
        zero = jnp.zeros_like(qp)
        qh += [jnp.where(lane < MLA_SLOT, qp, zero), jnp.where(lane >= MLA_SLOT, qp, zero)]
    m_ref[...] = jnp.full(m_ref.shape, NEG, F32)
    acc_ref[...] = jnp.zeros(acc_ref.shape, F32)
    hq = tq // 2
    dn = (((1,), (1,)), ((), ()))

    def step(start, width, r0, mask):
        kb = k_ref[0, pl.ds(start, width), :]
        vb = v_ref[0, pl.ds(start, width), :]
        for h in range(MLA_HPS):
            pp = h // 2
            s = lax.dot_general(qh[h][r0:], kb[:, pp * pair_w:(pp + 1) * pair_w], dn,
                                preferred_element_type=F32)
            if mask is not None:
                s = jnp.where(mask, s, NEG)
            m_prev = m_ref[h, r0:, :]
            m_new = jnp.maximum(m_prev, jnp.max(s, axis=-1, keepdims=True))
            alpha = jnp.exp2(m_prev - m_new)
            p = jnp.exp2(s - jnp.concatenate([m_new] * (width // 128), axis=-1))
            pv = jnp.dot(p.astype(BF16), vb[:, h * MLA_SLOT:(h + 1) * MLA_SLOT], preferred_element_type=F32)
            acc_ref[h, r0:, :] = alpha * acc_ref[h, r0:, :] + pv
            m_ref[h, r0:, :] = m_new

    def body(j, c):
        step(pl.multiple_of(j * tq, tq), tq, 0, None)
        return c

    lax.fori_loop(0, i, body, 0)
    d0 = pl.multiple_of(i * tq, tq)
    def causal(rows):
        return lax.broadcasted_iota(jnp.int32, (rows, hq), 1) <= lax.broadcasted_iota(jnp.int32, (rows, hq), 0)

    step(d0, hq, 0, causal(tq))
    step(pl.multiple_of(d0 + hq, hq), hq, hq, causal(hq))
    lane_o = lax.broadcasted_iota(jnp.int32, (tq, 2 * MLA_V), 1)
    for pp in range(MLA_HPS // 2):
        outs = []
        for h in (2 * pp, 2 * pp + 1):
            a = acc_ref[h]
            outs.append(a / pltpu.roll(a, MLA_V, 1))
        o_ref[0, :, pp * 2 * MLA_V:(pp + 1) * 2 * MLA_V] = jnp.where(
            lane_o < MLA_V, outs[0], pltpu.roll(outs[1], MLA_V, 1)).astype(BF16)


def _mla_attn(q, k, v, B, S, tq):
    q = q.reshape(B, S, MLA_HEADS * MLA_SLOT)
    k = k.reshape(B, S, MLA_HEADS * MLA_SLOT)
    v = v.reshape(B, S, MLA_HEADS * MLA_SLOT)
    return pl.pallas_call(
        functools.partial(_mla_attn_kernel, tq=tq),
        grid=(B, MLA_HEADS // MLA_HPS, S // tq),
        in_specs=[
            pl.BlockSpec((1, tq, MLA_HPS * MLA_SLOT), lambda b, h, i: (b, i, h)),
            pl.BlockSpec((1, S, MLA_HPS * MLA_SLOT), lambda b, h, i: (b, 0, h)),
            pl.BlockSpec((1, S, MLA_HPS * MLA_SLOT), lambda b, h, i: (b, 0, h)),
        ],
        out_specs=pl.BlockSpec((1, tq, MLA_HPS * MLA_V), lambda b, h, i: (b, i, h)),
        out_shape=jax.ShapeDtypeStruct((B, S, MLA_HEADS * MLA_V), BF16),
        scratch_shapes=[pltpu.VMEM((MLA_HPS, tq, 128), F32), pltpu.VMEM((MLA_HPS, tq, MLA_SLOT), F32)],
        compiler_params=_cparams(("parallel", "parallel", "arbitrary")),
        name="mla_attn",
    )(q, k, v)


def _dil_kernel(qkv_ref, o_ref, lse_ref, bias_ref, *, nb):
    dil = qkv_ref.shape[1]
    width = 2 * DIL_BLK if nb > 1 else DIL_BLK
    row = lax.broadcasted_iota(jnp.int32, (DIL_BLK, width), 0)
    col = lax.broadcasted_iota(jnp.int32, (DIL_BLK, width), 1)
    bias_ref[0] = jnp.where(col <= row, 0.0, NEG)
    if nb > 1:
        later = jnp.logical_or(jnp.logical_and(col >= DIL_BLK, col - DIL_BLK <= row),
                               jnp.logical_and(col < DIL_BLK, col >= row))
        bias_ref[1] = jnp.where(later, 0.0, NEG)
    head_of_lane = lax.broadcasted_iota(jnp.int32, (DIL_BLK, DIL_W), 1) // DIL_HEAD_DIM
    dn = (((1,), (1,)), ((), ()))

    def by_head(parts):
        out = parts[DIL_HEADS - 1]
        for h in range(DIL_HEADS - 2, -1, -1):
            out = jnp.where(head_of_lane == h, parts[h], out)
        return out

    def unit(u):
        r = u // nb
        n = u % nb
        q0 = pl.multiple_of(n * DIL_BLK, DIL_BLK)
        q = qkv_ref[0, r, pl.ds(q0, DIL_BLK), 0:DIL_W]
        if nb > 1:
            w0 = pl.multiple_of(jnp.maximum(n - 1, 0) * DIL_BLK, DIL_BLK)
            bias = bias_ref[jnp.minimum(n, 1)]
        else:
            w0 = 0
            bias = bias_ref[0]
        kw = qkv_ref[0, r, pl.ds(w0, width), DIL_W:2 * DIL_W]
        vw = qkv_ref[0, r, pl.ds(w0, width), 2 * DIL_W:3 * DIL_W]
        zero = jnp.zeros_like(q)
        qs = jnp.concatenate([jnp.where(head_of_lane == h, q, zero) for h in range(DIL_HEADS)], axis=0)
        s = lax.dot_general(qs, kw, dn, preferred_element_type=F32)
        s = (s.reshape(DIL_HEADS, DIL_BLK, width) + bias[None]).reshape(DIL_HEADS * DIL_BLK, width)
        m = jnp.max(s, axis=-1, keepdims=True)
        e = jnp.exp2(s - m)
        den = jnp.sum(e, axis=-1, keepdims=True)
        pv = jnp.dot(e.astype(BF16), vw, preferred_element_type=F32)
        lse = m + jnp.log2(den)
        blk = lambda t, h: t[h * DIL_BLK:(h + 1) * DIL_BLK]
        o = by_head([blk(pv, h) for h in range(DIL_HEADS)]) / by_head(
            [jnp.broadcast_to(blk(den, h), (DIL_BLK, DIL_W)) for h in range(DIL_HEADS)])
        o_ref[0, r, pl.ds(q0, DIL_BLK), :] = o.astype(BF16)
        lse_ref[0, r, pl.ds(q0, DIL_BLK), :] = by_head(
            [jnp.broadcast_to(blk(lse, h), (DIL_BLK, DIL_W)) for h in range(DIL_HEADS)])

    def body(t, carry):
        unit(2 * t)
        unit(2 * t + 1)
        return carry

    lax.fori_loop(0, dil * nb // 2, body, 0)


def _dil_attn(qkv, B, dil, L):
    nb = L // DIL_BLK
    return pl.pallas_call(
        functools.partial(_dil_kernel, nb=nb),
        grid=(B,),
        in_specs=[pl.BlockSpec((1, dil, L, 3 * DIL_W), lambda b: (b, 0, 0, 0))],
        out_specs=[
            pl.BlockSpec((1, dil, L, DIL_W), lambda b: (b, 0, 0, 0)),
            pl.BlockSpec((1, dil, L, DIL_W), lambda b: (b, 0, 0, 0)),
        ],
        out_shape=[
            jax.ShapeDtypeStruct((B, dil, L, DIL_W), BF16),
            jax.ShapeDtypeStruct((B, dil, L, DIL_W), F32),
        ],
        scratch_shapes=[pltpu.VMEM((2, DIL_BLK, 2 * DIL_BLK if nb > 1 else DIL_BLK), F32)],
        compiler_params=_cparams(("parallel",)),
        name=f"dil_attn_d{dil}",
    )(qkv)


def _post_kernel(x_ref, oa_ref, og0_ref, og1_ref, og2_ref, ls0_ref, ls1_ref, ls2_ref, gates_ref,
                 wa_ref, wb_ref, wo_ref, gffn_ref, wr_ref, br_ref,
                 x1_ref, h2_ref, lg_ref, ob_ref, *, tm):
    ls =[r[0] for r in (ls0_ref, ls1_ref, ls2_ref)]
    og = [r[0] for r in (og0_ref, og1_ref, og2_ref)]
    for gi, (_, dil) in enumerate(DIL_PATTERN):
        rows = tm // dil
        for r in range(dil):
            for c in range(2):
                sl = slice(c * 128, (c + 1) * 128)
                ob_ref[2 * gi + c, pl.ds(r, rows, stride=dil), :] = og[gi][r][:, sl].astype(F32)
                ob_ref[6 + 2 * gi + c, pl.ds(r, rows, stride=dil), :] = ls[gi][r][:, sl]

    def tok_major(k):
        return jnp.concatenate([ob_ref[2 * k], ob_ref[2 * k + 1]], axis=-1)

    l0, l1, l2 = tok_major(3), tok_major(4), tok_major(5)
    mx = jnp.maximum(jnp.maximum(l0, l1), l2)
    w0, w1, w2 = jnp.exp2(l0 - mx), jnp.exp2(l1 - mx), jnp.exp2(l2 - mx)
    ob = (w0 * tok_major(0) + w1 * tok_major(1) + w2 * tok_major(2)) / (w0 + w1 + w2)
    ya = jnp.dot(oa_ref[...], wa_ref[...], preferred_element_type=F32)
    yb = jnp.dot(ob.astype(BF16), wb_ref[...], preferred_element_type=F32)
    merged = (jax.nn.sigmoid(gates_ref[:, 0:D_MODEL].astype(F32)) * ya
              + jax.nn.sigmoid(gates_ref[:, D_MODEL:2 * D_MODEL].astype(F32)) * yb)
    x1 = x_ref[...] + jnp.dot(merged.astype(BF16), wo_ref[...], preferred_element_type=F32)
    x1_ref[...] = x1
    h2 = _rms(x1, gffn_ref[...])
    for c in range(D_MODEL // 128):
        h2_ref[pl.ds(c, tm, stride=D_MODEL // 128), :] = h2[:, c * 128:(c + 1) * 128]
    lg_ref[...] = jnp.dot(h2.astype(BF16), wr_ref[...], preferred_element_type=F32) + br_ref[...]


def _post(x2d, oa, ogs, lss, main, wa, wb, wo, gffn, wr, br, B, S, tm):
    T = x2d.shape[0]
    nt = S // tm
    const = lambda i: (0, 0)
    res_specs = [pl.BlockSpec((1, d, tm // d, DIL_W), lambda i, nt=nt: (i // nt, 0, i % nt, 0))
                 for _, d in DIL_PATTERN]
    return pl.pallas_call(
        functools.partial(_post_kernel, tm=tm),
        grid=(T // tm,),
        in_specs=[pl.BlockSpec((tm, D_MODEL), lambda i: (i, 0)),
                  pl.BlockSpec((tm, D_MODEL), lambda i: (i, 0))]
                 + res_specs + res_specs
                 + [pl.BlockSpec((tm, 2 * D_MODEL), lambda i: (i, 0)),
                    pl.BlockSpec((D_MODEL, D_MODEL), const),
                    pl.BlockSpec((DIL_W, D_MODEL), const),
                    pl.BlockSpec((D_MODEL, D_MODEL), const),
                    pl.BlockSpec((1, D_MODEL), const),
                    pl.BlockSpec((D_MODEL, 128), const),
                    pl.BlockSpec((1, 128), const)],
        out_specs=[pl.BlockSpec((tm, D_MODEL), lambda i: (i, 0)),
                   pl.BlockSpec((tm * ROW_TILE, 128), lambda i: (i, 0)),
                   pl.BlockSpec((tm, 128), lambda i: (i, 0))],
        out_shape=[jax.ShapeDtypeStruct((T, D_MODEL), F32),
                   jax.ShapeDtypeStruct((T * ROW_TILE, 128), F32),
                   jax.ShapeDtypeStruct((T, 128), F32)],
        scratch_shapes=[pltpu.VMEM((12, tm, 128), F32)],
        compiler_params=_cparams(("parallel",)),
        name="post_attn",
    )(x2d, oa, *ogs, *lss, main, wa, wb, wo, gffn, wr, br)


def _route_kernel(lg_ref, ri_ref, rw_ref, cnt_ref, carry_ref, *, tm):
    i = pl.program_id(0)

    @pl.when(i == 0)
    def _():
        carry_ref[...] = jnp.zeros(carry_ref.shape, F32)

    lg = lg_ref[...]
    lane = lax.broadcasted_iota(jnp.int32, lg.shape, 1)
    lane_f = lane.astype(F32)
    ninf = jnp.float32(-jnp.inf)

    def first_max(vals):
        vmax = jnp.max(vals, axis=-1, keepdims=True)
        idx = jnp.min(jnp.where(vals == vmax, lane_f, 128.0), axis=-1, keepdims=True)
        return vmax, idx.astype(jnp.int32)

    gl = jnp.where(lane < N_GROUPS, lg, ninf)
    gmax, g_sel = first_max(gl)
    p_g = 1.0 / jnp.sum(jnp.exp(gl - gmax), axis=-1, keepdims=True)
    lo = N_GROUPS + g_sel * EXPERTS_PER_GROUP
    el = jnp.where(jnp.logical_and(lane >= lo, lane < lo + EXPERTS_PER_GROUP), lg, ninf)
    v0, i0 = first_max(el)
    v1, i1 = first_max(jnp.where(lane == i0, ninf, el))
    t = jnp.exp(v1 - v0)
    w0 = p_g / (1.0 + t)
    w1 = p_g * t / (1.0 + t)
    e0 = i0 - N_GROUPS
    e1 = i1 - N_GROUPS
    hit0 = lane == e0
    hit1 = lane == e1
    oh = jnp.where(jnp.logical_or(hit0, hit1), 1.0, 0.0).astype(F32)
    r_i = lax.broadcasted_iota(jnp.int32, (tm, tm), 0)
    c_i = lax.broadcasted_iota(jnp.int32, (tm, tm), 1)
    lower = jnp.where(c_i < r_i, 1.0, 0.0).astype(BF16)
    excl = jnp.dot(lower, oh.astype(BF16), preferred_element_type=F32) + carry_ref[...]
    r0 = jnp.sum(jnp.where(hit0, excl, 0.0), axis=-1, keepdims=True).astype(jnp.int32)
    r1 = jnp.sum(jnp.where(hit1, excl, 0.0), axis=-1, keepdims=True).astype(jnp.int32)
    carry_ref[...] = carry_ref[...] + jnp.sum(oh, axis=0, keepdims=True)
    zi = jnp.zeros(lg.shape, jnp.int32)
    ri_ref[...] = jnp.where(lane == 0, e0, jnp.where(lane == 1, e1, jnp.where(lane == 2, r0, jnp.where(lane == 3, r1, zi))))
    rw_ref[...] = jnp.where(lane == 0, w0, jnp.where(lane == 1, w1, jnp.zeros(lg.shape, F32)))
    cnt_ref[...] = carry_ref[...]


def _route(lg, tm):
    T = lg.shape[0]
    return pl.pallas_call(
        functools.partial(_route_kernel, tm=tm),
        grid=(T // tm,),
        in_specs=[pl.BlockSpec((tm, 128), lambda i: (i, 0))],
        out_specs=[pl.BlockSpec((tm, 128), lambda i: (i, 0)),
                   pl.BlockSpec((tm, 128), lambda i: (i, 0)),
                   pl.BlockSpec((1, 128), lambda i: (0, 0))],
        out_shape=[jax.ShapeDtypeStruct((T, 128), jnp.int32),
                   jax.ShapeDtypeStruct((T, 128), F32),
                   jax.ShapeDtypeStruct((1, 128), F32)],
        scratch_shapes=[pltpu.VMEM((1, 128), F32)],
        compiler_params=_cparams(("arbitrary",)),
        name="route",
    )(lg)


def _issue_rows(idx_ref, n_rows, src_hbm, buf, sem, unrolled=True):
    def start(r):
        src0 = pl.multiple_of(idx_ref[0, 0, r], ROW_TILE)
        pltpu.make_async_copy(src_hbm.at[pl.ds(src0, ROW_TILE), :], buf.at[pl.ds(r * ROW_TILE, ROW_TILE), :],
                              sem).start()

    if unrolled:
        for r in range(n_rows):
            start(r)
    else:
        lax.fori_loop(0, n_rows, lambda r, c: (start(r), c)[1], 0)


def _wait_rows(n_rows, src_hbm, buf, sem):
    pltpu.make_async_copy(src_hbm.at[pl.ds(0, n_rows * ROW_TILE), :], buf, sem).wait()


def _rows_from_tiles(ref, first, n_rows):
    return jnp.concatenate(
        [ref[pl.ds(first * ROW_TILE + c, n_rows, stride=ROW_TILE), :] for c in range(ROW_TILE)], axis=-1)


def _rows_to_tiles(ref, val):
    for c in range(ROW_TILE):
        ref[pl.ds(c, val.shape[0], stride=ROW_TILE), :] = val[:, c * 128:(c + 1) * 128]


def _ffn_kernel(be_ref, nvb_ref, rt_cur_ref, rt_nxt_ref, h_hbm, w1_ref, w3_ref, w2_ref, y_ref, xbuf, sem):
    b = pl.program_id(0)
    nvb = nvb_ref[0]
    slot = b % 2

    @pl.when(jnp.logical_and(b == 0, nvb > 0))
    def _():
        _issue_rows(rt_cur_ref, FFN_BM, h_hbm, xbuf.at[0], sem.at[0], unrolled=False)

    @pl.when(b + 1 < nvb)
    def _():
        _issue_rows(rt_nxt_ref, FFN_BM, h_hbm, xbuf.at[1 - slot], sem.at[1 - slot])

    @pl.when(b < nvb)
    def _():
        _wait_rows(FFN_BM, h_hbm, xbuf.at[slot], sem.at[slot])
        xb = _rows_from_tiles(xbuf.at[slot], 0, FFN_BM).astype(BF16)
        h1 = jnp.dot(xb, w1_ref[0, 0].astype(BF16), preferred_element_type=F32)
        h3 = jnp.dot(xb, w3_ref[0, 0].astype(BF16), preferred_element_type=F32)
        a = (jax.nn.silu(h1) * h3).astype(BF16)
        _rows_to_tiles(y_ref, jnp.dot(a, w2_ref[0, 0].astype(BF16), preferred_element_type=F32))

    @pl.when(b >= nvb)
    def _():
        y_ref[...] = jnp.zeros(y_ref.shape, F32)


def _expert_ffn(block_e, nvb, row_tok, h2, w1, w3, w2, layer):
    nb = block_e.shape[0]
    row_tok = row_tok.reshape(nb, 1, FFN_BM)
    return pl.pallas_call(
        _ffn_kernel,
        grid_spec=pltpu.PrefetchScalarGridSpec(
            num_scalar_prefetch=2,
            grid=(nb,),
            in_specs=[
                pl.BlockSpec((1, 1, FFN_BM), lambda b, be, nv: (b, 0, 0), memory_space=pltpu.SMEM),
                pl.BlockSpec((1, 1, FFN_BM), lambda b, be, nv: (jnp.minimum(b + 1, nb - 1), 0, 0),
                             memory_space=pltpu.SMEM),
                pl.BlockSpec(memory_space=pl.ANY),
                pl.BlockSpec((1, 1, D_MODEL, EXPERT_FF), lambda b, be, nv: (layer, be[b], 0, 0)),
                pl.BlockSpec((1, 1, D_MODEL, EXPERT_FF), lambda b, be, nv: (layer, be[b], 0, 0)),
                pl.BlockSpec((1, 1, EXPERT_FF, D_MODEL), lambda b, be, nv: (layer, be[b], 0, 0)),
            ],
            out_specs=pl.BlockSpec((FFN_BM * ROW_TILE, 128), lambda b, be, nv: (b, 0)),
            scratch_shapes=[pltpu.VMEM((2, FFN_BM * ROW_TILE, 128), F32), pltpu.SemaphoreType.DMA((2,))],
        ),
        out_shape=jax.ShapeDtypeStruct((nb * FFN_BM * ROW_TILE, 128), F32),
        compiler_params=_cparams(("arbitrary",), bounds_checks=False),
        name="expert_ffn",
    )(block_e, nvb, row_tok, row_tok, h2, w1, w3, w2)


def _comb_kernel(d_cur_ref, d_nxt_ref, ys_hbm, x1_ref, rw_ref, p_ref, gple_ref, wpg_ref, wpp_ref, o_ref,
                 ybuf, sem, *, tm):
    i = pl.program_id(0)
    slot = i % 2

    @pl.when(i == 0)
    def _():
        _issue_rows(d_cur_ref, 2 * tm, ys_hbm, ybuf.at[0], sem.at[0], unrolled=False)

    @pl.when(i + 1 < pl.num_programs(0))
    def _():
        _issue_rows(d_nxt_ref, 2 * tm, ys_hbm, ybuf.at[1 - slot], sem.at[1 - slot])

    _wait_rows(2 * tm, ys_hbm, ybuf.at[slot], sem.at[slot])
    w = rw_ref[...]
    y0 = _rows_from_tiles(ybuf.at[slot], 0, tm)
    y1 = _rows_from_tiles(ybuf.at[slot], tm, tm)
    x2 = x1_ref[...] + (y0 * w[:, 0:1] + y1 * w[:, 1:2])
    e = jnp.dot(p_ref[...].astype(BF16), wpp_ref[...], preferred_element_type=F32)
    gate = jax.nn.sigmoid(jnp.dot(_rms(x2, gple_ref[...]).astype(BF16), wpg_ref[...], preferred_element_type=F32))
    o_ref[...] = x2 + gate * e


def _combine_ple(dest_t, ys, x1, rw, p2d, gple, wpg, wpp, tm, layer):
    T = x1.shape[0]
    nt = T // tm
    const = lambda i: (0, 0)
    return pl.pallas_call(
        functools.partial(_comb_kernel, tm=tm),
        grid=(nt,),
        in_specs=[
            pl.BlockSpec((1, 1, 2 * tm), lambda i: (i, 0, 0), memory_space=pltpu.SMEM),
            pl.BlockSpec((1, 1, 2 * tm), lambda i: (jnp.minimum(i + 1, nt - 1), 0, 0), memory_space=pltpu.SMEM),
            pl.BlockSpec(memory_space=pl.ANY),
            pl.BlockSpec((tm, D_MODEL), lambda i: (i, 0)),
            pl.BlockSpec((tm, 128), lambda i: (i, 0)),
            pl.BlockSpec((tm, PLE_DIM), lambda i: (layer * nt + i, 0)),
            pl.BlockSpec((1, D_MODEL), const),
            pl.BlockSpec((D_MODEL, D_MODEL), const),
            pl.BlockSpec((PLE_DIM, D_MODEL), const),
        ],
        out_specs=pl.BlockSpec((tm, D_MODEL), lambda i: (i, 0)),
        out_shape=jax.ShapeDtypeStruct((T, D_MODEL), F32),
        scratch_shapes=[pltpu.VMEM((2, 2 * tm * ROW_TILE, 128), F32), pltpu.SemaphoreType.DMA((2,))],
        compiler_params=_cparams(("arbitrary",)),
        name="combine_ple",
    )(dest_t, dest_t, ys, x1, rw, p2d, gple, wpg, wpp)


def _final_kernel(x_ref, g_ref, o_ref):
    o_ref[...] = _rms(x_ref[...], g_ref[...])


def _final_norm(x2d, g, tm):
    T = x2d.shape[0]
    return pl.pallas_call(
        _final_kernel,
        grid=(T // tm,),
        in_specs=[pl.BlockSpec((tm, D_MODEL), lambda i: (i, 0)), pl.BlockSpec((1, D_MODEL), lambda i: (0, 0))],
        out_specs=pl.BlockSpec((tm, D_MODEL), lambda i: (i, 0)),
        out_shape=jax.ShapeDtypeStruct((T, D_MODEL), F32),
        compiler_params=_cparams(("parallel",)),
        name="final_norm",
    )(x2d, g)


def _rope_tables(pos, theta, rot_dim, period, offset, scale):
    half = rot_dim // 2
    inv = jnp.float32(theta) ** (-jnp.arange(half, dtype=F32) * 2.0 / rot_dim)
    ang = pos.astype(F32)[:, None] * inv
    cos, sin = jnp.cos(ang), jnp.sin(ang)
    T = pos.shape[0]
    c = jnp.ones((T, period), F32).at[:, offset:offset + half].set(cos).at[:, offset + half:offset + rot_dim].set(cos)
    s_up = jnp.zeros((T, period), F32).at[:, offset + half:offset + rot_dim].set(sin)
    s_dn = jnp.zeros((T, period), F32).at[:, offset:offset + half].set(-sin)
    rep = 128 // period
    return jnp.concatenate([jnp.tile(t, (1, rep)) for t in (c, s_up, s_dn)], axis=1) * jnp.float32(scale)


def _prep_w_in(w):
    c_q = w[:, 0:512]
    c_kv = w[:, 512:768]
    k_pe = w[:, 768:800]
    dil = w[:, 800:800 + 2304]
    gates = w[:, 3104:5152]
    kslot = jnp.zeros((D_MODEL, 256), w.dtype).at[:, MLA_NOPE:MLA_NOPE + MLA_ROPE].set(k_pe)
    return jnp.concatenate([gates, c_q, c_kv, kslot, dil], axis=1).astype(BF16)


def _prep_w_q(w):
    w = w.reshape(MLA_Q_LORA, MLA_HEADS, MLA_NOPE + MLA_ROPE)
    w = jnp.pad(w, ((0, 0), (0, 0), (0, MLA_SLOT - MLA_NOPE - MLA_ROPE)))
    return w.reshape(MLA_Q_LORA, MLA_HEADS * MLA_SLOT).astype(BF16)


def _prep_w_kv(w):
    w = w.reshape(MLA_KV_LORA, MLA_HEADS, MLA_NOPE + MLA_V)
    wk = jnp.pad(w[:, :, :MLA_NOPE], ((0, 0), (0, 0), (0, MLA_SLOT - MLA_NOPE)))
    wv = jnp.pad(w[:, :, MLA_NOPE:], ((0, 0), (0, 0), (0, MLA_SLOT - MLA_V)))
    return (wk.reshape(MLA_KV_LORA, MLA_HEADS * MLA_SLOT).astype(BF16),
            wv.reshape(MLA_KV_LORA, MLA_HEADS * MLA_SLOT).astype(BF16))


def _dest_kernel(ri_ref, ps_ref, o_ref):
    ri = ri_ref[...].astype(F32)
    lane = lax.broadcasted_iota(jnp.int32, ri.shape, 1)
    ps = ps_ref[...]

    def col(k):
        return jnp.sum(jnp.where(lane == k, ri, 0.0), axis=-1, keepdims=True)

    def dest(k):
        start = jnp.sum(jnp.where(lane == col(k).astype(jnp.int32), ps, 0.0), axis=-1, keepdims=True)
        return (start + col(2 + k)).astype(jnp.int32)

    o_ref[...] = jnp.where(lane == 0, dest(0), jnp.where(lane == 1, dest(1), 0))


def _dest_rows(ri, pstarts, tm):
    T = ri.shape[0]
    ps = jnp.zeros((1, 128), F32).at[0, :N_EXPERTS].set(pstarts.astype(F32))
    return pl.pallas_call(
        _dest_kernel,
        grid=(T // tm,),
        in_specs=[pl.BlockSpec((tm, 128), lambda i: (i, 0)), pl.BlockSpec((1, 128), lambda i: (0, 0))],
        out_specs=pl.BlockSpec((tm, 128), lambda i: (i, 0)),
        out_shape=jax.ShapeDtypeStruct((T, 128), jnp.int32),
        compiler_params=_cparams(("parallel",)),
        name="dest_rows",
    )(ri, ps)


def _dispatch_plan(ri, cnt, T, tm):
    counts = cnt[0, :N_EXPERTS].astype(jnp.int32)
    pcounts = (counts + FFN_BM - 1) // FFN_BM * FFN_BM
    pends = jnp.cumsum(pcounts)
    pstarts = pends - pcounts
    dest = _dest_rows(ri, pstarts, tm)[:, 0:TOP_K]
    n_blocks = (T * TOP_K) // FFN_BM + N_EXPERTS
    row_tok = jnp.zeros((n_blocks * FFN_BM,), jnp.int32).at[dest.reshape(-1)].set(
        jnp.arange(T * TOP_K, dtype=jnp.int32) // TOP_K * ROW_TILE, unique_indices=True)
    block_e = jnp.minimum(jnp.searchsorted(pends, jnp.arange(n_blocks, dtype=jnp.int32) * FFN_BM, side='right'),
                          N_EXPERTS - 1).astype(jnp.int32)
    nvb = (pends[-1] // FFN_BM).astype(jnp.int32).reshape(1)
    return dest * ROW_TILE, row_tok, block_e, nvb


def kernel(x, p, positions, g_mix, w_in, g_q_lat, w_q_up, g_kv_lat, w_kv_up, w_branch_a, w_branch_b, w_out, g_ffn, w_router_grp, b_router_grp, w_router_exp, b_router_exp, w_exp_gate, w_exp_up, w_exp_down, g_ple, w_ple_gate, w_ple_proj, g_final):
    B, S, D = x.shape
    T = B * S
    TM_IN, TM_QKV, TQ, TM_POST, TM_ROUTE, TM_COMB = 512, 512, 512, 256, 512, 256

    pos = positions.reshape(T)
    dil_scale = DIL_HEAD_DIM ** -0.5 * LOG2E
    mla_scale = (MLA_NOPE + MLA_ROPE) ** -0.5 * LOG2E
    tab_dil = jnp.concatenate([_rope_tables(pos, ROPE_THETA, PARTIAL_ROT, DIL_HEAD_DIM, 0, dil_scale),
                               _rope_tables(pos, ROPE_THETA, PARTIAL_ROT, DIL_HEAD_DIM, 0, 1.0)], axis=1)
    tab_mla = jnp.concatenate([_rope_tables(pos, MLA_ROPE_THETA, MLA_ROPE, MLA_SLOT, MLA_NOPE, mla_scale),
                               _rope_tables(pos, MLA_ROPE_THETA, MLA_ROPE, MLA_SLOT, MLA_NOPE, 1.0)], axis=1)

    xc = x.reshape(T, D)
    for i in range(DEPTH):
        w_in_p = _prep_w_in(w_in[i])
        wq = _prep_w_q(w_q_up[i])
        wk, wv = _prep_w_kv(w_kv_up[i])
        wr = jnp.zeros((D, 128), F32).at[:, :N_GROUPS].set(w_router_grp[i]).at[:, N_GROUPS:N_GROUPS + N_EXPERTS].set(
            w_router_exp[i]).astype(BF16)
        br = jnp.zeros((1, 128), F32).at[0, :N_GROUPS].set(b_router_grp[i]).at[0, N_GROUPS:N_GROUPS + N_EXPERTS].set(
            b_router_exp[i].reshape(N_EXPERTS))

        main, d0, d1, d2 = _in_proj(xc, g_mix[i].reshape(1, D), w_in_p, tab_dil, B, S, TM_IN)
        q, k, v = _mla_qkv(main, g_q_lat[i].reshape(1, -1), g_kv_lat[i].reshape(1, -1), wq, wk, wv, tab_mla, TM_QKV)
        o_a = _mla_attn(q, k, v, B, S, TQ).reshape(T, MLA_HEADS * MLA_V)
        ogs, lss = [], []
        for (_, dil), qkv in zip(DIL_PATTERN, (d0, d1, d2)):
            og, ls = _dil_attn(qkv, B, dil, S // dil)
            ogs.append(og)
            lss.append(ls)
        x1, h2, lg = _post(xc, o_a, ogs, lss, main, w_branch_a[i].astype(BF16), w_branch_b[i].astype(BF16),
                           w_out[i].astype(BF16), g_ffn[i].reshape(1, D), wr, br, B, S, TM_POST)
        ri, rw, cnt = _route(lg, TM_ROUTE)
        dest, row_tok, block_e, nvb = _dispatch_plan(ri, cnt, T, TM_ROUTE)
        ys = _expert_ffn(block_e, nvb, row_tok, h2, w_exp_gate, w_exp_up, w_exp_down, i)
        dest_t = dest.reshape(T // TM_COMB, TM_COMB, TOP_K).transpose(0, 2, 1).reshape(T // TM_COMB, 1, TOP_K * TM_COMB)
        xc = _combine_ple(dest_t, ys, x1, rw, p.reshape(DEPTH * T, PLE_DIM), g_ple[i].reshape(1, D),
                          w_ple_gate[i].astype(BF16), w_ple_proj[i].astype(BF16), TM_COMB, i)
    return _final_norm(xc, g_final.reshape(1, D), 512).reshape(B, S, D)
```

```python
import functools
import math

import jax
import jax.numpy as jnp
from jax import lax
from jax.experimental import pallas as pl
from jax.experimental.pallas import tpu as pltpu
from jax.experimental.pallas import tpu_sc as plsc

F32 = jnp.float32
BF16 = jnp.bfloat16

D_MODEL = 1024
DEPTH = 4
RMS_EPS = 1e-6
NEG = -1e30
LOG2E = math.log2(math.e)

MLA_HEADS = 16
MLA_Q_LORA = 512
MLA_KV_LORA = 256
MLA_NOPE = 64
MLA_ROPE = 32
MLA_V = 64
MLA_ROPE_THETA = 10000.0
MLA_SLOT = 128
MLA_HPS = 2

DIL_PATTERN = ((128, 1), (512, 4), (2048, 16))
DIL_GROUPS = 3
DIL_HEADS = 4
DIL_HEAD_DIM = 64
DIL_W = DIL_HEADS * DIL_HEAD_DIM
DIL_BLK = 128
ROPE_THETA = 500000.0
PARTIAL_ROT = DIL_HEAD_DIM // 4

N_GROUPS = 8
EXPERTS_PER_GROUP = 8
N_EXPERTS = 64
TOP_K = 2
EXPERT_FF = 256
PLE_DIM = 256

IN_TN = 768
MAIN_COLS = 3072
IN_COLS_PAD = MAIN_COLS + DIL_GROUPS * 3 * DIL_W
N_MAIN_TILES = MAIN_COLS // IN_TN

FFN_BM = 256
ROW_TILE = D_MODEL // 128
VMEM_LIMIT = 48 * 1024 * 1024

SC_CORES = 2
SC_SUBCORES = 16
SC_CHUNK = 32


def _cparams(sem, bounds_checks=True):
    return pltpu.CompilerParams(dimension_semantics=sem, vmem_limit_bytes=VMEM_LIMIT,
                                disable_bounds_checks=not bounds_checks)


def _rms(x, g):
    return x * lax.rsqrt(jnp.mean(x * x, axis=-1, keepdims=True) + RMS_EPS) * g


def _rope128(x, c, s_up, s_dn, half):
    return x * c + pltpu.roll(x, half, 1) * s_up + pltpu.roll(x, 128 - half, 1) * s_dn


def _in_kernel(x_ref, g_ref, w_ref, tab_ref, main_ref, d0_ref, d1_ref, d2_ref, xn_ref, acc_ref):
    j = pl.program_id(1)

    @pl.when(j == 0)
    def _():
        xn_ref[...] = _rms(x_ref[...], g_ref[...]).astype(BF16)

    acc = jnp.dot(xn_ref[...], w_ref[...], preferred_element_type=F32)

    @pl.when(j < N_MAIN_TILES)
    def _():
        main_ref[...] = acc.astype(BF16)

    def dil_tile(out_ref, dil):
        for c in range(4):
            t0 = 0 if c < 2 else 3
            x = acc[:, c * 128:(c + 1) * 128]
            acc_ref[c] = _rope128(
                x, tab_ref[:, t0 * 128:(t0 + 1) * 128], tab_ref[:, (t0 + 1) * 128:(t0 + 2) * 128],
                tab_ref[:, (t0 + 2) * 128:(t0 + 3) * 128], PARTIAL_ROT // 2)
        for c in range(4, 6):
            acc_ref[c] = acc[:, c * 128:(c + 1) * 128]
        rows = acc_ref.shape[1] // dil
        for r in range(dil):
            for c in range(6):
                out_ref[0, r, :, c * 128:(c + 1) * 128] = acc_ref[c, pl.ds(r, rows, stride=dil), :].astype(BF16)

    for gi, (_, dil) in enumerate(DIL_PATTERN):
        pl.when(j == N_MAIN_TILES + gi)(functools.partial(dil_tile, (d0_ref, d1_ref, d2_ref)[gi], dil))


def _in_proj(x2d, g, w, tab, B, S, tm):
    T = x2d.shape[0]
    nt = S // tm
    dil_shapes = [jax.ShapeDtypeStruct((B, d, S // d, 3 * DIL_W), BF16) for _, d in DIL_PATTERN]
    dil_specs = [pl.BlockSpec((1, d, tm // d, 3 * DIL_W), lambda i, j, nt=nt: (i // nt, 0, i % nt, 0))
                 for _, d in DIL_PATTERN]
    return pl.pallas_call(
        _in_kernel,
        grid=(T // tm, IN_COLS_PAD // IN_TN),
        in_specs=[
            pl.BlockSpec((tm, D_MODEL), lambda i, j: (i, 0)),
            pl.BlockSpec((1, D_MODEL), lambda i, j: (0, 0)),
            pl.BlockSpec((D_MODEL, IN_TN), lambda i, j: (0, j)),
            pl.BlockSpec((tm, 6 * 128), lambda i, j: (i, 0)),
        ],
        out_specs=[pl.BlockSpec((tm, IN_TN), lambda i, j: (i, jnp.minimum(j, N_MAIN_TILES - 1)))] + dil_specs,
        out_shape=[jax.ShapeDtypeStruct((T, MAIN_COLS), BF16)] + dil_shapes,
        scratch_shapes=[pltpu.VMEM((tm, D_MODEL), BF16), pltpu.VMEM((IN_TN // 128, tm, 128), F32)],
        compiler_params=_cparams(("parallel", "arbitrary")),
        name="in_proj",
    )(x2d, g, w, tab)


def _qkv_kernel(cq_ref, ckv_ref, kpe_ref, gq_ref, gkv_ref, wq_ref, wk_ref, wv_ref, tab_ref,
                q_ref, k_ref, v_ref):
    qn = _rms(cq_ref[...].astype(F32), gq_ref[...]).astype(BF16)
    kvn = _rms(ckv_ref[...].astype(F32), gkv_ref[...]).astype(BF16)
    qacc = jnp.dot(qn, wq_ref[...], preferred_element_type=F32)
    kacc = jnp.dot(kvn, wk_ref[...], preferred_element_type=F32)
    vacc = jnp.dot(kvn, wv_ref[...], preferred_element_type=F32)
    lane_v = lax.broadcasted_iota(jnp.int32, vacc.shape, 1)
    v_ref[...] = jnp.where(lane_v % MLA_SLOT < MLA_V, vacc, 1.0).astype(BF16)
    half = MLA_ROPE // 2
    tq = [tab_ref[:, t * 128:(t + 1) * 128] for t in range(3)]
    tk = [tab_ref[:, t * 128:(t + 1) * 128] for t in range(3, 6)]
    krot = _rope128(kpe_ref[:, 0:128].astype(F32), tk[0], tk[1], tk[2], half)
    for h in range(MLA_HEADS):
        sl = slice(h * MLA_SLOT, (h + 1) * MLA_SLOT)
        q_ref[:, sl] = _rope128(qacc[:, sl], tq[0], tq[1], tq[2], half).astype(BF16)
        k_ref[:, sl] = (kacc[:, sl] + krot).astype(BF16)


def _mla_qkv(main, gq, gkv, wq, wk, wv, tab, tm):
    T = main.shape[0]
    const = lambda i: (0, 0)
    return pl.pallas_call(
        _qkv_kernel,
        grid=(T // tm,),
        in_specs=[
            pl.BlockSpec((tm, MLA_Q_LORA), lambda i: (i, 2048 // MLA_Q_LORA)),
            pl.BlockSpec((tm, MLA_KV_LORA), lambda i: (i, 2560 // MLA_KV_LORA)),
            pl.BlockSpec((tm, 256), lambda i: (i, 2816 // 256)),
            pl.BlockSpec((1, MLA_Q_LORA), const),
            pl.BlockSpec((1, MLA_KV_LORA), const),
            pl.BlockSpec((MLA_Q_LORA, MLA_HEADS * MLA_SLOT), const),
            pl.BlockSpec((MLA_KV_LORA, MLA_HEADS * MLA_SLOT), const),
            pl.BlockSpec((MLA_KV_LORA, MLA_HEADS * MLA_SLOT), const),
            pl.BlockSpec((tm, 6 * 128), lambda i: (i, 0)),
        ],
        out_specs=[
            pl.BlockSpec((tm, MLA_HEADS * MLA_SLOT), lambda i: (i, 0)),
            pl.BlockSpec((tm, MLA_HEADS * MLA_SLOT), lambda i: (i, 0)),
            pl.BlockSpec((tm, MLA_HEADS * MLA_SLOT), lambda i: (i, 0)),
        ],
        out_shape=[
            jax.ShapeDtypeStruct((T, MLA_HEADS * MLA_SLOT), BF16),
            jax.ShapeDtypeStruct((T, MLA_HEADS * MLA_SLOT), BF16),
            jax.ShapeDtypeStruct((T, MLA_HEADS * MLA_SLOT), BF16),
        ],
        compiler_params=_cparams(("parallel",)),
        name="mla_qkv",
    )(main, main, main, gq, gkv, wq, wk, wv, tab)


def _mla_attn_kernel(q_ref, k_ref, v_ref, o_ref, m_ref, acc_ref, *, tq):
    i = pl.program_id(2)
    pair_w = 2 * MLA_SLOT
    lane = lax.broadcasted_iota(jnp.int32, (tq, pair_w), 1)
    qh = []
    for pp in range(MLA_HPS // 2):
        qp = q_ref[0, :, pp * pair_w:(pp + 1) * pair_w]
        zero = jnp.zeros_like(qp)
        qh += [jnp.where(lane < MLA_SLOT, qp, zero), jnp.where(lane >= MLA_SLOT, qp, zero)]
    m_ref[...] = jnp.full(m_ref.shape, NEG, F32)
    acc_ref[...] = jnp.zeros(acc_ref.shape, F32)
    hq = tq // 2
    dn = (((1,), (1,)), ((), ()))

    def step(start, width, r0, mask):
        kb = k_ref[0, pl.ds(start, width), :]
        vb = v_ref[0, pl.ds(start, width), :]
        for h in range(MLA_HPS):
            pp = h // 2
            s = lax.dot_general(qh[h][r0:], kb[:, pp * pair_w:(pp + 1) * pair_w], dn,
                                preferred_element_type=F32)
            if mask is not None:
                s = jnp.where(mask, s, NEG)
            m_prev = m_ref[h, r0:, :]
            m_new = jnp.maximum(m_prev, jnp.max(s, axis=-1, keepdims=True))
            alpha = jnp.exp2(m_prev - m_new)
            p = jnp.exp2(s - jnp.concatenate([m_new] * (width // 128), axis=-1))
            pv = jnp.dot(p.astype(BF16), vb[:, h * MLA_SLOT:(h + 1) * MLA_SLOT], preferred_element_type=F32)
            acc_ref[h, r0:, :] = alpha * acc_ref[h, r0:, :] + pv
            m_ref[h, r0:, :] = m_new

    def body(j, c):
        step(pl.multiple_of(j * tq, tq), tq, 0, None)
        return c

    lax.fori_loop(0, i, body, 0)
    d0 = pl.multiple_of(i * tq, tq)
    def causal(rows):
        return lax.broadcasted_iota(jnp.int32, (rows, hq), 1) <= lax.broadcasted_iota(jnp.int32, (rows, hq), 0)

    step(d0, hq, 0, causal(tq))
    step(pl.multiple_of(d0 + hq, hq), hq, hq, causal(hq))
    lane_o = lax.broadcasted_iota(jnp.int32, (tq, 2 * MLA_V), 1)
    for pp in range(MLA_HPS // 2):
        outs = []
        for h in (2 * pp, 2 * pp + 1):
            a = acc_ref[h]
            outs.append(a / pltpu.roll(a, MLA_V, 1))
        o_ref[0, :, pp * 2 * MLA_V:(pp + 1) * 2 * MLA_V] = jnp.where(
            lane_o < MLA_V, outs[0], pltpu.roll(outs[1], MLA_V, 1)).astype(BF16)


def _mla_attn(q, k, v, B, S, tq):
    q = q.reshape(B, S, MLA_HEADS * MLA_SLOT)
    k = k.reshape(B, S, MLA_HEADS * MLA_SLOT)
    v = v.reshape(B, S, MLA_HEADS * MLA_SLOT)
    return pl.pallas_call(
        functools.partial(_mla_attn_kernel, tq=tq),
        grid=(B, MLA_HEADS // MLA_HPS, S // tq),
        in_specs=[
            pl.BlockSpec((1, tq, MLA_HPS * MLA_SLOT), lambda b, h, i: (b, i, h)),
            pl.BlockSpec((1, S, MLA_HPS * MLA_SLOT), lambda b, h, i: (b, 0, h)),
            pl.BlockSpec((1, S, MLA_HPS * MLA_SLOT), lambda b, h, i: (b, 0, h)),
        ],
        out_specs=pl.BlockSpec((1, tq, MLA_HPS * MLA_V), lambda b, h, i: (b, i, h)),
        out_shape=jax.ShapeDtypeStruct((B, S, MLA_HEADS * MLA_V), BF16),
        scratch_shapes=[pltpu.VMEM((MLA_HPS, tq, 128), F32), pltpu.VMEM((MLA_HPS, tq, MLA_SLOT), F32)],
        compiler_params=_cparams(("parallel", "parallel", "arbitrary")),
        name="mla_attn",
    )(q, k, v)


def _dil_kernel(qkv_ref, o_ref, lse_ref, bias_ref, *, nb):
    dil = qkv_ref.shape[1]
    width = 2 * DIL_BLK if nb > 1 else DIL_BLK
    row = lax.broadcasted_iota(jnp.int32, (DIL_BLK, width), 0)
    col = lax.broadcasted_iota(jnp.int32, (DIL_BLK, width), 1)
    bias_ref[0] = jnp.where(col <= row, 0.0, NEG)
    if nb > 1:
        later = jnp.logical_or(jnp.logical_and(col >= DIL_BLK, col - DIL_BLK <= row),
                               jnp.logical_and(col < DIL_BLK, col >= row))
        bias_ref[1] = jnp.where(later, 0.0, NEG)
    head_of_lane = lax.broadcasted_iota(jnp.int32, (DIL_BLK, DIL_W), 1) // DIL_HEAD_DIM
    dn = (((1,), (1,)), ((), ()))

    def by_head(parts):
        out = parts[DIL_HEADS - 1]
        for h in range(DIL_HEADS - 2, -1, -1):
            out = jnp.where(head_of_lane == h, parts[h], out)
        return out

    def unit(u):
        r = u // nb
        n = u % nb
        q0 = pl.multiple_of(n * DIL_BLK, DIL_BLK)
        q = qkv_ref[0, r, pl.ds(q0, DIL_BLK), 0:DIL_W]
        if nb > 1:
            w0 = pl.multiple_of(jnp.maximum(n - 1, 0) * DIL_BLK, DIL_BLK)
            bias = bias_ref[jnp.minimum(n, 1)]
        else:
            w0 = 0
            bias = bias_ref[0]
        kw = qkv_ref[0, r, pl.ds(w0, width), DIL_W:2 * DIL_W]
        vw = qkv_ref[0, r, pl.ds(w0, width), 2 * DIL_W:3 * DIL_W]
        zero = jnp.zeros_like(q)
        qs = jnp.concatenate([jnp.where(head_of_lane == h, q, zero) for h in range(DIL_HEADS)], axis=0)
        s = lax.dot_general(qs, kw, dn, preferred_element_type=F32)
        s = (s.reshape(DIL_HEADS, DIL_BLK, width) + bias[None]).reshape(DIL_HEADS * DIL_BLK, width)
        m = jnp.max(s, axis=-1, keepdims=True)
        e = jnp.exp2(s - m)
        den = jnp.sum(e, axis=-1, keepdims=True)
        pv = jnp.dot(e.astype(BF16), vw, preferred_element_type=F32)
        lse = m + jnp.log2(den)
        blk = lambda t, h: t[h * DIL_BLK:(h + 1) * DIL_BLK]
        o = by_head([blk(pv, h) for h in range(DIL_HEADS)]) / by_head(
            [jnp.broadcast_to(blk(den, h), (DIL_BLK, DIL_W)) for h in range(DIL_HEADS)])
        o_ref[0, r, pl.ds(q0, DIL_BLK), :] = o.astype(BF16)
        lse_ref[0, r, pl.ds(q0, DIL_BLK), :] = by_head(
            [jnp.broadcast_to(blk(lse, h), (DIL_BLK, DIL_W)) for h in range(DIL_HEADS)])

    def body(t, carry):
        unit(2 * t)
        unit(2 * t + 1)
        return carry

    lax.fori_loop(0, dil * nb // 2, body, 0)


def _dil_attn(qkv, B, dil, L):
    nb = L // DIL_BLK
    return pl.pallas_call(
        functools.partial(_dil_kernel, nb=nb),
        grid=(B,),
        in_specs=[pl.BlockSpec((1, dil, L, 3 * DIL_W), lambda b: (b, 0, 0, 0))],
        out_specs=[
            pl.BlockSpec((1, dil, L, DIL_W), lambda b: (b, 0, 0, 0)),
            pl.BlockSpec((1, dil, L, DIL_W), lambda b: (b, 0, 0, 0)),
        ],
        out_shape=[
            jax.ShapeDtypeStruct((B, dil, L, DIL_W), BF16),
            jax.ShapeDtypeStruct((B, dil, L, DIL_W), F32),
        ],
        scratch_shapes=[pltpu.VMEM((2, DIL_BLK, 2 * DIL_BLK if nb > 1 else DIL_BLK), F32)],
        compiler_params=_cparams(("parallel",)),
        name=f"dil_attn_d{dil}",
    )(qkv)


def _post_kernel(x_ref, oa_ref, og0_ref, og1_ref, og2_ref, ls0_ref, ls1_ref, ls2_ref, gates_ref,
                 wa_ref, wb_ref, wo_ref, gffn_ref, wr_ref, br_ref,
                 x1_ref, h2_ref, lg_ref, ob_ref, *, tm):
    ls =[r[0] for r in (ls0_ref, ls1_ref, ls2_ref)]
    og = [r[0] for r in (og0_ref, og1_ref, og2_ref)]
    for gi, (_, dil) in enumerate(DIL_PATTERN):
        rows = tm // dil
        for r in range(dil):
            for c in range(2):
                sl = slice(c * 128, (c + 1) * 128)
                ob_ref[2 * gi + c, pl.ds(r, rows, stride=dil), :] = og[gi][r][:, sl].astype(F32)
                ob_ref[6 + 2 * gi + c, pl.ds(r, rows, stride=dil), :] = ls[gi][r][:, sl]

    def tok_major(k):
        return jnp.concatenate([ob_ref[2 * k], ob_ref[2 * k + 1]], axis=-1)

    l0, l1, l2 = tok_major(3), tok_major(4), tok_major(5)
    mx = jnp.maximum(jnp.maximum(l0, l1), l2)
    w0, w1, w2 = jnp.exp2(l0 - mx), jnp.exp2(l1 - mx), jnp.exp2(l2 - mx)
    ob = (w0 * tok_major(0) + w1 * tok_major(1) + w2 * tok_major(2)) / (w0 + w1 + w2)
    ya = jnp.dot(oa_ref[...], wa_ref[...], preferred_element_type=F32)
    yb = jnp.dot(ob.astype(BF16), wb_ref[...], preferred_element_type=F32)
    merged = (jax.nn.sigmoid(gates_ref[:, 0:D_MODEL].astype(F32)) * ya
              + jax.nn.sigmoid(gates_ref[:, D_MODEL:2 * D_MODEL].astype(F32)) * yb)
    x1 = x_ref[...] + jnp.dot(merged.astype(BF16), wo_ref[...], preferred_element_type=F32)
    x1_ref[...] = x1
    h2 = _rms(x1, gffn_ref[...])
    for c in range(D_MODEL // 128):
        h2_ref[pl.ds(c, tm, stride=D_MODEL // 128), :] = h2[:, c * 128:(c + 1) * 128]
    lg_ref[...] = jnp.dot(h2.astype(BF16), wr_ref[...], preferred_element_type=F32) + br_ref[...]


def _post(x2d, oa, ogs, lss, main, wa, wb, wo, gffn, wr, br, B, S, tm):
    T = x2d.shape[0]
    nt = S // tm
    const = lambda i: (0, 0)
    res_specs = [pl.BlockSpec((1, d, tm // d, DIL_W), lambda i, nt=nt: (i // nt, 0, i % nt, 0))
                 for _, d in DIL_PATTERN]
    return pl.pallas_call(
        functools.partial(_post_kernel, tm=tm),
        grid=(T // tm,),
        in_specs=[pl.BlockSpec((tm, D_MODEL), lambda i: (i, 0)),
                  pl.BlockSpec((tm, D_MODEL), lambda i: (i, 0))]
                 + res_specs + res_specs
                 + [pl.BlockSpec((tm, 2 * D_MODEL), lambda i: (i, 0)),
                    pl.BlockSpec((D_MODEL, D_MODEL), const),
                    pl.BlockSpec((DIL_W, D_MODEL), const),
                    pl.BlockSpec((D_MODEL, D_MODEL), const),
                    pl.BlockSpec((1, D_MODEL), const),
                    pl.BlockSpec((D_MODEL, 128), const),
                    pl.BlockSpec((1, 128), const)],
        out_specs=[pl.BlockSpec((tm, D_MODEL), lambda i: (i, 0)),
                   pl.BlockSpec((tm * ROW_TILE, 128), lambda i: (i, 0)),
                   pl.BlockSpec((tm, 128), lambda i: (i, 0))],
        out_shape=[jax.ShapeDtypeStruct((T, D_MODEL), F32),
                   jax.ShapeDtypeStruct((T * ROW_TILE, 128), F32),
                   jax.ShapeDtypeStruct((T, 128), F32)],
        scratch_shapes=[pltpu.VMEM((12, tm, 128), F32)],
        compiler_params=_cparams(("parallel",)),
        name="post_attn",
    )(x2d, oa, *ogs, *lss, main, wa, wb, wo, gffn, wr, br)


def _route_kernel(lg_ref, ri_ref, rw_ref, cnt_ref, carry_ref, *, tm):
    i = pl.program_id(0)

    @pl.when(i == 0)
    def _():
        carry_ref[...] = jnp.zeros(carry_ref.shape, F32)

    lg = lg_ref[...]
    lane = lax.broadcasted_iota(jnp.int32, lg.shape, 1)
    lane_f = lane.astype(F32)
    ninf = jnp.float32(-jnp.inf)

    def first_max(vals):
        vmax = jnp.max(vals, axis=-1, keepdims=True)
        idx = jnp.min(jnp.where(vals == vmax, lane_f, 128.0), axis=-1, keepdims=True)
        return vmax, idx.astype(jnp.int32)

    gl = jnp.where(lane < N_GROUPS, lg, ninf)
    gmax, g_sel = first_max(gl)
    p_g = 1.0 / jnp.sum(jnp.exp(gl - gmax), axis=-1, keepdims=True)
    lo = N_GROUPS + g_sel * EXPERTS_PER_GROUP
    el = jnp.where(jnp.logical_and(lane >= lo, lane < lo + EXPERTS_PER_GROUP), lg, ninf)
    v0, i0 = first_max(el)
    v1, i1 = first_max(jnp.where(lane == i0, ninf, el))
    t = jnp.exp(v1 - v0)
    w0 = p_g / (1.0 + t)
    w1 = p_g * t / (1.0 + t)
    e0 = i0 - N_GROUPS
    e1 = i1 - N_GROUPS
    hit0 = lane == e0
    hit1 = lane == e1
    oh = jnp.where(jnp.logical_or(hit0, hit1), 1.0, 0.0).astype(F32)
    r_i = lax.broadcasted_iota(jnp.int32, (tm, tm), 0)
    c_i = lax.broadcasted_iota(jnp.int32, (tm, tm), 1)
    lower = jnp.where(c_i < r_i, 1.0, 0.0).astype(BF16)
    excl = jnp.dot(lower, oh.astype(BF16), preferred_element_type=F32) + carry_ref[...]
    r0 = jnp.sum(jnp.where(hit0, excl, 0.0), axis=-1, keepdims=True).astype(jnp.int32)
    r1 = jnp.sum(jnp.where(hit1, excl, 0.0), axis=-1, keepdims=True).astype(jnp.int32)
    carry_ref[...] = carry_ref[...] + jnp.sum(oh, axis=0, keepdims=True)
    zi = jnp.zeros(lg.shape, jnp.int32)
    ri_ref[...] = jnp.where(lane == 0, e0, jnp.where(lane == 1, e1, jnp.where(lane == 2, r0, jnp.where(lane == 3, r1, zi))))
    rw_ref[...] = jnp.where(lane == 0, w0, jnp.where(lane == 1, w1, jnp.zeros(lg.shape, F32)))
    cnt_ref[...] = carry_ref[...]


def _route(lg, tm):
    T = lg.shape[0]
    return pl.pallas_call(
        functools.partial(_route_kernel, tm=tm),
        grid=(T // tm,),
        in_specs=[pl.BlockSpec((tm, 128), lambda i: (i, 0))],
        out_specs=[pl.BlockSpec((tm, 128), lambda i: (i, 0)),
                   pl.BlockSpec((tm, 128), lambda i: (i, 0)),
                   pl.BlockSpec((1, 128), lambda i: (0, 0))],
        out_shape=[jax.ShapeDtypeStruct((T, 128), jnp.int32),
                   jax.ShapeDtypeStruct((T, 128), F32),
                   jax.ShapeDtypeStruct((1, 128), F32)],
        scratch_shapes=[pltpu.VMEM((1, 128), F32)],
        compiler_params=_cparams(("arbitrary",)),
        name="route",
    )(lg)


def _sc_gather_rows(table, idx):
    n = idx.shape[0]
    n_workers = SC_CORES * SC_SUBCORES
    per_w = n // n_workers
    assert per_w * n_workers == n and per_w % SC_CHUNK == 0
    mesh = plsc.VectorSubcoreMesh(core_axis_name="c", subcore_axis_name="s",
                                  num_cores=SC_CORES, num_subcores=SC_SUBCORES)

    @functools.partial(
        pl.kernel, mesh=mesh,
        out_type=jax.ShapeDtypeStruct((n,) + table.shape[1:], table.dtype),
        scratch_types=[pltpu.VMEM((per_w,), jnp.int32),
                       pltpu.VMEM((SC_CHUNK,) + table.shape[1:], table.dtype),
                       pltpu.SemaphoreType.DMA],
        compiler_params=pltpu.CompilerParams(use_tc_tiling_on_sc=True),
        name="sc_gather_rows",
    )
    def gather(table_hbm, idx_hbm, out_hbm, idx_v, rows_v, sem):
        wid = lax.axis_index("s") * SC_CORES + lax.axis_index("c")
        base = wid * per_w
        pltpu.sync_copy(idx_hbm.at[pl.ds(base, per_w)], idx_v)

        @pl.loop(0, per_w // SC_CHUNK)
        def _(j):
            off = j * SC_CHUNK
            pltpu.async_copy(table_hbm.at[idx_v.at[pl.ds(off, SC_CHUNK)]], rows_v, sem).wait()
            pltpu.sync_copy(rows_v, out_hbm.at[pl.ds(base + off, SC_CHUNK)])

    return gather(table, idx)


def _rows_from_tiles(ref, first, n_rows, stride=ROW_TILE):
    return jnp.concatenate([ref[pl.ds(first + c, n_rows, stride=stride), :] for c in range(ROW_TILE)], axis=-1)


def _rows_to_tiles(ref, val):
    for c in range(ROW_TILE):
        ref[pl.ds(c, val.shape[0], stride=ROW_TILE), :] = val[:, c * 128:(c + 1) * 128]


def _ffn_kernel(be_ref, nvb_ref, x_ref, w1_ref, w3_ref, w2_ref, y_ref):
    b = pl.program_id(0)
    nvb = nvb_ref[0]

    @pl.when(b < nvb)
    def _():
        xb = _rows_from_tiles(x_ref, 0, FFN_BM).astype(BF16)
        h1 = jnp.dot(xb, w1_ref[0, 0].astype(BF16), preferred_element_type=F32)
        h3 = jnp.dot(xb, w3_ref[0, 0].astype(BF16), preferred_element_type=F32)
        a = (jax.nn.silu(h1) * h3).astype(BF16)
        _rows_to_tiles(y_ref, jnp.dot(a, w2_ref[0, 0].astype(BF16), preferred_element_type=F32))

    @pl.when(b >= nvb)
    def _():
        y_ref[...] = jnp.zeros(y_ref.shape, F32)


def _expert_ffn(block_e, nvb, xs, w1, w3, w2, layer):
    nb = block_e.shape[0]

    def used(b, nv):
        return jnp.minimum(b, jnp.maximum(nv[0] - 1, 0))

    return pl.pallas_call(
        _ffn_kernel,
        grid_spec=pltpu.PrefetchScalarGridSpec(
            num_scalar_prefetch=2,
            grid=(nb,),
            in_specs=[
                pl.BlockSpec((FFN_BM * ROW_TILE, 128), lambda b, be, nv: (used(b, nv), 0)),
                pl.BlockSpec((1, 1, D_MODEL, EXPERT_FF), lambda b, be, nv: (layer, be[b], 0, 0)),
                pl.BlockSpec((1, 1, D_MODEL, EXPERT_FF), lambda b, be, nv: (layer, be[b], 0, 0)),
                pl.BlockSpec((1, 1, EXPERT_FF, D_MODEL), lambda b, be, nv: (layer, be[b], 0, 0)),
            ],
            out_specs=pl.BlockSpec((FFN_BM * ROW_TILE, 128), lambda b, be, nv: (b, 0)),
        ),
        out_shape=jax.ShapeDtypeStruct((nb * FFN_BM * ROW_TILE, 128), F32),
        compiler_params=_cparams(("arbitrary",)),
        name="expert_ffn",
    )(block_e, nvb, xs, w1, w3, w2)


def _comb_kernel(yg_ref, x1_ref, rw_ref, p_ref, gple_ref, wpg_ref, wpp_ref, o_ref, *, tm):
    w = rw_ref[...]
    y0 = _rows_from_tiles(yg_ref, 0, tm, stride=TOP_K * ROW_TILE)
    y1 = _rows_from_tiles(yg_ref, ROW_TILE, tm, stride=TOP_K * ROW_TILE)
    x2 = x1_ref[...] + (y0 * w[:, 0:1] + y1 * w[:, 1:2])
    e = jnp.dot(p_ref[...].astype(BF16), wpp_ref[...], preferred_element_type=F32)
    gate = jax.nn.sigmoid(jnp.dot(_rms(x2, gple_ref[...]).astype(BF16), wpg_ref[...], preferred_element_type=F32))
    o_ref[...] = x2 + gate * e


def _combine_ple(yg, x1, rw, p2d, gple, wpg, wpp, tm, layer):
    T = x1.shape[0]
    nt = T // tm
    const = lambda i: (0, 0)
    return pl.pallas_call(
        functools.partial(_comb_kernel, tm=tm),
        grid=(nt,),
        in_specs=[
            pl.BlockSpec((tm * TOP_K * ROW_TILE, 128), lambda i: (i, 0)),
            pl.BlockSpec((tm, D_MODEL), lambda i: (i, 0)),
            pl.BlockSpec((tm, 128), lambda i: (i, 0)),
            pl.BlockSpec((tm, PLE_DIM), lambda i: (layer * nt + i, 0)),
            pl.BlockSpec((1, D_MODEL), const),
            pl.BlockSpec((D_MODEL, D_MODEL), const),
            pl.BlockSpec((PLE_DIM, D_MODEL), const),
        ],
        out_specs=pl.BlockSpec((tm, D_MODEL), lambda i: (i, 0)),
        out_shape=jax.ShapeDtypeStruct((T, D_MODEL), F32),
        compiler_params=_cparams(("parallel",)),
        name="combine_ple",
    )(yg, x1, rw, p2d, gple, wpg, wpp)


def _final_kernel(x_ref, g_ref, o_ref):
    o_ref[...] = _rms(x_ref[...], g_ref[...])


def _final_norm(x2d, g, tm):
    T = x2d.shape[0]
    return pl.pallas_call(
        _final_kernel,
        grid=(T // tm,),
        in_specs=[pl.BlockSpec((tm, D_MODEL), lambda i: (i, 0)), pl.BlockSpec((1, D_MODEL), lambda i: (0, 0))],
        out_specs=pl.BlockSpec((tm, D_MODEL), lambda i: (i, 0)),
        out_shape=jax.ShapeDtypeStruct((T, D_MODEL), F32),
        compiler_params=_cparams(("parallel",)),
        name="final_norm",
    )(x2d, g)


def _rope_tables(pos, theta, rot_dim, period, offset, scale):
    half = rot_dim // 2
    inv = jnp.float32(theta) ** (-jnp.arange(half, dtype=F32) * 2.0 / rot_dim)
    ang = pos.astype(F32)[:, None] * inv
    cos, sin = jnp.cos(ang), jnp.sin(ang)
    T = pos.shape[0]
    c = jnp.ones((T, period), F32).at[:, offset:offset + half].set(cos).at[:, offset + half:offset + rot_dim].set(cos)
    s_up = jnp.zeros((T, period), F32).at[:, offset + half:offset + rot_dim].set(sin)
    s_dn = jnp.zeros((T, period), F32).at[:, offset:offset + half].set(-sin)
    rep = 128 // period
    return jnp.concatenate([jnp.tile(t, (1, rep)) for t in (c, s_up, s_dn)], axis=1) * jnp.float32(scale)


def _prep_w_in(w):
    c_q = w[:, 0:512]
    c_kv = w[:, 512:768]
    k_pe = w[:, 768:800]
    dil = w[:, 800:800 + 2304]
    gates = w[:, 3104:5152]
    kslot = jnp.zeros((D_MODEL, 256), w.dtype).at[:, MLA_NOPE:MLA_NOPE + MLA_ROPE].set(k_pe)
    return jnp.concatenate([gates, c_q, c_kv, kslot, dil], axis=1).astype(BF16)


def _prep_w_q(w):
    w = w.reshape(MLA_Q_LORA, MLA_HEADS, MLA_NOPE + MLA_ROPE)
    w = jnp.pad(w, ((0, 0), (0, 0), (0, MLA_SLOT - MLA_NOPE - MLA_ROPE)))
    return w.reshape(MLA_Q_LORA, MLA_HEADS * MLA_SLOT).astype(BF16)


def _prep_w_kv(w):
    w = w.reshape(MLA_KV_LORA, MLA_HEADS, MLA_NOPE + MLA_V)
    wk = jnp.pad(w[:, :, :MLA_NOPE], ((0, 0), (0, 0), (0, MLA_SLOT - MLA_NOPE)))
    wv = jnp.pad(w[:, :, MLA_NOPE:], ((0, 0), (0, 0), (0, MLA_SLOT - MLA_V)))
    return (wk.reshape(MLA_KV_LORA, MLA_HEADS * MLA_SLOT).astype(BF16),
            wv.reshape(MLA_KV_LORA, MLA_HEADS * MLA_SLOT).astype(BF16))


def _dest_kernel(ri_ref, ps_ref, o_ref):
    ri = ri_ref[...].astype(F32)
    lane = lax.broadcasted_iota(jnp.int32, ri.shape, 1)
    ps = ps_ref[...]

    def col(k):
        return jnp.sum(jnp.where(lane == k, ri, 0.0), axis=-1, keepdims=True)

    def dest(k):
        start = jnp.sum(jnp.where(lane == col(k).astype(jnp.int32), ps, 0.0), axis=-1, keepdims=True)
        return (start + col(2 + k)).astype(jnp.int32)

    o_ref[...] = jnp.where(lane == 0, dest(0), jnp.where(lane == 1, dest(1), 0))


def _dest_rows(ri, pstarts, tm):
    T = ri.shape[0]
    ps = jnp.zeros((1, 128), F32).at[0, :N_EXPERTS].set(pstarts.astype(F32))
    return pl.pallas_call(
        _dest_kernel,
        grid=(T // tm,),
        in_specs=[pl.BlockSpec((tm, 128), lambda i: (i, 0)), pl.BlockSpec((1, 128), lambda i: (0, 0))],
        out_specs=pl.BlockSpec((tm, 128), lambda i: (i, 0)),
        out_shape=jax.ShapeDtypeStruct((T, 128), jnp.int32),
        compiler_params=_cparams(("parallel",)),
        name="dest_rows",
    )(ri, ps)


def _dispatch_plan(ri, cnt, T, tm):
    counts = cnt[0, :N_EXPERTS].astype(jnp.int32)
    pcounts = (counts + FFN_BM - 1) // FFN_BM * FFN_BM
    pends = jnp.cumsum(pcounts)
    pstarts = pends - pcounts
    dest = _dest_rows(ri, pstarts, tm)[:, 0:TOP_K]
    n_blocks = (T * TOP_K) // FFN_BM + N_EXPERTS
    row_tok = jnp.zeros((n_blocks * FFN_BM,), jnp.int32).at[dest.reshape(-1)].set(
        jnp.arange(T * TOP_K, dtype=jnp.int32) // TOP_K, unique_indices=True)
    block_e = jnp.minimum(jnp.searchsorted(pends, jnp.arange(n_blocks, dtype=jnp.int32) * FFN_BM, side='right'),
                          N_EXPERTS - 1).astype(jnp.int32)
    nvb = (pends[-1] // FFN_BM).astype(jnp.int32).reshape(1)
    return dest, row_tok, block_e, nvb


def kernel(x, p, positions, g_mix, w_in, g_q_lat, w_q_up, g_kv_lat, w_kv_up, w_branch_a, w_branch_b, w_out, g_ffn, w_router_grp, b_router_grp, w_router_exp, b_router_exp, w_exp_gate, w_exp_up, w_exp_down, g_ple, w_ple_gate, w_ple_proj, g_final):
    B, S, D = x.shape
    T = B * S
    TM_IN, TM_QKV, TQ, TM_POST, TM_ROUTE, TM_COMB = 512, 512, 512, 256, 512, 256

    pos = positions.reshape(T)
    dil_scale = DIL_HEAD_DIM ** -0.5 * LOG2E
    mla_scale = (MLA_NOPE + MLA_ROPE) ** -0.5 * LOG2E
    tab_dil = jnp.concatenate([_rope_tables(pos, ROPE_THETA, PARTIAL_ROT, DIL_HEAD_DIM, 0, dil_scale),
                               _rope_tables(pos, ROPE_THETA, PARTIAL_ROT, DIL_HEAD_DIM, 0, 1.0)], axis=1)
    tab_mla = jnp.concatenate([_rope_tables(pos, MLA_ROPE_THETA, MLA_ROPE, MLA_SLOT, MLA_NOPE, mla_scale),
                               _rope_tables(pos, MLA_ROPE_THETA, MLA_ROPE, MLA_SLOT, MLA_NOPE, 1.0)], axis=1)

    xc = x.reshape(T, D)
    for i in range(DEPTH):
        w_in_p = _prep_w_in(w_in[i])
        wq = _prep_w_q(w_q_up[i])
        wk, wv = _prep_w_kv(w_kv_up[i])
        wr = jnp.zeros((D, 128), F32).at[:, :N_GROUPS].set(w_router_grp[i]).at[:, N_GROUPS:N_GROUPS + N_EXPERTS].set(
            w_router_exp[i]).astype(BF16)
        br = jnp.zeros((1, 128), F32).at[0, :N_GROUPS].set(b_router_grp[i]).at[0, N_GROUPS:N_GROUPS + N_EXPERTS].set(
            b_router_exp[i].reshape(N_EXPERTS))

        main, d0, d1, d2 = _in_proj(xc, g_mix[i].reshape(1, D), w_in_p, tab_dil, B, S, TM_IN)
        q, k, v = _mla_qkv(main, g_q_lat[i].reshape(1, -1), g_kv_lat[i].reshape(1, -1), wq, wk, wv, tab_mla, TM_QKV)
        o_a = _mla_attn(q, k, v, B, S, TQ).reshape(T, MLA_HEADS * MLA_V)
        ogs, lss = [], []
        for (_, dil), qkv in zip(DIL_PATTERN, (d0, d1, d2)):
            og, ls = _dil_attn(qkv, B, dil, S // dil)
            ogs.append(og)
            lss.append(ls)
        x1, h2, lg = _post(xc, o_a, ogs, lss, main, w_branch_a[i].astype(BF16), w_branch_b[i].astype(BF16),
                           w_out[i].astype(BF16), g_ffn[i].reshape(1, D), wr, br, B, S, TM_POST)
        ri, rw, cnt = _route(lg, TM_ROUTE)
        dest, row_tok, block_e, nvb = _dispatch_plan(ri, cnt, T, TM_ROUTE)
        xs = _sc_gather_rows(h2.reshape(T, ROW_TILE, 128), row_tok)
        ys = _expert_ffn(block_e, nvb, xs.reshape(-1, 128), w_exp_gate, w_exp_up, w_exp_down, i)
        yg = _sc_gather_rows(ys.reshape(-1, ROW_TILE, 128), dest.reshape(T * TOP_K))
        xc = _combine_ple(yg.reshape(-1, 128), x1, rw, p.reshape(DEPTH * T, PLE_DIM), g_ple[i].reshape(1, D),
                          w_ple_gate[i].astype(BF16), w_ple_proj[i].astype(BF16), TM_COMB, i)
    return _final_norm(xc, g_final.reshape(1, D), 512).reshape(B, S, D)
```

```python
import functools
import math

import jax
import jax.numpy as jnp
from jax import lax
from jax.experimental import pallas as pl
from jax.experimental.pallas import tpu as pltpu
from jax.experimental.pallas import tpu_sc as plsc

F32 = jnp.float32
BF16 = jnp.bfloat16

D_MODEL = 1024
DEPTH = 4
RMS_EPS = 1e-6
NEG = -1e30
LOG2E = math.log2(math.e)

MLA_HEADS = 16
MLA_Q_LORA = 512
MLA_KV_LORA = 256
MLA_NOPE = 64
MLA_ROPE = 32
MLA_V = 64
MLA_ROPE_THETA = 10000.0
MLA_SLOT = 128
MLA_HPS = 2

DIL_PATTERN = ((128, 1), (512, 4), (2048, 16))
DIL_GROUPS = 3
DIL_HEADS = 4
DIL_HEAD_DIM = 64
DIL_W = DIL_HEADS * DIL_HEAD_DIM
DIL_BLK = 128
ROPE_THETA = 500000.0
PARTIAL_ROT = DIL_HEAD_DIM // 4

N_GROUPS = 8
EXPERTS_PER_GROUP = 8
N_EXPERTS = 64
TOP_K = 2
EXPERT_FF = 256
PLE_DIM = 256

IN_TN = 768
MAIN_COLS = 3072
IN_COLS_PAD = MAIN_COLS + DIL_GROUPS * 3 * DIL_W
N_MAIN_TILES = MAIN_COLS // IN_TN

FFN_BM = 256
ROW_TILE = D_MODEL // 128
VMEM_LIMIT = 48 * 1024 * 1024

SC_CORES = 2
SC_SUBCORES = 16
SC_CHUNK = 32


def _cparams(sem, bounds_checks=True):
    return pltpu.CompilerParams(dimension_semantics=sem, vmem_limit_bytes=VMEM_LIMIT,
                                disable_bounds_checks=not bounds_checks)


def _rms(x, g):
    return x * lax.rsqrt(jnp.mean(x * x, axis=-1, keepdims=True) + RMS_EPS) * g


def _rope128(x, c, s_up, s_dn, half):
    return x * c + pltpu.roll(x, half, 1) * s_up + pltpu.roll(x, 128 - half, 1) * s_dn


def _in_kernel(x_ref, g_ref, w_ref, tab_ref, main_ref, d0_ref, d1_ref, d2_ref, xn_ref, acc_ref):
    j = pl.program_id(1)

    @pl.when(j == 0)
    def _():
        xn_ref[...] = _rms(x_ref[...], g_ref[...]).astype(BF16)

    acc = jnp.dot(xn_ref[...], w_ref[...], preferred_element_type=F32)

    @pl.when(j < N_MAIN_TILES)
    def _():
        main_ref[...] = acc.astype(BF16)

    def dil_tile(out_ref, dil):
        for c in range(4):
            t0 = 0 if c < 2 else 3
            x = acc[:, c * 128:(c + 1) * 128]
            acc_ref[c] = _rope128(
                x, tab_ref[:, t0 * 128:(t0 + 1) * 128], tab_ref[:, (t0 + 1) * 128:(t0 + 2) * 128],
                tab_ref[:, (t0 + 2) * 128:(t0 + 3) * 128], PARTIAL_ROT // 2)
        for c in range(4, 6):
            acc_ref[c] = acc[:, c * 128:(c + 1) * 128]
        rows = acc_ref.shape[1] // dil
        for r in range(dil):
            for c in range(6):
                out_ref[0, r, :, c * 128:(c + 1) * 128] = acc_ref[c, pl.ds(r, rows, stride=dil), :].astype(BF16)

    for gi, (_, dil) in enumerate(DIL_PATTERN):
        pl.when(j == N_MAIN_TILES + gi)(functools.partial(dil_tile, (d0_ref, d1_ref, d2_ref)[gi], dil))


def _in_proj(x2d, g, w, tab, B, S, tm):
    T = x2d.shape[0]
    nt = S // tm
    dil_shapes = [jax.ShapeDtypeStruct((B, d, S // d, 3 * DIL_W), BF16) for _, d in DIL_PATTERN]
    dil_specs = [pl.BlockSpec((1, d, tm // d, 3 * DIL_W), lambda i, j, nt=nt: (i // nt, 0, i % nt, 0))
                 for _, d in DIL_PATTERN]
    return pl.pallas_call(
        _in_kernel,
        grid=(T // tm, IN_COLS_PAD // IN_TN),
        in_specs=[
            pl.BlockSpec((tm, D_MODEL), lambda i, j: (i, 0)),
            pl.BlockSpec((1, D_MODEL), lambda i, j: (0, 0)),
            pl.BlockSpec((D_MODEL, IN_TN), lambda i, j: (0, j)),
            pl.BlockSpec((tm, 6 * 128), lambda i, j: (i, 0)),
        ],
        out_specs=[pl.BlockSpec((tm, IN_TN), lambda i, j: (i, jnp.minimum(j, N_MAIN_TILES - 1)))] + dil_specs,
        out_shape=[jax.ShapeDtypeStruct((T, MAIN_COLS), BF16)] + dil_shapes,
        scratch_shapes=[pltpu.VMEM((tm, D_MODEL), BF16), pltpu.VMEM((IN_TN // 128, tm, 128), F32)],
        compiler_params=_cparams(("parallel", "arbitrary")),
        name="in_proj",
    )(x2d, g, w, tab)


def _qkv_kernel(cq_ref, ckv_ref, kpe_ref, gq_ref, gkv_ref, wq_ref, wk_ref, wv_ref, tab_ref,
                q_ref, k_ref, v_ref):
    qn = _rms(cq_ref[...].astype(F32), gq_ref[...]).astype(BF16)
    kvn = _rms(ckv_ref[...].astype(F32), gkv_ref[...]).astype(BF16)
    qacc = jnp.dot(qn, wq_ref[...], preferred_element_type=F32)
    kacc = jnp.dot(kvn, wk_ref[...], preferred_element_type=F32)
    vacc = jnp.dot(kvn, wv_ref[...], preferred_element_type=F32)
    lane_v = lax.broadcasted_iota(jnp.int32, vacc.shape, 1)
    v_ref[...] = jnp.where(lane_v % MLA_SLOT < MLA_V, vacc, 1.0).astype(BF16)
    half = MLA_ROPE // 2
    tq = [tab_ref[:, t * 128:(t + 1) * 128] for t in range(3)]
    tk = [tab_ref[:, t * 128:(t + 1) * 128] for t in range(3, 6)]
    krot = _rope128(kpe_ref[:, 0:128].astype(F32), tk[0], tk[1], tk[2], half)
    for h in range(MLA_HEADS):
        sl = slice(h * MLA_SLOT, (h + 1) * MLA_SLOT)
        q_ref[:, sl] = _rope128(qacc[:, sl], tq[0], tq[1], tq[2], half).astype(BF16)
        k_ref[:, sl] = (kacc[:, sl] + krot).astype(BF16)


def _mla_qkv(main, gq, gkv, wq, wk, wv, tab, tm):
    T = main.shape[0]
    const = lambda i: (0, 0)
    return pl.pallas_call(
        _qkv_kernel,
        grid=(T // tm,),
        in_specs=[
            pl.BlockSpec((tm, MLA_Q_LORA), lambda i: (i, 2048 // MLA_Q_LORA)),
            pl.BlockSpec((tm, MLA_KV_LORA), lambda i: (i, 2560 // MLA_KV_LORA)),
            pl.BlockSpec((tm, 256), lambda i: (i, 2816 // 256)),
            pl.BlockSpec((1, MLA_Q_LORA), const),
            pl.BlockSpec((1, MLA_KV_LORA), const),
            pl.BlockSpec((MLA_Q_LORA, MLA_HEADS * MLA_SLOT), const),
            pl.BlockSpec((MLA_KV_LORA, MLA_HEADS * MLA_SLOT), const),
            pl.BlockSpec((MLA_KV_LORA, MLA_HEADS * MLA_SLOT), const),
            pl.BlockSpec((tm, 6 * 128), lambda i: (i, 0)),
        ],
        out_specs=[
            pl.BlockSpec((tm, MLA_HEADS * MLA_SLOT), lambda i: (i, 0)),
            pl.BlockSpec((tm, MLA_HEADS * MLA_SLOT), lambda i: (i, 0)),
            pl.BlockSpec((tm, MLA_HEADS * MLA_SLOT), lambda i: (i, 0)),
        ],
        out_shape=[
            jax.ShapeDtypeStruct((T, MLA_HEADS * MLA_SLOT), BF16),
            jax.ShapeDtypeStruct((T, MLA_HEADS * MLA_SLOT), BF16),
            jax.ShapeDtypeStruct((T, MLA_HEADS * MLA_SLOT), BF16),
        ],
        compiler_params=_cparams(("parallel",)),
        name="mla_qkv",
    )(main, main, main, gq, gkv, wq, wk, wv, tab)


def _mla_attn_kernel(q_ref, k_ref, v_ref, o_ref, m_ref, acc_ref, *, tq):
    i = pl.program_id(2)
    pair_w = 2 * MLA_SLOT
    lane = lax.broadcasted_iota(jnp.int32, (tq, pair_w), 1)
    qh = []
    for pp in range(MLA_HPS // 2):
        qp = q_ref[0, :, pp * pair_w:(pp + 1) * pair_w]
        zero = jnp.zeros_like(qp)
        qh += [jnp.where(lane < MLA_SLOT, qp, zero), jnp.where(lane >= MLA_SLOT, qp, zero)]
    m_ref[...] = jnp.full(m_ref.shape, NEG, F32)
    acc_ref[...] = jnp.zeros(acc_ref.shape, F32)
    hq = tq // 2
    dn = (((1,), (1,)), ((), ()))

    def step(start, width, r0, mask):
        kb = k_ref[0, pl.ds(start, width), :]
        vb = v_ref[0, pl.ds(start, width), :]
        for h in range(MLA_HPS):
            pp = h // 2
            s = lax.dot_general(qh[h][r0:], kb[:, pp * pair_w:(pp + 1) * pair_w], dn,
                                preferred_element_type=F32)
            if mask is not None:
                s = jnp.where(mask, s, NEG)
            m_prev = m_ref[h, r0:, :]
            m_new = jnp.maximum(m_prev, jnp.max(s, axis=-1, keepdims=True))
            alpha = jnp.exp2(m_prev - m_new)
            p = jnp.exp2(s - jnp.concatenate([m_new] * (width // 128), axis=-1))
            pv = jnp.dot(p.astype(BF16), vb[:, h * MLA_SLOT:(h + 1) * MLA_SLOT], preferred_element_type=F32)
            acc_ref[h, r0:, :] = alpha * acc_ref[h, r0:, :] + pv
            m_ref[h, r0:, :] = m_new

    def causal(rows):
        return lax.broadcasted_iota(jnp.int32, (rows, hq), 1) <= lax.broadcasted_iota(jnp.int32, (rows, hq), 0)

    def q_block(n_full):
        for j in range(n_full):
            step(j * tq, tq, 0, None)
        step(n_full * tq, hq, 0, causal(tq))
        step(n_full * tq + hq, hq, hq, causal(hq))

    for n_full in range(k_ref.shape[1] // tq):
        pl.when(i == n_full)(functools.partial(q_block, n_full))
    lane_o = lax.broadcasted_iota(jnp.int32, (tq, 2 * MLA_V), 1)
    for pp in range(MLA_HPS // 2):
        outs = []
        for h in (2 * pp, 2 * pp + 1):
            a = acc_ref[h]
            outs.append(a / pltpu.roll(a, MLA_V, 1))
        o_ref[0, :, pp * 2 * MLA_V:(pp + 1) * 2 * MLA_V] = jnp.where(
            lane_o < MLA_V, outs[0], pltpu.roll(outs[1], MLA_V, 1)).astype(BF16)


def _mla_attn(q, k, v, B, S, tq):
    q = q.reshape(B, S, MLA_HEADS * MLA_SLOT)
    k = k.reshape(B, S, MLA_HEADS * MLA_SLOT)
    v = v.reshape(B, S, MLA_HEADS * MLA_SLOT)
    return pl.pallas_call(
        functools.partial(_mla_attn_kernel, tq=tq),
        grid=(B, MLA_HEADS // MLA_HPS, S // tq),
        in_specs=[
            pl.BlockSpec((1, tq, MLA_HPS * MLA_SLOT), lambda b, h, i: (b, i, h)),
            pl.BlockSpec((1, S, MLA_HPS * MLA_SLOT), lambda b, h, i: (b, 0, h)),
            pl.BlockSpec((1, S, MLA_HPS * MLA_SLOT), lambda b, h, i: (b, 0, h)),
        ],
        out_specs=pl.BlockSpec((1, tq, MLA_HPS * MLA_V), lambda b, h, i: (b, i, h)),
        out_shape=jax.ShapeDtypeStruct((B, S, MLA_HEADS * MLA_V), BF16),
        scratch_shapes=[pltpu.VMEM((MLA_HPS, tq, 128), F32), pltpu.VMEM((MLA_HPS, tq, MLA_SLOT), F32)],
        compiler_params=_cparams(("parallel", "parallel", "arbitrary")),
        name="mla_attn",
    )(q, k, v)


def _dil_kernel(qkv_ref, o_ref, lse_ref, bias_ref, *, nb):
    dil = qkv_ref.shape[1]
    width = 2 * DIL_BLK if nb > 1 else DIL_BLK
    row = lax.broadcasted_iota(jnp.int32, (DIL_BLK, width), 0)
    col = lax.broadcasted_iota(jnp.int32, (DIL_BLK, width), 1)
    bias_ref[0] = jnp.where(col <= row, 0.0, NEG)
    if nb > 1:
        later = jnp.logical_or(jnp.logical_and(col >= DIL_BLK, col - DIL_BLK <= row),
                               jnp.logical_and(col < DIL_BLK, col >= row))
        bias_ref[1] = jnp.where(later, 0.0, NEG)
    head_of_lane = lax.broadcasted_iota(jnp.int32, (DIL_BLK, DIL_W), 1) // DIL_HEAD_DIM
    dn = (((1,), (1,)), ((), ()))

    def by_head(parts):
        out = parts[DIL_HEADS - 1]
        for h in range(DIL_HEADS - 2, -1, -1):
            out = jnp.where(head_of_lane == h, parts[h], out)
        return out

    def unit(u):
        r = u // nb
        n = u % nb
        q0 = pl.multiple_of(n * DIL_BLK, DIL_BLK)
        q = qkv_ref[0, r, pl.ds(q0, DIL_BLK), 0:DIL_W]
        if nb > 1:
            w0 = pl.multiple_of(jnp.maximum(n - 1, 0) * DIL_BLK, DIL_BLK)
            bias = bias_ref[jnp.minimum(n, 1)]
        else:
            w0 = 0
            bias = bias_ref[0]
        kw = qkv_ref[0, r, pl.ds(w0, width), DIL_W:2 * DIL_W]
        vw = qkv_ref[0, r, pl.ds(w0, width), 2 * DIL_W:3 * DIL_W]
        zero = jnp.zeros_like(q)
        qs = jnp.concatenate([jnp.where(head_of_lane == h, q, zero) for h in range(DIL_HEADS)], axis=0)
        s = lax.dot_general(qs, kw, dn, preferred_element_type=F32)
        s = (s.reshape(DIL_HEADS, DIL_BLK, width) + bias[None]).reshape(DIL_HEADS * DIL_BLK, width)
        m = jnp.max(s, axis=-1, keepdims=True)
        e = jnp.exp2(s - m)
        den = jnp.sum(e, axis=-1, keepdims=True)
        pv = jnp.dot(e.astype(BF16), vw, preferred_element_type=F32)
        lse = m + jnp.log2(den)
        blk = lambda t, h: t[h * DIL_BLK:(h + 1) * DIL_BLK]
        o = by_head([blk(pv, h) for h in range(DIL_HEADS)]) / by_head(
            [jnp.broadcast_to(blk(den, h), (DIL_BLK, DIL_W)) for h in range(DIL_HEADS)])
        o_ref[0, r, pl.ds(q0, DIL_BLK), :] = o.astype(BF16)
        lse_ref[0, r, pl.ds(q0, DIL_BLK), :] = by_head(
            [jnp.broadcast_to(blk(lse, h), (DIL_BLK, DIL_W)) for h in range(DIL_HEADS)])

    def body(t, carry):
        unit(2 * t)
        unit(2 * t + 1)
        return carry

    lax.fori_loop(0, dil * nb // 2, body, 0)


def _dil_attn(qkv, B, dil, L):
    nb = L // DIL_BLK
    return pl.pallas_call(
        functools.partial(_dil_kernel, nb=nb),
        grid=(B,),
        in_specs=[pl.BlockSpec((1, dil, L, 3 * DIL_W), lambda b: (b, 0, 0, 0))],
        out_specs=[
            pl.BlockSpec((1, dil, L, DIL_W), lambda b: (b, 0, 0, 0)),
            pl.BlockSpec((1, dil, L, DIL_W), lambda b: (b, 0, 0, 0)),
        ],
        out_shape=[
            jax.ShapeDtypeStruct((B, dil, L, DIL_W), BF16),
            jax.ShapeDtypeStruct((B, dil, L, DIL_W), F32),
        ],
        scratch_shapes=[pltpu.VMEM((2, DIL_BLK, 2 * DIL_BLK if nb > 1 else DIL_BLK), F32)],
        compiler_params=_cparams(("parallel",)),
        name=f"dil_attn_d{dil}",
    )(qkv)


def _post_kernel(x_ref, oa_ref, og0_ref, og1_ref, og2_ref, ls0_ref, ls1_ref, ls2_ref, gates_ref,
                 wa_ref, wb_ref, wo_ref, gffn_ref, wr_ref, br_ref,
                 x1_ref, h2_ref, lg_ref, ob_ref, *, tm):
    ls =[r[0] for r in (ls0_ref, ls1_ref, ls2_ref)]
    og = [r[0] for r in (og0_ref, og1_ref, og2_ref)]
    for gi, (_, dil) in enumerate(DIL_PATTERN):
        rows = tm // dil
        for r in range(dil):
            for c in range(2):
                sl = slice(c * 128, (c + 1) * 128)
                ob_ref[2 * gi + c, pl.ds(r, rows, stride=dil), :] = og[gi][r][:, sl].astype(F32)
                ob_ref[6 + 2 * gi + c, pl.ds(r, rows, stride=dil), :] = ls[gi][r][:, sl]

    def tok_major(k):
        return jnp.concatenate([ob_ref[2 * k], ob_ref[2 * k + 1]], axis=-1)

    l0, l1, l2 = tok_major(3), tok_major(4), tok_major(5)
    mx = jnp.maximum(jnp.maximum(l0, l1), l2)
    w0, w1, w2 = jnp.exp2(l0 - mx), jnp.exp2(l1 - mx), jnp.exp2(l2 - mx)
    ob = (w0 * tok_major(0) + w1 * tok_major(1) + w2 * tok_major(2)) / (w0 + w1 + w2)
    ya = jnp.dot(oa_ref[...], wa_ref[...], preferred_element_type=F32)
    yb = jnp.dot(ob.astype(BF16), wb_ref[...], preferred_element_type=F32)
    merged = (jax.nn.sigmoid(gates_ref[:, 0:D_MODEL].astype(F32)) * ya
              + jax.nn.sigmoid(gates_ref[:, D_MODEL:2 * D_MODEL].astype(F32)) * yb)
    x1 = x_ref[...] + jnp.dot(merged.astype(BF16), wo_ref[...], preferred_element_type=F32)
    x1_ref[...] = x1
    h2 = _rms(x1, gffn_ref[...])
    for c in range(D_MODEL // 128):
        h2_ref[pl.ds(c, tm, stride=D_MODEL // 128), :] = h2[:, c * 128:(c + 1) * 128]
    lg_ref[...] = jnp.dot(h2.astype(BF16), wr_ref[...], preferred_element_type=F32) + br_ref[...]


def _post(x2d, oa, ogs, lss, main, wa, wb, wo, gffn, wr, br, B, S, tm):
    T = x2d.shape[0]
    nt = S // tm
    const = lambda i: (0, 0)
    res_specs = [pl.BlockSpec((1, d, tm // d, DIL_W), lambda i, nt=nt: (i // nt, 0, i % nt, 0))
                 for _, d in DIL_PATTERN]
    return pl.pallas_call(
        functools.partial(_post_kernel, tm=tm),
        grid=(T // tm,),
        in_specs=[pl.BlockSpec((tm, D_MODEL), lambda i: (i, 0)),
                  pl.BlockSpec((tm, D_MODEL), lambda i: (i, 0))]
                 + res_specs + res_specs
                 + [pl.BlockSpec((tm, 2 * D_MODEL), lambda i: (i, 0)),
                    pl.BlockSpec((D_MODEL, D_MODEL), const),
                    pl.BlockSpec((DIL_W, D_MODEL), const),
                    pl.BlockSpec((D_MODEL, D_MODEL), const),
                    pl.BlockSpec((1, D_MODEL), const),
                    pl.BlockSpec((D_MODEL, 128), const),
                    pl.BlockSpec((1, 128), const)],
        out_specs=[pl.BlockSpec((tm, D_MODEL), lambda i: (i, 0)),
                   pl.BlockSpec((tm * ROW_TILE, 128), lambda i: (i, 0)),
                   pl.BlockSpec((tm, 128), lambda i: (i, 0))],
        out_shape=[jax.ShapeDtypeStruct((T, D_MODEL), F32),
                   jax.ShapeDtypeStruct((T * ROW_TILE, 128), F32),
                   jax.ShapeDtypeStruct((T, 128), F32)],
        scratch_shapes=[pltpu.VMEM((12, tm, 128), F32)],
        compiler_params=_cparams(("parallel",)),
        name="post_attn",
    )(x2d, oa, *ogs, *lss, main, wa, wb, wo, gffn, wr, br)


def _route_kernel(lg_ref, ri_ref, rw_ref, cnt_ref, carry_ref, *, tm):
    i = pl.program_id(0)

    @pl.when(i == 0)
    def _():
        carry_ref[...] = jnp.zeros(carry_ref.shape, F32)

    lg = lg_ref[...]
    lane = lax.broadcasted_iota(jnp.int32, lg.shape, 1)
    lane_f = lane.astype(F32)
    ninf = jnp.float32(-jnp.inf)

    def first_max(vals):
        vmax = jnp.max(vals, axis=-1, keepdims=True)
        idx = jnp.min(jnp.where(vals == vmax, lane_f, 128.0), axis=-1, keepdims=True)
        return vmax, idx.astype(jnp.int32)

    gl = jnp.where(lane < N_GROUPS, lg, ninf)
    gmax, g_sel = first_max(gl)
    p_g = 1.0 / jnp.sum(jnp.exp(gl - gmax), axis=-1, keepdims=True)
    lo = N_GROUPS + g_sel * EXPERTS_PER_GROUP
    el = jnp.where(jnp.logical_and(lane >= lo, lane < lo + EXPERTS_PER_GROUP), lg, ninf)
    v0, i0 = first_max(el)
    v1, i1 = first_max(jnp.where(lane == i0, ninf, el))
    t = jnp.exp(v1 - v0)
    w0 = p_g / (1.0 + t)
    w1 = p_g * t / (1.0 + t)
    e0 = i0 - N_GROUPS
    e1 = i1 - N_GROUPS
    hit0 = lane == e0
    hit1 = lane == e1
    oh = jnp.where(jnp.logical_or(hit0, hit1), 1.0, 0.0).astype(F32)
    r_i = lax.broadcasted_iota(jnp.int32, (tm, tm), 0)
    c_i = lax.broadcasted_iota(jnp.int32, (tm, tm), 1)
    lower = jnp.where(c_i < r_i, 1.0, 0.0).astype(BF16)
    excl = jnp.dot(lower, oh.astype(BF16), preferred_element_type=F32) + carry_ref[...]
    r0 = jnp.sum(jnp.where(hit0, excl, 0.0), axis=-1, keepdims=True).astype(jnp.int32)
    r1 = jnp.sum(jnp.where(hit1, excl, 0.0), axis=-1, keepdims=True).astype(jnp.int32)
    carry_ref[...] = carry_ref[...] + jnp.sum(oh, axis=0, keepdims=True)
    zi = jnp.zeros(lg.shape, jnp.int32)
    ri_ref[...] = jnp.where(lane == 0, e0, jnp.where(lane == 1, e1, jnp.where(lane == 2, r0, jnp.where(lane == 3, r1, zi))))
    rw_ref[...] = jnp.where(lane == 0, w0, jnp.where(lane == 1, w1, jnp.zeros(lg.shape, F32)))
    cnt_ref[...] = carry_ref[...]


def _route(lg, tm):
    T = lg.shape[0]
    return pl.pallas_call(
        functools.partial(_route_kernel, tm=tm),
        grid=(T // tm,),
        in_specs=[pl.BlockSpec((tm, 128), lambda i: (i, 0))],
        out_specs=[pl.BlockSpec((tm, 128), lambda i: (i, 0)),
                   pl.BlockSpec((tm, 128), lambda i: (i, 0)),
                   pl.BlockSpec((1, 128), lambda i: (0, 0))],
        out_shape=[jax.ShapeDtypeStruct((T, 128), jnp.int32),
                   jax.ShapeDtypeStruct((T, 128), F32),
                   jax.ShapeDtypeStruct((1, 128), F32)],
        scratch_shapes=[pltpu.VMEM((1, 128), F32)],
        compiler_params=_cparams(("arbitrary",)),
        name="route",
    )(lg)


def _sc_gather_rows(table, idx):
    n = idx.shape[0]
    n_workers = SC_CORES * SC_SUBCORES
    per_w = n // n_workers
    assert per_w * n_workers == n and per_w % SC_CHUNK == 0
    mesh = plsc.VectorSubcoreMesh(core_axis_name="c", subcore_axis_name="s",
                                  num_cores=SC_CORES, num_subcores=SC_SUBCORES)

    @functools.partial(
        pl.kernel, mesh=mesh,
        out_type=jax.ShapeDtypeStruct((n,) + table.shape[1:], table.dtype),
        scratch_types=[pltpu.VMEM((per_w,), jnp.int32),
                       pltpu.VMEM((SC_CHUNK,) + table.shape[1:], table.dtype),
                       pltpu.SemaphoreType.DMA],
        compiler_params=pltpu.CompilerParams(use_tc_tiling_on_sc=True),
        name="sc_gather_rows",
    )
    def gather(table_hbm, idx_hbm, out_hbm, idx_v, rows_v, sem):
        wid = lax.axis_index("s") * SC_CORES + lax.axis_index("c")
        base = wid * per_w
        pltpu.sync_copy(idx_hbm.at[pl.ds(base, per_w)], idx_v)

        @pl.loop(0, per_w // SC_CHUNK)
        def _(j):
            off = j * SC_CHUNK
            pltpu.async_copy(table_hbm.at[idx_v.at[pl.ds(off, SC_CHUNK)]], rows_v, sem).wait()
            pltpu.sync_copy(rows_v, out_hbm.at[pl.ds(base + off, SC_CHUNK)])

    return gather(table, idx)


def _rows_from_tiles(ref, first, n_rows, stride=ROW_TILE):
    return jnp.concatenate([ref[pl.ds(first + c, n_rows, stride=stride), :] for c in range(ROW_TILE)], axis=-1)


def _rows_to_tiles(ref, val):
    for c in range(ROW_TILE):
        ref[pl.ds(c, val.shape[0], stride=ROW_TILE), :] = val[:, c * 128:(c + 1) * 128]


def _ffn_kernel(be_ref, nvb_ref, x_ref, w1_ref, w3_ref, w2_ref, y_ref):
    b = pl.program_id(0)
    nvb = nvb_ref[0]

    @pl.when(b < nvb)
    def _():
        xb = _rows_from_tiles(x_ref, 0, FFN_BM).astype(BF16)
        h1 = jnp.dot(xb, w1_ref[0, 0].astype(BF16), preferred_element_type=F32)
        h3 = jnp.dot(xb, w3_ref[0, 0].astype(BF16), preferred_element_type=F32)
        a = (jax.nn.silu(h1) * h3).astype(BF16)
        _rows_to_tiles(y_ref, jnp.dot(a, w2_ref[0, 0].astype(BF16), preferred_element_type=F32))

    @pl.when(b >= nvb)
    def _():
        y_ref[...] = jnp.zeros(y_ref.shape, F32)


def _expert_ffn(block_e, nvb, xs, w1, w3, w2, layer):
    nb = block_e.shape[0]

    def used(b, nv):
        return jnp.minimum(b, jnp.maximum(nv[0] - 1, 0))

    return pl.pallas_call(
        _ffn_kernel,
        grid_spec=pltpu.PrefetchScalarGridSpec(
            num_scalar_prefetch=2,
            grid=(nb,),
            in_specs=[
                pl.BlockSpec((FFN_BM * ROW_TILE, 128), lambda b, be, nv: (used(b, nv), 0)),
                pl.BlockSpec((1, 1, D_MODEL, EXPERT_FF), lambda b, be, nv: (layer, be[b], 0, 0)),
                pl.BlockSpec((1, 1, D_MODEL, EXPERT_FF), lambda b, be, nv: (layer, be[b], 0, 0)),
                pl.BlockSpec((1, 1, EXPERT_FF, D_MODEL), lambda b, be, nv: (layer, be[b], 0, 0)),
            ],
            out_specs=pl.BlockSpec((FFN_BM * ROW_TILE, 128), lambda b, be, nv: (b, 0)),
        ),
        out_shape=jax.ShapeDtypeStruct((nb * FFN_BM * ROW_TILE, 128), F32),
        compiler_params=_cparams(("arbitrary",)),
        name="expert_ffn",
    )(block_e, nvb, xs, w1, w3, w2)


def _comb_kernel(yg_ref, x1_ref, rw_ref, p_ref, gple_ref, wpg_ref, wpp_ref, o_ref, *, tm):
    w = rw_ref[...]
    y0 = _rows_from_tiles(yg_ref, 0, tm, stride=TOP_K * ROW_TILE)
    y1 = _rows_from_tiles(yg_ref, ROW_TILE, tm, stride=TOP_K * ROW_TILE)
    x2 = x1_ref[...] + (y0 * w[:, 0:1] + y1 * w[:, 1:2])
    e = jnp.dot(p_ref[...].astype(BF16), wpp_ref[...], preferred_element_type=F32)
    gate = jax.nn.sigmoid(jnp.dot(_rms(x2, gple_ref[...]).astype(BF16), wpg_ref[...], preferred_element_type=F32))
    o_ref[...] = x2 + gate * e


def _combine_ple(yg, x1, rw, p2d, gple, wpg, wpp, tm, layer):
    T = x1.shape[0]
    nt = T // tm
    const = lambda i: (0, 0)
    return pl.pallas_call(
        functools.partial(_comb_kernel, tm=tm),
        grid=(nt,),
        in_specs=[
            pl.BlockSpec((tm * TOP_K * ROW_TILE, 128), lambda i: (i, 0)),
            pl.BlockSpec((tm, D_MODEL), lambda i: (i, 0)),
            pl.BlockSpec((tm, 128), lambda i: (i, 0)),
            pl.BlockSpec((tm, PLE_DIM), lambda i: (layer * nt + i, 0)),
            pl.BlockSpec((1, D_MODEL), const),
            pl.BlockSpec((D_MODEL, D_MODEL), const),
            pl.BlockSpec((PLE_DIM, D_MODEL), const),
        ],
        out_specs=pl.BlockSpec((tm, D_MODEL), lambda i: (i, 0)),
        out_shape=jax.ShapeDtypeStruct((T, D_MODEL), F32),
        compiler_params=_cparams(("parallel",)),
        name="combine_ple",
    )(yg, x1, rw, p2d, gple, wpg, wpp)


def _final_kernel(x_ref, g_ref, o_ref):
    o_ref[...] = _rms(x_ref[...], g_ref[...])


def _final_norm(x2d, g, tm):
    T = x2d.shape[0]
    return pl.pallas_call(
        _final_kernel,
        grid=(T // tm,),
        in_specs=[pl.BlockSpec((tm, D_MODEL), lambda i: (i, 0)), pl.BlockSpec((1, D_MODEL), lambda i: (0, 0))],
        out_specs=pl.BlockSpec((tm, D_MODEL), lambda i: (i, 0)),
        out_shape=jax.ShapeDtypeStruct((T, D_MODEL), F32),
        compiler_params=_cparams(("parallel",)),
        name="final_norm",
    )(x2d, g)


def _rope_tables(pos, theta, rot_dim, period, offset, scale):
    half = rot_dim // 2
    inv = jnp.float32(theta) ** (-jnp.arange(half, dtype=F32) * 2.0 / rot_dim)
    ang = pos.astype(F32)[:, None] * inv
    cos, sin = jnp.cos(ang), jnp.sin(ang)
    T = pos.shape[0]
    c = jnp.ones((T, period), F32).at[:, offset:offset + half].set(cos).at[:, offset + half:offset + rot_dim].set(cos)
    s_up = jnp.zeros((T, period), F32).at[:, offset + half:offset + rot_dim].set(sin)
    s_dn = jnp.zeros((T, period), F32).at[:, offset:offset + half].set(-sin)
    rep = 128 // period
    return jnp.concatenate([jnp.tile(t, (1, rep)) for t in (c, s_up, s_dn)], axis=1) * jnp.float32(scale)


def _prep_w_in(w):
    c_q = w[:, 0:512]
    c_kv = w[:, 512:768]
    k_pe = w[:, 768:800]
    dil = w[:, 800:800 + 2304]
    gates = w[:, 3104:5152]
    kslot = jnp.zeros((D_MODEL, 256), w.dtype).at[:, MLA_NOPE:MLA_NOPE + MLA_ROPE].set(k_pe)
    return jnp.concatenate([gates, c_q, c_kv, kslot, dil], axis=1).astype(BF16)


def _prep_w_q(w):
    w = w.reshape(MLA_Q_LORA, MLA_HEADS, MLA_NOPE + MLA_ROPE)
    w = jnp.pad(w, ((0, 0), (0, 0), (0, MLA_SLOT - MLA_NOPE - MLA_ROPE)))
    return w.reshape(MLA_Q_LORA, MLA_HEADS * MLA_SLOT).astype(BF16)


def _prep_w_kv(w):
    w = w.reshape(MLA_KV_LORA, MLA_HEADS, MLA_NOPE + MLA_V)
    wk = jnp.pad(w[:, :, :MLA_NOPE], ((0, 0), (0, 0), (0, MLA_SLOT - MLA_NOPE)))
    wv = jnp.pad(w[:, :, MLA_NOPE:], ((0, 0), (0, 0), (0, MLA_SLOT - MLA_V)))
    return (wk.reshape(MLA_KV_LORA, MLA_HEADS * MLA_SLOT).astype(BF16),
            wv.reshape(MLA_KV_LORA, MLA_HEADS * MLA_SLOT).astype(BF16))


def _dest_kernel(ri_ref, ps_ref, o_ref):
    ri = ri_ref[...].astype(F32)
    lane = lax.broadcasted_iota(jnp.int32, ri.shape, 1)
    ps = ps_ref[...]

    def col(k):
        return jnp.sum(jnp.where(lane == k, ri, 0.0), axis=-1, keepdims=True)

    def dest(k):
        start = jnp.sum(jnp.where(lane == col(k).astype(jnp.int32), ps, 0.0), axis=-1, keepdims=True)
        return (start + col(2 + k)).astype(jnp.int32)

    o_ref[...] = jnp.where(lane == 0, dest(0), jnp.where(lane == 1, dest(1), 0))


def _dest_rows(ri, pstarts, tm):
    T = ri.shape[0]
    ps = jnp.zeros((1, 128), F32).at[0, :N_EXPERTS].set(pstarts.astype(F32))
    return pl.pallas_call(
        _dest_kernel,
        grid=(T // tm,),
        in_specs=[pl.BlockSpec((tm, 128), lambda i: (i, 0)), pl.BlockSpec((1, 128), lambda i: (0, 0))],
        out_specs=pl.BlockSpec((tm, 128), lambda i: (i, 0)),
        out_shape=jax.ShapeDtypeStruct((T, 128), jnp.int32),
        compiler_params=_cparams(("parallel",)),
        name="dest_rows",
    )(ri, ps)


def _dispatch_plan(ri, cnt, T, tm):
    counts = cnt[0, :N_EXPERTS].astype(jnp.int32)
    pcounts = (counts + FFN_BM - 1) // FFN_BM * FFN_BM
    pends = jnp.cumsum(pcounts)
    pstarts = pends - pcounts
    dest = _dest_rows(ri, pstarts, tm)[:, 0:TOP_K]
    n_blocks = (T * TOP_K) // FFN_BM + N_EXPERTS
    row_tok = (jnp.arange(n_blocks * FFN_BM, dtype=jnp.int32) % T).at[dest.reshape(-1)].set(
        jnp.arange(T * TOP_K, dtype=jnp.int32) // TOP_K, unique_indices=True)
    block_e = jnp.minimum(jnp.searchsorted(pends, jnp.arange(n_blocks, dtype=jnp.int32) * FFN_BM, side='right'),
                          N_EXPERTS - 1).astype(jnp.int32)
    nvb = (pends[-1] // FFN_BM).astype(jnp.int32).reshape(1)
    return dest, row_tok, block_e, nvb


def kernel(x, p, positions, g_mix, w_in, g_q_lat, w_q_up, g_kv_lat, w_kv_up, w_branch_a, w_branch_b, w_out, g_ffn, w_router_grp, b_router_grp, w_router_exp, b_router_exp, w_exp_gate, w_exp_up, w_exp_down, g_ple, w_ple_gate, w_ple_proj, g_final):
    B, S, D = x.shape
    T = B * S
    TM_IN, TM_QKV, TQ, TM_POST, TM_ROUTE, TM_COMB = 512, 512, 512, 256, 512, 256

    pos = positions.reshape(T)
    dil_scale = DIL_HEAD_DIM ** -0.5 * LOG2E
    mla_scale = (MLA_NOPE + MLA_ROPE) ** -0.5 * LOG2E
    tab_dil = jnp.concatenate([_rope_tables(pos, ROPE_THETA, PARTIAL_ROT, DIL_HEAD_DIM, 0, dil_scale),
                               _rope_tables(pos, ROPE_THETA, PARTIAL_ROT, DIL_HEAD_DIM, 0, 1.0)], axis=1)
    tab_mla = jnp.concatenate([_rope_tables(pos, MLA_ROPE_THETA, MLA_ROPE, MLA_SLOT, MLA_NOPE, mla_scale),
                               _rope_tables(pos, MLA_ROPE_THETA, MLA_ROPE, MLA_SLOT, MLA_NOPE, 1.0)], axis=1)

    xc = x.reshape(T, D)
    for i in range(DEPTH):
        w_in_p = _prep_w_in(w_in[i])
        wq = _prep_w_q(w_q_up[i])
        wk, wv = _prep_w_kv(w_kv_up[i])
        wr = jnp.zeros((D, 128), F32).at[:, :N_GROUPS].set(w_router_grp[i]).at[:, N_GROUPS:N_GROUPS + N_EXPERTS].set(
            w_router_exp[i]).astype(BF16)
        br = jnp.zeros((1, 128), F32).at[0, :N_GROUPS].set(b_router_grp[i]).at[0, N_GROUPS:N_GROUPS + N_EXPERTS].set(
            b_router_exp[i].reshape(N_EXPERTS))

        main, d0, d1, d2 = _in_proj(xc, g_mix[i].reshape(1, D), w_in_p, tab_dil, B, S, TM_IN)
        q, k, v = _mla_qkv(main, g_q_lat[i].reshape(1, -1), g_kv_lat[i].reshape(1, -1), wq, wk, wv, tab_mla, TM_QKV)
        o_a = _mla_attn(q, k, v, B, S, TQ).reshape(T, MLA_HEADS * MLA_V)
        ogs, lss = [], []
        for (_, dil), qkv in zip(DIL_PATTERN, (d0, d1, d2)):
            og, ls = _dil_attn(qkv, B, dil, S // dil)
            ogs.append(og)
            lss.append(ls)
        x1, h2, lg = _post(xc, o_a, ogs, lss, main, w_branch_a[i].astype(BF16), w_branch_b[i].astype(BF16),
                           w_out[i].astype(BF16), g_ffn[i].reshape(1, D), wr, br, B, S, TM_POST)
        ri, rw, cnt = _route(lg, TM_ROUTE)
        dest, row_tok, block_e, nvb = _dispatch_plan(ri, cnt, T, TM_ROUTE)
        xs = _sc_gather_rows(h2.reshape(T, ROW_TILE, 128), row_tok)
        ys = _expert_ffn(block_e, nvb, xs.reshape(-1, 128), w_exp_gate, w_exp_up, w_exp_down, i)
        yg = _sc_gather_rows(ys.reshape(-1, ROW_TILE, 128), dest.reshape(T * TOP_K))
        xc = _combine_ple(yg.reshape(-1, 128), x1, rw, p.reshape(DEPTH * T, PLE_DIM), g_ple[i].reshape(1, D),
                          w_ple_gate[i].astype(BF16), w_ple_proj[i].astype(BF16), TM_COMB, i)
    return _final_norm(xc, g_final.reshape(1, D), 512).reshape(B, S, D)
```

```python
import functools
import math

import jax
import jax.numpy as jnp
from jax import lax
from jax.experimental import pallas as pl
from jax.experimental.pallas import tpu as pltpu
from jax.experimental.pallas import tpu_sc as plsc

F32 = jnp.float32
BF16 = jnp.bfloat16

D_MODEL = 1024
DEPTH = 4
RMS_EPS = 1e-6
NEG = -1e30
LOG2E = math.log2(math.e)

MLA_HEADS = 16
MLA_Q_LORA = 512
MLA_KV_LORA = 256
MLA_NOPE = 64
MLA_ROPE = 32
MLA_V = 64
MLA_ROPE_THETA = 10000.0
MLA_SLOT = 128
MLA_HPS = 2

DIL_PATTERN = ((128, 1), (512, 4), (2048, 16))
DIL_GROUPS = 3
DIL_HEADS = 4
DIL_HEAD_DIM = 64
DIL_W = DIL_HEADS * DIL_HEAD_DIM
DIL_BLK = 128
ROPE_THETA = 500000.0
PARTIAL_ROT = DIL_HEAD_DIM // 4

N_GROUPS = 8
EXPERTS_PER_GROUP = 8
N_EXPERTS = 64
TOP_K = 2
EXPERT_FF = 256
PLE_DIM = 256

IN_TN = 768
MAIN_COLS = 3072
IN_COLS_PAD = MAIN_COLS + DIL_GROUPS * 3 * DIL_W
N_MAIN_TILES = MAIN_COLS // IN_TN

FFN_BM = 256
ROW_TILE = D_MODEL // 128
VMEM_LIMIT = 48 * 1024 * 1024

SC_CORES = 2
SC_SUBCORES = 16
SC_LANES = 16
SC_CHUNK = 32


def _cparams(sem, bounds_checks=True):
    return pltpu.CompilerParams(dimension_semantics=sem, vmem_limit_bytes=VMEM_LIMIT,
                                disable_bounds_checks=not bounds_checks)


def _rms(x, g):
    return x * lax.rsqrt(jnp.mean(x * x, axis=-1, keepdims=True) + RMS_EPS) * g


def _rope128(x, c, s_up, s_dn, half):
    return x * c + pltpu.roll(x, half, 1) * s_up + pltpu.roll(x, 128 - half, 1) * s_dn


def _in_kernel(x_ref, g_ref, w_ref, tab_ref, main_ref, d0_ref, d1_ref, d2_ref, xn_ref, acc_ref):
    j = pl.program_id(1)

    @pl.when(j == 0)
    def _():
        xn_ref[...] = _rms(x_ref[...], g_ref[...]).astype(BF16)

    acc = jnp.dot(xn_ref[...], w_ref[...], preferred_element_type=F32)

    @pl.when(j < N_MAIN_TILES)
    def _():
        main_ref[...] = acc.astype(BF16)

    def dil_tile(out_ref, dil):
        for c in range(4):
            t0 = 0 if c < 2 else 3
            x = acc[:, c * 128:(c + 1) * 128]
            acc_ref[c] = _rope128(
                x, tab_ref[:, t0 * 128:(t0 + 1) * 128], tab_ref[:, (t0 + 1) * 128:(t0 + 2) * 128],
                tab_ref[:, (t0 + 2) * 128:(t0 + 3) * 128], PARTIAL_ROT // 2)
        for c in range(4, 6):
            acc_ref[c] = acc[:, c * 128:(c + 1) * 128]
        rows = acc_ref.shape[1] // dil
        for r in range(dil):
            for c in range(6):
                out_ref[0, r, :, c * 128:(c + 1) * 128] = acc_ref[c, pl.ds(r, rows, stride=dil), :].astype(BF16)

    for gi, (_, dil) in enumerate(DIL_PATTERN):
        pl.when(j == N_MAIN_TILES + gi)(functools.partial(dil_tile, (d0_ref, d1_ref, d2_ref)[gi], dil))


def _in_proj(x2d, g, w, tab, B, S, tm):
    T = x2d.shape[0]
    nt = S // tm
    dil_shapes = [jax.ShapeDtypeStruct((B, d, S // d, 3 * DIL_W), BF16) for _, d in DIL_PATTERN]
    dil_specs = [pl.BlockSpec((1, d, tm // d, 3 * DIL_W), lambda i, j, nt=nt: (i // nt, 0, i % nt, 0))
                 for _, d in DIL_PATTERN]
    return pl.pallas_call(
        _in_kernel,
        grid=(T // tm, IN_COLS_PAD // IN_TN),
        in_specs=[
            pl.BlockSpec((tm, D_MODEL), lambda i, j: (i, 0)),
            pl.BlockSpec((1, D_MODEL), lambda i, j: (0, 0)),
            pl.BlockSpec((D_MODEL, IN_TN), lambda i, j: (0, j)),
            pl.BlockSpec((tm, 6 * 128), lambda i, j: (i, 0)),
        ],
        out_specs=[pl.BlockSpec((tm, IN_TN), lambda i, j: (i, jnp.minimum(j, N_MAIN_TILES - 1)))] + dil_specs,
        out_shape=[jax.ShapeDtypeStruct((T, MAIN_COLS), BF16)] + dil_shapes,
        scratch_shapes=[pltpu.VMEM((tm, D_MODEL), BF16), pltpu.VMEM((IN_TN // 128, tm, 128), F32)],
        compiler_params=_cparams(("parallel", "arbitrary")),
        name="in_proj",
    )(x2d, g, w, tab)


def _qkv_kernel(cq_ref, ckv_ref, kpe_ref, gq_ref, gkv_ref, wq_ref, wk_ref, wv_ref, tab_ref,
                q_ref, k_ref, v_ref):
    qn = _rms(cq_ref[...].astype(F32), gq_ref[...]).astype(BF16)
    kvn = _rms(ckv_ref[...].astype(F32), gkv_ref[...]).astype(BF16)
    qacc = jnp.dot(qn, wq_ref[...], preferred_element_type=F32)
    kacc = jnp.dot(kvn, wk_ref[...], preferred_element_type=F32)
    vacc = jnp.dot(kvn, wv_ref[...], preferred_element_type=F32)
    lane_v = lax.broadcasted_iota(jnp.int32, vacc.shape, 1)
    v_ref[...] = jnp.where(lane_v % MLA_SLOT < MLA_V, vacc, 1.0).astype(BF16)
    half = MLA_ROPE // 2
    tq = [tab_ref[:, t * 128:(t + 1) * 128] for t in range(3)]
    tk = [tab_ref[:, t * 128:(t + 1) * 128] for t in range(3, 6)]
    krot = _rope128(kpe_ref[:, 0:128].astype(F32), tk[0], tk[1], tk[2], half)
    for h in range(MLA_HEADS):
        sl = slice(h * MLA_SLOT, (h + 1) * MLA_SLOT)
        q_ref[:, sl] = _rope128(qacc[:, sl], tq[0], tq[1], tq[2], half).astype(BF16)
        k_ref[:, sl] = (kacc[:, sl] + krot).astype(BF16)


def _mla_qkv(main, gq, gkv, wq, wk, wv, tab, tm):
    T = main.shape[0]
    const = lambda i: (0, 0)
    return pl.pallas_call(
        _qkv_kernel,
        grid=(T // tm,),
        in_specs=[
            pl.BlockSpec((tm, MLA_Q_LORA), lambda i: (i, 2048 // MLA_Q_LORA)),
            pl.BlockSpec((tm, MLA_KV_LORA), lambda i: (i, 2560 // MLA_KV_LORA)),
            pl.BlockSpec((tm, 256), lambda i: (i, 2816 // 256)),
            pl.BlockSpec((1, MLA_Q_LORA), const),
            pl.BlockSpec((1, MLA_KV_LORA), const),
            pl.BlockSpec((MLA_Q_LORA, MLA_HEADS * MLA_SLOT), const),
            pl.BlockSpec((MLA_KV_LORA, MLA_HEADS * MLA_SLOT), const),
            pl.BlockSpec((MLA_KV_LORA, MLA_HEADS * MLA_SLOT), const),
            pl.BlockSpec((tm, 6 * 128), lambda i: (i, 0)),
        ],
        out_specs=[
            pl.BlockSpec((tm, MLA_HEADS * MLA_SLOT), lambda i: (i, 0)),
            pl.BlockSpec((tm, MLA_HEADS * MLA_SLOT), lambda i: (i, 0)),
            pl.BlockSpec((tm, MLA_HEADS * MLA_SLOT), lambda i: (i, 0)),
        ],
        out_shape=[
            jax.ShapeDtypeStruct((T, MLA_HEADS * MLA_SLOT), BF16),
            jax.ShapeDtypeStruct((T, MLA_HEADS * MLA_SLOT), BF16),
            jax.ShapeDtypeStruct((T, MLA_HEADS * MLA_SLOT), BF16),
        ],
        compiler_params=_cparams(("parallel",)),
        name="mla_qkv",
    )(main, main, main, gq, gkv, wq, wk, wv, tab)


def _mla_attn_kernel(q_ref, k_ref, v_ref, o_ref, m_ref, acc_ref, *, tq):
    i = pl.program_id(2)
    pair_w = 2 * MLA_SLOT
    lane = lax.broadcasted_iota(jnp.int32, (tq, pair_w), 1)
    qh = []
    for pp in range(MLA_HPS // 2):
        qp = q_ref[0, :, pp * pair_w:(pp + 1) * pair_w]
        zero = jnp.zeros_like(qp)
        qh += [jnp.where(lane < MLA_SLOT, qp, zero), jnp.where(lane >= MLA_SLOT, qp, zero)]
    m_ref[...] = jnp.full(m_ref.shape, NEG, F32)
    acc_ref[...] = jnp.zeros(acc_ref.shape, F32)
    hq = tq // 2
    dn = (((1,), (1,)), ((), ()))

    def step(start, width, r0, mask):
        kb = k_ref[0, pl.ds(start, width), :]
        vb = v_ref[0, pl.ds(start, width), :]
        for h in range(MLA_HPS):
            pp = h // 2
            s = lax.dot_general(qh[h][r0:], kb[:, pp * pair_w:(pp + 1) * pair_w], dn,
                                preferred_element_type=F32)
            if mask is not None:
                s = jnp.where(mask, s, NEG)
            m_prev = m_ref[h, r0:, :]
            m_new = jnp.maximum(m_prev, jnp.max(s, axis=-1, keepdims=True))
            alpha = jnp.exp2(m_prev - m_new)
            p = jnp.exp2(s - jnp.concatenate([m_new] * (width // 128), axis=-1))
            pv = jnp.dot(p.astype(BF16), vb[:, h * MLA_SLOT:(h + 1) * MLA_SLOT], preferred_element_type=F32)
            acc_ref[h, r0:, :] = alpha * acc_ref[h, r0:, :] + pv
            m_ref[h, r0:, :] = m_new

    def causal(rows):
        return lax.broadcasted_iota(jnp.int32, (rows, hq), 1) <= lax.broadcasted_iota(jnp.int32, (rows, hq), 0)

    def q_block(n_full):
        for j in range(n_full):
            step(j * tq, tq, 0, None)
        step(n_full * tq, hq, 0, causal(tq))
        step(n_full * tq + hq, hq, hq, causal(hq))

    for n_full in range(k_ref.shape[1] // tq):
        pl.when(i == n_full)(functools.partial(q_block, n_full))
    lane_o = lax.broadcasted_iota(jnp.int32, (tq, 2 * MLA_V), 1)
    for pp in range(MLA_HPS // 2):
        outs = []
        for h in (2 * pp, 2 * pp + 1):
            a = acc_ref[h]
            outs.append(a / pltpu.roll(a, MLA_V, 1))
        o_ref[0, :, pp * 2 * MLA_V:(pp + 1) * 2 * MLA_V] = jnp.where(
            lane_o < MLA_V, outs[0], pltpu.roll(outs[1], MLA_V, 1)).astype(BF16)


def _mla_attn(q, k, v, B, S, tq):
    q = q.reshape(B, S, MLA_HEADS * MLA_SLOT)
    k = k.reshape(B, S, MLA_HEADS * MLA_SLOT)
    v = v.reshape(B, S, MLA_HEADS * MLA_SLOT)
    return pl.pallas_call(
        functools.partial(_mla_attn_kernel, tq=tq),
        grid=(B, MLA_HEADS // MLA_HPS, S // tq),
        in_specs=[
            pl.BlockSpec((1, tq, MLA_HPS * MLA_SLOT), lambda b, h, i: (b, i, h)),
            pl.BlockSpec((1, S, MLA_HPS * MLA_SLOT), lambda b, h, i: (b, 0, h)),
            pl.BlockSpec((1, S, MLA_HPS * MLA_SLOT), lambda b, h, i: (b, 0, h)),
        ],
        out_specs=pl.BlockSpec((1, tq, MLA_HPS * MLA_V), lambda b, h, i: (b, i, h)),
        out_shape=jax.ShapeDtypeStruct((B, S, MLA_HEADS * MLA_V), BF16),
        scratch_shapes=[pltpu.VMEM((MLA_HPS, tq, 128), F32), pltpu.VMEM((MLA_HPS, tq, MLA_SLOT), F32)],
        compiler_params=_cparams(("parallel", "parallel", "arbitrary")),
        name="mla_attn",
    )(q, k, v)


def _dil_kernel(qkv_ref, o_ref, lse_ref, bias_ref, *, nb):
    dil = qkv_ref.shape[1]
    width = 2 * DIL_BLK if nb > 1 else DIL_BLK
    row = lax.broadcasted_iota(jnp.int32, (DIL_BLK, width), 0)
    col = lax.broadcasted_iota(jnp.int32, (DIL_BLK, width), 1)
    bias_ref[0] = jnp.where(col <= row, 0.0, NEG)
    if nb > 1:
        later = jnp.logical_or(jnp.logical_and(col >= DIL_BLK, col - DIL_BLK <= row),
                               jnp.logical_and(col < DIL_BLK, col >= row))
        bias_ref[1] = jnp.where(later, 0.0, NEG)
    head_of_lane = lax.broadcasted_iota(jnp.int32, (DIL_BLK, DIL_W), 1) // DIL_HEAD_DIM
    dn = (((1,), (1,)), ((), ()))

    def by_head(parts):
        out = parts[DIL_HEADS - 1]
        for h in range(DIL_HEADS - 2, -1, -1):
            out = jnp.where(head_of_lane == h, parts[h], out)
        return out

    def unit(u):
        r = u // nb
        n = u % nb
        q0 = pl.multiple_of(n * DIL_BLK, DIL_BLK)
        q = qkv_ref[0, r, pl.ds(q0, DIL_BLK), 0:DIL_W]
        if nb > 1:
            w0 = pl.multiple_of(jnp.maximum(n - 1, 0) * DIL_BLK, DIL_BLK)
            bias = bias_ref[jnp.minimum(n, 1)]
        else:
            w0 = 0
            bias = bias_ref[0]
        kw = qkv_ref[0, r, pl.ds(w0, width), DIL_W:2 * DIL_W]
        vw = qkv_ref[0, r, pl.ds(w0, width), 2 * DIL_W:3 * DIL_W]
        zero = jnp.zeros_like(q)
        qs = jnp.concatenate([jnp.where(head_of_lane == h, q, zero) for h in range(DIL_HEADS)], axis=0)
        s = lax.dot_general(qs, kw, dn, preferred_element_type=F32)
        s = (s.reshape(DIL_HEADS, DIL_BLK, width) + bias[None]).reshape(DIL_HEADS * DIL_BLK, width)
        m = jnp.max(s, axis=-1, keepdims=True)
        e = jnp.exp2(s - m)
        den = jnp.sum(e, axis=-1, keepdims=True)
        pv = jnp.dot(e.astype(BF16), vw, preferred_element_type=F32)
        lse = m + jnp.log2(den)
        blk = lambda t, h: t[h * DIL_BLK:(h + 1) * DIL_BLK]
        o = by_head([blk(pv, h) for h in range(DIL_HEADS)]) / by_head(
            [jnp.broadcast_to(blk(den, h), (DIL_BLK, DIL_W)) for h in range(DIL_HEADS)])
        o_ref[0, r, pl.ds(q0, DIL_BLK), :] = o.astype(BF16)
        lse_ref[0, r, pl.ds(q0, DIL_BLK), :] = by_head(
            [jnp.broadcast_to(blk(lse, h), (DIL_BLK, DIL_W)) for h in range(DIL_HEADS)])

    def body(t, carry):
        unit(2 * t)
        unit(2 * t + 1)
        return carry

    lax.fori_loop(0, dil * nb // 2, body, 0)


def _dil_attn(qkv, B, dil, L):
    nb = L // DIL_BLK
    return pl.pallas_call(
        functools.partial(_dil_kernel, nb=nb),
        grid=(B,),
        in_specs=[pl.BlockSpec((1, dil, L, 3 * DIL_W), lambda b: (b, 0, 0, 0))],
        out_specs=[
            pl.BlockSpec((1, dil, L, DIL_W), lambda b: (b, 0, 0, 0)),
            pl.BlockSpec((1, dil, L, DIL_W), lambda b: (b, 0, 0, 0)),
        ],
        out_shape=[
            jax.ShapeDtypeStruct((B, dil, L, DIL_W), BF16),
            jax.ShapeDtypeStruct((B, dil, L, DIL_W), F32),
        ],
        scratch_shapes=[pltpu.VMEM((2, DIL_BLK, 2 * DIL_BLK if nb > 1 else DIL_BLK), F32)],
        compiler_params=_cparams(("parallel",)),
        name=f"dil_attn_d{dil}",
    )(qkv)


def _post_kernel(x_ref, oa_ref, og0_ref, og1_ref, og2_ref, ls0_ref, ls1_ref, ls2_ref, gates_ref,
                 wa_ref, wb_ref, wo_ref, gffn_ref, wr_ref, br_ref,
                 x1_ref, h2_ref, lg_ref, ob_ref, *, tm):
    ls =[r[0] for r in (ls0_ref, ls1_ref, ls2_ref)]
    og = [r[0] for r in (og0_ref, og1_ref, og2_ref)]
    for gi, (_, dil) in enumerate(DIL_PATTERN):
        rows = tm // dil
        for r in range(dil):
            for c in range(2):
                sl = slice(c * 128, (c + 1) * 128)
                ob_ref[2 * gi + c, pl.ds(r, rows, stride=dil), :] = og[gi][r][:, sl].astype(F32)
                ob_ref[6 + 2 * gi + c, pl.ds(r, rows, stride=dil), :] = ls[gi][r][:, sl]

    def tok_major(k):
        return jnp.concatenate([ob_ref[2 * k], ob_ref[2 * k + 1]], axis=-1)

    l0, l1, l2 = tok_major(3), tok_major(4), tok_major(5)
    mx = jnp.maximum(jnp.maximum(l0, l1), l2)
    w0, w1, w2 = jnp.exp2(l0 - mx), jnp.exp2(l1 - mx), jnp.exp2(l2 - mx)
    ob = (w0 * tok_major(0) + w1 * tok_major(1) + w2 * tok_major(2)) / (w0 + w1 + w2)
    ya = jnp.dot(oa_ref[...], wa_ref[...], preferred_element_type=F32)
    yb = jnp.dot(ob.astype(BF16), wb_ref[...], preferred_element_type=F32)
    merged = (jax.nn.sigmoid(gates_ref[:, 0:D_MODEL].astype(F32)) * ya
              + jax.nn.sigmoid(gates_ref[:, D_MODEL:2 * D_MODEL].astype(F32)) * yb)
    x1 = x_ref[...] + jnp.dot(merged.astype(BF16), wo_ref[...], preferred_element_type=F32)
    x1_ref[...] = x1
    h2 = _rms(x1, gffn_ref[...])
    for c in range(D_MODEL // 128):
        h2_ref[pl.ds(c, tm, stride=D_MODEL // 128), :] = h2[:, c * 128:(c + 1) * 128]
    lg_ref[...] = jnp.dot(h2.astype(BF16), wr_ref[...], preferred_element_type=F32) + br_ref[...]


def _post(x2d, oa, ogs, lss, main, wa, wb, wo, gffn, wr, br, B, S, tm):
    T = x2d.shape[0]
    nt = S // tm
    const = lambda i: (0, 0)
    res_specs = [pl.BlockSpec((1, d, tm // d, DIL_W), lambda i, nt=nt: (i // nt, 0, i % nt, 0))
                 for _, d in DIL_PATTERN]
    return pl.pallas_call(
        functools.partial(_post_kernel, tm=tm),
        grid=(T // tm,),
        in_specs=[pl.BlockSpec((tm, D_MODEL), lambda i: (i, 0)),
                  pl.BlockSpec((tm, D_MODEL), lambda i: (i, 0))]
                 + res_specs + res_specs
                 + [pl.BlockSpec((tm, 2 * D_MODEL), lambda i: (i, 0)),
                    pl.BlockSpec((D_MODEL, D_MODEL), const),
                    pl.BlockSpec((DIL_W, D_MODEL), const),
                    pl.BlockSpec((D_MODEL, D_MODEL), const),
                    pl.BlockSpec((1, D_MODEL), const),
                    pl.BlockSpec((D_MODEL, 128), const),
                    pl.BlockSpec((1, 128), const)],
        out_specs=[pl.BlockSpec((tm, D_MODEL), lambda i: (i, 0)),
                   pl.BlockSpec((tm * ROW_TILE, 128), lambda i: (i, 0)),
                   pl.BlockSpec((tm, 128), lambda i: (i, 0))],
        out_shape=[jax.ShapeDtypeStruct((T, D_MODEL), F32),
                   jax.ShapeDtypeStruct((T * ROW_TILE, 128), F32),
                   jax.ShapeDtypeStruct((T, 128), F32)],
        scratch_shapes=[pltpu.VMEM((12, tm, 128), F32)],
        compiler_params=_cparams(("parallel",)),
        name="post_attn",
    )(x2d, oa, *ogs, *lss, main, wa, wb, wo, gffn, wr, br)


def _route_kernel(lg_ref, ri_ref, rw_ref, cnt_ref, carry_ref, *, tm):
    i = pl.program_id(0)

    @pl.when(i == 0)
    def _():
        carry_ref[...] = jnp.zeros(carry_ref.shape, F32)

    lg = lg_ref[...]
    lane = lax.broadcasted_iota(jnp.int32, lg.shape, 1)
    lane_f = lane.astype(F32)
    ninf = jnp.float32(-jnp.inf)

    def first_max(vals):
        vmax = jnp.max(vals, axis=-1, keepdims=True)
        idx = jnp.min(jnp.where(vals == vmax, lane_f, 128.0), axis=-1, keepdims=True)
        return vmax, idx.astype(jnp.int32)

    gl = jnp.where(lane < N_GROUPS, lg, ninf)
    gmax, g_sel = first_max(gl)
    p_g = 1.0 / jnp.sum(jnp.exp(gl - gmax), axis=-1, keepdims=True)
    lo = N_GROUPS + g_sel * EXPERTS_PER_GROUP
    el = jnp.where(jnp.logical_and(lane >= lo, lane < lo + EXPERTS_PER_GROUP), lg, ninf)
    v0, i0 = first_max(el)
    v1, i1 = first_max(jnp.where(lane == i0, ninf, el))
    t = jnp.exp(v1 - v0)
    w0 = p_g / (1.0 + t)
    w1 = p_g * t / (1.0 + t)
    e0 = i0 - N_GROUPS
    e1 = i1 - N_GROUPS
    hit0 = lane == e0
    hit1 = lane == e1
    oh = jnp.where(jnp.logical_or(hit0, hit1), 1.0, 0.0).astype(F32)
    r_i = lax.broadcasted_iota(jnp.int32, (tm, tm), 0)
    c_i = lax.broadcasted_iota(jnp.int32, (tm, tm), 1)
    lower = jnp.where(c_i < r_i, 1.0, 0.0).astype(BF16)
    excl = jnp.dot(lower, oh.astype(BF16), preferred_element_type=F32) + carry_ref[...]
    r0 = jnp.sum(jnp.where(hit0, excl, 0.0), axis=-1, keepdims=True).astype(jnp.int32)
    r1 = jnp.sum(jnp.where(hit1, excl, 0.0), axis=-1, keepdims=True).astype(jnp.int32)
    carry_ref[...] = carry_ref[...] + jnp.sum(oh, axis=0, keepdims=True)
    zi = jnp.zeros(lg.shape, jnp.int32)
    ri_ref[...] = jnp.where(lane == 0, e0, jnp.where(lane == 1, e1, jnp.where(lane == 2, r0, jnp.where(lane == 3, r1, zi))))
    rw_ref[...] = jnp.where(lane == 0, w0, jnp.where(lane == 1, w1, jnp.zeros(lg.shape, F32)))
    cnt_ref[...] = carry_ref[...]


def _route(lg, tm):
    T = lg.shape[0]
    return pl.pallas_call(
        functools.partial(_route_kernel, tm=tm),
        grid=(T // tm,),
        in_specs=[pl.BlockSpec((tm, 128), lambda i: (i, 0))],
        out_specs=[pl.BlockSpec((tm, 128), lambda i: (i, 0)),
                   pl.BlockSpec((tm, 128), lambda i: (i, 0)),
                   pl.BlockSpec((1, 128), lambda i: (0, 0))],
        out_shape=[jax.ShapeDtypeStruct((T, 128), jnp.int32),
                   jax.ShapeDtypeStruct((T, 128), F32),
                   jax.ShapeDtypeStruct((1, 128), F32)],
        scratch_shapes=[pltpu.VMEM((1, 128), F32)],
        compiler_params=_cparams(("arbitrary",)),
        name="route",
    )(lg)


def _sc_gather_rows(table, idx, n_out=None):
    inverse = n_out is not None
    n_src = idx.shape[0]
    n = n_out if inverse else n_src
    n_workers = SC_CORES * SC_SUBCORES
    per_w = n // n_workers
    n_chunks = per_w // SC_CHUNK
    assert per_w * n_workers == n and n_chunks * SC_CHUNK == per_w and n_chunks % 2 == 0
    mesh = plsc.VectorSubcoreMesh(core_axis_name="c", subcore_axis_name="s",
                                  num_cores=SC_CORES, num_subcores=SC_SUBCORES)

    @functools.partial(
        pl.kernel, mesh=mesh,
        out_type=jax.ShapeDtypeStruct((n,) + table.shape[1:], table.dtype),
        scratch_types=[pltpu.VMEM((per_w,), jnp.int32),
                       pltpu.VMEM((SC_CHUNK,) + table.shape[1:], table.dtype),
                       pltpu.VMEM((SC_CHUNK,) + table.shape[1:], table.dtype),
                       pltpu.SemaphoreType.DMA, pltpu.SemaphoreType.DMA,
                       pltpu.VMEM((n_src if inverse else SC_LANES,), jnp.int32)],
        compiler_params=pltpu.CompilerParams(use_tc_tiling_on_sc=True, needs_layout_passes=not inverse),
        name="sc_dispatch_rows" if inverse else "sc_gather_rows",
    )
    def gather(table_hbm, idx_hbm, out_hbm, idx_v, rows_a, rows_b, sem_a, sem_b, map_v):
        wid = lax.axis_index("s") * SC_CORES + lax.axis_index("c")
        base = wid * per_w
        if inverse:
            pltpu.sync_copy(idx_hbm, map_v)
            lanes = lax.iota(jnp.int32, SC_LANES)

            @pl.loop(0, per_w // SC_LANES)
            def _(j):
                idx_v[pl.ds(j * SC_LANES, SC_LANES)] = lax.rem(base + j * SC_LANES + lanes, table.shape[0])

            @pl.loop(0, n_src // SC_LANES)
            def _(a):
                local = map_v[pl.ds(a * SC_LANES, SC_LANES)] - base
                mine = jnp.logical_and(local >= 0, local < per_w)
                plsc.store_scatter(idx_v, [local], lax.div(a * SC_LANES + lanes, TOP_K), mask=mine)
        else:
            pltpu.sync_copy(idx_hbm.at[pl.ds(base, per_w)], idx_v)

        def fetch(chunk, rows_v, sem):
            return pltpu.make_async_copy(table_hbm.at[idx_v.at[pl.ds(chunk * SC_CHUNK, SC_CHUNK)]], rows_v, sem)

        def flush(chunk, rows_v):
            pltpu.sync_copy(rows_v, out_hbm.at[pl.ds(base + chunk * SC_CHUNK, SC_CHUNK)])

        fetch(0, rows_a, sem_a).start()

        @pl.loop(0, n_chunks, step=2)
        def _(c):
            fetch(c + 1, rows_b, sem_b).start()
            fetch(c, rows_a, sem_a).wait()
            flush(c, rows_a)

            @pl.when(c + 2 < n_chunks)
            def _():
                fetch(c + 2, rows_a, sem_a).start()

            fetch(c + 1, rows_b, sem_b).wait()
            flush(c + 1, rows_b)

    return gather(table, idx)


def _rows_from_tiles(ref, first, n_rows, stride=ROW_TILE):
    return jnp.concatenate([ref[pl.ds(first + c, n_rows, stride=stride), :] for c in range(ROW_TILE)], axis=-1)


def _rows_to_tiles(ref, val):
    for c in range(ROW_TILE):
        ref[pl.ds(c, val.shape[0], stride=ROW_TILE), :] = val[:, c * 128:(c + 1) * 128]


def _ffn_kernel(be_ref, nvb_ref, x_ref, w1_ref, w3_ref, w2_ref, y_ref):
    b = pl.program_id(0)
    nvb = nvb_ref[0]

    @pl.when(b < nvb)
    def _():
        xb = _rows_from_tiles(x_ref, 0, FFN_BM).astype(BF16)
        h1 = jnp.dot(xb, w1_ref[0, 0].astype(BF16), preferred_element_type=F32)
        h3 = jnp.dot(xb, w3_ref[0, 0].astype(BF16), preferred_element_type=F32)
        a = (jax.nn.silu(h1) * h3).astype(BF16)
        _rows_to_tiles(y_ref, jnp.dot(a, w2_ref[0, 0].astype(BF16), preferred_element_type=F32))

    @pl.when(b >= nvb)
    def _():
        y_ref[...] = jnp.zeros(y_ref.shape, F32)


def _expert_ffn(block_e, nvb, xs, w1, w3, w2, layer):
    nb = block_e.shape[0]

    def used(b, nv):
        return jnp.minimum(b, jnp.maximum(nv[0] - 1, 0))

    return pl.pallas_call(
        _ffn_kernel,
        grid_spec=pltpu.PrefetchScalarGridSpec(
            num_scalar_prefetch=2,
            grid=(nb,),
            in_specs=[
                pl.BlockSpec((FFN_BM * ROW_TILE, 128), lambda b, be, nv: (used(b, nv), 0)),
                pl.BlockSpec((1, 1, D_MODEL, EXPERT_FF), lambda b, be, nv: (layer, be[b], 0, 0)),
                pl.BlockSpec((1, 1, D_MODEL, EXPERT_FF), lambda b, be, nv: (layer, be[b], 0, 0)),
                pl.BlockSpec((1, 1, EXPERT_FF, D_MODEL), lambda b, be, nv: (layer, be[b], 0, 0)),
            ],
            out_specs=pl.BlockSpec((FFN_BM * ROW_TILE, 128), lambda b, be, nv: (b, 0)),
        ),
        out_shape=jax.ShapeDtypeStruct((nb * FFN_BM * ROW_TILE, 128), F32),
        compiler_params=_cparams(("arbitrary",)),
        name="expert_ffn",
    )(block_e, nvb, xs, w1, w3, w2)


def _comb_kernel(yg_ref, x1_ref, rw_ref, p_ref, gple_ref, wpg_ref, wpp_ref, gout_ref, o_ref, *, tm, final):
    w = rw_ref[...]
    y0 = _rows_from_tiles(yg_ref, 0, tm, stride=TOP_K * ROW_TILE)
    y1 = _rows_from_tiles(yg_ref, ROW_TILE, tm, stride=TOP_K * ROW_TILE)
    x2 = x1_ref[...] + (y0 * w[:, 0:1] + y1 * w[:, 1:2])
    e = jnp.dot(p_ref[...].astype(BF16), wpp_ref[...], preferred_element_type=F32)
    gate = jax.nn.sigmoid(jnp.dot(_rms(x2, gple_ref[...]).astype(BF16), wpg_ref[...], preferred_element_type=F32))
    x3 = x2 + gate * e
    o_ref[...] = _rms(x3, gout_ref[...]) if final else x3


def _combine_ple(yg, x1, rw, p2d, gple, wpg, wpp, gout, tm, layer):
    T = x1.shape[0]
    nt = T // tm
    const = lambda i: (0, 0)
    return pl.pallas_call(
        functools.partial(_comb_kernel, tm=tm, final=layer == DEPTH - 1),
        grid=(nt,),
        in_specs=[
            pl.BlockSpec((tm * TOP_K * ROW_TILE, 128), lambda i: (i, 0)),
            pl.BlockSpec((tm, D_MODEL), lambda i: (i, 0)),
            pl.BlockSpec((tm, 128), lambda i: (i, 0)),
            pl.BlockSpec((tm, PLE_DIM), lambda i: (layer * nt + i, 0)),
            pl.BlockSpec((1, D_MODEL), const),
            pl.BlockSpec((D_MODEL, D_MODEL), const),
            pl.BlockSpec((PLE_DIM, D_MODEL), const),
            pl.BlockSpec((1, D_MODEL), const),
        ],
        out_specs=pl.BlockSpec((tm, D_MODEL), lambda i: (i, 0)),
        out_shape=jax.ShapeDtypeStruct((T, D_MODEL), F32),
        compiler_params=_cparams(("parallel",)),
        name="combine_ple",
    )(yg, x1, rw, p2d, gple, wpg, wpp, gout)


def _rope_tables(pos, theta, rot_dim, period, offset, scale):
    half = rot_dim // 2
    inv = jnp.float32(theta) ** (-jnp.arange(half, dtype=F32) * 2.0 / rot_dim)
    ang = pos.astype(F32)[:, None] * inv
    cos, sin = jnp.cos(ang), jnp.sin(ang)
    T = pos.shape[0]
    c = jnp.ones((T, period), F32).at[:, offset:offset + half].set(cos).at[:, offset + half:offset + rot_dim].set(cos)
    s_up = jnp.zeros((T, period), F32).at[:, offset + half:offset + rot_dim].set(sin)
    s_dn = jnp.zeros((T, period), F32).at[:, offset:offset + half].set(-sin)
    rep = 128 // period
    return jnp.concatenate([jnp.tile(t, (1, rep)) for t in (c, s_up, s_dn)], axis=1) * jnp.float32(scale)


def _prep_w_in(w):
    c_q = w[:, 0:512]
    c_kv = w[:, 512:768]
    k_pe = w[:, 768:800]
    dil = w[:, 800:800 + 2304]
    gates = w[:, 3104:5152]
    kslot = jnp.zeros((D_MODEL, 256), w.dtype).at[:, MLA_NOPE:MLA_NOPE + MLA_ROPE].set(k_pe)
    return jnp.concatenate([gates, c_q, c_kv, kslot, dil], axis=1).astype(BF16)


def _prep_w_q(w):
    w = w.reshape(MLA_Q_LORA, MLA_HEADS, MLA_NOPE + MLA_ROPE)
    w = jnp.pad(w, ((0, 0), (0, 0), (0, MLA_SLOT - MLA_NOPE - MLA_ROPE)))
    return w.reshape(MLA_Q_LORA, MLA_HEADS * MLA_SLOT).astype(BF16)


def _prep_w_kv(w):
    w = w.reshape(MLA_KV_LORA, MLA_HEADS, MLA_NOPE + MLA_V)
    wk = jnp.pad(w[:, :, :MLA_NOPE], ((0, 0), (0, 0), (0, MLA_SLOT - MLA_NOPE)))
    wv = jnp.pad(w[:, :, MLA_NOPE:], ((0, 0), (0, 0), (0, MLA_SLOT - MLA_V)))
    return (wk.reshape(MLA_KV_LORA, MLA_HEADS * MLA_SLOT).astype(BF16),
            wv.reshape(MLA_KV_LORA, MLA_HEADS * MLA_SLOT).astype(BF16))


def _dest_kernel(ri_ref, ps_ref, o_ref):
    ri = ri_ref[...].astype(F32)
    lane = lax.broadcasted_iota(jnp.int32, ri.shape, 1)
    ps = ps_ref[...]

    def col(k):
        return jnp.sum(jnp.where(lane == k, ri, 0.0), axis=-1, keepdims=True)

    def dest(k):
        start = jnp.sum(jnp.where(lane == col(k).astype(jnp.int32), ps, 0.0), axis=-1, keepdims=True)
        return (start + col(2 + k)).astype(jnp.int32)

    o_ref[...] = jnp.where(lane == 0, dest(0), jnp.where(lane == 1, dest(1), 0))


def _dest_rows(ri, pstarts, tm):
    T = ri.shape[0]
    ps = jnp.zeros((1, 128), F32).at[0, :N_EXPERTS].set(pstarts.astype(F32))
    return pl.pallas_call(
        _dest_kernel,
        grid=(T // tm,),
        in_specs=[pl.BlockSpec((tm, 128), lambda i: (i, 0)), pl.BlockSpec((1, 128), lambda i: (0, 0))],
        out_specs=pl.BlockSpec((tm, 128), lambda i: (i, 0)),
        out_shape=jax.ShapeDtypeStruct((T, 128), jnp.int32),
        compiler_params=_cparams(("parallel",)),
        name="dest_rows",
    )(ri, ps)


def _dispatch_plan(ri, cnt, T, tm):
    counts = cnt[0, :N_EXPERTS].astype(jnp.int32)
    pcounts = (counts + FFN_BM - 1) // FFN_BM * FFN_BM
    pends = jnp.cumsum(pcounts)
    pstarts = pends - pcounts
    dest = _dest_rows(ri, pstarts, tm)[:, 0:TOP_K]
    n_blocks = (T * TOP_K) // FFN_BM + N_EXPERTS
    block_e = jnp.minimum(jnp.searchsorted(pends, jnp.arange(n_blocks, dtype=jnp.int32) * FFN_BM, side='right'),
                          N_EXPERTS - 1).astype(jnp.int32)
    nvb = (pends[-1] // FFN_BM).astype(jnp.int32).reshape(1)
    return dest.reshape(T * TOP_K), n_blocks * FFN_BM, block_e, nvb


def kernel(x, p, positions, g_mix, w_in, g_q_lat, w_q_up, g_kv_lat, w_kv_up, w_branch_a, w_branch_b, w_out, g_ffn, w_router_grp, b_router_grp, w_router_exp, b_router_exp, w_exp_gate, w_exp_up, w_exp_down, g_ple, w_ple_gate, w_ple_proj, g_final):
    B, S, D = x.shape
    T = B * S
    TM_IN, TM_QKV, TQ, TM_POST, TM_ROUTE, TM_COMB = 512, 512, 512, 256, 512, 256

    pos = positions.reshape(T)
    dil_scale = DIL_HEAD_DIM ** -0.5 * LOG2E
    mla_scale = (MLA_NOPE + MLA_ROPE) ** -0.5 * LOG2E
    tab_dil = jnp.concatenate([_rope_tables(pos, ROPE_THETA, PARTIAL_ROT, DIL_HEAD_DIM, 0, dil_scale),
                               _rope_tables(pos, ROPE_THETA, PARTIAL_ROT, DIL_HEAD_DIM, 0, 1.0)], axis=1)
    tab_mla = jnp.concatenate([_rope_tables(pos, MLA_ROPE_THETA, MLA_ROPE, MLA_SLOT, MLA_NOPE, mla_scale),
                               _rope_tables(pos, MLA_ROPE_THETA, MLA_ROPE, MLA_SLOT, MLA_NOPE, 1.0)], axis=1)

    xc = x.reshape(T, D)
    for i in range(DEPTH):
        w_in_p = _prep_w_in(w_in[i])
        wq = _prep_w_q(w_q_up[i])
        wk, wv = _prep_w_kv(w_kv_up[i])
        wr = jnp.zeros((D, 128), F32).at[:, :N_GROUPS].set(w_router_grp[i]).at[:, N_GROUPS:N_GROUPS + N_EXPERTS].set(
            w_router_exp[i]).astype(BF16)
        br = jnp.zeros((1, 128), F32).at[0, :N_GROUPS].set(b_router_grp[i]).at[0, N_GROUPS:N_GROUPS + N_EXPERTS].set(
            b_router_exp[i].reshape(N_EXPERTS))

        main, d0, d1, d2 = _in_proj(xc, g_mix[i].reshape(1, D), w_in_p, tab_dil, B, S, TM_IN)
        q, k, v = _mla_qkv(main, g_q_lat[i].reshape(1, -1), g_kv_lat[i].reshape(1, -1), wq, wk, wv, tab_mla, TM_QKV)
        o_a = _mla_attn(q, k, v, B, S, TQ).reshape(T, MLA_HEADS * MLA_V)
        ogs, lss = [], []
        for (_, dil), qkv in zip(DIL_PATTERN, (d0, d1, d2)):
            og, ls = _dil_attn(qkv, B, dil, S // dil)
            ogs.append(og)
            lss.append(ls)
        x1, h2, lg = _post(xc, o_a, ogs, lss, main, w_branch_a[i].astype(BF16), w_branch_b[i].astype(BF16),
                           w_out[i].astype(BF16), g_ffn[i].reshape(1, D), wr, br, B, S, TM_POST)
        ri, rw, cnt = _route(lg, TM_ROUTE)
        dest, n_rows, block_e, nvb = _dispatch_plan(ri, cnt, T, TM_ROUTE)
        xs = _sc_gather_rows(h2.reshape(T, ROW_TILE, 128), dest, n_out=n_rows)
        ys = _expert_ffn(block_e, nvb, xs.reshape(-1, 128), w_exp_gate, w_exp_up, w_exp_down, i)
        yg = _sc_gather_rows(ys.reshape(-1, ROW_TILE, 128), dest)
        xc = _combine_ple(yg.reshape(-1, 128), x1, rw, p.reshape(DEPTH * T, PLE_DIM), g_ple[i].reshape(1, D),
                          w_ple_gate[i].astype(BF16), w_ple_proj[i].astype(BF16), g_final.reshape(1, D), TM_COMB, i)
    return xc.reshape(B, S, D)
```

```python
import functools
import math

import jax
import jax.numpy as jnp
from jax import lax
from jax.experimental import pallas as pl
from jax.experimental.pallas import tpu as pltpu
from jax.experimental.pallas import tpu_sc as plsc

F32 = jnp.float32
BF16 = jnp.bfloat16

D_MODEL = 1024
DEPTH = 4
RMS_EPS = 1e-6
NEG = -1e30
LOG2E = math.log2(math.e)

MLA_HEADS = 16
MLA_Q_LORA = 512
MLA_KV_LORA = 256
MLA_NOPE = 64
MLA_ROPE = 32
MLA_V = 64
MLA_ROPE_THETA = 10000.0
MLA_SLOT = 128

DIL_PATTERN = ((128, 1), (512, 4), (2048, 16))
DIL_GROUPS = 3
DIL_HEADS = 4
DIL_HEAD_DIM = 64
DIL_W = DIL_HEADS * DIL_HEAD_DIM
DIL_BLK = 128
ROPE_THETA = 500000.0
PARTIAL_ROT = DIL_HEAD_DIM // 4

N_GROUPS = 8
EXPERTS_PER_GROUP = 8
N_EXPERTS = 64
TOP_K = 2
EXPERT_FF = 256
PLE_DIM = 256

IN_TN = 768
MAIN_COLS = 3072
IN_COLS_PAD = MAIN_COLS + DIL_GROUPS * 3 * DIL_W
N_MAIN_TILES = MAIN_COLS // IN_TN

FFN_BM = 256
ROW_TILE = D_MODEL // 128
VMEM_LIMIT = 48 * 1024 * 1024

SC_CORES = 2
SC_SUBCORES = 16
SC_LANES = 16
SC_CHUNK = 32


def _cparams(sem, bounds_checks=True):
    return pltpu.CompilerParams(dimension_semantics=sem, vmem_limit_bytes=VMEM_LIMIT,
                                disable_bounds_checks=not bounds_checks)


def _rms(x, g):
    return x * lax.rsqrt(jnp.mean(x * x, axis=-1, keepdims=True) + RMS_EPS) * g


def _rope128(x, c, s_up, s_dn, half):
    return x * c + pltpu.roll(x, half, 1) * s_up + pltpu.roll(x, 128 - half, 1) * s_dn


def _in_kernel(x_ref, g_ref, w_ref, tab_ref, main_ref, d0_ref, d1_ref, d2_ref, xn_ref, acc_ref):
    j = pl.program_id(1)

    @pl.when(j == 0)
    def _():
        xn_ref[...] = _rms(x_ref[...], g_ref[...]).astype(BF16)

    acc = jnp.dot(xn_ref[...], w_ref[...], preferred_element_type=F32)

    @pl.when(j < N_MAIN_TILES)
    def _():
        main_ref[...] = acc.astype(BF16)

    def dil_tile(out_ref, dil):
        for c in range(4):
            t0 = 0 if c < 2 else 3
            x = acc[:, c * 128:(c + 1) * 128]
            acc_ref[c] = _rope128(
                x, tab_ref[:, t0 * 128:(t0 + 1) * 128], tab_ref[:, (t0 + 1) * 128:(t0 + 2) * 128],
                tab_ref[:, (t0 + 2) * 128:(t0 + 3) * 128], PARTIAL_ROT // 2)
        for c in range(4, 6):
            acc_ref[c] = acc[:, c * 128:(c + 1) * 128]
        rows = acc_ref.shape[1] // dil
        for r in range(dil):
            for c in range(6):
                out_ref[0, r, :, c * 128:(c + 1) * 128] = acc_ref[c, pl.ds(r, rows, stride=dil), :].astype(BF16)

    for gi, (_, dil) in enumerate(DIL_PATTERN):
        pl.when(j == N_MAIN_TILES + gi)(functools.partial(dil_tile, (d0_ref, d1_ref, d2_ref)[gi], dil))


def _in_proj(x2d, g, w, tab, B, S, tm):
    T = x2d.shape[0]
    nt = S // tm
    dil_shapes = [jax.ShapeDtypeStruct((B, d, S // d, 3 * DIL_W), BF16) for _, d in DIL_PATTERN]
    dil_specs = [pl.BlockSpec((1, d, tm // d, 3 * DIL_W), lambda i, j, nt=nt: (i // nt, 0, i % nt, 0))
                 for _, d in DIL_PATTERN]
    return pl.pallas_call(
        _in_kernel,
        grid=(T // tm, IN_COLS_PAD // IN_TN),
        in_specs=[
            pl.BlockSpec((tm, D_MODEL), lambda i, j: (i, 0)),
            pl.BlockSpec((1, D_MODEL), lambda i, j: (0, 0)),
            pl.BlockSpec((D_MODEL, IN_TN), lambda i, j: (0, j)),
            pl.BlockSpec((tm, 6 * 128), lambda i, j: (i, 0)),
        ],
        out_specs=[pl.BlockSpec((tm, IN_TN), lambda i, j: (i, jnp.minimum(j, N_MAIN_TILES - 1)))] + dil_specs,
        out_shape=[jax.ShapeDtypeStruct((T, MAIN_COLS), BF16)] + dil_shapes,
        scratch_shapes=[pltpu.VMEM((tm, D_MODEL), BF16), pltpu.VMEM((IN_TN // 128, tm, 128), F32)],
        compiler_params=_cparams(("parallel", "arbitrary")),
        name="in_proj",
    )(x2d, g, w, tab)


def _qkv_kernel(cq_ref, ckv_ref, kpe_ref, gq_ref, gkv_ref, wq_ref, wk_ref, wv_ref, tab_ref,
                q_ref, k_ref, v_ref):
    qn = _rms(cq_ref[...].astype(F32), gq_ref[...]).astype(BF16)
    kvn = _rms(ckv_ref[...].astype(F32), gkv_ref[...]).astype(BF16)
    qacc = jnp.dot(qn, wq_ref[...], preferred_element_type=F32)
    kacc = jnp.dot(kvn, wk_ref[...], preferred_element_type=F32)
    vacc = jnp.dot(kvn, wv_ref[...], preferred_element_type=F32)
    lane_v = lax.broadcasted_iota(jnp.int32, vacc.shape, 1)
    v_ref[...] = jnp.where(lane_v % MLA_SLOT < MLA_V, vacc, 1.0).astype(BF16)
    half = MLA_ROPE // 2
    tq = [tab_ref[:, t * 128:(t + 1) * 128] for t in range(3)]
    tk = [tab_ref[:, t * 128:(t + 1) * 128] for t in range(3, 6)]
    krot = _rope128(kpe_ref[:, 0:128].astype(F32), tk[0], tk[1], tk[2], half)
    for h in range(MLA_HEADS):
        sl = slice(h * MLA_SLOT, (h + 1) * MLA_SLOT)
        q_ref[:, sl] = _rope128(qacc[:, sl], tq[0], tq[1], tq[2], half).astype(BF16)
        k_ref[:, sl] = (kacc[:, sl] + krot).astype(BF16)


def _mla_qkv(main, gq, gkv, wq, wk, wv, tab, tm):
    T = main.shape[0]
    const = lambda i: (0, 0)
    return pl.pallas_call(
        _qkv_kernel,
        grid=(T // tm,),
        in_specs=[
            pl.BlockSpec((tm, MLA_Q_LORA), lambda i: (i, 2048 // MLA_Q_LORA)),
            pl.BlockSpec((tm, MLA_KV_LORA), lambda i: (i, 2560 // MLA_KV_LORA)),
            pl.BlockSpec((tm, 256), lambda i: (i, 2816 // 256)),
            pl.BlockSpec((1, MLA_Q_LORA), const),
            pl.BlockSpec((1, MLA_KV_LORA), const),
            pl.BlockSpec((MLA_Q_LORA, MLA_HEADS * MLA_SLOT), const),
            pl.BlockSpec((MLA_KV_LORA, MLA_HEADS * MLA_SLOT), const),
            pl.BlockSpec((MLA_KV_LORA, MLA_HEADS * MLA_SLOT), const),
            pl.BlockSpec((tm, 6 * 128), lambda i: (i, 0)),
        ],
        out_specs=[
            pl.BlockSpec((tm, MLA_HEADS * MLA_SLOT), lambda i: (i, 0)),
            pl.BlockSpec((tm, MLA_HEADS * MLA_SLOT), lambda i: (i, 0)),
            pl.BlockSpec((tm, MLA_HEADS * MLA_SLOT), lambda i: (i, 0)),
        ],
        out_shape=[
            jax.ShapeDtypeStruct((T, MLA_HEADS * MLA_SLOT), BF16),
            jax.ShapeDtypeStruct((T, MLA_HEADS * MLA_SLOT), BF16),
            jax.ShapeDtypeStruct((T, MLA_HEADS * MLA_SLOT), BF16),
        ],
        compiler_params=_cparams(("parallel",)),
        name="mla_qkv",
    )(main, main, main, gq, gkv, wq, wk, wv, tab)


def _mla_attn_kernel(qa_ref, qb_ref, k_ref, v_ref, o_ref, m_ref, acc_ref, *, tq):
    p_id = pl.program_id(2)
    nq = k_ref.shape[1] // tq
    pair_w = 2 * MLA_SLOT
    lane = lax.broadcasted_iota(jnp.int32, (tq, pair_w), 1)
    qh = []
    for q_ref in (qa_ref, qb_ref):
        qp = q_ref[0]
        zero = jnp.zeros_like(qp)
        qh.append([jnp.where(lane < MLA_SLOT, qp, zero), jnp.where(lane >= MLA_SLOT, qp, zero)])
    hq = tq // 2
    dn = (((1,), (1,)), ((), ()))

    def step(blk, start, width, r0, mask, first=False):
        kb = k_ref[0, pl.ds(start, width), :]
        vb = v_ref[0, pl.ds(start, width), :]
        for h in range(2):
            s = lax.dot_general(qh[blk][h][r0:], kb, dn, preferred_element_type=F32)
            if mask is not None:
                s = jnp.where(mask, s, NEG)
            m_cur = jnp.max(s, axis=-1, keepdims=True)
            if first:
                m_new = jnp.broadcast_to(m_cur, (tq - r0, 128))
            else:
                m_prev = m_ref[blk, h, r0:, :]
                m_new = jnp.maximum(m_prev, m_cur)
            p = jnp.exp2(s - jnp.concatenate([m_new] * (width // 128), axis=-1))
            pv = jnp.dot(p.astype(BF16), vb[:, h * MLA_SLOT:(h + 1) * MLA_SLOT], preferred_element_type=F32)
            if first:
                acc_ref[blk, h, r0:, :] = pv
            else:
                acc_ref[blk, h, r0:, :] = jnp.exp2(m_prev - m_new) * acc_ref[blk, h, r0:, :] + pv
            m_ref[blk, h, r0:, :] = m_new

    def causal(rows):
        return lax.broadcasted_iota(jnp.int32, (rows, hq), 1) <= lax.broadcasted_iota(jnp.int32, (rows, hq), 0)

    def q_blocks(n_a):
        n_b = nq - 1 - n_a
        for blk, n_full in ((1, n_b), (0, n_a)):
            for j in range(n_full):
                step(blk, j * tq, tq, 0, None, first=j == 0)
            step(blk, n_full * tq, hq, 0, causal(tq), first=n_full == 0)
            step(blk, n_full * tq + hq, hq, hq, causal(hq))

    for n_a in range(nq // 2):
        pl.when(p_id == n_a)(functools.partial(q_blocks, n_a))
    lane_o = lax.broadcasted_iota(jnp.int32, (tq, 2 * MLA_V), 1)
    for blk in range(2):
        outs = []
        for h in range(2):
            a = acc_ref[blk, h]
            outs.append(a / pltpu.roll(a, MLA_V, 1))
        o_ref[0, blk, 0] = jnp.where(lane_o < MLA_V, outs[0], pltpu.roll(outs[1], MLA_V, 1)).astype(BF16)


def _mla_attn(q, k, v, B, S, tq):
    nq = S // tq
    q = q.reshape(B, S, MLA_HEADS * MLA_SLOT)
    k = k.reshape(B, S, MLA_HEADS * MLA_SLOT)
    v = v.reshape(B, S, MLA_HEADS * MLA_SLOT)
    return pl.pallas_call(
        functools.partial(_mla_attn_kernel, tq=tq),
        grid=(B, MLA_HEADS // 2, nq // 2),
        in_specs=[
            pl.BlockSpec((1, tq, 2 * MLA_SLOT), lambda b, h, p: (b, p, h)),
            pl.BlockSpec((1, tq, 2 * MLA_SLOT), lambda b, h, p: (b, nq - 1 - p, h)),
            pl.BlockSpec((1, S, 2 * MLA_SLOT), lambda b, h, p: (b, 0, h)),
            pl.BlockSpec((1, S, 2 * MLA_SLOT), lambda b, h, p: (b, 0, h)),
        ],
        out_specs=pl.BlockSpec((1, 2, 1, tq, 2 * MLA_V), lambda b, h, p: (b, 0, p, 0, h)),
        out_shape=jax.ShapeDtypeStruct((B, 2, nq // 2, tq, MLA_HEADS * MLA_V), BF16),
        scratch_shapes=[pltpu.VMEM((2, 2, tq, 128), F32), pltpu.VMEM((2, 2, tq, MLA_SLOT), F32)],
        compiler_params=_cparams(("parallel", "parallel", "arbitrary")),
        name="mla_attn",
    )(q, q, k, v)


def _dil_kernel(qkv_ref, o_ref, lse_ref, bias_ref, *, nb):
    dil = qkv_ref.shape[1]
    width = 2 * DIL_BLK if nb > 1 else DIL_BLK
    row = lax.broadcasted_iota(jnp.int32, (DIL_BLK, width), 0)
    col = lax.broadcasted_iota(jnp.int32, (DIL_BLK, width), 1)
    bias_ref[0] = jnp.where(col <= row, 0.0, NEG)
    if nb > 1:
        later = jnp.logical_or(jnp.logical_and(col >= DIL_BLK, col - DIL_BLK <= row),
                               jnp.logical_and(col < DIL_BLK, col >= row))
        bias_ref[1] = jnp.where(later, 0.0, NEG)
    head_of_lane = lax.broadcasted_iota(jnp.int32, (DIL_BLK, DIL_W), 1) // DIL_HEAD_DIM
    dn = (((1,), (1,)), ((), ()))

    def by_head(parts):
        out = parts[DIL_HEADS - 1]
        for h in range(DIL_HEADS - 2, -1, -1):
            out = jnp.where(head_of_lane == h, parts[h], out)
        return out

    def unit(u):
        r = u // nb
        n = u % nb
        q0 = pl.multiple_of(n * DIL_BLK, DIL_BLK)
        q = qkv_ref[0, r, pl.ds(q0, DIL_BLK), 0:DIL_W]
        if nb > 1:
            w0 = pl.multiple_of(jnp.maximum(n - 1, 0) * DIL_BLK, DIL_BLK)
            bias = bias_ref[jnp.minimum(n, 1)]
        else:
            w0 = 0
            bias = bias_ref[0]
        kw = qkv_ref[0, r, pl.ds(w0, width), DIL_W:2 * DIL_W]
        vw = qkv_ref[0, r, pl.ds(w0, width), 2 * DIL_W:3 * DIL_W]
        zero = jnp.zeros_like(q)
        qs = jnp.concatenate([jnp.where(head_of_lane == h, q, zero) for h in range(DIL_HEADS)], axis=0)
        s = lax.dot_general(qs, kw, dn, preferred_element_type=F32)
        s = (s.reshape(DIL_HEADS, DIL_BLK, width) + bias[None]).reshape(DIL_HEADS * DIL_BLK, width)
        m = jnp.max(s, axis=-1, keepdims=True)
        e = jnp.exp2(s - m)
        den = jnp.sum(e, axis=-1, keepdims=True)
        pv = jnp.dot(e.astype(BF16), vw, preferred_element_type=F32)
        lse = m + jnp.log2(den)
        blk = lambda t, h: t[h * DIL_BLK:(h + 1) * DIL_BLK]
        o = by_head([blk(pv, h) for h in range(DIL_HEADS)]) / by_head(
            [jnp.broadcast_to(blk(den, h), (DIL_BLK, DIL_W)) for h in range(DIL_HEADS)])
        o_ref[0, r, pl.ds(q0, DIL_BLK), :] = o.astype(BF16)
        lse_ref[0, r, pl.ds(q0, DIL_BLK), :] = by_head(
            [jnp.broadcast_to(blk(lse, h), (DIL_BLK, DIL_W)) for h in range(DIL_HEADS)])

    def body(t, carry):
        unit(2 * t)
        unit(2 * t + 1)
        return carry

    lax.fori_loop(0, dil * nb // 2, body, 0)


def _dil_attn(qkv, B, dil, L):
    nb = L // DIL_BLK
    return pl.pallas_call(
        functools.partial(_dil_kernel, nb=nb),
        grid=(B,),
        in_specs=[pl.BlockSpec((1, dil, L, 3 * DIL_W), lambda b: (b, 0, 0, 0))],
        out_specs=[
            pl.BlockSpec((1, dil, L, DIL_W), lambda b: (b, 0, 0, 0)),
            pl.BlockSpec((1, dil, L, DIL_W), lambda b: (b, 0, 0, 0)),
        ],
        out_shape=[
            jax.ShapeDtypeStruct((B, dil, L, DIL_W), BF16),
            jax.ShapeDtypeStruct((B, dil, L, DIL_W), F32),
        ],
        scratch_shapes=[pltpu.VMEM((2, DIL_BLK, 2 * DIL_BLK if nb > 1 else DIL_BLK), F32)],
        compiler_params=_cparams(("parallel",)),
        name=f"dil_attn_d{dil}",
    )(qkv)


def _post_kernel(x_ref, oa_ref, og0_ref, og1_ref, og2_ref, ls0_ref, ls1_ref, ls2_ref, gates_ref,
                 wa_ref, wb_ref, wo_ref, gffn_ref, wr_ref, br_ref,
                 x1_ref, h2_ref, lg_ref, ob_ref, *, tm):
    ls =[r[0] for r in (ls0_ref, ls1_ref, ls2_ref)]
    og = [r[0] for r in (og0_ref, og1_ref, og2_ref)]
    for gi, (_, dil) in enumerate(DIL_PATTERN):
        rows = tm // dil
        for r in range(dil):
            for c in range(2):
                sl = slice(c * 128, (c + 1) * 128)
                ob_ref[2 * gi + c, pl.ds(r, rows, stride=dil), :] = og[gi][r][:, sl].astype(F32)
                ob_ref[6 + 2 * gi + c, pl.ds(r, rows, stride=dil), :] = ls[gi][r][:, sl]

    def tok_major(k):
        return jnp.concatenate([ob_ref[2 * k], ob_ref[2 * k + 1]], axis=-1)

    l0, l1, l2 = tok_major(3), tok_major(4), tok_major(5)
    mx = jnp.maximum(jnp.maximum(l0, l1), l2)
    w0, w1, w2 = jnp.exp2(l0 - mx), jnp.exp2(l1 - mx), jnp.exp2(l2 - mx)
    ob = (w0 * tok_major(0) + w1 * tok_major(1) + w2 * tok_major(2)) / (w0 + w1 + w2)
    ya = jnp.dot(oa_ref[0, 0, 0], wa_ref[...], preferred_element_type=F32)
    yb = jnp.dot(ob.astype(BF16), wb_ref[...], preferred_element_type=F32)
    merged = (jax.nn.sigmoid(gates_ref[:, 0:D_MODEL].astype(F32)) * ya
              + jax.nn.sigmoid(gates_ref[:, D_MODEL:2 * D_MODEL].astype(F32)) * yb)
    x1 = x_ref[...] + jnp.dot(merged.astype(BF16), wo_ref[...], preferred_element_type=F32)
    x1_ref[...] = x1
    h2 = _rms(x1, gffn_ref[...])
    for c in range(D_MODEL // 128):
        h2_ref[pl.ds(c, tm, stride=D_MODEL // 128), :] = h2[:, c * 128:(c + 1) * 128]
    lg_ref[...] = jnp.dot(h2.astype(BF16), wr_ref[...], preferred_element_type=F32) + br_ref[...]


def _post(x2d, oa, ogs, lss, main, wa, wb, wo, gffn, wr, br, B, S, tm):
    T = x2d.shape[0]
    nt = S // tm
    const = lambda i: (0, 0)
    res_specs = [pl.BlockSpec((1, d, tm // d, DIL_W), lambda i, nt=nt: (i // nt, 0, i % nt, 0))
                 for _, d in DIL_PATTERN]
    assert oa.shape[3] == tm and oa.shape[2] * 2 == nt

    def oa_index(i):
        it = i % nt
        late = it >= nt // 2
        return (i // nt, late.astype(jnp.int32), jnp.where(late, nt - 1 - it, it), 0, 0)

    return pl.pallas_call(
        functools.partial(_post_kernel, tm=tm),
        grid=(T // tm,),
        in_specs=[pl.BlockSpec((tm, D_MODEL), lambda i: (i, 0)),
                  pl.BlockSpec((1, 1, 1, tm, D_MODEL), oa_index)]
                 + res_specs + res_specs
                 + [pl.BlockSpec((tm, 2 * D_MODEL), lambda i: (i, 0)),
                    pl.BlockSpec((D_MODEL, D_MODEL), const),
                    pl.BlockSpec((DIL_W, D_MODEL), const),
                    pl.BlockSpec((D_MODEL, D_MODEL), const),
                    pl.BlockSpec((1, D_MODEL), const),
                    pl.BlockSpec((D_MODEL, 128), const),
                    pl.BlockSpec((1, 128), const)],
        out_specs=[pl.BlockSpec((tm, D_MODEL), lambda i: (i, 0)),
                   pl.BlockSpec((tm * ROW_TILE, 128), lambda i: (i, 0)),
                   pl.BlockSpec((tm, 128), lambda i: (i, 0))],
        out_shape=[jax.ShapeDtypeStruct((T, D_MODEL), F32),
                   jax.ShapeDtypeStruct((T * ROW_TILE, 128), F32),
                   jax.ShapeDtypeStruct((T, 128), F32)],
        scratch_shapes=[pltpu.VMEM((12, tm, 128), F32)],
        compiler_params=_cparams(("parallel",)),
        name="post_attn",
    )(x2d, oa, *ogs, *lss, main, wa, wb, wo, gffn, wr, br)


def _route_kernel(lg_ref, ri_ref, rw_ref, cnt_ref, carry_ref, *, tm):
    i = pl.program_id(0)

    @pl.when(i == 0)
    def _():
        carry_ref[...] = jnp.zeros(carry_ref.shape, F32)

    lg = lg_ref[...]
    lane = lax.broadcasted_iota(jnp.int32, lg.shape, 1)
    lane_f = lane.astype(F32)
    ninf = jnp.float32(-jnp.inf)

    def first_max(vals):
        vmax = jnp.max(vals, axis=-1, keepdims=True)
        idx = jnp.min(jnp.where(vals == vmax, lane_f, 128.0), axis=-1, keepdims=True)
        return vmax, idx.astype(jnp.int32)

    gl = jnp.where(lane < N_GROUPS, lg, ninf)
    gmax, g_sel = first_max(gl)
    p_g = 1.0 / jnp.sum(jnp.exp(gl - gmax), axis=-1, keepdims=True)
    lo = N_GROUPS + g_sel * EXPERTS_PER_GROUP
    el = jnp.where(jnp.logical_and(lane >= lo, lane < lo + EXPERTS_PER_GROUP), lg, ninf)
    v0, i0 = first_max(el)
    v1, i1 = first_max(jnp.where(lane == i0, ninf, el))
    t = jnp.exp(v1 - v0)
    w0 = p_g / (1.0 + t)
    w1 = p_g * t / (1.0 + t)
    e0 = i0 - N_GROUPS
    e1 = i1 - N_GROUPS
    hit0 = lane == e0
    hit1 = lane == e1
    oh = jnp.where(jnp.logical_or(hit0, hit1), 1.0, 0.0).astype(F32)
    r_i = lax.broadcasted_iota(jnp.int32, (tm, tm), 0)
    c_i = lax.broadcasted_iota(jnp.int32, (tm, tm), 1)
    lower = jnp.where(c_i < r_i, 1.0, 0.0).astype(BF16)
    excl = jnp.dot(lower, oh.astype(BF16), preferred_element_type=F32) + carry_ref[...]
    r0 = jnp.sum(jnp.where(hit0, excl, 0.0), axis=-1, keepdims=True).astype(jnp.int32)
    r1 = jnp.sum(jnp.where(hit1, excl, 0.0), axis=-1, keepdims=True).astype(jnp.int32)
    carry_ref[...] = carry_ref[...] + jnp.sum(oh, axis=0, keepdims=True)
    zi = jnp.zeros(lg.shape, jnp.int32)
    ri_ref[...] = jnp.where(lane == 0, e0, jnp.where(lane == 1, e1, jnp.where(lane == 2, r0, jnp.where(lane == 3, r1, zi))))
    rw_ref[...] = jnp.where(lane == 0, w0, jnp.where(lane == 1, w1, jnp.zeros(lg.shape, F32)))
    cnt_ref[...] = carry_ref[...]


def _route(lg, tm):
    T = lg.shape[0]
    return pl.pallas_call(
        functools.partial(_route_kernel, tm=tm),
        grid=(T // tm,),
        in_specs=[pl.BlockSpec((tm, 128), lambda i: (i, 0))],
        out_specs=[pl.BlockSpec((tm, 128), lambda i: (i, 0)),
                   pl.BlockSpec((tm, 128), lambda i: (i, 0)),
                   pl.BlockSpec((1, 128), lambda i: (0, 0))],
        out_shape=[jax.ShapeDtypeStruct((T, 128), jnp.int32),
                   jax.ShapeDtypeStruct((T, 128), F32),
                   jax.ShapeDtypeStruct((1, 128), F32)],
        scratch_shapes=[pltpu.VMEM((1, 128), F32)],
        compiler_params=_cparams(("arbitrary",)),
        name="route",
    )(lg)


def _sc_gather_rows(table, idx, n_out=None):
    inverse = n_out is not None
    n_src = idx.shape[0]
    n = n_out if inverse else n_src
    n_workers = SC_CORES * SC_SUBCORES
    per_w = n // n_workers
    n_chunks = per_w // SC_CHUNK
    assert per_w * n_workers == n and n_chunks * SC_CHUNK == per_w and n_chunks % 2 == 0
    mesh = plsc.VectorSubcoreMesh(core_axis_name="c", subcore_axis_name="s",
                                  num_cores=SC_CORES, num_subcores=SC_SUBCORES)

    @functools.partial(
        pl.kernel, mesh=mesh,
        out_type=jax.ShapeDtypeStruct((n,) + table.shape[1:], table.dtype),
        scratch_types=[pltpu.VMEM((per_w,), jnp.int32),
                       pltpu.VMEM((SC_CHUNK,) + table.shape[1:], table.dtype),
                       pltpu.VMEM((SC_CHUNK,) + table.shape[1:], table.dtype),
                       pltpu.SemaphoreType.DMA, pltpu.SemaphoreType.DMA,
                       pltpu.VMEM((n_src if inverse else SC_LANES,), jnp.int32)],
        compiler_params=pltpu.CompilerParams(use_tc_tiling_on_sc=True, needs_layout_passes=not inverse),
        name="sc_dispatch_rows" if inverse else "sc_gather_rows",
    )
    def gather(table_hbm, idx_hbm, out_hbm, idx_v, rows_a, rows_b, sem_a, sem_b, map_v):
        wid = lax.axis_index("s") * SC_CORES + lax.axis_index("c")
        base = wid * per_w
        if inverse:
            pltpu.sync_copy(idx_hbm, map_v)
            lanes = lax.iota(jnp.int32, SC_LANES)

            @pl.loop(0, per_w // SC_LANES)
            def _(j):
                idx_v[pl.ds(j * SC_LANES, SC_LANES)] = lax.rem(base + j * SC_LANES + lanes, table.shape[0])

            @pl.loop(0, n_src // SC_LANES)
            def _(a):
                local = map_v[pl.ds(a * SC_LANES, SC_LANES)] - base
                mine = jnp.logical_and(local >= 0, local < per_w)
                plsc.store_scatter(idx_v, [local], lax.div(a * SC_LANES + lanes, TOP_K), mask=mine)
        else:
            pltpu.sync_copy(idx_hbm.at[pl.ds(base, per_w)], idx_v)

        def fetch(chunk, rows_v, sem):
            return pltpu.make_async_copy(table_hbm.at[idx_v.at[pl.ds(chunk * SC_CHUNK, SC_CHUNK)]], rows_v, sem)

        def flush(chunk, rows_v):
            pltpu.sync_copy(rows_v, out_hbm.at[pl.ds(base + chunk * SC_CHUNK, SC_CHUNK)])

        fetch(0, rows_a, sem_a).start()

        @pl.loop(0, n_chunks, step=2)
        def _(c):
            fetch(c + 1, rows_b, sem_b).start()
            fetch(c, rows_a, sem_a).wait()
            flush(c, rows_a)

            @pl.when(c + 2 < n_chunks)
            def _():
                fetch(c + 2, rows_a, sem_a).start()

            fetch(c + 1, rows_b, sem_b).wait()
            flush(c + 1, rows_b)

    return gather(table, idx)


def _rows_from_tiles(ref, first, n_rows, stride=ROW_TILE):
    return jnp.concatenate([ref[pl.ds(first + c, n_rows, stride=stride), :] for c in range(ROW_TILE)], axis=-1)


def _rows_to_tiles(ref, val):
    for c in range(ROW_TILE):
        ref[pl.ds(c, val.shape[0], stride=ROW_TILE), :] = val[:, c * 128:(c + 1) * 128]


def _ffn_kernel(be_ref, nvb_ref, x_ref, w1_ref, w3_ref, w2_ref, y_ref):
    b = pl.program_id(0)
    nvb = nvb_ref[0]

    @pl.when(b < nvb)
    def _():
        xb = _rows_from_tiles(x_ref, 0, FFN_BM).astype(BF16)
        h1 = jnp.dot(xb, w1_ref[0, 0].astype(BF16), preferred_element_type=F32)
        h3 = jnp.dot(xb, w3_ref[0, 0].astype(BF16), preferred_element_type=F32)
        a = (jax.nn.silu(h1) * h3).astype(BF16)
        _rows_to_tiles(y_ref, jnp.dot(a, w2_ref[0, 0].astype(BF16), preferred_element_type=F32))

    @pl.when(b >= nvb)
    def _():
        y_ref[...] = jnp.zeros(y_ref.shape, F32)


def _expert_ffn(block_e, nvb, xs, w1, w3, w2, layer):
    nb = block_e.shape[0]

    def used(b, nv):
        return jnp.minimum(b, jnp.maximum(nv[0] - 1, 0))

    return pl.pallas_call(
        _ffn_kernel,
        grid_spec=pltpu.PrefetchScalarGridSpec(
            num_scalar_prefetch=2,
            grid=(nb,),
            in_specs=[
                pl.BlockSpec((FFN_BM * ROW_TILE, 128), lambda b, be, nv: (used(b, nv), 0)),
                pl.BlockSpec((1, 1, D_MODEL, EXPERT_FF), lambda b, be, nv: (layer, be[b], 0, 0)),
                pl.BlockSpec((1, 1, D_MODEL, EXPERT_FF), lambda b, be, nv: (layer, be[b], 0, 0)),
                pl.BlockSpec((1, 1, EXPERT_FF, D_MODEL), lambda b, be, nv: (layer, be[b], 0, 0)),
            ],
            out_specs=pl.BlockSpec((FFN_BM * ROW_TILE, 128), lambda b, be, nv: (b, 0)),
        ),
        out_shape=jax.ShapeDtypeStruct((nb * FFN_BM * ROW_TILE, 128), F32),
        compiler_params=_cparams(("arbitrary",)),
        name="expert_ffn",
    )(block_e, nvb, xs, w1, w3, w2)


def _comb_kernel(yg_ref, x1_ref, rw_ref, p_ref, gple_ref, wpg_ref, wpp_ref, gout_ref, o_ref, *, tm, final):
    w = rw_ref[...]
    y0 = _rows_from_tiles(yg_ref, 0, tm, stride=TOP_K * ROW_TILE)
    y1 = _rows_from_tiles(yg_ref, ROW_TILE, tm, stride=TOP_K * ROW_TILE)
    x2 = x1_ref[...] + (y0 * w[:, 0:1] + y1 * w[:, 1:2])
    e = jnp.dot(p_ref[...].astype(BF16), wpp_ref[...], preferred_element_type=F32)
    gate = jax.nn.sigmoid(jnp.dot(_rms(x2, gple_ref[...]).astype(BF16), wpg_ref[...], preferred_element_type=F32))
    x3 = x2 + gate * e
    o_ref[...] = _rms(x3, gout_ref[...]) if final else x3


def _combine_ple(yg, x1, rw, p2d, gple, wpg, wpp, gout, tm, layer):
    T = x1.shape[0]
    nt = T // tm
    const = lambda i: (0, 0)
    return pl.pallas_call(
        functools.partial(_comb_kernel, tm=tm, final=layer == DEPTH - 1),
        grid=(nt,),
        in_specs=[
            pl.BlockSpec((tm * TOP_K * ROW_TILE, 128), lambda i: (i, 0)),
            pl.BlockSpec((tm, D_MODEL), lambda i: (i, 0)),
            pl.BlockSpec((tm, 128), lambda i: (i, 0)),
            pl.BlockSpec((tm, PLE_DIM), lambda i: (layer * nt + i, 0)),
            pl.BlockSpec((1, D_MODEL), const),
            pl.BlockSpec((D_MODEL, D_MODEL), const),
            pl.BlockSpec((PLE_DIM, D_MODEL), const),
            pl.BlockSpec((1, D_MODEL), const),
        ],
        out_specs=pl.BlockSpec((tm, D_MODEL), lambda i: (i, 0)),
        out_shape=jax.ShapeDtypeStruct((T, D_MODEL), F32),
        compiler_params=_cparams(("parallel",)),
        name="combine_ple",
    )(yg, x1, rw, p2d, gple, wpg, wpp, gout)


def _rope_tables(pos, theta, rot_dim, period, offset, scale):
    half = rot_dim // 2
    inv = jnp.float32(theta) ** (-jnp.arange(half, dtype=F32) * 2.0 / rot_dim)
    ang = pos.astype(F32)[:, None] * inv
    cos, sin = jnp.cos(ang), jnp.sin(ang)
    T = pos.shape[0]
    c = jnp.ones((T, period), F32).at[:, offset:offset + half].set(cos).at[:, offset + half:offset + rot_dim].set(cos)
    s_up = jnp.zeros((T, period), F32).at[:, offset + half:offset + rot_dim].set(sin)
    s_dn = jnp.zeros((T, period), F32).at[:, offset:offset + half].set(-sin)
    rep = 128 // period
    return jnp.concatenate([jnp.tile(t, (1, rep)) for t in (c, s_up, s_dn)], axis=1) * jnp.float32(scale)


def _prep_w_in(w):
    c_q = w[:, 0:512]
    c_kv = w[:, 512:768]
    k_pe = w[:, 768:800]
    dil = w[:, 800:800 + 2304]
    gates = w[:, 3104:5152]
    kslot = jnp.zeros((D_MODEL, 256), w.dtype).at[:, MLA_NOPE:MLA_NOPE + MLA_ROPE].set(k_pe)
    return jnp.concatenate([gates, c_q, c_kv, kslot, dil], axis=1).astype(BF16)


def _prep_w_q(w):
    w = w.reshape(MLA_Q_LORA, MLA_HEADS, MLA_NOPE + MLA_ROPE)
    w = jnp.pad(w, ((0, 0), (0, 0), (0, MLA_SLOT - MLA_NOPE - MLA_ROPE)))
    return w.reshape(MLA_Q_LORA, MLA_HEADS * MLA_SLOT).astype(BF16)


def _prep_w_kv(w):
    w = w.reshape(MLA_KV_LORA, MLA_HEADS, MLA_NOPE + MLA_V)
    wk = jnp.pad(w[:, :, :MLA_NOPE], ((0, 0), (0, 0), (0, MLA_SLOT - MLA_NOPE)))
    wv = jnp.pad(w[:, :, MLA_NOPE:], ((0, 0), (0, 0), (0, MLA_SLOT - MLA_V)))
    return (wk.reshape(MLA_KV_LORA, MLA_HEADS * MLA_SLOT).astype(BF16),
            wv.reshape(MLA_KV_LORA, MLA_HEADS * MLA_SLOT).astype(BF16))


def _dest_kernel(ri_ref, ps_ref, o_ref):
    ri = ri_ref[...].astype(F32)
    lane = lax.broadcasted_iota(jnp.int32, ri.shape, 1)
    ps = ps_ref[...]

    def col(k):
        return jnp.sum(jnp.where(lane == k, ri, 0.0), axis=-1, keepdims=True)

    def dest(k):
        start = jnp.sum(jnp.where(lane == col(k).astype(jnp.int32), ps, 0.0), axis=-1, keepdims=True)
        return (start + col(2 + k)).astype(jnp.int32)

    o_ref[...] = jnp.where(lane == 0, dest(0), jnp.where(lane == 1, dest(1), 0))


def _dest_rows(ri, pstarts, tm):
    T = ri.shape[0]
    ps = jnp.zeros((1, 128), F32).at[0, :N_EXPERTS].set(pstarts.astype(F32))
    return pl.pallas_call(
        _dest_kernel,
        grid=(T // tm,),
        in_specs=[pl.BlockSpec((tm, 128), lambda i: (i, 0)), pl.BlockSpec((1, 128), lambda i: (0, 0))],
        out_specs=pl.BlockSpec((tm, 128), lambda i: (i, 0)),
        out_shape=jax.ShapeDtypeStruct((T, 128), jnp.int32),
        compiler_params=_cparams(("parallel",)),
        name="dest_rows",
    )(ri, ps)


def _dispatch_plan(ri, cnt, T, tm):
    counts = cnt[0, :N_EXPERTS].astype(jnp.int32)
    pcounts = (counts + FFN_BM - 1) // FFN_BM * FFN_BM
    pends = jnp.cumsum(pcounts)
    pstarts = pends - pcounts
    dest = _dest_rows(ri, pstarts, tm)[:, 0:TOP_K]
    n_blocks = (T * TOP_K) // FFN_BM + N_EXPERTS
    block_e = jnp.minimum(jnp.searchsorted(pends, jnp.arange(n_blocks, dtype=jnp.int32) * FFN_BM, side='right'),
                          N_EXPERTS - 1).astype(jnp.int32)
    nvb = (pends[-1] // FFN_BM).astype(jnp.int32).reshape(1)
    return dest.reshape(T * TOP_K), n_blocks * FFN_BM, block_e, nvb


def kernel(x, p, positions, g_mix, w_in, g_q_lat, w_q_up, g_kv_lat, w_kv_up, w_branch_a, w_branch_b, w_out, g_ffn, w_router_grp, b_router_grp, w_router_exp, b_router_exp, w_exp_gate, w_exp_up, w_exp_down, g_ple, w_ple_gate, w_ple_proj, g_final):
    B, S, D = x.shape
    T = B * S
    TM_IN, TM_QKV, TQ, TM_POST, TM_ROUTE, TM_COMB = 512, 512, 512, 512, 512, 256

    pos = positions.reshape(T)
    dil_scale = DIL_HEAD_DIM ** -0.5 * LOG2E
    mla_scale = (MLA_NOPE + MLA_ROPE) ** -0.5 * LOG2E
    tab_dil = jnp.concatenate([_rope_tables(pos, ROPE_THETA, PARTIAL_ROT, DIL_HEAD_DIM, 0, dil_scale),
                               _rope_tables(pos, ROPE_THETA, PARTIAL_ROT, DIL_HEAD_DIM, 0, 1.0)], axis=1)
    tab_mla = jnp.concatenate([_rope_tables(pos, MLA_ROPE_THETA, MLA_ROPE, MLA_SLOT, MLA_NOPE, mla_scale),
                               _rope_tables(pos, MLA_ROPE_THETA, MLA_ROPE, MLA_SLOT, MLA_NOPE, 1.0)], axis=1)

    xc = x.reshape(T, D)
    for i in range(DEPTH):
        w_in_p = _prep_w_in(w_in[i])
        wq = _prep_w_q(w_q_up[i])
        wk, wv = _prep_w_kv(w_kv_up[i])
        wr = jnp.zeros((D, 128), F32).at[:, :N_GROUPS].set(w_router_grp[i]).at[:, N_GROUPS:N_GROUPS + N_EXPERTS].set(
            w_router_exp[i]).astype(BF16)
        br = jnp.zeros((1, 128), F32).at[0, :N_GROUPS].set(b_router_grp[i]).at[0, N_GROUPS:N_GROUPS + N_EXPERTS].set(
            b_router_exp[i].reshape(N_EXPERTS))

        main, d0, d1, d2 = _in_proj(xc, g_mix[i].reshape(1, D), w_in_p, tab_dil, B, S, TM_IN)
        q, k, v = _mla_qkv(main, g_q_lat[i].reshape(1, -1), g_kv_lat[i].reshape(1, -1), wq, wk, wv, tab_mla, TM_QKV)
        o_a = _mla_attn(q, k, v, B, S, TQ)
        ogs, lss = [], []
        for (_, dil), qkv in zip(DIL_PATTERN, (d0, d1, d2)):
            og, ls = _dil_attn(qkv, B, dil, S // dil)
            ogs.append(og)
            lss.append(ls)
        x1, h2, lg = _post(xc, o_a, ogs, lss, main, w_branch_a[i].astype(BF16), w_branch_b[i].astype(BF16),
                           w_out[i].astype(BF16), g_ffn[i].reshape(1, D), wr, br, B, S, TM_POST)
        ri, rw, cnt = _route(lg, TM_ROUTE)
        dest, n_rows, block_e, nvb = _dispatch_plan(ri, cnt, T, TM_ROUTE)
        xs = _sc_gather_rows(h2.reshape(T, ROW_TILE, 128), dest, n_out=n_rows)
        ys = _expert_ffn(block_e, nvb, xs.reshape(-1, 128), w_exp_gate, w_exp_up, w_exp_down, i)
        yg = _sc_gather_rows(ys.reshape(-1, ROW_TILE, 128), dest)
        xc = _combine_ple(yg.reshape(-1, 128), x1, rw, p.reshape(DEPTH * T, PLE_DIM), g_ple[i].reshape(1, D),
                          w_ple_gate[i].astype(BF16), w_ple_proj[i].astype(BF16), g_final.reshape(1, D), TM_COMB, i)
    return xc.reshape(B, S, D)
```

```python
import functools
import math

import jax
import jax.numpy as jnp
from jax import lax
from jax.experimental import pallas as pl
from jax.experimental.pallas import tpu as pltpu
from jax.experimental.pallas import tpu_sc as plsc

F32 = jnp.float32
BF16 = jnp.bfloat16

D_MODEL = 1024
DEPTH = 4
RMS_EPS = 1e-6
NEG = -1e30
LOG2E = math.log2(math.e)

MLA_HEADS = 16
MLA_Q_LORA = 512
MLA_KV_LORA = 256
MLA_NOPE = 64
MLA_ROPE = 32
MLA_V = 64
MLA_ROPE_THETA = 10000.0
MLA_SLOT = 128

DIL_PATTERN = ((128, 1), (512, 4), (2048, 16))
DIL_GROUPS = 3
DIL_HEADS = 4
DIL_HEAD_DIM = 64
DIL_W = DIL_HEADS * DIL_HEAD_DIM
DIL_BLK = 128
ROPE_THETA = 500000.0
PARTIAL_ROT = DIL_HEAD_DIM // 4

N_GROUPS = 8
EXPERTS_PER_GROUP = 8
N_EXPERTS = 64
TOP_K = 2
EXPERT_FF = 256
PLE_DIM = 256

IN_TN = 768
MAIN_COLS = 3072
IN_COLS_PAD = MAIN_COLS + DIL_GROUPS * 3 * DIL_W
N_MAIN_TILES = MAIN_COLS // IN_TN

FFN_BM = 256
ROW_TILE = D_MODEL // 256
U32 = jnp.uint32
VMEM_LIMIT = 48 * 1024 * 1024

SC_CORES = 2
SC_SUBCORES = 16
SC_LANES = 16
SC_CHUNK = 64


def _cparams(sem, bounds_checks=True):
    return pltpu.CompilerParams(dimension_semantics=sem, vmem_limit_bytes=VMEM_LIMIT,
                                disable_bounds_checks=not bounds_checks)


def _rms(x, g):
    return x * lax.rsqrt(jnp.mean(x * x, axis=-1, keepdims=True) + RMS_EPS) * g


def _rope128(x, c, s_up, s_dn, half):
    return x * c + pltpu.roll(x, half, 1) * s_up + pltpu.roll(x, 128 - half, 1) * s_dn


def _in_kernel(x_ref, g_ref, w_ref, tab_ref, main_ref, d0_ref, d1_ref, d2_ref, xn_ref, acc_ref):
    j = pl.program_id(1)

    @pl.when(j == 0)
    def _():
        xn_ref[...] = _rms(x_ref[...], g_ref[...]).astype(BF16)

    acc = jnp.dot(xn_ref[...], w_ref[...], preferred_element_type=F32)

    @pl.when(j < N_MAIN_TILES)
    def _():
        main_ref[...] = acc.astype(BF16)

    def dil_tile(out_ref, dil):
        for c in range(4):
            t0 = 0 if c < 2 else 3
            x = acc[:, c * 128:(c + 1) * 128]
            acc_ref[c] = _rope128(
                x, tab_ref[:, t0 * 128:(t0 + 1) * 128], tab_ref[:, (t0 + 1) * 128:(t0 + 2) * 128],
                tab_ref[:, (t0 + 2) * 128:(t0 + 3) * 128], PARTIAL_ROT // 2)
        for c in range(4, 6):
            acc_ref[c] = acc[:, c * 128:(c + 1) * 128]
        rows = acc_ref.shape[1] // dil
        for r in range(dil):
            for c in range(6):
                out_ref[0, r, :, c * 128:(c + 1) * 128] = acc_ref[c, pl.ds(r, rows, stride=dil), :].astype(BF16)

    for gi, (_, dil) in enumerate(DIL_PATTERN):
        pl.when(j == N_MAIN_TILES + gi)(functools.partial(dil_tile, (d0_ref, d1_ref, d2_ref)[gi], dil))


def _in_proj(x2d, g, w, tab, B, S, tm):
    T = x2d.shape[0]
    nt = S // tm
    dil_shapes = [jax.ShapeDtypeStruct((B, d, S // d, 3 * DIL_W), BF16) for _, d in DIL_PATTERN]
    dil_specs = [pl.BlockSpec((1, d, tm // d, 3 * DIL_W), lambda i, j, nt=nt: (i // nt, 0, i % nt, 0))
                 for _, d in DIL_PATTERN]
    return pl.pallas_call(
        _in_kernel,
        grid=(T // tm, IN_COLS_PAD // IN_TN),
        in_specs=[
            pl.BlockSpec((tm, D_MODEL), lambda i, j: (i, 0)),
            pl.BlockSpec((1, D_MODEL), lambda i, j: (0, 0)),
            pl.BlockSpec((D_MODEL, IN_TN), lambda i, j: (0, j)),
            pl.BlockSpec((tm, 6 * 128), lambda i, j: (i, 0)),
        ],
        out_specs=[pl.BlockSpec((tm, IN_TN), lambda i, j: (i, jnp.minimum(j, N_MAIN_TILES - 1)))] + dil_specs,
        out_shape=[jax.ShapeDtypeStruct((T, MAIN_COLS), BF16)] + dil_shapes,
        scratch_shapes=[pltpu.VMEM((tm, D_MODEL), BF16), pltpu.VMEM((IN_TN // 128, tm, 128), F32)],
        compiler_params=_cparams(("parallel", "arbitrary")),
        name="in_proj",
    )(x2d, g, w, tab)


def _qkv_kernel(cq_ref, ckv_ref, kpe_ref, gq_ref, gkv_ref, wq_ref, wk_ref, wv_ref, tab_ref,
                q_ref, k_ref, v_ref):
    qn = _rms(cq_ref[...].astype(F32), gq_ref[...]).astype(BF16)
    kvn = _rms(ckv_ref[...].astype(F32), gkv_ref[...]).astype(BF16)
    qacc = jnp.dot(qn, wq_ref[...], preferred_element_type=F32)
    kacc = jnp.dot(kvn, wk_ref[...], preferred_element_type=F32)
    vacc = jnp.dot(kvn, wv_ref[...], preferred_element_type=F32)
    lane_v = lax.broadcasted_iota(jnp.int32, vacc.shape, 1)
    v_ref[...] = jnp.where(lane_v % MLA_SLOT < MLA_V, vacc, 1.0).astype(BF16)
    half = MLA_ROPE // 2
    tq = [tab_ref[:, t * 128:(t + 1) * 128] for t in range(3)]
    tk = [tab_ref[:, t * 128:(t + 1) * 128] for t in range(3, 6)]
    krot = _rope128(kpe_ref[:, 0:128].astype(F32), tk[0], tk[1], tk[2], half)
    for h in range(MLA_HEADS):
        sl = slice(h * MLA_SLOT, (h + 1) * MLA_SLOT)
        q_ref[:, sl] = _rope128(qacc[:, sl], tq[0], tq[1], tq[2], half).astype(BF16)
        k_ref[:, sl] = (kacc[:, sl] + krot).astype(BF16)


def _mla_qkv(main, gq, gkv, wq, wk, wv, tab, tm):
    T = main.shape[0]
    const = lambda i: (0, 0)
    return pl.pallas_call(
        _qkv_kernel,
        grid=(T // tm,),
        in_specs=[
            pl.BlockSpec((tm, MLA_Q_LORA), lambda i: (i, 2048 // MLA_Q_LORA)),
            pl.BlockSpec((tm, MLA_KV_LORA), lambda i: (i, 2560 // MLA_KV_LORA)),
            pl.BlockSpec((tm, 256), lambda i: (i, 2816 // 256)),
            pl.BlockSpec((1, MLA_Q_LORA), const),
            pl.BlockSpec((1, MLA_KV_LORA), const),
            pl.BlockSpec((MLA_Q_LORA, MLA_HEADS * MLA_SLOT), const),
            pl.BlockSpec((MLA_KV_LORA, MLA_HEADS * MLA_SLOT), const),
            pl.BlockSpec((MLA_KV_LORA, MLA_HEADS * MLA_SLOT), const),
            pl.BlockSpec((tm, 6 * 128), lambda i: (i, 0)),
        ],
        out_specs=[
            pl.BlockSpec((tm, MLA_HEADS * MLA_SLOT), lambda i: (i, 0)),
            pl.BlockSpec((tm, MLA_HEADS * MLA_SLOT), lambda i: (i, 0)),
            pl.BlockSpec((tm, MLA_HEADS * MLA_SLOT), lambda i: (i, 0)),
        ],
        out_shape=[
            jax.ShapeDtypeStruct((T, MLA_HEADS * MLA_SLOT), BF16),
            jax.ShapeDtypeStruct((T, MLA_HEADS * MLA_SLOT), BF16),
            jax.ShapeDtypeStruct((T, MLA_HEADS * MLA_SLOT), BF16),
        ],
        compiler_params=_cparams(("parallel",)),
        name="mla_qkv",
    )(main, main, main, gq, gkv, wq, wk, wv, tab)


def _mla_attn_kernel(qa_ref, qb_ref, k_ref, v_ref, o_ref, m_ref, acc_ref, *, tq):
    p_id = pl.program_id(2)
    nq = k_ref.shape[1] // tq
    pair_w = 2 * MLA_SLOT
    lane = lax.broadcasted_iota(jnp.int32, (tq, pair_w), 1)
    qh = []
    for q_ref in (qa_ref, qb_ref):
        qp = q_ref[0]
        zero = jnp.zeros_like(qp)
        qh.append([jnp.where(lane < MLA_SLOT, qp, zero), jnp.where(lane >= MLA_SLOT, qp, zero)])
    hq = tq // 2
    dn = (((1,), (1,)), ((), ()))

    def step(blk, start, width, r0, mask, first=False):
        kb = k_ref[0, pl.ds(start, width), :]
        vb = v_ref[0, pl.ds(start, width), :]
        for h in range(2):
            s = lax.dot_general(qh[blk][h][r0:], kb, dn, preferred_element_type=F32)
            if mask is not None:
                s = jnp.where(mask, s, NEG)
            m_cur = jnp.max(s, axis=-1, keepdims=True)
            if first:
                m_new = jnp.broadcast_to(m_cur, (tq - r0, 128))
            else:
                m_prev = m_ref[blk, h, r0:, :]
                m_new = jnp.maximum(m_prev, m_cur)
            p = jnp.exp2(s - jnp.concatenate([m_new] * (width // 128), axis=-1))
            pv = jnp.dot(p.astype(BF16), vb[:, h * MLA_SLOT:(h + 1) * MLA_SLOT], preferred_element_type=F32)
            if first:
                acc_ref[blk, h, r0:, :] = pv
            else:
                acc_ref[blk, h, r0:, :] = jnp.exp2(m_prev - m_new) * acc_ref[blk, h, r0:, :] + pv
            m_ref[blk, h, r0:, :] = m_new

    def causal(rows):
        return lax.broadcasted_iota(jnp.int32, (rows, hq), 1) <= lax.broadcasted_iota(jnp.int32, (rows, hq), 0)

    def q_blocks(n_a):
        n_b = nq - 1 - n_a
        for blk, n_full in ((1, n_b), (0, n_a)):
            for j in range(n_full):
                step(blk, j * tq, tq, 0, None, first=j == 0)
            step(blk, n_full * tq, hq, 0, causal(tq), first=n_full == 0)
            step(blk, n_full * tq + hq, hq, hq, causal(hq))

    for n_a in range(nq // 2):
        pl.when(p_id == n_a)(functools.partial(q_blocks, n_a))
    lane_o = lax.broadcasted_iota(jnp.int32, (tq, 2 * MLA_V), 1)
    for blk in range(2):
        outs = []
        for h in range(2):
            a = acc_ref[blk, h]
            outs.append(a / pltpu.roll(a, MLA_V, 1))
        o_ref[0, blk, 0] = jnp.where(lane_o < MLA_V, outs[0], pltpu.roll(outs[1], MLA_V, 1)).astype(BF16)


def _mla_attn(q, k, v, B, S, tq):
    nq = S // tq
    q = q.reshape(B, S, MLA_HEADS * MLA_SLOT)
    k = k.reshape(B, S, MLA_HEADS * MLA_SLOT)
    v = v.reshape(B, S, MLA_HEADS * MLA_SLOT)
    return pl.pallas_call(
        functools.partial(_mla_attn_kernel, tq=tq),
        grid=(B, MLA_HEADS // 2, nq // 2),
        in_specs=[
            pl.BlockSpec((1, tq, 2 * MLA_SLOT), lambda b, h, p: (b, p, h)),
            pl.BlockSpec((1, tq, 2 * MLA_SLOT), lambda b, h, p: (b, nq - 1 - p, h)),
            pl.BlockSpec((1, S, 2 * MLA_SLOT), lambda b, h, p: (b, 0, h)),
            pl.BlockSpec((1, S, 2 * MLA_SLOT), lambda b, h, p: (b, 0, h)),
        ],
        out_specs=pl.BlockSpec((1, 2, 1, tq, 2 * MLA_V), lambda b, h, p: (b, 0, p, 0, h)),
        out_shape=jax.ShapeDtypeStruct((B, 2, nq // 2, tq, MLA_HEADS * MLA_V), BF16),
        scratch_shapes=[pltpu.VMEM((2, 2, tq, 128), F32), pltpu.VMEM((2, 2, tq, MLA_SLOT), F32)],
        compiler_params=_cparams(("parallel", "parallel", "arbitrary")),
        name="mla_attn",
    )(q, q, k, v)


def _dil_kernel(qkv_ref, o_ref, lse_ref, bias_ref, *, nb):
    dil = qkv_ref.shape[1]
    width = 2 * DIL_BLK if nb > 1 else DIL_BLK
    row = lax.broadcasted_iota(jnp.int32, (DIL_BLK, width), 0)
    col = lax.broadcasted_iota(jnp.int32, (DIL_BLK, width), 1)
    bias_ref[0] = jnp.where(col <= row, 0.0, NEG)
    if nb > 1:
        later = jnp.logical_or(jnp.logical_and(col >= DIL_BLK, col - DIL_BLK <= row),
                               jnp.logical_and(col < DIL_BLK, col >= row))
        bias_ref[1] = jnp.where(later, 0.0, NEG)
    head_of_lane = lax.broadcasted_iota(jnp.int32, (DIL_BLK, DIL_W), 1) // DIL_HEAD_DIM
    dn = (((1,), (1,)), ((), ()))

    def by_head(parts):
        out = parts[DIL_HEADS - 1]
        for h in range(DIL_HEADS - 2, -1, -1):
            out = jnp.where(head_of_lane == h, parts[h], out)
        return out

    def unit(u):
        r = u // nb
        n = u % nb
        q0 = pl.multiple_of(n * DIL_BLK, DIL_BLK)
        q = qkv_ref[0, r, pl.ds(q0, DIL_BLK), 0:DIL_W]
        if nb > 1:
            w0 = pl.multiple_of(jnp.maximum(n - 1, 0) * DIL_BLK, DIL_BLK)
            bias = bias_ref[jnp.minimum(n, 1)]
        else:
            w0 = 0
            bias = bias_ref[0]
        kw = qkv_ref[0, r, pl.ds(w0, width), DIL_W:2 * DIL_W]
        vw = qkv_ref[0, r, pl.ds(w0, width), 2 * DIL_W:3 * DIL_W]
        zero = jnp.zeros_like(q)
        qs = jnp.concatenate([jnp.where(head_of_lane == h, q, zero) for h in range(DIL_HEADS)], axis=0)
        s = lax.dot_general(qs, kw, dn, preferred_element_type=F32)
        s = (s.reshape(DIL_HEADS, DIL_BLK, width) + bias[None]).reshape(DIL_HEADS * DIL_BLK, width)
        m = jnp.max(s, axis=-1, keepdims=True)
        e = jnp.exp2(s - m)
        den = jnp.sum(e, axis=-1, keepdims=True)
        pv = jnp.dot(e.astype(BF16), vw, preferred_element_type=F32)
        lse = m + jnp.log2(den)
        blk = lambda t, h: t[h * DIL_BLK:(h + 1) * DIL_BLK]
        o = by_head([blk(pv, h) for h in range(DIL_HEADS)]) / by_head(
            [jnp.broadcast_to(blk(den, h), (DIL_BLK, DIL_W)) for h in range(DIL_HEADS)])
        o_ref[0, r, pl.ds(q0, DIL_BLK), :] = o.astype(BF16)
        lse_ref[0, r, pl.ds(q0, DIL_BLK), :] = by_head(
            [jnp.broadcast_to(blk(lse, h), (DIL_BLK, DIL_W)) for h in range(DIL_HEADS)])

    def body(t, carry):
        unit(2 * t)
        unit(2 * t + 1)
        return carry

    lax.fori_loop(0, dil * nb // 2, body, 0)


def _dil_attn(qkv, B, dil, L):
    nb = L // DIL_BLK
    return pl.pallas_call(
        functools.partial(_dil_kernel, nb=nb),
        grid=(B,),
        in_specs=[pl.BlockSpec((1, dil, L, 3 * DIL_W), lambda b: (b, 0, 0, 0))],
        out_specs=[
            pl.BlockSpec((1, dil, L, DIL_W), lambda b: (b, 0, 0, 0)),
            pl.BlockSpec((1, dil, L, DIL_W), lambda b: (b, 0, 0, 0)),
        ],
        out_shape=[
            jax.ShapeDtypeStruct((B, dil, L, DIL_W), BF16),
            jax.ShapeDtypeStruct((B, dil, L, DIL_W), F32),
        ],
        scratch_shapes=[pltpu.VMEM((2, DIL_BLK, 2 * DIL_BLK if nb > 1 else DIL_BLK), F32)],
        compiler_params=_cparams(("parallel",)),
        name=f"dil_attn_d{dil}",
    )(qkv)


def _post_kernel(x_ref, oa_ref, og0_ref, og1_ref, og2_ref, ls0_ref, ls1_ref, ls2_ref, gates_ref,
                 wa_ref, wb_ref, wo_ref, gffn_ref, wr_ref, br_ref,
                 x1_ref, h2_ref, lg_ref, ob_ref, *, tm):
    ls =[r[0] for r in (ls0_ref, ls1_ref, ls2_ref)]
    og = [r[0] for r in (og0_ref, og1_ref, og2_ref)]
    for gi, (_, dil) in enumerate(DIL_PATTERN):
        rows = tm // dil
        for r in range(dil):
            for c in range(2):
                sl = slice(c * 128, (c + 1) * 128)
                ob_ref[2 * gi + c, pl.ds(r, rows, stride=dil), :] = og[gi][r][:, sl].astype(F32)
                ob_ref[6 + 2 * gi + c, pl.ds(r, rows, stride=dil), :] = ls[gi][r][:, sl]

    def tok_major(k):
        return jnp.concatenate([ob_ref[2 * k], ob_ref[2 * k + 1]], axis=-1)

    l0, l1, l2 = tok_major(3), tok_major(4), tok_major(5)
    mx = jnp.maximum(jnp.maximum(l0, l1), l2)
    w0, w1, w2 = jnp.exp2(l0 - mx), jnp.exp2(l1 - mx), jnp.exp2(l2 - mx)
    ob = (w0 * tok_major(0) + w1 * tok_major(1) + w2 * tok_major(2)) / (w0 + w1 + w2)
    ya = jnp.dot(oa_ref[0, 0, 0], wa_ref[...], preferred_element_type=F32)
    yb = jnp.dot(ob.astype(BF16), wb_ref[...], preferred_element_type=F32)
    merged = (jax.nn.sigmoid(gates_ref[:, 0:D_MODEL].astype(F32)) * ya
              + jax.nn.sigmoid(gates_ref[:, D_MODEL:2 * D_MODEL].astype(F32)) * yb)
    x1 = x_ref[...] + jnp.dot(merged.astype(BF16), wo_ref[...], preferred_element_type=F32)
    x1_ref[...] = x1
    h2 = _rms(x1, gffn_ref[...])
    _rows_to_tiles(h2_ref, h2)
    lg_ref[...] = jnp.dot(h2.astype(BF16), wr_ref[...], preferred_element_type=F32) + br_ref[...]


def _post(x2d, oa, ogs, lss, main, wa, wb, wo, gffn, wr, br, B, S, tm):
    T = x2d.shape[0]
    nt = S // tm
    const = lambda i: (0, 0)
    res_specs = [pl.BlockSpec((1, d, tm // d, DIL_W), lambda i, nt=nt: (i // nt, 0, i % nt, 0))
                 for _, d in DIL_PATTERN]
    assert oa.shape[3] == tm and oa.shape[2] * 2 == nt

    def oa_index(i):
        it = i % nt
        late = it >= nt // 2
        return (i // nt, late.astype(jnp.int32), jnp.where(late, nt - 1 - it, it), 0, 0)

    return pl.pallas_call(
        functools.partial(_post_kernel, tm=tm),
        grid=(T // tm,),
        in_specs=[pl.BlockSpec((tm, D_MODEL), lambda i: (i, 0)),
                  pl.BlockSpec((1, 1, 1, tm, D_MODEL), oa_index)]
                 + res_specs + res_specs
                 + [pl.BlockSpec((tm, 2 * D_MODEL), lambda i: (i, 0)),
                    pl.BlockSpec((D_MODEL, D_MODEL), const),
                    pl.BlockSpec((DIL_W, D_MODEL), const),
                    pl.BlockSpec((D_MODEL, D_MODEL), const),
                    pl.BlockSpec((1, D_MODEL), const),
                    pl.BlockSpec((D_MODEL, 128), const),
                    pl.BlockSpec((1, 128), const)],
        out_specs=[pl.BlockSpec((tm, D_MODEL), lambda i: (i, 0)),
                   pl.BlockSpec((tm * ROW_TILE, 128), lambda i: (i, 0)),
                   pl.BlockSpec((tm, 128), lambda i: (i, 0))],
        out_shape=[jax.ShapeDtypeStruct((T, D_MODEL), F32),
                   jax.ShapeDtypeStruct((T * ROW_TILE, 128), U32),
                   jax.ShapeDtypeStruct((T, 128), F32)],
        scratch_shapes=[pltpu.VMEM((12, tm, 128), F32)],
        compiler_params=_cparams(("parallel",)),
        name="post_attn",
    )(x2d, oa, *ogs, *lss, main, wa, wb, wo, gffn, wr, br)


def _route_kernel(lg_ref, ri_ref, rw_ref, cnt_ref, carry_ref, *, tm):
    i = pl.program_id(0)

    @pl.when(i == 0)
    def _():
        carry_ref[...] = jnp.zeros(carry_ref.shape, F32)

    lg = lg_ref[...]
    lane = lax.broadcasted_iota(jnp.int32, lg.shape, 1)
    lane_f = lane.astype(F32)
    ninf = jnp.float32(-jnp.inf)

    def first_max(vals):
        vmax = jnp.max(vals, axis=-1, keepdims=True)
        idx = jnp.min(jnp.where(vals == vmax, lane_f, 128.0), axis=-1, keepdims=True)
        return vmax, idx.astype(jnp.int32)

    gl = jnp.where(lane < N_GROUPS, lg, ninf)
    gmax, g_sel = first_max(gl)
    p_g = 1.0 / jnp.sum(jnp.exp(gl - gmax), axis=-1, keepdims=True)
    lo = N_GROUPS + g_sel * EXPERTS_PER_GROUP
    el = jnp.where(jnp.logical_and(lane >= lo, lane < lo + EXPERTS_PER_GROUP), lg, ninf)
    v0, i0 = first_max(el)
    v1, i1 = first_max(jnp.where(lane == i0, ninf, el))
    t = jnp.exp(v1 - v0)
    w0 = p_g / (1.0 + t)
    w1 = p_g * t / (1.0 + t)
    e0 = i0 - N_GROUPS
    e1 = i1 - N_GROUPS
    hit0 = lane == e0
    hit1 = lane == e1
    oh = jnp.where(jnp.logical_or(hit0, hit1), 1.0, 0.0).astype(F32)
    r_i = lax.broadcasted_iota(jnp.int32, (tm, tm), 0)
    c_i = lax.broadcasted_iota(jnp.int32, (tm, tm), 1)
    lower = jnp.where(c_i < r_i, 1.0, 0.0).astype(BF16)
    excl = jnp.dot(lower, oh.astype(BF16), preferred_element_type=F32) + carry_ref[...]
    r0 = jnp.sum(jnp.where(hit0, excl, 0.0), axis=-1, keepdims=True).astype(jnp.int32)
    r1 = jnp.sum(jnp.where(hit1, excl, 0.0), axis=-1, keepdims=True).astype(jnp.int32)
    carry_ref[...] = carry_ref[...] + jnp.sum(oh, axis=0, keepdims=True)
    zi = jnp.zeros(lg.shape, jnp.int32)
    ri_ref[...] = jnp.where(lane == 0, e0, jnp.where(lane == 1, e1, jnp.where(lane == 2, r0, jnp.where(lane == 3, r1, zi))))
    rw_ref[...] = jnp.where(lane == 0, w0, jnp.where(lane == 1, w1, jnp.zeros(lg.shape, F32)))
    cnt_ref[...] = carry_ref[...]


def _route(lg, tm):
    T = lg.shape[0]
    return pl.pallas_call(
        functools.partial(_route_kernel, tm=tm),
        grid=(T // tm,),
        in_specs=[pl.BlockSpec((tm, 128), lambda i: (i, 0))],
        out_specs=[pl.BlockSpec((tm, 128), lambda i: (i, 0)),
                   pl.BlockSpec((tm, 128), lambda i: (i, 0)),
                   pl.BlockSpec((1, 128), lambda i: (0, 0))],
        out_shape=[jax.ShapeDtypeStruct((T, 128), jnp.int32),
                   jax.ShapeDtypeStruct((T, 128), F32),
                   jax.ShapeDtypeStruct((1, 128), F32)],
        scratch_shapes=[pltpu.VMEM((1, 128), F32)],
        compiler_params=_cparams(("arbitrary",)),
        name="route",
    )(lg)


def _sc_gather_rows(table, idx, n_out=None):
    inverse = n_out is not None
    n_src = idx.shape[0]
    n = n_out if inverse else n_src
    n_workers = SC_CORES * SC_SUBCORES
    per_w = n // n_workers
    n_chunks = per_w // SC_CHUNK
    assert per_w * n_workers == n and n_chunks * SC_CHUNK == per_w and n_chunks % 2 == 0
    mesh = plsc.VectorSubcoreMesh(core_axis_name="c", subcore_axis_name="s",
                                  num_cores=SC_CORES, num_subcores=SC_SUBCORES)

    @functools.partial(
        pl.kernel, mesh=mesh,
        out_type=jax.ShapeDtypeStruct((n,) + table.shape[1:], table.dtype),
        scratch_types=[pltpu.VMEM((per_w,), jnp.int32),
                       pltpu.VMEM((SC_CHUNK,) + table.shape[1:], table.dtype),
                       pltpu.VMEM((SC_CHUNK,) + table.shape[1:], table.dtype),
                       pltpu.SemaphoreType.DMA, pltpu.SemaphoreType.DMA,
                       pltpu.VMEM((n_src if inverse else SC_LANES,), jnp.int32)],
        compiler_params=pltpu.CompilerParams(use_tc_tiling_on_sc=True, needs_layout_passes=not inverse),
        name="sc_dispatch_rows" if inverse else "sc_gather_rows",
    )
    def gather(table_hbm, idx_hbm, out_hbm, idx_v, rows_a, rows_b, sem_a, sem_b, map_v):
        wid = lax.axis_index("s") * SC_CORES + lax.axis_index("c")
        base = wid * per_w
        if inverse:
            pltpu.sync_copy(idx_hbm, map_v)
            lanes = lax.iota(jnp.int32, SC_LANES)

            @pl.loop(0, per_w // SC_LANES)
            def _(j):
                idx_v[pl.ds(j * SC_LANES, SC_LANES)] = lax.rem(base + j * SC_LANES + lanes, table.shape[0])

            @pl.loop(0, n_src // SC_LANES)
            def _(a):
                local = map_v[pl.ds(a * SC_LANES, SC_LANES)] - base
                mine = jnp.logical_and(local >= 0, local < per_w)
                plsc.store_scatter(idx_v, [local], lax.div(a * SC_LANES + lanes, TOP_K), mask=mine)
        else:
            pltpu.sync_copy(idx_hbm.at[pl.ds(base, per_w)], idx_v)

        def fetch(chunk, rows_v, sem):
            return pltpu.make_async_copy(table_hbm.at[idx_v.at[pl.ds(chunk * SC_CHUNK, SC_CHUNK)]], rows_v, sem)

        def flush(chunk, rows_v):
            pltpu.sync_copy(rows_v, out_hbm.at[pl.ds(base + chunk * SC_CHUNK, SC_CHUNK)])

        fetch(0, rows_a, sem_a).start()

        @pl.loop(0, n_chunks, step=2)
        def _(c):
            fetch(c + 1, rows_b, sem_b).start()
            fetch(c, rows_a, sem_a).wait()
            flush(c, rows_a)

            @pl.when(c + 2 < n_chunks)
            def _():
                fetch(c + 2, rows_a, sem_a).start()

            fetch(c + 1, rows_b, sem_b).wait()
            flush(c + 1, rows_b)

    return gather(table, idx)


def _rows_from_tiles(ref, first, n_rows, stride=ROW_TILE):
    words = [ref[pl.ds(first + c, n_rows, stride=stride), :] for c in range(ROW_TILE)]
    lo = [pltpu.bitcast(w << 16, F32) for w in words]
    hi = [pltpu.bitcast(w & U32(0xFFFF0000), F32) for w in words]
    return jnp.concatenate(lo + hi, axis=-1)


def _rows_to_tiles(ref, val):
    half = D_MODEL // 2

    def bits(x):
        return pltpu.bitcast(x.astype(BF16).astype(F32), U32)

    for c in range(ROW_TILE):
        lo = bits(val[:, c * 128:(c + 1) * 128])
        hi = bits(val[:, half + c * 128:half + (c + 1) * 128])
        ref[pl.ds(c, val.shape[0], stride=ROW_TILE), :] = hi | (lo >> 16)


def _ffn_kernel(be_ref, nvb_ref, x_ref, w1_ref, w3_ref, w2_ref, y_ref):
    b = pl.program_id(0)
    nvb = nvb_ref[0]

    @pl.when(b < nvb)
    def _():
        xb = _rows_from_tiles(x_ref, 0, FFN_BM).astype(BF16)
        h1 = jnp.dot(xb, w1_ref[0, 0].astype(BF16), preferred_element_type=F32)
        h3 = jnp.dot(xb, w3_ref[0, 0].astype(BF16), preferred_element_type=F32)
        a = (jax.nn.silu(h1) * h3).astype(BF16)
        _rows_to_tiles(y_ref, jnp.dot(a, w2_ref[0, 0].astype(BF16), preferred_element_type=F32))

    @pl.when(b >= nvb)
    def _():
        y_ref[...] = jnp.zeros(y_ref.shape, U32)


def _expert_ffn(block_e, nvb, xs, w1, w3, w2, layer):
    nb = block_e.shape[0]

    def used(b, nv):
        return jnp.minimum(b, jnp.maximum(nv[0] - 1, 0))

    return pl.pallas_call(
        _ffn_kernel,
        grid_spec=pltpu.PrefetchScalarGridSpec(
            num_scalar_prefetch=2,
            grid=(nb,),
            in_specs=[
                pl.BlockSpec((FFN_BM * ROW_TILE, 128), lambda b, be, nv: (used(b, nv), 0)),
                pl.BlockSpec((1, 1, D_MODEL, EXPERT_FF), lambda b, be, nv: (layer, be[b], 0, 0)),
                pl.BlockSpec((1, 1, D_MODEL, EXPERT_FF), lambda b, be, nv: (layer, be[b], 0, 0)),
                pl.BlockSpec((1, 1, EXPERT_FF, D_MODEL), lambda b, be, nv: (layer, be[b], 0, 0)),
            ],
            out_specs=pl.BlockSpec((FFN_BM * ROW_TILE, 128), lambda b, be, nv: (b, 0)),
        ),
        out_shape=jax.ShapeDtypeStruct((nb * FFN_BM * ROW_TILE, 128), U32),
        compiler_params=_cparams(("arbitrary",)),
        name="expert_ffn",
    )(block_e, nvb, xs, w1, w3, w2)


def _comb_kernel(yg_ref, x1_ref, rw_ref, p_ref, gple_ref, wpg_ref, wpp_ref, gout_ref, o_ref, *, tm, final):
    w = rw_ref[...]
    y0 = _rows_from_tiles(yg_ref, 0, tm, stride=TOP_K * ROW_TILE)
    y1 = _rows_from_tiles(yg_ref, ROW_TILE, tm, stride=TOP_K * ROW_TILE)
    x2 = x1_ref[...] + (y0 * w[:, 0:1] + y1 * w[:, 1:2])
    e = jnp.dot(p_ref[...].astype(BF16), wpp_ref[...], preferred_element_type=F32)
    gate = jax.nn.sigmoid(jnp.dot(_rms(x2, gple_ref[...]).astype(BF16), wpg_ref[...], preferred_element_type=F32))
    x3 = x2 + gate * e
    o_ref[...] = _rms(x3, gout_ref[...]) if final else x3


def _combine_ple(yg, x1, rw, p2d, gple, wpg, wpp, gout, tm, layer):
    T = x1.shape[0]
    nt = T // tm
    const = lambda i: (0, 0)
    return pl.pallas_call(
        functools.partial(_comb_kernel, tm=tm, final=layer == DEPTH - 1),
        grid=(nt,),
        in_specs=[
            pl.BlockSpec((tm * TOP_K * ROW_TILE, 128), lambda i: (i, 0)),
            pl.BlockSpec((tm, D_MODEL), lambda i: (i, 0)),
            pl.BlockSpec((tm, 128), lambda i: (i, 0)),
            pl.BlockSpec((tm, PLE_DIM), lambda i: (layer * nt + i, 0)),
            pl.BlockSpec((1, D_MODEL), const),
            pl.BlockSpec((D_MODEL, D_MODEL), const),
            pl.BlockSpec((PLE_DIM, D_MODEL), const),
            pl.BlockSpec((1, D_MODEL), const),
        ],
        out_specs=pl.BlockSpec((tm, D_MODEL), lambda i: (i, 0)),
        out_shape=jax.ShapeDtypeStruct((T, D_MODEL), F32),
        compiler_params=_cparams(("parallel",)),
        name="combine_ple",
    )(yg, x1, rw, p2d, gple, wpg, wpp, gout)


def _rope_tables(pos, theta, rot_dim, period, offset, scale):
    half = rot_dim // 2
    inv = jnp.float32(theta) ** (-jnp.arange(half, dtype=F32) * 2.0 / rot_dim)
    ang = pos.astype(F32)[:, None] * inv
    cos, sin = jnp.cos(ang), jnp.sin(ang)
    T = pos.shape[0]
    c = jnp.ones((T, period), F32).at[:, offset:offset + half].set(cos).at[:, offset + half:offset + rot_dim].set(cos)
    s_up = jnp.zeros((T, period), F32).at[:, offset + half:offset + rot_dim].set(sin)
    s_dn = jnp.zeros((T, period), F32).at[:, offset:offset + half].set(-sin)
    rep = 128 // period
    return jnp.concatenate([jnp.tile(t, (1, rep)) for t in (c, s_up, s_dn)], axis=1) * jnp.float32(scale)


def _prep_w_in(w):
    c_q = w[:, 0:512]
    c_kv = w[:, 512:768]
    k_pe = w[:, 768:800]
    dil = w[:, 800:800 + 2304]
    gates = w[:, 3104:5152]
    kslot = jnp.zeros((D_MODEL, 256), w.dtype).at[:, MLA_NOPE:MLA_NOPE + MLA_ROPE].set(k_pe)
    return jnp.concatenate([gates, c_q, c_kv, kslot, dil], axis=1).astype(BF16)


def _prep_w_q(w):
    w = w.reshape(MLA_Q_LORA, MLA_HEADS, MLA_NOPE + MLA_ROPE)
    w = jnp.pad(w, ((0, 0), (0, 0), (0, MLA_SLOT - MLA_NOPE - MLA_ROPE)))
    return w.reshape(MLA_Q_LORA, MLA_HEADS * MLA_SLOT).astype(BF16)


def _prep_w_kv(w):
    w = w.reshape(MLA_KV_LORA, MLA_HEADS, MLA_NOPE + MLA_V)
    wk = jnp.pad(w[:, :, :MLA_NOPE], ((0, 0), (0, 0), (0, MLA_SLOT - MLA_NOPE)))
    wv = jnp.pad(w[:, :, MLA_NOPE:], ((0, 0), (0, 0), (0, MLA_SLOT - MLA_V)))
    return (wk.reshape(MLA_KV_LORA, MLA_HEADS * MLA_SLOT).astype(BF16),
            wv.reshape(MLA_KV_LORA, MLA_HEADS * MLA_SLOT).astype(BF16))


def _dest_kernel(ri_ref, ps_ref, o_ref):
    ri = ri_ref[...].astype(F32)
    lane = lax.broadcasted_iota(jnp.int32, ri.shape, 1)
    ps = ps_ref[...]

    def col(k):
        return jnp.sum(jnp.where(lane == k, ri, 0.0), axis=-1, keepdims=True)

    def dest(k):
        start = jnp.sum(jnp.where(lane == col(k).astype(jnp.int32), ps, 0.0), axis=-1, keepdims=True)
        return (start + col(2 + k)).astype(jnp.int32)

    o_ref[...] = jnp.where(lane == 0, dest(0), jnp.where(lane == 1, dest(1), 0))


def _dest_rows(ri, pstarts, tm):
    T = ri.shape[0]
    ps = jnp.zeros((1, 128), F32).at[0, :N_EXPERTS].set(pstarts.astype(F32))
    return pl.pallas_call(
        _dest_kernel,
        grid=(T // tm,),
        in_specs=[pl.BlockSpec((tm, 128), lambda i: (i, 0)), pl.BlockSpec((1, 128), lambda i: (0, 0))],
        out_specs=pl.BlockSpec((tm, 128), lambda i: (i, 0)),
        out_shape=jax.ShapeDtypeStruct((T, 128), jnp.int32),
        compiler_params=_cparams(("parallel",)),
        name="dest_rows",
    )(ri, ps)


def _dispatch_plan(ri, cnt, T, tm):
    counts = cnt[0, :N_EXPERTS].astype(jnp.int32)
    pcounts = (counts + FFN_BM - 1) // FFN_BM * FFN_BM
    pends = jnp.cumsum(pcounts)
    pstarts = pends - pcounts
    dest = _dest_rows(ri, pstarts, tm)[:, 0:TOP_K]
    n_blocks = (T * TOP_K) // FFN_BM + N_EXPERTS
    block_e = jnp.minimum(jnp.searchsorted(pends, jnp.arange(n_blocks, dtype=jnp.int32) * FFN_BM, side='right'),
                          N_EXPERTS - 1).astype(jnp.int32)
    nvb = (pends[-1] // FFN_BM).astype(jnp.int32).reshape(1)
    return dest.reshape(T * TOP_K), n_blocks * FFN_BM, block_e, nvb


def kernel(x, p, positions, g_mix, w_in, g_q_lat, w_q_up, g_kv_lat, w_kv_up, w_branch_a, w_branch_b, w_out, g_ffn, w_router_grp, b_router_grp, w_router_exp, b_router_exp, w_exp_gate, w_exp_up, w_exp_down, g_ple, w_ple_gate, w_ple_proj, g_final):
    B, S, D = x.shape
    T = B * S
    TM_IN, TM_QKV, TQ, TM_POST, TM_ROUTE, TM_COMB = 512, 512, 512, 512, 512, 256

    pos = positions.reshape(T)
    dil_scale = DIL_HEAD_DIM ** -0.5 * LOG2E
    mla_scale = (MLA_NOPE + MLA_ROPE) ** -0.5 * LOG2E
    tab_dil = jnp.concatenate([_rope_tables(pos, ROPE_THETA, PARTIAL_ROT, DIL_HEAD_DIM, 0, dil_scale),
                               _rope_tables(pos, ROPE_THETA, PARTIAL_ROT, DIL_HEAD_DIM, 0, 1.0)], axis=1)
    tab_mla = jnp.concatenate([_rope_tables(pos, MLA_ROPE_THETA, MLA_ROPE, MLA_SLOT, MLA_NOPE, mla_scale),
                               _rope_tables(pos, MLA_ROPE_THETA, MLA_ROPE, MLA_SLOT, MLA_NOPE, 1.0)], axis=1)

    xc = x.reshape(T, D)
    for i in range(DEPTH):
        w_in_p = _prep_w_in(w_in[i])
        wq = _prep_w_q(w_q_up[i])
        wk, wv = _prep_w_kv(w_kv_up[i])
        wr = jnp.zeros((D, 128), F32).at[:, :N_GROUPS].set(w_router_grp[i]).at[:, N_GROUPS:N_GROUPS + N_EXPERTS].set(
            w_router_exp[i]).astype(BF16)
        br = jnp.zeros((1, 128), F32).at[0, :N_GROUPS].set(b_router_grp[i]).at[0, N_GROUPS:N_GROUPS + N_EXPERTS].set(
            b_router_exp[i].reshape(N_EXPERTS))

        main, d0, d1, d2 = _in_proj(xc, g_mix[i].reshape(1, D), w_in_p, tab_dil, B, S, TM_IN)
        q, k, v = _mla_qkv(main, g_q_lat[i].reshape(1, -1), g_kv_lat[i].reshape(1, -1), wq, wk, wv, tab_mla, TM_QKV)
        o_a = _mla_attn(q, k, v, B, S, TQ)
        ogs, lss = [], []
        for (_, dil), qkv in zip(DIL_PATTERN, (d0, d1, d2)):
            og, ls = _dil_attn(qkv, B, dil, S // dil)
            ogs.append(og)
            lss.append(ls)
        x1, h2, lg = _post(xc, o_a, ogs, lss, main, w_branch_a[i].astype(BF16), w_branch_b[i].astype(BF16),
                           w_out[i].astype(BF16), g_ffn[i].reshape(1, D), wr, br, B, S, TM_POST)
        ri, rw, cnt = _route(lg, TM_ROUTE)
        dest, n_rows, block_e, nvb = _dispatch_plan(ri, cnt, T, TM_ROUTE)
        xs = _sc_gather_rows(h2.reshape(T, ROW_TILE, 128), dest, n_out=n_rows)
        ys = _expert_ffn(block_e, nvb, xs.reshape(-1, 128), w_exp_gate, w_exp_up, w_exp_down, i)
        yg = _sc_gather_rows(ys.reshape(-1, ROW_TILE, 128), dest)
        xc = _combine_ple(yg.reshape(-1, 128), x1, rw, p.reshape(DEPTH * T, PLE_DIM), g_ple[i].reshape(1, D),
                          w_ple_gate[i].astype(BF16), w_ple_proj[i].astype(BF16), g_final.reshape(1, D), TM_COMB, i)
    return xc.reshape(B, S, D)
```

```python
import functools
import math

import jax
import jax.numpy as jnp
import numpy as np
from jax import lax
from jax.experimental import pallas as pl
from jax.experimental.pallas import tpu as pltpu
from jax.experimental.pallas import tpu_sc as plsc

F32 = jnp.float32
BF16 = jnp.bfloat16

D_MODEL = 1024
DEPTH = 4
RMS_EPS = 1e-6
NEG = -1e30
LOG2E = math.log2(math.e)

MLA_HEADS = 16
MLA_Q_LORA = 512
MLA_KV_LORA = 256
MLA_NOPE = 64
MLA_ROPE = 32
MLA_V = 64
MLA_ROPE_THETA = 10000.0
MLA_SLOT = 128

DIL_PATTERN = ((128, 1), (512, 4), (2048, 16))
DIL_GROUPS = 3
DIL_HEADS = 4
DIL_HEAD_DIM = 64
DIL_W = DIL_HEADS * DIL_HEAD_DIM
DIL_BLK = 128
DIL_UNROLL = 16
ROPE_THETA = 500000.0
PARTIAL_ROT = DIL_HEAD_DIM // 4

N_GROUPS = 8
EXPERTS_PER_GROUP = 8
N_EXPERTS = 64
TOP_K = 2
EXPERT_FF = 256
PLE_DIM = 256

IN_TN = 768
MAIN_COLS = 3072
IN_COLS_PAD = MAIN_COLS + DIL_GROUPS * 3 * DIL_W
N_MAIN_TILES = MAIN_COLS // IN_TN

FFN_BM = 256
ROW_TILE = D_MODEL // 256
U32 = jnp.uint32
VMEM_LIMIT = 48 * 1024 * 1024

SC_CORES = 2
SC_SUBCORES = 16
SC_LANES = 16
SC_CHUNK = 64


def _cparams(sem, bounds_checks=True):
    return pltpu.CompilerParams(dimension_semantics=sem, vmem_limit_bytes=VMEM_LIMIT,
                                disable_bounds_checks=not bounds_checks)


def _rms(x, g):
    return x * lax.rsqrt(jnp.mean(x * x, axis=-1, keepdims=True) + RMS_EPS) * g


def _rope128(x, c, s_up, s_dn, half):
    return x * c + pltpu.roll(x, half, 1) * s_up + pltpu.roll(x, 128 - half, 1) * s_dn


def _in_kernel(x_ref, g_ref, w_ref, tab_ref, main_ref, d0_ref, d1_ref, d2_ref, xn_ref, acc_ref):
    j = pl.program_id(1)

    @pl.when(j == 0)
    def _():
        xn_ref[...] = _rms(x_ref[...], g_ref[...]).astype(BF16)

    acc = jnp.dot(xn_ref[...], w_ref[...], preferred_element_type=F32)

    @pl.when(j < N_MAIN_TILES)
    def _():
        main_ref[...] = acc.astype(BF16)

    def dil_tile(out_ref, dil):
        for c in range(4):
            t0 = 0 if c < 2 else 3
            x = acc[:, c * 128:(c + 1) * 128]
            acc_ref[c] = _rope128(
                x, tab_ref[:, t0 * 128:(t0 + 1) * 128], tab_ref[:, (t0 + 1) * 128:(t0 + 2) * 128],
                tab_ref[:, (t0 + 2) * 128:(t0 + 3) * 128], PARTIAL_ROT // 2)
        for c in range(4, 6):
            acc_ref[c] = acc[:, c * 128:(c + 1) * 128]
        rows = acc_ref.shape[1] // dil
        for r in range(dil):
            for c in range(6):
                out_ref[0, r, :, c * 128:(c + 1) * 128] = acc_ref[c, pl.ds(r, rows, stride=dil), :].astype(BF16)

    for gi, (_, dil) in enumerate(DIL_PATTERN):
        pl.when(j == N_MAIN_TILES + gi)(functools.partial(dil_tile, (d0_ref, d1_ref, d2_ref)[gi], dil))


def _in_proj(x2d, g, w, tab, B, S, tm):
    T = x2d.shape[0]
    nt = S // tm
    dil_shapes = [jax.ShapeDtypeStruct((B, d, S // d, 3 * DIL_W), BF16) for _, d in DIL_PATTERN]
    dil_specs = [pl.BlockSpec((1, d, tm // d, 3 * DIL_W), lambda i, j, nt=nt: (i // nt, 0, i % nt, 0))
                 for _, d in DIL_PATTERN]
    return pl.pallas_call(
        _in_kernel,
        grid=(T // tm, IN_COLS_PAD // IN_TN),
        in_specs=[
            pl.BlockSpec((tm, D_MODEL), lambda i, j: (i, 0)),
            pl.BlockSpec((1, D_MODEL), lambda i, j: (0, 0)),
            pl.BlockSpec((D_MODEL, IN_TN), lambda i, j: (0, j)),
            pl.BlockSpec((tm, 6 * 128), lambda i, j: (i, 0)),
        ],
        out_specs=[pl.BlockSpec((tm, IN_TN), lambda i, j: (i, jnp.minimum(j, N_MAIN_TILES - 1)))] + dil_specs,
        out_shape=[jax.ShapeDtypeStruct((T, MAIN_COLS), BF16)] + dil_shapes,
        scratch_shapes=[pltpu.VMEM((tm, D_MODEL), BF16), pltpu.VMEM((IN_TN // 128, tm, 128), F32)],
        compiler_params=_cparams(("parallel", "arbitrary")),
        name="in_proj",
    )(x2d, g, w, tab)


def _qkv_kernel(cq_ref, ckv_ref, kpe_ref, gq_ref, gkv_ref, wq_ref, wk_ref, wv_ref, tab_ref,
                q_ref, k_ref, v_ref):
    qn = _rms(cq_ref[...].astype(F32), gq_ref[...]).astype(BF16)
    kvn = _rms(ckv_ref[...].astype(F32), gkv_ref[...]).astype(BF16)
    qacc = jnp.dot(qn, wq_ref[...], preferred_element_type=F32)
    kacc = jnp.dot(kvn, wk_ref[...], preferred_element_type=F32)
    vacc = jnp.dot(kvn, wv_ref[...], preferred_element_type=F32)
    lane_v = lax.broadcasted_iota(jnp.int32, vacc.shape, 1)
    v_ref[...] = jnp.where(lane_v % MLA_SLOT < MLA_V, vacc, 1.0).astype(BF16)
    half = MLA_ROPE // 2
    tq = [tab_ref[:, t * 128:(t + 1) * 128] for t in range(3)]
    tk = [tab_ref[:, t * 128:(t + 1) * 128] for t in range(3, 6)]
    krot = _rope128(kpe_ref[:, 0:128].astype(F32), tk[0], tk[1], tk[2], half)
    for h in range(MLA_HEADS):
        sl = slice(h * MLA_SLOT, (h + 1) * MLA_SLOT)
        q_ref[:, sl] = _rope128(qacc[:, sl], tq[0], tq[1], tq[2], half).astype(BF16)
        k_ref[:, sl] = (kacc[:, sl] + krot).astype(BF16)


def _mla_qkv(main, gq, gkv, wq, wk, wv, tab, tm):
    T = main.shape[0]
    const = lambda i: (0, 0)
    return pl.pallas_call(
        _qkv_kernel,
        grid=(T // tm,),
        in_specs=[
            pl.BlockSpec((tm, MLA_Q_LORA), lambda i: (i, 2048 // MLA_Q_LORA)),
            pl.BlockSpec((tm, MLA_KV_LORA), lambda i: (i, 2560 // MLA_KV_LORA)),
            pl.BlockSpec((tm, 256), lambda i: (i, 2816 // 256)),
            pl.BlockSpec((1, MLA_Q_LORA), const),
            pl.BlockSpec((1, MLA_KV_LORA), const),
            pl.BlockSpec((MLA_Q_LORA, MLA_HEADS * MLA_SLOT), const),
            pl.BlockSpec((MLA_KV_LORA, MLA_HEADS * MLA_SLOT), const),
            pl.BlockSpec((MLA_KV_LORA, MLA_HEADS * MLA_SLOT), const),
            pl.BlockSpec((tm, 6 * 128), lambda i: (i, 0)),
        ],
        out_specs=[
            pl.BlockSpec((tm, MLA_HEADS * MLA_SLOT), lambda i: (i, 0)),
            pl.BlockSpec((tm, MLA_HEADS * MLA_SLOT), lambda i: (i, 0)),
            pl.BlockSpec((tm, MLA_HEADS * MLA_SLOT), lambda i: (i, 0)),
        ],
        out_shape=[
            jax.ShapeDtypeStruct((T, MLA_HEADS * MLA_SLOT), BF16),
            jax.ShapeDtypeStruct((T, MLA_HEADS * MLA_SLOT), BF16),
            jax.ShapeDtypeStruct((T, MLA_HEADS * MLA_SLOT), BF16),
        ],
        compiler_params=_cparams(("parallel",)),
        name="mla_qkv",
    )(main, main, main, gq, gkv, wq, wk, wv, tab)


def _mla_attn_kernel(qa_ref, qb_ref, k_ref, v_ref, o_ref, m_ref, acc_ref, *, tq):
    p_id = pl.program_id(2)
    nq = k_ref.shape[1] // tq
    pair_w = 2 * MLA_SLOT
    lane = lax.broadcasted_iota(jnp.int32, (tq, pair_w), 1)
    qh = []
    for q_ref in (qa_ref, qb_ref):
        qp = q_ref[0]
        zero = jnp.zeros_like(qp)
        qh.append([jnp.where(lane < MLA_SLOT, qp, zero), jnp.where(lane >= MLA_SLOT, qp, zero)])
    hq = tq // 2
    dn = (((1,), (1,)), ((), ()))

    def step(blk, start, width, r0, mask, first=False):
        kb = k_ref[0, pl.ds(start, width), :]
        vb = v_ref[0, pl.ds(start, width), :]
        for h in range(2):
            s = lax.dot_general(qh[blk][h][r0:], kb, dn, preferred_element_type=F32)
            if mask is not None:
                s = jnp.where(mask, s, NEG)
            m_cur = jnp.max(s, axis=-1, keepdims=True)
            if first:
                m_new = jnp.broadcast_to(m_cur, (tq - r0, 128))
            else:
                m_prev = m_ref[blk, h, r0:, :]
                m_new = jnp.maximum(m_prev, m_cur)
            p = jnp.exp2(s - jnp.concatenate([m_new] * (width // 128), axis=-1))
            pv = jnp.dot(p.astype(BF16), vb[:, h * MLA_SLOT:(h + 1) * MLA_SLOT], preferred_element_type=F32)
            if first:
                acc_ref[blk, h, r0:, :] = pv
            else:
                acc_ref[blk, h, r0:, :] = jnp.exp2(m_prev - m_new) * acc_ref[blk, h, r0:, :] + pv
            m_ref[blk, h, r0:, :] = m_new

    def causal(rows):
        return lax.broadcasted_iota(jnp.int32, (rows, hq), 1) <= lax.broadcasted_iota(jnp.int32, (rows, hq), 0)

    def q_blocks(n_a):
        n_b = nq - 1 - n_a
        for blk, n_full in ((1, n_b), (0, n_a)):
            for j in range(n_full):
                step(blk, j * tq, tq, 0, None, first=j == 0)
            step(blk, n_full * tq, hq, 0, causal(tq), first=n_full == 0)
            step(blk, n_full * tq + hq, hq, hq, causal(hq))

    for n_a in range(nq // 2):
        pl.when(p_id == n_a)(functools.partial(q_blocks, n_a))
    lane_o = lax.broadcasted_iota(jnp.int32, (tq, 2 * MLA_V), 1)
    for blk in range(2):
        outs = []
        for h in range(2):
            a = acc_ref[blk, h]
            outs.append(a / pltpu.roll(a, MLA_V, 1))
        o_ref[0, blk, 0] = jnp.where(lane_o < MLA_V, outs[0], pltpu.roll(outs[1], MLA_V, 1)).astype(BF16)


def _mla_attn(q, k, v, B, S, tq):
    nq = S // tq
    q = q.reshape(B, S, MLA_HEADS * MLA_SLOT)
    k = k.reshape(B, S, MLA_HEADS * MLA_SLOT)
    v = v.reshape(B, S, MLA_HEADS * MLA_SLOT)
    return pl.pallas_call(
        functools.partial(_mla_attn_kernel, tq=tq),
        grid=(B, MLA_HEADS // 2, nq // 2),
        in_specs=[
            pl.BlockSpec((1, tq, 2 * MLA_SLOT), lambda b, h, p: (b, p, h)),
            pl.BlockSpec((1, tq, 2 * MLA_SLOT), lambda b, h, p: (b, nq - 1 - p, h)),
            pl.BlockSpec((1, S, 2 * MLA_SLOT), lambda b, h, p: (b, 0, h)),
            pl.BlockSpec((1, S, 2 * MLA_SLOT), lambda b, h, p: (b, 0, h)),
        ],
        out_specs=pl.BlockSpec((1, 2, 1, tq, 2 * MLA_V), lambda b, h, p: (b, 0, p, 0, h)),
        out_shape=jax.ShapeDtypeStruct((B, 2, nq // 2, tq, MLA_HEADS * MLA_V), BF16),
        scratch_shapes=[pltpu.VMEM((2, 2, tq, 128), F32), pltpu.VMEM((2, 2, tq, MLA_SLOT), F32)],
        compiler_params=_cparams(("parallel", "parallel", "arbitrary")),
        name="mla_attn",
    )(q, q, k, v)


def _dil_kernel(qkv_ref, o_ref, lse_ref, bias_ref, *, nb):
    dil = qkv_ref.shape[1]
    width = 2 * DIL_BLK if nb > 1 else DIL_BLK
    row = lax.broadcasted_iota(jnp.int32, (DIL_BLK, width), 0)
    col = lax.broadcasted_iota(jnp.int32, (DIL_BLK, width), 1)
    bias_ref[0] = jnp.where(col <= row, 0.0, NEG)
    if nb > 1:
        later = jnp.logical_or(jnp.logical_and(col >= DIL_BLK, col - DIL_BLK <= row),
                               jnp.logical_and(col < DIL_BLK, col >= row))
        bias_ref[1] = jnp.where(later, 0.0, NEG)
    head_of_lane = lax.broadcasted_iota(jnp.int32, (DIL_BLK, DIL_W), 1) // DIL_HEAD_DIM
    dn = (((1,), (1,)), ((), ()))

    def by_head(parts):
        out = parts[DIL_HEADS - 1]
        for h in range(DIL_HEADS - 2, -1, -1):
            out = jnp.where(head_of_lane == h, parts[h], out)
        return out

    def unit(u):
        r = u // nb
        n = u % nb
        q0 = pl.multiple_of(n * DIL_BLK, DIL_BLK)
        q = qkv_ref[0, r, pl.ds(q0, DIL_BLK), 0:DIL_W]
        if nb > 1:
            w0 = pl.multiple_of(jnp.maximum(n - 1, 0) * DIL_BLK, DIL_BLK)
            bias = bias_ref[jnp.minimum(n, 1)]
        else:
            w0 = 0
            bias = bias_ref[0]
        kw = qkv_ref[0, r, pl.ds(w0, width), DIL_W:2 * DIL_W]
        vw = qkv_ref[0, r, pl.ds(w0, width), 2 * DIL_W:3 * DIL_W]
        zero = jnp.zeros_like(q)
        qs = jnp.concatenate([jnp.where(head_of_lane == h, q, zero) for h in range(DIL_HEADS)], axis=0)
        s = lax.dot_general(qs, kw, dn, preferred_element_type=F32)
        s = (s.reshape(DIL_HEADS, DIL_BLK, width) + bias[None]).reshape(DIL_HEADS * DIL_BLK, width)
        m = jnp.max(s, axis=-1, keepdims=True)
        e = jnp.exp2(s - m)
        den = jnp.sum(e, axis=-1, keepdims=True)
        pv = jnp.dot(e.astype(BF16), vw, preferred_element_type=F32)
        lse = m + jnp.log2(den)
        blk = lambda t, h: t[h * DIL_BLK:(h + 1) * DIL_BLK]
        o = by_head([blk(pv, h) for h in range(DIL_HEADS)]) / by_head(
            [jnp.broadcast_to(blk(den, h), (DIL_BLK, DIL_W)) for h in range(DIL_HEADS)])
        o_ref[0, r, pl.ds(q0, DIL_BLK), :] = o.astype(BF16)
        lse_ref[0, r, pl.ds(q0, DIL_BLK), :] = by_head(
            [jnp.broadcast_to(blk(lse, h), (DIL_BLK, DIL_W)) for h in range(DIL_HEADS)])

    def body(t, carry):
        for u in range(DIL_UNROLL):
            unit(DIL_UNROLL * t + u)
        return carry

    lax.fori_loop(0, dil * nb // DIL_UNROLL, body, 0)


def _dil_attn(qkv, B, dil, L):
    nb = L // DIL_BLK
    return pl.pallas_call(
        functools.partial(_dil_kernel, nb=nb),
        grid=(B,),
        in_specs=[pl.BlockSpec((1, dil, L, 3 * DIL_W), lambda b: (b, 0, 0, 0))],
        out_specs=[
            pl.BlockSpec((1, dil, L, DIL_W), lambda b: (b, 0, 0, 0)),
            pl.BlockSpec((1, dil, L, DIL_W), lambda b: (b, 0, 0, 0)),
        ],
        out_shape=[
            jax.ShapeDtypeStruct((B, dil, L, DIL_W), BF16),
            jax.ShapeDtypeStruct((B, dil, L, DIL_W), F32),
        ],
        scratch_shapes=[pltpu.VMEM((2, DIL_BLK, 2 * DIL_BLK if nb > 1 else DIL_BLK), F32)],
        compiler_params=_cparams(("parallel",)),
        name=f"dil_attn_d{dil}",
    )(qkv)


def _post_kernel(x_ref, oa_ref, og0_ref, og1_ref, og2_ref, ls0_ref, ls1_ref, ls2_ref, gates_ref,
                 wa_ref, wb_ref, wo_ref, gffn_ref, wr_ref, br_ref,
                 x1_ref, h2_ref, lg_ref, ob_ref, *, tm):
    ls =[r[0] for r in (ls0_ref, ls1_ref, ls2_ref)]
    og = [r[0] for r in (og0_ref, og1_ref, og2_ref)]
    for gi, (_, dil) in enumerate(DIL_PATTERN):
        rows = tm // dil
        for r in range(dil):
            for c in range(2):
                sl = slice(c * 128, (c + 1) * 128)
                ob_ref[2 * gi + c, pl.ds(r, rows, stride=dil), :] = og[gi][r][:, sl].astype(F32)
                ob_ref[6 + 2 * gi + c, pl.ds(r, rows, stride=dil), :] = ls[gi][r][:, sl]

    def tok_major(k):
        return jnp.concatenate([ob_ref[2 * k], ob_ref[2 * k + 1]], axis=-1)

    l0, l1, l2 = tok_major(3), tok_major(4), tok_major(5)
    mx = jnp.maximum(jnp.maximum(l0, l1), l2)
    w0, w1, w2 = jnp.exp2(l0 - mx), jnp.exp2(l1 - mx), jnp.exp2(l2 - mx)
    ob = (w0 * tok_major(0) + w1 * tok_major(1) + w2 * tok_major(2)) / (w0 + w1 + w2)
    ya = jnp.dot(oa_ref[0, 0, 0], wa_ref[...], preferred_element_type=F32)
    yb = jnp.dot(ob.astype(BF16), wb_ref[...], preferred_element_type=F32)
    merged = (jax.nn.sigmoid(gates_ref[:, 0:D_MODEL].astype(F32)) * ya
              + jax.nn.sigmoid(gates_ref[:, D_MODEL:2 * D_MODEL].astype(F32)) * yb)
    x1 = x_ref[...] + jnp.dot(merged.astype(BF16), wo_ref[...], preferred_element_type=F32)
    x1_ref[...] = x1
    h2 = _rms(x1, gffn_ref[...])
    _rows_to_tiles(h2_ref, h2)
    lg_ref[...] = jnp.dot(h2.astype(BF16), wr_ref[...], preferred_element_type=F32) + br_ref[...]


def _post(x2d, oa, ogs, lss, main, wa, wb, wo, gffn, wr, br, B, S, tm):
    T = x2d.shape[0]
    nt = S // tm
    const = lambda i: (0, 0)
    res_specs = [pl.BlockSpec((1, d, tm // d, DIL_W), lambda i, nt=nt: (i // nt, 0, i % nt, 0))
                 for _, d in DIL_PATTERN]
    assert oa.shape[3] == tm and oa.shape[2] * 2 == nt

    def oa_index(i):
        it = i % nt
        late = it >= nt // 2
        return (i // nt, late.astype(jnp.int32), jnp.where(late, nt - 1 - it, it), 0, 0)

    return pl.pallas_call(
        functools.partial(_post_kernel, tm=tm),
        grid=(T // tm,),
        in_specs=[pl.BlockSpec((tm, D_MODEL), lambda i: (i, 0)),
                  pl.BlockSpec((1, 1, 1, tm, D_MODEL), oa_index)]
                 + res_specs + res_specs
                 + [pl.BlockSpec((tm, 2 * D_MODEL), lambda i: (i, 0)),
                    pl.BlockSpec((D_MODEL, D_MODEL), const),
                    pl.BlockSpec((DIL_W, D_MODEL), const),
                    pl.BlockSpec((D_MODEL, D_MODEL), const),
                    pl.BlockSpec((1, D_MODEL), const),
                    pl.BlockSpec((D_MODEL, 128), const),
                    pl.BlockSpec((1, 128), const)],
        out_specs=[pl.BlockSpec((tm, D_MODEL), lambda i: (i, 0)),
                   pl.BlockSpec((tm * ROW_TILE, 128), lambda i: (i, 0)),
                   pl.BlockSpec((tm, 128), lambda i: (i, 0))],
        out_shape=[jax.ShapeDtypeStruct((T, D_MODEL), F32),
                   jax.ShapeDtypeStruct((T * ROW_TILE, 128), U32),
                   jax.ShapeDtypeStruct((T, 128), F32)],
        scratch_shapes=[pltpu.VMEM((12, tm, 128), F32)],
        compiler_params=_cparams(("parallel",)),
        name="post_attn",
    )(x2d, oa, *ogs, *lss, main, wa, wb, wo, gffn, wr, br)


def _route_kernel(lg_ref, ri_ref, rw_ref, cnt_ref, carry_ref, *, tm):
    i = pl.program_id(0)

    @pl.when(i == 0)
    def _():
        carry_ref[...] = jnp.zeros(carry_ref.shape, F32)

    lg = lg_ref[...]
    lane = lax.broadcasted_iota(jnp.int32, lg.shape, 1)
    lane_f = lane.astype(F32)
    ninf = jnp.float32(-jnp.inf)

    def first_max(vals):
        vmax = jnp.max(vals, axis=-1, keepdims=True)
        idx = jnp.min(jnp.where(vals == vmax, lane_f, 128.0), axis=-1, keepdims=True)
        return vmax, idx.astype(jnp.int32)

    gl = jnp.where(lane < N_GROUPS, lg, ninf)
    gmax, g_sel = first_max(gl)
    p_g = 1.0 / jnp.sum(jnp.exp(gl - gmax), axis=-1, keepdims=True)
    lo = N_GROUPS + g_sel * EXPERTS_PER_GROUP
    el = jnp.where(jnp.logical_and(lane >= lo, lane < lo + EXPERTS_PER_GROUP), lg, ninf)
    v0, i0 = first_max(el)
    v1, i1 = first_max(jnp.where(lane == i0, ninf, el))
    t = jnp.exp(v1 - v0)
    w0 = p_g / (1.0 + t)
    w1 = p_g * t / (1.0 + t)
    e0 = i0 - N_GROUPS
    e1 = i1 - N_GROUPS
    hit0 = lane == e0
    hit1 = lane == e1
    oh = jnp.where(jnp.logical_or(hit0, hit1), 1.0, 0.0).astype(F32)
    r_i = lax.broadcasted_iota(jnp.int32, (tm, tm), 0)
    c_i = lax.broadcasted_iota(jnp.int32, (tm, tm), 1)
    lower = jnp.where(c_i < r_i, 1.0, 0.0).astype(BF16)
    excl = jnp.dot(lower, oh.astype(BF16), preferred_element_type=F32) + carry_ref[...]
    r0 = jnp.sum(jnp.where(hit0, excl, 0.0), axis=-1, keepdims=True).astype(jnp.int32)
    r1 = jnp.sum(jnp.where(hit1, excl, 0.0), axis=-1, keepdims=True).astype(jnp.int32)
    carry_ref[...] = carry_ref[...] + jnp.sum(oh, axis=0, keepdims=True)
    zi = jnp.zeros(lg.shape, jnp.int32)
    ri_ref[...] = jnp.where(lane == 0, e0, jnp.where(lane == 1, e1, jnp.where(lane == 2, r0, jnp.where(lane == 3, r1, zi))))
    rw_ref[...] = jnp.where(lane == 0, w0, jnp.where(lane == 1, w1, jnp.zeros(lg.shape, F32)))
    cnt_ref[...] = carry_ref[...]


def _route(lg, tm):
    T = lg.shape[0]
    return pl.pallas_call(
        functools.partial(_route_kernel, tm=tm),
        grid=(T // tm,),
        in_specs=[pl.BlockSpec((tm, 128), lambda i: (i, 0))],
        out_specs=[pl.BlockSpec((tm, 128), lambda i: (i, 0)),
                   pl.BlockSpec((tm, 128), lambda i: (i, 0)),
                   pl.BlockSpec((1, 128), lambda i: (0, 0))],
        out_shape=[jax.ShapeDtypeStruct((T, 128), jnp.int32),
                   jax.ShapeDtypeStruct((T, 128), F32),
                   jax.ShapeDtypeStruct((1, 128), F32)],
        scratch_shapes=[pltpu.VMEM((1, 128), F32)],
        compiler_params=_cparams(("arbitrary",)),
        name="route",
    )(lg)


def _sc_gather_rows(table, idx, n_out=None):
    inverse = n_out is not None
    n_src = idx.shape[0]
    n = n_out if inverse else n_src
    n_workers = SC_CORES * SC_SUBCORES
    per_w = n // n_workers
    n_chunks = per_w // SC_CHUNK
    assert per_w * n_workers == n and n_chunks * SC_CHUNK == per_w and n_chunks % 2 == 0
    mesh = plsc.VectorSubcoreMesh(core_axis_name="c", subcore_axis_name="s",
                                  num_cores=SC_CORES, num_subcores=SC_SUBCORES)

    @functools.partial(
        pl.kernel, mesh=mesh,
        out_type=jax.ShapeDtypeStruct((n,) + table.shape[1:], table.dtype),
        scratch_types=[pltpu.VMEM((per_w,), jnp.int32),
                       pltpu.VMEM((SC_CHUNK,) + table.shape[1:], table.dtype),
                       pltpu.VMEM((SC_CHUNK,) + table.shape[1:], table.dtype),
                       pltpu.SemaphoreType.DMA, pltpu.SemaphoreType.DMA,
                       pltpu.VMEM((n_src if inverse else SC_LANES,), jnp.int32)],
        compiler_params=pltpu.CompilerParams(use_tc_tiling_on_sc=True, needs_layout_passes=not inverse),
        name="sc_dispatch_rows" if inverse else "sc_gather_rows",
    )
    def gather(table_hbm, idx_hbm, out_hbm, idx_v, rows_a, rows_b, sem_a, sem_b, map_v):
        wid = lax.axis_index("s") * SC_CORES + lax.axis_index("c")
        base = wid * per_w
        if inverse:
            pltpu.sync_copy(idx_hbm, map_v)
            lanes = lax.iota(jnp.int32, SC_LANES)

            @pl.loop(0, per_w // SC_LANES)
            def _(j):
                idx_v[pl.ds(j * SC_LANES, SC_LANES)] = lax.rem(base + j * SC_LANES + lanes, table.shape[0])

            @pl.loop(0, n_src // SC_LANES)
            def _(a):
                local = map_v[pl.ds(a * SC_LANES, SC_LANES)] - base
                mine = jnp.logical_and(local >= 0, local < per_w)
                plsc.store_scatter(idx_v, [local], lax.div(a * SC_LANES + lanes, TOP_K), mask=mine)
        else:
            pltpu.sync_copy(idx_hbm.at[pl.ds(base, per_w)], idx_v)

        def fetch(chunk, rows_v, sem):
            return pltpu.make_async_copy(table_hbm.at[idx_v.at[pl.ds(chunk * SC_CHUNK, SC_CHUNK)]], rows_v, sem)

        def flush(chunk, rows_v):
            pltpu.sync_copy(rows_v, out_hbm.at[pl.ds(base + chunk * SC_CHUNK, SC_CHUNK)])

        fetch(0, rows_a, sem_a).start()

        @pl.loop(0, n_chunks, step=2)
        def _(c):
            fetch(c + 1, rows_b, sem_b).start()
            fetch(c, rows_a, sem_a).wait()
            flush(c, rows_a)

            @pl.when(c + 2 < n_chunks)
            def _():
                fetch(c + 2, rows_a, sem_a).start()

            fetch(c + 1, rows_b, sem_b).wait()
            flush(c + 1, rows_b)

    return gather(table, idx)


def _rows_from_tiles(ref, first, n_rows, stride=ROW_TILE):
    words = [ref[pl.ds(first + c, n_rows, stride=stride), :] for c in range(ROW_TILE)]
    lo = [pltpu.bitcast(w << 16, F32) for w in words]
    hi = [pltpu.bitcast(w & U32(0xFFFF0000), F32) for w in words]
    return jnp.concatenate(lo + hi, axis=-1)


def _rows_to_tiles(ref, val):
    half = D_MODEL // 2

    def bits(x):
        return pltpu.bitcast(x.astype(BF16).astype(F32), U32)

    for c in range(ROW_TILE):
        lo = bits(val[:, c * 128:(c + 1) * 128])
        hi = bits(val[:, half + c * 128:half + (c + 1) * 128])
        ref[pl.ds(c, val.shape[0], stride=ROW_TILE), :] = hi | (lo >> 16)


def _ffn_kernel(be_ref, nvb_ref, x_ref, w1_ref, w3_ref, w2_ref, y_ref, w1b_ref, w3b_ref, w2b_ref):
    b = pl.program_id(0)
    nvb = nvb_ref[0]
    new_expert = jnp.logical_or(b == 0, be_ref[b] != be_ref[jnp.maximum(b - 1, 0)])

    @pl.when(jnp.logical_and(b < nvb, new_expert))
    def _():
        w1b_ref[...] = w1_ref[0, 0].astype(BF16)
        w3b_ref[...] = w3_ref[0, 0].astype(BF16)
        w2b_ref[...] = w2_ref[0, 0].astype(BF16)

    @pl.when(b < nvb)
    def _():
        xb = _rows_from_tiles(x_ref, 0, FFN_BM).astype(BF16)
        h1 = jnp.dot(xb, w1b_ref[...], preferred_element_type=F32)
        h3 = jnp.dot(xb, w3b_ref[...], preferred_element_type=F32)
        a = (jax.nn.silu(h1) * h3).astype(BF16)
        _rows_to_tiles(y_ref, jnp.dot(a, w2b_ref[...], preferred_element_type=F32))

    @pl.when(b >= nvb)
    def _():
        y_ref[...] = jnp.zeros(y_ref.shape, U32)


def _expert_ffn(block_e, nvb, xs, w1, w3, w2, layer):
    nb = block_e.shape[0]

    def used(b, nv):
        return jnp.minimum(b, jnp.maximum(nv[0] - 1, 0))

    return pl.pallas_call(
        _ffn_kernel,
        grid_spec=pltpu.PrefetchScalarGridSpec(
            num_scalar_prefetch=2,
            grid=(nb,),
            in_specs=[
                pl.BlockSpec((FFN_BM * ROW_TILE, 128), lambda b, be, nv: (used(b, nv), 0)),
                pl.BlockSpec((1, 1, D_MODEL, EXPERT_FF), lambda b, be, nv: (layer, be[b], 0, 0)),
                pl.BlockSpec((1, 1, D_MODEL, EXPERT_FF), lambda b, be, nv: (layer, be[b], 0, 0)),
                pl.BlockSpec((1, 1, EXPERT_FF, D_MODEL), lambda b, be, nv: (layer, be[b], 0, 0)),
            ],
            out_specs=pl.BlockSpec((FFN_BM * ROW_TILE, 128), lambda b, be, nv: (b, 0)),
            scratch_shapes=[pltpu.VMEM((D_MODEL, EXPERT_FF), BF16), pltpu.VMEM((D_MODEL, EXPERT_FF), BF16),
                            pltpu.VMEM((EXPERT_FF, D_MODEL), BF16)],
        ),
        out_shape=jax.ShapeDtypeStruct((nb * FFN_BM * ROW_TILE, 128), U32),
        compiler_params=_cparams(("arbitrary",)),
        name="expert_ffn",
    )(block_e, nvb, xs, w1, w3, w2)


def _comb_kernel(yg_ref, x1_ref, rw_ref, p_ref, gple_ref, wpg_ref, wpp_ref, gout_ref, o_ref, *, tm, final):
    w = rw_ref[...]
    y0 = _rows_from_tiles(yg_ref, 0, tm, stride=TOP_K * ROW_TILE)
    y1 = _rows_from_tiles(yg_ref, ROW_TILE, tm, stride=TOP_K * ROW_TILE)
    x2 = x1_ref[...] + (y0 * w[:, 0:1] + y1 * w[:, 1:2])
    e = jnp.dot(p_ref[...].astype(BF16), wpp_ref[...], preferred_element_type=F32)
    gate = jax.nn.sigmoid(jnp.dot(_rms(x2, gple_ref[...]).astype(BF16), wpg_ref[...], preferred_element_type=F32))
    x3 = x2 + gate * e
    o_ref[...] = _rms(x3, gout_ref[...]) if final else x3


def _combine_ple(yg, x1, rw, p2d, gple, wpg, wpp, gout, tm, layer):
    T = x1.shape[0]
    nt = T // tm
    const = lambda i: (0, 0)
    return pl.pallas_call(
        functools.partial(_comb_kernel, tm=tm, final=layer == DEPTH - 1),
        grid=(nt,),
        in_specs=[
            pl.BlockSpec((tm * TOP_K * ROW_TILE, 128), lambda i: (i, 0)),
            pl.BlockSpec((tm, D_MODEL), lambda i: (i, 0)),
            pl.BlockSpec((tm, 128), lambda i: (i, 0)),
            pl.BlockSpec((tm, PLE_DIM), lambda i: (layer * nt + i, 0)),
            pl.BlockSpec((1, D_MODEL), const),
            pl.BlockSpec((D_MODEL, D_MODEL), const),
            pl.BlockSpec((PLE_DIM, D_MODEL), const),
            pl.BlockSpec((1, D_MODEL), const),
        ],
        out_specs=pl.BlockSpec((tm, D_MODEL), lambda i: (i, 0)),
        out_shape=jax.ShapeDtypeStruct((T, D_MODEL), F32),
        compiler_params=_cparams(("parallel",)),
        name="combine_ple",
    )(yg, x1, rw, p2d, gple, wpg, wpp, gout)


def _rope_tables(pos, theta, rot_dim, period, offset, scale):
    half = rot_dim // 2
    inv = jnp.float32(theta) ** (-jnp.arange(half, dtype=F32) * 2.0 / rot_dim)
    rel = np.arange(128) % period - offset
    rot = (rel >= 0) & (rel < rot_dim)
    upper = rot & (rel >= half)
    lower = rot & (rel < half)
    ang = pos.astype(F32)[:, None] * inv[np.where(rot, rel % half, 0)][None, :]
    cos, sin = jnp.cos(ang), jnp.sin(ang)
    tabs = (jnp.where(rot, cos, 1.0), jnp.where(upper, sin, 0.0), jnp.where(lower, -sin, 0.0))
    return jnp.concatenate(tabs, axis=1) * jnp.float32(scale)


def _prep_w_in(w):
    c_q = w[:, 0:512]
    c_kv = w[:, 512:768]
    k_pe = w[:, 768:800]
    dil = w[:, 800:800 + 2304]
    gates = w[:, 3104:5152]
    kslot = jnp.zeros((D_MODEL, 256), w.dtype).at[:, MLA_NOPE:MLA_NOPE + MLA_ROPE].set(k_pe)
    return jnp.concatenate([gates, c_q, c_kv, kslot, dil], axis=1).astype(BF16)


def _prep_w_q(w):
    w = w.reshape(MLA_Q_LORA, MLA_HEADS, MLA_NOPE + MLA_ROPE)
    w = jnp.pad(w, ((0, 0), (0, 0), (0, MLA_SLOT - MLA_NOPE - MLA_ROPE)))
    return w.reshape(MLA_Q_LORA, MLA_HEADS * MLA_SLOT).astype(BF16)


def _prep_w_kv(w):
    w = w.reshape(MLA_KV_LORA, MLA_HEADS, MLA_NOPE + MLA_V)
    wk = jnp.pad(w[:, :, :MLA_NOPE], ((0, 0), (0, 0), (0, MLA_SLOT - MLA_NOPE)))
    wv = jnp.pad(w[:, :, MLA_NOPE:], ((0, 0), (0, 0), (0, MLA_SLOT - MLA_V)))
    return (wk.reshape(MLA_KV_LORA, MLA_HEADS * MLA_SLOT).astype(BF16),
            wv.reshape(MLA_KV_LORA, MLA_HEADS * MLA_SLOT).astype(BF16))


def _dest_kernel(ri_ref, ps_ref, o_ref):
    ri = ri_ref[...].astype(F32)
    lane = lax.broadcasted_iota(jnp.int32, ri.shape, 1)
    ps = ps_ref[...]

    def col(k):
        return jnp.sum(jnp.where(lane == k, ri, 0.0), axis=-1, keepdims=True)

    def dest(k):
        start = jnp.sum(jnp.where(lane == col(k).astype(jnp.int32), ps, 0.0), axis=-1, keepdims=True)
        return (start + col(2 + k)).astype(jnp.int32)

    o_ref[...] = jnp.where(lane == 0, dest(0), jnp.where(lane == 1, dest(1), 0))


def _dest_rows(ri, pstarts, tm):
    T = ri.shape[0]
    ps = jnp.zeros((1, 128), F32).at[0, :N_EXPERTS].set(pstarts.astype(F32))
    return pl.pallas_call(
        _dest_kernel,
        grid=(T // tm,),
        in_specs=[pl.BlockSpec((tm, 128), lambda i: (i, 0)), pl.BlockSpec((1, 128), lambda i: (0, 0))],
        out_specs=pl.BlockSpec((tm, 128), lambda i: (i, 0)),
        out_shape=jax.ShapeDtypeStruct((T, 128), jnp.int32),
        compiler_params=_cparams(("parallel",)),
        name="dest_rows",
    )(ri, ps)


def _dispatch_plan(ri, cnt, T, tm):
    counts = cnt[0, :N_EXPERTS].astype(jnp.int32)
    pcounts = (counts + FFN_BM - 1) // FFN_BM * FFN_BM
    pends = jnp.cumsum(pcounts)
    pstarts = pends - pcounts
    dest = _dest_rows(ri, pstarts, tm)[:, 0:TOP_K]
    n_blocks = (T * TOP_K) // FFN_BM + N_EXPERTS
    block_e = jnp.minimum(jnp.searchsorted(pends, jnp.arange(n_blocks, dtype=jnp.int32) * FFN_BM, side='right'),
                          N_EXPERTS - 1).astype(jnp.int32)
    nvb = (pends[-1] // FFN_BM).astype(jnp.int32).reshape(1)
    return dest.reshape(T * TOP_K), n_blocks * FFN_BM, block_e, nvb


def kernel(x, p, positions, g_mix, w_in, g_q_lat, w_q_up, g_kv_lat, w_kv_up, w_branch_a, w_branch_b, w_out, g_ffn, w_router_grp, b_router_grp, w_router_exp, b_router_exp, w_exp_gate, w_exp_up, w_exp_down, g_ple, w_ple_gate, w_ple_proj, g_final):
    B, S, D = x.shape
    T = B * S
    TM_IN, TM_QKV, TQ, TM_POST, TM_ROUTE, TM_COMB = 512, 512, 512, 512, 512, 256

    pos = positions.reshape(T)
    dil_scale = DIL_HEAD_DIM ** -0.5 * LOG2E
    mla_scale = (MLA_NOPE + MLA_ROPE) ** -0.5 * LOG2E
    tab_dil = jnp.concatenate([_rope_tables(pos, ROPE_THETA, PARTIAL_ROT, DIL_HEAD_DIM, 0, dil_scale),
                               _rope_tables(pos, ROPE_THETA, PARTIAL_ROT, DIL_HEAD_DIM, 0, 1.0)], axis=1)
    tab_mla = jnp.concatenate([_rope_tables(pos, MLA_ROPE_THETA, MLA_ROPE, MLA_SLOT, MLA_NOPE, mla_scale),
                               _rope_tables(pos, MLA_ROPE_THETA, MLA_ROPE, MLA_SLOT, MLA_NOPE, 1.0)], axis=1)

    xc = x.reshape(T, D)
    for i in range(DEPTH):
        w_in_p = _prep_w_in(w_in[i])
        wq = _prep_w_q(w_q_up[i])
        wk, wv = _prep_w_kv(w_kv_up[i])
        wr = jnp.zeros((D, 128), F32).at[:, :N_GROUPS].set(w_router_grp[i]).at[:, N_GROUPS:N_GROUPS + N_EXPERTS].set(
            w_router_exp[i]).astype(BF16)
        br = jnp.zeros((1, 128), F32).at[0, :N_GROUPS].set(b_router_grp[i]).at[0, N_GROUPS:N_GROUPS + N_EXPERTS].set(
            b_router_exp[i].reshape(N_EXPERTS))

        main, d0, d1, d2 = _in_proj(xc, g_mix[i].reshape(1, D), w_in_p, tab_dil, B, S, TM_IN)
        q, k, v = _mla_qkv(main, g_q_lat[i].reshape(1, -1), g_kv_lat[i].reshape(1, -1), wq, wk, wv, tab_mla, TM_QKV)
        o_a = _mla_attn(q, k, v, B, S, TQ)
        ogs, lss = [], []
        for (_, dil), qkv in zip(DIL_PATTERN, (d0, d1, d2)):
            og, ls = _dil_attn(qkv, B, dil, S // dil)
            ogs.append(og)
            lss.append(ls)
        x1, h2, lg = _post(xc, o_a, ogs, lss, main, w_branch_a[i].astype(BF16), w_branch_b[i].astype(BF16),
                           w_out[i].astype(BF16), g_ffn[i].reshape(1, D), wr, br, B, S, TM_POST)
        ri, rw, cnt = _route(lg, TM_ROUTE)
        dest, n_rows, block_e, nvb = _dispatch_plan(ri, cnt, T, TM_ROUTE)
        xs = _sc_gather_rows(h2.reshape(T, ROW_TILE, 128), dest, n_out=n_rows)
        ys = _expert_ffn(block_e, nvb, xs.reshape(-1, 128), w_exp_gate, w_exp_up, w_exp_down, i)
        yg = _sc_gather_rows(ys.reshape(-1, ROW_TILE, 128), dest)
        xc = _combine_ple(yg.reshape(-1, 128), x1, rw, p.reshape(DEPTH * T, PLE_DIM), g_ple[i].reshape(1, D),
                          w_ple_gate[i].astype(BF16), w_ple_proj[i].astype(BF16), g_final.reshape(1, D), TM_COMB, i)
    return xc.reshape(B, S, D)
```

```python
import functools
import math

import jax
import jax.numpy as jnp
import numpy as np
from jax import lax
from jax.experimental import pallas as pl
from jax.experimental.pallas import tpu as pltpu
from jax.experimental.pallas import tpu_sc as plsc

F32 = jnp.float32
BF16 = jnp.bfloat16

D_MODEL = 1024
DEPTH = 4
RMS_EPS = 1e-6
NEG = -1e30
LOG2E = math.log2(math.e)

MLA_HEADS = 16
MLA_Q_LORA = 512
MLA_KV_LORA = 256
MLA_NOPE = 64
MLA_ROPE = 32
MLA_V = 64
MLA_ROPE_THETA = 10000.0
MLA_SLOT = 128

DIL_PATTERN = ((128, 1), (512, 4), (2048, 16))
DIL_GROUPS = 3
DIL_HEADS = 4
DIL_HEAD_DIM = 64
DIL_W = DIL_HEADS * DIL_HEAD_DIM
DIL_BLK = 128
DIL_UNROLL = 16
ROPE_THETA = 500000.0
PARTIAL_ROT = DIL_HEAD_DIM // 4

N_GROUPS = 8
EXPERTS_PER_GROUP = 8
N_EXPERTS = 64
TOP_K = 2
EXPERT_FF = 256
PLE_DIM = 256

IN_TN = 768
MAIN_COLS = 3072
IN_COLS_PAD = MAIN_COLS + DIL_GROUPS * 3 * DIL_W
N_MAIN_TILES = MAIN_COLS // IN_TN

FFN_BM = 256
ROW_TILE = D_MODEL // 256
U32 = jnp.uint32
VMEM_LIMIT = 48 * 1024 * 1024

SC_CORES = 2
SC_SUBCORES = 16
SC_LANES = 16
SC_CHUNK = 64


def _cparams(sem, bounds_checks=True):
    return pltpu.CompilerParams(dimension_semantics=sem, vmem_limit_bytes=VMEM_LIMIT,
                                disable_bounds_checks=not bounds_checks)


def _layer_spec(shape, layer, col=None):
    def index(*grid_idx):
        return (layer,) + (0,) * (len(shape) - 1) + ((col(*grid_idx),) if col else (0,))

    return pl.BlockSpec((None,) + tuple(shape), index)


def _rms(x, g):
    return x * lax.rsqrt(jnp.mean(x * x, axis=-1, keepdims=True) + RMS_EPS) * g


def _rope128(x, c, s_up, s_dn, half):
    return x * c + pltpu.roll(x, half, 1) * s_up + pltpu.roll(x, 128 - half, 1) * s_dn


def _in_kernel(x_ref, g_ref, w_ref, tab_ref, main_ref, d0_ref, d1_ref, d2_ref, xn_ref, acc_ref):
    j = pl.program_id(1)

    @pl.when(j == 0)
    def _():
        xn_ref[...] = _rms(x_ref[...], g_ref[...]).astype(BF16)

    acc = jnp.dot(xn_ref[...], w_ref[...], preferred_element_type=F32)

    @pl.when(j < N_MAIN_TILES)
    def _():
        main_ref[...] = acc.astype(BF16)

    def dil_tile(out_ref, dil):
        for c in range(4):
            t0 = 0 if c < 2 else 3
            x = acc[:, c * 128:(c + 1) * 128]
            acc_ref[c] = _rope128(
                x, tab_ref[:, t0 * 128:(t0 + 1) * 128], tab_ref[:, (t0 + 1) * 128:(t0 + 2) * 128],
                tab_ref[:, (t0 + 2) * 128:(t0 + 3) * 128], PARTIAL_ROT // 2)
        for c in range(4, 6):
            acc_ref[c] = acc[:, c * 128:(c + 1) * 128]
        rows = acc_ref.shape[1] // dil
        for r in range(dil):
            for c in range(6):
                out_ref[0, r, :, c * 128:(c + 1) * 128] = acc_ref[c, pl.ds(r, rows, stride=dil), :].astype(BF16)

    for gi, (_, dil) in enumerate(DIL_PATTERN):
        pl.when(j == N_MAIN_TILES + gi)(functools.partial(dil_tile, (d0_ref, d1_ref, d2_ref)[gi], dil))


def _in_proj(x2d, g, w, tab, B, S, tm, layer):
    T = x2d.shape[0]
    nt = S // tm
    dil_shapes = [jax.ShapeDtypeStruct((B, d, S // d, 3 * DIL_W), BF16) for _, d in DIL_PATTERN]
    dil_specs = [pl.BlockSpec((1, d, tm // d, 3 * DIL_W), lambda i, j, nt=nt: (i // nt, 0, i % nt, 0))
                 for _, d in DIL_PATTERN]
    return pl.pallas_call(
        _in_kernel,
        grid=(T // tm, IN_COLS_PAD // IN_TN),
        in_specs=[
            pl.BlockSpec((tm, D_MODEL), lambda i, j: (i, 0)),
            _layer_spec((1, D_MODEL), layer),
            _layer_spec((D_MODEL, IN_TN), layer, col=lambda i, j: j),
            pl.BlockSpec((tm, 6 * 128), lambda i, j: (i, 0)),
        ],
        out_specs=[pl.BlockSpec((tm, IN_TN), lambda i, j: (i, jnp.minimum(j, N_MAIN_TILES - 1)))] + dil_specs,
        out_shape=[jax.ShapeDtypeStruct((T, MAIN_COLS), BF16)] + dil_shapes,
        scratch_shapes=[pltpu.VMEM((tm, D_MODEL), BF16), pltpu.VMEM((IN_TN // 128, tm, 128), F32)],
        compiler_params=_cparams(("parallel", "arbitrary")),
        name="in_proj",
    )(x2d, g, w, tab)


def _qkv_kernel(cq_ref, ckv_ref, kpe_ref, gq_ref, gkv_ref, wq_ref, wk_ref, wv_ref, tab_ref,
                q_ref, k_ref, v_ref):
    qn = _rms(cq_ref[...].astype(F32), gq_ref[...]).astype(BF16)
    kvn = _rms(ckv_ref[...].astype(F32), gkv_ref[...]).astype(BF16)
    qacc = jnp.dot(qn, wq_ref[...], preferred_element_type=F32)
    kacc = jnp.dot(kvn, wk_ref[...], preferred_element_type=F32)
    vacc = jnp.dot(kvn, wv_ref[...], preferred_element_type=F32)
    lane_v = lax.broadcasted_iota(jnp.int32, vacc.shape, 1)
    v_ref[...] = jnp.where(lane_v % MLA_SLOT < MLA_V, vacc, 1.0).astype(BF16)
    half = MLA_ROPE // 2
    tq = [tab_ref[:, t * 128:(t + 1) * 128] for t in range(3)]
    tk = [tab_ref[:, t * 128:(t + 1) * 128] for t in range(3, 6)]
    krot = _rope128(kpe_ref[:, 0:128].astype(F32), tk[0], tk[1], tk[2], half)
    for h in range(MLA_HEADS):
        sl = slice(h * MLA_SLOT, (h + 1) * MLA_SLOT)
        q_ref[:, sl] = _rope128(qacc[:, sl], tq[0], tq[1], tq[2], half).astype(BF16)
        k_ref[:, sl] = (kacc[:, sl] + krot).astype(BF16)


def _mla_qkv(main, gq, gkv, wq, wk, wv, tab, tm, layer):
    T = main.shape[0]
    return pl.pallas_call(
        _qkv_kernel,
        grid=(T // tm,),
        in_specs=[
            pl.BlockSpec((tm, MLA_Q_LORA), lambda i: (i, 2048 // MLA_Q_LORA)),
            pl.BlockSpec((tm, MLA_KV_LORA), lambda i: (i, 2560 // MLA_KV_LORA)),
            pl.BlockSpec((tm, 256), lambda i: (i, 2816 // 256)),
            _layer_spec((1, MLA_Q_LORA), layer),
            _layer_spec((1, MLA_KV_LORA), layer),
            _layer_spec((MLA_Q_LORA, MLA_HEADS * MLA_SLOT), layer),
            _layer_spec((MLA_KV_LORA, MLA_HEADS * MLA_SLOT), layer),
            _layer_spec((MLA_KV_LORA, MLA_HEADS * MLA_SLOT), layer),
            pl.BlockSpec((tm, 6 * 128), lambda i: (i, 0)),
        ],
        out_specs=[
            pl.BlockSpec((tm, MLA_HEADS * MLA_SLOT), lambda i: (i, 0)),
            pl.BlockSpec((tm, MLA_HEADS * MLA_SLOT), lambda i: (i, 0)),
            pl.BlockSpec((tm, MLA_HEADS * MLA_SLOT), lambda i: (i, 0)),
        ],
        out_shape=[
            jax.ShapeDtypeStruct((T, MLA_HEADS * MLA_SLOT), BF16),
            jax.ShapeDtypeStruct((T, MLA_HEADS * MLA_SLOT), BF16),
            jax.ShapeDtypeStruct((T, MLA_HEADS * MLA_SLOT), BF16),
        ],
        compiler_params=_cparams(("parallel",)),
        name="mla_qkv",
    )(main, main, main, gq, gkv, wq, wk, wv, tab)


def _mla_attn_kernel(qa_ref, qb_ref, k_ref, v_ref, o_ref, m_ref, acc_ref, *, tq):
    p_id = pl.program_id(2)
    nq = k_ref.shape[1] // tq
    pair_w = 2 * MLA_SLOT
    lane = lax.broadcasted_iota(jnp.int32, (tq, pair_w), 1)
    qh = []
    for q_ref in (qa_ref, qb_ref):
        qp = q_ref[0]
        zero = jnp.zeros_like(qp)
        qh.append([jnp.where(lane < MLA_SLOT, qp, zero), jnp.where(lane >= MLA_SLOT, qp, zero)])
    hq = tq // 2
    dn = (((1,), (1,)), ((), ()))

    def step(blk, start, width, r0, mask, first=False):
        kb = k_ref[0, pl.ds(start, width), :]
        vb = v_ref[0, pl.ds(start, width), :]
        for h in range(2):
            s = lax.dot_general(qh[blk][h][r0:], kb, dn, preferred_element_type=F32)
            if mask is not None:
                s = jnp.where(mask, s, NEG)
            m_cur = jnp.max(s, axis=-1, keepdims=True)
            if first:
                m_new = jnp.broadcast_to(m_cur, (tq - r0, 128))
            else:
                m_prev = m_ref[blk, h, r0:, :]
                m_new = jnp.maximum(m_prev, m_cur)
            p = jnp.exp2(s - jnp.concatenate([m_new] * (width // 128), axis=-1))
            pv = jnp.dot(p.astype(BF16), vb[:, h * MLA_SLOT:(h + 1) * MLA_SLOT], preferred_element_type=F32)
            if first:
                acc_ref[blk, h, r0:, :] = pv
            else:
                acc_ref[blk, h, r0:, :] = jnp.exp2(m_prev - m_new) * acc_ref[blk, h, r0:, :] + pv
            m_ref[blk, h, r0:, :] = m_new

    def causal(rows):
        return lax.broadcasted_iota(jnp.int32, (rows, hq), 1) <= lax.broadcasted_iota(jnp.int32, (rows, hq), 0)

    def q_blocks(n_a):
        n_b = nq - 1 - n_a
        for blk, n_full in ((1, n_b), (0, n_a)):
            for j in range(n_full):
                step(blk, j * tq, tq, 0, None, first=j == 0)
            step(blk, n_full * tq, hq, 0, causal(tq), first=n_full == 0)
            step(blk, n_full * tq + hq, hq, hq, causal(hq))

    for n_a in range(nq // 2):
        pl.when(p_id == n_a)(functools.partial(q_blocks, n_a))
    lane_o = lax.broadcasted_iota(jnp.int32, (tq, 2 * MLA_V), 1)
    for blk in range(2):
        outs = []
        for h in range(2):
            a = acc_ref[blk, h]
            outs.append(a / pltpu.roll(a, MLA_V, 1))
        o_ref[0, blk, 0] = jnp.where(lane_o < MLA_V, outs[0], pltpu.roll(outs[1], MLA_V, 1)).astype(BF16)


def _mla_attn(q, k, v, B, S, tq):
    nq = S // tq
    q = q.reshape(B, S, MLA_HEADS * MLA_SLOT)
    k = k.reshape(B, S, MLA_HEADS * MLA_SLOT)
    v = v.reshape(B, S, MLA_HEADS * MLA_SLOT)
    return pl.pallas_call(
        functools.partial(_mla_attn_kernel, tq=tq),
        grid=(B, MLA_HEADS // 2, nq // 2),
        in_specs=[
            pl.BlockSpec((1, tq, 2 * MLA_SLOT), lambda b, h, p: (b, p, h)),
            pl.BlockSpec((1, tq, 2 * MLA_SLOT), lambda b, h, p: (b, nq - 1 - p, h)),
            pl.BlockSpec((1, S, 2 * MLA_SLOT), lambda b, h, p: (b, 0, h)),
            pl.BlockSpec((1, S, 2 * MLA_SLOT), lambda b, h, p: (b, 0, h)),
        ],
        out_specs=pl.BlockSpec((1, 2, 1, tq, 2 * MLA_V), lambda b, h, p: (b, 0, p, 0, h)),
        out_shape=jax.ShapeDtypeStruct((B, 2, nq // 2, tq, MLA_HEADS * MLA_V), BF16),
        scratch_shapes=[pltpu.VMEM((2, 2, tq, 128), F32), pltpu.VMEM((2, 2, tq, MLA_SLOT), F32)],
        compiler_params=_cparams(("parallel", "parallel", "arbitrary")),
        name="mla_attn",
    )(q, q, k, v)


def _dil_kernel(qkv_ref, o_ref, lse_ref, bias_ref, *, nb):
    dil = qkv_ref.shape[1]
    width = 2 * DIL_BLK if nb > 1 else DIL_BLK
    row = lax.broadcasted_iota(jnp.int32, (DIL_BLK, width), 0)
    col = lax.broadcasted_iota(jnp.int32, (DIL_BLK, width), 1)
    bias_ref[0] = jnp.where(col <= row, 0.0, NEG)
    if nb > 1:
        later = jnp.logical_or(jnp.logical_and(col >= DIL_BLK, col - DIL_BLK <= row),
                               jnp.logical_and(col < DIL_BLK, col >= row))
        bias_ref[1] = jnp.where(later, 0.0, NEG)
    head_of_lane = lax.broadcasted_iota(jnp.int32, (DIL_BLK, DIL_W), 1) // DIL_HEAD_DIM
    dn = (((1,), (1,)), ((), ()))

    def by_head(parts):
        out = parts[DIL_HEADS - 1]
        for h in range(DIL_HEADS - 2, -1, -1):
            out = jnp.where(head_of_lane == h, parts[h], out)
        return out

    def unit(u):
        r = u // nb
        n = u % nb
        q0 = pl.multiple_of(n * DIL_BLK, DIL_BLK)
        q = qkv_ref[0, r, pl.ds(q0, DIL_BLK), 0:DIL_W]
        if nb > 1:
            w0 = pl.multiple_of(jnp.maximum(n - 1, 0) * DIL_BLK, DIL_BLK)
            bias = bias_ref[jnp.minimum(n, 1)]
        else:
            w0 = 0
            bias = bias_ref[0]
        kw = qkv_ref[0, r, pl.ds(w0, width), DIL_W:2 * DIL_W]
        vw = qkv_ref[0, r, pl.ds(w0, width), 2 * DIL_W:3 * DIL_W]
        zero = jnp.zeros_like(q)
        qs = jnp.concatenate([jnp.where(head_of_lane == h, q, zero) for h in range(DIL_HEADS)], axis=0)
        s = lax.dot_general(qs, kw, dn, preferred_element_type=F32)
        s = (s.reshape(DIL_HEADS, DIL_BLK, width) + bias[None]).reshape(DIL_HEADS * DIL_BLK, width)
        m = jnp.max(s, axis=-1, keepdims=True)
        e = jnp.exp2(s - m)
        den = jnp.sum(e, axis=-1, keepdims=True)
        pv = jnp.dot(e.astype(BF16), vw, preferred_element_type=F32)
        lse = m + jnp.log2(den)
        blk = lambda t, h: t[h * DIL_BLK:(h + 1) * DIL_BLK]
        o = by_head([blk(pv, h) for h in range(DIL_HEADS)]) / by_head(
            [jnp.broadcast_to(blk(den, h), (DIL_BLK, DIL_W)) for h in range(DIL_HEADS)])
        o_ref[0, r, pl.ds(q0, DIL_BLK), :] = o.astype(BF16)
        lse_ref[0, r, pl.ds(q0, DIL_BLK), :] = by_head(
            [jnp.broadcast_to(blk(lse, h), (DIL_BLK, DIL_W)) for h in range(DIL_HEADS)])

    def body(t, carry):
        for u in range(DIL_UNROLL):
            unit(DIL_UNROLL * t + u)
        return carry

    lax.fori_loop(0, dil * nb // DIL_UNROLL, body, 0)


def _dil_attn(qkv, B, dil, L):
    nb = L // DIL_BLK
    return pl.pallas_call(
        functools.partial(_dil_kernel, nb=nb),
        grid=(B,),
        in_specs=[pl.BlockSpec((1, dil, L, 3 * DIL_W), lambda b: (b, 0, 0, 0))],
        out_specs=[
            pl.BlockSpec((1, dil, L, DIL_W), lambda b: (b, 0, 0, 0)),
            pl.BlockSpec((1, dil, L, DIL_W), lambda b: (b, 0, 0, 0)),
        ],
        out_shape=[
            jax.ShapeDtypeStruct((B, dil, L, DIL_W), BF16),
            jax.ShapeDtypeStruct((B, dil, L, DIL_W), F32),
        ],
        scratch_shapes=[pltpu.VMEM((2, DIL_BLK, 2 * DIL_BLK if nb > 1 else DIL_BLK), F32)],
        compiler_params=_cparams(("parallel",)),
        name=f"dil_attn_d{dil}",
    )(qkv)


def _post_kernel(x_ref, oa_ref, og0_ref, og1_ref, og2_ref, ls0_ref, ls1_ref, ls2_ref, gates_ref,
                 wa_ref, wb_ref, wo_ref, gffn_ref, wr_ref, br_ref,
                 x1_ref, h2_ref, lg_ref, ob_ref, *, tm):
    ls =[r[0] for r in (ls0_ref, ls1_ref, ls2_ref)]
    og = [r[0] for r in (og0_ref, og1_ref, og2_ref)]
    for gi, (_, dil) in enumerate(DIL_PATTERN):
        rows = tm // dil
        for r in range(dil):
            for c in range(2):
                sl = slice(c * 128, (c + 1) * 128)
                ob_ref[2 * gi + c, pl.ds(r, rows, stride=dil), :] = og[gi][r][:, sl].astype(F32)
                ob_ref[6 + 2 * gi + c, pl.ds(r, rows, stride=dil), :] = ls[gi][r][:, sl]

    def tok_major(k):
        return jnp.concatenate([ob_ref[2 * k], ob_ref[2 * k + 1]], axis=-1)

    l0, l1, l2 = tok_major(3), tok_major(4), tok_major(5)
    mx = jnp.maximum(jnp.maximum(l0, l1), l2)
    w0, w1, w2 = jnp.exp2(l0 - mx), jnp.exp2(l1 - mx), jnp.exp2(l2 - mx)
    ob = (w0 * tok_major(0) + w1 * tok_major(1) + w2 * tok_major(2)) / (w0 + w1 + w2)
    ya = jnp.dot(oa_ref[0, 0, 0], wa_ref[...], preferred_element_type=F32)
    yb = jnp.dot(ob.astype(BF16), wb_ref[...], preferred_element_type=F32)
    merged = (jax.nn.sigmoid(gates_ref[:, 0:D_MODEL].astype(F32)) * ya
              + jax.nn.sigmoid(gates_ref[:, D_MODEL:2 * D_MODEL].astype(F32)) * yb)
    x1 = x_ref[...] + jnp.dot(merged.astype(BF16), wo_ref[...], preferred_element_type=F32)
    x1_ref[...] = x1
    h2 = _rms(x1, gffn_ref[...])
    _rows_to_tiles(h2_ref, h2)
    lg_ref[...] = jnp.dot(h2.astype(BF16), wr_ref[...], preferred_element_type=F32) + br_ref[...]


def _post(x2d, oa, ogs, lss, main, wa, wb, wo, gffn, wr, br, B, S, tm, layer):
    T = x2d.shape[0]
    nt = S // tm
    res_specs = [pl.BlockSpec((1, d, tm // d, DIL_W), lambda i, nt=nt: (i // nt, 0, i % nt, 0))
                 for _, d in DIL_PATTERN]
    assert oa.shape[3] == tm and oa.shape[2] * 2 == nt

    def oa_index(i):
        it = i % nt
        late = it >= nt // 2
        return (i // nt, late.astype(jnp.int32), jnp.where(late, nt - 1 - it, it), 0, 0)

    return pl.pallas_call(
        functools.partial(_post_kernel, tm=tm),
        grid=(T // tm,),
        in_specs=[pl.BlockSpec((tm, D_MODEL), lambda i: (i, 0)),
                  pl.BlockSpec((1, 1, 1, tm, D_MODEL), oa_index)]
                 + res_specs + res_specs
                 + [pl.BlockSpec((tm, 2 * D_MODEL), lambda i: (i, 0)),
                    _layer_spec((D_MODEL, D_MODEL), layer),
                    _layer_spec((DIL_W, D_MODEL), layer),
                    _layer_spec((D_MODEL, D_MODEL), layer),
                    _layer_spec((1, D_MODEL), layer),
                    _layer_spec((D_MODEL, 128), layer),
                    _layer_spec((1, 128), layer)],
        out_specs=[pl.BlockSpec((tm, D_MODEL), lambda i: (i, 0)),
                   pl.BlockSpec((tm * ROW_TILE, 128), lambda i: (i, 0)),
                   pl.BlockSpec((tm, 128), lambda i: (i, 0))],
        out_shape=[jax.ShapeDtypeStruct((T, D_MODEL), F32),
                   jax.ShapeDtypeStruct((T * ROW_TILE, 128), U32),
                   jax.ShapeDtypeStruct((T, 128), F32)],
        scratch_shapes=[pltpu.VMEM((12, tm, 128), F32)],
        compiler_params=_cparams(("parallel",)),
        name="post_attn",
    )(x2d, oa, *ogs, *lss, main, wa, wb, wo, gffn, wr, br)


def _route_kernel(lg_ref, ri_ref, rw_ref, cnt_ref, carry_ref, *, tm):
    i = pl.program_id(0)

    @pl.when(i == 0)
    def _():
        carry_ref[...] = jnp.zeros(carry_ref.shape, F32)

    lg = lg_ref[...]
    lane = lax.broadcasted_iota(jnp.int32, lg.shape, 1)
    lane_f = lane.astype(F32)
    ninf = jnp.float32(-jnp.inf)

    def first_max(vals):
        vmax = jnp.max(vals, axis=-1, keepdims=True)
        idx = jnp.min(jnp.where(vals == vmax, lane_f, 128.0), axis=-1, keepdims=True)
        return vmax, idx.astype(jnp.int32)

    gl = jnp.where(lane < N_GROUPS, lg, ninf)
    gmax, g_sel = first_max(gl)
    p_g = 1.0 / jnp.sum(jnp.exp(gl - gmax), axis=-1, keepdims=True)
    lo = N_GROUPS + g_sel * EXPERTS_PER_GROUP
    el = jnp.where(jnp.logical_and(lane >= lo, lane < lo + EXPERTS_PER_GROUP), lg, ninf)
    v0, i0 = first_max(el)
    v1, i1 = first_max(jnp.where(lane == i0, ninf, el))
    t = jnp.exp(v1 - v0)
    w0 = p_g / (1.0 + t)
    w1 = p_g * t / (1.0 + t)
    e0 = i0 - N_GROUPS
    e1 = i1 - N_GROUPS
    hit0 = lane == e0
    hit1 = lane == e1
    oh = jnp.where(jnp.logical_or(hit0, hit1), 1.0, 0.0).astype(F32)
    r_i = lax.broadcasted_iota(jnp.int32, (tm, tm), 0)
    c_i = lax.broadcasted_iota(jnp.int32, (tm, tm), 1)
    lower = jnp.where(c_i < r_i, 1.0, 0.0).astype(BF16)
    excl = jnp.dot(lower, oh.astype(BF16), preferred_element_type=F32) + carry_ref[...]
    r0 = jnp.sum(jnp.where(hit0, excl, 0.0), axis=-1, keepdims=True).astype(jnp.int32)
    r1 = jnp.sum(jnp.where(hit1, excl, 0.0), axis=-1, keepdims=True).astype(jnp.int32)
    carry_ref[...] = carry_ref[...] + jnp.sum(oh, axis=0, keepdims=True)
    zi = jnp.zeros(lg.shape, jnp.int32)
    ri_ref[...] = jnp.where(lane == 0, e0, jnp.where(lane == 1, e1, jnp.where(lane == 2, r0, jnp.where(lane == 3, r1, zi))))
    rw_ref[...] = jnp.where(lane == 0, w0, jnp.where(lane == 1, w1, jnp.zeros(lg.shape, F32)))
    cnt_ref[...] = carry_ref[...]


def _route(lg, tm):
    T = lg.shape[0]
    return pl.pallas_call(
        functools.partial(_route_kernel, tm=tm),
        grid=(T // tm,),
        in_specs=[pl.BlockSpec((tm, 128), lambda i: (i, 0))],
        out_specs=[pl.BlockSpec((tm, 128), lambda i: (i, 0)),
                   pl.BlockSpec((tm, 128), lambda i: (i, 0)),
                   pl.BlockSpec((1, 128), lambda i: (0, 0))],
        out_shape=[jax.ShapeDtypeStruct((T, 128), jnp.int32),
                   jax.ShapeDtypeStruct((T, 128), F32),
                   jax.ShapeDtypeStruct((1, 128), F32)],
        scratch_shapes=[pltpu.VMEM((1, 128), F32)],
        compiler_params=_cparams(("arbitrary",)),
        name="route",
    )(lg)


def _sc_gather_rows(table, idx, n_out=None):
    inverse = n_out is not None
    n_src = idx.shape[0]
    n = n_out if inverse else n_src
    n_workers = SC_CORES * SC_SUBCORES
    per_w = n // n_workers
    n_chunks = per_w // SC_CHUNK
    assert per_w * n_workers == n and n_chunks * SC_CHUNK == per_w and n_chunks % 2 == 0
    mesh = plsc.VectorSubcoreMesh(core_axis_name="c", subcore_axis_name="s",
                                  num_cores=SC_CORES, num_subcores=SC_SUBCORES)

    @functools.partial(
        pl.kernel, mesh=mesh,
        out_type=jax.ShapeDtypeStruct((n,) + table.shape[1:], table.dtype),
        scratch_types=[pltpu.VMEM((per_w,), jnp.int32),
                       pltpu.VMEM((SC_CHUNK,) + table.shape[1:], table.dtype),
                       pltpu.VMEM((SC_CHUNK,) + table.shape[1:], table.dtype),
                       pltpu.SemaphoreType.DMA, pltpu.SemaphoreType.DMA,
                       pltpu.VMEM((n_src if inverse else SC_LANES,), jnp.int32)],
        compiler_params=pltpu.CompilerParams(use_tc_tiling_on_sc=True, needs_layout_passes=not inverse),
        name="sc_dispatch_rows" if inverse else "sc_gather_rows",
    )
    def gather(table_hbm, idx_hbm, out_hbm, idx_v, rows_a, rows_b, sem_a, sem_b, map_v):
        wid = lax.axis_index("s") * SC_CORES + lax.axis_index("c")
        base = wid * per_w
        if inverse:
            pltpu.sync_copy(idx_hbm, map_v)
            lanes = lax.iota(jnp.int32, SC_LANES)

            @pl.loop(0, per_w // SC_LANES)
            def _(j):
                idx_v[pl.ds(j * SC_LANES, SC_LANES)] = lax.rem(base + j * SC_LANES + lanes, table.shape[0])

            @pl.loop(0, n_src // SC_LANES)
            def _(a):
                local = map_v[pl.ds(a * SC_LANES, SC_LANES)] - base
                mine = jnp.logical_and(local >= 0, local < per_w)
                plsc.store_scatter(idx_v, [local], lax.div(a * SC_LANES + lanes, TOP_K), mask=mine)
        else:
            pltpu.sync_copy(idx_hbm.at[pl.ds(base, per_w)], idx_v)

        def fetch(chunk, rows_v, sem):
            return pltpu.make_async_copy(table_hbm.at[idx_v.at[pl.ds(chunk * SC_CHUNK, SC_CHUNK)]], rows_v, sem)

        def flush(chunk, rows_v):
            pltpu.sync_copy(rows_v, out_hbm.at[pl.ds(base + chunk * SC_CHUNK, SC_CHUNK)])

        fetch(0, rows_a, sem_a).start()

        @pl.loop(0, n_chunks, step=2)
        def _(c):
            fetch(c + 1, rows_b, sem_b).start()
            fetch(c, rows_a, sem_a).wait()
            flush(c, rows_a)

            @pl.when(c + 2 < n_chunks)
            def _():
                fetch(c + 2, rows_a, sem_a).start()

            fetch(c + 1, rows_b, sem_b).wait()
            flush(c + 1, rows_b)

    return gather(table, idx)


def _rows_from_tiles(ref, first, n_rows, stride=ROW_TILE):
    words = [ref[pl.ds(first + c, n_rows, stride=stride), :] for c in range(ROW_TILE)]
    lo = [pltpu.bitcast(w << 16, F32) for w in words]
    hi = [pltpu.bitcast(w & U32(0xFFFF0000), F32) for w in words]
    return jnp.concatenate(lo + hi, axis=-1)


def _rows_to_tiles(ref, val):
    half = D_MODEL // 2

    def bits(x):
        return pltpu.bitcast(x.astype(BF16).astype(F32), U32)

    for c in range(ROW_TILE):
        lo = bits(val[:, c * 128:(c + 1) * 128])
        hi = bits(val[:, half + c * 128:half + (c + 1) * 128])
        ref[pl.ds(c, val.shape[0], stride=ROW_TILE), :] = hi | (lo >> 16)


def _ffn_kernel(be_ref, nvb_ref, x_ref, w1_ref, w3_ref, w2_ref, y_ref, w1b_ref, w3b_ref, w2b_ref):
    b = pl.program_id(0)
    nvb = nvb_ref[0]
    new_expert = jnp.logical_or(b == 0, be_ref[b] != be_ref[jnp.maximum(b - 1, 0)])

    @pl.when(jnp.logical_and(b < nvb, new_expert))
    def _():
        w1b_ref[...] = w1_ref[0, 0].astype(BF16)
        w3b_ref[...] = w3_ref[0, 0].astype(BF16)
        w2b_ref[...] = w2_ref[0, 0].astype(BF16)

    @pl.when(b < nvb)
    def _():
        xb = _rows_from_tiles(x_ref, 0, FFN_BM).astype(BF16)
        h1 = jnp.dot(xb, w1b_ref[...], preferred_element_type=F32)
        h3 = jnp.dot(xb, w3b_ref[...], preferred_element_type=F32)
        a = (jax.nn.silu(h1) * h3).astype(BF16)
        _rows_to_tiles(y_ref, jnp.dot(a, w2b_ref[...], preferred_element_type=F32))

    @pl.when(b >= nvb)
    def _():
        y_ref[...] = jnp.zeros(y_ref.shape, U32)


def _expert_ffn(block_e, nvb, xs, w1, w3, w2, layer):
    nb = block_e.shape[0]

    def used(b, nv):
        return jnp.minimum(b, jnp.maximum(nv[0] - 1, 0))

    return pl.pallas_call(
        _ffn_kernel,
        grid_spec=pltpu.PrefetchScalarGridSpec(
            num_scalar_prefetch=2,
            grid=(nb,),
            in_specs=[
                pl.BlockSpec((FFN_BM * ROW_TILE, 128), lambda b, be, nv: (used(b, nv), 0)),
                pl.BlockSpec((1, 1, D_MODEL, EXPERT_FF), lambda b, be, nv: (layer, be[b], 0, 0)),
                pl.BlockSpec((1, 1, D_MODEL, EXPERT_FF), lambda b, be, nv: (layer, be[b], 0, 0)),
                pl.BlockSpec((1, 1, EXPERT_FF, D_MODEL), lambda b, be, nv: (layer, be[b], 0, 0)),
            ],
            out_specs=pl.BlockSpec((FFN_BM * ROW_TILE, 128), lambda b, be, nv: (b, 0)),
            scratch_shapes=[pltpu.VMEM((D_MODEL, EXPERT_FF), BF16), pltpu.VMEM((D_MODEL, EXPERT_FF), BF16),
                            pltpu.VMEM((EXPERT_FF, D_MODEL), BF16)],
        ),
        out_shape=jax.ShapeDtypeStruct((nb * FFN_BM * ROW_TILE, 128), U32),
        compiler_params=_cparams(("arbitrary",)),
        name="expert_ffn",
    )(block_e, nvb, xs, w1, w3, w2)


def _comb_kernel(yg_ref, x1_ref, rw_ref, p_ref, gple_ref, wpg_ref, wpp_ref, gout_ref, o_ref, *, tm, final):
    w = rw_ref[...]
    y0 = _rows_from_tiles(yg_ref, 0, tm, stride=TOP_K * ROW_TILE)
    y1 = _rows_from_tiles(yg_ref, ROW_TILE, tm, stride=TOP_K * ROW_TILE)
    x2 = x1_ref[...] + (y0 * w[:, 0:1] + y1 * w[:, 1:2])
    e = jnp.dot(p_ref[...].astype(BF16), wpp_ref[...], preferred_element_type=F32)
    gate = jax.nn.sigmoid(jnp.dot(_rms(x2, gple_ref[...]).astype(BF16), wpg_ref[...], preferred_element_type=F32))
    x3 = x2 + gate * e
    o_ref[...] = _rms(x3, gout_ref[...]) if final else x3


def _combine_ple(yg, x1, rw, p2d, gple, wpg, wpp, gout, tm, layer):
    T = x1.shape[0]
    nt = T // tm
    return pl.pallas_call(
        functools.partial(_comb_kernel, tm=tm, final=layer == DEPTH - 1),
        grid=(nt,),
        in_specs=[
            pl.BlockSpec((tm * TOP_K * ROW_TILE, 128), lambda i: (i, 0)),
            pl.BlockSpec((tm, D_MODEL), lambda i: (i, 0)),
            pl.BlockSpec((tm, 128), lambda i: (i, 0)),
            pl.BlockSpec((tm, PLE_DIM), lambda i: (layer * nt + i, 0)),
            _layer_spec((1, D_MODEL), layer),
            _layer_spec((D_MODEL, D_MODEL), layer),
            _layer_spec((PLE_DIM, D_MODEL), layer),
            pl.BlockSpec((1, D_MODEL), lambda i: (0, 0)),
        ],
        out_specs=pl.BlockSpec((tm, D_MODEL), lambda i: (i, 0)),
        out_shape=jax.ShapeDtypeStruct((T, D_MODEL), F32),
        compiler_params=_cparams(("parallel",)),
        name="combine_ple",
    )(yg, x1, rw, p2d, gple, wpg, wpp, gout)


def _rope_tables(pos, theta, rot_dim, period, offset, scale):
    half = rot_dim // 2
    inv = jnp.float32(theta) ** (-jnp.arange(half, dtype=F32) * 2.0 / rot_dim)
    rel = np.arange(128) % period - offset
    rot = (rel >= 0) & (rel < rot_dim)
    upper = rot & (rel >= half)
    lower = rot & (rel < half)
    ang = pos.astype(F32)[:, None] * inv[np.where(rot, rel % half, 0)][None, :]
    cos, sin = jnp.cos(ang), jnp.sin(ang)
    tabs = (jnp.where(rot, cos, 1.0), jnp.where(upper, sin, 0.0), jnp.where(lower, -sin, 0.0))
    return jnp.concatenate(tabs, axis=1) * jnp.float32(scale)


def _prep_w_in(w):
    c_q = w[..., 0:512]
    c_kv = w[..., 512:768]
    k_pe = w[..., 768:800]
    dil = w[..., 800:800 + 2304]
    gates = w[..., 3104:5152]
    zeros = lambda n: jnp.zeros(w.shape[:-1] + (n,), w.dtype)
    kslot = jnp.concatenate([zeros(MLA_NOPE), k_pe, zeros(256 - MLA_NOPE - MLA_ROPE)], axis=-1)
    return jnp.concatenate([gates, c_q, c_kv, kslot, dil], axis=-1).astype(BF16)


def _pad_heads(w, width):
    w = jnp.pad(w, ((0, 0), (0, 0), (0, 0), (0, MLA_SLOT - width)))
    return w.reshape(w.shape[0], w.shape[1], MLA_HEADS * MLA_SLOT).astype(BF16)


def _prep_w_q(w):
    return _pad_heads(w.reshape(DEPTH, MLA_Q_LORA, MLA_HEADS, MLA_NOPE + MLA_ROPE), MLA_NOPE + MLA_ROPE)


def _prep_w_kv(w):
    w = w.reshape(DEPTH, MLA_KV_LORA, MLA_HEADS, MLA_NOPE + MLA_V)
    return _pad_heads(w[..., :MLA_NOPE], MLA_NOPE), _pad_heads(w[..., MLA_NOPE:], MLA_V)


def _dest_kernel(ri_ref, ps_ref, o_ref):
    ri = ri_ref[...].astype(F32)
    lane = lax.broadcasted_iota(jnp.int32, ri.shape, 1)
    ps = ps_ref[...]

    def col(k):
        return jnp.sum(jnp.where(lane == k, ri, 0.0), axis=-1, keepdims=True)

    def dest(k):
        start = jnp.sum(jnp.where(lane == col(k).astype(jnp.int32), ps, 0.0), axis=-1, keepdims=True)
        return (start + col(2 + k)).astype(jnp.int32)

    o_ref[...] = jnp.where(lane == 0, dest(0), jnp.where(lane == 1, dest(1), 0))


def _dest_rows(ri, pstarts, tm):
    T = ri.shape[0]
    ps = jnp.zeros((1, 128), F32).at[0, :N_EXPERTS].set(pstarts.astype(F32))
    return pl.pallas_call(
        _dest_kernel,
        grid=(T // tm,),
        in_specs=[pl.BlockSpec((tm, 128), lambda i: (i, 0)), pl.BlockSpec((1, 128), lambda i: (0, 0))],
        out_specs=pl.BlockSpec((tm, 128), lambda i: (i, 0)),
        out_shape=jax.ShapeDtypeStruct((T, 128), jnp.int32),
        compiler_params=_cparams(("parallel",)),
        name="dest_rows",
    )(ri, ps)


def _dispatch_plan(ri, cnt, T, tm):
    counts = cnt[0, :N_EXPERTS].astype(jnp.int32)
    pcounts = (counts + FFN_BM - 1) // FFN_BM * FFN_BM
    pends = jnp.cumsum(pcounts)
    pstarts = pends - pcounts
    dest = _dest_rows(ri, pstarts, tm)[:, 0:TOP_K]
    n_blocks = (T * TOP_K) // FFN_BM + N_EXPERTS
    first_row = jnp.arange(n_blocks, dtype=jnp.int32) * FFN_BM
    block_e = jnp.minimum(jnp.sum((pends[None, :] <= first_row[:, None]).astype(jnp.int32), axis=1), N_EXPERTS - 1)
    nvb = (pends[-1] // FFN_BM).astype(jnp.int32).reshape(1)
    return dest.reshape(T * TOP_K), n_blocks * FFN_BM, block_e, nvb


def kernel(x, p, positions, g_mix, w_in, g_q_lat, w_q_up, g_kv_lat, w_kv_up, w_branch_a, w_branch_b, w_out, g_ffn, w_router_grp, b_router_grp, w_router_exp, b_router_exp, w_exp_gate, w_exp_up, w_exp_down, g_ple, w_ple_gate, w_ple_proj, g_final):
    B, S, D = x.shape
    T = B * S
    TM_IN, TM_QKV, TQ, TM_POST, TM_ROUTE, TM_COMB = 512, 512, 512, 512, 512, 256

    pos = positions.reshape(T)
    dil_scale = DIL_HEAD_DIM ** -0.5 * LOG2E
    mla_scale = (MLA_NOPE + MLA_ROPE) ** -0.5 * LOG2E
    tab_dil = jnp.concatenate([_rope_tables(pos, ROPE_THETA, PARTIAL_ROT, DIL_HEAD_DIM, 0, dil_scale),
                               _rope_tables(pos, ROPE_THETA, PARTIAL_ROT, DIL_HEAD_DIM, 0, 1.0)], axis=1)
    tab_mla = jnp.concatenate([_rope_tables(pos, MLA_ROPE_THETA, MLA_ROPE, MLA_SLOT, MLA_NOPE, mla_scale),
                               _rope_tables(pos, MLA_ROPE_THETA, MLA_ROPE, MLA_SLOT, MLA_NOPE, 1.0)], axis=1)

    w_in_p = _prep_w_in(w_in)
    wq = _prep_w_q(w_q_up)
    wk, wv = _prep_w_kv(w_kv_up)
    wr = jnp.concatenate([w_router_grp, w_router_exp, jnp.zeros((DEPTH, D, 128 - N_GROUPS - N_EXPERTS), F32)],
                         axis=-1).astype(BF16)
    br = jnp.concatenate([b_router_grp, b_router_exp.reshape(DEPTH, N_EXPERTS),
                          jnp.zeros((DEPTH, 128 - N_GROUPS - N_EXPERTS), F32)], axis=-1).reshape(DEPTH, 1, 128)
    wa, wb, wo = w_branch_a.astype(BF16), w_branch_b.astype(BF16), w_out.astype(BF16)
    wpg, wpp = w_ple_gate.astype(BF16), w_ple_proj.astype(BF16)
    gains = lambda g: g.reshape(DEPTH, 1, -1)

    xc = x.reshape(T, D)
    for i in range(DEPTH):
        main, d0, d1, d2 = _in_proj(xc, gains(g_mix), w_in_p, tab_dil, B, S, TM_IN, i)
        q, k, v = _mla_qkv(main, gains(g_q_lat), gains(g_kv_lat), wq, wk, wv, tab_mla, TM_QKV, i)
        o_a = _mla_attn(q, k, v, B, S, TQ)
        ogs, lss = [], []
        for (_, dil), qkv in zip(DIL_PATTERN, (d0, d1, d2)):
            og, ls = _dil_attn(qkv, B, dil, S // dil)
            ogs.append(og)
            lss.append(ls)
        x1, h2, lg = _post(xc, o_a, ogs, lss, main, wa, wb, wo, gains(g_ffn), wr, br, B, S, TM_POST, i)
        ri, rw, cnt = _route(lg, TM_ROUTE)
        dest, n_rows, block_e, nvb = _dispatch_plan(ri, cnt, T, TM_ROUTE)
        xs = _sc_gather_rows(h2.reshape(T, ROW_TILE, 128), dest, n_out=n_rows)
        ys = _expert_ffn(block_e, nvb, xs.reshape(-1, 128), w_exp_gate, w_exp_up, w_exp_down, i)
        yg = _sc_gather_rows(ys.reshape(-1, ROW_TILE, 128), dest)
        xc = _combine_ple(yg.reshape(-1, 128), x1, rw, p.reshape(DEPTH * T, PLE_DIM), gains(g_ple), wpg, wpp,
                          g_final.reshape(1, D), TM_COMB, i)
    return xc.reshape(B, S, D)
```

```python
import functools
import math

import jax
import jax.numpy as jnp
import numpy as np
from jax import lax
from jax.experimental import pallas as pl
from jax.experimental.pallas import tpu as pltpu
from jax.experimental.pallas import tpu_sc as plsc

F32 = jnp.float32
BF16 = jnp.bfloat16

D_MODEL = 1024
DEPTH = 4
RMS_EPS = 1e-6
NEG = -1e30
LOG2E = math.log2(math.e)

MLA_HEADS = 16
MLA_Q_LORA = 512
MLA_KV_LORA = 256
MLA_NOPE = 64
MLA_ROPE = 32
MLA_V = 64
MLA_ROPE_THETA = 10000.0
MLA_SLOT = 128

DIL_PATTERN = ((128, 1), (512, 4), (2048, 16))
DIL_GROUPS = 3
DIL_HEADS = 4
DIL_HEAD_DIM = 64
DIL_W = DIL_HEADS * DIL_HEAD_DIM
DIL_BLK = 128
DIL_UNROLL = 16
ROPE_THETA = 500000.0
PARTIAL_ROT = DIL_HEAD_DIM // 4

N_GROUPS = 8
EXPERTS_PER_GROUP = 8
N_EXPERTS = 64
TOP_K = 2
EXPERT_FF = 256
PLE_DIM = 256

IN_TN = 768
MAIN_COLS = 3072
IN_COLS_PAD = MAIN_COLS + DIL_GROUPS * 3 * DIL_W
N_MAIN_TILES = MAIN_COLS // IN_TN

FFN_BM = 256
ROW_TILE = D_MODEL // 256
U32 = jnp.uint32
VMEM_LIMIT = 48 * 1024 * 1024

SC_CORES = 2
SC_SUBCORES = 16
SC_LANES = 16
SC_CHUNK = 64


def _cparams(sem, bounds_checks=True):
    return pltpu.CompilerParams(dimension_semantics=sem, vmem_limit_bytes=VMEM_LIMIT,
                                disable_bounds_checks=not bounds_checks)


def _layer_spec(shape, layer, col=None):
    def index(*grid_idx):
        return (layer,) + (0,) * (len(shape) - 1) + ((col(*grid_idx),) if col else (0,))

    return pl.BlockSpec((None,) + tuple(shape), index)


def _rms(x, g):
    return x * lax.rsqrt(jnp.mean(x * x, axis=-1, keepdims=True) + RMS_EPS) * g


def _rope128(x, c, s_up, s_dn, half):
    return x * c + pltpu.roll(x, half, 1) * s_up + pltpu.roll(x, 128 - half, 1) * s_dn


def _in_kernel(x_ref, g_ref, w_ref, tab_ref, main_ref, d0_ref, d1_ref, d2_ref, xn_ref, acc_ref):
    j = pl.program_id(1)

    @pl.when(j == 0)
    def _():
        xn_ref[...] = _rms(x_ref[...], g_ref[...]).astype(BF16)

    acc = jnp.dot(xn_ref[...], w_ref[...], preferred_element_type=F32)

    @pl.when(j < N_MAIN_TILES)
    def _():
        main_ref[...] = acc.astype(BF16)

    def dil_tile(out_ref, dil):
        for c in range(4):
            t0 = 0 if c < 2 else 3
            x = acc[:, c * 128:(c + 1) * 128]
            acc_ref[c] = _rope128(
                x, tab_ref[:, t0 * 128:(t0 + 1) * 128], tab_ref[:, (t0 + 1) * 128:(t0 + 2) * 128],
                tab_ref[:, (t0 + 2) * 128:(t0 + 3) * 128], PARTIAL_ROT // 2)
        for c in range(4, 6):
            acc_ref[c] = acc[:, c * 128:(c + 1) * 128]
        rows = acc_ref.shape[1] // dil
        for r in range(dil):
            for c in range(6):
                out_ref[0, r, :, c * 128:(c + 1) * 128] = acc_ref[c, pl.ds(r, rows, stride=dil), :].astype(BF16)

    for gi, (_, dil) in enumerate(DIL_PATTERN):
        pl.when(j == N_MAIN_TILES + gi)(functools.partial(dil_tile, (d0_ref, d1_ref, d2_ref)[gi], dil))


def _in_proj(x2d, g, w, tab, B, S, tm, layer):
    T = x2d.shape[0]
    nt = S // tm
    dil_shapes = [jax.ShapeDtypeStruct((B, d, S // d, 3 * DIL_W), BF16) for _, d in DIL_PATTERN]
    dil_specs = [pl.BlockSpec((1, d, tm // d, 3 * DIL_W), lambda i, j, nt=nt: (i // nt, 0, i % nt, 0))
                 for _, d in DIL_PATTERN]
    return pl.pallas_call(
        _in_kernel,
        grid=(T // tm, IN_COLS_PAD // IN_TN),
        in_specs=[
            pl.BlockSpec((tm, D_MODEL), lambda i, j: (i, 0)),
            _layer_spec((1, D_MODEL), layer),
            _layer_spec((D_MODEL, IN_TN), layer, col=lambda i, j: j),
            pl.BlockSpec((tm, 6 * 128), lambda i, j: (i, 0)),
        ],
        out_specs=[pl.BlockSpec((tm, IN_TN), lambda i, j: (i, jnp.minimum(j, N_MAIN_TILES - 1)))] + dil_specs,
        out_shape=[jax.ShapeDtypeStruct((T, MAIN_COLS), BF16)] + dil_shapes,
        scratch_shapes=[pltpu.VMEM((tm, D_MODEL), BF16), pltpu.VMEM((IN_TN // 128, tm, 128), F32)],
        compiler_params=_cparams(("parallel", "arbitrary")),
        name="in_proj",
    )(x2d, g, w, tab)


def _qkv_kernel(cq_ref, ckv_ref, kpe_ref, gq_ref, gkv_ref, wq_ref, wk_ref, wv_ref, tab_ref,
                q_ref, k_ref, v_ref):
    qn = _rms(cq_ref[...].astype(F32), gq_ref[...]).astype(BF16)
    kvn = _rms(ckv_ref[...].astype(F32), gkv_ref[...]).astype(BF16)
    qacc = jnp.dot(qn, wq_ref[...], preferred_element_type=F32)
    kacc = jnp.dot(kvn, wk_ref[...], preferred_element_type=F32)
    vacc = jnp.dot(kvn, wv_ref[...], preferred_element_type=F32)
    lane_v = lax.broadcasted_iota(jnp.int32, vacc.shape, 1)
    v_ref[...] = jnp.where(lane_v % MLA_SLOT < MLA_V, vacc, 1.0).astype(BF16)
    half = MLA_ROPE // 2
    tq = [tab_ref[:, t * 128:(t + 1) * 128] for t in range(3)]
    tk = [tab_ref[:, t * 128:(t + 1) * 128] for t in range(3, 6)]
    krot = _rope128(kpe_ref[:, 0:128].astype(F32), tk[0], tk[1], tk[2], half)
    for h in range(MLA_HEADS):
        sl = slice(h * MLA_SLOT, (h + 1) * MLA_SLOT)
        q_ref[:, sl] = _rope128(qacc[:, sl], tq[0], tq[1], tq[2], half).astype(BF16)
        k_ref[:, sl] = (kacc[:, sl] + krot).astype(BF16)


def _mla_qkv(main, gq, gkv, wq, wk, wv, tab, tm, layer):
    T = main.shape[0]
    return pl.pallas_call(
        _qkv_kernel,
        grid=(T // tm,),
        in_specs=[
            pl.BlockSpec((tm, MLA_Q_LORA), lambda i: (i, 2048 // MLA_Q_LORA)),
            pl.BlockSpec((tm, MLA_KV_LORA), lambda i: (i, 2560 // MLA_KV_LORA)),
            pl.BlockSpec((tm, 256), lambda i: (i, 2816 // 256)),
            _layer_spec((1, MLA_Q_LORA), layer),
            _layer_spec((1, MLA_KV_LORA), layer),
            _layer_spec((MLA_Q_LORA, MLA_HEADS * MLA_SLOT), layer),
            _layer_spec((MLA_KV_LORA, MLA_HEADS * MLA_SLOT), layer),
            _layer_spec((MLA_KV_LORA, MLA_HEADS * MLA_SLOT), layer),
            pl.BlockSpec((tm, 6 * 128), lambda i: (i, 0)),
        ],
        out_specs=[
            pl.BlockSpec((tm, MLA_HEADS * MLA_SLOT), lambda i: (i, 0)),
            pl.BlockSpec((tm, MLA_HEADS * MLA_SLOT), lambda i: (i, 0)),
            pl.BlockSpec((tm, MLA_HEADS * MLA_SLOT), lambda i: (i, 0)),
        ],
        out_shape=[
            jax.ShapeDtypeStruct((T, MLA_HEADS * MLA_SLOT), BF16),
            jax.ShapeDtypeStruct((T, MLA_HEADS * MLA_SLOT), BF16),
            jax.ShapeDtypeStruct((T, MLA_HEADS * MLA_SLOT), BF16),
        ],
        compiler_params=_cparams(("parallel",)),
        name="mla_qkv",
    )(main, main, main, gq, gkv, wq, wk, wv, tab)


def _mla_attn_kernel(qa_ref, qb_ref, k_ref, v_ref, o_ref, m_ref, acc_ref, *, tq):
    p_id = pl.program_id(2)
    nq = k_ref.shape[1] // tq
    pair_w = 2 * MLA_SLOT
    lane = lax.broadcasted_iota(jnp.int32, (tq, pair_w), 1)
    qh = []
    for q_ref in (qa_ref, qb_ref):
        qp = q_ref[0]
        zero = jnp.zeros_like(qp)
        qh.append([jnp.where(lane < MLA_SLOT, qp, zero), jnp.where(lane >= MLA_SLOT, qp, zero)])
    hq = tq // 2
    dn = (((1,), (1,)), ((), ()))

    def step(blk, start, width, r0, mask, first=False):
        kb = k_ref[0, pl.ds(start, width), :]
        vb = v_ref[0, pl.ds(start, width), :]
        rows = tq - r0
        s_pair = lax.dot_general(jnp.concatenate([qh[blk][0][r0:], qh[blk][1][r0:]], axis=0), kb, dn,
                                 preferred_element_type=F32)
        for h in range(2):
            s = s_pair[h * rows:(h + 1) * rows]
            if mask is not None:
                s = jnp.where(mask, s, NEG)
            m_cur = jnp.max(s, axis=-1, keepdims=True)
            if first:
                m_new = jnp.broadcast_to(m_cur, (tq - r0, 128))
            else:
                m_prev = m_ref[blk, h, r0:, :]
                m_new = jnp.maximum(m_prev, m_cur)
            p = jnp.exp2(s - jnp.concatenate([m_new] * (width // 128), axis=-1))
            pv = jnp.dot(p.astype(BF16), vb[:, h * MLA_SLOT:(h + 1) * MLA_SLOT], preferred_element_type=F32)
            if first:
                acc_ref[blk, h, r0:, :] = pv
            else:
                acc_ref[blk, h, r0:, :] = jnp.exp2(m_prev - m_new) * acc_ref[blk, h, r0:, :] + pv
            m_ref[blk, h, r0:, :] = m_new

    def causal(rows):
        return lax.broadcasted_iota(jnp.int32, (rows, hq), 1) <= lax.broadcasted_iota(jnp.int32, (rows, hq), 0)

    def q_blocks(n_a):
        n_b = nq - 1 - n_a
        for blk, n_full in ((1, n_b), (0, n_a)):
            for j in range(n_full):
                step(blk, j * tq, tq, 0, None, first=j == 0)
            step(blk, n_full * tq, hq, 0, causal(tq), first=n_full == 0)
            step(blk, n_full * tq + hq, hq, hq, causal(hq))

    for n_a in range(nq // 2):
        pl.when(p_id == n_a)(functools.partial(q_blocks, n_a))
    lane_o = lax.broadcasted_iota(jnp.int32, (tq, 2 * MLA_V), 1)
    for blk in range(2):
        outs = []
        for h in range(2):
            a = acc_ref[blk, h]
            outs.append(a / pltpu.roll(a, MLA_V, 1))
        o_ref[0, blk, 0] = jnp.where(lane_o < MLA_V, outs[0], pltpu.roll(outs[1], MLA_V, 1)).astype(BF16)


def _mla_attn(q, k, v, B, S, tq):
    nq = S // tq
    q = q.reshape(B, S, MLA_HEADS * MLA_SLOT)
    k = k.reshape(B, S, MLA_HEADS * MLA_SLOT)
    v = v.reshape(B, S, MLA_HEADS * MLA_SLOT)
    return pl.pallas_call(
        functools.partial(_mla_attn_kernel, tq=tq),
        grid=(B, MLA_HEADS // 2, nq // 2),
        in_specs=[
            pl.BlockSpec((1, tq, 2 * MLA_SLOT), lambda b, h, p: (b, p, h)),
            pl.BlockSpec((1, tq, 2 * MLA_SLOT), lambda b, h, p: (b, nq - 1 - p, h)),
            pl.BlockSpec((1, S, 2 * MLA_SLOT), lambda b, h, p: (b, 0, h)),
            pl.BlockSpec((1, S, 2 * MLA_SLOT), lambda b, h, p: (b, 0, h)),
        ],
        out_specs=pl.BlockSpec((1, 2, 1, tq, 2 * MLA_V), lambda b, h, p: (b, 0, p, 0, h)),
        out_shape=jax.ShapeDtypeStruct((B, 2, nq // 2, tq, MLA_HEADS * MLA_V), BF16),
        scratch_shapes=[pltpu.VMEM((2, 2, tq, 128), F32), pltpu.VMEM((2, 2, tq, MLA_SLOT), F32)],
        compiler_params=_cparams(("parallel", "parallel", "arbitrary")),
        name="mla_attn",
    )(q, q, k, v)


def _dil_kernel(qkv_ref, o_ref, lse_ref, bias_ref, *, nb):
    dil = qkv_ref.shape[1]
    width = 2 * DIL_BLK if nb > 1 else DIL_BLK
    row = lax.broadcasted_iota(jnp.int32, (DIL_BLK, width), 0)
    col = lax.broadcasted_iota(jnp.int32, (DIL_BLK, width), 1)
    bias_ref[0] = jnp.where(col <= row, 0.0, NEG)
    if nb > 1:
        later = jnp.logical_or(jnp.logical_and(col >= DIL_BLK, col - DIL_BLK <= row),
                               jnp.logical_and(col < DIL_BLK, col >= row))
        bias_ref[1] = jnp.where(later, 0.0, NEG)
    head_of_lane = lax.broadcasted_iota(jnp.int32, (DIL_BLK, DIL_W), 1) // DIL_HEAD_DIM
    dn = (((1,), (1,)), ((), ()))

    def by_head(parts):
        out = parts[DIL_HEADS - 1]
        for h in range(DIL_HEADS - 2, -1, -1):
            out = jnp.where(head_of_lane == h, parts[h], out)
        return out

    def unit(u):
        r = u // nb
        n = u % nb
        q0 = pl.multiple_of(n * DIL_BLK, DIL_BLK)
        q = qkv_ref[0, r, pl.ds(q0, DIL_BLK), 0:DIL_W]
        if nb > 1:
            w0 = pl.multiple_of(jnp.maximum(n - 1, 0) * DIL_BLK, DIL_BLK)
            bias = bias_ref[jnp.minimum(n, 1)]
        else:
            w0 = 0
            bias = bias_ref[0]
        kw = qkv_ref[0, r, pl.ds(w0, width), DIL_W:2 * DIL_W]
        vw = qkv_ref[0, r, pl.ds(w0, width), 2 * DIL_W:3 * DIL_W]
        zero = jnp.zeros_like(q)
        qs = jnp.concatenate([jnp.where(head_of_lane == h, q, zero) for h in range(DIL_HEADS)], axis=0)
        s = lax.dot_general(qs, kw, dn, preferred_element_type=F32)
        s = (s.reshape(DIL_HEADS, DIL_BLK, width) + bias[None]).reshape(DIL_HEADS * DIL_BLK, width)
        m = jnp.max(s, axis=-1, keepdims=True)
        e = jnp.exp2(s - m)
        den = jnp.sum(e, axis=-1, keepdims=True)
        pv = jnp.dot(e.astype(BF16), vw, preferred_element_type=F32)
        lse = m + jnp.log2(den)
        blk = lambda t, h: t[h * DIL_BLK:(h + 1) * DIL_BLK]
        o = by_head([blk(pv, h) for h in range(DIL_HEADS)]) / by_head(
            [jnp.broadcast_to(blk(den, h), (DIL_BLK, DIL_W)) for h in range(DIL_HEADS)])
        o_ref[0, r, pl.ds(q0, DIL_BLK), :] = o.astype(BF16)
        lse_ref[0, r, pl.ds(q0, DIL_BLK), :] = by_head(
            [jnp.broadcast_to(blk(lse, h), (DIL_BLK, DIL_W)) for h in range(DIL_HEADS)])

    def body(t, carry):
        for u in range(DIL_UNROLL):
            unit(DIL_UNROLL * t + u)
        return carry

    lax.fori_loop(0, dil * nb // DIL_UNROLL, body, 0)


def _dil_attn(qkv, B, dil, L):
    nb = L // DIL_BLK
    return pl.pallas_call(
        functools.partial(_dil_kernel, nb=nb),
        grid=(B,),
        in_specs=[pl.BlockSpec((1, dil, L, 3 * DIL_W), lambda b: (b, 0, 0, 0))],
        out_specs=[
            pl.BlockSpec((1, dil, L, DIL_W), lambda b: (b, 0, 0, 0)),
            pl.BlockSpec((1, dil, L, DIL_W), lambda b: (b, 0, 0, 0)),
        ],
        out_shape=[
            jax.ShapeDtypeStruct((B, dil, L, DIL_W), BF16),
            jax.ShapeDtypeStruct((B, dil, L, DIL_W), F32),
        ],
        scratch_shapes=[pltpu.VMEM((2, DIL_BLK, 2 * DIL_BLK if nb > 1 else DIL_BLK), F32)],
        compiler_params=_cparams(("parallel",)),
        name=f"dil_attn_d{dil}",
    )(qkv)


def _post_kernel(x_ref, oa_ref, og0_ref, og1_ref, og2_ref, ls0_ref, ls1_ref, ls2_ref, gates_ref,
                 wa_ref, wb_ref, wo_ref, gffn_ref, wr_ref, br_ref,
                 x1_ref, h2_ref, lg_ref, ob_ref, *, tm):
    ls =[r[0] for r in (ls0_ref, ls1_ref, ls2_ref)]
    og = [r[0] for r in (og0_ref, og1_ref, og2_ref)]
    for gi, (_, dil) in enumerate(DIL_PATTERN):
        rows = tm // dil
        for r in range(dil):
            for c in range(2):
                sl = slice(c * 128, (c + 1) * 128)
                ob_ref[2 * gi + c, pl.ds(r, rows, stride=dil), :] = og[gi][r][:, sl].astype(F32)
                ob_ref[6 + 2 * gi + c, pl.ds(r, rows, stride=dil), :] = ls[gi][r][:, sl]

    def tok_major(k):
        return jnp.concatenate([ob_ref[2 * k], ob_ref[2 * k + 1]], axis=-1)

    l0, l1, l2 = tok_major(3), tok_major(4), tok_major(5)
    mx = jnp.maximum(jnp.maximum(l0, l1), l2)
    w0, w1, w2 = jnp.exp2(l0 - mx), jnp.exp2(l1 - mx), jnp.exp2(l2 - mx)
    ob = (w0 * tok_major(0) + w1 * tok_major(1) + w2 * tok_major(2)) / (w0 + w1 + w2)
    ya = jnp.dot(oa_ref[0, 0, 0], wa_ref[...], preferred_element_type=F32)
    yb = jnp.dot(ob.astype(BF16), wb_ref[...], preferred_element_type=F32)
    merged = (jax.nn.sigmoid(gates_ref[:, 0:D_MODEL].astype(F32)) * ya
              + jax.nn.sigmoid(gates_ref[:, D_MODEL:2 * D_MODEL].astype(F32)) * yb)
    x1 = x_ref[...] + jnp.dot(merged.astype(BF16), wo_ref[...], preferred_element_type=F32)
    x1_ref[...] = x1
    h2 = _rms(x1, gffn_ref[...])
    _rows_to_tiles(h2_ref, h2)
    lg_ref[...] = jnp.dot(h2.astype(BF16), wr_ref[...], preferred_element_type=F32) + br_ref[...]


def _post(x2d, oa, ogs, lss, main, wa, wb, wo, gffn, wr, br, B, S, tm, layer):
    T = x2d.shape[0]
    nt = S // tm
    res_specs = [pl.BlockSpec((1, d, tm // d, DIL_W), lambda i, nt=nt: (i // nt, 0, i % nt, 0))
                 for _, d in DIL_PATTERN]
    assert oa.shape[3] == tm and oa.shape[2] * 2 == nt

    def oa_index(i):
        it = i % nt
        late = it >= nt // 2
        return (i // nt, late.astype(jnp.int32), jnp.where(late, nt - 1 - it, it), 0, 0)

    return pl.pallas_call(
        functools.partial(_post_kernel, tm=tm),
        grid=(T // tm,),
        in_specs=[pl.BlockSpec((tm, D_MODEL), lambda i: (i, 0)),
                  pl.BlockSpec((1, 1, 1, tm, D_MODEL), oa_index)]
                 + res_specs + res_specs
                 + [pl.BlockSpec((tm, 2 * D_MODEL), lambda i: (i, 0)),
                    _layer_spec((D_MODEL, D_MODEL), layer),
                    _layer_spec((DIL_W, D_MODEL), layer),
                    _layer_spec((D_MODEL, D_MODEL), layer),
                    _layer_spec((1, D_MODEL), layer),
                    _layer_spec((D_MODEL, 128), layer),
                    _layer_spec((1, 128), layer)],
        out_specs=[pl.BlockSpec((tm, D_MODEL), lambda i: (i, 0)),
                   pl.BlockSpec((tm * ROW_TILE, 128), lambda i: (i, 0)),
                   pl.BlockSpec((tm, 128), lambda i: (i, 0))],
        out_shape=[jax.ShapeDtypeStruct((T, D_MODEL), F32),
                   jax.ShapeDtypeStruct((T * ROW_TILE, 128), U32),
                   jax.ShapeDtypeStruct((T, 128), F32)],
        scratch_shapes=[pltpu.VMEM((12, tm, 128), F32)],
        compiler_params=_cparams(("parallel",)),
        name="post_attn",
    )(x2d, oa, *ogs, *lss, main, wa, wb, wo, gffn, wr, br)


def _route_kernel(lg_ref, ri_ref, rw_ref, cnt_ref, carry_ref, lower_ref, *, tm):
    i = pl.program_id(0)

    @pl.when(i == 0)
    def _():
        carry_ref[...] = jnp.zeros(carry_ref.shape, F32)
        r_i = lax.broadcasted_iota(jnp.int32, (tm, tm), 0)
        c_i = lax.broadcasted_iota(jnp.int32, (tm, tm), 1)
        lower_ref[...] = jnp.where(c_i < r_i, 1.0, 0.0).astype(BF16)

    lg = lg_ref[...]
    lane = lax.broadcasted_iota(jnp.int32, lg.shape, 1)
    lane_f = lane.astype(F32)
    ninf = jnp.float32(-jnp.inf)

    def first_max(vals):
        vmax = jnp.max(vals, axis=-1, keepdims=True)
        idx = jnp.min(jnp.where(vals == vmax, lane_f, 128.0), axis=-1, keepdims=True)
        return vmax, idx.astype(jnp.int32)

    gl = jnp.where(lane < N_GROUPS, lg, ninf)
    gmax, g_sel = first_max(gl)
    p_g = 1.0 / jnp.sum(jnp.exp(gl - gmax), axis=-1, keepdims=True)
    lo = N_GROUPS + g_sel * EXPERTS_PER_GROUP
    el = jnp.where(jnp.logical_and(lane >= lo, lane < lo + EXPERTS_PER_GROUP), lg, ninf)
    v0, i0 = first_max(el)
    v1, i1 = first_max(jnp.where(lane == i0, ninf, el))
    t = jnp.exp(v1 - v0)
    w0 = p_g / (1.0 + t)
    w1 = p_g * t / (1.0 + t)
    e0 = i0 - N_GROUPS
    e1 = i1 - N_GROUPS
    hit0 = lane == e0
    hit1 = lane == e1
    oh = jnp.where(jnp.logical_or(hit0, hit1), 1.0, 0.0).astype(F32)
    excl = jnp.dot(lower_ref[...], oh.astype(BF16), preferred_element_type=F32) + carry_ref[...]
    r0 = jnp.sum(jnp.where(hit0, excl, 0.0), axis=-1, keepdims=True).astype(jnp.int32)
    r1 = jnp.sum(jnp.where(hit1, excl, 0.0), axis=-1, keepdims=True).astype(jnp.int32)
    carry_ref[...] = carry_ref[...] + jnp.sum(oh, axis=0, keepdims=True)
    zi = jnp.zeros(lg.shape, jnp.int32)
    ri_ref[...] = jnp.where(lane == 0, e0, jnp.where(lane == 1, e1, jnp.where(lane == 2, r0, jnp.where(lane == 3, r1, zi))))
    rw_ref[...] = jnp.where(lane == 0, w0, jnp.where(lane == 1, w1, jnp.zeros(lg.shape, F32)))
    cnt_ref[...] = carry_ref[...]


def _route(lg, tm):
    T = lg.shape[0]
    return pl.pallas_call(
        functools.partial(_route_kernel, tm=tm),
        grid=(T // tm,),
        in_specs=[pl.BlockSpec((tm, 128), lambda i: (i, 0))],
        out_specs=[pl.BlockSpec((tm, 128), lambda i: (i, 0)),
                   pl.BlockSpec((tm, 128), lambda i: (i, 0)),
                   pl.BlockSpec((1, 128), lambda i: (0, 0))],
        out_shape=[jax.ShapeDtypeStruct((T, 128), jnp.int32),
                   jax.ShapeDtypeStruct((T, 128), F32),
                   jax.ShapeDtypeStruct((1, 128), F32)],
        scratch_shapes=[pltpu.VMEM((1, 128), F32), pltpu.VMEM((tm, tm), BF16)],
        compiler_params=_cparams(("arbitrary",)),
        name="route",
    )(lg)


def _sc_gather_rows(table, idx, n_out=None):
    inverse = n_out is not None
    n_src = idx.shape[0]
    n = n_out if inverse else n_src
    n_workers = SC_CORES * SC_SUBCORES
    per_w = n // n_workers
    n_chunks = per_w // SC_CHUNK
    assert per_w * n_workers == n and n_chunks * SC_CHUNK == per_w and n_chunks % 2 == 0
    mesh = plsc.VectorSubcoreMesh(core_axis_name="c", subcore_axis_name="s",
                                  num_cores=SC_CORES, num_subcores=SC_SUBCORES)

    @functools.partial(
        pl.kernel, mesh=mesh,
        out_type=jax.ShapeDtypeStruct((n,) + table.shape[1:], table.dtype),
        scratch_types=[pltpu.VMEM((per_w,), jnp.int32),
                       pltpu.VMEM((SC_CHUNK,) + table.shape[1:], table.dtype),
                       pltpu.VMEM((SC_CHUNK,) + table.shape[1:], table.dtype),
                       pltpu.SemaphoreType.DMA, pltpu.SemaphoreType.DMA,
                       pltpu.VMEM((n_src if inverse else SC_LANES,), jnp.int32)],
        compiler_params=pltpu.CompilerParams(use_tc_tiling_on_sc=True, needs_layout_passes=not inverse),
        name="sc_dispatch_rows" if inverse else "sc_gather_rows",
    )
    def gather(table_hbm, idx_hbm, out_hbm, idx_v, rows_a, rows_b, sem_a, sem_b, map_v):
        wid = lax.axis_index("s") * SC_CORES + lax.axis_index("c")
        base = wid * per_w
        if inverse:
            pltpu.sync_copy(idx_hbm, map_v)
            lanes = lax.iota(jnp.int32, SC_LANES)

            @pl.loop(0, per_w // SC_LANES)
            def _(j):
                idx_v[pl.ds(j * SC_LANES, SC_LANES)] = lax.rem(base + j * SC_LANES + lanes, table.shape[0])

            @pl.loop(0, n_src // SC_LANES)
            def _(a):
                local = map_v[pl.ds(a * SC_LANES, SC_LANES)] - base
                mine = jnp.logical_and(local >= 0, local < per_w)
                plsc.store_scatter(idx_v, [local], lax.div(a * SC_LANES + lanes, TOP_K), mask=mine)
        else:
            pltpu.sync_copy(idx_hbm.at[pl.ds(base, per_w)], idx_v)

        def fetch(chunk, rows_v, sem):
            return pltpu.make_async_copy(table_hbm.at[idx_v.at[pl.ds(chunk * SC_CHUNK, SC_CHUNK)]], rows_v, sem)

        def flush(chunk, rows_v):
            pltpu.sync_copy(rows_v, out_hbm.at[pl.ds(base + chunk * SC_CHUNK, SC_CHUNK)])

        fetch(0, rows_a, sem_a).start()

        @pl.loop(0, n_chunks, step=2)
        def _(c):
            fetch(c + 1, rows_b, sem_b).start()
            fetch(c, rows_a, sem_a).wait()
            flush(c, rows_a)

            @pl.when(c + 2 < n_chunks)
            def _():
                fetch(c + 2, rows_a, sem_a).start()

            fetch(c + 1, rows_b, sem_b).wait()
            flush(c + 1, rows_b)

    return gather(table, idx)


def _rows_from_tiles(ref, first, n_rows, stride=ROW_TILE):
    words = [ref[pl.ds(first + c, n_rows, stride=stride), :] for c in range(ROW_TILE)]
    lo = [pltpu.bitcast(w << 16, F32) for w in words]
    hi = [pltpu.bitcast(w & U32(0xFFFF0000), F32) for w in words]
    return jnp.concatenate(lo + hi, axis=-1)


def _rows_to_tiles(ref, val):
    half = D_MODEL // 2

    def bits(x):
        return pltpu.bitcast(x.astype(BF16).astype(F32), U32)

    for c in range(ROW_TILE):
        lo = bits(val[:, c * 128:(c + 1) * 128])
        hi = bits(val[:, half + c * 128:half + (c + 1) * 128])
        ref[pl.ds(c, val.shape[0], stride=ROW_TILE), :] = hi | (lo >> 16)


def _ffn_kernel(be_ref, nvb_ref, x_ref, w1_ref, w3_ref, w2_ref, y_ref, w1b_ref, w3b_ref, w2b_ref):
    b = pl.program_id(0)
    nvb = nvb_ref[0]
    new_expert = jnp.logical_or(b == 0, be_ref[b] != be_ref[jnp.maximum(b - 1, 0)])

    @pl.when(jnp.logical_and(b < nvb, new_expert))
    def _():
        w1b_ref[...] = w1_ref[0, 0].astype(BF16)
        w3b_ref[...] = w3_ref[0, 0].astype(BF16)
        w2b_ref[...] = w2_ref[0, 0].astype(BF16)

    @pl.when(b < nvb)
    def _():
        xb = _rows_from_tiles(x_ref, 0, FFN_BM).astype(BF16)
        h1 = jnp.dot(xb, w1b_ref[...], preferred_element_type=F32)
        h3 = jnp.dot(xb, w3b_ref[...], preferred_element_type=F32)
        a = (jax.nn.silu(h1) * h3).astype(BF16)
        _rows_to_tiles(y_ref, jnp.dot(a, w2b_ref[...], preferred_element_type=F32))

    @pl.when(b >= nvb)
    def _():
        y_ref[...] = jnp.zeros(y_ref.shape, U32)


def _expert_ffn(block_e, nvb, xs, w1, w3, w2, layer):
    nb = block_e.shape[0]

    def used(b, nv):
        return jnp.minimum(b, jnp.maximum(nv[0] - 1, 0))

    return pl.pallas_call(
        _ffn_kernel,
        grid_spec=pltpu.PrefetchScalarGridSpec(
            num_scalar_prefetch=2,
            grid=(nb,),
            in_specs=[
                pl.BlockSpec((FFN_BM * ROW_TILE, 128), lambda b, be, nv: (used(b, nv), 0)),
                pl.BlockSpec((1, 1, D_MODEL, EXPERT_FF), lambda b, be, nv: (layer, be[b], 0, 0)),
                pl.BlockSpec((1, 1, D_MODEL, EXPERT_FF), lambda b, be, nv: (layer, be[b], 0, 0)),
                pl.BlockSpec((1, 1, EXPERT_FF, D_MODEL), lambda b, be, nv: (layer, be[b], 0, 0)),
            ],
            out_specs=pl.BlockSpec((FFN_BM * ROW_TILE, 128), lambda b, be, nv: (b, 0)),
            scratch_shapes=[pltpu.VMEM((D_MODEL, EXPERT_FF), BF16), pltpu.VMEM((D_MODEL, EXPERT_FF), BF16),
                            pltpu.VMEM((EXPERT_FF, D_MODEL), BF16)],
        ),
        out_shape=jax.ShapeDtypeStruct((nb * FFN_BM * ROW_TILE, 128), U32),
        compiler_params=_cparams(("arbitrary",)),
        name="expert_ffn",
    )(block_e, nvb, xs, w1, w3, w2)


def _comb_kernel(yg_ref, x1_ref, rw_ref, p_ref, gple_ref, wpg_ref, wpp_ref, gout_ref, o_ref, *, tm, final):
    w = rw_ref[...]
    y0 = _rows_from_tiles(yg_ref, 0, tm, stride=TOP_K * ROW_TILE)
    y1 = _rows_from_tiles(yg_ref, ROW_TILE, tm, stride=TOP_K * ROW_TILE)
    x2 = x1_ref[...] + (y0 * w[:, 0:1] + y1 * w[:, 1:2])
    e = jnp.dot(p_ref[...].astype(BF16), wpp_ref[...], preferred_element_type=F32)
    gate = jax.nn.sigmoid(jnp.dot(_rms(x2, gple_ref[...]).astype(BF16), wpg_ref[...], preferred_element_type=F32))
    x3 = x2 + gate * e
    o_ref[...] = _rms(x3, gout_ref[...]) if final else x3


def _combine_ple(yg, x1, rw, p2d, gple, wpg, wpp, gout, tm, layer):
    T = x1.shape[0]
    nt = T // tm
    return pl.pallas_call(
        functools.partial(_comb_kernel, tm=tm, final=layer == DEPTH - 1),
        grid=(nt,),
        in_specs=[
            pl.BlockSpec((tm * TOP_K * ROW_TILE, 128), lambda i: (i, 0)),
            pl.BlockSpec((tm, D_MODEL), lambda i: (i, 0)),
            pl.BlockSpec((tm, 128), lambda i: (i, 0)),
            pl.BlockSpec((tm, PLE_DIM), lambda i: (layer * nt + i, 0)),
            _layer_spec((1, D_MODEL), layer),
            _layer_spec((D_MODEL, D_MODEL), layer),
            _layer_spec((PLE_DIM, D_MODEL), layer),
            pl.BlockSpec((1, D_MODEL), lambda i: (0, 0)),
        ],
        out_specs=pl.BlockSpec((tm, D_MODEL), lambda i: (i, 0)),
        out_shape=jax.ShapeDtypeStruct((T, D_MODEL), F32),
        compiler_params=_cparams(("parallel",)),
        name="combine_ple",
    )(yg, x1, rw, p2d, gple, wpg, wpp, gout)


def _rope_tables(pos, theta, rot_dim, period, offset, scale):
    half = rot_dim // 2
    inv = jnp.float32(theta) ** (-jnp.arange(half, dtype=F32) * 2.0 / rot_dim)
    rel = np.arange(128) % period - offset
    rot = (rel >= 0) & (rel < rot_dim)
    upper = rot & (rel >= half)
    lower = rot & (rel < half)
    ang = pos.astype(F32)[:, None] * inv[np.where(rot, rel % half, 0)][None, :]
    cos, sin = jnp.cos(ang), jnp.sin(ang)
    tabs = (jnp.where(rot, cos, 1.0), jnp.where(upper, sin, 0.0), jnp.where(lower, -sin, 0.0))
    return jnp.concatenate(tabs, axis=1) * jnp.float32(scale)


def _prep_w_in(w):
    c_q = w[..., 0:512]
    c_kv = w[..., 512:768]
    k_pe = w[..., 768:800]
    dil = w[..., 800:800 + 2304]
    gates = w[..., 3104:5152]
    zeros = lambda n: jnp.zeros(w.shape[:-1] + (n,), w.dtype)
    kslot = jnp.concatenate([zeros(MLA_NOPE), k_pe, zeros(256 - MLA_NOPE - MLA_ROPE)], axis=-1)
    return jnp.concatenate([gates, c_q, c_kv, kslot, dil], axis=-1).astype(BF16)


def _pad_heads(w, width):
    w = jnp.pad(w, ((0, 0), (0, 0), (0, 0), (0, MLA_SLOT - width)))
    return w.reshape(w.shape[0], w.shape[1], MLA_HEADS * MLA_SLOT).astype(BF16)


def _prep_w_q(w):
    return _pad_heads(w.reshape(DEPTH, MLA_Q_LORA, MLA_HEADS, MLA_NOPE + MLA_ROPE), MLA_NOPE + MLA_ROPE)


def _prep_w_kv(w):
    w = w.reshape(DEPTH, MLA_KV_LORA, MLA_HEADS, MLA_NOPE + MLA_V)
    return _pad_heads(w[..., :MLA_NOPE], MLA_NOPE), _pad_heads(w[..., MLA_NOPE:], MLA_V)


def _dest_kernel(ri_ref, ps_ref, o_ref):
    ri = ri_ref[...].astype(F32)
    lane = lax.broadcasted_iota(jnp.int32, ri.shape, 1)
    ps = ps_ref[...]

    def col(k):
        return jnp.sum(jnp.where(lane == k, ri, 0.0), axis=-1, keepdims=True)

    def dest(k):
        start = jnp.sum(jnp.where(lane == col(k).astype(jnp.int32), ps, 0.0), axis=-1, keepdims=True)
        return (start + col(2 + k)).astype(jnp.int32)

    o_ref[...] = jnp.where(lane == 0, dest(0), jnp.where(lane == 1, dest(1), 0))


def _dest_rows(ri, pstarts, tm):
    T = ri.shape[0]
    ps = jnp.zeros((1, 128), F32).at[0, :N_EXPERTS].set(pstarts.astype(F32))
    return pl.pallas_call(
        _dest_kernel,
        grid=(T // tm,),
        in_specs=[pl.BlockSpec((tm, 128), lambda i: (i, 0)), pl.BlockSpec((1, 128), lambda i: (0, 0))],
        out_specs=pl.BlockSpec((tm, 128), lambda i: (i, 0)),
        out_shape=jax.ShapeDtypeStruct((T, 128), jnp.int32),
        compiler_params=_cparams(("parallel",)),
        name="dest_rows",
    )(ri, ps)


def _dispatch_plan(ri, cnt, T, tm):
    counts = cnt[0, :N_EXPERTS].astype(jnp.int32)
    pcounts = (counts + FFN_BM - 1) // FFN_BM * FFN_BM
    pends = jnp.cumsum(pcounts)
    pstarts = pends - pcounts
    dest = _dest_rows(ri, pstarts, tm)[:, 0:TOP_K]
    n_blocks = (T * TOP_K) // FFN_BM + N_EXPERTS
    first_row = jnp.arange(n_blocks, dtype=jnp.int32) * FFN_BM
    block_e = jnp.minimum(jnp.sum((pends[None, :] <= first_row[:, None]).astype(jnp.int32), axis=1), N_EXPERTS - 1)
    nvb = (pends[-1] // FFN_BM).astype(jnp.int32).reshape(1)
    return dest.reshape(T * TOP_K), n_blocks * FFN_BM, block_e, nvb


def kernel(x, p, positions, g_mix, w_in, g_q_lat, w_q_up, g_kv_lat, w_kv_up, w_branch_a, w_branch_b, w_out, g_ffn, w_router_grp, b_router_grp, w_router_exp, b_router_exp, w_exp_gate, w_exp_up, w_exp_down, g_ple, w_ple_gate, w_ple_proj, g_final):
    B, S, D = x.shape
    T = B * S
    TM_IN, TM_QKV, TQ, TM_POST, TM_ROUTE, TM_COMB = 512, 512, 512, 512, 512, 256

    pos = positions.reshape(T)
    dil_scale = DIL_HEAD_DIM ** -0.5 * LOG2E
    mla_scale = (MLA_NOPE + MLA_ROPE) ** -0.5 * LOG2E
    tab_dil = jnp.concatenate([_rope_tables(pos, ROPE_THETA, PARTIAL_ROT, DIL_HEAD_DIM, 0, dil_scale),
                               _rope_tables(pos, ROPE_THETA, PARTIAL_ROT, DIL_HEAD_DIM, 0, 1.0)], axis=1)
    tab_mla = jnp.concatenate([_rope_tables(pos, MLA_ROPE_THETA, MLA_ROPE, MLA_SLOT, MLA_NOPE, mla_scale),
                               _rope_tables(pos, MLA_ROPE_THETA, MLA_ROPE, MLA_SLOT, MLA_NOPE, 1.0)], axis=1)

    w_in_p = _prep_w_in(w_in)
    wq = _prep_w_q(w_q_up)
    wk, wv = _prep_w_kv(w_kv_up)
    wr = jnp.concatenate([w_router_grp, w_router_exp, jnp.zeros((DEPTH, D, 128 - N_GROUPS - N_EXPERTS), F32)],
                         axis=-1).astype(BF16)
    br = jnp.concatenate([b_router_grp, b_router_exp.reshape(DEPTH, N_EXPERTS),
                          jnp.zeros((DEPTH, 128 - N_GROUPS - N_EXPERTS), F32)], axis=-1).reshape(DEPTH, 1, 128)
    wa, wb, wo = w_branch_a.astype(BF16), w_branch_b.astype(BF16), w_out.astype(BF16)
    wpg, wpp = w_ple_gate.astype(BF16), w_ple_proj.astype(BF16)
    gains = lambda g: g.reshape(DEPTH, 1, -1)

    xc = x.reshape(T, D)
    for i in range(DEPTH):
        main, d0, d1, d2 = _in_proj(xc, gains(g_mix), w_in_p, tab_dil, B, S, TM_IN, i)
        q, k, v = _mla_qkv(main, gains(g_q_lat), gains(g_kv_lat), wq, wk, wv, tab_mla, TM_QKV, i)
        o_a = _mla_attn(q, k, v, B, S, TQ)
        ogs, lss = [], []
        for (_, dil), qkv in zip(DIL_PATTERN, (d0, d1, d2)):
            og, ls = _dil_attn(qkv, B, dil, S // dil)
            ogs.append(og)
            lss.append(ls)
        x1, h2, lg = _post(xc, o_a, ogs, lss, main, wa, wb, wo, gains(g_ffn), wr, br, B, S, TM_POST, i)
        ri, rw, cnt = _route(lg, TM_ROUTE)
        dest, n_rows, block_e, nvb = _dispatch_plan(ri, cnt, T, TM_ROUTE)
        xs = _sc_gather_rows(h2.reshape(T, ROW_TILE, 128), dest, n_out=n_rows)
        ys = _expert_ffn(block_e, nvb, xs.reshape(-1, 128), w_exp_gate, w_exp_up, w_exp_down, i)
        yg = _sc_gather_rows(ys.reshape(-1, ROW_TILE, 128), dest)
        xc = _combine_ple(yg.reshape(-1, 128), x1, rw, p.reshape(DEPTH * T, PLE_DIM), gains(g_ple), wpg, wpp,
                          g_final.reshape(1, D), TM_COMB, i)
    return xc.reshape(B, S, D)
```

```python
import functools
import math

import jax
import jax.numpy as jnp
import numpy as np
from jax import lax
from jax.experimental import pallas as pl
from jax.experimental.pallas import tpu as pltpu
from jax.experimental.pallas import tpu_sc as plsc

F32 = jnp.float32
BF16 = jnp.bfloat16

D_MODEL = 1024
DEPTH = 4
RMS_EPS = 1e-6
NEG = -1e30
LOG2E = math.log2(math.e)

MLA_HEADS = 16
MLA_Q_LORA = 512
MLA_KV_LORA = 256
MLA_NOPE = 64
MLA_ROPE = 32
MLA_V = 64
MLA_ROPE_THETA = 10000.0
MLA_SLOT = 128

DIL_PATTERN = ((128, 1), (512, 4), (2048, 16))
DIL_GROUPS = 3
DIL_HEADS = 4
DIL_HEAD_DIM = 64
DIL_W = DIL_HEADS * DIL_HEAD_DIM
DIL_BLK = 128
DIL_UNROLL = 16
ROPE_THETA = 500000.0
PARTIAL_ROT = DIL_HEAD_DIM // 4

N_GROUPS = 8
EXPERTS_PER_GROUP = 8
N_EXPERTS = 64
TOP_K = 2
EXPERT_FF = 256
PLE_DIM = 256

IN_TN = 768
MAIN_COLS = 3072
IN_COLS_PAD = MAIN_COLS + DIL_GROUPS * 3 * DIL_W
N_MAIN_TILES = MAIN_COLS // IN_TN

FFN_BM = 256
ROW_TILE = D_MODEL // 256
U32 = jnp.uint32
VMEM_LIMIT = 48 * 1024 * 1024

SC_CORES = 2
SC_SUBCORES = 16
SC_LANES = 16
SC_CHUNK = 64


def _cparams(sem, bounds_checks=True):
    return pltpu.CompilerParams(dimension_semantics=sem, vmem_limit_bytes=VMEM_LIMIT,
                                disable_bounds_checks=not bounds_checks)


def _layer_spec(shape, layer, col=None):
    def index(*grid_idx):
        return (layer,) + (0,) * (len(shape) - 1) + ((col(*grid_idx),) if col else (0,))

    return pl.BlockSpec((None,) + tuple(shape), index)


def _rms(x, g):
    return x * lax.rsqrt(jnp.mean(x * x, axis=-1, keepdims=True) + RMS_EPS) * g


def _rope128(x, c, s_up, s_dn, half):
    return x * c + pltpu.roll(x, half, 1) * s_up + pltpu.roll(x, 128 - half, 1) * s_dn


def _in_kernel(x_ref, g_ref, w_ref, tab_ref, main_ref, d0_ref, d1_ref, d2_ref, xn_ref, acc_ref):
    j = pl.program_id(1)

    @pl.when(j == 0)
    def _():
        xn_ref[...] = _rms(x_ref[...], g_ref[...]).astype(BF16)

    acc = jnp.dot(xn_ref[...], w_ref[...], preferred_element_type=F32)

    @pl.when(j < N_MAIN_TILES)
    def _():
        main_ref[...] = acc.astype(BF16)

    def dil_tile(out_ref, dil):
        for c in range(4):
            t0 = 0 if c < 2 else 3
            x = acc[:, c * 128:(c + 1) * 128]
            acc_ref[c] = _rope128(
                x, tab_ref[:, t0 * 128:(t0 + 1) * 128], tab_ref[:, (t0 + 1) * 128:(t0 + 2) * 128],
                tab_ref[:, (t0 + 2) * 128:(t0 + 3) * 128], PARTIAL_ROT // 2)
        for c in range(4, 6):
            acc_ref[c] = acc[:, c * 128:(c + 1) * 128]
        rows = acc_ref.shape[1] // dil
        for r in range(dil):
            for c in range(6):
                out_ref[0, r, :, c * 128:(c + 1) * 128] = acc_ref[c, pl.ds(r, rows, stride=dil), :].astype(BF16)

    for gi, (_, dil) in enumerate(DIL_PATTERN):
        pl.when(j == N_MAIN_TILES + gi)(functools.partial(dil_tile, (d0_ref, d1_ref, d2_ref)[gi], dil))


def _in_proj(x2d, g, w, tab, B, S, tm, layer):
    T = x2d.shape[0]
    nt = S // tm
    dil_shapes = [jax.ShapeDtypeStruct((B, d, S // d, 3 * DIL_W), BF16) for _, d in DIL_PATTERN]
    dil_specs = [pl.BlockSpec((1, d, tm // d, 3 * DIL_W), lambda i, j, nt=nt: (i // nt, 0, i % nt, 0))
                 for _, d in DIL_PATTERN]
    return pl.pallas_call(
        _in_kernel,
        grid=(T // tm, IN_COLS_PAD // IN_TN),
        in_specs=[
            pl.BlockSpec((tm, D_MODEL), lambda i, j: (i, 0)),
            _layer_spec((1, D_MODEL), layer),
            _layer_spec((D_MODEL, IN_TN), layer, col=lambda i, j: j),
            pl.BlockSpec((tm, 6 * 128), lambda i, j: (i, 0)),
        ],
        out_specs=[pl.BlockSpec((tm, IN_TN), lambda i, j: (i, jnp.minimum(j, N_MAIN_TILES - 1)))] + dil_specs,
        out_shape=[jax.ShapeDtypeStruct((T, MAIN_COLS), BF16)] + dil_shapes,
        scratch_shapes=[pltpu.VMEM((tm, D_MODEL), BF16), pltpu.VMEM((IN_TN // 128, tm, 128), F32)],
        compiler_params=_cparams(("parallel", "arbitrary")),
        name="in_proj",
    )(x2d, g, w, tab)


def _qkv_kernel(cq_ref, ckv_ref, kpe_ref, gq_ref, gkv_ref, wq_ref, wk_ref, wv_ref, tab_ref,
                q_ref, k_ref, v_ref):
    qn = _rms(cq_ref[...].astype(F32), gq_ref[...]).astype(BF16)
    kvn = _rms(ckv_ref[...].astype(F32), gkv_ref[...]).astype(BF16)
    qacc = jnp.dot(qn, wq_ref[...], preferred_element_type=F32)
    kacc = jnp.dot(kvn, wk_ref[...], preferred_element_type=F32)
    vacc = jnp.dot(kvn, wv_ref[...], preferred_element_type=F32)
    lane_v = lax.broadcasted_iota(jnp.int32, vacc.shape, 1)
    v_ref[...] = jnp.where(lane_v % MLA_SLOT < MLA_V, vacc, 1.0).astype(BF16)
    half = MLA_ROPE // 2
    tq = [tab_ref[:, t * 128:(t + 1) * 128] for t in range(3)]
    tk = [tab_ref[:, t * 128:(t + 1) * 128] for t in range(3, 6)]
    krot = _rope128(kpe_ref[:, 0:128].astype(F32), tk[0], tk[1], tk[2], half)
    for h in range(MLA_HEADS):
        sl = slice(h * MLA_SLOT, (h + 1) * MLA_SLOT)
        q_ref[:, sl] = _rope128(qacc[:, sl], tq[0], tq[1], tq[2], half).astype(BF16)
        k_ref[:, sl] = (kacc[:, sl] + krot).astype(BF16)


def _mla_qkv(main, gq, gkv, wq, wk, wv, tab, tm, layer):
    T = main.shape[0]
    return pl.pallas_call(
        _qkv_kernel,
        grid=(T // tm,),
        in_specs=[
            pl.BlockSpec((tm, MLA_Q_LORA), lambda i: (i, 2048 // MLA_Q_LORA)),
            pl.BlockSpec((tm, MLA_KV_LORA), lambda i: (i, 2560 // MLA_KV_LORA)),
            pl.BlockSpec((tm, 256), lambda i: (i, 2816 // 256)),
            _layer_spec((1, MLA_Q_LORA), layer),
            _layer_spec((1, MLA_KV_LORA), layer),
            _layer_spec((MLA_Q_LORA, MLA_HEADS * MLA_SLOT), layer),
            _layer_spec((MLA_KV_LORA, MLA_HEADS * MLA_SLOT), layer),
            _layer_spec((MLA_KV_LORA, MLA_HEADS * MLA_SLOT), layer),
            pl.BlockSpec((tm, 6 * 128), lambda i: (i, 0)),
        ],
        out_specs=[
            pl.BlockSpec((tm, MLA_HEADS * MLA_SLOT), lambda i: (i, 0)),
            pl.BlockSpec((tm, MLA_HEADS * MLA_SLOT), lambda i: (i, 0)),
            pl.BlockSpec((tm, MLA_HEADS * MLA_SLOT), lambda i: (i, 0)),
        ],
        out_shape=[
            jax.ShapeDtypeStruct((T, MLA_HEADS * MLA_SLOT), BF16),
            jax.ShapeDtypeStruct((T, MLA_HEADS * MLA_SLOT), BF16),
            jax.ShapeDtypeStruct((T, MLA_HEADS * MLA_SLOT), BF16),
        ],
        compiler_params=_cparams(("parallel",)),
        name="mla_qkv",
    )(main, main, main, gq, gkv, wq, wk, wv, tab)


def _mla_attn_kernel(qa_ref, qb_ref, k_ref, v_ref, o_ref, m_ref, acc_ref, *, tq):
    p_id = pl.program_id(2)
    nq = k_ref.shape[1] // tq
    pair_w = 2 * MLA_SLOT
    lane = lax.broadcasted_iota(jnp.int32, (tq, pair_w), 1)
    qh = []
    for q_ref in (qa_ref, qb_ref):
        qp = q_ref[0]
        zero = jnp.zeros_like(qp)
        qh.append([jnp.where(lane < MLA_SLOT, qp, zero), jnp.where(lane >= MLA_SLOT, qp, zero)])
    hq = tq // 2
    dn = (((1,), (1,)), ((), ()))

    def step(blk, start, width, r0, mask, first=False):
        kb = k_ref[0, pl.ds(start, width), :]
        vb = v_ref[0, pl.ds(start, width), :]
        rows = tq - r0
        s_pair = lax.dot_general(jnp.concatenate([qh[blk][0][r0:], qh[blk][1][r0:]], axis=0), kb, dn,
                                 preferred_element_type=F32)
        for h in range(2):
            s = s_pair[h * rows:(h + 1) * rows]
            if mask is not None:
                s = jnp.where(mask, s, NEG)
            m_cur = jnp.max(s, axis=-1, keepdims=True)
            if first:
                m_new = jnp.broadcast_to(m_cur, (tq - r0, 128))
            else:
                m_prev = m_ref[blk, h, r0:, :]
                m_new = jnp.maximum(m_prev, m_cur)
            p = jnp.exp2(s - jnp.concatenate([m_new] * (width // 128), axis=-1))
            pv = jnp.dot(p.astype(BF16), vb[:, h * MLA_SLOT:(h + 1) * MLA_SLOT], preferred_element_type=F32)
            if first:
                acc_ref[blk, h, r0:, :] = pv
            else:
                acc_ref[blk, h, r0:, :] = jnp.exp2(m_prev - m_new) * acc_ref[blk, h, r0:, :] + pv
            m_ref[blk, h, r0:, :] = m_new

    def causal(rows):
        return lax.broadcasted_iota(jnp.int32, (rows, hq), 1) <= lax.broadcasted_iota(jnp.int32, (rows, hq), 0)

    def q_blocks(n_a):
        n_b = nq - 1 - n_a
        for blk, n_full in ((1, n_b), (0, n_a)):
            for j in range(n_full):
                step(blk, j * tq, tq, 0, None, first=j == 0)
            step(blk, n_full * tq, hq, 0, causal(tq), first=n_full == 0)
            step(blk, n_full * tq + hq, hq, hq, causal(hq))

    for n_a in range(nq // 2):
        pl.when(p_id == n_a)(functools.partial(q_blocks, n_a))
    lane_o = lax.broadcasted_iota(jnp.int32, (tq, 2 * MLA_V), 1)
    for blk in range(2):
        outs = []
        for h in range(2):
            a = acc_ref[blk, h]
            outs.append(a / pltpu.roll(a, MLA_V, 1))
        o_ref[0, blk, 0] = jnp.where(lane_o < MLA_V, outs[0], pltpu.roll(outs[1], MLA_V, 1)).astype(BF16)


def _mla_attn(q, k, v, B, S, tq):
    nq = S // tq
    q = q.reshape(B, S, MLA_HEADS * MLA_SLOT)
    k = k.reshape(B, S, MLA_HEADS * MLA_SLOT)
    v = v.reshape(B, S, MLA_HEADS * MLA_SLOT)
    return pl.pallas_call(
        functools.partial(_mla_attn_kernel, tq=tq),
        grid=(B, MLA_HEADS // 2, nq // 2),
        in_specs=[
            pl.BlockSpec((1, tq, 2 * MLA_SLOT), lambda b, h, p: (b, p, h)),
            pl.BlockSpec((1, tq, 2 * MLA_SLOT), lambda b, h, p: (b, nq - 1 - p, h)),
            pl.BlockSpec((1, S, 2 * MLA_SLOT), lambda b, h, p: (b, 0, h)),
            pl.BlockSpec((1, S, 2 * MLA_SLOT), lambda b, h, p: (b, 0, h)),
        ],
        out_specs=pl.BlockSpec((1, 2, 1, tq, 2 * MLA_V), lambda b, h, p: (b, 0, p, 0, h)),
        out_shape=jax.ShapeDtypeStruct((B, 2, nq // 2, tq, MLA_HEADS * MLA_V), BF16),
        scratch_shapes=[pltpu.VMEM((2, 2, tq, 128), F32), pltpu.VMEM((2, 2, tq, MLA_SLOT), F32)],
        compiler_params=_cparams(("parallel", "parallel", "arbitrary")),
        name="mla_attn",
    )(q, q, k, v)


def _dil_kernel(qkv_ref, o_ref, lse_ref, bias_ref, *, nb):
    dil = qkv_ref.shape[1]
    width = 2 * DIL_BLK if nb > 1 else DIL_BLK
    row = lax.broadcasted_iota(jnp.int32, (DIL_BLK, width), 0)
    col = lax.broadcasted_iota(jnp.int32, (DIL_BLK, width), 1)
    bias_ref[0] = jnp.where(col <= row, 0.0, NEG)
    if nb > 1:
        later = jnp.logical_or(jnp.logical_and(col >= DIL_BLK, col - DIL_BLK <= row),
                               jnp.logical_and(col < DIL_BLK, col >= row))
        bias_ref[1] = jnp.where(later, 0.0, NEG)
    head_of_lane = lax.broadcasted_iota(jnp.int32, (DIL_BLK, DIL_W), 1) // DIL_HEAD_DIM
    dn = (((1,), (1,)), ((), ()))

    def by_head(parts):
        out = parts[DIL_HEADS - 1]
        for h in range(DIL_HEADS - 2, -1, -1):
            out = jnp.where(head_of_lane == h, parts[h], out)
        return out

    def unit(u):
        r = u // nb
        n = u % nb
        q0 = pl.multiple_of(n * DIL_BLK, DIL_BLK)
        q = qkv_ref[0, r, pl.ds(q0, DIL_BLK), 0:DIL_W]
        if nb > 1:
            w0 = pl.multiple_of(jnp.maximum(n - 1, 0) * DIL_BLK, DIL_BLK)
            bias = bias_ref[jnp.minimum(n, 1)]
        else:
            w0 = 0
            bias = bias_ref[0]
        kw = qkv_ref[0, r, pl.ds(w0, width), DIL_W:2 * DIL_W]
        vw = qkv_ref[0, r, pl.ds(w0, width), 2 * DIL_W:3 * DIL_W]
        zero = jnp.zeros_like(q)
        qs = jnp.concatenate([jnp.where(head_of_lane == h, q, zero) for h in range(DIL_HEADS)], axis=0)
        s = lax.dot_general(qs, kw, dn, preferred_element_type=F32)
        s = (s.reshape(DIL_HEADS, DIL_BLK, width) + bias[None]).reshape(DIL_HEADS * DIL_BLK, width)
        m = jnp.max(s, axis=-1, keepdims=True)
        e = jnp.exp2(s - m)
        den = jnp.sum(e, axis=-1, keepdims=True)
        pv = jnp.dot(e.astype(BF16), vw, preferred_element_type=F32)
        lse = m + jnp.log2(den)
        blk = lambda t, h: t[h * DIL_BLK:(h + 1) * DIL_BLK]
        o = by_head([blk(pv, h) for h in range(DIL_HEADS)]) / by_head(
            [jnp.broadcast_to(blk(den, h), (DIL_BLK, DIL_W)) for h in range(DIL_HEADS)])
        o_ref[0, r, pl.ds(q0, DIL_BLK), :] = o.astype(BF16)
        lse_ref[0, r, pl.ds(q0, DIL_BLK), :] = by_head(
            [jnp.broadcast_to(blk(lse, h), (DIL_BLK, DIL_W)) for h in range(DIL_HEADS)])

    def body(t, carry):
        for u in range(DIL_UNROLL):
            unit(DIL_UNROLL * t + u)
        return carry

    lax.fori_loop(0, dil * nb // DIL_UNROLL, body, 0)


def _dil_attn(qkv, B, dil, L):
    nb = L // DIL_BLK
    return pl.pallas_call(
        functools.partial(_dil_kernel, nb=nb),
        grid=(B,),
        in_specs=[pl.BlockSpec((1, dil, L, 3 * DIL_W), lambda b: (b, 0, 0, 0))],
        out_specs=[
            pl.BlockSpec((1, dil, L, DIL_W), lambda b: (b, 0, 0, 0)),
            pl.BlockSpec((1, dil, L, DIL_W), lambda b: (b, 0, 0, 0)),
        ],
        out_shape=[
            jax.ShapeDtypeStruct((B, dil, L, DIL_W), BF16),
            jax.ShapeDtypeStruct((B, dil, L, DIL_W), F32),
        ],
        scratch_shapes=[pltpu.VMEM((2, DIL_BLK, 2 * DIL_BLK if nb > 1 else DIL_BLK), F32)],
        compiler_params=_cparams(("parallel",)),
        name=f"dil_attn_d{dil}",
    )(qkv)


def _post_kernel(x_ref, oa_ref, og0_ref, og1_ref, og2_ref, ls0_ref, ls1_ref, ls2_ref, gates_ref,
                 wa_ref, wb_ref, wo_ref, gffn_ref, wr_ref, br_ref,
                 x1_ref, h2_ref, lg_ref, ob_ref, *, tm):
    ls =[r[0] for r in (ls0_ref, ls1_ref, ls2_ref)]
    og = [r[0] for r in (og0_ref, og1_ref, og2_ref)]
    for gi, (_, dil) in enumerate(DIL_PATTERN):
        rows = tm // dil
        for r in range(dil):
            for c in range(2):
                sl = slice(c * 128, (c + 1) * 128)
                ob_ref[2 * gi + c, pl.ds(r, rows, stride=dil), :] = og[gi][r][:, sl].astype(F32)
                ob_ref[6 + 2 * gi + c, pl.ds(r, rows, stride=dil), :] = ls[gi][r][:, sl]

    def tok_major(k):
        return jnp.concatenate([ob_ref[2 * k], ob_ref[2 * k + 1]], axis=-1)

    l0, l1, l2 = tok_major(3), tok_major(4), tok_major(5)
    mx = jnp.maximum(jnp.maximum(l0, l1), l2)
    w0, w1, w2 = jnp.exp2(l0 - mx), jnp.exp2(l1 - mx), jnp.exp2(l2 - mx)
    ob = (w0 * tok_major(0) + w1 * tok_major(1) + w2 * tok_major(2)) / (w0 + w1 + w2)
    ya = jnp.dot(oa_ref[0, 0, 0], wa_ref[...], preferred_element_type=F32)
    yb = jnp.dot(ob.astype(BF16), wb_ref[...], preferred_element_type=F32)
    merged = (jax.nn.sigmoid(gates_ref[:, 0:D_MODEL].astype(F32)) * ya
              + jax.nn.sigmoid(gates_ref[:, D_MODEL:2 * D_MODEL].astype(F32)) * yb)
    x1 = x_ref[...] + jnp.dot(merged.astype(BF16), wo_ref[...], preferred_element_type=F32)
    x1_ref[...] = x1
    h2 = _rms(x1, gffn_ref[...])
    _rows_to_tiles(h2_ref, h2)
    lg_ref[...] = jnp.dot(h2.astype(BF16), wr_ref[...], preferred_element_type=F32) + br_ref[...]


def _post(x2d, oa, ogs, lss, main, wa, wb, wo, gffn, wr, br, B, S, tm, layer):
    T = x2d.shape[0]
    nt = S // tm
    res_specs = [pl.BlockSpec((1, d, tm // d, DIL_W), lambda i, nt=nt: (i // nt, 0, i % nt, 0))
                 for _, d in DIL_PATTERN]
    assert oa.shape[3] == tm and oa.shape[2] * 2 == nt

    def oa_index(i):
        it = i % nt
        late = it >= nt // 2
        return (i // nt, late.astype(jnp.int32), jnp.where(late, nt - 1 - it, it), 0, 0)

    return pl.pallas_call(
        functools.partial(_post_kernel, tm=tm),
        grid=(T // tm,),
        in_specs=[pl.BlockSpec((tm, D_MODEL), lambda i: (i, 0)),
                  pl.BlockSpec((1, 1, 1, tm, D_MODEL), oa_index)]
                 + res_specs + res_specs
                 + [pl.BlockSpec((tm, 2 * D_MODEL), lambda i: (i, 0)),
                    _layer_spec((D_MODEL, D_MODEL), layer),
                    _layer_spec((DIL_W, D_MODEL), layer),
                    _layer_spec((D_MODEL, D_MODEL), layer),
                    _layer_spec((1, D_MODEL), layer),
                    _layer_spec((D_MODEL, 128), layer),
                    _layer_spec((1, 128), layer)],
        out_specs=[pl.BlockSpec((tm, D_MODEL), lambda i: (i, 0)),
                   pl.BlockSpec((tm * ROW_TILE, 128), lambda i: (i, 0)),
                   pl.BlockSpec((tm, 128), lambda i: (i, 0))],
        out_shape=[jax.ShapeDtypeStruct((T, D_MODEL), F32),
                   jax.ShapeDtypeStruct((T * ROW_TILE, 128), U32),
                   jax.ShapeDtypeStruct((T, 128), F32)],
        scratch_shapes=[pltpu.VMEM((12, tm, 128), F32)],
        compiler_params=_cparams(("parallel",)),
        name="post_attn",
    )(x2d, oa, *ogs, *lss, main, wa, wb, wo, gffn, wr, br)


def _route_kernel(lg_ref, ri_ref, rw_ref, cnt_ref, carry_ref, lower_ref, *, tm):
    i = pl.program_id(0)

    @pl.when(i == 0)
    def _():
        carry_ref[...] = jnp.zeros(carry_ref.shape, F32)
        r_i = lax.broadcasted_iota(jnp.int32, (tm, tm), 0)
        c_i = lax.broadcasted_iota(jnp.int32, (tm, tm), 1)
        lower_ref[...] = jnp.where(c_i < r_i, 1.0, 0.0).astype(BF16)

    lg = lg_ref[...]
    lane = lax.broadcasted_iota(jnp.int32, lg.shape, 1)
    lane_f = lane.astype(F32)
    ninf = jnp.float32(-jnp.inf)

    def first_max(vals):
        vmax = jnp.max(vals, axis=-1, keepdims=True)
        idx = jnp.min(jnp.where(vals == vmax, lane_f, 128.0), axis=-1, keepdims=True)
        return vmax, idx.astype(jnp.int32)

    gl = jnp.where(lane < N_GROUPS, lg, ninf)
    gmax, g_sel = first_max(gl)
    p_g = 1.0 / jnp.sum(jnp.exp(gl - gmax), axis=-1, keepdims=True)
    lo = N_GROUPS + g_sel * EXPERTS_PER_GROUP
    el = jnp.where(jnp.logical_and(lane >= lo, lane < lo + EXPERTS_PER_GROUP), lg, ninf)
    v0, i0 = first_max(el)
    v1, i1 = first_max(jnp.where(lane == i0, ninf, el))
    t = jnp.exp(v1 - v0)
    w0 = p_g / (1.0 + t)
    w1 = p_g * t / (1.0 + t)
    e0 = i0 - N_GROUPS
    e1 = i1 - N_GROUPS
    hit0 = lane == e0
    hit1 = lane == e1
    oh = jnp.where(jnp.logical_or(hit0, hit1), 1.0, 0.0).astype(F32)
    excl = jnp.dot(lower_ref[...], oh.astype(BF16), preferred_element_type=F32) + carry_ref[...]
    r0 = jnp.sum(jnp.where(hit0, excl, 0.0), axis=-1, keepdims=True).astype(jnp.int32)
    r1 = jnp.sum(jnp.where(hit1, excl, 0.0), axis=-1, keepdims=True).astype(jnp.int32)
    carry_ref[...] = carry_ref[...] + jnp.sum(oh, axis=0, keepdims=True)
    zi = jnp.zeros(lg.shape, jnp.int32)
    ri_ref[...] = jnp.where(lane == 0, e0, jnp.where(lane == 1, e1, jnp.where(lane == 2, r0, jnp.where(lane == 3, r1, zi))))
    rw_ref[...] = jnp.where(lane == 0, w0, jnp.where(lane == 1, w1, jnp.zeros(lg.shape, F32)))
    cnt_ref[...] = carry_ref[...]


def _route(lg, tm):
    T = lg.shape[0]
    return pl.pallas_call(
        functools.partial(_route_kernel, tm=tm),
        grid=(T // tm,),
        in_specs=[pl.BlockSpec((tm, 128), lambda i: (i, 0))],
        out_specs=[pl.BlockSpec((tm, 128), lambda i: (i, 0)),
                   pl.BlockSpec((tm, 128), lambda i: (i, 0)),
                   pl.BlockSpec((1, 128), lambda i: (0, 0))],
        out_shape=[jax.ShapeDtypeStruct((T, 128), jnp.int32),
                   jax.ShapeDtypeStruct((T, 128), F32),
                   jax.ShapeDtypeStruct((1, 128), F32)],
        scratch_shapes=[pltpu.VMEM((1, 128), F32), pltpu.VMEM((tm, tm), BF16)],
        compiler_params=_cparams(("arbitrary",)),
        name="route",
    )(lg)


def _sc_gather_rows(table, idx, n_out=None):
    inverse = n_out is not None
    n_src = idx.shape[0]
    n = n_out if inverse else n_src
    n_workers = SC_CORES * SC_SUBCORES
    per_w = n // n_workers
    n_chunks = per_w // SC_CHUNK
    assert per_w * n_workers == n and n_chunks * SC_CHUNK == per_w and n_chunks % 2 == 0
    mesh = plsc.VectorSubcoreMesh(core_axis_name="c", subcore_axis_name="s",
                                  num_cores=SC_CORES, num_subcores=SC_SUBCORES)

    @functools.partial(
        pl.kernel, mesh=mesh,
        out_type=jax.ShapeDtypeStruct((n,) + table.shape[1:], table.dtype),
        scratch_types=[pltpu.VMEM((per_w,), jnp.int32),
                       pltpu.VMEM((SC_CHUNK,) + table.shape[1:], table.dtype),
                       pltpu.VMEM((SC_CHUNK,) + table.shape[1:], table.dtype),
                       pltpu.SemaphoreType.DMA, pltpu.SemaphoreType.DMA,
                       pltpu.VMEM((n_src if inverse else SC_LANES,), jnp.int32)],
        compiler_params=pltpu.CompilerParams(use_tc_tiling_on_sc=True, needs_layout_passes=not inverse),
        name="sc_dispatch_rows" if inverse else "sc_gather_rows",
    )
    def gather(table_hbm, idx_hbm, out_hbm, idx_v, rows_a, rows_b, sem_a, sem_b, map_v):
        wid = lax.axis_index("s") * SC_CORES + lax.axis_index("c")
        base = wid * per_w
        if inverse:
            pltpu.sync_copy(idx_hbm, map_v)
            lanes = lax.iota(jnp.int32, SC_LANES)

            @pl.loop(0, per_w // SC_LANES)
            def _(j):
                idx_v[pl.ds(j * SC_LANES, SC_LANES)] = lax.rem(base + j * SC_LANES + lanes, table.shape[0])

            @pl.loop(0, n_src // SC_LANES)
            def _(a):
                local = map_v[pl.ds(a * SC_LANES, SC_LANES)] - base
                mine = jnp.logical_and(local >= 0, local < per_w)
                plsc.store_scatter(idx_v, [local], lax.div(a * SC_LANES + lanes, TOP_K), mask=mine)
        else:
            pltpu.sync_copy(idx_hbm.at[pl.ds(base, per_w)], idx_v)

        def fetch(chunk, rows_v, sem):
            return pltpu.make_async_copy(table_hbm.at[idx_v.at[pl.ds(chunk * SC_CHUNK, SC_CHUNK)]], rows_v, sem)

        def flush(chunk, rows_v):
            pltpu.sync_copy(rows_v, out_hbm.at[pl.ds(base + chunk * SC_CHUNK, SC_CHUNK)])

        fetch(0, rows_a, sem_a).start()

        @pl.loop(0, n_chunks, step=2)
        def _(c):
            fetch(c + 1, rows_b, sem_b).start()
            fetch(c, rows_a, sem_a).wait()
            flush(c, rows_a)

            @pl.when(c + 2 < n_chunks)
            def _():
                fetch(c + 2, rows_a, sem_a).start()

            fetch(c + 1, rows_b, sem_b).wait()
            flush(c + 1, rows_b)

    return gather(table, idx)


def _rows_from_tiles(ref, first, n_rows, stride=ROW_TILE):
    words = [ref[pl.ds(first + c, n_rows, stride=stride), :] for c in range(ROW_TILE)]
    lo = [pltpu.bitcast(w << 16, F32) for w in words]
    hi = [pltpu.bitcast(w & U32(0xFFFF0000), F32) for w in words]
    return jnp.concatenate(lo + hi, axis=-1)


def _rows_to_tiles(ref, val):
    half = D_MODEL // 2

    def bits(x):
        return pltpu.bitcast(x.astype(BF16).astype(F32), U32)

    for c in range(ROW_TILE):
        lo = bits(val[:, c * 128:(c + 1) * 128])
        hi = bits(val[:, half + c * 128:half + (c + 1) * 128])
        ref[pl.ds(c, val.shape[0], stride=ROW_TILE), :] = hi | (lo >> 16)


def _ffn_kernel(be_ref, nx_ref, par_ref, nvb_ref, x_ref, w1_hbm, w3_hbm, w2_hbm, y_ref,
                w1f_ref, w3f_ref, w2f_ref, w1b_ref, w3b_ref, w2b_ref, sem, *, layer):
    b = pl.program_id(0)
    nvb = nvb_ref[0]
    expert = be_ref[b]
    slot = par_ref[b]
    new_expert = jnp.logical_or(b == 0, expert != be_ref[jnp.maximum(b - 1, 0)])

    def fetch(e, sl):
        return [pltpu.make_async_copy(w_hbm.at[layer, e], wf_ref.at[sl], sem.at[sl])
                for w_hbm, wf_ref in ((w1_hbm, w1f_ref), (w3_hbm, w3f_ref), (w2_hbm, w2f_ref))]

    @pl.when(jnp.logical_and(b == 0, nvb > 0))
    def _():
        for cp in fetch(expert, slot):
            cp.start()

    @pl.when(jnp.logical_and(b < nvb, new_expert))
    def _():
        for cp in fetch(expert, slot):
            cp.wait()

        @pl.when(nx_ref[b] >= 0)
        def _():
            for cp in fetch(nx_ref[b], 1 - slot):
                cp.start()

        w1b_ref[...] = w1f_ref[slot].astype(BF16)
        w3b_ref[...] = w3f_ref[slot].astype(BF16)
        w2b_ref[...] = w2f_ref[slot].astype(BF16)

    @pl.when(b < nvb)
    def _():
        xb = _rows_from_tiles(x_ref, 0, FFN_BM).astype(BF16)
        h1 = jnp.dot(xb, w1b_ref[...], preferred_element_type=F32)
        h3 = jnp.dot(xb, w3b_ref[...], preferred_element_type=F32)
        a = (jax.nn.silu(h1) * h3).astype(BF16)
        _rows_to_tiles(y_ref, jnp.dot(a, w2b_ref[...], preferred_element_type=F32))

    @pl.when(b >= nvb)
    def _():
        y_ref[...] = jnp.zeros(y_ref.shape, U32)


def _expert_ffn(block_e, next_e, parity, nvb, xs, w1, w3, w2, layer):
    nb = block_e.shape[0]

    def used(b, *prefetch):
        return jnp.minimum(b, jnp.maximum(prefetch[-1][0] - 1, 0))

    return pl.pallas_call(
        functools.partial(_ffn_kernel, layer=layer),
        grid_spec=pltpu.PrefetchScalarGridSpec(
            num_scalar_prefetch=4,
            grid=(nb,),
            in_specs=[
                pl.BlockSpec((FFN_BM * ROW_TILE, 128), lambda b, *prefetch: (used(b, *prefetch), 0)),
                pl.BlockSpec(memory_space=pl.ANY),
                pl.BlockSpec(memory_space=pl.ANY),
                pl.BlockSpec(memory_space=pl.ANY),
            ],
            out_specs=pl.BlockSpec((FFN_BM * ROW_TILE, 128), lambda b, *prefetch: (b, 0)),
            scratch_shapes=[pltpu.VMEM((2, D_MODEL, EXPERT_FF), F32), pltpu.VMEM((2, D_MODEL, EXPERT_FF), F32),
                            pltpu.VMEM((2, EXPERT_FF, D_MODEL), F32),
                            pltpu.VMEM((D_MODEL, EXPERT_FF), BF16), pltpu.VMEM((D_MODEL, EXPERT_FF), BF16),
                            pltpu.VMEM((EXPERT_FF, D_MODEL), BF16),
                            pltpu.SemaphoreType.DMA((2,))],
        ),
        out_shape=jax.ShapeDtypeStruct((nb * FFN_BM * ROW_TILE, 128), U32),
        compiler_params=_cparams(("arbitrary",)),
        name="expert_ffn",
    )(block_e, next_e, parity, nvb, xs, w1, w3, w2)


def _comb_kernel(yg_ref, x1_ref, rw_ref, p_ref, gple_ref, wpg_ref, wpp_ref, gout_ref, o_ref, *, tm, final):
    w = rw_ref[...]
    y0 = _rows_from_tiles(yg_ref, 0, tm, stride=TOP_K * ROW_TILE)
    y1 = _rows_from_tiles(yg_ref, ROW_TILE, tm, stride=TOP_K * ROW_TILE)
    x2 = x1_ref[...] + (y0 * w[:, 0:1] + y1 * w[:, 1:2])
    e = jnp.dot(p_ref[...].astype(BF16), wpp_ref[...], preferred_element_type=F32)
    gate = jax.nn.sigmoid(jnp.dot(_rms(x2, gple_ref[...]).astype(BF16), wpg_ref[...], preferred_element_type=F32))
    x3 = x2 + gate * e
    o_ref[...] = _rms(x3, gout_ref[...]) if final else x3


def _combine_ple(yg, x1, rw, p2d, gple, wpg, wpp, gout, tm, layer):
    T = x1.shape[0]
    nt = T // tm
    return pl.pallas_call(
        functools.partial(_comb_kernel, tm=tm, final=layer == DEPTH - 1),
        grid=(nt,),
        in_specs=[
            pl.BlockSpec((tm * TOP_K * ROW_TILE, 128), lambda i: (i, 0)),
            pl.BlockSpec((tm, D_MODEL), lambda i: (i, 0)),
            pl.BlockSpec((tm, 128), lambda i: (i, 0)),
            pl.BlockSpec((tm, PLE_DIM), lambda i: (layer * nt + i, 0)),
            _layer_spec((1, D_MODEL), layer),
            _layer_spec((D_MODEL, D_MODEL), layer),
            _layer_spec((PLE_DIM, D_MODEL), layer),
            pl.BlockSpec((1, D_MODEL), lambda i: (0, 0)),
        ],
        out_specs=pl.BlockSpec((tm, D_MODEL), lambda i: (i, 0)),
        out_shape=jax.ShapeDtypeStruct((T, D_MODEL), F32),
        compiler_params=_cparams(("parallel",)),
        name="combine_ple",
    )(yg, x1, rw, p2d, gple, wpg, wpp, gout)


def _rope_tables(pos, theta, rot_dim, period, offset, scale):
    half = rot_dim // 2
    inv = jnp.float32(theta) ** (-jnp.arange(half, dtype=F32) * 2.0 / rot_dim)
    rel = np.arange(128) % period - offset
    rot = (rel >= 0) & (rel < rot_dim)
    upper = rot & (rel >= half)
    lower = rot & (rel < half)
    ang = pos.astype(F32)[:, None] * inv[np.where(rot, rel % half, 0)][None, :]
    cos, sin = jnp.cos(ang), jnp.sin(ang)
    tabs = (jnp.where(rot, cos, 1.0), jnp.where(upper, sin, 0.0), jnp.where(lower, -sin, 0.0))
    return jnp.concatenate(tabs, axis=1) * jnp.float32(scale)


def _prep_w_in(w):
    c_q = w[..., 0:512]
    c_kv = w[..., 512:768]
    k_pe = w[..., 768:800]
    dil = w[..., 800:800 + 2304]
    gates = w[..., 3104:5152]
    zeros = lambda n: jnp.zeros(w.shape[:-1] + (n,), w.dtype)
    kslot = jnp.concatenate([zeros(MLA_NOPE), k_pe, zeros(256 - MLA_NOPE - MLA_ROPE)], axis=-1)
    return jnp.concatenate([gates, c_q, c_kv, kslot, dil], axis=-1).astype(BF16)


def _pad_heads(w, width):
    w = jnp.pad(w, ((0, 0), (0, 0), (0, 0), (0, MLA_SLOT - width)))
    return w.reshape(w.shape[0], w.shape[1], MLA_HEADS * MLA_SLOT).astype(BF16)


def _prep_w_q(w):
    return _pad_heads(w.reshape(DEPTH, MLA_Q_LORA, MLA_HEADS, MLA_NOPE + MLA_ROPE), MLA_NOPE + MLA_ROPE)


def _prep_w_kv(w):
    w = w.reshape(DEPTH, MLA_KV_LORA, MLA_HEADS, MLA_NOPE + MLA_V)
    return _pad_heads(w[..., :MLA_NOPE], MLA_NOPE), _pad_heads(w[..., MLA_NOPE:], MLA_V)


def _dest_kernel(ri_ref, ps_ref, o_ref):
    ri = ri_ref[...].astype(F32)
    lane = lax.broadcasted_iota(jnp.int32, ri.shape, 1)
    ps = ps_ref[...]

    def col(k):
        return jnp.sum(jnp.where(lane == k, ri, 0.0), axis=-1, keepdims=True)

    def dest(k):
        start = jnp.sum(jnp.where(lane == col(k).astype(jnp.int32), ps, 0.0), axis=-1, keepdims=True)
        return (start + col(2 + k)).astype(jnp.int32)

    o_ref[...] = jnp.where(lane == 0, dest(0), jnp.where(lane == 1, dest(1), 0))


def _dest_rows(ri, pstarts, tm):
    T = ri.shape[0]
    ps = jnp.zeros((1, 128), F32).at[0, :N_EXPERTS].set(pstarts.astype(F32))
    return pl.pallas_call(
        _dest_kernel,
        grid=(T // tm,),
        in_specs=[pl.BlockSpec((tm, 128), lambda i: (i, 0)), pl.BlockSpec((1, 128), lambda i: (0, 0))],
        out_specs=pl.BlockSpec((tm, 128), lambda i: (i, 0)),
        out_shape=jax.ShapeDtypeStruct((T, 128), jnp.int32),
        compiler_params=_cparams(("parallel",)),
        name="dest_rows",
    )(ri, ps)


def _dispatch_plan(ri, cnt, T, tm):
    counts = cnt[0, :N_EXPERTS].astype(jnp.int32)
    pcounts = (counts + FFN_BM - 1) // FFN_BM * FFN_BM
    pends = jnp.cumsum(pcounts)
    pstarts = pends - pcounts
    dest = _dest_rows(ri, pstarts, tm)[:, 0:TOP_K]
    n_blocks = (T * TOP_K) // FFN_BM + N_EXPERTS
    first_row = jnp.arange(n_blocks, dtype=jnp.int32) * FFN_BM
    block_e = jnp.minimum(jnp.sum((pends[None, :] <= first_row[:, None]).astype(jnp.int32), axis=1), N_EXPERTS - 1)
    nvb = pends[-1] // FFN_BM
    after = pends // FFN_BM
    next_of_expert = jnp.where(after < nvb, block_e[jnp.minimum(after, n_blocks - 1)], -1)
    run_of_expert = jnp.cumsum((pcounts > 0).astype(jnp.int32)) - 1
    ffn_plan = (block_e, next_of_expert[block_e].astype(jnp.int32), (run_of_expert[block_e] % 2).astype(jnp.int32),
                nvb.astype(jnp.int32).reshape(1))
    return dest.reshape(T * TOP_K), n_blocks * FFN_BM, ffn_plan


def kernel(x, p, positions, g_mix, w_in, g_q_lat, w_q_up, g_kv_lat, w_kv_up, w_branch_a, w_branch_b, w_out, g_ffn, w_router_grp, b_router_grp, w_router_exp, b_router_exp, w_exp_gate, w_exp_up, w_exp_down, g_ple, w_ple_gate, w_ple_proj, g_final):
    B, S, D = x.shape
    T = B * S
    TM_IN, TM_QKV, TQ, TM_POST, TM_ROUTE, TM_COMB = 512, 512, 512, 512, 512, 256

    pos = positions.reshape(T)
    dil_scale = DIL_HEAD_DIM ** -0.5 * LOG2E
    mla_scale = (MLA_NOPE + MLA_ROPE) ** -0.5 * LOG2E
    tab_dil = jnp.concatenate([_rope_tables(pos, ROPE_THETA, PARTIAL_ROT, DIL_HEAD_DIM, 0, dil_scale),
                               _rope_tables(pos, ROPE_THETA, PARTIAL_ROT, DIL_HEAD_DIM, 0, 1.0)], axis=1)
    tab_mla = jnp.concatenate([_rope_tables(pos, MLA_ROPE_THETA, MLA_ROPE, MLA_SLOT, MLA_NOPE, mla_scale),
                               _rope_tables(pos, MLA_ROPE_THETA, MLA_ROPE, MLA_SLOT, MLA_NOPE, 1.0)], axis=1)

    w_in_p = _prep_w_in(w_in)
    wq = _prep_w_q(w_q_up)
    wk, wv = _prep_w_kv(w_kv_up)
    wr = jnp.concatenate([w_router_grp, w_router_exp, jnp.zeros((DEPTH, D, 128 - N_GROUPS - N_EXPERTS), F32)],
                         axis=-1).astype(BF16)
    br = jnp.concatenate([b_router_grp, b_router_exp.reshape(DEPTH, N_EXPERTS),
                          jnp.zeros((DEPTH, 128 - N_GROUPS - N_EXPERTS), F32)], axis=-1).reshape(DEPTH, 1, 128)
    wa, wb, wo = w_branch_a.astype(BF16), w_branch_b.astype(BF16), w_out.astype(BF16)
    wpg, wpp = w_ple_gate.astype(BF16), w_ple_proj.astype(BF16)
    gains = lambda g: g.reshape(DEPTH, 1, -1)

    xc = x.reshape(T, D)
    for i in range(DEPTH):
        main, d0, d1, d2 = _in_proj(xc, gains(g_mix), w_in_p, tab_dil, B, S, TM_IN, i)
        q, k, v = _mla_qkv(main, gains(g_q_lat), gains(g_kv_lat), wq, wk, wv, tab_mla, TM_QKV, i)
        o_a = _mla_attn(q, k, v, B, S, TQ)
        ogs, lss = [], []
        for (_, dil), qkv in zip(DIL_PATTERN, (d0, d1, d2)):
            og, ls = _dil_attn(qkv, B, dil, S // dil)
            ogs.append(og)
            lss.append(ls)
        x1, h2, lg = _post(xc, o_a, ogs, lss, main, wa, wb, wo, gains(g_ffn), wr, br, B, S, TM_POST, i)
        ri, rw, cnt = _route(lg, TM_ROUTE)
        dest, n_rows, ffn_plan = _dispatch_plan(ri, cnt, T, TM_ROUTE)
        xs = _sc_gather_rows(h2.reshape(T, ROW_TILE, 128), dest, n_out=n_rows)
        ys = _expert_ffn(*ffn_plan, xs.reshape(-1, 128), w_exp_gate, w_exp_up, w_exp_down, i)
        yg = _sc_gather_rows(ys.reshape(-1, ROW_TILE, 128), dest)
        xc = _combine_ple(yg.reshape(-1, 128), x1, rw, p.reshape(DEPTH * T, PLE_DIM), gains(g_ple), wpg, wpp,
                          g_final.reshape(1, D), TM_COMB, i)
    return xc.reshape(B, S, D)
```

```python
import functools
import math

import jax
import jax.numpy as jnp
import numpy as np
from jax import lax
from jax.experimental import pallas as pl
from jax.experimental.pallas import tpu as pltpu
from jax.experimental.pallas import tpu_sc as plsc

F32 = jnp.float32
BF16 = jnp.bfloat16

D_MODEL = 1024
DEPTH = 4
RMS_EPS = 1e-6
NEG = -1e30
LOG2E = math.log2(math.e)

MLA_HEADS = 16
MLA_Q_LORA = 512
MLA_KV_LORA = 256
MLA_NOPE = 64
MLA_ROPE = 32
MLA_V = 64
MLA_ROPE_THETA = 10000.0
MLA_SLOT = 128
MLA_HALF = MLA_ROPE // 2


def _mla_head_lanes(lane, head):
    l = lane % MLA_SLOT
    nope = jnp.logical_and(lane // MLA_SLOT == head, l < MLA_NOPE)
    rope = jnp.logical_and(l >= MLA_NOPE + head * MLA_HALF, l < MLA_NOPE + (head + 1) * MLA_HALF)
    return jnp.logical_or(nope, rope)

DIL_PATTERN = ((128, 1), (512, 4), (2048, 16))
DIL_GROUPS = 3
DIL_HEADS = 4
DIL_HEAD_DIM = 64
DIL_W = DIL_HEADS * DIL_HEAD_DIM
DIL_BLK = 128
DIL_UNROLL = 16
ROPE_THETA = 500000.0
PARTIAL_ROT = DIL_HEAD_DIM // 4

N_GROUPS = 8
EXPERTS_PER_GROUP = 8
N_EXPERTS = 64
TOP_K = 2
EXPERT_FF = 256
PLE_DIM = 256

IN_TN = 768
MAIN_COLS = 3072
IN_COLS_PAD = MAIN_COLS + DIL_GROUPS * 3 * DIL_W
N_MAIN_TILES = MAIN_COLS // IN_TN

FFN_BM = 256
ROW_TILE = D_MODEL // 256
U32 = jnp.uint32
VMEM_LIMIT = 48 * 1024 * 1024

SC_CORES = 2
SC_SUBCORES = 16
SC_LANES = 16
SC_CHUNK = 64


def _cparams(sem, bounds_checks=True):
    return pltpu.CompilerParams(dimension_semantics=sem, vmem_limit_bytes=VMEM_LIMIT,
                                disable_bounds_checks=not bounds_checks)


def _layer_spec(shape, layer, col=None):
    def index(*grid_idx):
        return (layer,) + (0,) * (len(shape) - 1) + ((col(*grid_idx),) if col else (0,))

    return pl.BlockSpec((None,) + tuple(shape), index)


def _rms(x, g):
    return x * lax.rsqrt(jnp.mean(x * x, axis=-1, keepdims=True) + RMS_EPS) * g


def _rope128(x, c, s_up, s_dn, half):
    return x * c + pltpu.roll(x, half, 1) * s_up + pltpu.roll(x, 128 - half, 1) * s_dn


def _in_kernel(x_ref, g_ref, w_ref, tab_ref, main_ref, d0_ref, d1_ref, d2_ref, xn_ref, acc_ref):
    j = pl.program_id(1)

    @pl.when(j == 0)
    def _():
        xn_ref[...] = _rms(x_ref[...], g_ref[...]).astype(BF16)

    acc = jnp.dot(xn_ref[...], w_ref[...], preferred_element_type=F32)

    @pl.when(j < N_MAIN_TILES)
    def _():
        main_ref[...] = acc.astype(BF16)

    def dil_tile(out_ref, dil):
        for c in range(4):
            t0 = 0 if c < 2 else 3
            x = acc[:, c * 128:(c + 1) * 128]
            acc_ref[c] = _rope128(
                x, tab_ref[:, t0 * 128:(t0 + 1) * 128], tab_ref[:, (t0 + 1) * 128:(t0 + 2) * 128],
                tab_ref[:, (t0 + 2) * 128:(t0 + 3) * 128], PARTIAL_ROT // 2)
        for c in range(4, 6):
            acc_ref[c] = acc[:, c * 128:(c + 1) * 128]
        rows = acc_ref.shape[1] // dil
        for r in range(dil):
            for c in range(6):
                out_ref[0, r, :, c * 128:(c + 1) * 128] = acc_ref[c, pl.ds(r, rows, stride=dil), :].astype(BF16)

    for gi, (_, dil) in enumerate(DIL_PATTERN):
        pl.when(j == N_MAIN_TILES + gi)(functools.partial(dil_tile, (d0_ref, d1_ref, d2_ref)[gi], dil))


def _in_proj(x2d, g, w, tab, B, S, tm, layer):
    T = x2d.shape[0]
    nt = S // tm
    dil_shapes = [jax.ShapeDtypeStruct((B, d, S // d, 3 * DIL_W), BF16) for _, d in DIL_PATTERN]
    dil_specs = [pl.BlockSpec((1, d, tm // d, 3 * DIL_W), lambda i, j, nt=nt: (i // nt, 0, i % nt, 0))
                 for _, d in DIL_PATTERN]
    return pl.pallas_call(
        _in_kernel,
        grid=(T // tm, IN_COLS_PAD // IN_TN),
        in_specs=[
            pl.BlockSpec((tm, D_MODEL), lambda i, j: (i, 0)),
            _layer_spec((1, D_MODEL), layer),
            _layer_spec((D_MODEL, IN_TN), layer, col=lambda i, j: j),
            pl.BlockSpec((tm, 6 * 128), lambda i, j: (i, 0)),
        ],
        out_specs=[pl.BlockSpec((tm, IN_TN), lambda i, j: (i, jnp.minimum(j, N_MAIN_TILES - 1)))] + dil_specs,
        out_shape=[jax.ShapeDtypeStruct((T, MAIN_COLS), BF16)] + dil_shapes,
        scratch_shapes=[pltpu.VMEM((tm, D_MODEL), BF16), pltpu.VMEM((IN_TN // 128, tm, 128), F32)],
        compiler_params=_cparams(("parallel", "arbitrary")),
        name="in_proj",
    )(x2d, g, w, tab)


def _qkv_kernel(cq_ref, ckv_ref, kpe_ref, gq_ref, gkv_ref, wq_ref, wk_ref, wv_ref, tab_ref,
                q_ref, k_ref, v_ref):
    qn = _rms(cq_ref[...].astype(F32), gq_ref[...]).astype(BF16)
    kvn = _rms(ckv_ref[...].astype(F32), gkv_ref[...]).astype(BF16)
    qacc = jnp.dot(qn, wq_ref[...], preferred_element_type=F32)
    kacc = jnp.dot(kvn, wk_ref[...], preferred_element_type=F32)
    vacc = jnp.dot(kvn, wv_ref[...], preferred_element_type=F32)
    lane_v = lax.broadcasted_iota(jnp.int32, vacc.shape, 1)
    v_ref[...] = jnp.where(lane_v % MLA_SLOT < MLA_V, vacc, 1.0).astype(BF16)
    cq, snq, spq, ck, snk, spk = [tab_ref[:, t * 128:(t + 1) * 128] for t in range(6)]
    kpe_lo = kpe_ref[:, 0:128].astype(F32)
    kpe_hi = kpe_ref[:, 128:256].astype(F32)
    krot = (kpe_lo * ck + kpe_hi * snk, kpe_hi * ck + kpe_lo * spk)
    for pair in range(MLA_HEADS // 2):
        s0 = slice(2 * pair * MLA_SLOT, (2 * pair + 1) * MLA_SLOT)
        s1 = slice((2 * pair + 1) * MLA_SLOT, (2 * pair + 2) * MLA_SLOT)
        q0, q1 = qacc[:, s0], qacc[:, s1]
        q_ref[:, s0] = (q0 * cq + q1 * snq).astype(BF16)
        q_ref[:, s1] = (q1 * cq + q0 * spq).astype(BF16)
        k_ref[:, s0] = (kacc[:, s0] + krot[0]).astype(BF16)
        k_ref[:, s1] = (kacc[:, s1] + krot[1]).astype(BF16)


def _mla_qkv(main, gq, gkv, wq, wk, wv, tab, tm, layer):
    T = main.shape[0]
    return pl.pallas_call(
        _qkv_kernel,
        grid=(T // tm,),
        in_specs=[
            pl.BlockSpec((tm, MLA_Q_LORA), lambda i: (i, 2048 // MLA_Q_LORA)),
            pl.BlockSpec((tm, MLA_KV_LORA), lambda i: (i, 2560 // MLA_KV_LORA)),
            pl.BlockSpec((tm, 256), lambda i: (i, 2816 // 256)),
            _layer_spec((1, MLA_Q_LORA), layer),
            _layer_spec((1, MLA_KV_LORA), layer),
            _layer_spec((MLA_Q_LORA, MLA_HEADS * MLA_SLOT), layer),
            _layer_spec((MLA_KV_LORA, MLA_HEADS * MLA_SLOT), layer),
            _layer_spec((MLA_KV_LORA, MLA_HEADS * MLA_SLOT), layer),
            pl.BlockSpec((tm, 6 * 128), lambda i: (i, 0)),
        ],
        out_specs=[
            pl.BlockSpec((tm, MLA_HEADS * MLA_SLOT), lambda i: (i, 0)),
            pl.BlockSpec((tm, MLA_HEADS * MLA_SLOT), lambda i: (i, 0)),
            pl.BlockSpec((tm, MLA_HEADS * MLA_SLOT), lambda i: (i, 0)),
        ],
        out_shape=[
            jax.ShapeDtypeStruct((T, MLA_HEADS * MLA_SLOT), BF16),
            jax.ShapeDtypeStruct((T, MLA_HEADS * MLA_SLOT), BF16),
            jax.ShapeDtypeStruct((T, MLA_HEADS * MLA_SLOT), BF16),
        ],
        compiler_params=_cparams(("parallel",)),
        name="mla_qkv",
    )(main, main, main, gq, gkv, wq, wk, wv, tab)


def _mla_attn_kernel(qa_ref, qb_ref, k_ref, v_ref, o_ref, m_ref, acc_ref, *, tq):
    p_id = pl.program_id(2)
    nq = k_ref.shape[1] // tq
    lane = lax.broadcasted_iota(jnp.int32, (tq, 2 * MLA_SLOT), 1)
    qh = []
    for q_ref in (qa_ref, qb_ref):
        qp = q_ref[0]
        zero = jnp.zeros_like(qp)
        qh.append([jnp.where(_mla_head_lanes(lane, h), qp, zero) for h in range(2)])
    hq = tq // 2
    dn = (((1,), (1,)), ((), ()))

    def step(blk, start, width, r0, mask, first=False):
        kb = k_ref[0, pl.ds(start, width), :]
        vb = v_ref[0, pl.ds(start, width), :]
        rows = tq - r0
        s_pair = lax.dot_general(jnp.concatenate([qh[blk][0][r0:], qh[blk][1][r0:]], axis=0), kb, dn,
                                 preferred_element_type=F32)
        for h in range(2):
            s = s_pair[h * rows:(h + 1) * rows]
            if mask is not None:
                s = jnp.where(mask, s, NEG)
            m_cur = jnp.max(s, axis=-1, keepdims=True)
            if first:
                m_new = jnp.broadcast_to(m_cur, (tq - r0, 128))
            else:
                m_prev = m_ref[blk, h, r0:, :]
                m_new = jnp.maximum(m_prev, m_cur)
            p = jnp.exp2(s - jnp.concatenate([m_new] * (width // 128), axis=-1))
            pv = jnp.dot(p.astype(BF16), vb[:, h * MLA_SLOT:(h + 1) * MLA_SLOT], preferred_element_type=F32)
            if first:
                acc_ref[blk, h, r0:, :] = pv
            else:
                acc_ref[blk, h, r0:, :] = jnp.exp2(m_prev - m_new) * acc_ref[blk, h, r0:, :] + pv
            m_ref[blk, h, r0:, :] = m_new

    def causal(rows):
        return lax.broadcasted_iota(jnp.int32, (rows, hq), 1) <= lax.broadcasted_iota(jnp.int32, (rows, hq), 0)

    def q_blocks(n_a):
        n_b = nq - 1 - n_a
        for blk, n_full in ((1, n_b), (0, n_a)):
            for j in range(n_full):
                step(blk, j * tq, tq, 0, None, first=j == 0)
            step(blk, n_full * tq, hq, 0, causal(tq), first=n_full == 0)
            step(blk, n_full * tq + hq, hq, hq, causal(hq))

    for n_a in range(nq // 2):
        pl.when(p_id == n_a)(functools.partial(q_blocks, n_a))
    lane_o = lax.broadcasted_iota(jnp.int32, (tq, 2 * MLA_V), 1)
    for blk in range(2):
        outs = []
        for h in range(2):
            a = acc_ref[blk, h]
            outs.append(a / pltpu.roll(a, MLA_V, 1))
        o_ref[0, blk, 0] = jnp.where(lane_o < MLA_V, outs[0], pltpu.roll(outs[1], MLA_V, 1)).astype(BF16)


def _mla_attn(q, k, v, B, S, tq):
    nq = S // tq
    q = q.reshape(B, S, MLA_HEADS * MLA_SLOT)
    k = k.reshape(B, S, MLA_HEADS * MLA_SLOT)
    v = v.reshape(B, S, MLA_HEADS * MLA_SLOT)
    return pl.pallas_call(
        functools.partial(_mla_attn_kernel, tq=tq),
        grid=(B, MLA_HEADS // 2, nq // 2),
        in_specs=[
            pl.BlockSpec((1, tq, 2 * MLA_SLOT), lambda b, h, p: (b, p, h)),
            pl.BlockSpec((1, tq, 2 * MLA_SLOT), lambda b, h, p: (b, nq - 1 - p, h)),
            pl.BlockSpec((1, S, 2 * MLA_SLOT), lambda b, h, p: (b, 0, h)),
            pl.BlockSpec((1, S, 2 * MLA_SLOT), lambda b, h, p: (b, 0, h)),
        ],
        out_specs=pl.BlockSpec((1, 2, 1, tq, 2 * MLA_V), lambda b, h, p: (b, 0, p, 0, h)),
        out_shape=jax.ShapeDtypeStruct((B, 2, nq // 2, tq, MLA_HEADS * MLA_V), BF16),
        scratch_shapes=[pltpu.VMEM((2, 2, tq, 128), F32), pltpu.VMEM((2, 2, tq, MLA_SLOT), F32)],
        compiler_params=_cparams(("parallel", "parallel", "arbitrary")),
        name="mla_attn",
    )(q, q, k, v)


def _dil_kernel(qkv_ref, o_ref, lse_ref, bias_ref, *, nb):
    dil = qkv_ref.shape[1]
    width = 2 * DIL_BLK if nb > 1 else DIL_BLK
    row = lax.broadcasted_iota(jnp.int32, (DIL_BLK, width), 0)
    col = lax.broadcasted_iota(jnp.int32, (DIL_BLK, width), 1)
    bias_ref[0] = jnp.where(col <= row, 0.0, NEG)
    if nb > 1:
        later = jnp.logical_or(jnp.logical_and(col >= DIL_BLK, col - DIL_BLK <= row),
                               jnp.logical_and(col < DIL_BLK, col >= row))
        bias_ref[1] = jnp.where(later, 0.0, NEG)
    head_of_lane = lax.broadcasted_iota(jnp.int32, (DIL_BLK, DIL_W), 1) // DIL_HEAD_DIM
    dn = (((1,), (1,)), ((), ()))

    def by_head(parts):
        out = parts[DIL_HEADS - 1]
        for h in range(DIL_HEADS - 2, -1, -1):
            out = jnp.where(head_of_lane == h, parts[h], out)
        return out

    def unit(u):
        r = u // nb
        n = u % nb
        q0 = pl.multiple_of(n * DIL_BLK, DIL_BLK)
        q = qkv_ref[0, r, pl.ds(q0, DIL_BLK), 0:DIL_W]
        if nb > 1:
            w0 = pl.multiple_of(jnp.maximum(n - 1, 0) * DIL_BLK, DIL_BLK)
            bias = bias_ref[jnp.minimum(n, 1)]
        else:
            w0 = 0
            bias = bias_ref[0]
        kw = qkv_ref[0, r, pl.ds(w0, width), DIL_W:2 * DIL_W]
        vw = qkv_ref[0, r, pl.ds(w0, width), 2 * DIL_W:3 * DIL_W]
        zero = jnp.zeros_like(q)
        qs = jnp.concatenate([jnp.where(head_of_lane == h, q, zero) for h in range(DIL_HEADS)], axis=0)
        s = lax.dot_general(qs, kw, dn, preferred_element_type=F32)
        s = (s.reshape(DIL_HEADS, DIL_BLK, width) + bias[None]).reshape(DIL_HEADS * DIL_BLK, width)
        m = jnp.max(s, axis=-1, keepdims=True)
        e = jnp.exp2(s - m)
        den = jnp.sum(e, axis=-1, keepdims=True)
        pv = jnp.dot(e.astype(BF16), vw, preferred_element_type=F32)
        lse = m + jnp.log2(den)
        blk = lambda t, h: t[h * DIL_BLK:(h + 1) * DIL_BLK]
        o = by_head([blk(pv, h) for h in range(DIL_HEADS)]) / by_head(
            [jnp.broadcast_to(blk(den, h), (DIL_BLK, DIL_W)) for h in range(DIL_HEADS)])
        o_ref[0, r, pl.ds(q0, DIL_BLK), :] = o.astype(BF16)
        lse_ref[0, r, pl.ds(q0, DIL_BLK), :] = by_head(
            [jnp.broadcast_to(blk(lse, h), (DIL_BLK, DIL_W)) for h in range(DIL_HEADS)])

    def body(t, carry):
        for u in range(DIL_UNROLL):
            unit(DIL_UNROLL * t + u)
        return carry

    lax.fori_loop(0, dil * nb // DIL_UNROLL, body, 0)


def _dil_attn(qkv, B, dil, L):
    nb = L // DIL_BLK
    return pl.pallas_call(
        functools.partial(_dil_kernel, nb=nb),
        grid=(B,),
        in_specs=[pl.BlockSpec((1, dil, L, 3 * DIL_W), lambda b: (b, 0, 0, 0))],
        out_specs=[
            pl.BlockSpec((1, dil, L, DIL_W), lambda b: (b, 0, 0, 0)),
            pl.BlockSpec((1, dil, L, DIL_W), lambda b: (b, 0, 0, 0)),
        ],
        out_shape=[
            jax.ShapeDtypeStruct((B, dil, L, DIL_W), BF16),
            jax.ShapeDtypeStruct((B, dil, L, DIL_W), F32),
        ],
        scratch_shapes=[pltpu.VMEM((2, DIL_BLK, 2 * DIL_BLK if nb > 1 else DIL_BLK), F32)],
        compiler_params=_cparams(("parallel",)),
        name=f"dil_attn_d{dil}",
    )(qkv)


def _post_kernel(x_ref, oa_ref, og0_ref, og1_ref, og2_ref, ls0_ref, ls1_ref, ls2_ref, gates_ref,
                 wa_ref, wb_ref, wo_ref, gffn_ref, wr_ref, br_ref,
                 x1_ref, h2_ref, lg_ref, ob_ref, *, tm):
    ls =[r[0] for r in (ls0_ref, ls1_ref, ls2_ref)]
    og = [r[0] for r in (og0_ref, og1_ref, og2_ref)]
    for gi, (_, dil) in enumerate(DIL_PATTERN):
        rows = tm // dil
        for r in range(dil):
            for c in range(2):
                sl = slice(c * 128, (c + 1) * 128)
                ob_ref[2 * gi + c, pl.ds(r, rows, stride=dil), :] = og[gi][r][:, sl].astype(F32)
                ob_ref[6 + 2 * gi + c, pl.ds(r, rows, stride=dil), :] = ls[gi][r][:, sl]

    def tok_major(k):
        return jnp.concatenate([ob_ref[2 * k], ob_ref[2 * k + 1]], axis=-1)

    l0, l1, l2 = tok_major(3), tok_major(4), tok_major(5)
    mx = jnp.maximum(jnp.maximum(l0, l1), l2)
    w0, w1, w2 = jnp.exp2(l0 - mx), jnp.exp2(l1 - mx), jnp.exp2(l2 - mx)
    ob = (w0 * tok_major(0) + w1 * tok_major(1) + w2 * tok_major(2)) / (w0 + w1 + w2)
    ya = jnp.dot(oa_ref[0, 0, 0], wa_ref[...], preferred_element_type=F32)
    yb = jnp.dot(ob.astype(BF16), wb_ref[...], preferred_element_type=F32)
    merged = (jax.nn.sigmoid(gates_ref[:, 0:D_MODEL].astype(F32)) * ya
              + jax.nn.sigmoid(gates_ref[:, D_MODEL:2 * D_MODEL].astype(F32)) * yb)
    x1 = x_ref[...] + jnp.dot(merged.astype(BF16), wo_ref[...], preferred_element_type=F32)
    x1_ref[...] = x1
    h2 = _rms(x1, gffn_ref[...])
    _rows_to_tiles(h2_ref, h2)
    lg_ref[...] = jnp.dot(h2.astype(BF16), wr_ref[...], preferred_element_type=F32) + br_ref[...]


def _post(x2d, oa, ogs, lss, main, wa, wb, wo, gffn, wr, br, B, S, tm, layer):
    T = x2d.shape[0]
    nt = S // tm
    res_specs = [pl.BlockSpec((1, d, tm // d, DIL_W), lambda i, nt=nt: (i // nt, 0, i % nt, 0))
                 for _, d in DIL_PATTERN]
    assert oa.shape[3] == tm and oa.shape[2] * 2 == nt

    def oa_index(i):
        it = i % nt
        late = it >= nt // 2
        return (i // nt, late.astype(jnp.int32), jnp.where(late, nt - 1 - it, it), 0, 0)

    return pl.pallas_call(
        functools.partial(_post_kernel, tm=tm),
        grid=(T // tm,),
        in_specs=[pl.BlockSpec((tm, D_MODEL), lambda i: (i, 0)),
                  pl.BlockSpec((1, 1, 1, tm, D_MODEL), oa_index)]
                 + res_specs + res_specs
                 + [pl.BlockSpec((tm, 2 * D_MODEL), lambda i: (i, 0)),
                    _layer_spec((D_MODEL, D_MODEL), layer),
                    _layer_spec((DIL_W, D_MODEL), layer),
                    _layer_spec((D_MODEL, D_MODEL), layer),
                    _layer_spec((1, D_MODEL), layer),
                    _layer_spec((D_MODEL, 128), layer),
                    _layer_spec((1, 128), layer)],
        out_specs=[pl.BlockSpec((tm, D_MODEL), lambda i: (i, 0)),
                   pl.BlockSpec((tm * ROW_TILE, 128), lambda i: (i, 0)),
                   pl.BlockSpec((tm, 128), lambda i: (i, 0))],
        out_shape=[jax.ShapeDtypeStruct((T, D_MODEL), F32),
                   jax.ShapeDtypeStruct((T * ROW_TILE, 128), U32),
                   jax.ShapeDtypeStruct((T, 128), F32)],
        scratch_shapes=[pltpu.VMEM((12, tm, 128), F32)],
        compiler_params=_cparams(("parallel",)),
        name="post_attn",
    )(x2d, oa, *ogs, *lss, main, wa, wb, wo, gffn, wr, br)


def _route_kernel(lg_ref, ri_ref, rw_ref, cnt_ref, carry_ref, lower_ref, *, tm):
    i = pl.program_id(0)

    @pl.when(i == 0)
    def _():
        carry_ref[...] = jnp.zeros(carry_ref.shape, F32)
        r_i = lax.broadcasted_iota(jnp.int32, (tm, tm), 0)
        c_i = lax.broadcasted_iota(jnp.int32, (tm, tm), 1)
        lower_ref[...] = jnp.where(c_i < r_i, 1.0, 0.0).astype(BF16)

    lg = lg_ref[...]
    lane = lax.broadcasted_iota(jnp.int32, lg.shape, 1)
    lane_f = lane.astype(F32)
    ninf = jnp.float32(-jnp.inf)

    def first_max(vals):
        vmax = jnp.max(vals, axis=-1, keepdims=True)
        idx = jnp.min(jnp.where(vals == vmax, lane_f, 128.0), axis=-1, keepdims=True)
        return vmax, idx.astype(jnp.int32)

    gl = jnp.where(lane < N_GROUPS, lg, ninf)
    gmax, g_sel = first_max(gl)
    p_g = 1.0 / jnp.sum(jnp.exp(gl - gmax), axis=-1, keepdims=True)
    lo = N_GROUPS + g_sel * EXPERTS_PER_GROUP
    el = jnp.where(jnp.logical_and(lane >= lo, lane < lo + EXPERTS_PER_GROUP), lg, ninf)
    v0, i0 = first_max(el)
    v1, i1 = first_max(jnp.where(lane == i0, ninf, el))
    t = jnp.exp(v1 - v0)
    w0 = p_g / (1.0 + t)
    w1 = p_g * t / (1.0 + t)
    e0 = i0 - N_GROUPS
    e1 = i1 - N_GROUPS
    hit0 = lane == e0
    hit1 = lane == e1
    oh = jnp.where(jnp.logical_or(hit0, hit1), 1.0, 0.0).astype(F32)
    excl = jnp.dot(lower_ref[...], oh.astype(BF16), preferred_element_type=F32) + carry_ref[...]
    r0 = jnp.sum(jnp.where(hit0, excl, 0.0), axis=-1, keepdims=True).astype(jnp.int32)
    r1 = jnp.sum(jnp.where(hit1, excl, 0.0), axis=-1, keepdims=True).astype(jnp.int32)
    carry_ref[...] = carry_ref[...] + jnp.sum(oh, axis=0, keepdims=True)
    zi = jnp.zeros(lg.shape, jnp.int32)
    ri_ref[...] = jnp.where(lane == 0, e0, jnp.where(lane == 1, e1, jnp.where(lane == 2, r0, jnp.where(lane == 3, r1, zi))))
    rw_ref[...] = jnp.where(lane == 0, w0, jnp.where(lane == 1, w1, jnp.zeros(lg.shape, F32)))
    cnt_ref[...] = carry_ref[...]


def _route(lg, tm):
    T = lg.shape[0]
    return pl.pallas_call(
        functools.partial(_route_kernel, tm=tm),
        grid=(T // tm,),
        in_specs=[pl.BlockSpec((tm, 128), lambda i: (i, 0))],
        out_specs=[pl.BlockSpec((tm, 128), lambda i: (i, 0)),
                   pl.BlockSpec((tm, 128), lambda i: (i, 0)),
                   pl.BlockSpec((1, 128), lambda i: (0, 0))],
        out_shape=[jax.ShapeDtypeStruct((T, 128), jnp.int32),
                   jax.ShapeDtypeStruct((T, 128), F32),
                   jax.ShapeDtypeStruct((1, 128), F32)],
        scratch_shapes=[pltpu.VMEM((1, 128), F32), pltpu.VMEM((tm, tm), BF16)],
        compiler_params=_cparams(("arbitrary",)),
        name="route",
    )(lg)


def _sc_gather_rows(table, idx, n_out=None):
    inverse = n_out is not None
    n_src = idx.shape[0]
    n = n_out if inverse else n_src
    n_workers = SC_CORES * SC_SUBCORES
    per_w = n // n_workers
    n_chunks = per_w // SC_CHUNK
    assert per_w * n_workers == n and n_chunks * SC_CHUNK == per_w and n_chunks % 2 == 0
    mesh = plsc.VectorSubcoreMesh(core_axis_name="c", subcore_axis_name="s",
                                  num_cores=SC_CORES, num_subcores=SC_SUBCORES)

    @functools.partial(
        pl.kernel, mesh=mesh,
        out_type=jax.ShapeDtypeStruct((n,) + table.shape[1:], table.dtype),
        scratch_types=[pltpu.VMEM((per_w,), jnp.int32),
                       pltpu.VMEM((SC_CHUNK,) + table.shape[1:], table.dtype),
                       pltpu.VMEM((SC_CHUNK,) + table.shape[1:], table.dtype),
                       pltpu.SemaphoreType.DMA, pltpu.SemaphoreType.DMA,
                       pltpu.VMEM((n_src if inverse else SC_LANES,), jnp.int32)],
        compiler_params=pltpu.CompilerParams(use_tc_tiling_on_sc=True, needs_layout_passes=not inverse),
        name="sc_dispatch_rows" if inverse else "sc_gather_rows",
    )
    def gather(table_hbm, idx_hbm, out_hbm, idx_v, rows_a, rows_b, sem_a, sem_b, map_v):
        wid = lax.axis_index("s") * SC_CORES + lax.axis_index("c")
        base = wid * per_w
        if inverse:
            pltpu.sync_copy(idx_hbm, map_v)
            lanes = lax.iota(jnp.int32, SC_LANES)

            @pl.loop(0, per_w // SC_LANES)
            def _(j):
                idx_v[pl.ds(j * SC_LANES, SC_LANES)] = lax.rem(base + j * SC_LANES + lanes, table.shape[0])

            @pl.loop(0, n_src // SC_LANES)
            def _(a):
                local = map_v[pl.ds(a * SC_LANES, SC_LANES)] - base
                mine = jnp.logical_and(local >= 0, local < per_w)
                plsc.store_scatter(idx_v, [local], lax.div(a * SC_LANES + lanes, TOP_K), mask=mine)
        else:
            pltpu.sync_copy(idx_hbm.at[pl.ds(base, per_w)], idx_v)

        def fetch(chunk, rows_v, sem):
            return pltpu.make_async_copy(table_hbm.at[idx_v.at[pl.ds(chunk * SC_CHUNK, SC_CHUNK)]], rows_v, sem)

        def flush(chunk, rows_v):
            pltpu.sync_copy(rows_v, out_hbm.at[pl.ds(base + chunk * SC_CHUNK, SC_CHUNK)])

        fetch(0, rows_a, sem_a).start()

        @pl.loop(0, n_chunks, step=2)
        def _(c):
            fetch(c + 1, rows_b, sem_b).start()
            fetch(c, rows_a, sem_a).wait()
            flush(c, rows_a)

            @pl.when(c + 2 < n_chunks)
            def _():
                fetch(c + 2, rows_a, sem_a).start()

            fetch(c + 1, rows_b, sem_b).wait()
            flush(c + 1, rows_b)

    return gather(table, idx)


def _rows_from_tiles(ref, first, n_rows, stride=ROW_TILE):
    words = [ref[pl.ds(first + c, n_rows, stride=stride), :] for c in range(ROW_TILE)]
    lo = [pltpu.bitcast(w << 16, F32) for w in words]
    hi = [pltpu.bitcast(w & U32(0xFFFF0000), F32) for w in words]
    return jnp.concatenate(lo + hi, axis=-1)


def _rows_to_tiles(ref, val):
    half = D_MODEL // 2

    def bits(x):
        return pltpu.bitcast(x.astype(BF16).astype(F32), U32)

    for c in range(ROW_TILE):
        lo = bits(val[:, c * 128:(c + 1) * 128])
        hi = bits(val[:, half + c * 128:half + (c + 1) * 128])
        ref[pl.ds(c, val.shape[0], stride=ROW_TILE), :] = hi | (lo >> 16)


def _ffn_kernel(be_ref, nx_ref, par_ref, nvb_ref, x_ref, w1_hbm, w3_hbm, w2_hbm, y_ref,
                w1f_ref, w3f_ref, w2f_ref, w1b_ref, w3b_ref, w2b_ref, sem, *, layer):
    b = pl.program_id(0)
    nvb = nvb_ref[0]
    expert = be_ref[b]
    slot = par_ref[b]
    new_expert = jnp.logical_or(b == 0, expert != be_ref[jnp.maximum(b - 1, 0)])

    def fetch(e, sl):
        return [pltpu.make_async_copy(w_hbm.at[layer, e], wf_ref.at[sl], sem.at[sl])
                for w_hbm, wf_ref in ((w1_hbm, w1f_ref), (w3_hbm, w3f_ref), (w2_hbm, w2f_ref))]

    @pl.when(jnp.logical_and(b == 0, nvb > 0))
    def _():
        for cp in fetch(expert, slot):
            cp.start()

    @pl.when(jnp.logical_and(b < nvb, new_expert))
    def _():
        for cp in fetch(expert, slot):
            cp.wait()

        @pl.when(nx_ref[b] >= 0)
        def _():
            for cp in fetch(nx_ref[b], 1 - slot):
                cp.start()

        w1b_ref[...] = w1f_ref[slot].astype(BF16)
        w3b_ref[...] = w3f_ref[slot].astype(BF16)
        w2b_ref[...] = w2f_ref[slot].astype(BF16)

    @pl.when(b < nvb)
    def _():
        xb = _rows_from_tiles(x_ref, 0, FFN_BM).astype(BF16)
        h1 = jnp.dot(xb, w1b_ref[...], preferred_element_type=F32)
        h3 = jnp.dot(xb, w3b_ref[...], preferred_element_type=F32)
        a = (jax.nn.silu(h1) * h3).astype(BF16)
        _rows_to_tiles(y_ref, jnp.dot(a, w2b_ref[...], preferred_element_type=F32))

    @pl.when(b >= nvb)
    def _():
        y_ref[...] = jnp.zeros(y_ref.shape, U32)


def _expert_ffn(block_e, next_e, parity, nvb, xs, w1, w3, w2, layer):
    nb = block_e.shape[0]

    def used(b, *prefetch):
        return jnp.minimum(b, jnp.maximum(prefetch[-1][0] - 1, 0))

    return pl.pallas_call(
        functools.partial(_ffn_kernel, layer=layer),
        grid_spec=pltpu.PrefetchScalarGridSpec(
            num_scalar_prefetch=4,
            grid=(nb,),
            in_specs=[
                pl.BlockSpec((FFN_BM * ROW_TILE, 128), lambda b, *prefetch: (used(b, *prefetch), 0)),
                pl.BlockSpec(memory_space=pl.ANY),
                pl.BlockSpec(memory_space=pl.ANY),
                pl.BlockSpec(memory_space=pl.ANY),
            ],
            out_specs=pl.BlockSpec((FFN_BM * ROW_TILE, 128), lambda b, *prefetch: (b, 0)),
            scratch_shapes=[pltpu.VMEM((2, D_MODEL, EXPERT_FF), F32), pltpu.VMEM((2, D_MODEL, EXPERT_FF), F32),
                            pltpu.VMEM((2, EXPERT_FF, D_MODEL), F32),
                            pltpu.VMEM((D_MODEL, EXPERT_FF), BF16), pltpu.VMEM((D_MODEL, EXPERT_FF), BF16),
                            pltpu.VMEM((EXPERT_FF, D_MODEL), BF16),
                            pltpu.SemaphoreType.DMA((2,))],
        ),
        out_shape=jax.ShapeDtypeStruct((nb * FFN_BM * ROW_TILE, 128), U32),
        compiler_params=_cparams(("arbitrary",)),
        name="expert_ffn",
    )(block_e, next_e, parity, nvb, xs, w1, w3, w2)


def _comb_kernel(yg_ref, x1_ref, rw_ref, p_ref, gple_ref, wpg_ref, wpp_ref, gout_ref, o_ref, *, tm, final):
    w = rw_ref[...]
    y0 = _rows_from_tiles(yg_ref, 0, tm, stride=TOP_K * ROW_TILE)
    y1 = _rows_from_tiles(yg_ref, ROW_TILE, tm, stride=TOP_K * ROW_TILE)
    x2 = x1_ref[...] + (y0 * w[:, 0:1] + y1 * w[:, 1:2])
    e = jnp.dot(p_ref[...].astype(BF16), wpp_ref[...], preferred_element_type=F32)
    gate = jax.nn.sigmoid(jnp.dot(_rms(x2, gple_ref[...]).astype(BF16), wpg_ref[...], preferred_element_type=F32))
    x3 = x2 + gate * e
    o_ref[...] = _rms(x3, gout_ref[...]) if final else x3


def _combine_ple(yg, x1, rw, p2d, gple, wpg, wpp, gout, tm, layer):
    T = x1.shape[0]
    nt = T // tm
    return pl.pallas_call(
        functools.partial(_comb_kernel, tm=tm, final=layer == DEPTH - 1),
        grid=(nt,),
        in_specs=[
            pl.BlockSpec((tm * TOP_K * ROW_TILE, 128), lambda i: (i, 0)),
            pl.BlockSpec((tm, D_MODEL), lambda i: (i, 0)),
            pl.BlockSpec((tm, 128), lambda i: (i, 0)),
            pl.BlockSpec((tm, PLE_DIM), lambda i: (layer * nt + i, 0)),
            _layer_spec((1, D_MODEL), layer),
            _layer_spec((D_MODEL, D_MODEL), layer),
            _layer_spec((PLE_DIM, D_MODEL), layer),
            pl.BlockSpec((1, D_MODEL), lambda i: (0, 0)),
        ],
        out_specs=pl.BlockSpec((tm, D_MODEL), lambda i: (i, 0)),
        out_shape=jax.ShapeDtypeStruct((T, D_MODEL), F32),
        compiler_params=_cparams(("parallel",)),
        name="combine_ple",
    )(yg, x1, rw, p2d, gple, wpg, wpp, gout)


def _rope_tables(pos, theta, rot_dim, period, offset, scale, paired=False):
    half = rot_dim // 2
    inv = jnp.float32(theta) ** (-jnp.arange(half, dtype=F32) * 2.0 / rot_dim)
    rel = np.arange(128) % period - offset
    rot = (rel >= 0) & (rel < rot_dim)
    upper = rot & (rel >= half)
    lower = rot & (rel < half)
    ang = pos.astype(F32)[:, None] * inv[np.where(rot, rel % half, 0)][None, :]
    cos, sin = jnp.cos(ang), jnp.sin(ang)
    if paired:
        tabs = (jnp.where(rot, cos, 1.0), jnp.where(rot, -sin, 0.0), jnp.where(rot, sin, 0.0))
    else:
        tabs = (jnp.where(rot, cos, 1.0), jnp.where(upper, sin, 0.0), jnp.where(lower, -sin, 0.0))
    return jnp.concatenate(tabs, axis=1) * jnp.float32(scale)


def _prep_w_in(w):
    c_q = w[..., 0:512]
    c_kv = w[..., 512:768]
    k_pe = w[..., 768:800]
    dil = w[..., 800:800 + 2304]
    gates = w[..., 3104:5152]
    zeros = lambda n: jnp.zeros(w.shape[:-1] + (n,), w.dtype)
    lo, hi = k_pe[..., :MLA_HALF], k_pe[..., MLA_HALF:]
    pad = MLA_SLOT - MLA_NOPE - MLA_ROPE
    kslot = jnp.concatenate([zeros(MLA_NOPE), lo, lo, zeros(pad), zeros(MLA_NOPE), hi, hi, zeros(pad)], axis=-1)
    return jnp.concatenate([gates, c_q, c_kv, kslot, dil], axis=-1).astype(BF16)


def _pad_heads(w, width):
    w = jnp.pad(w, ((0, 0), (0, 0), (0, 0), (0, MLA_SLOT - width)))
    return w.reshape(w.shape[0], w.shape[1], MLA_HEADS * MLA_SLOT).astype(BF16)


def _prep_w_q(w):
    w = w.reshape(DEPTH, MLA_Q_LORA, MLA_HEADS // 2, 2, MLA_NOPE + MLA_ROPE)
    nope = w[..., :MLA_NOPE]
    lo = w[..., MLA_NOPE:MLA_NOPE + MLA_HALF]
    hi = w[..., MLA_NOPE + MLA_HALF:]
    both = lambda t: t.reshape(t.shape[:-2] + (2 * MLA_HALF,))
    zeros = jnp.zeros(w.shape[:3] + (MLA_SLOT - MLA_NOPE - MLA_ROPE,), w.dtype)
    slot0 = jnp.concatenate([nope[..., 0, :], both(lo), zeros], axis=-1)
    slot1 = jnp.concatenate([nope[..., 1, :], both(hi), zeros], axis=-1)
    return jnp.stack([slot0, slot1], axis=-2).reshape(DEPTH, MLA_Q_LORA, MLA_HEADS * MLA_SLOT).astype(BF16)


def _prep_w_kv(w):
    w = w.reshape(DEPTH, MLA_KV_LORA, MLA_HEADS, MLA_NOPE + MLA_V)
    return _pad_heads(w[..., :MLA_NOPE], MLA_NOPE), _pad_heads(w[..., MLA_NOPE:], MLA_V)


def _dest_kernel(ri_ref, ps_ref, o_ref):
    ri = ri_ref[...].astype(F32)
    lane = lax.broadcasted_iota(jnp.int32, ri.shape, 1)
    ps = ps_ref[...]

    def col(k):
        return jnp.sum(jnp.where(lane == k, ri, 0.0), axis=-1, keepdims=True)

    def dest(k):
        start = jnp.sum(jnp.where(lane == col(k).astype(jnp.int32), ps, 0.0), axis=-1, keepdims=True)
        return (start + col(2 + k)).astype(jnp.int32)

    o_ref[...] = jnp.where(lane == 0, dest(0), jnp.where(lane == 1, dest(1), 0))


def _dest_rows(ri, pstarts, tm):
    T = ri.shape[0]
    ps = jnp.zeros((1, 128), F32).at[0, :N_EXPERTS].set(pstarts.astype(F32))
    return pl.pallas_call(
        _dest_kernel,
        grid=(T // tm,),
        in_specs=[pl.BlockSpec((tm, 128), lambda i: (i, 0)), pl.BlockSpec((1, 128), lambda i: (0, 0))],
        out_specs=pl.BlockSpec((tm, 128), lambda i: (i, 0)),
        out_shape=jax.ShapeDtypeStruct((T, 128), jnp.int32),
        compiler_params=_cparams(("parallel",)),
        name="dest_rows",
    )(ri, ps)


def _dispatch_plan(ri, cnt, T, tm):
    counts = cnt[0, :N_EXPERTS].astype(jnp.int32)
    pcounts = (counts + FFN_BM - 1) // FFN_BM * FFN_BM
    pends = jnp.cumsum(pcounts)
    pstarts = pends - pcounts
    dest = _dest_rows(ri, pstarts, tm)[:, 0:TOP_K]
    n_blocks = (T * TOP_K) // FFN_BM + N_EXPERTS
    first_row = jnp.arange(n_blocks, dtype=jnp.int32) * FFN_BM
    block_e = jnp.minimum(jnp.sum((pends[None, :] <= first_row[:, None]).astype(jnp.int32), axis=1), N_EXPERTS - 1)
    nvb = pends[-1] // FFN_BM
    after = pends // FFN_BM
    next_of_expert = jnp.where(after < nvb, block_e[jnp.minimum(after, n_blocks - 1)], -1)
    run_of_expert = jnp.cumsum((pcounts > 0).astype(jnp.int32)) - 1
    ffn_plan = (block_e, next_of_expert[block_e].astype(jnp.int32), (run_of_expert[block_e] % 2).astype(jnp.int32),
                nvb.astype(jnp.int32).reshape(1))
    return dest.reshape(T * TOP_K), n_blocks * FFN_BM, ffn_plan


def kernel(x, p, positions, g_mix, w_in, g_q_lat, w_q_up, g_kv_lat, w_kv_up, w_branch_a, w_branch_b, w_out, g_ffn, w_router_grp, b_router_grp, w_router_exp, b_router_exp, w_exp_gate, w_exp_up, w_exp_down, g_ple, w_ple_gate, w_ple_proj, g_final):
    B, S, D = x.shape
    T = B * S
    TM_IN, TM_QKV, TQ, TM_POST, TM_ROUTE, TM_COMB = 512, 512, 512, 512, 512, 256

    pos = positions.reshape(T)
    dil_scale = DIL_HEAD_DIM ** -0.5 * LOG2E
    mla_scale = (MLA_NOPE + MLA_ROPE) ** -0.5 * LOG2E
    tab_dil = jnp.concatenate([_rope_tables(pos, ROPE_THETA, PARTIAL_ROT, DIL_HEAD_DIM, 0, dil_scale),
                               _rope_tables(pos, ROPE_THETA, PARTIAL_ROT, DIL_HEAD_DIM, 0, 1.0)], axis=1)
    tab_mla = jnp.concatenate([_rope_tables(pos, MLA_ROPE_THETA, MLA_ROPE, MLA_SLOT, MLA_NOPE, mla_scale, paired=True),
                               _rope_tables(pos, MLA_ROPE_THETA, MLA_ROPE, MLA_SLOT, MLA_NOPE, 1.0, paired=True)],
                              axis=1)

    w_in_p = _prep_w_in(w_in)
    wq = _prep_w_q(w_q_up)
    wk, wv = _prep_w_kv(w_kv_up)
    wr = jnp.concatenate([w_router_grp, w_router_exp, jnp.zeros((DEPTH, D, 128 - N_GROUPS - N_EXPERTS), F32)],
                         axis=-1).astype(BF16)
    br = jnp.concatenate([b_router_grp, b_router_exp.reshape(DEPTH, N_EXPERTS),
                          jnp.zeros((DEPTH, 128 - N_GROUPS - N_EXPERTS), F32)], axis=-1).reshape(DEPTH, 1, 128)
    wa, wb, wo = w_branch_a.astype(BF16), w_branch_b.astype(BF16), w_out.astype(BF16)
    wpg, wpp = w_ple_gate.astype(BF16), w_ple_proj.astype(BF16)
    gains = lambda g: g.reshape(DEPTH, 1, -1)

    xc = x.reshape(T, D)
    for i in range(DEPTH):
        main, d0, d1, d2 = _in_proj(xc, gains(g_mix), w_in_p, tab_dil, B, S, TM_IN, i)
        q, k, v = _mla_qkv(main, gains(g_q_lat), gains(g_kv_lat), wq, wk, wv, tab_mla, TM_QKV, i)
        o_a = _mla_attn(q, k, v, B, S, TQ)
        ogs, lss = [], []
        for (_, dil), qkv in zip(DIL_PATTERN, (d0, d1, d2)):
            og, ls = _dil_attn(qkv, B, dil, S // dil)
            ogs.append(og)
            lss.append(ls)
        x1, h2, lg = _post(xc, o_a, ogs, lss, main, wa, wb, wo, gains(g_ffn), wr, br, B, S, TM_POST, i)
        ri, rw, cnt = _route(lg, TM_ROUTE)
        dest, n_rows, ffn_plan = _dispatch_plan(ri, cnt, T, TM_ROUTE)
        xs = _sc_gather_rows(h2.reshape(T, ROW_TILE, 128), dest, n_out=n_rows)
        ys = _expert_ffn(*ffn_plan, xs.reshape(-1, 128), w_exp_gate, w_exp_up, w_exp_down, i)
        yg = _sc_gather_rows(ys.reshape(-1, ROW_TILE, 128), dest)
        xc = _combine_ple(yg.reshape(-1, 128), x1, rw, p.reshape(DEPTH * T, PLE_DIM), gains(g_ple), wpg, wpp,
                          g_final.reshape(1, D), TM_COMB, i)
    return xc.reshape(B, S, D)
```

```python
import functools
import math

import jax
import jax.numpy as jnp
import numpy as np
from jax import lax
from jax.experimental import pallas as pl
from jax.experimental.pallas import tpu as pltpu
from jax.experimental.pallas import tpu_sc as plsc

F32 = jnp.float32
BF16 = jnp.bfloat16

D_MODEL = 1024
DEPTH = 4
RMS_EPS = 1e-6
NEG = -1e30
LOG2E = math.log2(math.e)

MLA_HEADS = 16
MLA_Q_LORA = 512
MLA_KV_LORA = 256
MLA_NOPE = 64
MLA_ROPE = 32
MLA_V = 64
MLA_ROPE_THETA = 10000.0
MLA_SLOT = 128
MLA_HALF = MLA_ROPE // 2


def _mla_head_lanes(lane, head):
    l = lane % MLA_SLOT
    nope = jnp.logical_and(lane // MLA_SLOT == head, l < MLA_NOPE)
    rope = jnp.logical_and(l >= MLA_NOPE + head * MLA_HALF, l < MLA_NOPE + (head + 1) * MLA_HALF)
    return jnp.logical_or(nope, rope)

DIL_PATTERN = ((128, 1), (512, 4), (2048, 16))
DIL_GROUPS = 3
DIL_HEADS = 4
DIL_HEAD_DIM = 64
DIL_W = DIL_HEADS * DIL_HEAD_DIM
DIL_BLK = 128
DIL_UNROLL = 16
ROPE_THETA = 500000.0
PARTIAL_ROT = DIL_HEAD_DIM // 4
DIL_HALF = PARTIAL_ROT // 2
DIL_ROT_LANES = DIL_HEADS * DIL_HALF
DIL_REST = (DIL_HEAD_DIM - PARTIAL_ROT) // 2


def _dil_qk_head_of_lane(lane):
    l = lane % 128
    plain = l - DIL_ROT_LANES
    plain_head = sum((plain >= h * DIL_REST).astype(jnp.int32) for h in range(1, DIL_HEADS))
    return jnp.where(l < DIL_ROT_LANES, l // DIL_HALF, plain_head)

N_GROUPS = 8
EXPERTS_PER_GROUP = 8
N_EXPERTS = 64
TOP_K = 2
EXPERT_FF = 256
PLE_DIM = 256

IN_TN = 768
MAIN_COLS = 3072
IN_COLS_PAD = MAIN_COLS + DIL_GROUPS * 3 * DIL_W
N_MAIN_TILES = MAIN_COLS // IN_TN

FFN_BM = 256
ROW_TILE = D_MODEL // 256
U32 = jnp.uint32
VMEM_LIMIT = 48 * 1024 * 1024

SC_CORES = 2
SC_SUBCORES = 16
SC_LANES = 16
SC_CHUNK = 64


def _cparams(sem, bounds_checks=True):
    return pltpu.CompilerParams(dimension_semantics=sem, vmem_limit_bytes=VMEM_LIMIT,
                                disable_bounds_checks=not bounds_checks)


def _layer_spec(shape, layer, col=None):
    def index(*grid_idx):
        return (layer,) + (0,) * (len(shape) - 1) + ((col(*grid_idx),) if col else (0,))

    return pl.BlockSpec((None,) + tuple(shape), index)


def _rms(x, g):
    return x * lax.rsqrt(jnp.mean(x * x, axis=-1, keepdims=True) + RMS_EPS) * g


def _rope128(x, c, s_up, s_dn, half):
    return x * c + pltpu.roll(x, half, 1) * s_up + pltpu.roll(x, 128 - half, 1) * s_dn


def _in_kernel(x_ref, g_ref, w_ref, tab_ref, main_ref, d0_ref, d1_ref, d2_ref, xn_ref, acc_ref):
    j = pl.program_id(1)

    @pl.when(j == 0)
    def _():
        xn_ref[...] = _rms(x_ref[...], g_ref[...]).astype(BF16)

    acc = jnp.dot(xn_ref[...], w_ref[...], preferred_element_type=F32)

    @pl.when(j < N_MAIN_TILES)
    def _():
        main_ref[...] = acc.astype(BF16)

    def dil_tile(out_ref, dil):
        chunk = lambda c: acc[:, c * 128:(c + 1) * 128]
        for base, t0 in ((0, 0), (2, 3)):
            cos, sin_n, sin_p = [tab_ref[:, (t0 + t) * 128:(t0 + t + 1) * 128] for t in range(3)]
            lo, hi = chunk(base), chunk(base + 1)
            acc_ref[base] = lo * cos + hi * sin_n
            acc_ref[base + 1] = hi * cos + lo * sin_p
        for c in range(4, 6):
            acc_ref[c] = chunk(c)
        rows = acc_ref.shape[1] // dil
        for r in range(dil):
            for c in range(6):
                out_ref[0, r, :, c * 128:(c + 1) * 128] = acc_ref[c, pl.ds(r, rows, stride=dil), :].astype(BF16)

    for gi, (_, dil) in enumerate(DIL_PATTERN):
        pl.when(j == N_MAIN_TILES + gi)(functools.partial(dil_tile, (d0_ref, d1_ref, d2_ref)[gi], dil))


def _in_proj(x2d, g, w, tab, B, S, tm, layer):
    T = x2d.shape[0]
    nt = S // tm
    dil_shapes = [jax.ShapeDtypeStruct((B, d, S // d, 3 * DIL_W), BF16) for _, d in DIL_PATTERN]
    dil_specs = [pl.BlockSpec((1, d, tm // d, 3 * DIL_W), lambda i, j, nt=nt: (i // nt, 0, i % nt, 0))
                 for _, d in DIL_PATTERN]
    return pl.pallas_call(
        _in_kernel,
        grid=(T // tm, IN_COLS_PAD // IN_TN),
        in_specs=[
            pl.BlockSpec((tm, D_MODEL), lambda i, j: (i, 0)),
            _layer_spec((1, D_MODEL), layer),
            _layer_spec((D_MODEL, IN_TN), layer, col=lambda i, j: j),
            pl.BlockSpec((tm, 6 * 128), lambda i, j: (i, 0)),
        ],
        out_specs=[pl.BlockSpec((tm, IN_TN), lambda i, j: (i, jnp.minimum(j, N_MAIN_TILES - 1)))] + dil_specs,
        out_shape=[jax.ShapeDtypeStruct((T, MAIN_COLS), BF16)] + dil_shapes,
        scratch_shapes=[pltpu.VMEM((tm, D_MODEL), BF16), pltpu.VMEM((IN_TN // 128, tm, 128), F32)],
        compiler_params=_cparams(("parallel", "arbitrary")),
        name="in_proj",
    )(x2d, g, w, tab)


def _qkv_kernel(cq_ref, ckv_ref, kpe_ref, gq_ref, gkv_ref, wq_ref, wk_ref, wv_ref, tab_ref,
                q_ref, k_ref, v_ref):
    qn = _rms(cq_ref[...].astype(F32), gq_ref[...]).astype(BF16)
    kvn = _rms(ckv_ref[...].astype(F32), gkv_ref[...]).astype(BF16)
    qacc = jnp.dot(qn, wq_ref[...], preferred_element_type=F32)
    kacc = jnp.dot(kvn, wk_ref[...], preferred_element_type=F32)
    vacc = jnp.dot(kvn, wv_ref[...], preferred_element_type=F32)
    lane_v = lax.broadcasted_iota(jnp.int32, vacc.shape, 1)
    v_ref[...] = jnp.where(lane_v % MLA_SLOT < MLA_V, vacc, 1.0).astype(BF16)
    cq, snq, spq, ck, snk, spk = [tab_ref[:, t * 128:(t + 1) * 128] for t in range(6)]
    kpe_lo = kpe_ref[:, 0:128].astype(F32)
    kpe_hi = kpe_ref[:, 128:256].astype(F32)
    krot = (kpe_lo * ck + kpe_hi * snk, kpe_hi * ck + kpe_lo * spk)
    for pair in range(MLA_HEADS // 2):
        s0 = slice(2 * pair * MLA_SLOT, (2 * pair + 1) * MLA_SLOT)
        s1 = slice((2 * pair + 1) * MLA_SLOT, (2 * pair + 2) * MLA_SLOT)
        q0, q1 = qacc[:, s0], qacc[:, s1]
        q_ref[:, s0] = (q0 * cq + q1 * snq).astype(BF16)
        q_ref[:, s1] = (q1 * cq + q0 * spq).astype(BF16)
        k_ref[:, s0] = (kacc[:, s0] + krot[0]).astype(BF16)
        k_ref[:, s1] = (kacc[:, s1] + krot[1]).astype(BF16)


def _mla_qkv(main, gq, gkv, wq, wk, wv, tab, tm, layer):
    T = main.shape[0]
    return pl.pallas_call(
        _qkv_kernel,
        grid=(T // tm,),
        in_specs=[
            pl.BlockSpec((tm, MLA_Q_LORA), lambda i: (i, 2048 // MLA_Q_LORA)),
            pl.BlockSpec((tm, MLA_KV_LORA), lambda i: (i, 2560 // MLA_KV_LORA)),
            pl.BlockSpec((tm, 256), lambda i: (i, 2816 // 256)),
            _layer_spec((1, MLA_Q_LORA), layer),
            _layer_spec((1, MLA_KV_LORA), layer),
            _layer_spec((MLA_Q_LORA, MLA_HEADS * MLA_SLOT), layer),
            _layer_spec((MLA_KV_LORA, MLA_HEADS * MLA_SLOT), layer),
            _layer_spec((MLA_KV_LORA, MLA_HEADS * MLA_SLOT), layer),
            pl.BlockSpec((tm, 6 * 128), lambda i: (i, 0)),
        ],
        out_specs=[
            pl.BlockSpec((tm, MLA_HEADS * MLA_SLOT), lambda i: (i, 0)),
            pl.BlockSpec((tm, MLA_HEADS * MLA_SLOT), lambda i: (i, 0)),
            pl.BlockSpec((tm, MLA_HEADS * MLA_SLOT), lambda i: (i, 0)),
        ],
        out_shape=[
            jax.ShapeDtypeStruct((T, MLA_HEADS * MLA_SLOT), BF16),
            jax.ShapeDtypeStruct((T, MLA_HEADS * MLA_SLOT), BF16),
            jax.ShapeDtypeStruct((T, MLA_HEADS * MLA_SLOT), BF16),
        ],
        compiler_params=_cparams(("parallel",)),
        name="mla_qkv",
    )(main, main, main, gq, gkv, wq, wk, wv, tab)


def _mla_attn_kernel(qa_ref, qb_ref, k_ref, v_ref, o_ref, m_ref, acc_ref, *, tq):
    p_id = pl.program_id(2)
    nq = k_ref.shape[1] // tq
    lane = lax.broadcasted_iota(jnp.int32, (tq, 2 * MLA_SLOT), 1)
    qh = []
    for q_ref in (qa_ref, qb_ref):
        qp = q_ref[0]
        zero = jnp.zeros_like(qp)
        qh.append([jnp.where(_mla_head_lanes(lane, h), qp, zero) for h in range(2)])
    hq = tq // 2
    dn = (((1,), (1,)), ((), ()))

    def step(blk, start, width, r0, mask, first=False):
        kb = k_ref[0, pl.ds(start, width), :]
        vb = v_ref[0, pl.ds(start, width), :]
        rows = tq - r0
        s_pair = lax.dot_general(jnp.concatenate([qh[blk][0][r0:], qh[blk][1][r0:]], axis=0), kb, dn,
                                 preferred_element_type=F32)
        for h in range(2):
            s = s_pair[h * rows:(h + 1) * rows]
            if mask is not None:
                s = jnp.where(mask, s, NEG)
            m_cur = jnp.max(s, axis=-1, keepdims=True)
            if first:
                m_new = jnp.broadcast_to(m_cur, (tq - r0, 128))
            else:
                m_prev = m_ref[blk, h, r0:, :]
                m_new = jnp.maximum(m_prev, m_cur)
            p = jnp.exp2(s - jnp.concatenate([m_new] * (width // 128), axis=-1))
            pv = jnp.dot(p.astype(BF16), vb[:, h * MLA_SLOT:(h + 1) * MLA_SLOT], preferred_element_type=F32)
            if first:
                acc_ref[blk, h, r0:, :] = pv
            else:
                acc_ref[blk, h, r0:, :] = jnp.exp2(m_prev - m_new) * acc_ref[blk, h, r0:, :] + pv
            m_ref[blk, h, r0:, :] = m_new

    def causal(rows):
        return lax.broadcasted_iota(jnp.int32, (rows, hq), 1) <= lax.broadcasted_iota(jnp.int32, (rows, hq), 0)

    def q_blocks(n_a):
        n_b = nq - 1 - n_a
        for blk, n_full in ((1, n_b), (0, n_a)):
            for j in range(n_full):
                step(blk, j * tq, tq, 0, None, first=j == 0)
            step(blk, n_full * tq, hq, 0, causal(tq), first=n_full == 0)
            step(blk, n_full * tq + hq, hq, hq, causal(hq))

    for n_a in range(nq // 2):
        pl.when(p_id == n_a)(functools.partial(q_blocks, n_a))
    lane_o = lax.broadcasted_iota(jnp.int32, (tq, 2 * MLA_V), 1)
    for blk in range(2):
        outs = []
        for h in range(2):
            a = acc_ref[blk, h]
            outs.append(a / pltpu.roll(a, MLA_V, 1))
        o_ref[0, blk, 0] = jnp.where(lane_o < MLA_V, outs[0], pltpu.roll(outs[1], MLA_V, 1)).astype(BF16)


def _mla_attn(q, k, v, B, S, tq):
    nq = S // tq
    q = q.reshape(B, S, MLA_HEADS * MLA_SLOT)
    k = k.reshape(B, S, MLA_HEADS * MLA_SLOT)
    v = v.reshape(B, S, MLA_HEADS * MLA_SLOT)
    return pl.pallas_call(
        functools.partial(_mla_attn_kernel, tq=tq),
        grid=(B, MLA_HEADS // 2, nq // 2),
        in_specs=[
            pl.BlockSpec((1, tq, 2 * MLA_SLOT), lambda b, h, p: (b, p, h)),
            pl.BlockSpec((1, tq, 2 * MLA_SLOT), lambda b, h, p: (b, nq - 1 - p, h)),
            pl.BlockSpec((1, S, 2 * MLA_SLOT), lambda b, h, p: (b, 0, h)),
            pl.BlockSpec((1, S, 2 * MLA_SLOT), lambda b, h, p: (b, 0, h)),
        ],
        out_specs=pl.BlockSpec((1, 2, 1, tq, 2 * MLA_V), lambda b, h, p: (b, 0, p, 0, h)),
        out_shape=jax.ShapeDtypeStruct((B, 2, nq // 2, tq, MLA_HEADS * MLA_V), BF16),
        scratch_shapes=[pltpu.VMEM((2, 2, tq, 128), F32), pltpu.VMEM((2, 2, tq, MLA_SLOT), F32)],
        compiler_params=_cparams(("parallel", "parallel", "arbitrary")),
        name="mla_attn",
    )(q, q, k, v)


def _dil_kernel(qkv_ref, o_ref, lse_ref, bias_ref, *, nb):
    dil = qkv_ref.shape[1]
    width = 2 * DIL_BLK if nb > 1 else DIL_BLK
    row = lax.broadcasted_iota(jnp.int32, (DIL_BLK, width), 0)
    col = lax.broadcasted_iota(jnp.int32, (DIL_BLK, width), 1)
    bias_ref[0] = jnp.where(col <= row, 0.0, NEG)
    if nb > 1:
        later = jnp.logical_or(jnp.logical_and(col >= DIL_BLK, col - DIL_BLK <= row),
                               jnp.logical_and(col < DIL_BLK, col >= row))
        bias_ref[1] = jnp.where(later, 0.0, NEG)
    lane = lax.broadcasted_iota(jnp.int32, (DIL_BLK, DIL_W), 1)
    head_of_lane = lane // DIL_HEAD_DIM
    qk_head_of_lane = _dil_qk_head_of_lane(lane)
    dn = (((1,), (1,)), ((), ()))

    def by_head(parts):
        out = parts[DIL_HEADS - 1]
        for h in range(DIL_HEADS - 2, -1, -1):
            out = jnp.where(head_of_lane == h, parts[h], out)
        return out

    def unit(u):
        r = u // nb
        n = u % nb
        q0 = pl.multiple_of(n * DIL_BLK, DIL_BLK)
        q = qkv_ref[0, r, pl.ds(q0, DIL_BLK), 0:DIL_W]
        if nb > 1:
            w0 = pl.multiple_of(jnp.maximum(n - 1, 0) * DIL_BLK, DIL_BLK)
            bias = bias_ref[jnp.minimum(n, 1)]
        else:
            w0 = 0
            bias = bias_ref[0]
        kw = qkv_ref[0, r, pl.ds(w0, width), DIL_W:2 * DIL_W]
        vw = qkv_ref[0, r, pl.ds(w0, width), 2 * DIL_W:3 * DIL_W]
        zero = jnp.zeros_like(q)
        qs = jnp.concatenate([jnp.where(qk_head_of_lane == h, q, zero) for h in range(DIL_HEADS)], axis=0)
        s = lax.dot_general(qs, kw, dn, preferred_element_type=F32)
        s = (s.reshape(DIL_HEADS, DIL_BLK, width) + bias[None]).reshape(DIL_HEADS * DIL_BLK, width)
        m = jnp.max(s, axis=-1, keepdims=True)
        e = jnp.exp2(s - m)
        den = jnp.sum(e, axis=-1, keepdims=True)
        pv = jnp.dot(e.astype(BF16), vw, preferred_element_type=F32)
        lse = m + jnp.log2(den)
        blk = lambda t, h: t[h * DIL_BLK:(h + 1) * DIL_BLK]
        o = by_head([blk(pv, h) for h in range(DIL_HEADS)]) / by_head(
            [jnp.broadcast_to(blk(den, h), (DIL_BLK, DIL_W)) for h in range(DIL_HEADS)])
        o_ref[0, r, pl.ds(q0, DIL_BLK), :] = o.astype(BF16)
        lse_ref[0, r, pl.ds(q0, DIL_BLK), :] = by_head(
            [jnp.broadcast_to(blk(lse, h), (DIL_BLK, DIL_W)) for h in range(DIL_HEADS)])

    def body(t, carry):
        for u in range(DIL_UNROLL):
            unit(DIL_UNROLL * t + u)
        return carry

    lax.fori_loop(0, dil * nb // DIL_UNROLL, body, 0)


def _dil_attn(qkv, B, dil, L):
    nb = L // DIL_BLK
    return pl.pallas_call(
        functools.partial(_dil_kernel, nb=nb),
        grid=(B,),
        in_specs=[pl.BlockSpec((1, dil, L, 3 * DIL_W), lambda b: (b, 0, 0, 0))],
        out_specs=[
            pl.BlockSpec((1, dil, L, DIL_W), lambda b: (b, 0, 0, 0)),
            pl.BlockSpec((1, dil, L, DIL_W), lambda b: (b, 0, 0, 0)),
        ],
        out_shape=[
            jax.ShapeDtypeStruct((B, dil, L, DIL_W), BF16),
            jax.ShapeDtypeStruct((B, dil, L, DIL_W), F32),
        ],
        scratch_shapes=[pltpu.VMEM((2, DIL_BLK, 2 * DIL_BLK if nb > 1 else DIL_BLK), F32)],
        compiler_params=_cparams(("parallel",)),
        name=f"dil_attn_d{dil}",
    )(qkv)


def _post_kernel(x_ref, oa_ref, og0_ref, og1_ref, og2_ref, ls0_ref, ls1_ref, ls2_ref, gates_ref,
                 wa_ref, wb_ref, wo_ref, gffn_ref, wr_ref, br_ref,
                 x1_ref, h2_ref, lg_ref, ob_ref, *, tm):
    ls =[r[0] for r in (ls0_ref, ls1_ref, ls2_ref)]
    og = [r[0] for r in (og0_ref, og1_ref, og2_ref)]
    for gi, (_, dil) in enumerate(DIL_PATTERN):
        rows = tm // dil
        for r in range(dil):
            for c in range(2):
                sl = slice(c * 128, (c + 1) * 128)
                ob_ref[2 * gi + c, pl.ds(r, rows, stride=dil), :] = og[gi][r][:, sl].astype(F32)
                ob_ref[6 + 2 * gi + c, pl.ds(r, rows, stride=dil), :] = ls[gi][r][:, sl]

    def tok_major(k):
        return jnp.concatenate([ob_ref[2 * k], ob_ref[2 * k + 1]], axis=-1)

    l0, l1, l2 = tok_major(3), tok_major(4), tok_major(5)
    mx = jnp.maximum(jnp.maximum(l0, l1), l2)
    w0, w1, w2 = jnp.exp2(l0 - mx), jnp.exp2(l1 - mx), jnp.exp2(l2 - mx)
    ob = (w0 * tok_major(0) + w1 * tok_major(1) + w2 * tok_major(2)) / (w0 + w1 + w2)
    ya = jnp.dot(oa_ref[0, 0, 0], wa_ref[...], preferred_element_type=F32)
    yb = jnp.dot(ob.astype(BF16), wb_ref[...], preferred_element_type=F32)
    merged = (jax.nn.sigmoid(gates_ref[:, 0:D_MODEL].astype(F32)) * ya
              + jax.nn.sigmoid(gates_ref[:, D_MODEL:2 * D_MODEL].astype(F32)) * yb)
    x1 = x_ref[...] + jnp.dot(merged.astype(BF16), wo_ref[...], preferred_element_type=F32)
    x1_ref[...] = x1
    h2 = _rms(x1, gffn_ref[...])
    _rows_to_tiles(h2_ref, h2)
    lg_ref[...] = jnp.dot(h2.astype(BF16), wr_ref[...], preferred_element_type=F32) + br_ref[...]


def _post(x2d, oa, ogs, lss, main, wa, wb, wo, gffn, wr, br, B, S, tm, layer):
    T = x2d.shape[0]
    nt = S // tm
    res_specs = [pl.BlockSpec((1, d, tm // d, DIL_W), lambda i, nt=nt: (i // nt, 0, i % nt, 0))
                 for _, d in DIL_PATTERN]
    assert oa.shape[3] == tm and oa.shape[2] * 2 == nt

    def oa_index(i):
        it = i % nt
        late = it >= nt // 2
        return (i // nt, late.astype(jnp.int32), jnp.where(late, nt - 1 - it, it), 0, 0)

    return pl.pallas_call(
        functools.partial(_post_kernel, tm=tm),
        grid=(T // tm,),
        in_specs=[pl.BlockSpec((tm, D_MODEL), lambda i: (i, 0)),
                  pl.BlockSpec((1, 1, 1, tm, D_MODEL), oa_index)]
                 + res_specs + res_specs
                 + [pl.BlockSpec((tm, 2 * D_MODEL), lambda i: (i, 0)),
                    _layer_spec((D_MODEL, D_MODEL), layer),
                    _layer_spec((DIL_W, D_MODEL), layer),
                    _layer_spec((D_MODEL, D_MODEL), layer),
                    _layer_spec((1, D_MODEL), layer),
                    _layer_spec((D_MODEL, 128), layer),
                    _layer_spec((1, 128), layer)],
        out_specs=[pl.BlockSpec((tm, D_MODEL), lambda i: (i, 0)),
                   pl.BlockSpec((tm * ROW_TILE, 128), lambda i: (i, 0)),
                   pl.BlockSpec((tm, 128), lambda i: (i, 0))],
        out_shape=[jax.ShapeDtypeStruct((T, D_MODEL), F32),
                   jax.ShapeDtypeStruct((T * ROW_TILE, 128), U32),
                   jax.ShapeDtypeStruct((T, 128), F32)],
        scratch_shapes=[pltpu.VMEM((12, tm, 128), F32)],
        compiler_params=_cparams(("parallel",)),
        name="post_attn",
    )(x2d, oa, *ogs, *lss, main, wa, wb, wo, gffn, wr, br)


def _route_kernel(lg_ref, ri_ref, rw_ref, cnt_ref, carry_ref, lower_ref, *, tm):
    i = pl.program_id(0)

    @pl.when(i == 0)
    def _():
        carry_ref[...] = jnp.zeros(carry_ref.shape, F32)
        r_i = lax.broadcasted_iota(jnp.int32, (tm, tm), 0)
        c_i = lax.broadcasted_iota(jnp.int32, (tm, tm), 1)
        lower_ref[...] = jnp.where(c_i < r_i, 1.0, 0.0).astype(BF16)

    lg = lg_ref[...]
    lane = lax.broadcasted_iota(jnp.int32, lg.shape, 1)
    lane_f = lane.astype(F32)
    ninf = jnp.float32(-jnp.inf)

    def first_max(vals):
        vmax = jnp.max(vals, axis=-1, keepdims=True)
        idx = jnp.min(jnp.where(vals == vmax, lane_f, 128.0), axis=-1, keepdims=True)
        return vmax, idx.astype(jnp.int32)

    gl = jnp.where(lane < N_GROUPS, lg, ninf)
    gmax, g_sel = first_max(gl)
    p_g = 1.0 / jnp.sum(jnp.exp(gl - gmax), axis=-1, keepdims=True)
    lo = N_GROUPS + g_sel * EXPERTS_PER_GROUP
    el = jnp.where(jnp.logical_and(lane >= lo, lane < lo + EXPERTS_PER_GROUP), lg, ninf)
    v0, i0 = first_max(el)
    v1, i1 = first_max(jnp.where(lane == i0, ninf, el))
    t = jnp.exp(v1 - v0)
    w0 = p_g / (1.0 + t)
    w1 = p_g * t / (1.0 + t)
    e0 = i0 - N_GROUPS
    e1 = i1 - N_GROUPS
    hit0 = lane == e0
    hit1 = lane == e1
    oh = jnp.where(jnp.logical_or(hit0, hit1), 1.0, 0.0).astype(F32)
    excl = jnp.dot(lower_ref[...], oh.astype(BF16), preferred_element_type=F32) + carry_ref[...]
    r0 = jnp.sum(jnp.where(hit0, excl, 0.0), axis=-1, keepdims=True).astype(jnp.int32)
    r1 = jnp.sum(jnp.where(hit1, excl, 0.0), axis=-1, keepdims=True).astype(jnp.int32)
    carry_ref[...] = carry_ref[...] + jnp.sum(oh, axis=0, keepdims=True)
    zi = jnp.zeros(lg.shape, jnp.int32)
    ri_ref[...] = jnp.where(lane == 0, e0, jnp.where(lane == 1, e1, jnp.where(lane == 2, r0, jnp.where(lane == 3, r1, zi))))
    rw_ref[...] = jnp.where(lane == 0, w0, jnp.where(lane == 1, w1, jnp.zeros(lg.shape, F32)))
    cnt_ref[...] = carry_ref[...]


def _route(lg, tm):
    T = lg.shape[0]
    return pl.pallas_call(
        functools.partial(_route_kernel, tm=tm),
        grid=(T // tm,),
        in_specs=[pl.BlockSpec((tm, 128), lambda i: (i, 0))],
        out_specs=[pl.BlockSpec((tm, 128), lambda i: (i, 0)),
                   pl.BlockSpec((tm, 128), lambda i: (i, 0)),
                   pl.BlockSpec((1, 128), lambda i: (0, 0))],
        out_shape=[jax.ShapeDtypeStruct((T, 128), jnp.int32),
                   jax.ShapeDtypeStruct((T, 128), F32),
                   jax.ShapeDtypeStruct((1, 128), F32)],
        scratch_shapes=[pltpu.VMEM((1, 128), F32), pltpu.VMEM((tm, tm), BF16)],
        compiler_params=_cparams(("arbitrary",)),
        name="route",
    )(lg)


def _sc_gather_rows(table, idx, n_out=None):
    inverse = n_out is not None
    n_src = idx.shape[0]
    n = n_out if inverse else n_src
    n_workers = SC_CORES * SC_SUBCORES
    per_w = n // n_workers
    n_chunks = per_w // SC_CHUNK
    assert per_w * n_workers == n and n_chunks * SC_CHUNK == per_w and n_chunks % 2 == 0
    mesh = plsc.VectorSubcoreMesh(core_axis_name="c", subcore_axis_name="s",
                                  num_cores=SC_CORES, num_subcores=SC_SUBCORES)

    @functools.partial(
        pl.kernel, mesh=mesh,
        out_type=jax.ShapeDtypeStruct((n,) + table.shape[1:], table.dtype),
        scratch_types=[pltpu.VMEM((per_w,), jnp.int32),
                       pltpu.VMEM((SC_CHUNK,) + table.shape[1:], table.dtype),
                       pltpu.VMEM((SC_CHUNK,) + table.shape[1:], table.dtype),
                       pltpu.SemaphoreType.DMA, pltpu.SemaphoreType.DMA,
                       pltpu.VMEM((n_src if inverse else SC_LANES,), jnp.int32)],
        compiler_params=pltpu.CompilerParams(use_tc_tiling_on_sc=True, needs_layout_passes=not inverse),
        name="sc_dispatch_rows" if inverse else "sc_gather_rows",
    )
    def gather(table_hbm, idx_hbm, out_hbm, idx_v, rows_a, rows_b, sem_a, sem_b, map_v):
        wid = lax.axis_index("s") * SC_CORES + lax.axis_index("c")
        base = wid * per_w
        if inverse:
            pltpu.sync_copy(idx_hbm, map_v)
            lanes = lax.iota(jnp.int32, SC_LANES)

            @pl.loop(0, per_w // SC_LANES)
            def _(j):
                idx_v[pl.ds(j * SC_LANES, SC_LANES)] = lax.rem(base + j * SC_LANES + lanes, table.shape[0])

            @pl.loop(0, n_src // SC_LANES)
            def _(a):
                local = map_v[pl.ds(a * SC_LANES, SC_LANES)] - base
                mine = jnp.logical_and(local >= 0, local < per_w)
                plsc.store_scatter(idx_v, [local], lax.div(a * SC_LANES + lanes, TOP_K), mask=mine)
        else:
            pltpu.sync_copy(idx_hbm.at[pl.ds(base, per_w)], idx_v)

        def fetch(chunk, rows_v, sem):
            return pltpu.make_async_copy(table_hbm.at[idx_v.at[pl.ds(chunk * SC_CHUNK, SC_CHUNK)]], rows_v, sem)

        def flush(chunk, rows_v):
            pltpu.sync_copy(rows_v, out_hbm.at[pl.ds(base + chunk * SC_CHUNK, SC_CHUNK)])

        fetch(0, rows_a, sem_a).start()

        @pl.loop(0, n_chunks, step=2)
        def _(c):
            fetch(c + 1, rows_b, sem_b).start()
            fetch(c, rows_a, sem_a).wait()
            flush(c, rows_a)

            @pl.when(c + 2 < n_chunks)
            def _():
                fetch(c + 2, rows_a, sem_a).start()

            fetch(c + 1, rows_b, sem_b).wait()
            flush(c + 1, rows_b)

    return gather(table, idx)


def _rows_from_tiles(ref, first, n_rows, stride=ROW_TILE):
    words = [ref[pl.ds(first + c, n_rows, stride=stride), :] for c in range(ROW_TILE)]
    lo = [pltpu.bitcast(w << 16, F32) for w in words]
    hi = [pltpu.bitcast(w & U32(0xFFFF0000), F32) for w in words]
    return jnp.concatenate(lo + hi, axis=-1)


def _rows_to_tiles(ref, val):
    half = D_MODEL // 2

    def bits(x):
        return pltpu.bitcast(x.astype(BF16).astype(F32), U32)

    for c in range(ROW_TILE):
        lo = bits(val[:, c * 128:(c + 1) * 128])
        hi = bits(val[:, half + c * 128:half + (c + 1) * 128])
        ref[pl.ds(c, val.shape[0], stride=ROW_TILE), :] = hi | (lo >> 16)


def _ffn_kernel(be_ref, nx_ref, par_ref, nvb_ref, x_ref, w1_hbm, w3_hbm, w2_hbm, y_ref,
                w1f_ref, w3f_ref, w2f_ref, w1b_ref, w3b_ref, w2b_ref, sem, *, layer):
    b = pl.program_id(0)
    nvb = nvb_ref[0]
    expert = be_ref[b]
    slot = par_ref[b]
    new_expert = jnp.logical_or(b == 0, expert != be_ref[jnp.maximum(b - 1, 0)])

    def fetch(e, sl):
        return [pltpu.make_async_copy(w_hbm.at[layer, e], wf_ref.at[sl], sem.at[sl])
                for w_hbm, wf_ref in ((w1_hbm, w1f_ref), (w3_hbm, w3f_ref), (w2_hbm, w2f_ref))]

    @pl.when(jnp.logical_and(b == 0, nvb > 0))
    def _():
        for cp in fetch(expert, slot):
            cp.start()

    @pl.when(jnp.logical_and(b < nvb, new_expert))
    def _():
        for cp in fetch(expert, slot):
            cp.wait()

        @pl.when(nx_ref[b] >= 0)
        def _():
            for cp in fetch(nx_ref[b], 1 - slot):
                cp.start()

        w1b_ref[...] = w1f_ref[slot].astype(BF16)
        w3b_ref[...] = w3f_ref[slot].astype(BF16)
        w2b_ref[...] = w2f_ref[slot].astype(BF16)

    @pl.when(b < nvb)
    def _():
        xb = _rows_from_tiles(x_ref, 0, FFN_BM).astype(BF16)
        h1 = jnp.dot(xb, w1b_ref[...], preferred_element_type=F32)
        h3 = jnp.dot(xb, w3b_ref[...], preferred_element_type=F32)
        a = (jax.nn.silu(h1) * h3).astype(BF16)
        _rows_to_tiles(y_ref, jnp.dot(a, w2b_ref[...], preferred_element_type=F32))

    @pl.when(b >= nvb)
    def _():
        y_ref[...] = jnp.zeros(y_ref.shape, U32)


def _expert_ffn(block_e, next_e, parity, nvb, xs, w1, w3, w2, layer):
    nb = block_e.shape[0]

    def used(b, *prefetch):
        return jnp.minimum(b, jnp.maximum(prefetch[-1][0] - 1, 0))

    return pl.pallas_call(
        functools.partial(_ffn_kernel, layer=layer),
        grid_spec=pltpu.PrefetchScalarGridSpec(
            num_scalar_prefetch=4,
            grid=(nb,),
            in_specs=[
                pl.BlockSpec((FFN_BM * ROW_TILE, 128), lambda b, *prefetch: (used(b, *prefetch), 0)),
                pl.BlockSpec(memory_space=pl.ANY),
                pl.BlockSpec(memory_space=pl.ANY),
                pl.BlockSpec(memory_space=pl.ANY),
            ],
            out_specs=pl.BlockSpec((FFN_BM * ROW_TILE, 128), lambda b, *prefetch: (b, 0)),
            scratch_shapes=[pltpu.VMEM((2, D_MODEL, EXPERT_FF), F32), pltpu.VMEM((2, D_MODEL, EXPERT_FF), F32),
                            pltpu.VMEM((2, EXPERT_FF, D_MODEL), F32),
                            pltpu.VMEM((D_MODEL, EXPERT_FF), BF16), pltpu.VMEM((D_MODEL, EXPERT_FF), BF16),
                            pltpu.VMEM((EXPERT_FF, D_MODEL), BF16),
                            pltpu.SemaphoreType.DMA((2,))],
        ),
        out_shape=jax.ShapeDtypeStruct((nb * FFN_BM * ROW_TILE, 128), U32),
        compiler_params=_cparams(("arbitrary",)),
        name="expert_ffn",
    )(block_e, next_e, parity, nvb, xs, w1, w3, w2)


def _comb_kernel(yg_ref, x1_ref, rw_ref, p_ref, gple_ref, wpg_ref, wpp_ref, gout_ref, o_ref, *, tm, final):
    w = rw_ref[...]
    y0 = _rows_from_tiles(yg_ref, 0, tm, stride=TOP_K * ROW_TILE)
    y1 = _rows_from_tiles(yg_ref, ROW_TILE, tm, stride=TOP_K * ROW_TILE)
    x2 = x1_ref[...] + (y0 * w[:, 0:1] + y1 * w[:, 1:2])
    e = jnp.dot(p_ref[...].astype(BF16), wpp_ref[...], preferred_element_type=F32)
    gate = jax.nn.sigmoid(jnp.dot(_rms(x2, gple_ref[...]).astype(BF16), wpg_ref[...], preferred_element_type=F32))
    x3 = x2 + gate * e
    o_ref[...] = _rms(x3, gout_ref[...]) if final else x3


def _combine_ple(yg, x1, rw, p2d, gple, wpg, wpp, gout, tm, layer):
    T = x1.shape[0]
    nt = T // tm
    return pl.pallas_call(
        functools.partial(_comb_kernel, tm=tm, final=layer == DEPTH - 1),
        grid=(nt,),
        in_specs=[
            pl.BlockSpec((tm * TOP_K * ROW_TILE, 128), lambda i: (i, 0)),
            pl.BlockSpec((tm, D_MODEL), lambda i: (i, 0)),
            pl.BlockSpec((tm, 128), lambda i: (i, 0)),
            pl.BlockSpec((tm, PLE_DIM), lambda i: (layer * nt + i, 0)),
            _layer_spec((1, D_MODEL), layer),
            _layer_spec((D_MODEL, D_MODEL), layer),
            _layer_spec((PLE_DIM, D_MODEL), layer),
            pl.BlockSpec((1, D_MODEL), lambda i: (0, 0)),
        ],
        out_specs=pl.BlockSpec((tm, D_MODEL), lambda i: (i, 0)),
        out_shape=jax.ShapeDtypeStruct((T, D_MODEL), F32),
        compiler_params=_cparams(("parallel",)),
        name="combine_ple",
    )(yg, x1, rw, p2d, gple, wpg, wpp, gout)


def _rope_tables(pos, theta, rot_dim, period, offset, scale, paired=False, span=None):
    half = rot_dim // 2
    span = rot_dim if span is None else span
    inv = jnp.float32(theta) ** (-jnp.arange(half, dtype=F32) * 2.0 / rot_dim)
    rel = np.arange(128) % period - offset
    rot = (rel >= 0) & (rel < span)
    upper = rot & (rel >= half)
    lower = rot & (rel < half)
    ang = pos.astype(F32)[:, None] * inv[np.where(rot, rel % half, 0)][None, :]
    cos, sin = jnp.cos(ang), jnp.sin(ang)
    if paired:
        tabs = (jnp.where(rot, cos, 1.0), jnp.where(rot, -sin, 0.0), jnp.where(rot, sin, 0.0))
    else:
        tabs = (jnp.where(rot, cos, 1.0), jnp.where(upper, sin, 0.0), jnp.where(lower, -sin, 0.0))
    return jnp.concatenate(tabs, axis=1) * jnp.float32(scale)


def _prep_w_in(w):
    c_q = w[..., 0:512]
    c_kv = w[..., 512:768]
    k_pe = w[..., 768:800]
    dil = w[..., 800:800 + 2304].reshape(w.shape[:-1] + (DIL_GROUPS, 3, DIL_HEADS, DIL_HEAD_DIM))

    def qk_layout(t):
        flat = lambda u: u.reshape(u.shape[:-2] + (-1,))
        return jnp.concatenate([flat(t[..., :DIL_HALF]), flat(t[..., PARTIAL_ROT:PARTIAL_ROT + DIL_REST]),
                                flat(t[..., DIL_HALF:PARTIAL_ROT]), flat(t[..., PARTIAL_ROT + DIL_REST:])], axis=-1)

    dil = jnp.concatenate([jnp.concatenate([qk_layout(dil[..., g, 0, :, :]), qk_layout(dil[..., g, 1, :, :]),
                                            dil[..., g, 2, :, :].reshape(w.shape[:-1] + (DIL_W,))], axis=-1)
                           for g in range(DIL_GROUPS)], axis=-1)
    gates = w[..., 3104:5152]
    zeros = lambda n: jnp.zeros(w.shape[:-1] + (n,), w.dtype)
    lo, hi = k_pe[..., :MLA_HALF], k_pe[..., MLA_HALF:]
    pad = MLA_SLOT - MLA_NOPE - MLA_ROPE
    kslot = jnp.concatenate([zeros(MLA_NOPE), lo, lo, zeros(pad), zeros(MLA_NOPE), hi, hi, zeros(pad)], axis=-1)
    return jnp.concatenate([gates, c_q, c_kv, kslot, dil], axis=-1).astype(BF16)


def _pad_heads(w, width):
    w = jnp.pad(w, ((0, 0), (0, 0), (0, 0), (0, MLA_SLOT - width)))
    return w.reshape(w.shape[0], w.shape[1], MLA_HEADS * MLA_SLOT).astype(BF16)


def _prep_w_q(w):
    w = w.reshape(DEPTH, MLA_Q_LORA, MLA_HEADS // 2, 2, MLA_NOPE + MLA_ROPE)
    nope = w[..., :MLA_NOPE]
    lo = w[..., MLA_NOPE:MLA_NOPE + MLA_HALF]
    hi = w[..., MLA_NOPE + MLA_HALF:]
    both = lambda t: t.reshape(t.shape[:-2] + (2 * MLA_HALF,))
    zeros = jnp.zeros(w.shape[:3] + (MLA_SLOT - MLA_NOPE - MLA_ROPE,), w.dtype)
    slot0 = jnp.concatenate([nope[..., 0, :], both(lo), zeros], axis=-1)
    slot1 = jnp.concatenate([nope[..., 1, :], both(hi), zeros], axis=-1)
    return jnp.stack([slot0, slot1], axis=-2).reshape(DEPTH, MLA_Q_LORA, MLA_HEADS * MLA_SLOT).astype(BF16)


def _prep_w_kv(w):
    w = w.reshape(DEPTH, MLA_KV_LORA, MLA_HEADS, MLA_NOPE + MLA_V)
    return _pad_heads(w[..., :MLA_NOPE], MLA_NOPE), _pad_heads(w[..., MLA_NOPE:], MLA_V)


def _dest_kernel(ri_ref, ps_ref, o_ref):
    ri = ri_ref[...].astype(F32)
    lane = lax.broadcasted_iota(jnp.int32, ri.shape, 1)
    ps = ps_ref[...]

    def col(k):
        return jnp.sum(jnp.where(lane == k, ri, 0.0), axis=-1, keepdims=True)

    def dest(k):
        start = jnp.sum(jnp.where(lane == col(k).astype(jnp.int32), ps, 0.0), axis=-1, keepdims=True)
        return (start + col(2 + k)).astype(jnp.int32)

    o_ref[...] = jnp.where(lane == 0, dest(0), jnp.where(lane == 1, dest(1), 0))


def _dest_rows(ri, pstarts, tm):
    T = ri.shape[0]
    ps = jnp.zeros((1, 128), F32).at[0, :N_EXPERTS].set(pstarts.astype(F32))
    return pl.pallas_call(
        _dest_kernel,
        grid=(T // tm,),
        in_specs=[pl.BlockSpec((tm, 128), lambda i: (i, 0)), pl.BlockSpec((1, 128), lambda i: (0, 0))],
        out_specs=pl.BlockSpec((tm, 128), lambda i: (i, 0)),
        out_shape=jax.ShapeDtypeStruct((T, 128), jnp.int32),
        compiler_params=_cparams(("parallel",)),
        name="dest_rows",
    )(ri, ps)


def _dispatch_plan(ri, cnt, T, tm):
    counts = cnt[0, :N_EXPERTS].astype(jnp.int32)
    pcounts = (counts + FFN_BM - 1) // FFN_BM * FFN_BM
    pends = jnp.cumsum(pcounts)
    pstarts = pends - pcounts
    dest = _dest_rows(ri, pstarts, tm)[:, 0:TOP_K]
    n_blocks = (T * TOP_K) // FFN_BM + N_EXPERTS
    first_row = jnp.arange(n_blocks, dtype=jnp.int32) * FFN_BM
    block_e = jnp.minimum(jnp.sum((pends[None, :] <= first_row[:, None]).astype(jnp.int32), axis=1), N_EXPERTS - 1)
    nvb = pends[-1] // FFN_BM
    after = pends // FFN_BM
    next_of_expert = jnp.where(after < nvb, block_e[jnp.minimum(after, n_blocks - 1)], -1)
    run_of_expert = jnp.cumsum((pcounts > 0).astype(jnp.int32)) - 1
    ffn_plan = (block_e, next_of_expert[block_e].astype(jnp.int32), (run_of_expert[block_e] % 2).astype(jnp.int32),
                nvb.astype(jnp.int32).reshape(1))
    return dest.reshape(T * TOP_K), n_blocks * FFN_BM, ffn_plan


def kernel(x, p, positions, g_mix, w_in, g_q_lat, w_q_up, g_kv_lat, w_kv_up, w_branch_a, w_branch_b, w_out, g_ffn, w_router_grp, b_router_grp, w_router_exp, b_router_exp, w_exp_gate, w_exp_up, w_exp_down, g_ple, w_ple_gate, w_ple_proj, g_final):
    B, S, D = x.shape
    T = B * S
    TM_IN, TM_QKV, TQ, TM_POST, TM_ROUTE, TM_COMB = 512, 512, 512, 512, 512, 256

    pos = positions.reshape(T)
    dil_scale = DIL_HEAD_DIM ** -0.5 * LOG2E
    mla_scale = (MLA_NOPE + MLA_ROPE) ** -0.5 * LOG2E
    tab_dil = jnp.concatenate(
        [_rope_tables(pos, ROPE_THETA, PARTIAL_ROT, 128, 0, sc, paired=True, span=DIL_ROT_LANES)
         for sc in (dil_scale, 1.0)], axis=1)
    tab_mla = jnp.concatenate([_rope_tables(pos, MLA_ROPE_THETA, MLA_ROPE, MLA_SLOT, MLA_NOPE, mla_scale, paired=True),
                               _rope_tables(pos, MLA_ROPE_THETA, MLA_ROPE, MLA_SLOT, MLA_NOPE, 1.0, paired=True)],
                              axis=1)

    w_in_p = _prep_w_in(w_in)
    wq = _prep_w_q(w_q_up)
    wk, wv = _prep_w_kv(w_kv_up)
    wr = jnp.concatenate([w_router_grp, w_router_exp, jnp.zeros((DEPTH, D, 128 - N_GROUPS - N_EXPERTS), F32)],
                         axis=-1).astype(BF16)
    br = jnp.concatenate([b_router_grp, b_router_exp.reshape(DEPTH, N_EXPERTS),
                          jnp.zeros((DEPTH, 128 - N_GROUPS - N_EXPERTS), F32)], axis=-1).reshape(DEPTH, 1, 128)
    wa, wb, wo = w_branch_a.astype(BF16), w_branch_b.astype(BF16), w_out.astype(BF16)
    wpg, wpp = w_ple_gate.astype(BF16), w_ple_proj.astype(BF16)
    gains = lambda g: g.reshape(DEPTH, 1, -1)

    xc = x.reshape(T, D)
    for i in range(DEPTH):
        main, d0, d1, d2 = _in_proj(xc, gains(g_mix), w_in_p, tab_dil, B, S, TM_IN, i)
        q, k, v = _mla_qkv(main, gains(g_q_lat), gains(g_kv_lat), wq, wk, wv, tab_mla, TM_QKV, i)
        o_a = _mla_attn(q, k, v, B, S, TQ)
        ogs, lss = [], []
        for (_, dil), qkv in zip(DIL_PATTERN, (d0, d1, d2)):
            og, ls = _dil_attn(qkv, B, dil, S // dil)
            ogs.append(og)
            lss.append(ls)
        x1, h2, lg = _post(xc, o_a, ogs, lss, main, wa, wb, wo, gains(g_ffn), wr, br, B, S, TM_POST, i)
        ri, rw, cnt = _route(lg, TM_ROUTE)
        dest, n_rows, ffn_plan = _dispatch_plan(ri, cnt, T, TM_ROUTE)
        xs = _sc_gather_rows(h2.reshape(T, ROW_TILE, 128), dest, n_out=n_rows)
        ys = _expert_ffn(*ffn_plan, xs.reshape(-1, 128), w_exp_gate, w_exp_up, w_exp_down, i)
        yg = _sc_gather_rows(ys.reshape(-1, ROW_TILE, 128), dest)
        xc = _combine_ple(yg.reshape(-1, 128), x1, rw, p.reshape(DEPTH * T, PLE_DIM), gains(g_ple), wpg, wpp,
                          g_final.reshape(1, D), TM_COMB, i)
    return xc.reshape(B, S, D)
```

```python
import functools
import math

import jax
import jax.numpy as jnp
import numpy as np
from jax import lax
from jax.experimental import pallas as pl
from jax.experimental.pallas import tpu as pltpu
from jax.experimental.pallas import tpu_sc as plsc

F32 = jnp.float32
BF16 = jnp.bfloat16

D_MODEL = 1024
DEPTH = 4
RMS_EPS = 1e-6
NEG = -1e30
LOG2E = math.log2(math.e)

MLA_HEADS = 16
MLA_Q_LORA = 512
MLA_KV_LORA = 256
MLA_NOPE = 64
MLA_ROPE = 32
MLA_V = 64
MLA_ROPE_THETA = 10000.0
MLA_SLOT = 128
MLA_HALF = MLA_ROPE // 2


def _mla_head_lanes(lane, head):
    l = lane % MLA_SLOT
    nope = jnp.logical_and(lane // MLA_SLOT == head, l < MLA_NOPE)
    rope = jnp.logical_and(l >= MLA_NOPE + head * MLA_HALF, l < MLA_NOPE + (head + 1) * MLA_HALF)
    return jnp.logical_or(nope, rope)

DIL_PATTERN = ((128, 1), (512, 4), (2048, 16))
DIL_GROUPS = 3
DIL_HEADS = 4
DIL_HEAD_DIM = 64
DIL_W = DIL_HEADS * DIL_HEAD_DIM
DIL_BLK = 128
DIL_UNROLL = 16
ROPE_THETA = 500000.0
PARTIAL_ROT = DIL_HEAD_DIM // 4
DIL_HALF = PARTIAL_ROT // 2
DIL_ROT_LANES = DIL_HEADS * DIL_HALF
DIL_REST = (DIL_HEAD_DIM - PARTIAL_ROT) // 2


def _dil_qk_head_of_lane(lane):
    l = lane % 128
    plain = l - DIL_ROT_LANES
    plain_head = sum((plain >= h * DIL_REST).astype(jnp.int32) for h in range(1, DIL_HEADS))
    return jnp.where(l < DIL_ROT_LANES, l // DIL_HALF, plain_head)

N_GROUPS = 8
EXPERTS_PER_GROUP = 8
N_EXPERTS = 64
TOP_K = 2
EXPERT_FF = 256
PLE_DIM = 256

IN_TN = 768
MAIN_COLS = 3072
IN_COLS_PAD = MAIN_COLS + DIL_GROUPS * 3 * DIL_W
N_MAIN_TILES = MAIN_COLS // IN_TN

FFN_BM = 256
ROW_TILE = D_MODEL // 256
U32 = jnp.uint32
VMEM_LIMIT = 48 * 1024 * 1024

SC_CORES = 2
SC_SUBCORES = 16
SC_LANES = 16
SC_CHUNK = 64


def _cparams(sem, bounds_checks=True):
    return pltpu.CompilerParams(dimension_semantics=sem, vmem_limit_bytes=VMEM_LIMIT,
                                disable_bounds_checks=not bounds_checks)


def _layer_spec(shape, layer, col=None):
    def index(*grid_idx):
        return (layer,) + (0,) * (len(shape) - 1) + ((col(*grid_idx),) if col else (0,))

    return pl.BlockSpec((None,) + tuple(shape), index)


def _rms(x, g):
    return x * lax.rsqrt(jnp.mean(x * x, axis=-1, keepdims=True) + RMS_EPS) * g


def _rope128(x, c, s_up, s_dn, half):
    return x * c + pltpu.roll(x, half, 1) * s_up + pltpu.roll(x, 128 - half, 1) * s_dn


def _in_kernel(x_ref, g_ref, w_ref, tab_ref, main_ref, d0_ref, d1_ref, d2_ref, xn_ref, acc_ref):
    j = pl.program_id(1)

    @pl.when(j == 0)
    def _():
        xn_ref[...] = _rms(x_ref[...], g_ref[...]).astype(BF16)

    acc = jnp.dot(xn_ref[...], w_ref[...], preferred_element_type=F32)

    @pl.when(j < N_MAIN_TILES)
    def _():
        main_ref[...] = acc.astype(BF16)

    def dil_tile(out_ref, dil):
        chunk = lambda c: acc[:, c * 128:(c + 1) * 128]
        for base, t0 in ((0, 0), (2, 3)):
            cos, sin_n, sin_p = [tab_ref[:, (t0 + t) * 128:(t0 + t + 1) * 128] for t in range(3)]
            lo, hi = chunk(base), chunk(base + 1)
            acc_ref[base] = lo * cos + hi * sin_n
            acc_ref[base + 1] = hi * cos + lo * sin_p
        for c in range(4, 6):
            acc_ref[c] = chunk(c)
        rows = acc_ref.shape[1] // dil
        for r in range(dil):
            for c in range(6):
                out_ref[0, r, :, c * 128:(c + 1) * 128] = acc_ref[c, pl.ds(r, rows, stride=dil), :].astype(BF16)

    for gi, (_, dil) in enumerate(DIL_PATTERN):
        pl.when(j == N_MAIN_TILES + gi)(functools.partial(dil_tile, (d0_ref, d1_ref, d2_ref)[gi], dil))


def _in_proj(x2d, g, w, tab, B, S, tm, layer):
    T = x2d.shape[0]
    nt = S // tm
    dil_shapes = [jax.ShapeDtypeStruct((B, d, S // d, 3 * DIL_W), BF16) for _, d in DIL_PATTERN]
    dil_specs = [pl.BlockSpec((1, d, tm // d, 3 * DIL_W), lambda i, j, nt=nt: (i // nt, 0, i % nt, 0))
                 for _, d in DIL_PATTERN]
    return pl.pallas_call(
        _in_kernel,
        grid=(T // tm, IN_COLS_PAD // IN_TN),
        in_specs=[
            pl.BlockSpec((tm, D_MODEL), lambda i, j: (i, 0)),
            _layer_spec((1, D_MODEL), layer),
            _layer_spec((D_MODEL, IN_TN), layer, col=lambda i, j: j),
            pl.BlockSpec((tm, 6 * 128), lambda i, j: (i, 0)),
        ],
        out_specs=[pl.BlockSpec((tm, IN_TN), lambda i, j: (i, jnp.minimum(j, N_MAIN_TILES - 1)))] + dil_specs,
        out_shape=[jax.ShapeDtypeStruct((T, MAIN_COLS), BF16)] + dil_shapes,
        scratch_shapes=[pltpu.VMEM((tm, D_MODEL), BF16), pltpu.VMEM((IN_TN // 128, tm, 128), F32)],
        compiler_params=_cparams(("parallel", "arbitrary")),
        name="in_proj",
    )(x2d, g, w, tab)


def _qkv_kernel(cq_ref, ckv_ref, kpe_ref, gq_ref, gkv_ref, wq_ref, wk_ref, wv_ref, tab_ref,
                q_ref, k_ref, v_ref):
    qn = _rms(cq_ref[...].astype(F32), gq_ref[...]).astype(BF16)
    kvn = _rms(ckv_ref[...].astype(F32), gkv_ref[...]).astype(BF16)
    qacc = jnp.dot(qn, wq_ref[...], preferred_element_type=F32)
    kacc = jnp.dot(kvn, wk_ref[...], preferred_element_type=F32)
    vacc = jnp.dot(kvn, wv_ref[...], preferred_element_type=F32)
    lane_v = lax.broadcasted_iota(jnp.int32, vacc.shape, 1)
    v_ref[...] = jnp.where(lane_v % MLA_SLOT < MLA_V, vacc, 1.0).astype(BF16)
    cq, snq, spq, ck, snk, spk = [tab_ref[:, t * 128:(t + 1) * 128] for t in range(6)]
    kpe_lo = kpe_ref[:, 0:128].astype(F32)
    kpe_hi = kpe_ref[:, 128:256].astype(F32)
    krot = (kpe_lo * ck + kpe_hi * snk, kpe_hi * ck + kpe_lo * spk)
    for pair in range(MLA_HEADS // 2):
        s0 = slice(2 * pair * MLA_SLOT, (2 * pair + 1) * MLA_SLOT)
        s1 = slice((2 * pair + 1) * MLA_SLOT, (2 * pair + 2) * MLA_SLOT)
        q0, q1 = qacc[:, s0], qacc[:, s1]
        q_ref[:, s0] = (q0 * cq + q1 * snq).astype(BF16)
        q_ref[:, s1] = (q1 * cq + q0 * spq).astype(BF16)
        k_ref[:, s0] = (kacc[:, s0] + krot[0]).astype(BF16)
        k_ref[:, s1] = (kacc[:, s1] + krot[1]).astype(BF16)


def _mla_qkv(main, gq, gkv, wq, wk, wv, tab, tm, layer):
    T = main.shape[0]
    return pl.pallas_call(
        _qkv_kernel,
        grid=(T // tm,),
        in_specs=[
            pl.BlockSpec((tm, MLA_Q_LORA), lambda i: (i, 2048 // MLA_Q_LORA)),
            pl.BlockSpec((tm, MLA_KV_LORA), lambda i: (i, 2560 // MLA_KV_LORA)),
            pl.BlockSpec((tm, 256), lambda i: (i, 2816 // 256)),
            _layer_spec((1, MLA_Q_LORA), layer),
            _layer_spec((1, MLA_KV_LORA), layer),
            _layer_spec((MLA_Q_LORA, MLA_HEADS * MLA_SLOT), layer),
            _layer_spec((MLA_KV_LORA, MLA_HEADS * MLA_SLOT), layer),
            _layer_spec((MLA_KV_LORA, MLA_HEADS * MLA_SLOT), layer),
            pl.BlockSpec((tm, 6 * 128), lambda i: (i, 0)),
        ],
        out_specs=[
            pl.BlockSpec((tm, MLA_HEADS * MLA_SLOT), lambda i: (i, 0)),
            pl.BlockSpec((tm, MLA_HEADS * MLA_SLOT), lambda i: (i, 0)),
            pl.BlockSpec((tm, MLA_HEADS * MLA_SLOT), lambda i: (i, 0)),
        ],
        out_shape=[
            jax.ShapeDtypeStruct((T, MLA_HEADS * MLA_SLOT), BF16),
            jax.ShapeDtypeStruct((T, MLA_HEADS * MLA_SLOT), BF16),
            jax.ShapeDtypeStruct((T, MLA_HEADS * MLA_SLOT), BF16),
        ],
        compiler_params=_cparams(("parallel",)),
        name="mla_qkv",
    )(main, main, main, gq, gkv, wq, wk, wv, tab)


def _mla_attn_kernel(qa_ref, qb_ref, k_ref, v_ref, o_ref, m_ref, acc_ref, *, tq):
    p_id = pl.program_id(2)
    nq = k_ref.shape[1] // tq
    lane = lax.broadcasted_iota(jnp.int32, (tq, 2 * MLA_SLOT), 1)
    qh = []
    for q_ref in (qa_ref, qb_ref):
        qp = q_ref[0]
        zero = jnp.zeros_like(qp)
        qh.append([jnp.where(_mla_head_lanes(lane, h), qp, zero) for h in range(2)])
    hq = tq // 2
    dn = (((1,), (1,)), ((), ()))

    def step(blk, start, width, r0, mask, first=False):
        kb = k_ref[0, pl.ds(start, width), :]
        vb = v_ref[0, pl.ds(start, width), :]
        rows = tq - r0
        s_pair = lax.dot_general(jnp.concatenate([qh[blk][0][r0:], qh[blk][1][r0:]], axis=0), kb, dn,
                                 preferred_element_type=F32)
        for h in range(2):
            s = s_pair[h * rows:(h + 1) * rows]
            if mask is not None:
                s = jnp.where(mask, s, NEG)
            m_cur = jnp.max(s, axis=-1, keepdims=True)
            if first:
                m_new = jnp.broadcast_to(m_cur, (tq - r0, 128))
            else:
                m_prev = m_ref[blk, h, r0:, :]
                m_new = jnp.maximum(m_prev, m_cur)
            p = jnp.exp2(s - jnp.concatenate([m_new] * (width // 128), axis=-1))
            pv = jnp.dot(p.astype(BF16), vb[:, h * MLA_SLOT:(h + 1) * MLA_SLOT], preferred_element_type=F32)
            if first:
                acc_ref[blk, h, r0:, :] = pv
            else:
                acc_ref[blk, h, r0:, :] = jnp.exp2(m_prev - m_new) * acc_ref[blk, h, r0:, :] + pv
            m_ref[blk, h, r0:, :] = m_new

    def causal(rows):
        return lax.broadcasted_iota(jnp.int32, (rows, hq), 1) <= lax.broadcasted_iota(jnp.int32, (rows, hq), 0)

    def q_blocks(n_a):
        n_b = nq - 1 - n_a
        for blk, n_full in ((1, n_b), (0, n_a)):
            for j in range(n_full):
                step(blk, j * tq, tq, 0, None, first=j == 0)
            step(blk, n_full * tq, hq, 0, causal(tq), first=n_full == 0)
            step(blk, n_full * tq + hq, hq, hq, causal(hq))

    for n_a in range(nq // 2):
        pl.when(p_id == n_a)(functools.partial(q_blocks, n_a))
    lane_o = lax.broadcasted_iota(jnp.int32, (tq, 2 * MLA_V), 1)
    for blk in range(2):
        outs = []
        for h in range(2):
            a = acc_ref[blk, h]
            outs.append(a / pltpu.roll(a, MLA_V, 1))
        o_ref[0, blk, 0] = jnp.where(lane_o < MLA_V, outs[0], pltpu.roll(outs[1], MLA_V, 1)).astype(BF16)


def _mla_attn(q, k, v, B, S, tq):
    nq = S // tq
    q = q.reshape(B, S, MLA_HEADS * MLA_SLOT)
    k = k.reshape(B, S, MLA_HEADS * MLA_SLOT)
    v = v.reshape(B, S, MLA_HEADS * MLA_SLOT)
    return pl.pallas_call(
        functools.partial(_mla_attn_kernel, tq=tq),
        grid=(B, MLA_HEADS // 2, nq // 2),
        in_specs=[
            pl.BlockSpec((1, tq, 2 * MLA_SLOT), lambda b, h, p: (b, p, h)),
            pl.BlockSpec((1, tq, 2 * MLA_SLOT), lambda b, h, p: (b, nq - 1 - p, h)),
            pl.BlockSpec((1, S, 2 * MLA_SLOT), lambda b, h, p: (b, 0, h)),
            pl.BlockSpec((1, S, 2 * MLA_SLOT), lambda b, h, p: (b, 0, h)),
        ],
        out_specs=pl.BlockSpec((1, 2, 1, tq, 2 * MLA_V), lambda b, h, p: (b, 0, p, 0, h)),
        out_shape=jax.ShapeDtypeStruct((B, 2, nq // 2, tq, MLA_HEADS * MLA_V), BF16),
        scratch_shapes=[pltpu.VMEM((2, 2, tq, 128), F32), pltpu.VMEM((2, 2, tq, MLA_SLOT), F32)],
        compiler_params=_cparams(("parallel", "parallel", "arbitrary")),
        name="mla_attn",
    )(q, q, k, v)


def _dil_kernel(qkv_ref, o_ref, lse_ref, bias_ref, *, nb):
    dil = qkv_ref.shape[1]
    width = 2 * DIL_BLK if nb > 1 else DIL_BLK
    row = lax.broadcasted_iota(jnp.int32, (DIL_BLK, width), 0)
    col = lax.broadcasted_iota(jnp.int32, (DIL_BLK, width), 1)
    bias_ref[0] = jnp.where(col <= row, 0.0, NEG)
    if nb > 1:
        later = jnp.logical_or(jnp.logical_and(col >= DIL_BLK, col - DIL_BLK <= row),
                               jnp.logical_and(col < DIL_BLK, col >= row))
        bias_ref[1] = jnp.where(later, 0.0, NEG)
    lane = lax.broadcasted_iota(jnp.int32, (DIL_BLK, DIL_W), 1)
    head_of_lane = lane // DIL_HEAD_DIM
    qk_head_of_lane = _dil_qk_head_of_lane(lane)
    dn = (((1,), (1,)), ((), ()))

    def by_head(parts):
        out = parts[DIL_HEADS - 1]
        for h in range(DIL_HEADS - 2, -1, -1):
            out = jnp.where(head_of_lane == h, parts[h], out)
        return out

    def unit(u):
        r = u // nb
        n = u % nb
        q0 = pl.multiple_of(n * DIL_BLK, DIL_BLK)
        q = qkv_ref[0, r, pl.ds(q0, DIL_BLK), 0:DIL_W]
        if nb > 1:
            w0 = pl.multiple_of(jnp.maximum(n - 1, 0) * DIL_BLK, DIL_BLK)
            bias = bias_ref[jnp.minimum(n, 1)]
        else:
            w0 = 0
            bias = bias_ref[0]
        kw = qkv_ref[0, r, pl.ds(w0, width), DIL_W:2 * DIL_W]
        vw = qkv_ref[0, r, pl.ds(w0, width), 2 * DIL_W:3 * DIL_W]
        zero = jnp.zeros_like(q)
        qs = jnp.concatenate([jnp.where(qk_head_of_lane == h, q, zero) for h in range(DIL_HEADS)], axis=0)
        s = lax.dot_general(qs, kw, dn, preferred_element_type=F32)
        s = (s.reshape(DIL_HEADS, DIL_BLK, width) + bias[None]).reshape(DIL_HEADS * DIL_BLK, width)
        m = jnp.max(s, axis=-1, keepdims=True)
        e = jnp.exp2(s - m)
        den = jnp.sum(e, axis=-1, keepdims=True)
        pv = jnp.dot(e.astype(BF16), vw, preferred_element_type=F32)
        lse = m + jnp.log2(den)
        blk = lambda t, h: t[h * DIL_BLK:(h + 1) * DIL_BLK]
        o = by_head([blk(pv, h) for h in range(DIL_HEADS)]) / by_head(
            [jnp.broadcast_to(blk(den, h), (DIL_BLK, DIL_W)) for h in range(DIL_HEADS)])
        o_ref[0, r, pl.ds(q0, DIL_BLK), :] = o.astype(BF16)
        lse_ref[0, r, pl.ds(q0, DIL_BLK), :] = by_head(
            [jnp.broadcast_to(blk(lse, h), (DIL_BLK, DIL_W)) for h in range(DIL_HEADS)])

    def body(t, carry):
        for u in range(DIL_UNROLL):
            unit(DIL_UNROLL * t + u)
        return carry

    lax.fori_loop(0, dil * nb // DIL_UNROLL, body, 0)


def _dil_attn(qkv, B, dil, L):
    nb = L // DIL_BLK
    return pl.pallas_call(
        functools.partial(_dil_kernel, nb=nb),
        grid=(B,),
        in_specs=[pl.BlockSpec((1, dil, L, 3 * DIL_W), lambda b: (b, 0, 0, 0))],
        out_specs=[
            pl.BlockSpec((1, dil, L, DIL_W), lambda b: (b, 0, 0, 0)),
            pl.BlockSpec((1, dil, L, DIL_W), lambda b: (b, 0, 0, 0)),
        ],
        out_shape=[
            jax.ShapeDtypeStruct((B, dil, L, DIL_W), BF16),
            jax.ShapeDtypeStruct((B, dil, L, DIL_W), F32),
        ],
        scratch_shapes=[pltpu.VMEM((2, DIL_BLK, 2 * DIL_BLK if nb > 1 else DIL_BLK), F32)],
        compiler_params=_cparams(("parallel",)),
        name=f"dil_attn_d{dil}",
    )(qkv)


def _post_kernel(x_ref, oa_ref, og0_ref, og1_ref, og2_ref, ls0_ref, ls1_ref, ls2_ref, gates_ref,
                 wa_ref, wb_ref, wo_ref, gffn_ref, wr_ref, br_ref,
                 x1_ref, h2_ref, lg_ref, ob_ref, *, tm):
    ls =[r[0] for r in (ls0_ref, ls1_ref, ls2_ref)]
    og = [r[0] for r in (og0_ref, og1_ref, og2_ref)]
    for gi, (_, dil) in enumerate(DIL_PATTERN):
        rows = tm // dil
        for r in range(dil):
            for c in range(2):
                sl = slice(c * 128, (c + 1) * 128)
                ob_ref[2 * gi + c, pl.ds(r, rows, stride=dil), :] = og[gi][r][:, sl].astype(F32)
                ob_ref[6 + 2 * gi + c, pl.ds(r, rows, stride=dil), :] = ls[gi][r][:, sl]

    def tok_major(k):
        return jnp.concatenate([ob_ref[2 * k], ob_ref[2 * k + 1]], axis=-1)

    l0, l1, l2 = tok_major(3), tok_major(4), tok_major(5)
    mx = jnp.maximum(jnp.maximum(l0, l1), l2)
    w0, w1, w2 = jnp.exp2(l0 - mx), jnp.exp2(l1 - mx), jnp.exp2(l2 - mx)
    ob = (w0 * tok_major(0) + w1 * tok_major(1) + w2 * tok_major(2)) / (w0 + w1 + w2)
    ya = jnp.dot(oa_ref[0, 0, 0], wa_ref[...], preferred_element_type=F32)
    yb = jnp.dot(ob.astype(BF16), wb_ref[...], preferred_element_type=F32)
    merged = (jax.nn.sigmoid(gates_ref[:, 0:D_MODEL].astype(F32)) * ya
              + jax.nn.sigmoid(gates_ref[:, D_MODEL:2 * D_MODEL].astype(F32)) * yb)
    x1 = x_ref[...] + jnp.dot(merged.astype(BF16), wo_ref[...], preferred_element_type=F32)
    x1_ref[...] = x1
    h2 = _rms(x1, gffn_ref[...])
    _rows_to_tiles(h2_ref, h2)
    lg_ref[...] = jnp.dot(h2.astype(BF16), wr_ref[...], preferred_element_type=F32) + br_ref[...]


def _post(x2d, oa, ogs, lss, main, wa, wb, wo, gffn, wr, br, B, S, tm, layer):
    T = x2d.shape[0]
    nt = S // tm
    res_specs = [pl.BlockSpec((1, d, tm // d, DIL_W), lambda i, nt=nt: (i // nt, 0, i % nt, 0))
                 for _, d in DIL_PATTERN]
    assert oa.shape[3] == tm and oa.shape[2] * 2 == nt

    def oa_index(i):
        it = i % nt
        late = it >= nt // 2
        return (i // nt, late.astype(jnp.int32), jnp.where(late, nt - 1 - it, it), 0, 0)

    return pl.pallas_call(
        functools.partial(_post_kernel, tm=tm),
        grid=(T // tm,),
        in_specs=[pl.BlockSpec((tm, D_MODEL), lambda i: (i, 0)),
                  pl.BlockSpec((1, 1, 1, tm, D_MODEL), oa_index)]
                 + res_specs + res_specs
                 + [pl.BlockSpec((tm, 2 * D_MODEL), lambda i: (i, 0)),
                    _layer_spec((D_MODEL, D_MODEL), layer),
                    _layer_spec((DIL_W, D_MODEL), layer),
                    _layer_spec((D_MODEL, D_MODEL), layer),
                    _layer_spec((1, D_MODEL), layer),
                    _layer_spec((D_MODEL, 128), layer),
                    _layer_spec((1, 128), layer)],
        out_specs=[pl.BlockSpec((tm, D_MODEL), lambda i: (i, 0)),
                   pl.BlockSpec((tm * ROW_TILE, 128), lambda i: (i, 0)),
                   pl.BlockSpec((tm, 128), lambda i: (i, 0))],
        out_shape=[jax.ShapeDtypeStruct((T, D_MODEL), F32),
                   jax.ShapeDtypeStruct((T * ROW_TILE, 128), U32),
                   jax.ShapeDtypeStruct((T, 128), F32)],
        scratch_shapes=[pltpu.VMEM((12, tm, 128), F32)],
        compiler_params=_cparams(("parallel",)),
        name="post_attn",
    )(x2d, oa, *ogs, *lss, main, wa, wb, wo, gffn, wr, br)


def _route_kernel(lg_ref, ri_ref, rw_ref, cnt_ref, carry_ref, lower_ref, *, tm):
    i = pl.program_id(0)

    @pl.when(i == 0)
    def _():
        carry_ref[...] = jnp.zeros(carry_ref.shape, F32)
        r_i = lax.broadcasted_iota(jnp.int32, (tm, tm), 0)
        c_i = lax.broadcasted_iota(jnp.int32, (tm, tm), 1)
        lower_ref[...] = jnp.where(c_i < r_i, 1.0, 0.0).astype(BF16)

    lg = lg_ref[...]
    lane = lax.broadcasted_iota(jnp.int32, lg.shape, 1)
    lane_f = lane.astype(F32)
    ninf = jnp.float32(-jnp.inf)

    def first_max(vals):
        vmax = jnp.max(vals, axis=-1, keepdims=True)
        idx = jnp.min(jnp.where(vals == vmax, lane_f, 128.0), axis=-1, keepdims=True)
        return vmax, idx.astype(jnp.int32)

    gl = jnp.where(lane < N_GROUPS, lg, ninf)
    gmax, g_sel = first_max(gl)
    p_g = 1.0 / jnp.sum(jnp.exp(gl - gmax), axis=-1, keepdims=True)
    lo = N_GROUPS + g_sel * EXPERTS_PER_GROUP
    el = jnp.where(jnp.logical_and(lane >= lo, lane < lo + EXPERTS_PER_GROUP), lg, ninf)
    v0, i0 = first_max(el)
    v1, i1 = first_max(jnp.where(lane == i0, ninf, el))
    t = jnp.exp(v1 - v0)
    w0 = p_g / (1.0 + t)
    w1 = p_g * t / (1.0 + t)
    e0 = i0 - N_GROUPS
    e1 = i1 - N_GROUPS
    hit0 = lane == e0
    hit1 = lane == e1
    oh = jnp.where(jnp.logical_or(hit0, hit1), 1.0, 0.0).astype(F32)
    excl = jnp.dot(lower_ref[...], oh.astype(BF16), preferred_element_type=F32) + carry_ref[...]
    r0 = jnp.sum(jnp.where(hit0, excl, 0.0), axis=-1, keepdims=True).astype(jnp.int32)
    r1 = jnp.sum(jnp.where(hit1, excl, 0.0), axis=-1, keepdims=True).astype(jnp.int32)
    carry_ref[...] = carry_ref[...] + jnp.sum(oh, axis=0, keepdims=True)
    zi = jnp.zeros(lg.shape, jnp.int32)
    ri_ref[...] = jnp.where(lane == 0, e0, jnp.where(lane == 1, e1, jnp.where(lane == 2, r0, jnp.where(lane == 3, r1, zi))))
    rw_ref[...] = jnp.where(lane == 0, w0, jnp.where(lane == 1, w1, jnp.zeros(lg.shape, F32)))
    cnt_ref[...] = carry_ref[...]


def _route(lg, tm):
    T = lg.shape[0]
    return pl.pallas_call(
        functools.partial(_route_kernel, tm=tm),
        grid=(T // tm,),
        in_specs=[pl.BlockSpec((tm, 128), lambda i: (i, 0))],
        out_specs=[pl.BlockSpec((tm, 128), lambda i: (i, 0)),
                   pl.BlockSpec((tm, 128), lambda i: (i, 0)),
                   pl.BlockSpec((1, 128), lambda i: (0, 0))],
        out_shape=[jax.ShapeDtypeStruct((T, 128), jnp.int32),
                   jax.ShapeDtypeStruct((T, 128), F32),
                   jax.ShapeDtypeStruct((1, 128), F32)],
        scratch_shapes=[pltpu.VMEM((1, 128), F32), pltpu.VMEM((tm, tm), BF16)],
        compiler_params=_cparams(("arbitrary",)),
        name="route",
    )(lg)


def _sc_gather_rows(table, idx, n_out=None):
    inverse = n_out is not None
    n_src = idx.shape[0]
    n = n_out if inverse else n_src
    n_workers = SC_CORES * SC_SUBCORES
    per_w = n // n_workers
    n_chunks = per_w // SC_CHUNK
    assert per_w * n_workers == n and n_chunks * SC_CHUNK == per_w and n_chunks % 2 == 0
    mesh = plsc.VectorSubcoreMesh(core_axis_name="c", subcore_axis_name="s",
                                  num_cores=SC_CORES, num_subcores=SC_SUBCORES)

    @functools.partial(
        pl.kernel, mesh=mesh,
        out_type=jax.ShapeDtypeStruct((n,) + table.shape[1:], table.dtype),
        scratch_types=[pltpu.VMEM((per_w,), jnp.int32),
                       pltpu.VMEM((SC_CHUNK,) + table.shape[1:], table.dtype),
                       pltpu.VMEM((SC_CHUNK,) + table.shape[1:], table.dtype),
                       pltpu.SemaphoreType.DMA, pltpu.SemaphoreType.DMA,
                       pltpu.VMEM((n_src if inverse else SC_LANES,), jnp.int32)],
        compiler_params=pltpu.CompilerParams(use_tc_tiling_on_sc=True, needs_layout_passes=not inverse),
        name="sc_dispatch_rows" if inverse else "sc_gather_rows",
    )
    def gather(table_hbm, idx_hbm, out_hbm, idx_v, rows_a, rows_b, sem_a, sem_b, map_v):
        wid = lax.axis_index("s") * SC_CORES + lax.axis_index("c")
        base = wid * per_w
        if inverse:
            pltpu.sync_copy(idx_hbm, map_v)
            lanes = lax.iota(jnp.int32, SC_LANES)

            @pl.loop(0, per_w // SC_LANES)
            def _(j):
                idx_v[pl.ds(j * SC_LANES, SC_LANES)] = lax.rem(base + j * SC_LANES + lanes, table.shape[0])

            @pl.loop(0, n_src // SC_LANES)
            def _(a):
                local = map_v[pl.ds(a * SC_LANES, SC_LANES)] - base
                mine = jnp.logical_and(local >= 0, local < per_w)
                plsc.store_scatter(idx_v, [local], lax.div(a * SC_LANES + lanes, TOP_K), mask=mine)
        else:
            pltpu.sync_copy(idx_hbm.at[pl.ds(base, per_w)], idx_v)

        def fetch(chunk, rows_v, sem):
            return pltpu.make_async_copy(table_hbm.at[idx_v.at[pl.ds(chunk * SC_CHUNK, SC_CHUNK)]], rows_v, sem)

        def flush(chunk, rows_v):
            pltpu.sync_copy(rows_v, out_hbm.at[pl.ds(base + chunk * SC_CHUNK, SC_CHUNK)])

        fetch(0, rows_a, sem_a).start()

        @pl.loop(0, n_chunks, step=2)
        def _(c):
            fetch(c + 1, rows_b, sem_b).start()
            fetch(c, rows_a, sem_a).wait()
            flush(c, rows_a)

            @pl.when(c + 2 < n_chunks)
            def _():
                fetch(c + 2, rows_a, sem_a).start()

            fetch(c + 1, rows_b, sem_b).wait()
            flush(c + 1, rows_b)

    return gather(table, idx)


def _rows_from_tiles(ref, first, n_rows, stride=ROW_TILE):
    words = [ref[pl.ds(first + c, n_rows, stride=stride), :] for c in range(ROW_TILE)]
    lo = [pltpu.bitcast(w << 16, F32) for w in words]
    hi = [pltpu.bitcast(w & U32(0xFFFF0000), F32) for w in words]
    return jnp.concatenate(lo + hi, axis=-1)


def _rows_to_tiles(ref, val):
    half = D_MODEL // 2

    def bits(x):
        return pltpu.bitcast(x.astype(BF16).astype(F32), U32)

    for c in range(ROW_TILE):
        lo = bits(val[:, c * 128:(c + 1) * 128])
        hi = bits(val[:, half + c * 128:half + (c + 1) * 128])
        ref[pl.ds(c, val.shape[0], stride=ROW_TILE), :] = hi | (lo >> 16)


def _ffn_kernel(be_ref, nx_ref, par_ref, nvb_ref, x_ref, w1_hbm, w3_hbm, w2_hbm, y_ref,
                w1f_ref, w3f_ref, w2f_ref, w1b_ref, w3b_ref, w2b_ref, sem, *, layer):
    b = pl.program_id(0)
    nvb = nvb_ref[0]
    expert = be_ref[b]
    slot = par_ref[b]
    new_expert = jnp.logical_or(b == 0, expert != be_ref[jnp.maximum(b - 1, 0)])

    def fetch(e, sl):
        return [pltpu.make_async_copy(w_hbm.at[layer, e], wf_ref.at[sl], sem.at[sl])
                for w_hbm, wf_ref in ((w1_hbm, w1f_ref), (w3_hbm, w3f_ref), (w2_hbm, w2f_ref))]

    @pl.when(jnp.logical_and(b == 0, nvb > 0))
    def _():
        for cp in fetch(expert, slot):
            cp.start()

    @pl.when(jnp.logical_and(b < nvb, new_expert))
    def _():
        for cp in fetch(expert, slot):
            cp.wait()

        @pl.when(nx_ref[b] >= 0)
        def _():
            for cp in fetch(nx_ref[b], 1 - slot):
                cp.start()

        w1b_ref[...] = w1f_ref[slot].astype(BF16)
        w3b_ref[...] = w3f_ref[slot].astype(BF16)
        w2b_ref[...] = w2f_ref[slot].astype(BF16)

    @pl.when(b < nvb)
    def _():
        xb = _rows_from_tiles(x_ref, 0, FFN_BM).astype(BF16)
        h1 = jnp.dot(xb, w1b_ref[...], preferred_element_type=F32)
        h3 = jnp.dot(xb, w3b_ref[...], preferred_element_type=F32)
        a = (jax.nn.silu(h1) * h3).astype(BF16)
        _rows_to_tiles(y_ref, jnp.dot(a, w2b_ref[...], preferred_element_type=F32))

    @pl.when(b >= nvb)
    def _():
        y_ref[...] = jnp.zeros(y_ref.shape, U32)


def _expert_ffn(block_e, next_e, parity, nvb, xs, w1, w3, w2, layer):
    nb = block_e.shape[0]

    def used(b, *prefetch):
        return jnp.minimum(b, jnp.maximum(prefetch[-1][0] - 1, 0))

    return pl.pallas_call(
        functools.partial(_ffn_kernel, layer=layer),
        grid_spec=pltpu.PrefetchScalarGridSpec(
            num_scalar_prefetch=4,
            grid=(nb,),
            in_specs=[
                pl.BlockSpec((FFN_BM * ROW_TILE, 128), lambda b, *prefetch: (used(b, *prefetch), 0)),
                pl.BlockSpec(memory_space=pl.ANY),
                pl.BlockSpec(memory_space=pl.ANY),
                pl.BlockSpec(memory_space=pl.ANY),
            ],
            out_specs=pl.BlockSpec((FFN_BM * ROW_TILE, 128), lambda b, *prefetch: (b, 0)),
            scratch_shapes=[pltpu.VMEM((2, D_MODEL, EXPERT_FF), F32), pltpu.VMEM((2, D_MODEL, EXPERT_FF), F32),
                            pltpu.VMEM((2, EXPERT_FF, D_MODEL), F32),
                            pltpu.VMEM((D_MODEL, EXPERT_FF), BF16), pltpu.VMEM((D_MODEL, EXPERT_FF), BF16),
                            pltpu.VMEM((EXPERT_FF, D_MODEL), BF16),
                            pltpu.SemaphoreType.DMA((2,))],
        ),
        out_shape=jax.ShapeDtypeStruct((nb * FFN_BM * ROW_TILE, 128), U32),
        compiler_params=_cparams(("arbitrary",)),
        name="expert_ffn",
    )(block_e, next_e, parity, nvb, xs, w1, w3, w2)


def _comb_kernel(yg_ref, x1_ref, rw_ref, p_ref, gple_ref, wpg_ref, wpp_ref, gout_ref, o_ref, *, tm, final):
    w = rw_ref[...]
    y0 = _rows_from_tiles(yg_ref, 0, tm, stride=TOP_K * ROW_TILE)
    y1 = _rows_from_tiles(yg_ref, ROW_TILE, tm, stride=TOP_K * ROW_TILE)
    x2 = x1_ref[...] + (y0 * w[:, 0:1] + y1 * w[:, 1:2])
    e = jnp.dot(p_ref[...].astype(BF16), wpp_ref[...], preferred_element_type=F32)
    gate = jax.nn.sigmoid(jnp.dot(_rms(x2, gple_ref[...]).astype(BF16), wpg_ref[...], preferred_element_type=F32))
    x3 = x2 + gate * e
    o_ref[...] = _rms(x3, gout_ref[...]) if final else x3


def _combine_ple(yg, x1, rw, p2d, gple, wpg, wpp, gout, tm, layer):
    T = x1.shape[0]
    nt = T // tm
    return pl.pallas_call(
        functools.partial(_comb_kernel, tm=tm, final=layer == DEPTH - 1),
        grid=(nt,),
        in_specs=[
            pl.BlockSpec((tm * TOP_K * ROW_TILE, 128), lambda i: (i, 0)),
            pl.BlockSpec((tm, D_MODEL), lambda i: (i, 0)),
            pl.BlockSpec((tm, 128), lambda i: (i, 0)),
            pl.BlockSpec((tm, PLE_DIM), lambda i: (layer * nt + i, 0)),
            _layer_spec((1, D_MODEL), layer),
            _layer_spec((D_MODEL, D_MODEL), layer),
            _layer_spec((PLE_DIM, D_MODEL), layer),
            pl.BlockSpec((1, D_MODEL), lambda i: (0, 0)),
        ],
        out_specs=pl.BlockSpec((tm, D_MODEL), lambda i: (i, 0)),
        out_shape=jax.ShapeDtypeStruct((T, D_MODEL), F32),
        compiler_params=_cparams(("parallel",)),
        name="combine_ple",
    )(yg, x1, rw, p2d, gple, wpg, wpp, gout)


def _rope_tables(pos, theta, rot_dim, period, offset, scale, paired=False, span=None):
    half = rot_dim // 2
    span = rot_dim if span is None else span
    inv = jnp.float32(theta) ** (-jnp.arange(half, dtype=F32) * 2.0 / rot_dim)
    rel = np.arange(128) % period - offset
    rot = (rel >= 0) & (rel < span)
    upper = rot & (rel >= half)
    lower = rot & (rel < half)
    ang = pos.astype(F32)[:, None] * inv[np.where(rot, rel % half, 0)][None, :]
    cos, sin = jnp.cos(ang), jnp.sin(ang)
    if paired:
        tabs = (jnp.where(rot, cos, 1.0), jnp.where(rot, -sin, 0.0), jnp.where(rot, sin, 0.0))
    else:
        tabs = (jnp.where(rot, cos, 1.0), jnp.where(upper, sin, 0.0), jnp.where(lower, -sin, 0.0))
    return jnp.concatenate(tabs, axis=1) * jnp.float32(scale)


def _prep_w_in(w):
    pad = MLA_SLOT - MLA_NOPE - MLA_ROPE
    lo, hi = (768, 768 + MLA_HALF), (768 + MLA_HALF, 800)
    runs = [(3104, 5152), (0, 512), (512, 768)]
    runs += [MLA_NOPE, lo, lo, pad, MLA_NOPE, hi, hi, pad]
    for g in range(DIL_GROUPS):
        for part in range(2):
            heads = [800 + (g * 3 + part) * DIL_W + h * DIL_HEAD_DIM for h in range(DIL_HEADS)]
            for a, b in ((0, DIL_HALF), (PARTIAL_ROT, PARTIAL_ROT + DIL_REST),
                         (DIL_HALF, PARTIAL_ROT), (PARTIAL_ROT + DIL_REST, DIL_HEAD_DIM)):
                runs += [(h0 + a, h0 + b) for h0 in heads]
        runs.append((800 + (g * 3 + 2) * DIL_W, 800 + (g * 3 + 3) * DIL_W))
    pieces = [jnp.zeros(w.shape[:-1] + (r,), w.dtype) if isinstance(r, int) else w[..., r[0]:r[1]] for r in runs]
    return jnp.concatenate(pieces, axis=-1).astype(BF16)


def _pad_heads(w, width):
    w = jnp.pad(w, ((0, 0), (0, 0), (0, 0), (0, MLA_SLOT - width)))
    return w.reshape(w.shape[0], w.shape[1], MLA_HEADS * MLA_SLOT).astype(BF16)


def _prep_w_q(w):
    w = w.reshape(DEPTH, MLA_Q_LORA, MLA_HEADS // 2, 2, MLA_NOPE + MLA_ROPE)
    nope = w[..., :MLA_NOPE]
    lo = w[..., MLA_NOPE:MLA_NOPE + MLA_HALF]
    hi = w[..., MLA_NOPE + MLA_HALF:]
    both = lambda t: t.reshape(t.shape[:-2] + (2 * MLA_HALF,))
    zeros = jnp.zeros(w.shape[:3] + (MLA_SLOT - MLA_NOPE - MLA_ROPE,), w.dtype)
    slot0 = jnp.concatenate([nope[..., 0, :], both(lo), zeros], axis=-1)
    slot1 = jnp.concatenate([nope[..., 1, :], both(hi), zeros], axis=-1)
    return jnp.stack([slot0, slot1], axis=-2).reshape(DEPTH, MLA_Q_LORA, MLA_HEADS * MLA_SLOT).astype(BF16)


def _prep_w_kv(w):
    w = w.reshape(DEPTH, MLA_KV_LORA, MLA_HEADS, MLA_NOPE + MLA_V)
    return _pad_heads(w[..., :MLA_NOPE], MLA_NOPE), _pad_heads(w[..., MLA_NOPE:], MLA_V)


def _dest_kernel(ri_ref, ps_ref, o_ref):
    ri = ri_ref[...].astype(F32)
    lane = lax.broadcasted_iota(jnp.int32, ri.shape, 1)
    ps = ps_ref[...]

    def col(k):
        return jnp.sum(jnp.where(lane == k, ri, 0.0), axis=-1, keepdims=True)

    def dest(k):
        start = jnp.sum(jnp.where(lane == col(k).astype(jnp.int32), ps, 0.0), axis=-1, keepdims=True)
        return (start + col(2 + k)).astype(jnp.int32)

    o_ref[...] = jnp.where(lane == 0, dest(0), jnp.where(lane == 1, dest(1), 0))


def _dest_rows(ri, pstarts, tm):
    T = ri.shape[0]
    ps = jnp.zeros((1, 128), F32).at[0, :N_EXPERTS].set(pstarts.astype(F32))
    return pl.pallas_call(
        _dest_kernel,
        grid=(T // tm,),
        in_specs=[pl.BlockSpec((tm, 128), lambda i: (i, 0)), pl.BlockSpec((1, 128), lambda i: (0, 0))],
        out_specs=pl.BlockSpec((tm, 128), lambda i: (i, 0)),
        out_shape=jax.ShapeDtypeStruct((T, 128), jnp.int32),
        compiler_params=_cparams(("parallel",)),
        name="dest_rows",
    )(ri, ps)


def _dispatch_plan(ri, cnt, T, tm):
    counts = cnt[0, :N_EXPERTS].astype(jnp.int32)
    pcounts = (counts + FFN_BM - 1) // FFN_BM * FFN_BM
    pends = jnp.cumsum(pcounts)
    pstarts = pends - pcounts
    dest = _dest_rows(ri, pstarts, tm)[:, 0:TOP_K]
    n_blocks = (T * TOP_K) // FFN_BM + N_EXPERTS
    first_row = jnp.arange(n_blocks, dtype=jnp.int32) * FFN_BM
    block_e = jnp.minimum(jnp.sum((pends[None, :] <= first_row[:, None]).astype(jnp.int32), axis=1), N_EXPERTS - 1)
    nvb = pends[-1] // FFN_BM
    after = pends // FFN_BM
    next_of_expert = jnp.where(after < nvb, block_e[jnp.minimum(after, n_blocks - 1)], -1)
    run_of_expert = jnp.cumsum((pcounts > 0).astype(jnp.int32)) - 1
    ffn_plan = (block_e, next_of_expert[block_e].astype(jnp.int32), (run_of_expert[block_e] % 2).astype(jnp.int32),
                nvb.astype(jnp.int32).reshape(1))
    return dest.reshape(T * TOP_K), n_blocks * FFN_BM, ffn_plan


def kernel(x, p, positions, g_mix, w_in, g_q_lat, w_q_up, g_kv_lat, w_kv_up, w_branch_a, w_branch_b, w_out, g_ffn, w_router_grp, b_router_grp, w_router_exp, b_router_exp, w_exp_gate, w_exp_up, w_exp_down, g_ple, w_ple_gate, w_ple_proj, g_final):
    B, S, D = x.shape
    T = B * S
    TM_IN, TM_QKV, TQ, TM_POST, TM_ROUTE, TM_COMB = 512, 512, 512, 512, 1024, 256

    pos = positions.reshape(T)
    dil_scale = DIL_HEAD_DIM ** -0.5 * LOG2E
    mla_scale = (MLA_NOPE + MLA_ROPE) ** -0.5 * LOG2E
    tab_dil = jnp.concatenate(
        [_rope_tables(pos, ROPE_THETA, PARTIAL_ROT, 128, 0, sc, paired=True, span=DIL_ROT_LANES)
         for sc in (dil_scale, 1.0)], axis=1)
    tab_mla = jnp.concatenate([_rope_tables(pos, MLA_ROPE_THETA, MLA_ROPE, MLA_SLOT, MLA_NOPE, mla_scale, paired=True),
                               _rope_tables(pos, MLA_ROPE_THETA, MLA_ROPE, MLA_SLOT, MLA_NOPE, 1.0, paired=True)],
                              axis=1)

    w_in_p = _prep_w_in(w_in)
    wq = _prep_w_q(w_q_up)
    wk, wv = _prep_w_kv(w_kv_up)
    wr = jnp.concatenate([w_router_grp, w_router_exp, jnp.zeros((DEPTH, D, 128 - N_GROUPS - N_EXPERTS), F32)],
                         axis=-1).astype(BF16)
    br = jnp.concatenate([b_router_grp, b_router_exp.reshape(DEPTH, N_EXPERTS),
                          jnp.zeros((DEPTH, 128 - N_GROUPS - N_EXPERTS), F32)], axis=-1).reshape(DEPTH, 1, 128)
    wa, wb, wo = w_branch_a.astype(BF16), w_branch_b.astype(BF16), w_out.astype(BF16)
    wpg, wpp = w_ple_gate.astype(BF16), w_ple_proj.astype(BF16)
    gains = lambda g: g.reshape(DEPTH, 1, -1)

    xc = x.reshape(T, D)
    for i in range(DEPTH):
        main, d0, d1, d2 = _in_proj(xc, gains(g_mix), w_in_p, tab_dil, B, S, TM_IN, i)
        q, k, v = _mla_qkv(main, gains(g_q_lat), gains(g_kv_lat), wq, wk, wv, tab_mla, TM_QKV, i)
        o_a = _mla_attn(q, k, v, B, S, TQ)
        ogs, lss = [], []
        for (_, dil), qkv in zip(DIL_PATTERN, (d0, d1, d2)):
            og, ls = _dil_attn(qkv, B, dil, S // dil)
            ogs.append(og)
            lss.append(ls)
        x1, h2, lg = _post(xc, o_a, ogs, lss, main, wa, wb, wo, gains(g_ffn), wr, br, B, S, TM_POST, i)
        ri, rw, cnt = _route(lg, TM_ROUTE)
        dest, n_rows, ffn_plan = _dispatch_plan(ri, cnt, T, TM_ROUTE)
        xs = _sc_gather_rows(h2.reshape(T, ROW_TILE, 128), dest, n_out=n_rows)
        ys = _expert_ffn(*ffn_plan, xs.reshape(-1, 128), w_exp_gate, w_exp_up, w_exp_down, i)
        yg = _sc_gather_rows(ys.reshape(-1, ROW_TILE, 128), dest)
        xc = _combine_ple(yg.reshape(-1, 128), x1, rw, p.reshape(DEPTH * T, PLE_DIM), gains(g_ple), wpg, wpp,
                          g_final.reshape(1, D), TM_COMB, i)
    return xc.reshape(B, S, D)
```

```python
import functools
import math

import jax
import jax.numpy as jnp
import numpy as np
from jax import lax
from jax.experimental import pallas as pl
from jax.experimental.pallas import tpu as pltpu
from jax.experimental.pallas import tpu_sc as plsc

F32 = jnp.float32
BF16 = jnp.bfloat16

D_MODEL = 1024
DEPTH = 4
RMS_EPS = 1e-6
NEG = -1e30
LOG2E = math.log2(math.e)

MLA_HEADS = 16
MLA_Q_LORA = 512
MLA_KV_LORA = 256
MLA_NOPE = 64
MLA_ROPE = 32
MLA_V = 64
MLA_ROPE_THETA = 10000.0
MLA_SLOT = 128
MLA_HALF = MLA_ROPE // 2


def _mla_head_lanes(lane, head):
    l = lane % MLA_SLOT
    nope = jnp.logical_and(lane // MLA_SLOT == head, l < MLA_NOPE)
    rope = jnp.logical_and(l >= MLA_NOPE + head * MLA_HALF, l < MLA_NOPE + (head + 1) * MLA_HALF)
    return jnp.logical_or(nope, rope)

DIL_PATTERN = ((128, 1), (512, 4), (2048, 16))
DIL_GROUPS = 3
DIL_HEADS = 4
DIL_HEAD_DIM = 64
DIL_W = DIL_HEADS * DIL_HEAD_DIM
DIL_BLK = 128
DIL_UNROLL = 16
ROPE_THETA = 500000.0
PARTIAL_ROT = DIL_HEAD_DIM // 4
DIL_HALF = PARTIAL_ROT // 2
DIL_ROT_LANES = DIL_HEADS * DIL_HALF
DIL_REST = (DIL_HEAD_DIM - PARTIAL_ROT) // 2


def _dil_qk_head_of_lane(lane):
    l = lane % 128
    plain = l - DIL_ROT_LANES
    plain_head = sum((plain >= h * DIL_REST).astype(jnp.int32) for h in range(1, DIL_HEADS))
    return jnp.where(l < DIL_ROT_LANES, l // DIL_HALF, plain_head)

N_GROUPS = 8
EXPERTS_PER_GROUP = 8
N_EXPERTS = 64
TOP_K = 2
EXPERT_FF = 256
PLE_DIM = 256

IN_TN = 768
MAIN_COLS = 3072
IN_COLS_PAD = MAIN_COLS + DIL_GROUPS * 3 * DIL_W
N_MAIN_TILES = MAIN_COLS // IN_TN

FFN_BM = 256
ROW_TILE = D_MODEL // 256
U32 = jnp.uint32
VMEM_LIMIT = 48 * 1024 * 1024

SC_CORES = 2
SC_SUBCORES = 16
SC_LANES = 16
SC_CHUNK = 64


def _cparams(sem, bounds_checks=True):
    return pltpu.CompilerParams(dimension_semantics=sem, vmem_limit_bytes=VMEM_LIMIT,
                                disable_bounds_checks=not bounds_checks)


def _layer_spec(shape, layer, col=None):
    def index(*grid_idx):
        return (layer,) + (0,) * (len(shape) - 1) + ((col(*grid_idx),) if col else (0,))

    return pl.BlockSpec((None,) + tuple(shape), index)


def _rms(x, g):
    return x * lax.rsqrt(jnp.mean(x * x, axis=-1, keepdims=True) + RMS_EPS) * g


def _rope128(x, c, s_up, s_dn, half):
    return x * c + pltpu.roll(x, half, 1) * s_up + pltpu.roll(x, 128 - half, 1) * s_dn


def _in_kernel(x_ref, g_ref, w_ref, tab_ref, main_ref, d0_ref, d1_ref, d2_ref, xn_ref, acc_ref):
    j = pl.program_id(1)

    @pl.when(j == 0)
    def _():
        xn_ref[...] = _rms(x_ref[...], g_ref[...]).astype(BF16)

    acc = jnp.dot(xn_ref[...], w_ref[...], preferred_element_type=F32)

    @pl.when(j < N_MAIN_TILES)
    def _():
        main_ref[...] = acc.astype(BF16)

    def dil_tile(out_ref, dil):
        chunk = lambda c: acc[:, c * 128:(c + 1) * 128]
        for base, t0 in ((0, 0), (2, 3)):
            cos, sin_n, sin_p = [tab_ref[:, (t0 + t) * 128:(t0 + t + 1) * 128] for t in range(3)]
            lo, hi = chunk(base), chunk(base + 1)
            acc_ref[base] = lo * cos + hi * sin_n
            acc_ref[base + 1] = hi * cos + lo * sin_p
        for c in range(4, 6):
            acc_ref[c] = chunk(c)
        rows = acc_ref.shape[1] // dil
        for r in range(dil):
            for c in range(6):
                out_ref[0, r, :, c * 128:(c + 1) * 128] = acc_ref[c, pl.ds(r, rows, stride=dil), :].astype(BF16)

    for gi, (_, dil) in enumerate(DIL_PATTERN):
        pl.when(j == N_MAIN_TILES + gi)(functools.partial(dil_tile, (d0_ref, d1_ref, d2_ref)[gi], dil))


def _in_proj(x2d, g, w, tab, B, S, tm, layer):
    T = x2d.shape[0]
    nt = S // tm
    dil_shapes = [jax.ShapeDtypeStruct((B, d, S // d, 3 * DIL_W), BF16) for _, d in DIL_PATTERN]
    dil_specs = [pl.BlockSpec((1, d, tm // d, 3 * DIL_W), lambda i, j, nt=nt: (i // nt, 0, i % nt, 0))
                 for _, d in DIL_PATTERN]
    return pl.pallas_call(
        _in_kernel,
        grid=(T // tm, IN_COLS_PAD // IN_TN),
        in_specs=[
            pl.BlockSpec((tm, D_MODEL), lambda i, j: (i, 0)),
            _layer_spec((1, D_MODEL), layer),
            _layer_spec((D_MODEL, IN_TN), layer, col=lambda i, j: j),
            pl.BlockSpec((tm, 6 * 128), lambda i, j: (i, 0)),
        ],
        out_specs=[pl.BlockSpec((tm, IN_TN), lambda i, j: (i, jnp.minimum(j, N_MAIN_TILES - 1)))] + dil_specs,
        out_shape=[jax.ShapeDtypeStruct((T, MAIN_COLS), BF16)] + dil_shapes,
        scratch_shapes=[pltpu.VMEM((tm, D_MODEL), BF16), pltpu.VMEM((IN_TN // 128, tm, 128), F32)],
        compiler_params=_cparams(("parallel", "arbitrary")),
        name="in_proj",
    )(x2d, g, w, tab)


def _qkv_kernel(cq_ref, ckv_ref, kpe_ref, gq_ref, gkv_ref, wq_ref, wk_ref, wv_ref, tab_ref,
                q_ref, k_ref, v_ref):
    qn = _rms(cq_ref[...].astype(F32), gq_ref[...]).astype(BF16)
    kvn = _rms(ckv_ref[...].astype(F32), gkv_ref[...]).astype(BF16)
    qacc = jnp.dot(qn, wq_ref[...], preferred_element_type=F32)
    kacc = jnp.dot(kvn, wk_ref[...], preferred_element_type=F32)
    vacc = jnp.dot(kvn, wv_ref[...], preferred_element_type=F32)
    lane_v = lax.broadcasted_iota(jnp.int32, vacc.shape, 1)
    v_ref[...] = jnp.where(lane_v % MLA_SLOT < MLA_V, vacc, 1.0).astype(BF16)
    cq, snq, spq, ck, snk, spk = [tab_ref[:, t * 128:(t + 1) * 128] for t in range(6)]
    kpe_lo = kpe_ref[:, 0:128].astype(F32)
    kpe_hi = kpe_ref[:, 128:256].astype(F32)
    krot = (kpe_lo * ck + kpe_hi * snk, kpe_hi * ck + kpe_lo * spk)
    for pair in range(MLA_HEADS // 2):
        s0 = slice(2 * pair * MLA_SLOT, (2 * pair + 1) * MLA_SLOT)
        s1 = slice((2 * pair + 1) * MLA_SLOT, (2 * pair + 2) * MLA_SLOT)
        q0, q1 = qacc[:, s0], qacc[:, s1]
        q_ref[:, s0] = (q0 * cq + q1 * snq).astype(BF16)
        q_ref[:, s1] = (q1 * cq + q0 * spq).astype(BF16)
        k_ref[:, s0] = (kacc[:, s0] + krot[0]).astype(BF16)
        k_ref[:, s1] = (kacc[:, s1] + krot[1]).astype(BF16)


def _mla_qkv(main, gq, gkv, wq, wk, wv, tab, tm, layer):
    T = main.shape[0]
    return pl.pallas_call(
        _qkv_kernel,
        grid=(T // tm,),
        in_specs=[
            pl.BlockSpec((tm, MLA_Q_LORA), lambda i: (i, 2048 // MLA_Q_LORA)),
            pl.BlockSpec((tm, MLA_KV_LORA), lambda i: (i, 2560 // MLA_KV_LORA)),
            pl.BlockSpec((tm, 256), lambda i: (i, 2816 // 256)),
            _layer_spec((1, MLA_Q_LORA), layer),
            _layer_spec((1, MLA_KV_LORA), layer),
            _layer_spec((MLA_Q_LORA, MLA_HEADS * MLA_SLOT), layer),
            _layer_spec((MLA_KV_LORA, MLA_HEADS * MLA_SLOT), layer),
            _layer_spec((MLA_KV_LORA, MLA_HEADS * MLA_SLOT), layer),
            pl.BlockSpec((tm, 6 * 128), lambda i: (i, 0)),
        ],
        out_specs=[
            pl.BlockSpec((tm, MLA_HEADS * MLA_SLOT), lambda i: (i, 0)),
            pl.BlockSpec((tm, MLA_HEADS * MLA_SLOT), lambda i: (i, 0)),
            pl.BlockSpec((tm, MLA_HEADS * MLA_SLOT), lambda i: (i, 0)),
        ],
        out_shape=[
            jax.ShapeDtypeStruct((T, MLA_HEADS * MLA_SLOT), BF16),
            jax.ShapeDtypeStruct((T, MLA_HEADS * MLA_SLOT), BF16),
            jax.ShapeDtypeStruct((T, MLA_HEADS * MLA_SLOT), BF16),
        ],
        compiler_params=_cparams(("parallel",)),
        name="mla_qkv",
    )(main, main, main, gq, gkv, wq, wk, wv, tab)


def _mla_attn_kernel(qa_ref, qb_ref, k_ref, v_ref, o_ref, m_ref, acc_ref, *, tq):
    p_id = pl.program_id(2)
    nq = k_ref.shape[1] // tq
    lane = lax.broadcasted_iota(jnp.int32, (tq, 2 * MLA_SLOT), 1)
    qh = []
    for q_ref in (qa_ref, qb_ref):
        qp = q_ref[0]
        zero = jnp.zeros_like(qp)
        qh.append([jnp.where(_mla_head_lanes(lane, h), qp, zero) for h in range(2)])
    hq = tq // 2
    dn = (((1,), (1,)), ((), ()))

    def step(blk, start, width, r0, mask, first=False):
        kb = k_ref[0, pl.ds(start, width), :]
        vb = v_ref[0, pl.ds(start, width), :]
        rows = tq - r0
        s_pair = lax.dot_general(jnp.concatenate([qh[blk][0][r0:], qh[blk][1][r0:]], axis=0), kb, dn,
                                 preferred_element_type=F32)
        for h in range(2):
            s = s_pair[h * rows:(h + 1) * rows]
            if mask is not None:
                s = jnp.where(mask, s, NEG)
            m_cur = jnp.max(s, axis=-1, keepdims=True)
            if first:
                m_new = jnp.broadcast_to(m_cur, (tq - r0, 128))
            else:
                m_prev = m_ref[blk, h, r0:, :]
                m_new = jnp.maximum(m_prev, m_cur)
            p = jnp.exp2(s - jnp.concatenate([m_new] * (width // 128), axis=-1))
            pv = jnp.dot(p.astype(BF16), vb[:, h * MLA_SLOT:(h + 1) * MLA_SLOT], preferred_element_type=F32)
            if first:
                acc_ref[blk, h, r0:, :] = pv
            else:
                acc_ref[blk, h, r0:, :] = jnp.exp2(m_prev - m_new) * acc_ref[blk, h, r0:, :] + pv
            m_ref[blk, h, r0:, :] = m_new

    def causal(rows):
        return lax.broadcasted_iota(jnp.int32, (rows, hq), 1) <= lax.broadcasted_iota(jnp.int32, (rows, hq), 0)

    def q_blocks(n_a):
        n_b = nq - 1 - n_a
        for blk, n_full in ((1, n_b), (0, n_a)):
            for j in range(n_full):
                step(blk, j * tq, tq, 0, None, first=j == 0)
            step(blk, n_full * tq, hq, 0, causal(tq), first=n_full == 0)
            step(blk, n_full * tq + hq, hq, hq, causal(hq))

    for n_a in range(nq // 2):
        pl.when(p_id == n_a)(functools.partial(q_blocks, n_a))
    lane_o = lax.broadcasted_iota(jnp.int32, (tq, 2 * MLA_V), 1)
    for blk in range(2):
        outs = []
        for h in range(2):
            a = acc_ref[blk, h]
            outs.append(a / pltpu.roll(a, MLA_V, 1))
        o_ref[0, blk, 0] = jnp.where(lane_o < MLA_V, outs[0], pltpu.roll(outs[1], MLA_V, 1)).astype(BF16)


def _mla_attn(q, k, v, B, S, tq):
    nq = S // tq
    q = q.reshape(B, S, MLA_HEADS * MLA_SLOT)
    k = k.reshape(B, S, MLA_HEADS * MLA_SLOT)
    v = v.reshape(B, S, MLA_HEADS * MLA_SLOT)
    return pl.pallas_call(
        functools.partial(_mla_attn_kernel, tq=tq),
        grid=(B, MLA_HEADS // 2, nq // 2),
        in_specs=[
            pl.BlockSpec((1, tq, 2 * MLA_SLOT), lambda b, h, p: (b, p, h)),
            pl.BlockSpec((1, tq, 2 * MLA_SLOT), lambda b, h, p: (b, nq - 1 - p, h)),
            pl.BlockSpec((1, S, 2 * MLA_SLOT), lambda b, h, p: (b, 0, h)),
            pl.BlockSpec((1, S, 2 * MLA_SLOT), lambda b, h, p: (b, 0, h)),
        ],
        out_specs=pl.BlockSpec((1, 2, 1, tq, 2 * MLA_V), lambda b, h, p: (b, 0, p, 0, h)),
        out_shape=jax.ShapeDtypeStruct((B, 2, nq // 2, tq, MLA_HEADS * MLA_V), BF16),
        scratch_shapes=[pltpu.VMEM((2, 2, tq, 128), F32), pltpu.VMEM((2, 2, tq, MLA_SLOT), F32)],
        compiler_params=_cparams(("parallel", "parallel", "arbitrary")),
        name="mla_attn",
    )(q, q, k, v)


def _dil_kernel(qkv_ref, o_ref, lse_ref, bias_ref, *, nb):
    dil = qkv_ref.shape[1]
    width = 2 * DIL_BLK if nb > 1 else DIL_BLK
    row = lax.broadcasted_iota(jnp.int32, (DIL_BLK, width), 0)
    col = lax.broadcasted_iota(jnp.int32, (DIL_BLK, width), 1)
    bias_ref[0] = jnp.where(col <= row, 0.0, NEG)
    if nb > 1:
        later = jnp.logical_or(jnp.logical_and(col >= DIL_BLK, col - DIL_BLK <= row),
                               jnp.logical_and(col < DIL_BLK, col >= row))
        bias_ref[1] = jnp.where(later, 0.0, NEG)
    lane = lax.broadcasted_iota(jnp.int32, (DIL_BLK, DIL_W), 1)
    head_of_lane = lane // DIL_HEAD_DIM
    qk_head_of_lane = _dil_qk_head_of_lane(lane)
    dn = (((1,), (1,)), ((), ()))

    def by_head(parts):
        out = parts[DIL_HEADS - 1]
        for h in range(DIL_HEADS - 2, -1, -1):
            out = jnp.where(head_of_lane == h, parts[h], out)
        return out

    def unit(u):
        r = u // nb
        n = u % nb
        q0 = pl.multiple_of(n * DIL_BLK, DIL_BLK)
        q = qkv_ref[0, r, pl.ds(q0, DIL_BLK), 0:DIL_W]
        if nb > 1:
            w0 = pl.multiple_of(jnp.maximum(n - 1, 0) * DIL_BLK, DIL_BLK)
            bias = bias_ref[jnp.minimum(n, 1)]
        else:
            w0 = 0
            bias = bias_ref[0]
        kw = qkv_ref[0, r, pl.ds(w0, width), DIL_W:2 * DIL_W]
        vw = qkv_ref[0, r, pl.ds(w0, width), 2 * DIL_W:3 * DIL_W]
        zero = jnp.zeros_like(q)
        qs = jnp.concatenate([jnp.where(qk_head_of_lane == h, q, zero) for h in range(DIL_HEADS)], axis=0)
        s = lax.dot_general(qs, kw, dn, preferred_element_type=F32)
        s = (s.reshape(DIL_HEADS, DIL_BLK, width) + bias[None]).reshape(DIL_HEADS * DIL_BLK, width)
        m = jnp.max(s, axis=-1, keepdims=True)
        e = jnp.exp2(s - m)
        den = jnp.sum(e, axis=-1, keepdims=True)
        pv = jnp.dot(e.astype(BF16), vw, preferred_element_type=F32)
        lse = m + jnp.log2(den)
        blk = lambda t, h: t[h * DIL_BLK:(h + 1) * DIL_BLK]
        o = by_head([blk(pv, h) for h in range(DIL_HEADS)]) / by_head(
            [jnp.broadcast_to(blk(den, h), (DIL_BLK, DIL_W)) for h in range(DIL_HEADS)])
        o_ref[0, r, pl.ds(q0, DIL_BLK), :] = o.astype(BF16)
        lse_ref[0, r, pl.ds(q0, DIL_BLK), :] = by_head(
            [jnp.broadcast_to(blk(lse, h), (DIL_BLK, DIL_W)) for h in range(DIL_HEADS)])

    def body(t, carry):
        for u in range(DIL_UNROLL):
            unit(DIL_UNROLL * t + u)
        return carry

    lax.fori_loop(0, dil * nb // DIL_UNROLL, body, 0)


def _dil_attn(qkv, B, dil, L):
    nb = L // DIL_BLK
    return pl.pallas_call(
        functools.partial(_dil_kernel, nb=nb),
        grid=(B,),
        in_specs=[pl.BlockSpec((1, dil, L, 3 * DIL_W), lambda b: (b, 0, 0, 0))],
        out_specs=[
            pl.BlockSpec((1, dil, L, DIL_W), lambda b: (b, 0, 0, 0)),
            pl.BlockSpec((1, dil, L, DIL_W), lambda b: (b, 0, 0, 0)),
        ],
        out_shape=[
            jax.ShapeDtypeStruct((B, dil, L, DIL_W), BF16),
            jax.ShapeDtypeStruct((B, dil, L, DIL_W), F32),
        ],
        scratch_shapes=[pltpu.VMEM((2, DIL_BLK, 2 * DIL_BLK if nb > 1 else DIL_BLK), F32)],
        compiler_params=_cparams(("parallel",)),
        name=f"dil_attn_d{dil}",
    )(qkv)


def _post_kernel(x_ref, oa_ref, og0_ref, og1_ref, og2_ref, ls0_ref, ls1_ref, ls2_ref, gates_ref,
                 wa_ref, wb_ref, wo_ref, gffn_ref, wr_ref, br_ref,
                 x1_ref, h2_ref, lg_ref, ob_ref, *, tm):
    ls =[r[0] for r in (ls0_ref, ls1_ref, ls2_ref)]
    og = [r[0] for r in (og0_ref, og1_ref, og2_ref)]
    for gi, (_, dil) in enumerate(DIL_PATTERN):
        rows = tm // dil
        for r in range(dil):
            for c in range(2):
                sl = slice(c * 128, (c + 1) * 128)
                ob_ref[2 * gi + c, pl.ds(r, rows, stride=dil), :] = og[gi][r][:, sl].astype(F32)
                ob_ref[6 + 2 * gi + c, pl.ds(r, rows, stride=dil), :] = ls[gi][r][:, sl]

    def tok_major(k):
        return jnp.concatenate([ob_ref[2 * k], ob_ref[2 * k + 1]], axis=-1)

    l0, l1, l2 = tok_major(3), tok_major(4), tok_major(5)
    mx = jnp.maximum(jnp.maximum(l0, l1), l2)
    w0, w1, w2 = jnp.exp2(l0 - mx), jnp.exp2(l1 - mx), jnp.exp2(l2 - mx)
    ob = (w0 * tok_major(0) + w1 * tok_major(1) + w2 * tok_major(2)) / (w0 + w1 + w2)
    ya = jnp.dot(oa_ref[0, 0, 0], wa_ref[...], preferred_element_type=F32)
    yb = jnp.dot(ob.astype(BF16), wb_ref[...], preferred_element_type=F32)
    merged = (jax.nn.sigmoid(gates_ref[:, 0:D_MODEL].astype(F32)) * ya
              + jax.nn.sigmoid(gates_ref[:, D_MODEL:2 * D_MODEL].astype(F32)) * yb)
    x1 = x_ref[...] + jnp.dot(merged.astype(BF16), wo_ref[...], preferred_element_type=F32)
    x1_ref[...] = x1
    h2 = _rms(x1, gffn_ref[...])
    _rows_to_tiles(h2_ref, h2)
    lg_ref[...] = jnp.dot(h2.astype(BF16), wr_ref[...], preferred_element_type=F32) + br_ref[...]


def _post(x2d, oa, ogs, lss, main, wa, wb, wo, gffn, wr, br, B, S, tm, layer):
    T = x2d.shape[0]
    nt = S // tm
    res_specs = [pl.BlockSpec((1, d, tm // d, DIL_W), lambda i, nt=nt: (i // nt, 0, i % nt, 0))
                 for _, d in DIL_PATTERN]
    assert oa.shape[3] == tm and oa.shape[2] * 2 == nt

    def oa_index(i):
        it = i % nt
        late = it >= nt // 2
        return (i // nt, late.astype(jnp.int32), jnp.where(late, nt - 1 - it, it), 0, 0)

    return pl.pallas_call(
        functools.partial(_post_kernel, tm=tm),
        grid=(T // tm,),
        in_specs=[pl.BlockSpec((tm, D_MODEL), lambda i: (i, 0)),
                  pl.BlockSpec((1, 1, 1, tm, D_MODEL), oa_index)]
                 + res_specs + res_specs
                 + [pl.BlockSpec((tm, 2 * D_MODEL), lambda i: (i, 0)),
                    _layer_spec((D_MODEL, D_MODEL), layer),
                    _layer_spec((DIL_W, D_MODEL), layer),
                    _layer_spec((D_MODEL, D_MODEL), layer),
                    _layer_spec((1, D_MODEL), layer),
                    _layer_spec((D_MODEL, 128), layer),
                    _layer_spec((1, 128), layer)],
        out_specs=[pl.BlockSpec((tm, D_MODEL), lambda i: (i, 0)),
                   pl.BlockSpec((tm * ROW_TILE, 128), lambda i: (i, 0)),
                   pl.BlockSpec((tm, 128), lambda i: (i, 0))],
        out_shape=[jax.ShapeDtypeStruct((T, D_MODEL), F32),
                   jax.ShapeDtypeStruct((T * ROW_TILE, 128), U32),
                   jax.ShapeDtypeStruct((T, 128), F32)],
        scratch_shapes=[pltpu.VMEM((12, tm, 128), F32)],
        compiler_params=_cparams(("parallel",)),
        name="post_attn",
    )(x2d, oa, *ogs, *lss, main, wa, wb, wo, gffn, wr, br)


def _route_kernel(lg_ref, ri_ref, rw_ref, cnt_ref, carry_ref, lower_ref, *, tm):
    i = pl.program_id(0)

    @pl.when(i == 0)
    def _():
        carry_ref[...] = jnp.zeros(carry_ref.shape, F32)
        r_i = lax.broadcasted_iota(jnp.int32, (tm, tm), 0)
        c_i = lax.broadcasted_iota(jnp.int32, (tm, tm), 1)
        lower_ref[...] = jnp.where(c_i < r_i, 1.0, 0.0).astype(BF16)

    lg = lg_ref[...]
    lane = lax.broadcasted_iota(jnp.int32, lg.shape, 1)
    lane_f = lane.astype(F32)
    ninf = jnp.float32(-jnp.inf)

    def first_max(vals):
        vmax = jnp.max(vals, axis=-1, keepdims=True)
        idx = jnp.min(jnp.where(vals == vmax, lane_f, 128.0), axis=-1, keepdims=True)
        return vmax, idx.astype(jnp.int32)

    gl = jnp.where(lane < N_GROUPS, lg, ninf)
    gmax, g_sel = first_max(gl)
    p_g = 1.0 / jnp.sum(jnp.exp(gl - gmax), axis=-1, keepdims=True)
    lo = N_GROUPS + g_sel * EXPERTS_PER_GROUP
    el = jnp.where(jnp.logical_and(lane >= lo, lane < lo + EXPERTS_PER_GROUP), lg, ninf)
    v0, i0 = first_max(el)
    v1, i1 = first_max(jnp.where(lane == i0, ninf, el))
    t = jnp.exp(v1 - v0)
    w0 = p_g / (1.0 + t)
    w1 = p_g * t / (1.0 + t)
    e0 = i0 - N_GROUPS
    e1 = i1 - N_GROUPS
    hit0 = lane == e0
    hit1 = lane == e1
    oh = jnp.where(jnp.logical_or(hit0, hit1), 1.0, 0.0).astype(F32)
    excl = jnp.dot(lower_ref[...], oh.astype(BF16), preferred_element_type=F32) + carry_ref[...]
    r0 = jnp.sum(jnp.where(hit0, excl, 0.0), axis=-1, keepdims=True).astype(jnp.int32)
    r1 = jnp.sum(jnp.where(hit1, excl, 0.0), axis=-1, keepdims=True).astype(jnp.int32)
    carry_ref[...] = carry_ref[...] + jnp.sum(oh, axis=0, keepdims=True)
    zi = jnp.zeros(lg.shape, jnp.int32)
    ri_ref[...] = jnp.where(lane == 0, e0, jnp.where(lane == 1, e1, jnp.where(lane == 2, r0, jnp.where(lane == 3, r1, zi))))
    rw_ref[...] = jnp.where(lane == 0, w0, jnp.where(lane == 1, w1, jnp.zeros(lg.shape, F32)))
    cnt_ref[...] = carry_ref[...]


def _route(lg, tm):
    T = lg.shape[0]
    return pl.pallas_call(
        functools.partial(_route_kernel, tm=tm),
        grid=(T // tm,),
        in_specs=[pl.BlockSpec((tm, 128), lambda i: (i, 0))],
        out_specs=[pl.BlockSpec((tm, 128), lambda i: (i, 0)),
                   pl.BlockSpec((tm, 128), lambda i: (i, 0)),
                   pl.BlockSpec((1, 128), lambda i: (0, 0))],
        out_shape=[jax.ShapeDtypeStruct((T, 128), jnp.int32),
                   jax.ShapeDtypeStruct((T, 128), F32),
                   jax.ShapeDtypeStruct((1, 128), F32)],
        scratch_shapes=[pltpu.VMEM((1, 128), F32), pltpu.VMEM((tm, tm), BF16)],
        compiler_params=_cparams(("arbitrary",)),
        name="route",
    )(lg)


def _sc_gather_rows(table, idx, n_out=None):
    inverse = n_out is not None
    n_src = idx.shape[0]
    n = n_out if inverse else n_src
    n_workers = SC_CORES * SC_SUBCORES
    per_w = n // n_workers
    n_chunks = per_w // SC_CHUNK
    assert per_w * n_workers == n and n_chunks * SC_CHUNK == per_w and n_chunks % 2 == 0
    mesh = plsc.VectorSubcoreMesh(core_axis_name="c", subcore_axis_name="s",
                                  num_cores=SC_CORES, num_subcores=SC_SUBCORES)

    @functools.partial(
        pl.kernel, mesh=mesh,
        out_type=jax.ShapeDtypeStruct((n,) + table.shape[1:], table.dtype),
        scratch_types=[pltpu.VMEM((per_w,), jnp.int32),
                       pltpu.VMEM((SC_CHUNK,) + table.shape[1:], table.dtype),
                       pltpu.VMEM((SC_CHUNK,) + table.shape[1:], table.dtype),
                       pltpu.SemaphoreType.DMA, pltpu.SemaphoreType.DMA,
                       pltpu.VMEM((n_src if inverse else SC_LANES,), jnp.int32)],
        compiler_params=pltpu.CompilerParams(use_tc_tiling_on_sc=True, needs_layout_passes=not inverse),
        name="sc_dispatch_rows" if inverse else "sc_gather_rows",
    )
    def gather(table_hbm, idx_hbm, out_hbm, idx_v, rows_a, rows_b, sem_a, sem_b, map_v):
        wid = lax.axis_index("s") * SC_CORES + lax.axis_index("c")
        base = wid * per_w
        if inverse:
            pltpu.sync_copy(idx_hbm, map_v)
            lanes = lax.iota(jnp.int32, SC_LANES)

            @pl.loop(0, per_w // SC_LANES)
            def _(j):
                idx_v[pl.ds(j * SC_LANES, SC_LANES)] = lax.rem(base + j * SC_LANES + lanes, table.shape[0])

            @pl.loop(0, n_src // SC_LANES)
            def _(a):
                local = map_v[pl.ds(a * SC_LANES, SC_LANES)] - base
                mine = jnp.logical_and(local >= 0, local < per_w)
                plsc.store_scatter(idx_v, [local], lax.div(a * SC_LANES + lanes, TOP_K), mask=mine)
        else:
            pltpu.sync_copy(idx_hbm.at[pl.ds(base, per_w)], idx_v)

        def fetch(chunk, rows_v, sem):
            return pltpu.make_async_copy(table_hbm.at[idx_v.at[pl.ds(chunk * SC_CHUNK, SC_CHUNK)]], rows_v, sem)

        def flush(chunk, rows_v):
            pltpu.sync_copy(rows_v, out_hbm.at[pl.ds(base + chunk * SC_CHUNK, SC_CHUNK)])

        fetch(0, rows_a, sem_a).start()

        @pl.loop(0, n_chunks, step=2)
        def _(c):
            fetch(c + 1, rows_b, sem_b).start()
            fetch(c, rows_a, sem_a).wait()
            flush(c, rows_a)

            @pl.when(c + 2 < n_chunks)
            def _():
                fetch(c + 2, rows_a, sem_a).start()

            fetch(c + 1, rows_b, sem_b).wait()
            flush(c + 1, rows_b)

    return gather(table, idx)


def _rows_from_tiles(ref, first, n_rows, stride=ROW_TILE):
    words = [ref[pl.ds(first + c, n_rows, stride=stride), :] for c in range(ROW_TILE)]
    lo = [pltpu.bitcast(w << 16, F32) for w in words]
    hi = [pltpu.bitcast(w & U32(0xFFFF0000), F32) for w in words]
    return jnp.concatenate(lo + hi, axis=-1)


def _rows_to_tiles(ref, val):
    half = D_MODEL // 2

    def bits(x):
        return pltpu.bitcast(x.astype(BF16).astype(F32), U32)

    for c in range(ROW_TILE):
        lo = bits(val[:, c * 128:(c + 1) * 128])
        hi = bits(val[:, half + c * 128:half + (c + 1) * 128])
        ref[pl.ds(c, val.shape[0], stride=ROW_TILE), :] = hi | (lo >> 16)


def _ffn_kernel(be_ref, nx_ref, par_ref, nvb_ref, x_ref, w1_hbm, w3_hbm, w2_hbm, y_ref,
                w1f_ref, w3f_ref, w2f_ref, w1b_ref, w3b_ref, w2b_ref, sem, *, layer):
    b = pl.program_id(0)
    nvb = nvb_ref[0]
    expert = be_ref[b]
    slot = par_ref[b]
    new_expert = jnp.logical_or(b == 0, expert != be_ref[jnp.maximum(b - 1, 0)])

    def fetch(e, sl):
        return [pltpu.make_async_copy(w_hbm.at[layer, e], wf_ref.at[sl], sem.at[sl])
                for w_hbm, wf_ref in ((w1_hbm, w1f_ref), (w3_hbm, w3f_ref), (w2_hbm, w2f_ref))]

    @pl.when(jnp.logical_and(b == 0, nvb > 0))
    def _():
        for cp in fetch(expert, slot):
            cp.start()

    @pl.when(jnp.logical_and(b < nvb, new_expert))
    def _():
        for cp in fetch(expert, slot):
            cp.wait()

        @pl.when(nx_ref[b] >= 0)
        def _():
            for cp in fetch(nx_ref[b], 1 - slot):
                cp.start()

        w1b_ref[...] = w1f_ref[slot].astype(BF16)
        w3b_ref[...] = w3f_ref[slot].astype(BF16)
        w2b_ref[...] = w2f_ref[slot].astype(BF16)

    @pl.when(b < nvb)
    def _():
        xb = _rows_from_tiles(x_ref, 0, FFN_BM).astype(BF16)
        h1 = jnp.dot(xb, w1b_ref[...], preferred_element_type=F32)
        h3 = jnp.dot(xb, w3b_ref[...], preferred_element_type=F32)
        a = (jax.nn.silu(h1) * h3).astype(BF16)
        _rows_to_tiles(y_ref, jnp.dot(a, w2b_ref[...], preferred_element_type=F32))

    @pl.when(b >= nvb)
    def _():
        y_ref[...] = jnp.zeros(y_ref.shape, U32)


def _expert_ffn(block_e, next_e, parity, nvb, xs, w1, w3, w2, layer):
    nb = block_e.shape[0]

    def used(b, *prefetch):
        return jnp.minimum(b, jnp.maximum(prefetch[-1][0] - 1, 0))

    return pl.pallas_call(
        functools.partial(_ffn_kernel, layer=layer),
        grid_spec=pltpu.PrefetchScalarGridSpec(
            num_scalar_prefetch=4,
            grid=(nb,),
            in_specs=[
                pl.BlockSpec((FFN_BM * ROW_TILE, 128), lambda b, *prefetch: (used(b, *prefetch), 0)),
                pl.BlockSpec(memory_space=pl.ANY),
                pl.BlockSpec(memory_space=pl.ANY),
                pl.BlockSpec(memory_space=pl.ANY),
            ],
            out_specs=pl.BlockSpec((FFN_BM * ROW_TILE, 128), lambda b, *prefetch: (b, 0)),
            scratch_shapes=[pltpu.VMEM((2, D_MODEL, EXPERT_FF), F32), pltpu.VMEM((2, D_MODEL, EXPERT_FF), F32),
                            pltpu.VMEM((2, EXPERT_FF, D_MODEL), F32),
                            pltpu.VMEM((D_MODEL, EXPERT_FF), BF16), pltpu.VMEM((D_MODEL, EXPERT_FF), BF16),
                            pltpu.VMEM((EXPERT_FF, D_MODEL), BF16),
                            pltpu.SemaphoreType.DMA((2,))],
        ),
        out_shape=jax.ShapeDtypeStruct((nb * FFN_BM * ROW_TILE, 128), U32),
        compiler_params=_cparams(("arbitrary",)),
        name="expert_ffn",
    )(block_e, next_e, parity, nvb, xs, w1, w3, w2)


def _comb_kernel(yg_ref, x1_ref, rw_ref, p_ref, gple_ref, wpg_ref, wpp_ref, gout_ref, o_ref, *, tm, final):
    w = rw_ref[...]
    y0 = _rows_from_tiles(yg_ref, 0, tm, stride=TOP_K * ROW_TILE)
    y1 = _rows_from_tiles(yg_ref, ROW_TILE, tm, stride=TOP_K * ROW_TILE)
    x2 = x1_ref[...] + (y0 * w[:, 0:1] + y1 * w[:, 1:2])
    e = jnp.dot(p_ref[...].astype(BF16), wpp_ref[...], preferred_element_type=F32)
    gate = jax.nn.sigmoid(jnp.dot(_rms(x2, gple_ref[...]).astype(BF16), wpg_ref[...], preferred_element_type=F32))
    x3 = x2 + gate * e
    o_ref[...] = _rms(x3, gout_ref[...]) if final else x3


def _combine_ple(yg, x1, rw, p2d, gple, wpg, wpp, gout, tm, layer):
    T = x1.shape[0]
    nt = T // tm
    return pl.pallas_call(
        functools.partial(_comb_kernel, tm=tm, final=layer == DEPTH - 1),
        grid=(nt,),
        in_specs=[
            pl.BlockSpec((tm * TOP_K * ROW_TILE, 128), lambda i: (i, 0)),
            pl.BlockSpec((tm, D_MODEL), lambda i: (i, 0)),
            pl.BlockSpec((tm, 128), lambda i: (i, 0)),
            pl.BlockSpec((tm, PLE_DIM), lambda i: (layer * nt + i, 0)),
            _layer_spec((1, D_MODEL), layer),
            _layer_spec((D_MODEL, D_MODEL), layer),
            _layer_spec((PLE_DIM, D_MODEL), layer),
            pl.BlockSpec((1, D_MODEL), lambda i: (0, 0)),
        ],
        out_specs=pl.BlockSpec((tm, D_MODEL), lambda i: (i, 0)),
        out_shape=jax.ShapeDtypeStruct((T, D_MODEL), F32),
        compiler_params=_cparams(("parallel",)),
        name="combine_ple",
    )(yg, x1, rw, p2d, gple, wpg, wpp, gout)


def _rope_tables(pos, theta, rot_dim, period, offset, scale, paired=False, span=None):
    half = rot_dim // 2
    span = rot_dim if span is None else span
    inv = jnp.float32(theta) ** (-jnp.arange(half, dtype=F32) * 2.0 / rot_dim)
    rel = np.arange(128) % period - offset
    rot = (rel >= 0) & (rel < span)
    upper = rot & (rel >= half)
    lower = rot & (rel < half)
    ang = pos.astype(F32)[:, None] * inv[np.where(rot, rel % half, 0)][None, :]
    cos, sin = jnp.cos(ang), jnp.sin(ang)
    if paired:
        tabs = (jnp.where(rot, cos, 1.0), jnp.where(rot, -sin, 0.0), jnp.where(rot, sin, 0.0))
    else:
        tabs = (jnp.where(rot, cos, 1.0), jnp.where(upper, sin, 0.0), jnp.where(lower, -sin, 0.0))
    return jnp.concatenate(tabs, axis=1) * jnp.float32(scale)


def _prep_w_in(w):
    w = w.astype(BF16)
    c_q = w[..., 0:512]
    c_kv = w[..., 512:768]
    k_pe = w[..., 768:800]
    dil = w[..., 800:800 + 2304].reshape(w.shape[:-1] + (DIL_GROUPS, 3, DIL_HEADS, DIL_HEAD_DIM))

    def qk_layout(t):
        flat = lambda u: u.reshape(u.shape[:-2] + (-1,))
        return jnp.concatenate([flat(t[..., :DIL_HALF]), flat(t[..., PARTIAL_ROT:PARTIAL_ROT + DIL_REST]),
                                flat(t[..., DIL_HALF:PARTIAL_ROT]), flat(t[..., PARTIAL_ROT + DIL_REST:])], axis=-1)

    dil = jnp.concatenate([jnp.concatenate([qk_layout(dil[..., g, 0, :, :]), qk_layout(dil[..., g, 1, :, :]),
                                            dil[..., g, 2, :, :].reshape(w.shape[:-1] + (DIL_W,))], axis=-1)
                           for g in range(DIL_GROUPS)], axis=-1)
    gates = w[..., 3104:5152]
    zeros = lambda n: jnp.zeros(w.shape[:-1] + (n,), w.dtype)
    lo, hi = k_pe[..., :MLA_HALF], k_pe[..., MLA_HALF:]
    pad = MLA_SLOT - MLA_NOPE - MLA_ROPE
    kslot = jnp.concatenate([zeros(MLA_NOPE), lo, lo, zeros(pad), zeros(MLA_NOPE), hi, hi, zeros(pad)], axis=-1)
    return jnp.concatenate([gates, c_q, c_kv, kslot, dil], axis=-1)


def _pad_heads(w, width):
    w = jnp.pad(w, ((0, 0), (0, 0), (0, 0), (0, MLA_SLOT - width)))
    return w.reshape(w.shape[0], w.shape[1], MLA_HEADS * MLA_SLOT).astype(BF16)


def _prep_w_q(w):
    w = w.reshape(DEPTH, MLA_Q_LORA, MLA_HEADS // 2, 2, MLA_NOPE + MLA_ROPE)
    nope = w[..., :MLA_NOPE]
    lo = w[..., MLA_NOPE:MLA_NOPE + MLA_HALF]
    hi = w[..., MLA_NOPE + MLA_HALF:]
    both = lambda t: t.reshape(t.shape[:-2] + (2 * MLA_HALF,))
    zeros = jnp.zeros(w.shape[:3] + (MLA_SLOT - MLA_NOPE - MLA_ROPE,), w.dtype)
    slot0 = jnp.concatenate([nope[..., 0, :], both(lo), zeros], axis=-1)
    slot1 = jnp.concatenate([nope[..., 1, :], both(hi), zeros], axis=-1)
    return jnp.stack([slot0, slot1], axis=-2).reshape(DEPTH, MLA_Q_LORA, MLA_HEADS * MLA_SLOT).astype(BF16)


def _prep_w_kv(w):
    w = w.reshape(DEPTH, MLA_KV_LORA, MLA_HEADS, MLA_NOPE + MLA_V)
    return _pad_heads(w[..., :MLA_NOPE], MLA_NOPE), _pad_heads(w[..., MLA_NOPE:], MLA_V)


def _dest_kernel(ri_ref, ps_ref, o_ref):
    ri = ri_ref[...].astype(F32)
    lane = lax.broadcasted_iota(jnp.int32, ri.shape, 1)
    ps = ps_ref[...]

    def col(k):
        return jnp.sum(jnp.where(lane == k, ri, 0.0), axis=-1, keepdims=True)

    def dest(k):
        start = jnp.sum(jnp.where(lane == col(k).astype(jnp.int32), ps, 0.0), axis=-1, keepdims=True)
        return (start + col(2 + k)).astype(jnp.int32)

    o_ref[...] = jnp.where(lane == 0, dest(0), jnp.where(lane == 1, dest(1), 0))


def _dest_rows(ri, pstarts, tm):
    T = ri.shape[0]
    ps = jnp.zeros((1, 128), F32).at[0, :N_EXPERTS].set(pstarts.astype(F32))
    return pl.pallas_call(
        _dest_kernel,
        grid=(T // tm,),
        in_specs=[pl.BlockSpec((tm, 128), lambda i: (i, 0)), pl.BlockSpec((1, 128), lambda i: (0, 0))],
        out_specs=pl.BlockSpec((tm, 128), lambda i: (i, 0)),
        out_shape=jax.ShapeDtypeStruct((T, 128), jnp.int32),
        compiler_params=_cparams(("parallel",)),
        name="dest_rows",
    )(ri, ps)


def _dispatch_plan(ri, cnt, T, tm):
    counts = cnt[0, :N_EXPERTS].astype(jnp.int32)
    pcounts = (counts + FFN_BM - 1) // FFN_BM * FFN_BM
    pends = jnp.cumsum(pcounts)
    pstarts = pends - pcounts
    dest = _dest_rows(ri, pstarts, tm)[:, 0:TOP_K]
    n_blocks = (T * TOP_K) // FFN_BM + N_EXPERTS
    first_row = jnp.arange(n_blocks, dtype=jnp.int32) * FFN_BM
    block_e = jnp.minimum(jnp.sum((pends[None, :] <= first_row[:, None]).astype(jnp.int32), axis=1), N_EXPERTS - 1)
    nvb = pends[-1] // FFN_BM
    after = pends // FFN_BM
    next_of_expert = jnp.where(after < nvb, block_e[jnp.minimum(after, n_blocks - 1)], -1)
    run_of_expert = jnp.cumsum((pcounts > 0).astype(jnp.int32)) - 1
    ffn_plan = (block_e, next_of_expert[block_e].astype(jnp.int32), (run_of_expert[block_e] % 2).astype(jnp.int32),
                nvb.astype(jnp.int32).reshape(1))
    return dest.reshape(T * TOP_K), n_blocks * FFN_BM, ffn_plan


def kernel(x, p, positions, g_mix, w_in, g_q_lat, w_q_up, g_kv_lat, w_kv_up, w_branch_a, w_branch_b, w_out, g_ffn, w_router_grp, b_router_grp, w_router_exp, b_router_exp, w_exp_gate, w_exp_up, w_exp_down, g_ple, w_ple_gate, w_ple_proj, g_final):
    B, S, D = x.shape
    T = B * S
    TM_IN, TM_QKV, TQ, TM_POST, TM_ROUTE, TM_COMB = 512, 512, 512, 512, 1024, 256

    pos = positions.reshape(T)
    dil_scale = DIL_HEAD_DIM ** -0.5 * LOG2E
    mla_scale = (MLA_NOPE + MLA_ROPE) ** -0.5 * LOG2E
    tab_dil = jnp.concatenate(
        [_rope_tables(pos, ROPE_THETA, PARTIAL_ROT, 128, 0, sc, paired=True, span=DIL_ROT_LANES)
         for sc in (dil_scale, 1.0)], axis=1)
    tab_mla = jnp.concatenate([_rope_tables(pos, MLA_ROPE_THETA, MLA_ROPE, MLA_SLOT, MLA_NOPE, mla_scale, paired=True),
                               _rope_tables(pos, MLA_ROPE_THETA, MLA_ROPE, MLA_SLOT, MLA_NOPE, 1.0, paired=True)],
                              axis=1)

    w_in_p = _prep_w_in(w_in)
    wq = _prep_w_q(w_q_up)
    wk, wv = _prep_w_kv(w_kv_up)
    wr = jnp.concatenate([w_router_grp, w_router_exp, jnp.zeros((DEPTH, D, 128 - N_GROUPS - N_EXPERTS), F32)],
                         axis=-1).astype(BF16)
    br = jnp.concatenate([b_router_grp, b_router_exp.reshape(DEPTH, N_EXPERTS),
                          jnp.zeros((DEPTH, 128 - N_GROUPS - N_EXPERTS), F32)], axis=-1).reshape(DEPTH, 1, 128)
    wa, wb, wo = w_branch_a.astype(BF16), w_branch_b.astype(BF16), w_out.astype(BF16)
    wpg, wpp = w_ple_gate.astype(BF16), w_ple_proj.astype(BF16)
    gains = lambda g: g.reshape(DEPTH, 1, -1)

    xc = x.reshape(T, D)
    for i in range(DEPTH):
        main, d0, d1, d2 = _in_proj(xc, gains(g_mix), w_in_p, tab_dil, B, S, TM_IN, i)
        q, k, v = _mla_qkv(main, gains(g_q_lat), gains(g_kv_lat), wq, wk, wv, tab_mla, TM_QKV, i)
        o_a = _mla_attn(q, k, v, B, S, TQ)
        ogs, lss = [], []
        for (_, dil), qkv in zip(DIL_PATTERN, (d0, d1, d2)):
            og, ls = _dil_attn(qkv, B, dil, S // dil)
            ogs.append(og)
            lss.append(ls)
        x1, h2, lg = _post(xc, o_a, ogs, lss, main, wa, wb, wo, gains(g_ffn), wr, br, B, S, TM_POST, i)
        ri, rw, cnt = _route(lg, TM_ROUTE)
        dest, n_rows, ffn_plan = _dispatch_plan(ri, cnt, T, TM_ROUTE)
        xs = _sc_gather_rows(h2.reshape(T, ROW_TILE, 128), dest, n_out=n_rows)
        ys = _expert_ffn(*ffn_plan, xs.reshape(-1, 128), w_exp_gate, w_exp_up, w_exp_down, i)
        yg = _sc_gather_rows(ys.reshape(-1, ROW_TILE, 128), dest)
        xc = _combine_ple(yg.reshape(-1, 128), x1, rw, p.reshape(DEPTH * T, PLE_DIM), gains(g_ple), wpg, wpp,
                          g_final.reshape(1, D), TM_COMB, i)
    return xc.reshape(B, S, D)
```

```python
import functools
import math

import jax
import jax.numpy as jnp
import numpy as np
from jax import lax
from jax.experimental import pallas as pl
from jax.experimental.pallas import tpu as pltpu
from jax.experimental.pallas import tpu_sc as plsc

F32 = jnp.float32
BF16 = jnp.bfloat16

D_MODEL = 1024
DEPTH = 4
RMS_EPS = 1e-6
NEG = -1e30
LOG2E = math.log2(math.e)

MLA_HEADS = 16
MLA_Q_LORA = 512
MLA_KV_LORA = 256
MLA_NOPE = 64
MLA_ROPE = 32
MLA_V = 64
MLA_ROPE_THETA = 10000.0
MLA_SLOT = 128
MLA_HALF = MLA_ROPE // 2


def _mla_head_lanes(lane, head):
    l = lane % MLA_SLOT
    nope = jnp.logical_and(lane // MLA_SLOT == head, l < MLA_NOPE)
    rope = jnp.logical_and(l >= MLA_NOPE + head * MLA_HALF, l < MLA_NOPE + (head + 1) * MLA_HALF)
    return jnp.logical_or(nope, rope)

DIL_PATTERN = ((128, 1), (512, 4), (2048, 16))
DIL_GROUPS = 3
DIL_HEADS = 4
DIL_HEAD_DIM = 64
DIL_W = DIL_HEADS * DIL_HEAD_DIM
DIL_BLK = 128
DIL_UNROLL = 16
ROPE_THETA = 500000.0
PARTIAL_ROT = DIL_HEAD_DIM // 4
DIL_HALF = PARTIAL_ROT // 2
DIL_ROT_LANES = DIL_HEADS * DIL_HALF
DIL_REST = (DIL_HEAD_DIM - PARTIAL_ROT) // 2


def _dil_qk_head_of_lane(lane):
    l = lane % 128
    plain = l - DIL_ROT_LANES
    plain_head = sum((plain >= h * DIL_REST).astype(jnp.int32) for h in range(1, DIL_HEADS))
    return jnp.where(l < DIL_ROT_LANES, l // DIL_HALF, plain_head)

N_GROUPS = 8
EXPERTS_PER_GROUP = 8
N_EXPERTS = 64
TOP_K = 2
EXPERT_FF = 256
PLE_DIM = 256

IN_TN = 768
MAIN_COLS = 3072
IN_COLS_PAD = MAIN_COLS + DIL_GROUPS * 3 * DIL_W
N_MAIN_TILES = MAIN_COLS // IN_TN

FFN_BM = 256
ROW_TILE = D_MODEL // 256
U32 = jnp.uint32
VMEM_LIMIT = 48 * 1024 * 1024

SC_CORES = 2
SC_SUBCORES = 16
SC_LANES = 16
SC_CHUNK = 64


def _cparams(sem, bounds_checks=True):
    return pltpu.CompilerParams(dimension_semantics=sem, vmem_limit_bytes=VMEM_LIMIT,
                                disable_bounds_checks=not bounds_checks)


def _layer_spec(shape, layer, col=None):
    def index(*grid_idx):
        return (layer,) + (0,) * (len(shape) - 1) + ((col(*grid_idx),) if col else (0,))

    return pl.BlockSpec((None,) + tuple(shape), index)


def _rms(x, g):
    return x * lax.rsqrt(jnp.mean(x * x, axis=-1, keepdims=True) + RMS_EPS) * g


def _in_kernel(x_ref, g_ref, w_ref, tab_ref, main_ref, d0_ref, d1_ref, d2_ref, xn_ref, acc_ref):
    j = pl.program_id(1)

    @pl.when(j == 0)
    def _():
        xn_ref[...] = _rms(x_ref[...], g_ref[...]).astype(BF16)

    acc = jnp.dot(xn_ref[...], w_ref[...], preferred_element_type=F32)

    @pl.when(j < N_MAIN_TILES)
    def _():
        main_ref[...] = acc.astype(BF16)

    def dil_tile(out_ref, dil):
        chunk = lambda c: acc[:, c * 128:(c + 1) * 128]
        for base, t0 in ((0, 0), (2, 3)):
            cos, sin_n, sin_p = [tab_ref[:, (t0 + t) * 128:(t0 + t + 1) * 128] for t in range(3)]
            lo, hi = chunk(base), chunk(base + 1)
            acc_ref[base] = lo * cos + hi * sin_n
            acc_ref[base + 1] = hi * cos + lo * sin_p
        for c in range(4, 6):
            acc_ref[c] = chunk(c)
        rows = acc_ref.shape[1] // dil
        for r in range(dil):
            for c in range(6):
                out_ref[0, r, :, c * 128:(c + 1) * 128] = acc_ref[c, pl.ds(r, rows, stride=dil), :].astype(BF16)

    for gi, (_, dil) in enumerate(DIL_PATTERN):
        pl.when(j == N_MAIN_TILES + gi)(functools.partial(dil_tile, (d0_ref, d1_ref, d2_ref)[gi], dil))


def _in_proj(x2d, g, w, tab, B, S, tm, layer):
    T = x2d.shape[0]
    nt = S // tm
    dil_shapes = [jax.ShapeDtypeStruct((B, d, S // d, 3 * DIL_W), BF16) for _, d in DIL_PATTERN]
    dil_specs = [pl.BlockSpec((1, d, tm // d, 3 * DIL_W), lambda i, j, nt=nt: (i // nt, 0, i % nt, 0))
                 for _, d in DIL_PATTERN]
    return pl.pallas_call(
        _in_kernel,
        grid=(T // tm, IN_COLS_PAD // IN_TN),
        in_specs=[
            pl.BlockSpec((tm, D_MODEL), lambda i, j: (i, 0)),
            _layer_spec((1, D_MODEL), layer),
            _layer_spec((D_MODEL, IN_TN), layer, col=lambda i, j: j),
            pl.BlockSpec((tm, 6 * 128), lambda i, j: (i, 0)),
        ],
        out_specs=[pl.BlockSpec((tm, IN_TN), lambda i, j: (i, jnp.minimum(j, N_MAIN_TILES - 1)))] + dil_specs,
        out_shape=[jax.ShapeDtypeStruct((T, MAIN_COLS), BF16)] + dil_shapes,
        scratch_shapes=[pltpu.VMEM((tm, D_MODEL), BF16), pltpu.VMEM((IN_TN // 128, tm, 128), F32)],
        compiler_params=_cparams(("parallel", "arbitrary")),
        name="in_proj",
    )(x2d, g, w, tab)


def _qkv_kernel(cq_ref, ckv_ref, kpe_ref, gq_ref, gkv_ref, wq_ref, wk_ref, wv_ref, tab_ref,
                q_ref, k_ref, v_ref):
    qn = _rms(cq_ref[...].astype(F32), gq_ref[...]).astype(BF16)
    kvn = _rms(ckv_ref[...].astype(F32), gkv_ref[...]).astype(BF16)
    qacc = jnp.dot(qn, wq_ref[...], preferred_element_type=F32)
    kacc = jnp.dot(kvn, wk_ref[...], preferred_element_type=F32)
    vacc = jnp.dot(kvn, wv_ref[...], preferred_element_type=F32)
    lane_v = lax.broadcasted_iota(jnp.int32, vacc.shape, 1)
    v_ref[...] = jnp.where(lane_v % MLA_SLOT < MLA_V, vacc, 1.0).astype(BF16)
    cq, snq, spq, ck, snk, spk = [tab_ref[:, t * 128:(t + 1) * 128] for t in range(6)]
    kpe_lo = kpe_ref[:, 0:128].astype(F32)
    kpe_hi = kpe_ref[:, 128:256].astype(F32)
    krot = (kpe_lo * ck + kpe_hi * snk, kpe_hi * ck + kpe_lo * spk)
    for pair in range(MLA_HEADS // 2):
        s0 = slice(2 * pair * MLA_SLOT, (2 * pair + 1) * MLA_SLOT)
        s1 = slice((2 * pair + 1) * MLA_SLOT, (2 * pair + 2) * MLA_SLOT)
        q0, q1 = qacc[:, s0], qacc[:, s1]
        q_ref[:, s0] = (q0 * cq + q1 * snq).astype(BF16)
        q_ref[:, s1] = (q1 * cq + q0 * spq).astype(BF16)
        k_ref[:, s0] = (kacc[:, s0] + krot[0]).astype(BF16)
        k_ref[:, s1] = (kacc[:, s1] + krot[1]).astype(BF16)


def _mla_qkv(main, gq, gkv, wq, wk, wv, tab, tm, layer):
    T = main.shape[0]
    return pl.pallas_call(
        _qkv_kernel,
        grid=(T // tm,),
        in_specs=[
            pl.BlockSpec((tm, MLA_Q_LORA), lambda i: (i, 2048 // MLA_Q_LORA)),
            pl.BlockSpec((tm, MLA_KV_LORA), lambda i: (i, 2560 // MLA_KV_LORA)),
            pl.BlockSpec((tm, 256), lambda i: (i, 2816 // 256)),
            _layer_spec((1, MLA_Q_LORA), layer),
            _layer_spec((1, MLA_KV_LORA), layer),
            _layer_spec((MLA_Q_LORA, MLA_HEADS * MLA_SLOT), layer),
            _layer_spec((MLA_KV_LORA, MLA_HEADS * MLA_SLOT), layer),
            _layer_spec((MLA_KV_LORA, MLA_HEADS * MLA_SLOT), layer),
            pl.BlockSpec((tm, 6 * 128), lambda i: (i, 0)),
        ],
        out_specs=[
            pl.BlockSpec((tm, MLA_HEADS * MLA_SLOT), lambda i: (i, 0)),
            pl.BlockSpec((tm, MLA_HEADS * MLA_SLOT), lambda i: (i, 0)),
            pl.BlockSpec((tm, MLA_HEADS * MLA_SLOT), lambda i: (i, 0)),
        ],
        out_shape=[
            jax.ShapeDtypeStruct((T, MLA_HEADS * MLA_SLOT), BF16),
            jax.ShapeDtypeStruct((T, MLA_HEADS * MLA_SLOT), BF16),
            jax.ShapeDtypeStruct((T, MLA_HEADS * MLA_SLOT), BF16),
        ],
        compiler_params=_cparams(("parallel",)),
        name="mla_qkv",
    )(main, main, main, gq, gkv, wq, wk, wv, tab)


def _mla_attn_kernel(qa_ref, qb_ref, k_ref, v_ref, o_ref, m_ref, acc_ref, *, tq):
    p_id = pl.program_id(2)
    nq = k_ref.shape[1] // tq
    lane = lax.broadcasted_iota(jnp.int32, (tq, 2 * MLA_SLOT), 1)
    qh = []
    for q_ref in (qa_ref, qb_ref):
        qp = q_ref[0]
        zero = jnp.zeros_like(qp)
        qh.append([jnp.where(_mla_head_lanes(lane, h), qp, zero) for h in range(2)])
    hq = tq // 2
    dn = (((1,), (1,)), ((), ()))

    def step(blk, start, width, r0, mask, first=False):
        kb = k_ref[0, pl.ds(start, width), :]
        vb = v_ref[0, pl.ds(start, width), :]
        rows = tq - r0
        s_pair = lax.dot_general(jnp.concatenate([qh[blk][0][r0:], qh[blk][1][r0:]], axis=0), kb, dn,
                                 preferred_element_type=F32)
        for h in range(2):
            s = s_pair[h * rows:(h + 1) * rows]
            if mask is not None:
                s = jnp.where(mask, s, NEG)
            m_cur = jnp.max(s, axis=-1, keepdims=True)
            if first:
                m_new = jnp.broadcast_to(m_cur, (tq - r0, 128))
            else:
                m_prev = m_ref[blk, h, r0:, :]
                m_new = jnp.maximum(m_prev, m_cur)
            p = jnp.exp2(s - jnp.concatenate([m_new] * (width // 128), axis=-1))
            pv = jnp.dot(p.astype(BF16), vb[:, h * MLA_SLOT:(h + 1) * MLA_SLOT], preferred_element_type=F32)
            if first:
                acc_ref[blk, h, r0:, :] = pv
            else:
                acc_ref[blk, h, r0:, :] = jnp.exp2(m_prev - m_new) * acc_ref[blk, h, r0:, :] + pv
            m_ref[blk, h, r0:, :] = m_new

    def causal(rows):
        return lax.broadcasted_iota(jnp.int32, (rows, hq), 1) <= lax.broadcasted_iota(jnp.int32, (rows, hq), 0)

    def q_blocks(n_a):
        n_b = nq - 1 - n_a
        for blk, n_full in ((1, n_b), (0, n_a)):
            for j in range(n_full):
                step(blk, j * tq, tq, 0, None, first=j == 0)
            step(blk, n_full * tq, hq, 0, causal(tq), first=n_full == 0)
            step(blk, n_full * tq + hq, hq, hq, causal(hq))

    for n_a in range(nq // 2):
        pl.when(p_id == n_a)(functools.partial(q_blocks, n_a))
    lane_o = lax.broadcasted_iota(jnp.int32, (tq, 2 * MLA_V), 1)
    for blk in range(2):
        outs = []
        for h in range(2):
            a = acc_ref[blk, h]
            outs.append(a / pltpu.roll(a, MLA_V, 1))
        o_ref[0, blk, 0] = jnp.where(lane_o < MLA_V, outs[0], pltpu.roll(outs[1], MLA_V, 1)).astype(BF16)


def _mla_attn(q, k, v, B, S, tq):
    nq = S // tq
    q = q.reshape(B, S, MLA_HEADS * MLA_SLOT)
    k = k.reshape(B, S, MLA_HEADS * MLA_SLOT)
    v = v.reshape(B, S, MLA_HEADS * MLA_SLOT)
    return pl.pallas_call(
        functools.partial(_mla_attn_kernel, tq=tq),
        grid=(B, MLA_HEADS // 2, nq // 2),
        in_specs=[
            pl.BlockSpec((1, tq, 2 * MLA_SLOT), lambda b, h, p: (b, p, h)),
            pl.BlockSpec((1, tq, 2 * MLA_SLOT), lambda b, h, p: (b, nq - 1 - p, h)),
            pl.BlockSpec((1, S, 2 * MLA_SLOT), lambda b, h, p: (b, 0, h)),
            pl.BlockSpec((1, S, 2 * MLA_SLOT), lambda b, h, p: (b, 0, h)),
        ],
        out_specs=pl.BlockSpec((1, 2, 1, tq, 2 * MLA_V), lambda b, h, p: (b, 0, p, 0, h)),
        out_shape=jax.ShapeDtypeStruct((B, 2, nq // 2, tq, MLA_HEADS * MLA_V), BF16),
        scratch_shapes=[pltpu.VMEM((2, 2, tq, 128), F32), pltpu.VMEM((2, 2, tq, MLA_SLOT), F32)],
        compiler_params=_cparams(("parallel", "parallel", "arbitrary")),
        name="mla_attn",
    )(q, q, k, v)


def _dil_kernel(qkv_ref, o_ref, lse_ref, bias_ref, *, nb):
    dil = qkv_ref.shape[1]
    width = 2 * DIL_BLK if nb > 1 else DIL_BLK
    row = lax.broadcasted_iota(jnp.int32, (DIL_BLK, width), 0)
    col = lax.broadcasted_iota(jnp.int32, (DIL_BLK, width), 1)
    bias_ref[0] = jnp.where(col <= row, 0.0, NEG)
    if nb > 1:
        later = jnp.logical_or(jnp.logical_and(col >= DIL_BLK, col - DIL_BLK <= row),
                               jnp.logical_and(col < DIL_BLK, col >= row))
        bias_ref[1] = jnp.where(later, 0.0, NEG)
    lane = lax.broadcasted_iota(jnp.int32, (DIL_BLK, DIL_W), 1)
    head_of_lane = lane // DIL_HEAD_DIM
    qk_head_of_lane = _dil_qk_head_of_lane(lane)
    dn = (((1,), (1,)), ((), ()))

    def by_head(parts):
        out = parts[DIL_HEADS - 1]
        for h in range(DIL_HEADS - 2, -1, -1):
            out = jnp.where(head_of_lane == h, parts[h], out)
        return out

    def unit(u):
        r = u // nb
        n = u % nb
        q0 = pl.multiple_of(n * DIL_BLK, DIL_BLK)
        q = qkv_ref[0, r, pl.ds(q0, DIL_BLK), 0:DIL_W]
        if nb > 1:
            w0 = pl.multiple_of(jnp.maximum(n - 1, 0) * DIL_BLK, DIL_BLK)
            bias = bias_ref[jnp.minimum(n, 1)]
        else:
            w0 = 0
            bias = bias_ref[0]
        kw = qkv_ref[0, r, pl.ds(w0, width), DIL_W:2 * DIL_W]
        vw = qkv_ref[0, r, pl.ds(w0, width), 2 * DIL_W:3 * DIL_W]
        zero = jnp.zeros_like(q)
        qs = jnp.concatenate([jnp.where(qk_head_of_lane == h, q, zero) for h in range(DIL_HEADS)], axis=0)
        s = lax.dot_general(qs, kw, dn, preferred_element_type=F32)
        s = (s.reshape(DIL_HEADS, DIL_BLK, width) + bias[None]).reshape(DIL_HEADS * DIL_BLK, width)
        m = jnp.max(s, axis=-1, keepdims=True)
        e = jnp.exp2(s - m)
        den = jnp.sum(e, axis=-1, keepdims=True)
        pv = jnp.dot(e.astype(BF16), vw, preferred_element_type=F32)
        lse = m + jnp.log2(den)
        blk = lambda t, h: t[h * DIL_BLK:(h + 1) * DIL_BLK]
        o = by_head([blk(pv, h) for h in range(DIL_HEADS)]) / by_head(
            [jnp.broadcast_to(blk(den, h), (DIL_BLK, DIL_W)) for h in range(DIL_HEADS)])
        o_ref[0, r, pl.ds(q0, DIL_BLK), :] = o.astype(BF16)
        lse_ref[0, r, pl.ds(q0, DIL_BLK), :] = by_head(
            [jnp.broadcast_to(blk(lse, h), (DIL_BLK, DIL_W)) for h in range(DIL_HEADS)])

    def body(t, carry):
        for u in range(DIL_UNROLL):
            unit(DIL_UNROLL * t + u)
        return carry

    lax.fori_loop(0, dil * nb // DIL_UNROLL, body, 0)


def _dil_attn(qkv, B, dil, L):
    nb = L // DIL_BLK
    return pl.pallas_call(
        functools.partial(_dil_kernel, nb=nb),
        grid=(B,),
        in_specs=[pl.BlockSpec((1, dil, L, 3 * DIL_W), lambda b: (b, 0, 0, 0))],
        out_specs=[
            pl.BlockSpec((1, dil, L, DIL_W), lambda b: (b, 0, 0, 0)),
            pl.BlockSpec((1, dil, L, DIL_W), lambda b: (b, 0, 0, 0)),
        ],
        out_shape=[
            jax.ShapeDtypeStruct((B, dil, L, DIL_W), BF16),
            jax.ShapeDtypeStruct((B, dil, L, DIL_W), F32),
        ],
        scratch_shapes=[pltpu.VMEM((2, DIL_BLK, 2 * DIL_BLK if nb > 1 else DIL_BLK), F32)],
        compiler_params=_cparams(("parallel",)),
        name=f"dil_attn_d{dil}",
    )(qkv)


def _post_kernel(x_ref, oa_ref, og0_ref, og1_ref, og2_ref, ls0_ref, ls1_ref, ls2_ref, gates_ref,
                 wa_ref, wb_ref, wo_ref, gffn_ref, wr_ref, br_ref,
                 x1_ref, h2_ref, lg_ref, ob_ref, *, tm):
    ls =[r[0] for r in (ls0_ref, ls1_ref, ls2_ref)]
    og = [r[0] for r in (og0_ref, og1_ref, og2_ref)]
    for gi, (_, dil) in enumerate(DIL_PATTERN):
        rows = tm // dil
        for r in range(dil):
            for c in range(2):
                sl = slice(c * 128, (c + 1) * 128)
                ob_ref[2 * gi + c, pl.ds(r, rows, stride=dil), :] = og[gi][r][:, sl].astype(F32)
                ob_ref[6 + 2 * gi + c, pl.ds(r, rows, stride=dil), :] = ls[gi][r][:, sl]

    def tok_major(k):
        return jnp.concatenate([ob_ref[2 * k], ob_ref[2 * k + 1]], axis=-1)

    l0, l1, l2 = tok_major(3), tok_major(4), tok_major(5)
    mx = jnp.maximum(jnp.maximum(l0, l1), l2)
    w0, w1, w2 = jnp.exp2(l0 - mx), jnp.exp2(l1 - mx), jnp.exp2(l2 - mx)
    ob = (w0 * tok_major(0) + w1 * tok_major(1) + w2 * tok_major(2)) / (w0 + w1 + w2)
    ya = jnp.dot(oa_ref[0, 0, 0], wa_ref[...], preferred_element_type=F32)
    yb = jnp.dot(ob.astype(BF16), wb_ref[...], preferred_element_type=F32)
    merged = (jax.nn.sigmoid(gates_ref[:, 0:D_MODEL].astype(F32)) * ya
              + jax.nn.sigmoid(gates_ref[:, D_MODEL:2 * D_MODEL].astype(F32)) * yb)
    x1 = x_ref[...] + jnp.dot(merged.astype(BF16), wo_ref[...], preferred_element_type=F32)
    x1_ref[...] = x1
    h2 = _rms(x1, gffn_ref[...])
    _rows_to_tiles(h2_ref, h2)
    lg_ref[...] = jnp.dot(h2.astype(BF16), wr_ref[...], preferred_element_type=F32) + br_ref[...]


def _post(x2d, oa, ogs, lss, main, wa, wb, wo, gffn, wr, br, B, S, tm, layer):
    T = x2d.shape[0]
    nt = S // tm
    res_specs = [pl.BlockSpec((1, d, tm // d, DIL_W), lambda i, nt=nt: (i // nt, 0, i % nt, 0))
                 for _, d in DIL_PATTERN]
    assert oa.shape[3] == tm and oa.shape[2] * 2 == nt

    def oa_index(i):
        it = i % nt
        late = it >= nt // 2
        return (i // nt, late.astype(jnp.int32), jnp.where(late, nt - 1 - it, it), 0, 0)

    return pl.pallas_call(
        functools.partial(_post_kernel, tm=tm),
        grid=(T // tm,),
        in_specs=[pl.BlockSpec((tm, D_MODEL), lambda i: (i, 0)),
                  pl.BlockSpec((1, 1, 1, tm, D_MODEL), oa_index)]
                 + res_specs + res_specs
                 + [pl.BlockSpec((tm, 2 * D_MODEL), lambda i: (i, 0)),
                    _layer_spec((D_MODEL, D_MODEL), layer),
                    _layer_spec((DIL_W, D_MODEL), layer),
                    _layer_spec((D_MODEL, D_MODEL), layer),
                    _layer_spec((1, D_MODEL), layer),
                    _layer_spec((D_MODEL, 128), layer),
                    _layer_spec((1, 128), layer)],
        out_specs=[pl.BlockSpec((tm, D_MODEL), lambda i: (i, 0)),
                   pl.BlockSpec((tm * ROW_TILE, 128), lambda i: (i, 0)),
                   pl.BlockSpec((tm, 128), lambda i: (i, 0))],
        out_shape=[jax.ShapeDtypeStruct((T, D_MODEL), F32),
                   jax.ShapeDtypeStruct((T * ROW_TILE, 128), U32),
                   jax.ShapeDtypeStruct((T, 128), F32)],
        scratch_shapes=[pltpu.VMEM((12, tm, 128), F32)],
        compiler_params=_cparams(("parallel",)),
        name="post_attn",
    )(x2d, oa, *ogs, *lss, main, wa, wb, wo, gffn, wr, br)


def _route_kernel(lg_ref, ri_ref, rw_ref, cnt_ref, carry_ref, lower_ref, *, tm):
    i = pl.program_id(0)

    @pl.when(i == 0)
    def _():
        carry_ref[...] = jnp.zeros(carry_ref.shape, F32)
        r_i = lax.broadcasted_iota(jnp.int32, (tm, tm), 0)
        c_i = lax.broadcasted_iota(jnp.int32, (tm, tm), 1)
        lower_ref[...] = jnp.where(c_i < r_i, 1.0, 0.0).astype(BF16)

    lg = lg_ref[...]
    lane = lax.broadcasted_iota(jnp.int32, lg.shape, 1)
    lane_f = lane.astype(F32)
    ninf = jnp.float32(-jnp.inf)

    def first_max(vals):
        vmax = jnp.max(vals, axis=-1, keepdims=True)
        idx = jnp.min(jnp.where(vals == vmax, lane_f, 128.0), axis=-1, keepdims=True)
        return vmax, idx.astype(jnp.int32)

    gl = jnp.where(lane < N_GROUPS, lg, ninf)
    gmax, g_sel = first_max(gl)
    p_g = 1.0 / jnp.sum(jnp.exp(gl - gmax), axis=-1, keepdims=True)
    lo = N_GROUPS + g_sel * EXPERTS_PER_GROUP
    el = jnp.where(jnp.logical_and(lane >= lo, lane < lo + EXPERTS_PER_GROUP), lg, ninf)
    v0, i0 = first_max(el)
    v1, i1 = first_max(jnp.where(lane == i0, ninf, el))
    t = jnp.exp(v1 - v0)
    w0 = p_g / (1.0 + t)
    w1 = p_g * t / (1.0 + t)
    e0 = i0 - N_GROUPS
    e1 = i1 - N_GROUPS
    hit0 = lane == e0
    hit1 = lane == e1
    oh = jnp.where(jnp.logical_or(hit0, hit1), 1.0, 0.0).astype(F32)
    excl = jnp.dot(lower_ref[...], oh.astype(BF16), preferred_element_type=F32) + carry_ref[...]
    r0 = jnp.sum(jnp.where(hit0, excl, 0.0), axis=-1, keepdims=True).astype(jnp.int32)
    r1 = jnp.sum(jnp.where(hit1, excl, 0.0), axis=-1, keepdims=True).astype(jnp.int32)
    carry_ref[...] = carry_ref[...] + jnp.sum(oh, axis=0, keepdims=True)
    zi = jnp.zeros(lg.shape, jnp.int32)
    ri_ref[...] = jnp.where(lane == 0, e0, jnp.where(lane == 1, e1, jnp.where(lane == 2, r0, jnp.where(lane == 3, r1, zi))))
    rw_ref[...] = jnp.where(lane == 0, w0, jnp.where(lane == 1, w1, jnp.zeros(lg.shape, F32)))
    cnt_ref[...] = carry_ref[...]


def _route(lg, tm):
    T = lg.shape[0]
    return pl.pallas_call(
        functools.partial(_route_kernel, tm=tm),
        grid=(T // tm,),
        in_specs=[pl.BlockSpec((tm, 128), lambda i: (i, 0))],
        out_specs=[pl.BlockSpec((tm, 128), lambda i: (i, 0)),
                   pl.BlockSpec((tm, 128), lambda i: (i, 0)),
                   pl.BlockSpec((1, 128), lambda i: (0, 0))],
        out_shape=[jax.ShapeDtypeStruct((T, 128), jnp.int32),
                   jax.ShapeDtypeStruct((T, 128), F32),
                   jax.ShapeDtypeStruct((1, 128), F32)],
        scratch_shapes=[pltpu.VMEM((1, 128), F32), pltpu.VMEM((tm, tm), BF16)],
        compiler_params=_cparams(("arbitrary",)),
        name="route",
    )(lg)


def _sc_gather_rows(table, idx, n_out=None):
    inverse = n_out is not None
    n_src = idx.shape[0]
    n = n_out if inverse else n_src
    n_workers = SC_CORES * SC_SUBCORES
    per_w = n // n_workers
    n_chunks = per_w // SC_CHUNK
    assert per_w * n_workers == n and n_chunks * SC_CHUNK == per_w and n_chunks % 2 == 0
    mesh = plsc.VectorSubcoreMesh(core_axis_name="c", subcore_axis_name="s",
                                  num_cores=SC_CORES, num_subcores=SC_SUBCORES)

    @functools.partial(
        pl.kernel, mesh=mesh,
        out_type=jax.ShapeDtypeStruct((n,) + table.shape[1:], table.dtype),
        scratch_types=[pltpu.VMEM((per_w,), jnp.int32),
                       pltpu.VMEM((SC_CHUNK,) + table.shape[1:], table.dtype),
                       pltpu.VMEM((SC_CHUNK,) + table.shape[1:], table.dtype),
                       pltpu.SemaphoreType.DMA, pltpu.SemaphoreType.DMA,
                       pltpu.VMEM((n_src if inverse else SC_LANES,), jnp.int32)],
        compiler_params=pltpu.CompilerParams(use_tc_tiling_on_sc=True, needs_layout_passes=not inverse),
        name="sc_dispatch_rows" if inverse else "sc_gather_rows",
    )
    def gather(table_hbm, idx_hbm, out_hbm, idx_v, rows_a, rows_b, sem_a, sem_b, map_v):
        wid = lax.axis_index("s") * SC_CORES + lax.axis_index("c")
        base = wid * per_w
        if inverse:
            pltpu.sync_copy(idx_hbm, map_v)
            lanes = lax.iota(jnp.int32, SC_LANES)

            @pl.loop(0, per_w // SC_LANES)
            def _(j):
                idx_v[pl.ds(j * SC_LANES, SC_LANES)] = lax.rem(base + j * SC_LANES + lanes, table.shape[0])

            @pl.loop(0, n_src // SC_LANES)
            def _(a):
                local = map_v[pl.ds(a * SC_LANES, SC_LANES)] - base
                mine = jnp.logical_and(local >= 0, local < per_w)
                plsc.store_scatter(idx_v, [local], lax.div(a * SC_LANES + lanes, TOP_K), mask=mine)
        else:
            pltpu.sync_copy(idx_hbm.at[pl.ds(base, per_w)], idx_v)

        def fetch(chunk, rows_v, sem):
            return pltpu.make_async_copy(table_hbm.at[idx_v.at[pl.ds(chunk * SC_CHUNK, SC_CHUNK)]], rows_v, sem)

        def flush(chunk, rows_v):
            pltpu.sync_copy(rows_v, out_hbm.at[pl.ds(base + chunk * SC_CHUNK, SC_CHUNK)])

        fetch(0, rows_a, sem_a).start()

        @pl.loop(0, n_chunks, step=2)
        def _(c):
            fetch(c + 1, rows_b, sem_b).start()
            fetch(c, rows_a, sem_a).wait()
            flush(c, rows_a)

            @pl.when(c + 2 < n_chunks)
            def _():
                fetch(c + 2, rows_a, sem_a).start()

            fetch(c + 1, rows_b, sem_b).wait()
            flush(c + 1, rows_b)

    return gather(table, idx)


def _rows_from_tiles(ref, first, n_rows, stride=ROW_TILE):
    words = [ref[pl.ds(first + c, n_rows, stride=stride), :] for c in range(ROW_TILE)]
    lo = [pltpu.bitcast(w << 16, F32) for w in words]
    hi = [pltpu.bitcast(w & U32(0xFFFF0000), F32) for w in words]
    return jnp.concatenate(lo + hi, axis=-1)


def _rows_to_tiles(ref, val):
    half = D_MODEL // 2

    def bits(x):
        return pltpu.bitcast(x.astype(BF16).astype(F32), U32)

    for c in range(ROW_TILE):
        lo = bits(val[:, c * 128:(c + 1) * 128])
        hi = bits(val[:, half + c * 128:half + (c + 1) * 128])
        ref[pl.ds(c, val.shape[0], stride=ROW_TILE), :] = hi | (lo >> 16)


def _ffn_kernel(be_ref, nx_ref, par_ref, nvb_ref, x_ref, w1_hbm, w3_hbm, w2_hbm, y_ref,
                w1f_ref, w3f_ref, w2f_ref, w1b_ref, w3b_ref, w2b_ref, sem, *, layer):
    b = pl.program_id(0)
    nvb = nvb_ref[0]
    expert = be_ref[b]
    slot = par_ref[b]
    new_expert = jnp.logical_or(b == 0, expert != be_ref[jnp.maximum(b - 1, 0)])

    def fetch(e, sl):
        return [pltpu.make_async_copy(w_hbm.at[layer, e], wf_ref.at[sl], sem.at[sl])
                for w_hbm, wf_ref in ((w1_hbm, w1f_ref), (w3_hbm, w3f_ref), (w2_hbm, w2f_ref))]

    @pl.when(jnp.logical_and(b == 0, nvb > 0))
    def _():
        for cp in fetch(expert, slot):
            cp.start()

    @pl.when(jnp.logical_and(b < nvb, new_expert))
    def _():
        for cp in fetch(expert, slot):
            cp.wait()

        @pl.when(nx_ref[b] >= 0)
        def _():
            for cp in fetch(nx_ref[b], 1 - slot):
                cp.start()

        w1b_ref[...] = w1f_ref[slot].astype(BF16)
        w3b_ref[...] = w3f_ref[slot].astype(BF16)
        w2b_ref[...] = w2f_ref[slot].astype(BF16)

    @pl.when(b < nvb)
    def _():
        xb = _rows_from_tiles(x_ref, 0, FFN_BM).astype(BF16)
        h1 = jnp.dot(xb, w1b_ref[...], preferred_element_type=F32)
        h3 = jnp.dot(xb, w3b_ref[...], preferred_element_type=F32)
        a = (jax.nn.silu(h1) * h3).astype(BF16)
        _rows_to_tiles(y_ref, jnp.dot(a, w2b_ref[...], preferred_element_type=F32))

    @pl.when(b >= nvb)
    def _():
        y_ref[...] = jnp.zeros(y_ref.shape, U32)


def _expert_ffn(block_e, next_e, parity, nvb, xs, w1, w3, w2, layer):
    nb = block_e.shape[0]

    def used(b, *prefetch):
        return jnp.minimum(b, jnp.maximum(prefetch[-1][0] - 1, 0))

    return pl.pallas_call(
        functools.partial(_ffn_kernel, layer=layer),
        grid_spec=pltpu.PrefetchScalarGridSpec(
            num_scalar_prefetch=4,
            grid=(nb,),
            in_specs=[
                pl.BlockSpec((FFN_BM * ROW_TILE, 128), lambda b, *prefetch: (used(b, *prefetch), 0)),
                pl.BlockSpec(memory_space=pl.ANY),
                pl.BlockSpec(memory_space=pl.ANY),
                pl.BlockSpec(memory_space=pl.ANY),
            ],
            out_specs=pl.BlockSpec((FFN_BM * ROW_TILE, 128), lambda b, *prefetch: (b, 0)),
            scratch_shapes=[pltpu.VMEM((2, D_MODEL, EXPERT_FF), F32), pltpu.VMEM((2, D_MODEL, EXPERT_FF), F32),
                            pltpu.VMEM((2, EXPERT_FF, D_MODEL), F32),
                            pltpu.VMEM((D_MODEL, EXPERT_FF), BF16), pltpu.VMEM((D_MODEL, EXPERT_FF), BF16),
                            pltpu.VMEM((EXPERT_FF, D_MODEL), BF16),
                            pltpu.SemaphoreType.DMA((2,))],
        ),
        out_shape=jax.ShapeDtypeStruct((nb * FFN_BM * ROW_TILE, 128), U32),
        compiler_params=_cparams(("arbitrary",)),
        name="expert_ffn",
    )(block_e, next_e, parity, nvb, xs, w1, w3, w2)


def _comb_kernel(yg_ref, x1_ref, rw_ref, p_ref, gple_ref, wpg_ref, wpp_ref, gout_ref, o_ref, *, tm, final):
    w = rw_ref[...]
    y0 = _rows_from_tiles(yg_ref, 0, tm, stride=TOP_K * ROW_TILE)
    y1 = _rows_from_tiles(yg_ref, ROW_TILE, tm, stride=TOP_K * ROW_TILE)
    x2 = x1_ref[...] + (y0 * w[:, 0:1] + y1 * w[:, 1:2])
    e = jnp.dot(p_ref[...].astype(BF16), wpp_ref[...], preferred_element_type=F32)
    gate = jax.nn.sigmoid(jnp.dot(_rms(x2, gple_ref[...]).astype(BF16), wpg_ref[...], preferred_element_type=F32))
    x3 = x2 + gate * e
    o_ref[...] = _rms(x3, gout_ref[...]) if final else x3


def _combine_ple(yg, x1, rw, p2d, gple, wpg, wpp, gout, tm, layer):
    T = x1.shape[0]
    nt = T // tm
    return pl.pallas_call(
        functools.partial(_comb_kernel, tm=tm, final=layer == DEPTH - 1),
        grid=(nt,),
        in_specs=[
            pl.BlockSpec((tm * TOP_K * ROW_TILE, 128), lambda i: (i, 0)),
            pl.BlockSpec((tm, D_MODEL), lambda i: (i, 0)),
            pl.BlockSpec((tm, 128), lambda i: (i, 0)),
            pl.BlockSpec((tm, PLE_DIM), lambda i: (layer * nt + i, 0)),
            _layer_spec((1, D_MODEL), layer),
            _layer_spec((D_MODEL, D_MODEL), layer),
            _layer_spec((PLE_DIM, D_MODEL), layer),
            pl.BlockSpec((1, D_MODEL), lambda i: (0, 0)),
        ],
        out_specs=pl.BlockSpec((tm, D_MODEL), lambda i: (i, 0)),
        out_shape=jax.ShapeDtypeStruct((T, D_MODEL), F32),
        compiler_params=_cparams(("parallel",)),
        name="combine_ple",
    )(yg, x1, rw, p2d, gple, wpg, wpp, gout)


def _rope_tables(pos, theta, rot_dim, offset, span, scale):
    half = rot_dim // 2
    inv = jnp.float32(theta) ** (-jnp.arange(half, dtype=F32) * 2.0 / rot_dim)
    ang = pos.astype(F32)[:, None] * inv
    rel = np.arange(128) - offset
    rot = (rel >= 0) & (rel < span)
    spread = np.zeros((half, 128), np.float32)
    spread[rel[rot] % half, np.nonzero(rot)[0]] = 1.0
    to_lanes = lambda t: jnp.dot(t, spread, precision=lax.Precision.HIGHEST)
    cos, sin = to_lanes(jnp.cos(ang)) + (~rot).astype(np.float32), to_lanes(jnp.sin(ang))
    return jnp.concatenate([cos, -sin, sin], axis=1) * jnp.float32(scale)


def _prep_w_in(w):
    c_q = w[..., 0:512]
    c_kv = w[..., 512:768]
    k_pe = w[..., 768:800]
    dil = w[..., 800:800 + 2304].reshape(w.shape[:-1] + (DIL_GROUPS, 3, DIL_HEADS, DIL_HEAD_DIM))

    def qk_layout(t):
        flat = lambda u: u.reshape(u.shape[:-2] + (-1,))
        return jnp.concatenate([flat(t[..., :DIL_HALF]), flat(t[..., PARTIAL_ROT:PARTIAL_ROT + DIL_REST]),
                                flat(t[..., DIL_HALF:PARTIAL_ROT]), flat(t[..., PARTIAL_ROT + DIL_REST:])], axis=-1)

    dil = jnp.concatenate([jnp.concatenate([qk_layout(dil[..., g, 0, :, :]), qk_layout(dil[..., g, 1, :, :]),
                                            dil[..., g, 2, :, :].reshape(w.shape[:-1] + (DIL_W,))], axis=-1)
                           for g in range(DIL_GROUPS)], axis=-1)
    gates = w[..., 3104:5152]
    zeros = lambda n: jnp.zeros(w.shape[:-1] + (n,), w.dtype)
    lo, hi = k_pe[..., :MLA_HALF], k_pe[..., MLA_HALF:]
    pad = MLA_SLOT - MLA_NOPE - MLA_ROPE
    kslot = jnp.concatenate([zeros(MLA_NOPE), lo, lo, zeros(pad), zeros(MLA_NOPE), hi, hi, zeros(pad)], axis=-1)
    return jnp.concatenate([gates, c_q, c_kv, kslot, dil], axis=-1).astype(BF16)


def _pad_heads(w, width):
    w = jnp.pad(w, ((0, 0), (0, 0), (0, 0), (0, MLA_SLOT - width)))
    return w.reshape(w.shape[0], w.shape[1], MLA_HEADS * MLA_SLOT).astype(BF16)


def _prep_w_q(w):
    w = w.reshape(DEPTH, MLA_Q_LORA, MLA_HEADS // 2, 2, MLA_NOPE + MLA_ROPE)
    nope = w[..., :MLA_NOPE]
    lo = w[..., MLA_NOPE:MLA_NOPE + MLA_HALF]
    hi = w[..., MLA_NOPE + MLA_HALF:]
    both = lambda t: t.reshape(t.shape[:-2] + (2 * MLA_HALF,))
    zeros = jnp.zeros(w.shape[:3] + (MLA_SLOT - MLA_NOPE - MLA_ROPE,), w.dtype)
    slot0 = jnp.concatenate([nope[..., 0, :], both(lo), zeros], axis=-1)
    slot1 = jnp.concatenate([nope[..., 1, :], both(hi), zeros], axis=-1)
    return jnp.stack([slot0, slot1], axis=-2).reshape(DEPTH, MLA_Q_LORA, MLA_HEADS * MLA_SLOT).astype(BF16)


def _prep_w_kv(w):
    w = w.reshape(DEPTH, MLA_KV_LORA, MLA_HEADS, MLA_NOPE + MLA_V)
    return _pad_heads(w[..., :MLA_NOPE], MLA_NOPE), _pad_heads(w[..., MLA_NOPE:], MLA_V)


def _dest_kernel(ri_ref, ps_ref, o_ref):
    ri = ri_ref[...].astype(F32)
    lane = lax.broadcasted_iota(jnp.int32, ri.shape, 1)
    ps = ps_ref[...]

    def col(k):
        return jnp.sum(jnp.where(lane == k, ri, 0.0), axis=-1, keepdims=True)

    def dest(k):
        start = jnp.sum(jnp.where(lane == col(k).astype(jnp.int32), ps, 0.0), axis=-1, keepdims=True)
        return (start + col(2 + k)).astype(jnp.int32)

    o_ref[...] = jnp.where(lane == 0, dest(0), jnp.where(lane == 1, dest(1), 0))


def _dest_rows(ri, pstarts, tm):
    T = ri.shape[0]
    ps = jnp.zeros((1, 128), F32).at[0, :N_EXPERTS].set(pstarts.astype(F32))
    return pl.pallas_call(
        _dest_kernel,
        grid=(T // tm,),
        in_specs=[pl.BlockSpec((tm, 128), lambda i: (i, 0)), pl.BlockSpec((1, 128), lambda i: (0, 0))],
        out_specs=pl.BlockSpec((tm, 128), lambda i: (i, 0)),
        out_shape=jax.ShapeDtypeStruct((T, 128), jnp.int32),
        compiler_params=_cparams(("parallel",)),
        name="dest_rows",
    )(ri, ps)


def _dispatch_plan(ri, cnt, T, tm):
    counts = cnt[0, :N_EXPERTS].astype(jnp.int32)
    pcounts = (counts + FFN_BM - 1) // FFN_BM * FFN_BM
    pends = jnp.cumsum(pcounts)
    pstarts = pends - pcounts
    dest = _dest_rows(ri, pstarts, tm)[:, 0:TOP_K]
    n_blocks = (T * TOP_K) // FFN_BM + N_EXPERTS
    first_row = jnp.arange(n_blocks, dtype=jnp.int32) * FFN_BM
    block_e = jnp.minimum(jnp.sum((pends[None, :] <= first_row[:, None]).astype(jnp.int32), axis=1), N_EXPERTS - 1)
    nvb = pends[-1] // FFN_BM
    after = pends // FFN_BM
    next_of_expert = jnp.where(after < nvb, block_e[jnp.minimum(after, n_blocks - 1)], -1)
    run_of_expert = jnp.cumsum((pcounts > 0).astype(jnp.int32)) - 1
    ffn_plan = (block_e, next_of_expert[block_e].astype(jnp.int32), (run_of_expert[block_e] % 2).astype(jnp.int32),
                nvb.astype(jnp.int32).reshape(1))
    return dest.reshape(T * TOP_K), n_blocks * FFN_BM, ffn_plan


def kernel(x, p, positions, g_mix, w_in, g_q_lat, w_q_up, g_kv_lat, w_kv_up, w_branch_a, w_branch_b, w_out, g_ffn, w_router_grp, b_router_grp, w_router_exp, b_router_exp, w_exp_gate, w_exp_up, w_exp_down, g_ple, w_ple_gate, w_ple_proj, g_final):
    B, S, D = x.shape
    T = B * S
    TM_IN, TM_QKV, TQ, TM_POST, TM_ROUTE, TM_COMB = 512, 512, 512, 512, 1024, 256

    pos = positions.reshape(T)
    dil_scale = DIL_HEAD_DIM ** -0.5 * LOG2E
    mla_scale = (MLA_NOPE + MLA_ROPE) ** -0.5 * LOG2E
    tab_dil = jnp.concatenate([_rope_tables(pos, ROPE_THETA, PARTIAL_ROT, 0, DIL_ROT_LANES, sc)
                               for sc in (dil_scale, 1.0)], axis=1)
    tab_mla = jnp.concatenate([_rope_tables(pos, MLA_ROPE_THETA, MLA_ROPE, MLA_NOPE, MLA_ROPE, sc)
                               for sc in (mla_scale, 1.0)], axis=1)

    w_in_p = _prep_w_in(w_in)
    wq = _prep_w_q(w_q_up)
    wk, wv = _prep_w_kv(w_kv_up)
    wr = jnp.concatenate([w_router_grp, w_router_exp, jnp.zeros((DEPTH, D, 128 - N_GROUPS - N_EXPERTS), F32)],
                         axis=-1).astype(BF16)
    br = jnp.concatenate([b_router_grp, b_router_exp.reshape(DEPTH, N_EXPERTS),
                          jnp.zeros((DEPTH, 128 - N_GROUPS - N_EXPERTS), F32)], axis=-1).reshape(DEPTH, 1, 128)
    wa, wb, wo = w_branch_a.astype(BF16), w_branch_b.astype(BF16), w_out.astype(BF16)
    wpg, wpp = w_ple_gate.astype(BF16), w_ple_proj.astype(BF16)
    gains = lambda g: g.reshape(DEPTH, 1, -1)

    xc = x.reshape(T, D)
    for i in range(DEPTH):
        main, d0, d1, d2 = _in_proj(xc, gains(g_mix), w_in_p, tab_dil, B, S, TM_IN, i)
        q, k, v = _mla_qkv(main, gains(g_q_lat), gains(g_kv_lat), wq, wk, wv, tab_mla, TM_QKV, i)
        o_a = _mla_attn(q, k, v, B, S, TQ)
        ogs, lss = [], []
        for (_, dil), qkv in zip(DIL_PATTERN, (d0, d1, d2)):
            og, ls = _dil_attn(qkv, B, dil, S // dil)
            ogs.append(og)
            lss.append(ls)
        x1, h2, lg = _post(xc, o_a, ogs, lss, main, wa, wb, wo, gains(g_ffn), wr, br, B, S, TM_POST, i)
        ri, rw, cnt = _route(lg, TM_ROUTE)
        dest, n_rows, ffn_plan = _dispatch_plan(ri, cnt, T, TM_ROUTE)
        xs = _sc_gather_rows(h2.reshape(T, ROW_TILE, 128), dest, n_out=n_rows)
        ys = _expert_ffn(*ffn_plan, xs.reshape(-1, 128), w_exp_gate, w_exp_up, w_exp_down, i)
        yg = _sc_gather_rows(ys.reshape(-1, ROW_TILE, 128), dest)
        xc = _combine_ple(yg.reshape(-1, 128), x1, rw, p.reshape(DEPTH * T, PLE_DIM), gains(g_ple), wpg, wpp,
                          g_final.reshape(1, D), TM_COMB, i)
    return xc.reshape(B, S, D)
```

```python
import functools
import math

import jax
import jax.numpy as jnp
import numpy as np
from jax import lax
from jax.experimental import pallas as pl
from jax.experimental.pallas import tpu as pltpu
from jax.experimental.pallas import tpu_sc as plsc

F32 = jnp.float32
BF16 = jnp.bfloat16

D_MODEL = 1024
DEPTH = 4
RMS_EPS = 1e-6
NEG = -1e30
LOG2E = math.log2(math.e)

MLA_HEADS = 16
MLA_Q_LORA = 512
MLA_KV_LORA = 256
MLA_NOPE = 64
MLA_ROPE = 32
MLA_V = 64
MLA_ROPE_THETA = 10000.0
MLA_SLOT = 128
MLA_HALF = MLA_ROPE // 2


def _mla_head_lanes(lane, head):
    l = lane % MLA_SLOT
    nope = jnp.logical_and(lane // MLA_SLOT == head, l < MLA_NOPE)
    rope = jnp.logical_and(l >= MLA_NOPE + head * MLA_HALF, l < MLA_NOPE + (head + 1) * MLA_HALF)
    return jnp.logical_or(nope, rope)

DIL_PATTERN = ((128, 1), (512, 4), (2048, 16))
DIL_GROUPS = 3
DIL_HEADS = 4
DIL_HEAD_DIM = 64
DIL_W = DIL_HEADS * DIL_HEAD_DIM
DIL_BLK = 128
DIL_UNROLL = 16
ROPE_THETA = 500000.0
PARTIAL_ROT = DIL_HEAD_DIM // 4
DIL_HALF = PARTIAL_ROT // 2
DIL_ROT_LANES = DIL_HEADS * DIL_HALF
DIL_REST = (DIL_HEAD_DIM - PARTIAL_ROT) // 2


def _dil_qk_head_of_lane(lane):
    l = lane % 128
    plain = l - DIL_ROT_LANES
    plain_head = sum((plain >= h * DIL_REST).astype(jnp.int32) for h in range(1, DIL_HEADS))
    return jnp.where(l < DIL_ROT_LANES, l // DIL_HALF, plain_head)

N_GROUPS = 8
EXPERTS_PER_GROUP = 8
N_EXPERTS = 64
TOP_K = 2
EXPERT_FF = 256
PLE_DIM = 256

IN_TN = 768
MAIN_COLS = 3072
IN_COLS_PAD = MAIN_COLS + DIL_GROUPS * 3 * DIL_W
N_MAIN_TILES = MAIN_COLS // IN_TN

FFN_BM = 256
ROW_TILE = D_MODEL // 256
U32 = jnp.uint32
VMEM_LIMIT = 48 * 1024 * 1024
TM_IN, TM_QKV, TQ, TM_POST, TM_ROUTE, TM_COMB = 512, 512, 512, 512, 1024, 512

SC_CORES = 2
SC_SUBCORES = 16
SC_LANES = 16
SC_CHUNK = 64


def _cparams(sem, bounds_checks=True):
    return pltpu.CompilerParams(dimension_semantics=sem, vmem_limit_bytes=VMEM_LIMIT,
                                disable_bounds_checks=not bounds_checks)


def _layer_spec(shape, layer, col=None):
    def index(*grid_idx):
        return (layer,) + (0,) * (len(shape) - 1) + ((col(*grid_idx),) if col else (0,))

    return pl.BlockSpec((None,) + tuple(shape), index)


def _rms(x, g):
    return x * lax.rsqrt(jnp.mean(x * x, axis=-1, keepdims=True) + RMS_EPS) * g


def _in_kernel(x_ref, g_ref, w_ref, tab_ref, main_ref, d0_ref, d1_ref, d2_ref, xn_ref, acc_ref):
    j = pl.program_id(1)

    @pl.when(j == 0)
    def _():
        xn_ref[...] = _rms(x_ref[...], g_ref[...]).astype(BF16)

    acc = jnp.dot(xn_ref[...], w_ref[...], preferred_element_type=F32)

    @pl.when(j < N_MAIN_TILES)
    def _():
        main_ref[...] = acc.astype(BF16)

    def dil_tile(out_ref, dil):
        chunk = lambda c: acc[:, c * 128:(c + 1) * 128]
        for base, t0 in ((0, 0), (2, 3)):
            cos, sin_n, sin_p = [tab_ref[:, (t0 + t) * 128:(t0 + t + 1) * 128] for t in range(3)]
            lo, hi = chunk(base), chunk(base + 1)
            acc_ref[base] = lo * cos + hi * sin_n
            acc_ref[base + 1] = hi * cos + lo * sin_p
        for c in range(4, 6):
            acc_ref[c] = chunk(c)
        rows = acc_ref.shape[1] // dil
        for r in range(dil):
            for c in range(6):
                out_ref[0, r, :, c * 128:(c + 1) * 128] = acc_ref[c, pl.ds(r, rows, stride=dil), :].astype(BF16)

    for gi, (_, dil) in enumerate(DIL_PATTERN):
        pl.when(j == N_MAIN_TILES + gi)(functools.partial(dil_tile, (d0_ref, d1_ref, d2_ref)[gi], dil))


def _in_proj(x2d, g, w, tab, B, S, tm, layer):
    T = x2d.shape[0]
    nt = S // tm
    dil_shapes = [jax.ShapeDtypeStruct((B, d, S // d, 3 * DIL_W), BF16) for _, d in DIL_PATTERN]
    dil_specs = [pl.BlockSpec((1, d, tm // d, 3 * DIL_W), lambda i, j, nt=nt: (i // nt, 0, i % nt, 0))
                 for _, d in DIL_PATTERN]
    return pl.pallas_call(
        _in_kernel,
        grid=(T // tm, IN_COLS_PAD // IN_TN),
        in_specs=[
            pl.BlockSpec((tm, D_MODEL), lambda i, j: (i, 0)),
            _layer_spec((1, D_MODEL), layer),
            _layer_spec((D_MODEL, IN_TN), layer, col=lambda i, j: j),
            pl.BlockSpec((tm, 6 * 128), lambda i, j: (i, 0)),
        ],
        out_specs=[pl.BlockSpec((tm, IN_TN), lambda i, j: (i, jnp.minimum(j, N_MAIN_TILES - 1)))] + dil_specs,
        out_shape=[jax.ShapeDtypeStruct((T, MAIN_COLS), BF16)] + dil_shapes,
        scratch_shapes=[pltpu.VMEM((tm, D_MODEL), BF16), pltpu.VMEM((IN_TN // 128, tm, 128), F32)],
        compiler_params=_cparams(("parallel", "arbitrary")),
        name="in_proj",
    )(x2d, g, w, tab)


def _qkv_kernel(cq_ref, ckv_ref, kpe_ref, gq_ref, gkv_ref, wq_ref, wk_ref, wv_ref, tab_ref,
                q_ref, k_ref, v_ref):
    qn = _rms(cq_ref[...].astype(F32), gq_ref[...]).astype(BF16)
    kvn = _rms(ckv_ref[...].astype(F32), gkv_ref[...]).astype(BF16)
    qacc = jnp.dot(qn, wq_ref[...], preferred_element_type=F32)
    kacc = jnp.dot(kvn, wk_ref[...], preferred_element_type=F32)
    vacc = jnp.dot(kvn, wv_ref[...], preferred_element_type=F32)
    lane_v = lax.broadcasted_iota(jnp.int32, vacc.shape, 1)
    v_ref[...] = jnp.where(lane_v % MLA_SLOT < MLA_V, vacc, 1.0).astype(BF16)
    cq, snq, spq, ck, snk, spk = [tab_ref[:, t * 128:(t + 1) * 128] for t in range(6)]
    kpe_lo = kpe_ref[:, 0:128].astype(F32)
    kpe_hi = kpe_ref[:, 128:256].astype(F32)
    krot = (kpe_lo * ck + kpe_hi * snk, kpe_hi * ck + kpe_lo * spk)
    for pair in range(MLA_HEADS // 2):
        s0 = slice(2 * pair * MLA_SLOT, (2 * pair + 1) * MLA_SLOT)
        s1 = slice((2 * pair + 1) * MLA_SLOT, (2 * pair + 2) * MLA_SLOT)
        q0, q1 = qacc[:, s0], qacc[:, s1]
        q_ref[:, s0] = (q0 * cq + q1 * snq).astype(BF16)
        q_ref[:, s1] = (q1 * cq + q0 * spq).astype(BF16)
        k_ref[:, s0] = (kacc[:, s0] + krot[0]).astype(BF16)
        k_ref[:, s1] = (kacc[:, s1] + krot[1]).astype(BF16)


def _mla_qkv(main, gq, gkv, wq, wk, wv, tab, tm, layer):
    T = main.shape[0]
    return pl.pallas_call(
        _qkv_kernel,
        grid=(T // tm,),
        in_specs=[
            pl.BlockSpec((tm, MLA_Q_LORA), lambda i: (i, 2048 // MLA_Q_LORA)),
            pl.BlockSpec((tm, MLA_KV_LORA), lambda i: (i, 2560 // MLA_KV_LORA)),
            pl.BlockSpec((tm, 256), lambda i: (i, 2816 // 256)),
            _layer_spec((1, MLA_Q_LORA), layer),
            _layer_spec((1, MLA_KV_LORA), layer),
            _layer_spec((MLA_Q_LORA, MLA_HEADS * MLA_SLOT), layer),
            _layer_spec((MLA_KV_LORA, MLA_HEADS * MLA_SLOT), layer),
            _layer_spec((MLA_KV_LORA, MLA_HEADS * MLA_SLOT), layer),
            pl.BlockSpec((tm, 6 * 128), lambda i: (i, 0)),
        ],
        out_specs=[
            pl.BlockSpec((tm, MLA_HEADS * MLA_SLOT), lambda i: (i, 0)),
            pl.BlockSpec((tm, MLA_HEADS * MLA_SLOT), lambda i: (i, 0)),
            pl.BlockSpec((tm, MLA_HEADS * MLA_SLOT), lambda i: (i, 0)),
        ],
        out_shape=[
            jax.ShapeDtypeStruct((T, MLA_HEADS * MLA_SLOT), BF16),
            jax.ShapeDtypeStruct((T, MLA_HEADS * MLA_SLOT), BF16),
            jax.ShapeDtypeStruct((T, MLA_HEADS * MLA_SLOT), BF16),
        ],
        compiler_params=_cparams(("parallel",)),
        name="mla_qkv",
    )(main, main, main, gq, gkv, wq, wk, wv, tab)


def _mla_attn_kernel(qa_ref, qb_ref, k_ref, v_ref, o_ref, m_ref, acc_ref, *, tq):
    p_id = pl.program_id(2)
    nq = k_ref.shape[1] // tq
    lane = lax.broadcasted_iota(jnp.int32, (tq, 2 * MLA_SLOT), 1)
    qh = []
    for q_ref in (qa_ref, qb_ref):
        qp = q_ref[0]
        zero = jnp.zeros_like(qp)
        qh.append([jnp.where(_mla_head_lanes(lane, h), qp, zero) for h in range(2)])
    hq = tq // 2
    dn = (((1,), (1,)), ((), ()))

    def step(blk, start, width, r0, mask, first=False):
        kb = k_ref[0, pl.ds(start, width), :]
        vb = v_ref[0, pl.ds(start, width), :]
        rows = tq - r0
        s_pair = lax.dot_general(jnp.concatenate([qh[blk][0][r0:], qh[blk][1][r0:]], axis=0), kb, dn,
                                 preferred_element_type=F32)
        for h in range(2):
            s = s_pair[h * rows:(h + 1) * rows]
            if mask is not None:
                s = jnp.where(mask, s, NEG)
            m_cur = jnp.max(s, axis=-1, keepdims=True)
            if first:
                m_new = jnp.broadcast_to(m_cur, (tq - r0, 128))
            else:
                m_prev = m_ref[blk, h, r0:, :]
                m_new = jnp.maximum(m_prev, m_cur)
            p = jnp.exp2(s - jnp.concatenate([m_new] * (width // 128), axis=-1))
            pv = jnp.dot(p.astype(BF16), vb[:, h * MLA_SLOT:(h + 1) * MLA_SLOT], preferred_element_type=F32)
            if first:
                acc_ref[blk, h, r0:, :] = pv
            else:
                acc_ref[blk, h, r0:, :] = jnp.exp2(m_prev - m_new) * acc_ref[blk, h, r0:, :] + pv
            m_ref[blk, h, r0:, :] = m_new

    def causal(rows):
        return lax.broadcasted_iota(jnp.int32, (rows, hq), 1) <= lax.broadcasted_iota(jnp.int32, (rows, hq), 0)

    def q_blocks(n_a):
        n_b = nq - 1 - n_a
        for blk, n_full in ((1, n_b), (0, n_a)):
            for j in range(n_full):
                step(blk, j * tq, tq, 0, None, first=j == 0)
            step(blk, n_full * tq, hq, 0, causal(tq), first=n_full == 0)
            step(blk, n_full * tq + hq, hq, hq, causal(hq))

    for n_a in range(nq // 2):
        pl.when(p_id == n_a)(functools.partial(q_blocks, n_a))
    lane_o = lax.broadcasted_iota(jnp.int32, (tq, 2 * MLA_V), 1)
    for blk in range(2):
        outs = []
        for h in range(2):
            a = acc_ref[blk, h]
            outs.append(a / pltpu.roll(a, MLA_V, 1))
        o_ref[0, blk, 0] = jnp.where(lane_o < MLA_V, outs[0], pltpu.roll(outs[1], MLA_V, 1)).astype(BF16)


def _mla_attn(q, k, v, B, S, tq):
    nq = S // tq
    q = q.reshape(B, S, MLA_HEADS * MLA_SLOT)
    k = k.reshape(B, S, MLA_HEADS * MLA_SLOT)
    v = v.reshape(B, S, MLA_HEADS * MLA_SLOT)
    return pl.pallas_call(
        functools.partial(_mla_attn_kernel, tq=tq),
        grid=(B, MLA_HEADS // 2, nq // 2),
        in_specs=[
            pl.BlockSpec((1, tq, 2 * MLA_SLOT), lambda b, h, p: (b, p, h)),
            pl.BlockSpec((1, tq, 2 * MLA_SLOT), lambda b, h, p: (b, nq - 1 - p, h)),
            pl.BlockSpec((1, S, 2 * MLA_SLOT), lambda b, h, p: (b, 0, h)),
            pl.BlockSpec((1, S, 2 * MLA_SLOT), lambda b, h, p: (b, 0, h)),
        ],
        out_specs=pl.BlockSpec((1, 2, 1, tq, 2 * MLA_V), lambda b, h, p: (b, 0, p, 0, h)),
        out_shape=jax.ShapeDtypeStruct((B, 2, nq // 2, tq, MLA_HEADS * MLA_V), BF16),
        scratch_shapes=[pltpu.VMEM((2, 2, tq, 128), F32), pltpu.VMEM((2, 2, tq, MLA_SLOT), F32)],
        compiler_params=_cparams(("parallel", "parallel", "arbitrary")),
        name="mla_attn",
    )(q, q, k, v)


def _dil_kernel(qkv_ref, o_ref, lse_ref, bias_ref, *, nb):
    dil = qkv_ref.shape[1]
    width = 2 * DIL_BLK if nb > 1 else DIL_BLK
    row = lax.broadcasted_iota(jnp.int32, (DIL_BLK, width), 0)
    col = lax.broadcasted_iota(jnp.int32, (DIL_BLK, width), 1)
    bias_ref[0] = jnp.where(col <= row, 0.0, NEG)
    if nb > 1:
        later = jnp.logical_or(jnp.logical_and(col >= DIL_BLK, col - DIL_BLK <= row),
                               jnp.logical_and(col < DIL_BLK, col >= row))
        bias_ref[1] = jnp.where(later, 0.0, NEG)
    lane = lax.broadcasted_iota(jnp.int32, (DIL_BLK, DIL_W), 1)
    head_of_lane = lane // DIL_HEAD_DIM
    qk_head_of_lane = _dil_qk_head_of_lane(lane)
    dn = (((1,), (1,)), ((), ()))

    def by_head(parts):
        out = parts[DIL_HEADS - 1]
        for h in range(DIL_HEADS - 2, -1, -1):
            out = jnp.where(head_of_lane == h, parts[h], out)
        return out

    def unit(u):
        r = u // nb
        n = u % nb
        q0 = pl.multiple_of(n * DIL_BLK, DIL_BLK)
        q = qkv_ref[0, r, pl.ds(q0, DIL_BLK), 0:DIL_W]
        if nb > 1:
            w0 = pl.multiple_of(jnp.maximum(n - 1, 0) * DIL_BLK, DIL_BLK)
            bias = bias_ref[jnp.minimum(n, 1)]
        else:
            w0 = 0
            bias = bias_ref[0]
        kw = qkv_ref[0, r, pl.ds(w0, width), DIL_W:2 * DIL_W]
        vw = qkv_ref[0, r, pl.ds(w0, width), 2 * DIL_W:3 * DIL_W]
        zero = jnp.zeros_like(q)
        qs = jnp.concatenate([jnp.where(qk_head_of_lane == h, q, zero) for h in range(DIL_HEADS)], axis=0)
        s = lax.dot_general(qs, kw, dn, preferred_element_type=F32)
        s = (s.reshape(DIL_HEADS, DIL_BLK, width) + bias[None]).reshape(DIL_HEADS * DIL_BLK, width)
        m = jnp.max(s, axis=-1, keepdims=True)
        e = jnp.exp2(s - m)
        den = jnp.sum(e, axis=-1, keepdims=True)
        pv = jnp.dot(e.astype(BF16), vw, preferred_element_type=F32)
        lse = m + jnp.log2(den)
        blk = lambda t, h: t[h * DIL_BLK:(h + 1) * DIL_BLK]
        o = by_head([blk(pv, h) for h in range(DIL_HEADS)]) / by_head(
            [jnp.broadcast_to(blk(den, h), (DIL_BLK, DIL_W)) for h in range(DIL_HEADS)])
        o_ref[0, r, pl.ds(q0, DIL_BLK), :] = o.astype(BF16)
        lse_ref[0, r, pl.ds(q0, DIL_BLK), :] = by_head(
            [jnp.broadcast_to(blk(lse, h), (DIL_BLK, DIL_W)) for h in range(DIL_HEADS)])

    def body(t, carry):
        for u in range(DIL_UNROLL):
            unit(DIL_UNROLL * t + u)
        return carry

    lax.fori_loop(0, dil * nb // DIL_UNROLL, body, 0)


def _dil_attn(qkv, B, dil, L):
    nb = L // DIL_BLK
    return pl.pallas_call(
        functools.partial(_dil_kernel, nb=nb),
        grid=(B,),
        in_specs=[pl.BlockSpec((1, dil, L, 3 * DIL_W), lambda b: (b, 0, 0, 0))],
        out_specs=[
            pl.BlockSpec((1, dil, L, DIL_W), lambda b: (b, 0, 0, 0)),
            pl.BlockSpec((1, dil, L, DIL_W), lambda b: (b, 0, 0, 0)),
        ],
        out_shape=[
            jax.ShapeDtypeStruct((B, dil, L, DIL_W), BF16),
            jax.ShapeDtypeStruct((B, dil, L, DIL_W), F32),
        ],
        scratch_shapes=[pltpu.VMEM((2, DIL_BLK, 2 * DIL_BLK if nb > 1 else DIL_BLK), F32)],
        compiler_params=_cparams(("parallel",)),
        name=f"dil_attn_d{dil}",
    )(qkv)


def _post_kernel(x_ref, oa_ref, og0_ref, og1_ref, og2_ref, ls0_ref, ls1_ref, ls2_ref, gates_ref,
                 wa_ref, wb_ref, wo_ref, gffn_ref, wr_ref, br_ref,
                 x1_ref, h2_ref, lg_ref, ob_ref, *, tm):
    ls =[r[0] for r in (ls0_ref, ls1_ref, ls2_ref)]
    og = [r[0] for r in (og0_ref, og1_ref, og2_ref)]
    for gi, (_, dil) in enumerate(DIL_PATTERN):
        rows = tm // dil
        for r in range(dil):
            for c in range(2):
                sl = slice(c * 128, (c + 1) * 128)
                ob_ref[2 * gi + c, pl.ds(r, rows, stride=dil), :] = og[gi][r][:, sl].astype(F32)
                ob_ref[6 + 2 * gi + c, pl.ds(r, rows, stride=dil), :] = ls[gi][r][:, sl]

    def tok_major(k):
        return jnp.concatenate([ob_ref[2 * k], ob_ref[2 * k + 1]], axis=-1)

    l0, l1, l2 = tok_major(3), tok_major(4), tok_major(5)
    mx = jnp.maximum(jnp.maximum(l0, l1), l2)
    w0, w1, w2 = jnp.exp2(l0 - mx), jnp.exp2(l1 - mx), jnp.exp2(l2 - mx)
    ob = (w0 * tok_major(0) + w1 * tok_major(1) + w2 * tok_major(2)) / (w0 + w1 + w2)
    ya = jnp.dot(oa_ref[0, 0, 0], wa_ref[...], preferred_element_type=F32)
    yb = jnp.dot(ob.astype(BF16), wb_ref[...], preferred_element_type=F32)
    merged = (jax.nn.sigmoid(gates_ref[:, 0:D_MODEL].astype(F32)) * ya
              + jax.nn.sigmoid(gates_ref[:, D_MODEL:2 * D_MODEL].astype(F32)) * yb)
    x1 = x_ref[...] + jnp.dot(merged.astype(BF16), wo_ref[...], preferred_element_type=F32)
    x1_ref[...] = x1
    h2 = _rms(x1, gffn_ref[...])
    _rows_to_tiles(h2_ref, h2)
    lg_ref[...] = jnp.dot(h2.astype(BF16), wr_ref[...], preferred_element_type=F32) + br_ref[...]


def _post(x2d, oa, ogs, lss, main, wa, wb, wo, gffn, wr, br, B, S, tm, layer):
    T = x2d.shape[0]
    nt = S // tm
    res_specs = [pl.BlockSpec((1, d, tm // d, DIL_W), lambda i, nt=nt: (i // nt, 0, i % nt, 0))
                 for _, d in DIL_PATTERN]
    assert oa.shape[3] == tm and oa.shape[2] * 2 == nt

    def oa_index(i):
        it = i % nt
        late = it >= nt // 2
        return (i // nt, late.astype(jnp.int32), jnp.where(late, nt - 1 - it, it), 0, 0)

    return pl.pallas_call(
        functools.partial(_post_kernel, tm=tm),
        grid=(T // tm,),
        in_specs=[pl.BlockSpec((tm, D_MODEL), lambda i: (i, 0)),
                  pl.BlockSpec((1, 1, 1, tm, D_MODEL), oa_index)]
                 + res_specs + res_specs
                 + [pl.BlockSpec((tm, 2 * D_MODEL), lambda i: (i, 0)),
                    _layer_spec((D_MODEL, D_MODEL), layer),
                    _layer_spec((DIL_W, D_MODEL), layer),
                    _layer_spec((D_MODEL, D_MODEL), layer),
                    _layer_spec((1, D_MODEL), layer),
                    _layer_spec((D_MODEL, 128), layer),
                    _layer_spec((1, 128), layer)],
        out_specs=[pl.BlockSpec((tm, D_MODEL), lambda i: (i, 0)),
                   pl.BlockSpec((tm * ROW_TILE, 128), lambda i: (i, 0)),
                   pl.BlockSpec((tm, 128), lambda i: (i, 0))],
        out_shape=[jax.ShapeDtypeStruct((T, D_MODEL), F32),
                   jax.ShapeDtypeStruct((T * ROW_TILE, 128), U32),
                   jax.ShapeDtypeStruct((T, 128), F32)],
        scratch_shapes=[pltpu.VMEM((12, tm, 128), F32)],
        compiler_params=_cparams(("parallel",)),
        name="post_attn",
    )(x2d, oa, *ogs, *lss, main, wa, wb, wo, gffn, wr, br)


def _route_kernel(lg_ref, ri_ref, rw_ref, cnt_ref, carry_ref, lower_ref, *, tm):
    i = pl.program_id(0)

    @pl.when(i == 0)
    def _():
        carry_ref[...] = jnp.zeros(carry_ref.shape, F32)
        r_i = lax.broadcasted_iota(jnp.int32, (tm, tm), 0)
        c_i = lax.broadcasted_iota(jnp.int32, (tm, tm), 1)
        lower_ref[...] = jnp.where(c_i < r_i, 1.0, 0.0).astype(BF16)

    lg = lg_ref[...]
    lane = lax.broadcasted_iota(jnp.int32, lg.shape, 1)
    lane_f = lane.astype(F32)
    ninf = jnp.float32(-jnp.inf)

    def first_max(vals):
        vmax = jnp.max(vals, axis=-1, keepdims=True)
        idx = jnp.min(jnp.where(vals == vmax, lane_f, 128.0), axis=-1, keepdims=True)
        return vmax, idx.astype(jnp.int32)

    gl = jnp.where(lane < N_GROUPS, lg, ninf)
    gmax, g_sel = first_max(gl)
    p_g = 1.0 / jnp.sum(jnp.exp(gl - gmax), axis=-1, keepdims=True)
    lo = N_GROUPS + g_sel * EXPERTS_PER_GROUP
    el = jnp.where(jnp.logical_and(lane >= lo, lane < lo + EXPERTS_PER_GROUP), lg, ninf)
    v0, i0 = first_max(el)
    v1, i1 = first_max(jnp.where(lane == i0, ninf, el))
    t = jnp.exp(v1 - v0)
    w0 = p_g / (1.0 + t)
    w1 = p_g * t / (1.0 + t)
    e0 = i0 - N_GROUPS
    e1 = i1 - N_GROUPS
    hit0 = lane == e0
    hit1 = lane == e1
    oh = jnp.where(jnp.logical_or(hit0, hit1), 1.0, 0.0).astype(F32)
    excl = jnp.dot(lower_ref[...], oh.astype(BF16), preferred_element_type=F32) + carry_ref[...]
    r0 = jnp.sum(jnp.where(hit0, excl, 0.0), axis=-1, keepdims=True).astype(jnp.int32)
    r1 = jnp.sum(jnp.where(hit1, excl, 0.0), axis=-1, keepdims=True).astype(jnp.int32)
    carry_ref[...] = carry_ref[...] + jnp.sum(oh, axis=0, keepdims=True)
    zi = jnp.zeros(lg.shape, jnp.int32)
    ri_ref[...] = jnp.where(lane == 0, e0, jnp.where(lane == 1, e1, jnp.where(lane == 2, r0, jnp.where(lane == 3, r1, zi))))
    rw_ref[...] = jnp.where(lane == 0, w0, jnp.where(lane == 1, w1, jnp.zeros(lg.shape, F32)))
    cnt_ref[...] = carry_ref[...]


def _route(lg, tm):
    T = lg.shape[0]
    return pl.pallas_call(
        functools.partial(_route_kernel, tm=tm),
        grid=(T // tm,),
        in_specs=[pl.BlockSpec((tm, 128), lambda i: (i, 0))],
        out_specs=[pl.BlockSpec((tm, 128), lambda i: (i, 0)),
                   pl.BlockSpec((tm, 128), lambda i: (i, 0)),
                   pl.BlockSpec((1, 128), lambda i: (0, 0))],
        out_shape=[jax.ShapeDtypeStruct((T, 128), jnp.int32),
                   jax.ShapeDtypeStruct((T, 128), F32),
                   jax.ShapeDtypeStruct((1, 128), F32)],
        scratch_shapes=[pltpu.VMEM((1, 128), F32), pltpu.VMEM((tm, tm), BF16)],
        compiler_params=_cparams(("arbitrary",)),
        name="route",
    )(lg)


def _sc_gather_rows(table, idx, n_out=None):
    inverse = n_out is not None
    n_src = idx.shape[0]
    n = n_out if inverse else n_src
    n_workers = SC_CORES * SC_SUBCORES
    per_w = n // n_workers
    n_chunks = per_w // SC_CHUNK
    assert per_w * n_workers == n and n_chunks * SC_CHUNK == per_w and n_chunks % 2 == 0
    mesh = plsc.VectorSubcoreMesh(core_axis_name="c", subcore_axis_name="s",
                                  num_cores=SC_CORES, num_subcores=SC_SUBCORES)

    @functools.partial(
        pl.kernel, mesh=mesh,
        out_type=jax.ShapeDtypeStruct((n,) + table.shape[1:], table.dtype),
        scratch_types=[pltpu.VMEM((per_w,), jnp.int32),
                       pltpu.VMEM((SC_CHUNK,) + table.shape[1:], table.dtype),
                       pltpu.VMEM((SC_CHUNK,) + table.shape[1:], table.dtype),
                       pltpu.SemaphoreType.DMA, pltpu.SemaphoreType.DMA,
                       pltpu.VMEM((n_src if inverse else SC_LANES,), jnp.int32)],
        compiler_params=pltpu.CompilerParams(use_tc_tiling_on_sc=True, needs_layout_passes=not inverse),
        name="sc_dispatch_rows" if inverse else "sc_gather_rows",
    )
    def gather(table_hbm, idx_hbm, out_hbm, idx_v, rows_a, rows_b, sem_a, sem_b, map_v):
        wid = lax.axis_index("s") * SC_CORES + lax.axis_index("c")
        base = wid * per_w
        if inverse:
            pltpu.sync_copy(idx_hbm, map_v)
            lanes = lax.iota(jnp.int32, SC_LANES)

            @pl.loop(0, per_w // SC_LANES)
            def _(j):
                idx_v[pl.ds(j * SC_LANES, SC_LANES)] = lax.rem(base + j * SC_LANES + lanes, table.shape[0])

            @pl.loop(0, n_src // SC_LANES)
            def _(a):
                local = map_v[pl.ds(a * SC_LANES, SC_LANES)] - base
                mine = jnp.logical_and(local >= 0, local < per_w)
                plsc.store_scatter(idx_v, [local], lax.div(a * SC_LANES + lanes, TOP_K), mask=mine)
        else:
            pltpu.sync_copy(idx_hbm.at[pl.ds(base, per_w)], idx_v)

        def fetch(chunk, rows_v, sem):
            return pltpu.make_async_copy(table_hbm.at[idx_v.at[pl.ds(chunk * SC_CHUNK, SC_CHUNK)]], rows_v, sem)

        def flush(chunk, rows_v):
            pltpu.sync_copy(rows_v, out_hbm.at[pl.ds(base + chunk * SC_CHUNK, SC_CHUNK)])

        fetch(0, rows_a, sem_a).start()

        @pl.loop(0, n_chunks, step=2)
        def _(c):
            fetch(c + 1, rows_b, sem_b).start()
            fetch(c, rows_a, sem_a).wait()
            flush(c, rows_a)

            @pl.when(c + 2 < n_chunks)
            def _():
                fetch(c + 2, rows_a, sem_a).start()

            fetch(c + 1, rows_b, sem_b).wait()
            flush(c + 1, rows_b)

    return gather(table, idx)


def _rows_from_tiles(ref, first, n_rows, stride=ROW_TILE):
    words = [ref[pl.ds(first + c, n_rows, stride=stride), :] for c in range(ROW_TILE)]
    lo = [pltpu.bitcast(w << 16, F32) for w in words]
    hi = [pltpu.bitcast(w & U32(0xFFFF0000), F32) for w in words]
    return jnp.concatenate(lo + hi, axis=-1)


def _rows_to_tiles(ref, val):
    half = D_MODEL // 2

    def bits(x):
        return pltpu.bitcast(x.astype(BF16).astype(F32), U32)

    for c in range(ROW_TILE):
        lo = bits(val[:, c * 128:(c + 1) * 128])
        hi = bits(val[:, half + c * 128:half + (c + 1) * 128])
        ref[pl.ds(c, val.shape[0], stride=ROW_TILE), :] = hi | (lo >> 16)


def _ffn_kernel(be_ref, nx_ref, par_ref, nvb_ref, x_ref, w1_hbm, w3_hbm, w2_hbm, y_ref,
                w1f_ref, w3f_ref, w2f_ref, w1b_ref, w3b_ref, w2b_ref, sem, *, layer):
    b = pl.program_id(0)
    nvb = nvb_ref[0]
    expert = be_ref[b]
    slot = par_ref[b]
    new_expert = jnp.logical_or(b == 0, expert != be_ref[jnp.maximum(b - 1, 0)])

    def fetch(e, sl):
        return [pltpu.make_async_copy(w_hbm.at[layer, e], wf_ref.at[sl], sem.at[sl])
                for w_hbm, wf_ref in ((w1_hbm, w1f_ref), (w3_hbm, w3f_ref), (w2_hbm, w2f_ref))]

    @pl.when(jnp.logical_and(b == 0, nvb > 0))
    def _():
        for cp in fetch(expert, slot):
            cp.start()

    @pl.when(jnp.logical_and(b < nvb, new_expert))
    def _():
        for cp in fetch(expert, slot):
            cp.wait()

        @pl.when(nx_ref[b] >= 0)
        def _():
            for cp in fetch(nx_ref[b], 1 - slot):
                cp.start()

        w1b_ref[...] = w1f_ref[slot].astype(BF16)
        w3b_ref[...] = w3f_ref[slot].astype(BF16)
        w2b_ref[...] = w2f_ref[slot].astype(BF16)

    @pl.when(b < nvb)
    def _():
        xb = _rows_from_tiles(x_ref, 0, FFN_BM).astype(BF16)
        h1 = jnp.dot(xb, w1b_ref[...], preferred_element_type=F32)
        h3 = jnp.dot(xb, w3b_ref[...], preferred_element_type=F32)
        a = (jax.nn.silu(h1) * h3).astype(BF16)
        _rows_to_tiles(y_ref, jnp.dot(a, w2b_ref[...], preferred_element_type=F32))

    @pl.when(b >= nvb)
    def _():
        y_ref[...] = jnp.zeros(y_ref.shape, U32)


def _expert_ffn(block_e, next_e, parity, nvb, xs, w1, w3, w2, layer):
    nb = block_e.shape[0]

    def used(b, *prefetch):
        return jnp.minimum(b, jnp.maximum(prefetch[-1][0] - 1, 0))

    return pl.pallas_call(
        functools.partial(_ffn_kernel, layer=layer),
        grid_spec=pltpu.PrefetchScalarGridSpec(
            num_scalar_prefetch=4,
            grid=(nb,),
            in_specs=[
                pl.BlockSpec((FFN_BM * ROW_TILE, 128), lambda b, *prefetch: (used(b, *prefetch), 0)),
                pl.BlockSpec(memory_space=pl.ANY),
                pl.BlockSpec(memory_space=pl.ANY),
                pl.BlockSpec(memory_space=pl.ANY),
            ],
            out_specs=pl.BlockSpec((FFN_BM * ROW_TILE, 128), lambda b, *prefetch: (b, 0)),
            scratch_shapes=[pltpu.VMEM((2, D_MODEL, EXPERT_FF), F32), pltpu.VMEM((2, D_MODEL, EXPERT_FF), F32),
                            pltpu.VMEM((2, EXPERT_FF, D_MODEL), F32),
                            pltpu.VMEM((D_MODEL, EXPERT_FF), BF16), pltpu.VMEM((D_MODEL, EXPERT_FF), BF16),
                            pltpu.VMEM((EXPERT_FF, D_MODEL), BF16),
                            pltpu.SemaphoreType.DMA((2,))],
        ),
        out_shape=jax.ShapeDtypeStruct((nb * FFN_BM * ROW_TILE, 128), U32),
        compiler_params=_cparams(("arbitrary",)),
        name="expert_ffn",
    )(block_e, next_e, parity, nvb, xs, w1, w3, w2)


def _comb_kernel(yg_ref, x1_ref, rw_ref, p_ref, gple_ref, wpg_ref, wpp_ref, gout_ref, o_ref, *, tm, final):
    w = rw_ref[...]
    y0 = _rows_from_tiles(yg_ref, 0, tm, stride=TOP_K * ROW_TILE)
    y1 = _rows_from_tiles(yg_ref, ROW_TILE, tm, stride=TOP_K * ROW_TILE)
    x2 = x1_ref[...] + (y0 * w[:, 0:1] + y1 * w[:, 1:2])
    e = jnp.dot(p_ref[...].astype(BF16), wpp_ref[...], preferred_element_type=F32)
    gate = jax.nn.sigmoid(jnp.dot(_rms(x2, gple_ref[...]).astype(BF16), wpg_ref[...], preferred_element_type=F32))
    x3 = x2 + gate * e
    o_ref[...] = _rms(x3, gout_ref[...]) if final else x3


def _combine_ple(yg, x1, rw, p2d, gple, wpg, wpp, gout, tm, layer):
    T = x1.shape[0]
    nt = T // tm
    return pl.pallas_call(
        functools.partial(_comb_kernel, tm=tm, final=layer == DEPTH - 1),
        grid=(nt,),
        in_specs=[
            pl.BlockSpec((tm * TOP_K * ROW_TILE, 128), lambda i: (i, 0)),
            pl.BlockSpec((tm, D_MODEL), lambda i: (i, 0)),
            pl.BlockSpec((tm, 128), lambda i: (i, 0)),
            pl.BlockSpec((tm, PLE_DIM), lambda i: (layer * nt + i, 0)),
            _layer_spec((1, D_MODEL), layer),
            _layer_spec((D_MODEL, D_MODEL), layer),
            _layer_spec((PLE_DIM, D_MODEL), layer),
            pl.BlockSpec((1, D_MODEL), lambda i: (0, 0)),
        ],
        out_specs=pl.BlockSpec((tm, D_MODEL), lambda i: (i, 0)),
        out_shape=jax.ShapeDtypeStruct((T, D_MODEL), F32),
        compiler_params=_cparams(("parallel",)),
        name="combine_ple",
    )(yg, x1, rw, p2d, gple, wpg, wpp, gout)


def _rope_tables(pos, theta, rot_dim, offset, span, scale):
    half = rot_dim // 2
    inv = jnp.float32(theta) ** (-jnp.arange(half, dtype=F32) * 2.0 / rot_dim)
    ang = pos.astype(F32)[:, None] * inv
    rel = np.arange(128) - offset
    rot = (rel >= 0) & (rel < span)
    spread = np.zeros((half, 128), np.float32)
    spread[rel[rot] % half, np.nonzero(rot)[0]] = 1.0
    to_lanes = lambda t: jnp.dot(t, spread, precision=lax.Precision.HIGHEST)
    cos, sin = to_lanes(jnp.cos(ang)) + (~rot).astype(np.float32), to_lanes(jnp.sin(ang))
    return jnp.concatenate([cos, -sin, sin], axis=1) * jnp.float32(scale)


def _prep_w_in(w):
    c_q = w[..., 0:512]
    c_kv = w[..., 512:768]
    k_pe = w[..., 768:800]
    dil = w[..., 800:800 + 2304].reshape(w.shape[:-1] + (DIL_GROUPS, 3, DIL_HEADS, DIL_HEAD_DIM))

    def qk_layout(t):
        flat = lambda u: u.reshape(u.shape[:-2] + (-1,))
        return jnp.concatenate([flat(t[..., :DIL_HALF]), flat(t[..., PARTIAL_ROT:PARTIAL_ROT + DIL_REST]),
                                flat(t[..., DIL_HALF:PARTIAL_ROT]), flat(t[..., PARTIAL_ROT + DIL_REST:])], axis=-1)

    dil = jnp.concatenate([jnp.concatenate([qk_layout(dil[..., g, 0, :, :]), qk_layout(dil[..., g, 1, :, :]),
                                            dil[..., g, 2, :, :].reshape(w.shape[:-1] + (DIL_W,))], axis=-1)
                           for g in range(DIL_GROUPS)], axis=-1)
    gates = w[..., 3104:5152]
    zeros = lambda n: jnp.zeros(w.shape[:-1] + (n,), w.dtype)
    lo, hi = k_pe[..., :MLA_HALF], k_pe[..., MLA_HALF:]
    pad = MLA_SLOT - MLA_NOPE - MLA_ROPE
    kslot = jnp.concatenate([zeros(MLA_NOPE), lo, lo, zeros(pad), zeros(MLA_NOPE), hi, hi, zeros(pad)], axis=-1)
    return jnp.concatenate([gates, c_q, c_kv, kslot, dil], axis=-1).astype(BF16)


def _pad_heads(w, width):
    w = jnp.pad(w, ((0, 0), (0, 0), (0, 0), (0, MLA_SLOT - width)))
    return w.reshape(w.shape[0], w.shape[1], MLA_HEADS * MLA_SLOT).astype(BF16)


def _prep_w_q(w):
    w = w.reshape(DEPTH, MLA_Q_LORA, MLA_HEADS // 2, 2, MLA_NOPE + MLA_ROPE)
    nope = w[..., :MLA_NOPE]
    lo = w[..., MLA_NOPE:MLA_NOPE + MLA_HALF]
    hi = w[..., MLA_NOPE + MLA_HALF:]
    both = lambda t: t.reshape(t.shape[:-2] + (2 * MLA_HALF,))
    zeros = jnp.zeros(w.shape[:3] + (MLA_SLOT - MLA_NOPE - MLA_ROPE,), w.dtype)
    slot0 = jnp.concatenate([nope[..., 0, :], both(lo), zeros], axis=-1)
    slot1 = jnp.concatenate([nope[..., 1, :], both(hi), zeros], axis=-1)
    return jnp.stack([slot0, slot1], axis=-2).reshape(DEPTH, MLA_Q_LORA, MLA_HEADS * MLA_SLOT).astype(BF16)


def _prep_w_kv(w):
    w = w.reshape(DEPTH, MLA_KV_LORA, MLA_HEADS, MLA_NOPE + MLA_V)
    return _pad_heads(w[..., :MLA_NOPE], MLA_NOPE), _pad_heads(w[..., MLA_NOPE:], MLA_V)


def _dest_kernel(ri_ref, ps_ref, o_ref):
    ri = ri_ref[...].astype(F32)
    lane = lax.broadcasted_iota(jnp.int32, ri.shape, 1)
    ps = ps_ref[...]

    def col(k):
        return jnp.sum(jnp.where(lane == k, ri, 0.0), axis=-1, keepdims=True)

    def dest(k):
        start = jnp.sum(jnp.where(lane == col(k).astype(jnp.int32), ps, 0.0), axis=-1, keepdims=True)
        return (start + col(2 + k)).astype(jnp.int32)

    o_ref[...] = jnp.where(lane == 0, dest(0), jnp.where(lane == 1, dest(1), 0))


def _dest_rows(ri, pstarts, tm):
    T = ri.shape[0]
    ps = jnp.zeros((1, 128), F32).at[0, :N_EXPERTS].set(pstarts.astype(F32))
    return pl.pallas_call(
        _dest_kernel,
        grid=(T // tm,),
        in_specs=[pl.BlockSpec((tm, 128), lambda i: (i, 0)), pl.BlockSpec((1, 128), lambda i: (0, 0))],
        out_specs=pl.BlockSpec((tm, 128), lambda i: (i, 0)),
        out_shape=jax.ShapeDtypeStruct((T, 128), jnp.int32),
        compiler_params=_cparams(("parallel",)),
        name="dest_rows",
    )(ri, ps)


def _dispatch_plan(ri, cnt, T, tm):
    counts = cnt[0, :N_EXPERTS].astype(jnp.int32)
    pcounts = (counts + FFN_BM - 1) // FFN_BM * FFN_BM
    pends = jnp.cumsum(pcounts)
    pstarts = pends - pcounts
    dest = _dest_rows(ri, pstarts, tm)[:, 0:TOP_K]
    n_blocks = (T * TOP_K) // FFN_BM + N_EXPERTS
    first_row = jnp.arange(n_blocks, dtype=jnp.int32) * FFN_BM
    block_e = jnp.minimum(jnp.sum((pends[None, :] <= first_row[:, None]).astype(jnp.int32), axis=1), N_EXPERTS - 1)
    nvb = pends[-1] // FFN_BM
    after = pends // FFN_BM
    next_of_expert = jnp.where(after < nvb, block_e[jnp.minimum(after, n_blocks - 1)], -1)
    run_of_expert = jnp.cumsum((pcounts > 0).astype(jnp.int32)) - 1
    ffn_plan = (block_e, next_of_expert[block_e].astype(jnp.int32), (run_of_expert[block_e] % 2).astype(jnp.int32),
                nvb.astype(jnp.int32).reshape(1))
    return dest.reshape(T * TOP_K), n_blocks * FFN_BM, ffn_plan


def kernel(x, p, positions, g_mix, w_in, g_q_lat, w_q_up, g_kv_lat, w_kv_up, w_branch_a, w_branch_b, w_out, g_ffn, w_router_grp, b_router_grp, w_router_exp, b_router_exp, w_exp_gate, w_exp_up, w_exp_down, g_ple, w_ple_gate, w_ple_proj, g_final):
    B, S, D = x.shape
    T = B * S
    pos = positions.reshape(T)
    dil_scale = DIL_HEAD_DIM ** -0.5 * LOG2E
    mla_scale = (MLA_NOPE + MLA_ROPE) ** -0.5 * LOG2E
    tab_dil = jnp.concatenate([_rope_tables(pos, ROPE_THETA, PARTIAL_ROT, 0, DIL_ROT_LANES, sc)
                               for sc in (dil_scale, 1.0)], axis=1)
    tab_mla = jnp.concatenate([_rope_tables(pos, MLA_ROPE_THETA, MLA_ROPE, MLA_NOPE, MLA_ROPE, sc)
                               for sc in (mla_scale, 1.0)], axis=1)

    w_in_p = _prep_w_in(w_in)
    wq = _prep_w_q(w_q_up)
    wk, wv = _prep_w_kv(w_kv_up)
    wr = jnp.concatenate([w_router_grp, w_router_exp, jnp.zeros((DEPTH, D, 128 - N_GROUPS - N_EXPERTS), F32)],
                         axis=-1).astype(BF16)
    br = jnp.concatenate([b_router_grp, b_router_exp.reshape(DEPTH, N_EXPERTS),
                          jnp.zeros((DEPTH, 128 - N_GROUPS - N_EXPERTS), F32)], axis=-1).reshape(DEPTH, 1, 128)
    wa, wb, wo = w_branch_a.astype(BF16), w_branch_b.astype(BF16), w_out.astype(BF16)
    wpg, wpp = w_ple_gate.astype(BF16), w_ple_proj.astype(BF16)
    gains = lambda g: g.reshape(DEPTH, 1, -1)

    xc = x.reshape(T, D)
    for i in range(DEPTH):
        main, d0, d1, d2 = _in_proj(xc, gains(g_mix), w_in_p, tab_dil, B, S, TM_IN, i)
        q, k, v = _mla_qkv(main, gains(g_q_lat), gains(g_kv_lat), wq, wk, wv, tab_mla, TM_QKV, i)
        o_a = _mla_attn(q, k, v, B, S, TQ)
        ogs, lss = [], []
        for (_, dil), qkv in zip(DIL_PATTERN, (d0, d1, d2)):
            og, ls = _dil_attn(qkv, B, dil, S // dil)
            ogs.append(og)
            lss.append(ls)
        x1, h2, lg = _post(xc, o_a, ogs, lss, main, wa, wb, wo, gains(g_ffn), wr, br, B, S, TM_POST, i)
        ri, rw, cnt = _route(lg, TM_ROUTE)
        dest, n_rows, ffn_plan = _dispatch_plan(ri, cnt, T, TM_ROUTE)
        xs = _sc_gather_rows(h2.reshape(T, ROW_TILE, 128), dest, n_out=n_rows)
        ys = _expert_ffn(*ffn_plan, xs.reshape(-1, 128), w_exp_gate, w_exp_up, w_exp_down, i)
        yg = _sc_gather_rows(ys.reshape(-1, ROW_TILE, 128), dest)
        xc = _combine_ple(yg.reshape(-1, 128), x1, rw, p.reshape(DEPTH * T, PLE_DIM), gains(g_ple), wpg, wpp,
                          g_final.reshape(1, D), TM_COMB, i)
    return xc.reshape(B, S, D)
```

```python
import functools
import math

import jax
import jax.numpy as jnp
import numpy as np
from jax import lax
from jax.experimental import pallas as pl
from jax.experimental.pallas import tpu as pltpu
from jax.experimental.pallas import tpu_sc as plsc

F32 = jnp.float32
BF16 = jnp.bfloat16

D_MODEL = 1024
DEPTH = 4
RMS_EPS = 1e-6
NEG = -1e30
LOG2E = math.log2(math.e)

MLA_HEADS = 16
MLA_Q_LORA = 512
MLA_KV_LORA = 256
MLA_NOPE = 64
MLA_ROPE = 32
MLA_V = 64
MLA_ROPE_THETA = 10000.0
MLA_SLOT = 128
MLA_HALF = MLA_ROPE // 2


def _mla_head_lanes(lane, head):
    l = lane % MLA_SLOT
    nope = jnp.logical_and(lane // MLA_SLOT == head, l < MLA_NOPE)
    rope = jnp.logical_and(l >= MLA_NOPE + head * MLA_HALF, l < MLA_NOPE + (head + 1) * MLA_HALF)
    return jnp.logical_or(nope, rope)

DIL_PATTERN = ((128, 1), (512, 4), (2048, 16))
DIL_GROUPS = 3
DIL_HEADS = 4
DIL_HEAD_DIM = 64
DIL_W = DIL_HEADS * DIL_HEAD_DIM
DIL_BLK = 128
DIL_UNROLL = 16
ROPE_THETA = 500000.0
PARTIAL_ROT = DIL_HEAD_DIM // 4
DIL_HALF = PARTIAL_ROT // 2
DIL_ROT_LANES = DIL_HEADS * DIL_HALF
DIL_REST = (DIL_HEAD_DIM - PARTIAL_ROT) // 2


def _dil_qk_head_of_lane(lane):
    l = lane % 128
    plain = l - DIL_ROT_LANES
    plain_head = sum((plain >= h * DIL_REST).astype(jnp.int32) for h in range(1, DIL_HEADS))
    return jnp.where(l < DIL_ROT_LANES, l // DIL_HALF, plain_head)

N_GROUPS = 8
EXPERTS_PER_GROUP = 8
N_EXPERTS = 64
TOP_K = 2
EXPERT_FF = 256
PLE_DIM = 256

IN_TN = 768
MAIN_COLS = 3072
IN_COLS_PAD = MAIN_COLS + DIL_GROUPS * 3 * DIL_W
N_MAIN_TILES = MAIN_COLS // IN_TN

FFN_BM = 256
ROW_TILE = D_MODEL // 256
U32 = jnp.uint32
VMEM_LIMIT = 48 * 1024 * 1024
TM_IN, TM_QKV, TQ, TM_POST, TM_ROUTE, TM_COMB = 512, 512, 512, 512, 1024, 512

SC_CORES = 2
SC_SUBCORES = 16
SC_LANES = 16
SC_CHUNK = 64


def _cparams(sem, bounds_checks=True):
    return pltpu.CompilerParams(dimension_semantics=sem, vmem_limit_bytes=VMEM_LIMIT,
                                disable_bounds_checks=not bounds_checks)


def _layer_spec(shape, layer, col=None):
    def index(*grid_idx):
        return (layer,) + (0,) * (len(shape) - 1) + ((col(*grid_idx),) if col else (0,))

    return pl.BlockSpec((None,) + tuple(shape), index)


def _rms(x, g):
    return x * lax.rsqrt(jnp.mean(x * x, axis=-1, keepdims=True) + RMS_EPS) * g


def _in_kernel(x_ref, g_ref, w_ref, tab_ref, main_ref, d0_ref, d1_ref, d2_ref, xn_ref, acc_ref):
    j = pl.program_id(1)

    @pl.when(j == 0)
    def _():
        xn_ref[...] = _rms(x_ref[...], g_ref[...]).astype(BF16)

    acc = jnp.dot(xn_ref[...], w_ref[...], preferred_element_type=F32)

    @pl.when(j < N_MAIN_TILES)
    def _():
        main_ref[...] = acc.astype(BF16)

    def dil_tile(out_ref, dil):
        chunk = lambda c: acc[:, c * 128:(c + 1) * 128]
        for base, t0 in ((0, 0), (2, 3)):
            cos, sin_n, sin_p = [tab_ref[:, (t0 + t) * 128:(t0 + t + 1) * 128] for t in range(3)]
            lo, hi = chunk(base), chunk(base + 1)
            acc_ref[base] = lo * cos + hi * sin_n
            acc_ref[base + 1] = hi * cos + lo * sin_p
        for c in range(4, 6):
            acc_ref[c] = chunk(c)
        rows = acc_ref.shape[1] // dil
        for r in range(dil):
            for c in range(6):
                out_ref[0, r, :, c * 128:(c + 1) * 128] = acc_ref[c, pl.ds(r, rows, stride=dil), :].astype(BF16)

    for gi, (_, dil) in enumerate(DIL_PATTERN):
        pl.when(j == N_MAIN_TILES + gi)(functools.partial(dil_tile, (d0_ref, d1_ref, d2_ref)[gi], dil))


def _in_proj(x2d, g, w, tab, B, S, tm, layer):
    T = x2d.shape[0]
    nt = S // tm
    dil_shapes = [jax.ShapeDtypeStruct((B, d, S // d, 3 * DIL_W), BF16) for _, d in DIL_PATTERN]
    dil_specs = [pl.BlockSpec((1, d, tm // d, 3 * DIL_W), lambda i, j, nt=nt: (i // nt, 0, i % nt, 0))
                 for _, d in DIL_PATTERN]
    return pl.pallas_call(
        _in_kernel,
        grid=(T // tm, IN_COLS_PAD // IN_TN),
        in_specs=[
            pl.BlockSpec((tm, D_MODEL), lambda i, j: (i, 0)),
            _layer_spec((1, D_MODEL), layer),
            _layer_spec((D_MODEL, IN_TN), layer, col=lambda i, j: j),
            pl.BlockSpec((tm, 6 * 128), lambda i, j: (i, 0)),
        ],
        out_specs=[pl.BlockSpec((tm, IN_TN), lambda i, j: (i, jnp.minimum(j, N_MAIN_TILES - 1)))] + dil_specs,
        out_shape=[jax.ShapeDtypeStruct((T, MAIN_COLS), BF16)] + dil_shapes,
        scratch_shapes=[pltpu.VMEM((tm, D_MODEL), BF16), pltpu.VMEM((IN_TN // 128, tm, 128), F32)],
        compiler_params=_cparams(("parallel", "arbitrary")),
        name="in_proj",
    )(x2d, g, w, tab)


def _qkv_kernel(cq_ref, ckv_ref, kpe_ref, gq_ref, gkv_ref, wq_ref, wk_ref, wv_ref, tab_ref,
                q_ref, k_ref, v_ref):
    qn = _rms(cq_ref[...].astype(F32), gq_ref[...]).astype(BF16)
    kvn = _rms(ckv_ref[...].astype(F32), gkv_ref[...]).astype(BF16)
    qacc = jnp.dot(qn, wq_ref[...], preferred_element_type=F32)
    kacc = jnp.dot(kvn, wk_ref[...], preferred_element_type=F32)
    vacc = jnp.dot(kvn, wv_ref[...], preferred_element_type=F32)
    lane_v = lax.broadcasted_iota(jnp.int32, vacc.shape, 1)
    v_ref[...] = jnp.where(lane_v % MLA_SLOT < MLA_V, vacc, 1.0).astype(BF16)
    cq, snq, spq, ck, snk, spk = [tab_ref[:, t * 128:(t + 1) * 128] for t in range(6)]
    kpe_lo = kpe_ref[:, 0:128].astype(F32)
    kpe_hi = kpe_ref[:, 128:256].astype(F32)
    krot = (kpe_lo * ck + kpe_hi * snk, kpe_hi * ck + kpe_lo * spk)
    for pair in range(MLA_HEADS // 2):
        s0 = slice(2 * pair * MLA_SLOT, (2 * pair + 1) * MLA_SLOT)
        s1 = slice((2 * pair + 1) * MLA_SLOT, (2 * pair + 2) * MLA_SLOT)
        q0, q1 = qacc[:, s0], qacc[:, s1]
        q_ref[:, s0] = (q0 * cq + q1 * snq).astype(BF16)
        q_ref[:, s1] = (q1 * cq + q0 * spq).astype(BF16)
        k_ref[:, s0] = (kacc[:, s0] + krot[0]).astype(BF16)
        k_ref[:, s1] = (kacc[:, s1] + krot[1]).astype(BF16)


def _mla_qkv(main, gq, gkv, wq, wk, wv, tab, tm, layer):
    T = main.shape[0]
    return pl.pallas_call(
        _qkv_kernel,
        grid=(T // tm,),
        in_specs=[
            pl.BlockSpec((tm, MLA_Q_LORA), lambda i: (i, 2048 // MLA_Q_LORA)),
            pl.BlockSpec((tm, MLA_KV_LORA), lambda i: (i, 2560 // MLA_KV_LORA)),
            pl.BlockSpec((tm, 256), lambda i: (i, 2816 // 256)),
            _layer_spec((1, MLA_Q_LORA), layer),
            _layer_spec((1, MLA_KV_LORA), layer),
            _layer_spec((MLA_Q_LORA, MLA_HEADS * MLA_SLOT), layer),
            _layer_spec((MLA_KV_LORA, MLA_HEADS * MLA_SLOT), layer),
            _layer_spec((MLA_KV_LORA, MLA_HEADS * MLA_SLOT), layer),
            pl.BlockSpec((tm, 6 * 128), lambda i: (i, 0)),
        ],
        out_specs=[
            pl.BlockSpec((tm, MLA_HEADS * MLA_SLOT), lambda i: (i, 0)),
            pl.BlockSpec((tm, MLA_HEADS * MLA_SLOT), lambda i: (i, 0)),
            pl.BlockSpec((tm, MLA_HEADS * MLA_SLOT), lambda i: (i, 0)),
        ],
        out_shape=[
            jax.ShapeDtypeStruct((T, MLA_HEADS * MLA_SLOT), BF16),
            jax.ShapeDtypeStruct((T, MLA_HEADS * MLA_SLOT), BF16),
            jax.ShapeDtypeStruct((T, MLA_HEADS * MLA_SLOT), BF16),
        ],
        compiler_params=_cparams(("parallel",)),
        name="mla_qkv",
    )(main, main, main, gq, gkv, wq, wk, wv, tab)


def _mla_attn_kernel(qa_ref, qb_ref, k_ref, v_ref, o_ref, m_ref, acc_ref, *, tq):
    p_id = pl.program_id(2)
    nq = k_ref.shape[1] // tq
    lane = lax.broadcasted_iota(jnp.int32, (tq, 2 * MLA_SLOT), 1)
    qh = []
    for q_ref in (qa_ref, qb_ref):
        qp = q_ref[0]
        zero = jnp.zeros_like(qp)
        qh.append([jnp.where(_mla_head_lanes(lane, h), qp, zero) for h in range(2)])
    hq = tq // 2
    dn = (((1,), (1,)), ((), ()))

    def step(blk, start, width, r0, mask, first=False):
        kb = k_ref[0, pl.ds(start, width), :]
        vb = v_ref[0, pl.ds(start, width), :]
        rows = tq - r0
        s_pair = lax.dot_general(jnp.concatenate([qh[blk][0][r0:], qh[blk][1][r0:]], axis=0), kb, dn,
                                 preferred_element_type=F32)
        for h in range(2):
            s = s_pair[h * rows:(h + 1) * rows]
            if mask is not None:
                s = jnp.where(mask, s, NEG)
            m_cur = jnp.max(s, axis=-1, keepdims=True)
            if first:
                m_new = jnp.broadcast_to(m_cur, (tq - r0, 128))
            else:
                m_prev = m_ref[blk, h, r0:, :]
                m_new = jnp.maximum(m_prev, m_cur)
            p = jnp.exp2(s - jnp.concatenate([m_new] * (width // 128), axis=-1))
            pv = jnp.dot(p.astype(BF16), vb[:, h * MLA_SLOT:(h + 1) * MLA_SLOT], preferred_element_type=F32)
            if first:
                acc_ref[blk, h, r0:, :] = pv
            else:
                acc_ref[blk, h, r0:, :] = jnp.exp2(m_prev - m_new) * acc_ref[blk, h, r0:, :] + pv
            m_ref[blk, h, r0:, :] = m_new

    def causal(rows):
        return lax.broadcasted_iota(jnp.int32, (rows, hq), 1) <= lax.broadcasted_iota(jnp.int32, (rows, hq), 0)

    def q_blocks(n_a):
        n_b = nq - 1 - n_a
        for blk, n_full in ((1, n_b), (0, n_a)):
            for j in range(n_full):
                step(blk, j * tq, tq, 0, None, first=j == 0)
            step(blk, n_full * tq, hq, 0, causal(tq), first=n_full == 0)
            step(blk, n_full * tq + hq, hq, hq, causal(hq))

    for n_a in range(nq // 2):
        pl.when(p_id == n_a)(functools.partial(q_blocks, n_a))
    lane_o = lax.broadcasted_iota(jnp.int32, (tq, 2 * MLA_V), 1)
    for blk in range(2):
        outs = []
        for h in range(2):
            a = acc_ref[blk, h]
            outs.append(a / pltpu.roll(a, MLA_V, 1))
        o_ref[0, blk, 0] = jnp.where(lane_o < MLA_V, outs[0], pltpu.roll(outs[1], MLA_V, 1)).astype(BF16)


def _mla_attn(q, k, v, B, S, tq):
    nq = S // tq
    q = q.reshape(B, S, MLA_HEADS * MLA_SLOT)
    k = k.reshape(B, S, MLA_HEADS * MLA_SLOT)
    v = v.reshape(B, S, MLA_HEADS * MLA_SLOT)
    return pl.pallas_call(
        functools.partial(_mla_attn_kernel, tq=tq),
        grid=(B, MLA_HEADS // 2, nq // 2),
        in_specs=[
            pl.BlockSpec((1, tq, 2 * MLA_SLOT), lambda b, h, p: (b, p, h)),
            pl.BlockSpec((1, tq, 2 * MLA_SLOT), lambda b, h, p: (b, nq - 1 - p, h)),
            pl.BlockSpec((1, S, 2 * MLA_SLOT), lambda b, h, p: (b, 0, h)),
            pl.BlockSpec((1, S, 2 * MLA_SLOT), lambda b, h, p: (b, 0, h)),
        ],
        out_specs=pl.BlockSpec((1, 2, 1, tq, 2 * MLA_V), lambda b, h, p: (b, 0, p, 0, h)),
        out_shape=jax.ShapeDtypeStruct((B, 2, nq // 2, tq, MLA_HEADS * MLA_V), BF16),
        scratch_shapes=[pltpu.VMEM((2, 2, tq, 128), F32), pltpu.VMEM((2, 2, tq, MLA_SLOT), F32)],
        compiler_params=_cparams(("parallel", "parallel", "arbitrary")),
        name="mla_attn",
    )(q, q, k, v)


def _dil_kernel(qkv_ref, o_ref, lse_ref, bias_ref, *, nb):
    dil = qkv_ref.shape[1]
    width = 2 * DIL_BLK if nb > 1 else DIL_BLK
    row = lax.broadcasted_iota(jnp.int32, (DIL_BLK, width), 0)
    col = lax.broadcasted_iota(jnp.int32, (DIL_BLK, width), 1)
    bias_ref[0] = jnp.where(col <= row, 0.0, NEG)
    if nb > 1:
        later = jnp.logical_or(jnp.logical_and(col >= DIL_BLK, col - DIL_BLK <= row),
                               jnp.logical_and(col < DIL_BLK, col >= row))
        bias_ref[1] = jnp.where(later, 0.0, NEG)
    lane = lax.broadcasted_iota(jnp.int32, (DIL_BLK, DIL_W), 1)
    head_of_lane = lane // DIL_HEAD_DIM
    qk_head_of_lane = _dil_qk_head_of_lane(lane)
    dn = (((1,), (1,)), ((), ()))

    def by_head(parts):
        out = parts[DIL_HEADS - 1]
        for h in range(DIL_HEADS - 2, -1, -1):
            out = jnp.where(head_of_lane == h, parts[h], out)
        return out

    def unit(u):
        r = u // nb
        n = u % nb
        q0 = pl.multiple_of(n * DIL_BLK, DIL_BLK)
        q = qkv_ref[0, r, pl.ds(q0, DIL_BLK), 0:DIL_W]
        if nb > 1:
            w0 = pl.multiple_of(jnp.maximum(n - 1, 0) * DIL_BLK, DIL_BLK)
            bias = bias_ref[jnp.minimum(n, 1)]
        else:
            w0 = 0
            bias = bias_ref[0]
        kw = qkv_ref[0, r, pl.ds(w0, width), DIL_W:2 * DIL_W]
        vw = qkv_ref[0, r, pl.ds(w0, width), 2 * DIL_W:3 * DIL_W]
        zero = jnp.zeros_like(q)
        qs = jnp.concatenate([jnp.where(qk_head_of_lane == h, q, zero) for h in range(DIL_HEADS)], axis=0)
        s = lax.dot_general(qs, kw, dn, preferred_element_type=F32)
        s = (s.reshape(DIL_HEADS, DIL_BLK, width) + bias[None]).reshape(DIL_HEADS * DIL_BLK, width)
        m = jnp.max(s, axis=-1, keepdims=True)
        e = jnp.exp2(s - m)
        den = jnp.sum(e, axis=-1, keepdims=True)
        pv = jnp.dot(e.astype(BF16), vw, preferred_element_type=F32)
        lse = m + jnp.log2(den)
        blk = lambda t, h: t[h * DIL_BLK:(h + 1) * DIL_BLK]
        o = by_head([blk(pv, h) for h in range(DIL_HEADS)]) / by_head(
            [jnp.broadcast_to(blk(den, h), (DIL_BLK, DIL_W)) for h in range(DIL_HEADS)])
        o_ref[0, r, pl.ds(q0, DIL_BLK), :] = o.astype(BF16)
        lse_ref[0, r, pl.ds(q0, DIL_BLK), :] = by_head(
            [jnp.broadcast_to(blk(lse, h), (DIL_BLK, DIL_W)) for h in range(DIL_HEADS)])

    def body(t, carry):
        for u in range(DIL_UNROLL):
            unit(DIL_UNROLL * t + u)
        return carry

    lax.fori_loop(0, dil * nb // DIL_UNROLL, body, 0)


def _dil_attn(qkv, B, dil, L):
    nb = L // DIL_BLK
    return pl.pallas_call(
        functools.partial(_dil_kernel, nb=nb),
        grid=(B,),
        in_specs=[pl.BlockSpec((1, dil, L, 3 * DIL_W), lambda b: (b, 0, 0, 0))],
        out_specs=[
            pl.BlockSpec((1, dil, L, DIL_W), lambda b: (b, 0, 0, 0)),
            pl.BlockSpec((1, dil, L, DIL_W), lambda b: (b, 0, 0, 0)),
        ],
        out_shape=[
            jax.ShapeDtypeStruct((B, dil, L, DIL_W), BF16),
            jax.ShapeDtypeStruct((B, dil, L, DIL_W), F32),
        ],
        scratch_shapes=[pltpu.VMEM((2, DIL_BLK, 2 * DIL_BLK if nb > 1 else DIL_BLK), F32)],
        compiler_params=_cparams(("parallel",)),
        name=f"dil_attn_d{dil}",
    )(qkv)


def _post_kernel(x_ref, oa_ref, og0_ref, og1_ref, og2_ref, ls0_ref, ls1_ref, ls2_ref, gates_ref,
                 wa_ref, wb_ref, wo_ref, gffn_ref, wr_ref, br_ref,
                 x1_ref, h2_ref, lg_ref, ob_ref, *, tm):
    ls =[r[0] for r in (ls0_ref, ls1_ref, ls2_ref)]
    og = [r[0] for r in (og0_ref, og1_ref, og2_ref)]
    for gi, (_, dil) in enumerate(DIL_PATTERN):
        rows = tm // dil
        for r in range(dil):
            for c in range(2):
                sl = slice(c * 128, (c + 1) * 128)
                ob_ref[2 * gi + c, pl.ds(r, rows, stride=dil), :] = og[gi][r][:, sl].astype(F32)
                ob_ref[6 + 2 * gi + c, pl.ds(r, rows, stride=dil), :] = ls[gi][r][:, sl]

    def tok_major(k):
        return jnp.concatenate([ob_ref[2 * k], ob_ref[2 * k + 1]], axis=-1)

    l0, l1, l2 = tok_major(3), tok_major(4), tok_major(5)
    mx = jnp.maximum(jnp.maximum(l0, l1), l2)
    w0, w1, w2 = jnp.exp2(l0 - mx), jnp.exp2(l1 - mx), jnp.exp2(l2 - mx)
    ob = (w0 * tok_major(0) + w1 * tok_major(1) + w2 * tok_major(2)) / (w0 + w1 + w2)
    ya = jnp.dot(oa_ref[0, 0, 0], wa_ref[...], preferred_element_type=F32)
    yb = jnp.dot(ob.astype(BF16), wb_ref[...], preferred_element_type=F32)
    merged = (jax.nn.sigmoid(gates_ref[:, 0:D_MODEL].astype(F32)) * ya
              + jax.nn.sigmoid(gates_ref[:, D_MODEL:2 * D_MODEL].astype(F32)) * yb)
    x1 = x_ref[...] + jnp.dot(merged.astype(BF16), wo_ref[...], preferred_element_type=F32)
    x1_ref[...] = x1
    h2 = _rms(x1, gffn_ref[...])
    _rows_to_tiles(h2_ref, h2)
    lg_ref[...] = jnp.dot(h2.astype(BF16), wr_ref[...], preferred_element_type=F32) + br_ref[...]


def _post(x2d, oa, ogs, lss, main, wa, wb, wo, gffn, wr, br, B, S, tm, layer):
    T = x2d.shape[0]
    nt = S // tm
    res_specs = [pl.BlockSpec((1, d, tm // d, DIL_W), lambda i, nt=nt: (i // nt, 0, i % nt, 0))
                 for _, d in DIL_PATTERN]
    assert oa.shape[3] == tm and oa.shape[2] * 2 == nt

    def oa_index(i):
        it = i % nt
        late = it >= nt // 2
        return (i // nt, late.astype(jnp.int32), jnp.where(late, nt - 1 - it, it), 0, 0)

    return pl.pallas_call(
        functools.partial(_post_kernel, tm=tm),
        grid=(T // tm,),
        in_specs=[pl.BlockSpec((tm, D_MODEL), lambda i: (i, 0)),
                  pl.BlockSpec((1, 1, 1, tm, D_MODEL), oa_index)]
                 + res_specs + res_specs
                 + [pl.BlockSpec((tm, 2 * D_MODEL), lambda i: (i, 0)),
                    _layer_spec((D_MODEL, D_MODEL), layer),
                    _layer_spec((DIL_W, D_MODEL), layer),
                    _layer_spec((D_MODEL, D_MODEL), layer),
                    _layer_spec((1, D_MODEL), layer),
                    _layer_spec((D_MODEL, 128), layer),
                    _layer_spec((1, 128), layer)],
        out_specs=[pl.BlockSpec((tm, D_MODEL), lambda i: (i, 0)),
                   pl.BlockSpec((tm * ROW_TILE, 128), lambda i: (i, 0)),
                   pl.BlockSpec((tm, 128), lambda i: (i, 0))],
        out_shape=[jax.ShapeDtypeStruct((T, D_MODEL), F32),
                   jax.ShapeDtypeStruct((T * ROW_TILE, 128), U32),
                   jax.ShapeDtypeStruct((T, 128), F32)],
        scratch_shapes=[pltpu.VMEM((12, tm, 128), F32)],
        compiler_params=_cparams(("parallel",)),
        name="post_attn",
    )(x2d, oa, *ogs, *lss, main, wa, wb, wo, gffn, wr, br)


def _route_kernel(lg_ref, ri_ref, rw_ref, cnt_ref, carry_ref, lower_ref, *, tm):
    i = pl.program_id(0)

    @pl.when(i == 0)
    def _():
        carry_ref[...] = jnp.zeros(carry_ref.shape, F32)
        r_i = lax.broadcasted_iota(jnp.int32, (tm, tm), 0)
        c_i = lax.broadcasted_iota(jnp.int32, (tm, tm), 1)
        lower_ref[...] = jnp.where(c_i < r_i, 1.0, 0.0).astype(BF16)

    lg = lg_ref[...]
    lane = lax.broadcasted_iota(jnp.int32, lg.shape, 1)
    lane_f = lane.astype(F32)
    ninf = jnp.float32(-jnp.inf)

    def first_max(vals):
        vmax = jnp.max(vals, axis=-1, keepdims=True)
        idx = jnp.min(jnp.where(vals == vmax, lane_f, 128.0), axis=-1, keepdims=True)
        return vmax, idx.astype(jnp.int32)

    gl = jnp.where(lane < N_GROUPS, lg, ninf)
    gmax, g_sel = first_max(gl)
    p_g = 1.0 / jnp.sum(jnp.exp(gl - gmax), axis=-1, keepdims=True)
    lo = N_GROUPS + g_sel * EXPERTS_PER_GROUP
    el = jnp.where(jnp.logical_and(lane >= lo, lane < lo + EXPERTS_PER_GROUP), lg, ninf)
    v0, i0 = first_max(el)
    v1, i1 = first_max(jnp.where(lane == i0, ninf, el))
    t = jnp.exp(v1 - v0)
    w0 = p_g / (1.0 + t)
    w1 = p_g * t / (1.0 + t)
    e0 = i0 - N_GROUPS
    e1 = i1 - N_GROUPS
    hit0 = lane == e0
    hit1 = lane == e1
    oh = jnp.where(jnp.logical_or(hit0, hit1), 1.0, 0.0).astype(F32)
    excl = jnp.dot(lower_ref[...], oh.astype(BF16), preferred_element_type=F32) + carry_ref[...]
    r0 = jnp.sum(jnp.where(hit0, excl, 0.0), axis=-1, keepdims=True).astype(jnp.int32)
    r1 = jnp.sum(jnp.where(hit1, excl, 0.0), axis=-1, keepdims=True).astype(jnp.int32)
    carry_ref[...] = carry_ref[...] + jnp.sum(oh, axis=0, keepdims=True)
    zi = jnp.zeros(lg.shape, jnp.int32)
    ri_ref[...] = jnp.where(lane == 0, e0, jnp.where(lane == 1, e1, jnp.where(lane == 2, r0, jnp.where(lane == 3, r1, zi))))
    rw_ref[...] = jnp.where(lane == 0, w0, jnp.where(lane == 1, w1, jnp.zeros(lg.shape, F32)))
    cnt_ref[...] = carry_ref[...]


def _route(lg, tm):
    T = lg.shape[0]
    return pl.pallas_call(
        functools.partial(_route_kernel, tm=tm),
        grid=(T // tm,),
        in_specs=[pl.BlockSpec((tm, 128), lambda i: (i, 0))],
        out_specs=[pl.BlockSpec((tm, 128), lambda i: (i, 0)),
                   pl.BlockSpec((tm, 128), lambda i: (i, 0)),
                   pl.BlockSpec((1, 128), lambda i: (0, 0))],
        out_shape=[jax.ShapeDtypeStruct((T, 128), jnp.int32),
                   jax.ShapeDtypeStruct((T, 128), F32),
                   jax.ShapeDtypeStruct((1, 128), F32)],
        scratch_shapes=[pltpu.VMEM((1, 128), F32), pltpu.VMEM((tm, tm), BF16)],
        compiler_params=_cparams(("arbitrary",)),
        name="route",
    )(lg)


def _sc_gather_rows(table, idx, n_out=None):
    inverse = n_out is not None
    n_src = idx.shape[0]
    n = n_out if inverse else n_src
    n_workers = SC_CORES * SC_SUBCORES
    per_w = n // n_workers
    n_chunks = per_w // SC_CHUNK
    assert per_w * n_workers == n and n_chunks * SC_CHUNK == per_w and n_chunks % 2 == 0
    mesh = plsc.VectorSubcoreMesh(core_axis_name="c", subcore_axis_name="s",
                                  num_cores=SC_CORES, num_subcores=SC_SUBCORES)

    @functools.partial(
        pl.kernel, mesh=mesh,
        out_type=jax.ShapeDtypeStruct((n,) + table.shape[1:], table.dtype),
        scratch_types=[pltpu.VMEM((per_w,), jnp.int32),
                       pltpu.VMEM((SC_CHUNK,) + table.shape[1:], table.dtype),
                       pltpu.VMEM((SC_CHUNK,) + table.shape[1:], table.dtype),
                       pltpu.SemaphoreType.DMA, pltpu.SemaphoreType.DMA,
                       pltpu.VMEM((n_src if inverse else SC_LANES,), jnp.int32)],
        compiler_params=pltpu.CompilerParams(use_tc_tiling_on_sc=True, needs_layout_passes=not inverse),
        name="sc_dispatch_rows" if inverse else "sc_gather_rows",
    )
    def gather(table_hbm, idx_hbm, out_hbm, idx_v, rows_a, rows_b, sem_a, sem_b, map_v):
        wid = lax.axis_index("s") * SC_CORES + lax.axis_index("c")
        base = wid * per_w
        if inverse:
            pltpu.sync_copy(idx_hbm, map_v)
            lanes = lax.iota(jnp.int32, SC_LANES)

            @pl.loop(0, per_w // SC_LANES)
            def _(j):
                idx_v[pl.ds(j * SC_LANES, SC_LANES)] = lax.rem(base + j * SC_LANES + lanes, table.shape[0])

            @pl.loop(0, n_src // SC_LANES)
            def _(a):
                local = map_v[pl.ds(a * SC_LANES, SC_LANES)] - base
                mine = jnp.logical_and(local >= 0, local < per_w)
                plsc.store_scatter(idx_v, [local], lax.div(a * SC_LANES + lanes, TOP_K), mask=mine)
        else:
            pltpu.sync_copy(idx_hbm.at[pl.ds(base, per_w)], idx_v)

        def fetch(chunk, rows_v, sem):
            return pltpu.make_async_copy(table_hbm.at[idx_v.at[pl.ds(chunk * SC_CHUNK, SC_CHUNK)]], rows_v, sem)

        def flush(chunk, rows_v):
            pltpu.sync_copy(rows_v, out_hbm.at[pl.ds(base + chunk * SC_CHUNK, SC_CHUNK)])

        fetch(0, rows_a, sem_a).start()

        @pl.loop(0, n_chunks, step=2)
        def _(c):
            fetch(c + 1, rows_b, sem_b).start()
            fetch(c, rows_a, sem_a).wait()
            flush(c, rows_a)

            @pl.when(c + 2 < n_chunks)
            def _():
                fetch(c + 2, rows_a, sem_a).start()

            fetch(c + 1, rows_b, sem_b).wait()
            flush(c + 1, rows_b)

    return gather(table, idx)


def _rows_from_tiles(ref, first, n_rows, stride=ROW_TILE):
    words = [ref[pl.ds(first + c, n_rows, stride=stride), :] for c in range(ROW_TILE)]
    lo = [pltpu.bitcast(w << 16, F32) for w in words]
    hi = [pltpu.bitcast(w & U32(0xFFFF0000), F32) for w in words]
    return jnp.concatenate(lo + hi, axis=-1)


def _rows_to_tiles(ref, val):
    half = D_MODEL // 2

    def bits(x):
        return pltpu.bitcast(x.astype(BF16).astype(F32), U32)

    for c in range(ROW_TILE):
        lo = bits(val[:, c * 128:(c + 1) * 128])
        hi = bits(val[:, half + c * 128:half + (c + 1) * 128])
        ref[pl.ds(c, val.shape[0], stride=ROW_TILE), :] = hi | (lo >> 16)


def _ffn_kernel(be_ref, nx_ref, par_ref, nvb_ref, x_ref, w1_hbm, w3_hbm, w2_hbm, y_ref,
                w1f_ref, w3f_ref, w2f_ref, w1b_ref, w3b_ref, w2b_ref, sem, *, layer):
    b = pl.program_id(0)
    nvb = nvb_ref[0]
    expert = be_ref[b]
    slot = par_ref[b]
    new_expert = jnp.logical_or(b == 0, expert != be_ref[jnp.maximum(b - 1, 0)])

    def fetch(e, sl):
        return [pltpu.make_async_copy(w_hbm.at[layer, e], wf_ref.at[sl], sem.at[sl])
                for w_hbm, wf_ref in ((w1_hbm, w1f_ref), (w3_hbm, w3f_ref), (w2_hbm, w2f_ref))]

    @pl.when(jnp.logical_and(b == 0, nvb > 0))
    def _():
        for cp in fetch(expert, slot):
            cp.start()

    @pl.when(jnp.logical_and(b < nvb, new_expert))
    def _():
        for cp in fetch(expert, slot):
            cp.wait()

        @pl.when(nx_ref[b] >= 0)
        def _():
            for cp in fetch(nx_ref[b], 1 - slot):
                cp.start()

        w1b_ref[...] = w1f_ref[slot].astype(BF16)
        w3b_ref[...] = w3f_ref[slot].astype(BF16)
        w2b_ref[...] = w2f_ref[slot].astype(BF16)

    @pl.when(b < nvb)
    def _():
        xb = _rows_from_tiles(x_ref, 0, FFN_BM).astype(BF16)
        h1 = jnp.dot(xb, w1b_ref[...], preferred_element_type=F32)
        h3 = jnp.dot(xb, w3b_ref[...], preferred_element_type=F32)
        a = (jax.nn.silu(h1) * h3).astype(BF16)
        _rows_to_tiles(y_ref, jnp.dot(a, w2b_ref[...], preferred_element_type=F32))

    @pl.when(b >= nvb)
    def _():
        y_ref[...] = jnp.zeros(y_ref.shape, U32)


def _expert_ffn(block_e, next_e, parity, nvb, xs, w1, w3, w2, layer):
    nb = block_e.shape[0]

    def used(b, *prefetch):
        return jnp.minimum(b, jnp.maximum(prefetch[-1][0] - 1, 0))

    return pl.pallas_call(
        functools.partial(_ffn_kernel, layer=layer),
        grid_spec=pltpu.PrefetchScalarGridSpec(
            num_scalar_prefetch=4,
            grid=(nb,),
            in_specs=[
                pl.BlockSpec((FFN_BM * ROW_TILE, 128), lambda b, *prefetch: (used(b, *prefetch), 0)),
                pl.BlockSpec(memory_space=pl.ANY),
                pl.BlockSpec(memory_space=pl.ANY),
                pl.BlockSpec(memory_space=pl.ANY),
            ],
            out_specs=pl.BlockSpec((FFN_BM * ROW_TILE, 128), lambda b, *prefetch: (b, 0)),
            scratch_shapes=[pltpu.VMEM((2, D_MODEL, EXPERT_FF), F32), pltpu.VMEM((2, D_MODEL, EXPERT_FF), F32),
                            pltpu.VMEM((2, EXPERT_FF, D_MODEL), F32),
                            pltpu.VMEM((D_MODEL, EXPERT_FF), BF16), pltpu.VMEM((D_MODEL, EXPERT_FF), BF16),
                            pltpu.VMEM((EXPERT_FF, D_MODEL), BF16),
                            pltpu.SemaphoreType.DMA((2,))],
        ),
        out_shape=jax.ShapeDtypeStruct((nb * FFN_BM * ROW_TILE, 128), U32),
        compiler_params=_cparams(("arbitrary",)),
        name="expert_ffn",
    )(block_e, next_e, parity, nvb, xs, w1, w3, w2)


def _comb_kernel(yg_ref, x1_ref, rw_ref, p_ref, gple_ref, wpg_ref, wpp_ref, gout_ref, prev_ref, o_ref, *, tm, final):
    w = rw_ref[...]
    y0 = _rows_from_tiles(yg_ref, 0, tm, stride=TOP_K * ROW_TILE)
    y1 = _rows_from_tiles(yg_ref, ROW_TILE, tm, stride=TOP_K * ROW_TILE)
    x2 = x1_ref[...] + (y0 * w[:, 0:1] + y1 * w[:, 1:2])
    e = jnp.dot(p_ref[...].astype(BF16), wpp_ref[...], preferred_element_type=F32)
    gate = jax.nn.sigmoid(jnp.dot(_rms(x2, gple_ref[...]).astype(BF16), wpg_ref[...], preferred_element_type=F32))
    x3 = x2 + gate * e
    o_ref[...] = _rms(x3, gout_ref[...]) if final else x3


def _combine_ple(yg, prev, first, x1, rw, p2d, gple, wpg, wpp, gout, tm, layer):
    T = x1.shape[0]
    nt = T // tm
    off = first // tm
    if prev is None:
        prev = x1
    return pl.pallas_call(
        functools.partial(_comb_kernel, tm=tm, final=layer == DEPTH - 1),
        grid=(yg.shape[0] // (tm * TOP_K * ROW_TILE),),
        in_specs=[
            pl.BlockSpec((tm * TOP_K * ROW_TILE, 128), lambda i: (i, 0)),
            pl.BlockSpec((tm, D_MODEL), lambda i: (off + i, 0)),
            pl.BlockSpec((tm, 128), lambda i: (off + i, 0)),
            pl.BlockSpec((tm, PLE_DIM), lambda i: (layer * nt + off + i, 0)),
            _layer_spec((1, D_MODEL), layer),
            _layer_spec((D_MODEL, D_MODEL), layer),
            _layer_spec((PLE_DIM, D_MODEL), layer),
            pl.BlockSpec((1, D_MODEL), lambda i: (0, 0)),
            pl.BlockSpec(memory_space=pl.ANY),
        ],
        out_specs=pl.BlockSpec((tm, D_MODEL), lambda i: (off + i, 0)),
        out_shape=jax.ShapeDtypeStruct((T, D_MODEL), F32),
        input_output_aliases={} if first == 0 else {8: 0},
        compiler_params=_cparams(("parallel",)),
        name="combine_ple",
    )(yg, x1, rw, p2d, gple, wpg, wpp, gout, prev)


def _rope_tables(pos, theta, rot_dim, offset, span, scale):
    half = rot_dim // 2
    inv = jnp.float32(theta) ** (-jnp.arange(half, dtype=F32) * 2.0 / rot_dim)
    ang = pos.astype(F32)[:, None] * inv
    rel = np.arange(128) - offset
    rot = (rel >= 0) & (rel < span)
    spread = np.zeros((half, 128), np.float32)
    spread[rel[rot] % half, np.nonzero(rot)[0]] = 1.0
    to_lanes = lambda t: jnp.dot(t, spread, precision=lax.Precision.HIGHEST)
    cos, sin = to_lanes(jnp.cos(ang)) + (~rot).astype(np.float32), to_lanes(jnp.sin(ang))
    return jnp.concatenate([cos, -sin, sin], axis=1) * jnp.float32(scale)


def _prep_w_in(w):
    c_q = w[..., 0:512]
    c_kv = w[..., 512:768]
    k_pe = w[..., 768:800]
    dil = w[..., 800:800 + 2304].reshape(w.shape[:-1] + (DIL_GROUPS, 3, DIL_HEADS, DIL_HEAD_DIM))

    def qk_layout(t):
        flat = lambda u: u.reshape(u.shape[:-2] + (-1,))
        return jnp.concatenate([flat(t[..., :DIL_HALF]), flat(t[..., PARTIAL_ROT:PARTIAL_ROT + DIL_REST]),
                                flat(t[..., DIL_HALF:PARTIAL_ROT]), flat(t[..., PARTIAL_ROT + DIL_REST:])], axis=-1)

    dil = jnp.concatenate([jnp.concatenate([qk_layout(dil[..., g, 0, :, :]), qk_layout(dil[..., g, 1, :, :]),
                                            dil[..., g, 2, :, :].reshape(w.shape[:-1] + (DIL_W,))], axis=-1)
                           for g in range(DIL_GROUPS)], axis=-1)
    gates = w[..., 3104:5152]
    zeros = lambda n: jnp.zeros(w.shape[:-1] + (n,), w.dtype)
    lo, hi = k_pe[..., :MLA_HALF], k_pe[..., MLA_HALF:]
    pad = MLA_SLOT - MLA_NOPE - MLA_ROPE
    kslot = jnp.concatenate([zeros(MLA_NOPE), lo, lo, zeros(pad), zeros(MLA_NOPE), hi, hi, zeros(pad)], axis=-1)
    return jnp.concatenate([gates, c_q, c_kv, kslot, dil], axis=-1).astype(BF16)


def _pad_heads(w, width):
    w = jnp.pad(w, ((0, 0), (0, 0), (0, 0), (0, MLA_SLOT - width)))
    return w.reshape(w.shape[0], w.shape[1], MLA_HEADS * MLA_SLOT).astype(BF16)


def _prep_w_q(w):
    w = w.reshape(DEPTH, MLA_Q_LORA, MLA_HEADS // 2, 2, MLA_NOPE + MLA_ROPE)
    nope = w[..., :MLA_NOPE]
    lo = w[..., MLA_NOPE:MLA_NOPE + MLA_HALF]
    hi = w[..., MLA_NOPE + MLA_HALF:]
    both = lambda t: t.reshape(t.shape[:-2] + (2 * MLA_HALF,))
    zeros = jnp.zeros(w.shape[:3] + (MLA_SLOT - MLA_NOPE - MLA_ROPE,), w.dtype)
    slot0 = jnp.concatenate([nope[..., 0, :], both(lo), zeros], axis=-1)
    slot1 = jnp.concatenate([nope[..., 1, :], both(hi), zeros], axis=-1)
    return jnp.stack([slot0, slot1], axis=-2).reshape(DEPTH, MLA_Q_LORA, MLA_HEADS * MLA_SLOT).astype(BF16)


def _prep_w_kv(w):
    w = w.reshape(DEPTH, MLA_KV_LORA, MLA_HEADS, MLA_NOPE + MLA_V)
    return _pad_heads(w[..., :MLA_NOPE], MLA_NOPE), _pad_heads(w[..., MLA_NOPE:], MLA_V)


def _dest_kernel(ri_ref, ps_ref, o_ref):
    ri = ri_ref[...].astype(F32)
    lane = lax.broadcasted_iota(jnp.int32, ri.shape, 1)
    ps = ps_ref[...]

    def col(k):
        return jnp.sum(jnp.where(lane == k, ri, 0.0), axis=-1, keepdims=True)

    def dest(k):
        start = jnp.sum(jnp.where(lane == col(k).astype(jnp.int32), ps, 0.0), axis=-1, keepdims=True)
        return (start + col(2 + k)).astype(jnp.int32)

    o_ref[...] = jnp.where(lane == 0, dest(0), jnp.where(lane == 1, dest(1), 0))


def _dest_rows(ri, pstarts, tm):
    T = ri.shape[0]
    ps = jnp.zeros((1, 128), F32).at[0, :N_EXPERTS].set(pstarts.astype(F32))
    return pl.pallas_call(
        _dest_kernel,
        grid=(T // tm,),
        in_specs=[pl.BlockSpec((tm, 128), lambda i: (i, 0)), pl.BlockSpec((1, 128), lambda i: (0, 0))],
        out_specs=pl.BlockSpec((tm, 128), lambda i: (i, 0)),
        out_shape=jax.ShapeDtypeStruct((T, 128), jnp.int32),
        compiler_params=_cparams(("parallel",)),
        name="dest_rows",
    )(ri, ps)


def _dispatch_plan(ri, cnt, T, tm):
    counts = cnt[0, :N_EXPERTS].astype(jnp.int32)
    pcounts = (counts + FFN_BM - 1) // FFN_BM * FFN_BM
    pends = jnp.cumsum(pcounts)
    pstarts = pends - pcounts
    dest = _dest_rows(ri, pstarts, tm)[:, 0:TOP_K]
    n_blocks = (T * TOP_K) // FFN_BM + N_EXPERTS
    first_row = jnp.arange(n_blocks, dtype=jnp.int32) * FFN_BM
    block_e = jnp.minimum(jnp.sum((pends[None, :] <= first_row[:, None]).astype(jnp.int32), axis=1), N_EXPERTS - 1)
    nvb = pends[-1] // FFN_BM
    after = pends // FFN_BM
    next_of_expert = jnp.where(after < nvb, block_e[jnp.minimum(after, n_blocks - 1)], -1)
    run_of_expert = jnp.cumsum((pcounts > 0).astype(jnp.int32)) - 1
    ffn_plan = (block_e, next_of_expert[block_e].astype(jnp.int32), (run_of_expert[block_e] % 2).astype(jnp.int32),
                nvb.astype(jnp.int32).reshape(1))
    return dest.reshape(T * TOP_K), n_blocks * FFN_BM, ffn_plan


def kernel(x, p, positions, g_mix, w_in, g_q_lat, w_q_up, g_kv_lat, w_kv_up, w_branch_a, w_branch_b, w_out, g_ffn, w_router_grp, b_router_grp, w_router_exp, b_router_exp, w_exp_gate, w_exp_up, w_exp_down, g_ple, w_ple_gate, w_ple_proj, g_final):
    B, S, D = x.shape
    T = B * S
    pos = positions.reshape(T)
    dil_scale = DIL_HEAD_DIM ** -0.5 * LOG2E
    mla_scale = (MLA_NOPE + MLA_ROPE) ** -0.5 * LOG2E
    tab_dil = jnp.concatenate([_rope_tables(pos, ROPE_THETA, PARTIAL_ROT, 0, DIL_ROT_LANES, sc)
                               for sc in (dil_scale, 1.0)], axis=1)
    tab_mla = jnp.concatenate([_rope_tables(pos, MLA_ROPE_THETA, MLA_ROPE, MLA_NOPE, MLA_ROPE, sc)
                               for sc in (mla_scale, 1.0)], axis=1)

    w_in_p = _prep_w_in(w_in)
    wq = _prep_w_q(w_q_up)
    wk, wv = _prep_w_kv(w_kv_up)
    wr = jnp.concatenate([w_router_grp, w_router_exp, jnp.zeros((DEPTH, D, 128 - N_GROUPS - N_EXPERTS), F32)],
                         axis=-1).astype(BF16)
    br = jnp.concatenate([b_router_grp, b_router_exp.reshape(DEPTH, N_EXPERTS),
                          jnp.zeros((DEPTH, 128 - N_GROUPS - N_EXPERTS), F32)], axis=-1).reshape(DEPTH, 1, 128)
    wa, wb, wo = w_branch_a.astype(BF16), w_branch_b.astype(BF16), w_out.astype(BF16)
    wpg, wpp = w_ple_gate.astype(BF16), w_ple_proj.astype(BF16)
    gains = lambda g: g.reshape(DEPTH, 1, -1)

    xc = x.reshape(T, D)
    for i in range(DEPTH):
        main, d0, d1, d2 = _in_proj(xc, gains(g_mix), w_in_p, tab_dil, B, S, TM_IN, i)
        q, k, v = _mla_qkv(main, gains(g_q_lat), gains(g_kv_lat), wq, wk, wv, tab_mla, TM_QKV, i)
        o_a = _mla_attn(q, k, v, B, S, TQ)
        ogs, lss = [], []
        for (_, dil), qkv in zip(DIL_PATTERN, (d0, d1, d2)):
            og, ls = _dil_attn(qkv, B, dil, S // dil)
            ogs.append(og)
            lss.append(ls)
        x1, h2, lg = _post(xc, o_a, ogs, lss, main, wa, wb, wo, gains(g_ffn), wr, br, B, S, TM_POST, i)
        ri, rw, cnt = _route(lg, TM_ROUTE)
        dest, n_rows, ffn_plan = _dispatch_plan(ri, cnt, T, TM_ROUTE)
        xs = _sc_gather_rows(h2.reshape(T, ROW_TILE, 128), dest, n_out=n_rows)
        ys = _expert_ffn(*ffn_plan, xs.reshape(-1, 128), w_exp_gate, w_exp_up, w_exp_down, i)
        xc = None
        for first in (0, T // 2):
            yg = _sc_gather_rows(ys.reshape(-1, ROW_TILE, 128), dest[first * TOP_K:(first + T // 2) * TOP_K])
            xc = _combine_ple(yg.reshape(-1, 128), xc, first, x1, rw, p.reshape(DEPTH * T, PLE_DIM), gains(g_ple),
                              wpg, wpp, g_final.reshape(1, D), TM_COMB, i)
    return xc.reshape(B, S, D)
```

```python
import functools
import math

import jax
import jax.numpy as jnp
import numpy as np
from jax import lax
from jax.experimental import pallas as pl
from jax.experimental.pallas import tpu as pltpu
from jax.experimental.pallas import tpu_sc as plsc

F32 = jnp.float32
BF16 = jnp.bfloat16

D_MODEL = 1024
DEPTH = 4
RMS_EPS = 1e-6
NEG = -1e30
LOG2E = math.log2(math.e)

MLA_HEADS = 16
MLA_Q_LORA = 512
MLA_KV_LORA = 256
MLA_NOPE = 64
MLA_ROPE = 32
MLA_V = 64
MLA_ROPE_THETA = 10000.0
MLA_SLOT = 128
MLA_HALF = MLA_ROPE // 2


def _mla_head_lanes(lane, head):
    l = lane % MLA_SLOT
    nope = jnp.logical_and(lane // MLA_SLOT == head, l < MLA_NOPE)
    rope = jnp.logical_and(l >= MLA_NOPE + head * MLA_HALF, l < MLA_NOPE + (head + 1) * MLA_HALF)
    return jnp.logical_or(nope, rope)

DIL_PATTERN = ((128, 1), (512, 4), (2048, 16))
DIL_GROUPS = 3
DIL_HEADS = 4
DIL_HEAD_DIM = 64
DIL_W = DIL_HEADS * DIL_HEAD_DIM
DIL_BLK = 128
DIL_UNROLL = 16
ROPE_THETA = 500000.0
PARTIAL_ROT = DIL_HEAD_DIM // 4
DIL_HALF = PARTIAL_ROT // 2
DIL_ROT_LANES = DIL_HEADS * DIL_HALF
DIL_REST = (DIL_HEAD_DIM - PARTIAL_ROT) // 2


def _dil_qk_head_of_lane(lane):
    l = lane % 128
    plain = l - DIL_ROT_LANES
    plain_head = sum((plain >= h * DIL_REST).astype(jnp.int32) for h in range(1, DIL_HEADS))
    return jnp.where(l < DIL_ROT_LANES, l // DIL_HALF, plain_head)

N_GROUPS = 8
EXPERTS_PER_GROUP = 8
N_EXPERTS = 64
TOP_K = 2
EXPERT_FF = 256
PLE_DIM = 256

IN_TN = 768
MAIN_COLS = 3072
IN_COLS_PAD = MAIN_COLS + DIL_GROUPS * 3 * DIL_W
N_MAIN_TILES = MAIN_COLS // IN_TN

FFN_BM = 256
FFN_RANGES = 2
ROW_TILE = D_MODEL // 256
U32 = jnp.uint32
VMEM_LIMIT = 48 * 1024 * 1024
TM_IN, TM_QKV, TQ, TM_POST, TM_ROUTE, TM_COMB = 512, 512, 512, 512, 1024, 512

SC_CORES = 2
SC_SUBCORES = 16
SC_LANES = 16
SC_CHUNK = 64


def _cparams(sem, bounds_checks=True):
    return pltpu.CompilerParams(dimension_semantics=sem, vmem_limit_bytes=VMEM_LIMIT,
                                disable_bounds_checks=not bounds_checks)


def _layer_spec(shape, layer, col=None):
    def index(*grid_idx):
        return (layer,) + (0,) * (len(shape) - 1) + ((col(*grid_idx),) if col else (0,))

    return pl.BlockSpec((None,) + tuple(shape), index)


def _rms(x, g):
    return x * lax.rsqrt(jnp.mean(x * x, axis=-1, keepdims=True) + RMS_EPS) * g


def _in_kernel(x_ref, g_ref, w_ref, tab_ref, main_ref, d0_ref, d1_ref, d2_ref, xn_ref, acc_ref):
    j = pl.program_id(1)

    @pl.when(j == 0)
    def _():
        xn_ref[...] = _rms(x_ref[...], g_ref[...]).astype(BF16)

    acc = jnp.dot(xn_ref[...], w_ref[...], preferred_element_type=F32)

    @pl.when(j < N_MAIN_TILES)
    def _():
        main_ref[...] = acc.astype(BF16)

    def dil_tile(out_ref, dil):
        chunk = lambda c: acc[:, c * 128:(c + 1) * 128]
        for base, t0 in ((0, 0), (2, 3)):
            cos, sin_n, sin_p = [tab_ref[:, (t0 + t) * 128:(t0 + t + 1) * 128] for t in range(3)]
            lo, hi = chunk(base), chunk(base + 1)
            acc_ref[base] = lo * cos + hi * sin_n
            acc_ref[base + 1] = hi * cos + lo * sin_p
        for c in range(4, 6):
            acc_ref[c] = chunk(c)
        rows = acc_ref.shape[1] // dil
        for r in range(dil):
            for c in range(6):
                out_ref[0, r, :, c * 128:(c + 1) * 128] = acc_ref[c, pl.ds(r, rows, stride=dil), :].astype(BF16)

    for gi, (_, dil) in enumerate(DIL_PATTERN):
        pl.when(j == N_MAIN_TILES + gi)(functools.partial(dil_tile, (d0_ref, d1_ref, d2_ref)[gi], dil))


def _in_proj(x2d, g, w, tab, B, S, tm, layer):
    T = x2d.shape[0]
    nt = S // tm
    dil_shapes = [jax.ShapeDtypeStruct((B, d, S // d, 3 * DIL_W), BF16) for _, d in DIL_PATTERN]
    dil_specs = [pl.BlockSpec((1, d, tm // d, 3 * DIL_W), lambda i, j, nt=nt: (i // nt, 0, i % nt, 0))
                 for _, d in DIL_PATTERN]
    return pl.pallas_call(
        _in_kernel,
        grid=(T // tm, IN_COLS_PAD // IN_TN),
        in_specs=[
            pl.BlockSpec((tm, D_MODEL), lambda i, j: (i, 0)),
            _layer_spec((1, D_MODEL), layer),
            _layer_spec((D_MODEL, IN_TN), layer, col=lambda i, j: j),
            pl.BlockSpec((tm, 6 * 128), lambda i, j: (i, 0)),
        ],
        out_specs=[pl.BlockSpec((tm, IN_TN), lambda i, j: (i, jnp.minimum(j, N_MAIN_TILES - 1)))] + dil_specs,
        out_shape=[jax.ShapeDtypeStruct((T, MAIN_COLS), BF16)] + dil_shapes,
        scratch_shapes=[pltpu.VMEM((tm, D_MODEL), BF16), pltpu.VMEM((IN_TN // 128, tm, 128), F32)],
        compiler_params=_cparams(("parallel", "arbitrary")),
        name="in_proj",
    )(x2d, g, w, tab)


def _qkv_kernel(cq_ref, ckv_ref, kpe_ref, gq_ref, gkv_ref, wq_ref, wk_ref, wv_ref, tab_ref,
                q_ref, k_ref, v_ref):
    qn = _rms(cq_ref[...].astype(F32), gq_ref[...]).astype(BF16)
    kvn = _rms(ckv_ref[...].astype(F32), gkv_ref[...]).astype(BF16)
    qacc = jnp.dot(qn, wq_ref[...], preferred_element_type=F32)
    kacc = jnp.dot(kvn, wk_ref[...], preferred_element_type=F32)
    vacc = jnp.dot(kvn, wv_ref[...], preferred_element_type=F32)
    lane_v = lax.broadcasted_iota(jnp.int32, vacc.shape, 1)
    v_ref[...] = jnp.where(lane_v % MLA_SLOT < MLA_V, vacc, 1.0).astype(BF16)
    cq, snq, spq, ck, snk, spk = [tab_ref[:, t * 128:(t + 1) * 128] for t in range(6)]
    kpe_lo = kpe_ref[:, 0:128].astype(F32)
    kpe_hi = kpe_ref[:, 128:256].astype(F32)
    krot = (kpe_lo * ck + kpe_hi * snk, kpe_hi * ck + kpe_lo * spk)
    for pair in range(MLA_HEADS // 2):
        s0 = slice(2 * pair * MLA_SLOT, (2 * pair + 1) * MLA_SLOT)
        s1 = slice((2 * pair + 1) * MLA_SLOT, (2 * pair + 2) * MLA_SLOT)
        q0, q1 = qacc[:, s0], qacc[:, s1]
        q_ref[:, s0] = (q0 * cq + q1 * snq).astype(BF16)
        q_ref[:, s1] = (q1 * cq + q0 * spq).astype(BF16)
        k_ref[:, s0] = (kacc[:, s0] + krot[0]).astype(BF16)
        k_ref[:, s1] = (kacc[:, s1] + krot[1]).astype(BF16)


def _mla_qkv(main, gq, gkv, wq, wk, wv, tab, tm, layer):
    T = main.shape[0]
    return pl.pallas_call(
        _qkv_kernel,
        grid=(T // tm,),
        in_specs=[
            pl.BlockSpec((tm, MLA_Q_LORA), lambda i: (i, 2048 // MLA_Q_LORA)),
            pl.BlockSpec((tm, MLA_KV_LORA), lambda i: (i, 2560 // MLA_KV_LORA)),
            pl.BlockSpec((tm, 256), lambda i: (i, 2816 // 256)),
            _layer_spec((1, MLA_Q_LORA), layer),
            _layer_spec((1, MLA_KV_LORA), layer),
            _layer_spec((MLA_Q_LORA, MLA_HEADS * MLA_SLOT), layer),
            _layer_spec((MLA_KV_LORA, MLA_HEADS * MLA_SLOT), layer),
            _layer_spec((MLA_KV_LORA, MLA_HEADS * MLA_SLOT), layer),
            pl.BlockSpec((tm, 6 * 128), lambda i: (i, 0)),
        ],
        out_specs=[
            pl.BlockSpec((tm, MLA_HEADS * MLA_SLOT), lambda i: (i, 0)),
            pl.BlockSpec((tm, MLA_HEADS * MLA_SLOT), lambda i: (i, 0)),
            pl.BlockSpec((tm, MLA_HEADS * MLA_SLOT), lambda i: (i, 0)),
        ],
        out_shape=[
            jax.ShapeDtypeStruct((T, MLA_HEADS * MLA_SLOT), BF16),
            jax.ShapeDtypeStruct((T, MLA_HEADS * MLA_SLOT), BF16),
            jax.ShapeDtypeStruct((T, MLA_HEADS * MLA_SLOT), BF16),
        ],
        compiler_params=_cparams(("parallel",)),
        name="mla_qkv",
    )(main, main, main, gq, gkv, wq, wk, wv, tab)


def _mla_attn_kernel(qa_ref, qb_ref, k_ref, v_ref, o_ref, m_ref, acc_ref, *, tq):
    p_id = pl.program_id(2)
    nq = k_ref.shape[1] // tq
    lane = lax.broadcasted_iota(jnp.int32, (tq, 2 * MLA_SLOT), 1)
    qh = []
    for q_ref in (qa_ref, qb_ref):
        qp = q_ref[0]
        zero = jnp.zeros_like(qp)
        qh.append([jnp.where(_mla_head_lanes(lane, h), qp, zero) for h in range(2)])
    hq = tq // 2
    dn = (((1,), (1,)), ((), ()))

    def step(blk, start, width, r0, mask, first=False):
        kb = k_ref[0, pl.ds(start, width), :]
        vb = v_ref[0, pl.ds(start, width), :]
        rows = tq - r0
        s_pair = lax.dot_general(jnp.concatenate([qh[blk][0][r0:], qh[blk][1][r0:]], axis=0), kb, dn,
                                 preferred_element_type=F32)
        for h in range(2):
            s = s_pair[h * rows:(h + 1) * rows]
            if mask is not None:
                s = jnp.where(mask, s, NEG)
            m_cur = jnp.max(s, axis=-1, keepdims=True)
            if first:
                m_new = jnp.broadcast_to(m_cur, (tq - r0, 128))
            else:
                m_prev = m_ref[blk, h, r0:, :]
                m_new = jnp.maximum(m_prev, m_cur)
            p = jnp.exp2(s - jnp.concatenate([m_new] * (width // 128), axis=-1))
            pv = jnp.dot(p.astype(BF16), vb[:, h * MLA_SLOT:(h + 1) * MLA_SLOT], preferred_element_type=F32)
            if first:
                acc_ref[blk, h, r0:, :] = pv
            else:
                acc_ref[blk, h, r0:, :] = jnp.exp2(m_prev - m_new) * acc_ref[blk, h, r0:, :] + pv
            m_ref[blk, h, r0:, :] = m_new

    def causal(rows):
        return lax.broadcasted_iota(jnp.int32, (rows, hq), 1) <= lax.broadcasted_iota(jnp.int32, (rows, hq), 0)

    def q_blocks(n_a):
        n_b = nq - 1 - n_a
        for blk, n_full in ((1, n_b), (0, n_a)):
            for j in range(n_full):
                step(blk, j * tq, tq, 0, None, first=j == 0)
            step(blk, n_full * tq, hq, 0, causal(tq), first=n_full == 0)
            step(blk, n_full * tq + hq, hq, hq, causal(hq))

    for n_a in range(nq // 2):
        pl.when(p_id == n_a)(functools.partial(q_blocks, n_a))
    lane_o = lax.broadcasted_iota(jnp.int32, (tq, 2 * MLA_V), 1)
    for blk in range(2):
        outs = []
        for h in range(2):
            a = acc_ref[blk, h]
            outs.append(a / pltpu.roll(a, MLA_V, 1))
        o_ref[0, blk, 0] = jnp.where(lane_o < MLA_V, outs[0], pltpu.roll(outs[1], MLA_V, 1)).astype(BF16)


def _mla_attn(q, k, v, B, S, tq):
    nq = S // tq
    q = q.reshape(B, S, MLA_HEADS * MLA_SLOT)
    k = k.reshape(B, S, MLA_HEADS * MLA_SLOT)
    v = v.reshape(B, S, MLA_HEADS * MLA_SLOT)
    return pl.pallas_call(
        functools.partial(_mla_attn_kernel, tq=tq),
        grid=(B, MLA_HEADS // 2, nq // 2),
        in_specs=[
            pl.BlockSpec((1, tq, 2 * MLA_SLOT), lambda b, h, p: (b, p, h)),
            pl.BlockSpec((1, tq, 2 * MLA_SLOT), lambda b, h, p: (b, nq - 1 - p, h)),
            pl.BlockSpec((1, S, 2 * MLA_SLOT), lambda b, h, p: (b, 0, h)),
            pl.BlockSpec((1, S, 2 * MLA_SLOT), lambda b, h, p: (b, 0, h)),
        ],
        out_specs=pl.BlockSpec((1, 2, 1, tq, 2 * MLA_V), lambda b, h, p: (b, 0, p, 0, h)),
        out_shape=jax.ShapeDtypeStruct((B, 2, nq // 2, tq, MLA_HEADS * MLA_V), BF16),
        scratch_shapes=[pltpu.VMEM((2, 2, tq, 128), F32), pltpu.VMEM((2, 2, tq, MLA_SLOT), F32)],
        compiler_params=_cparams(("parallel", "parallel", "arbitrary")),
        name="mla_attn",
    )(q, q, k, v)


def _dil_kernel(qkv_ref, o_ref, lse_ref, bias_ref, *, nb):
    dil = qkv_ref.shape[1]
    width = 2 * DIL_BLK if nb > 1 else DIL_BLK
    row = lax.broadcasted_iota(jnp.int32, (DIL_BLK, width), 0)
    col = lax.broadcasted_iota(jnp.int32, (DIL_BLK, width), 1)
    bias_ref[0] = jnp.where(col <= row, 0.0, NEG)
    if nb > 1:
        later = jnp.logical_or(jnp.logical_and(col >= DIL_BLK, col - DIL_BLK <= row),
                               jnp.logical_and(col < DIL_BLK, col >= row))
        bias_ref[1] = jnp.where(later, 0.0, NEG)
    lane = lax.broadcasted_iota(jnp.int32, (DIL_BLK, DIL_W), 1)
    head_of_lane = lane // DIL_HEAD_DIM
    qk_head_of_lane = _dil_qk_head_of_lane(lane)
    dn = (((1,), (1,)), ((), ()))

    def by_head(parts):
        out = parts[DIL_HEADS - 1]
        for h in range(DIL_HEADS - 2, -1, -1):
            out = jnp.where(head_of_lane == h, parts[h], out)
        return out

    def unit(u):
        r = u // nb
        n = u % nb
        q0 = pl.multiple_of(n * DIL_BLK, DIL_BLK)
        q = qkv_ref[0, r, pl.ds(q0, DIL_BLK), 0:DIL_W]
        if nb > 1:
            w0 = pl.multiple_of(jnp.maximum(n - 1, 0) * DIL_BLK, DIL_BLK)
            bias = bias_ref[jnp.minimum(n, 1)]
        else:
            w0 = 0
            bias = bias_ref[0]
        kw = qkv_ref[0, r, pl.ds(w0, width), DIL_W:2 * DIL_W]
        vw = qkv_ref[0, r, pl.ds(w0, width), 2 * DIL_W:3 * DIL_W]
        zero = jnp.zeros_like(q)
        qs = jnp.concatenate([jnp.where(qk_head_of_lane == h, q, zero) for h in range(DIL_HEADS)], axis=0)
        s = lax.dot_general(qs, kw, dn, preferred_element_type=F32)
        s = (s.reshape(DIL_HEADS, DIL_BLK, width) + bias[None]).reshape(DIL_HEADS * DIL_BLK, width)
        m = jnp.max(s, axis=-1, keepdims=True)
        e = jnp.exp2(s - m)
        den = jnp.sum(e, axis=-1, keepdims=True)
        pv = jnp.dot(e.astype(BF16), vw, preferred_element_type=F32)
        lse = m + jnp.log2(den)
        blk = lambda t, h: t[h * DIL_BLK:(h + 1) * DIL_BLK]
        o = by_head([blk(pv, h) for h in range(DIL_HEADS)]) / by_head(
            [jnp.broadcast_to(blk(den, h), (DIL_BLK, DIL_W)) for h in range(DIL_HEADS)])
        o_ref[0, r, pl.ds(q0, DIL_BLK), :] = o.astype(BF16)
        lse_ref[0, r, pl.ds(q0, DIL_BLK), :] = by_head(
            [jnp.broadcast_to(blk(lse, h), (DIL_BLK, DIL_W)) for h in range(DIL_HEADS)])

    def body(t, carry):
        for u in range(DIL_UNROLL):
            unit(DIL_UNROLL * t + u)
        return carry

    lax.fori_loop(0, dil * nb // DIL_UNROLL, body, 0)


def _dil_attn(qkv, B, dil, L):
    nb = L // DIL_BLK
    return pl.pallas_call(
        functools.partial(_dil_kernel, nb=nb),
        grid=(B,),
        in_specs=[pl.BlockSpec((1, dil, L, 3 * DIL_W), lambda b: (b, 0, 0, 0))],
        out_specs=[
            pl.BlockSpec((1, dil, L, DIL_W), lambda b: (b, 0, 0, 0)),
            pl.BlockSpec((1, dil, L, DIL_W), lambda b: (b, 0, 0, 0)),
        ],
        out_shape=[
            jax.ShapeDtypeStruct((B, dil, L, DIL_W), BF16),
            jax.ShapeDtypeStruct((B, dil, L, DIL_W), F32),
        ],
        scratch_shapes=[pltpu.VMEM((2, DIL_BLK, 2 * DIL_BLK if nb > 1 else DIL_BLK), F32)],
        compiler_params=_cparams(("parallel",)),
        name=f"dil_attn_d{dil}",
    )(qkv)


def _post_kernel(x_ref, oa_ref, og0_ref, og1_ref, og2_ref, ls0_ref, ls1_ref, ls2_ref, gates_ref,
                 wa_ref, wb_ref, wo_ref, gffn_ref, wr_ref, br_ref,
                 x1_ref, h2_ref, lg_ref, ob_ref, *, tm):
    ls =[r[0] for r in (ls0_ref, ls1_ref, ls2_ref)]
    og = [r[0] for r in (og0_ref, og1_ref, og2_ref)]
    for gi, (_, dil) in enumerate(DIL_PATTERN):
        rows = tm // dil
        for r in range(dil):
            for c in range(2):
                sl = slice(c * 128, (c + 1) * 128)
                ob_ref[2 * gi + c, pl.ds(r, rows, stride=dil), :] = og[gi][r][:, sl].astype(F32)
                ob_ref[6 + 2 * gi + c, pl.ds(r, rows, stride=dil), :] = ls[gi][r][:, sl]

    def tok_major(k):
        return jnp.concatenate([ob_ref[2 * k], ob_ref[2 * k + 1]], axis=-1)

    l0, l1, l2 = tok_major(3), tok_major(4), tok_major(5)
    mx = jnp.maximum(jnp.maximum(l0, l1), l2)
    w0, w1, w2 = jnp.exp2(l0 - mx), jnp.exp2(l1 - mx), jnp.exp2(l2 - mx)
    ob = (w0 * tok_major(0) + w1 * tok_major(1) + w2 * tok_major(2)) / (w0 + w1 + w2)
    ya = jnp.dot(oa_ref[0, 0, 0], wa_ref[...], preferred_element_type=F32)
    yb = jnp.dot(ob.astype(BF16), wb_ref[...], preferred_element_type=F32)
    merged = (jax.nn.sigmoid(gates_ref[:, 0:D_MODEL].astype(F32)) * ya
              + jax.nn.sigmoid(gates_ref[:, D_MODEL:2 * D_MODEL].astype(F32)) * yb)
    x1 = x_ref[...] + jnp.dot(merged.astype(BF16), wo_ref[...], preferred_element_type=F32)
    x1_ref[...] = x1
    h2 = _rms(x1, gffn_ref[...])
    _rows_to_tiles(h2_ref, h2)
    lg_ref[...] = jnp.dot(h2.astype(BF16), wr_ref[...], preferred_element_type=F32) + br_ref[...]


def _post(x2d, oa, ogs, lss, main, wa, wb, wo, gffn, wr, br, B, S, tm, layer):
    T = x2d.shape[0]
    nt = S // tm
    res_specs = [pl.BlockSpec((1, d, tm // d, DIL_W), lambda i, nt=nt: (i // nt, 0, i % nt, 0))
                 for _, d in DIL_PATTERN]
    assert oa.shape[3] == tm and oa.shape[2] * 2 == nt

    def oa_index(i):
        it = i % nt
        late = it >= nt // 2
        return (i // nt, late.astype(jnp.int32), jnp.where(late, nt - 1 - it, it), 0, 0)

    return pl.pallas_call(
        functools.partial(_post_kernel, tm=tm),
        grid=(T // tm,),
        in_specs=[pl.BlockSpec((tm, D_MODEL), lambda i: (i, 0)),
                  pl.BlockSpec((1, 1, 1, tm, D_MODEL), oa_index)]
                 + res_specs + res_specs
                 + [pl.BlockSpec((tm, 2 * D_MODEL), lambda i: (i, 0)),
                    _layer_spec((D_MODEL, D_MODEL), layer),
                    _layer_spec((DIL_W, D_MODEL), layer),
                    _layer_spec((D_MODEL, D_MODEL), layer),
                    _layer_spec((1, D_MODEL), layer),
                    _layer_spec((D_MODEL, 128), layer),
                    _layer_spec((1, 128), layer)],
        out_specs=[pl.BlockSpec((tm, D_MODEL), lambda i: (i, 0)),
                   pl.BlockSpec((tm * ROW_TILE, 128), lambda i: (i, 0)),
                   pl.BlockSpec((tm, 128), lambda i: (i, 0))],
        out_shape=[jax.ShapeDtypeStruct((T, D_MODEL), F32),
                   jax.ShapeDtypeStruct((T * ROW_TILE, 128), U32),
                   jax.ShapeDtypeStruct((T, 128), F32)],
        scratch_shapes=[pltpu.VMEM((12, tm, 128), F32)],
        compiler_params=_cparams(("parallel",)),
        name="post_attn",
    )(x2d, oa, *ogs, *lss, main, wa, wb, wo, gffn, wr, br)


def _route_kernel(lg_ref, ri_ref, rw_ref, cnt_ref, carry_ref, lower_ref, *, tm):
    i = pl.program_id(0)

    @pl.when(i == 0)
    def _():
        carry_ref[...] = jnp.zeros(carry_ref.shape, F32)
        r_i = lax.broadcasted_iota(jnp.int32, (tm, tm), 0)
        c_i = lax.broadcasted_iota(jnp.int32, (tm, tm), 1)
        lower_ref[...] = jnp.where(c_i < r_i, 1.0, 0.0).astype(BF16)

    lg = lg_ref[...]
    lane = lax.broadcasted_iota(jnp.int32, lg.shape, 1)
    lane_f = lane.astype(F32)
    ninf = jnp.float32(-jnp.inf)

    def first_max(vals):
        vmax = jnp.max(vals, axis=-1, keepdims=True)
        idx = jnp.min(jnp.where(vals == vmax, lane_f, 128.0), axis=-1, keepdims=True)
        return vmax, idx.astype(jnp.int32)

    gl = jnp.where(lane < N_GROUPS, lg, ninf)
    gmax, g_sel = first_max(gl)
    p_g = 1.0 / jnp.sum(jnp.exp(gl - gmax), axis=-1, keepdims=True)
    lo = N_GROUPS + g_sel * EXPERTS_PER_GROUP
    el = jnp.where(jnp.logical_and(lane >= lo, lane < lo + EXPERTS_PER_GROUP), lg, ninf)
    v0, i0 = first_max(el)
    v1, i1 = first_max(jnp.where(lane == i0, ninf, el))
    t = jnp.exp(v1 - v0)
    w0 = p_g / (1.0 + t)
    w1 = p_g * t / (1.0 + t)
    e0 = i0 - N_GROUPS
    e1 = i1 - N_GROUPS
    hit0 = lane == e0
    hit1 = lane == e1
    oh = jnp.where(jnp.logical_or(hit0, hit1), 1.0, 0.0).astype(F32)
    excl = jnp.dot(lower_ref[...], oh.astype(BF16), preferred_element_type=F32) + carry_ref[...]
    r0 = jnp.sum(jnp.where(hit0, excl, 0.0), axis=-1, keepdims=True).astype(jnp.int32)
    r1 = jnp.sum(jnp.where(hit1, excl, 0.0), axis=-1, keepdims=True).astype(jnp.int32)
    carry_ref[...] = carry_ref[...] + jnp.sum(oh, axis=0, keepdims=True)
    zi = jnp.zeros(lg.shape, jnp.int32)
    ri_ref[...] = jnp.where(lane == 0, e0, jnp.where(lane == 1, e1, jnp.where(lane == 2, r0, jnp.where(lane == 3, r1, zi))))
    rw_ref[...] = jnp.where(lane == 0, w0, jnp.where(lane == 1, w1, jnp.zeros(lg.shape, F32)))
    cnt_ref[...] = carry_ref[...]


def _route(lg, tm):
    T = lg.shape[0]
    return pl.pallas_call(
        functools.partial(_route_kernel, tm=tm),
        grid=(T // tm,),
        in_specs=[pl.BlockSpec((tm, 128), lambda i: (i, 0))],
        out_specs=[pl.BlockSpec((tm, 128), lambda i: (i, 0)),
                   pl.BlockSpec((tm, 128), lambda i: (i, 0)),
                   pl.BlockSpec((1, 128), lambda i: (0, 0))],
        out_shape=[jax.ShapeDtypeStruct((T, 128), jnp.int32),
                   jax.ShapeDtypeStruct((T, 128), F32),
                   jax.ShapeDtypeStruct((1, 128), F32)],
        scratch_shapes=[pltpu.VMEM((1, 128), F32), pltpu.VMEM((tm, tm), BF16)],
        compiler_params=_cparams(("arbitrary",)),
        name="route",
    )(lg)


def _sc_gather_rows(table, idx, n_out=None, row0=0):
    inverse = n_out is not None
    n_src = idx.shape[0]
    n = n_out if inverse else n_src
    n_workers = SC_CORES * SC_SUBCORES
    per_w = n // n_workers
    n_chunks = per_w // SC_CHUNK
    assert per_w * n_workers == n and n_chunks * SC_CHUNK == per_w and n_chunks % 2 == 0
    mesh = plsc.VectorSubcoreMesh(core_axis_name="c", subcore_axis_name="s",
                                  num_cores=SC_CORES, num_subcores=SC_SUBCORES)

    @functools.partial(
        pl.kernel, mesh=mesh,
        out_type=jax.ShapeDtypeStruct((n,) + table.shape[1:], table.dtype),
        scratch_types=[pltpu.VMEM((per_w,), jnp.int32),
                       pltpu.VMEM((SC_CHUNK,) + table.shape[1:], table.dtype),
                       pltpu.VMEM((SC_CHUNK,) + table.shape[1:], table.dtype),
                       pltpu.SemaphoreType.DMA, pltpu.SemaphoreType.DMA,
                       pltpu.VMEM((n_src if inverse else SC_LANES,), jnp.int32)],
        compiler_params=pltpu.CompilerParams(use_tc_tiling_on_sc=True, needs_layout_passes=not inverse),
        name="sc_dispatch_rows" if inverse else "sc_gather_rows",
    )
    def gather(table_hbm, idx_hbm, out_hbm, idx_v, rows_a, rows_b, sem_a, sem_b, map_v):
        wid = lax.axis_index("s") * SC_CORES + lax.axis_index("c")
        base = wid * per_w
        if inverse:
            pltpu.sync_copy(idx_hbm, map_v)
            lanes = lax.iota(jnp.int32, SC_LANES)

            @pl.loop(0, per_w // SC_LANES)
            def _(j):
                idx_v[pl.ds(j * SC_LANES, SC_LANES)] = lax.rem(row0 + base + j * SC_LANES + lanes, table.shape[0])

            @pl.loop(0, n_src // SC_LANES)
            def _(a):
                local = map_v[pl.ds(a * SC_LANES, SC_LANES)] - (row0 + base)
                mine = jnp.logical_and(local >= 0, local < per_w)
                plsc.store_scatter(idx_v, [local], lax.div(a * SC_LANES + lanes, TOP_K), mask=mine)
        else:
            pltpu.sync_copy(idx_hbm.at[pl.ds(base, per_w)], idx_v)

        def fetch(chunk, rows_v, sem):
            return pltpu.make_async_copy(table_hbm.at[idx_v.at[pl.ds(chunk * SC_CHUNK, SC_CHUNK)]], rows_v, sem)

        def flush(chunk, rows_v):
            pltpu.sync_copy(rows_v, out_hbm.at[pl.ds(base + chunk * SC_CHUNK, SC_CHUNK)])

        fetch(0, rows_a, sem_a).start()

        @pl.loop(0, n_chunks, step=2)
        def _(c):
            fetch(c + 1, rows_b, sem_b).start()
            fetch(c, rows_a, sem_a).wait()
            flush(c, rows_a)

            @pl.when(c + 2 < n_chunks)
            def _():
                fetch(c + 2, rows_a, sem_a).start()

            fetch(c + 1, rows_b, sem_b).wait()
            flush(c + 1, rows_b)

    return gather(table, idx)


def _rows_from_tiles(ref, first, n_rows, stride=ROW_TILE):
    words = [ref[pl.ds(first + c, n_rows, stride=stride), :] for c in range(ROW_TILE)]
    lo = [pltpu.bitcast(w << 16, F32) for w in words]
    hi = [pltpu.bitcast(w & U32(0xFFFF0000), F32) for w in words]
    return jnp.concatenate(lo + hi, axis=-1)


def _rows_to_tiles(ref, val):
    half = D_MODEL // 2

    def bits(x):
        return pltpu.bitcast(x.astype(BF16).astype(F32), U32)

    for c in range(ROW_TILE):
        lo = bits(val[:, c * 128:(c + 1) * 128])
        hi = bits(val[:, half + c * 128:half + (c + 1) * 128])
        ref[pl.ds(c, val.shape[0], stride=ROW_TILE), :] = hi | (lo >> 16)


def _ffn_kernel(be_ref, nx_ref, par_ref, nvb_ref, x_ref, w1_hbm, w3_hbm, w2_hbm, prev_ref, y_ref,
                w1f_ref, w3f_ref, w2f_ref, w1b_ref, w3b_ref, w2b_ref, sem, *, layer):
    b = pl.program_id(0)
    nvb = nvb_ref[0]
    expert = be_ref[b]
    slot = par_ref[b]
    new_expert = jnp.logical_or(b == 0, expert != be_ref[jnp.maximum(b - 1, 0)])

    def fetch(e, sl):
        return [pltpu.make_async_copy(w_hbm.at[layer, e], wf_ref.at[sl], sem.at[sl])
                for w_hbm, wf_ref in ((w1_hbm, w1f_ref), (w3_hbm, w3f_ref), (w2_hbm, w2f_ref))]

    @pl.when(jnp.logical_and(b == 0, nvb > 0))
    def _():
        for cp in fetch(expert, slot):
            cp.start()

    @pl.when(jnp.logical_and(b < nvb, new_expert))
    def _():
        for cp in fetch(expert, slot):
            cp.wait()

        @pl.when(nx_ref[b] >= 0)
        def _():
            for cp in fetch(nx_ref[b], 1 - slot):
                cp.start()

        w1b_ref[...] = w1f_ref[slot].astype(BF16)
        w3b_ref[...] = w3f_ref[slot].astype(BF16)
        w2b_ref[...] = w2f_ref[slot].astype(BF16)

    @pl.when(b < nvb)
    def _():
        xb = _rows_from_tiles(x_ref, 0, FFN_BM).astype(BF16)
        h1 = jnp.dot(xb, w1b_ref[...], preferred_element_type=F32)
        h3 = jnp.dot(xb, w3b_ref[...], preferred_element_type=F32)
        a = (jax.nn.silu(h1) * h3).astype(BF16)
        _rows_to_tiles(y_ref, jnp.dot(a, w2b_ref[...], preferred_element_type=F32))

    @pl.when(b >= nvb)
    def _():
        y_ref[...] = jnp.zeros(y_ref.shape, U32)


def _expert_ffn(block_e, next_e, parity, nvb, xs, prev, first, n_blocks, w1, w3, w2, layer):
    nb = block_e.shape[0]
    if prev is None:
        prev = xs

    def used(b, *prefetch):
        return jnp.minimum(b, jnp.maximum(prefetch[-1][0] - 1, 0))

    return pl.pallas_call(
        functools.partial(_ffn_kernel, layer=layer),
        grid_spec=pltpu.PrefetchScalarGridSpec(
            num_scalar_prefetch=4,
            grid=(nb,),
            in_specs=[
                pl.BlockSpec((FFN_BM * ROW_TILE, 128), lambda b, *prefetch: (used(b, *prefetch), 0)),
                pl.BlockSpec(memory_space=pl.ANY),
                pl.BlockSpec(memory_space=pl.ANY),
                pl.BlockSpec(memory_space=pl.ANY),
                pl.BlockSpec(memory_space=pl.ANY),
            ],
            out_specs=pl.BlockSpec((FFN_BM * ROW_TILE, 128), lambda b, *prefetch: (first + b, 0)),
            scratch_shapes=[pltpu.VMEM((2, D_MODEL, EXPERT_FF), F32), pltpu.VMEM((2, D_MODEL, EXPERT_FF), F32),
                            pltpu.VMEM((2, EXPERT_FF, D_MODEL), F32),
                            pltpu.VMEM((D_MODEL, EXPERT_FF), BF16), pltpu.VMEM((D_MODEL, EXPERT_FF), BF16),
                            pltpu.VMEM((EXPERT_FF, D_MODEL), BF16),
                            pltpu.SemaphoreType.DMA((2,))],
        ),
        out_shape=jax.ShapeDtypeStruct((n_blocks * FFN_BM * ROW_TILE, 128), U32),
        input_output_aliases={} if first == 0 else {8: 0},
        compiler_params=_cparams(("arbitrary",)),
        name="expert_ffn",
    )(block_e, next_e, parity, nvb, xs, w1, w3, w2, prev)


def _comb_kernel(yg_ref, x1_ref, rw_ref, p_ref, gple_ref, wpg_ref, wpp_ref, gout_ref, prev_ref, o_ref, *, tm, final):
    w = rw_ref[...]
    y0 = _rows_from_tiles(yg_ref, 0, tm, stride=TOP_K * ROW_TILE)
    y1 = _rows_from_tiles(yg_ref, ROW_TILE, tm, stride=TOP_K * ROW_TILE)
    x2 = x1_ref[...] + (y0 * w[:, 0:1] + y1 * w[:, 1:2])
    e = jnp.dot(p_ref[...].astype(BF16), wpp_ref[...], preferred_element_type=F32)
    gate = jax.nn.sigmoid(jnp.dot(_rms(x2, gple_ref[...]).astype(BF16), wpg_ref[...], preferred_element_type=F32))
    x3 = x2 + gate * e
    o_ref[...] = _rms(x3, gout_ref[...]) if final else x3


def _combine_ple(yg, prev, first, x1, rw, p2d, gple, wpg, wpp, gout, tm, layer):
    T = x1.shape[0]
    nt = T // tm
    off = first // tm
    if prev is None:
        prev = x1
    return pl.pallas_call(
        functools.partial(_comb_kernel, tm=tm, final=layer == DEPTH - 1),
        grid=(yg.shape[0] // (tm * TOP_K * ROW_TILE),),
        in_specs=[
            pl.BlockSpec((tm * TOP_K * ROW_TILE, 128), lambda i: (i, 0)),
            pl.BlockSpec((tm, D_MODEL), lambda i: (off + i, 0)),
            pl.BlockSpec((tm, 128), lambda i: (off + i, 0)),
            pl.BlockSpec((tm, PLE_DIM), lambda i: (layer * nt + off + i, 0)),
            _layer_spec((1, D_MODEL), layer),
            _layer_spec((D_MODEL, D_MODEL), layer),
            _layer_spec((PLE_DIM, D_MODEL), layer),
            pl.BlockSpec((1, D_MODEL), lambda i: (0, 0)),
            pl.BlockSpec(memory_space=pl.ANY),
        ],
        out_specs=pl.BlockSpec((tm, D_MODEL), lambda i: (off + i, 0)),
        out_shape=jax.ShapeDtypeStruct((T, D_MODEL), F32),
        input_output_aliases={} if first == 0 else {8: 0},
        compiler_params=_cparams(("parallel",)),
        name="combine_ple",
    )(yg, x1, rw, p2d, gple, wpg, wpp, gout, prev)


def _rope_tables(pos, theta, rot_dim, offset, span, scale):
    half = rot_dim // 2
    inv = jnp.float32(theta) ** (-jnp.arange(half, dtype=F32) * 2.0 / rot_dim)
    ang = pos.astype(F32)[:, None] * inv
    rel = np.arange(128) - offset
    rot = (rel >= 0) & (rel < span)
    spread = np.zeros((half, 128), np.float32)
    spread[rel[rot] % half, np.nonzero(rot)[0]] = 1.0
    to_lanes = lambda t: jnp.dot(t, spread, precision=lax.Precision.HIGHEST)
    cos, sin = to_lanes(jnp.cos(ang)) + (~rot).astype(np.float32), to_lanes(jnp.sin(ang))
    return jnp.concatenate([cos, -sin, sin], axis=1) * jnp.float32(scale)


def _prep_w_in(w):
    c_q = w[..., 0:512]
    c_kv = w[..., 512:768]
    k_pe = w[..., 768:800]
    dil = w[..., 800:800 + 2304].reshape(w.shape[:-1] + (DIL_GROUPS, 3, DIL_HEADS, DIL_HEAD_DIM))

    def qk_layout(t):
        flat = lambda u: u.reshape(u.shape[:-2] + (-1,))
        return jnp.concatenate([flat(t[..., :DIL_HALF]), flat(t[..., PARTIAL_ROT:PARTIAL_ROT + DIL_REST]),
                                flat(t[..., DIL_HALF:PARTIAL_ROT]), flat(t[..., PARTIAL_ROT + DIL_REST:])], axis=-1)

    dil = jnp.concatenate([jnp.concatenate([qk_layout(dil[..., g, 0, :, :]), qk_layout(dil[..., g, 1, :, :]),
                                            dil[..., g, 2, :, :].reshape(w.shape[:-1] + (DIL_W,))], axis=-1)
                           for g in range(DIL_GROUPS)], axis=-1)
    gates = w[..., 3104:5152]
    zeros = lambda n: jnp.zeros(w.shape[:-1] + (n,), w.dtype)
    lo, hi = k_pe[..., :MLA_HALF], k_pe[..., MLA_HALF:]
    pad = MLA_SLOT - MLA_NOPE - MLA_ROPE
    kslot = jnp.concatenate([zeros(MLA_NOPE), lo, lo, zeros(pad), zeros(MLA_NOPE), hi, hi, zeros(pad)], axis=-1)
    return jnp.concatenate([gates, c_q, c_kv, kslot, dil], axis=-1).astype(BF16)


def _pad_heads(w, width):
    w = jnp.pad(w, ((0, 0), (0, 0), (0, 0), (0, MLA_SLOT - width)))
    return w.reshape(w.shape[0], w.shape[1], MLA_HEADS * MLA_SLOT).astype(BF16)


def _prep_w_q(w):
    w = w.reshape(DEPTH, MLA_Q_LORA, MLA_HEADS // 2, 2, MLA_NOPE + MLA_ROPE)
    nope = w[..., :MLA_NOPE]
    lo = w[..., MLA_NOPE:MLA_NOPE + MLA_HALF]
    hi = w[..., MLA_NOPE + MLA_HALF:]
    both = lambda t: t.reshape(t.shape[:-2] + (2 * MLA_HALF,))
    zeros = jnp.zeros(w.shape[:3] + (MLA_SLOT - MLA_NOPE - MLA_ROPE,), w.dtype)
    slot0 = jnp.concatenate([nope[..., 0, :], both(lo), zeros], axis=-1)
    slot1 = jnp.concatenate([nope[..., 1, :], both(hi), zeros], axis=-1)
    return jnp.stack([slot0, slot1], axis=-2).reshape(DEPTH, MLA_Q_LORA, MLA_HEADS * MLA_SLOT).astype(BF16)


def _prep_w_kv(w):
    w = w.reshape(DEPTH, MLA_KV_LORA, MLA_HEADS, MLA_NOPE + MLA_V)
    return _pad_heads(w[..., :MLA_NOPE], MLA_NOPE), _pad_heads(w[..., MLA_NOPE:], MLA_V)


def _dest_kernel(ri_ref, ps_ref, o_ref):
    ri = ri_ref[...].astype(F32)
    lane = lax.broadcasted_iota(jnp.int32, ri.shape, 1)
    ps = ps_ref[...]

    def col(k):
        return jnp.sum(jnp.where(lane == k, ri, 0.0), axis=-1, keepdims=True)

    def dest(k):
        start = jnp.sum(jnp.where(lane == col(k).astype(jnp.int32), ps, 0.0), axis=-1, keepdims=True)
        return (start + col(2 + k)).astype(jnp.int32)

    o_ref[...] = jnp.where(lane == 0, dest(0), jnp.where(lane == 1, dest(1), 0))


def _dest_rows(ri, pstarts, tm):
    T = ri.shape[0]
    ps = jnp.zeros((1, 128), F32).at[0, :N_EXPERTS].set(pstarts.astype(F32))
    return pl.pallas_call(
        _dest_kernel,
        grid=(T // tm,),
        in_specs=[pl.BlockSpec((tm, 128), lambda i: (i, 0)), pl.BlockSpec((1, 128), lambda i: (0, 0))],
        out_specs=pl.BlockSpec((tm, 128), lambda i: (i, 0)),
        out_shape=jax.ShapeDtypeStruct((T, 128), jnp.int32),
        compiler_params=_cparams(("parallel",)),
        name="dest_rows",
    )(ri, ps)


def _dispatch_plan(ri, cnt, T, tm):
    counts = cnt[0, :N_EXPERTS].astype(jnp.int32)
    pcounts = (counts + FFN_BM - 1) // FFN_BM * FFN_BM
    pends = jnp.cumsum(pcounts)
    pstarts = pends - pcounts
    dest = _dest_rows(ri, pstarts, tm)[:, 0:TOP_K]
    n_blocks = (T * TOP_K) // FFN_BM + N_EXPERTS
    first_row = jnp.arange(n_blocks, dtype=jnp.int32) * FFN_BM
    block_e = jnp.minimum(jnp.sum((pends[None, :] <= first_row[:, None]).astype(jnp.int32), axis=1), N_EXPERTS - 1)
    nvb = pends[-1] // FFN_BM
    after = pends // FFN_BM
    run_of_expert = jnp.cumsum((pcounts > 0).astype(jnp.int32)) - 1
    parity = (run_of_expert[block_e] % 2).astype(jnp.int32)
    per_range = n_blocks // FFN_RANGES
    ffn_plans = []
    for r in range(FFN_RANGES):
        lo, hi = r * per_range, (r + 1) * per_range
        end = jnp.minimum(nvb, hi)
        next_of_expert = jnp.where(after < end, block_e[jnp.minimum(after, n_blocks - 1)], -1)
        ffn_plans.append((block_e[lo:hi], next_of_expert[block_e[lo:hi]].astype(jnp.int32), parity[lo:hi],
                          jnp.clip(nvb - lo, 0, per_range).astype(jnp.int32).reshape(1)))
    return dest.reshape(T * TOP_K), n_blocks, ffn_plans


def kernel(x, p, positions, g_mix, w_in, g_q_lat, w_q_up, g_kv_lat, w_kv_up, w_branch_a, w_branch_b, w_out, g_ffn, w_router_grp, b_router_grp, w_router_exp, b_router_exp, w_exp_gate, w_exp_up, w_exp_down, g_ple, w_ple_gate, w_ple_proj, g_final):
    B, S, D = x.shape
    T = B * S
    pos = positions.reshape(T)
    dil_scale = DIL_HEAD_DIM ** -0.5 * LOG2E
    mla_scale = (MLA_NOPE + MLA_ROPE) ** -0.5 * LOG2E
    tab_dil = jnp.concatenate([_rope_tables(pos, ROPE_THETA, PARTIAL_ROT, 0, DIL_ROT_LANES, sc)
                               for sc in (dil_scale, 1.0)], axis=1)
    tab_mla = jnp.concatenate([_rope_tables(pos, MLA_ROPE_THETA, MLA_ROPE, MLA_NOPE, MLA_ROPE, sc)
                               for sc in (mla_scale, 1.0)], axis=1)

    w_in_p = _prep_w_in(w_in)
    wq = _prep_w_q(w_q_up)
    wk, wv = _prep_w_kv(w_kv_up)
    wr = jnp.concatenate([w_router_grp, w_router_exp, jnp.zeros((DEPTH, D, 128 - N_GROUPS - N_EXPERTS), F32)],
                         axis=-1).astype(BF16)
    br = jnp.concatenate([b_router_grp, b_router_exp.reshape(DEPTH, N_EXPERTS),
                          jnp.zeros((DEPTH, 128 - N_GROUPS - N_EXPERTS), F32)], axis=-1).reshape(DEPTH, 1, 128)
    wa, wb, wo = w_branch_a.astype(BF16), w_branch_b.astype(BF16), w_out.astype(BF16)
    wpg, wpp = w_ple_gate.astype(BF16), w_ple_proj.astype(BF16)
    gains = lambda g: g.reshape(DEPTH, 1, -1)

    xc = x.reshape(T, D)
    for i in range(DEPTH):
        main, d0, d1, d2 = _in_proj(xc, gains(g_mix), w_in_p, tab_dil, B, S, TM_IN, i)
        q, k, v = _mla_qkv(main, gains(g_q_lat), gains(g_kv_lat), wq, wk, wv, tab_mla, TM_QKV, i)
        o_a = _mla_attn(q, k, v, B, S, TQ)
        ogs, lss = [], []
        for (_, dil), qkv in zip(DIL_PATTERN, (d0, d1, d2)):
            og, ls = _dil_attn(qkv, B, dil, S // dil)
            ogs.append(og)
            lss.append(ls)
        x1, h2, lg = _post(xc, o_a, ogs, lss, main, wa, wb, wo, gains(g_ffn), wr, br, B, S, TM_POST, i)
        ri, rw, cnt = _route(lg, TM_ROUTE)
        dest, n_blocks, ffn_plans = _dispatch_plan(ri, cnt, T, TM_ROUTE)
        range_blocks = n_blocks // FFN_RANGES
        ys = None
        for r, plan in enumerate(ffn_plans):
            xs = _sc_gather_rows(h2.reshape(T, ROW_TILE, 128), dest, n_out=range_blocks * FFN_BM,
                                 row0=r * range_blocks * FFN_BM)
            ys = _expert_ffn(*plan, xs.reshape(-1, 128), ys, r * range_blocks, n_blocks,
                             w_exp_gate, w_exp_up, w_exp_down, i)
        xc = None
        for first in (0, T // 2):
            yg = _sc_gather_rows(ys.reshape(-1, ROW_TILE, 128), dest[first * TOP_K:(first + T // 2) * TOP_K])
            xc = _combine_ple(yg.reshape(-1, 128), xc, first, x1, rw, p.reshape(DEPTH * T, PLE_DIM), gains(g_ple),
                              wpg, wpp, g_final.reshape(1, D), TM_COMB, i)
    return xc.reshape(B, S, D)
```

```python
import functools
import math

import jax
import jax.numpy as jnp
import numpy as np
from jax import lax
from jax.experimental import pallas as pl
from jax.experimental.pallas import tpu as pltpu
from jax.experimental.pallas import tpu_sc as plsc

F32 = jnp.float32
BF16 = jnp.bfloat16

D_MODEL = 1024
DEPTH = 4
RMS_EPS = 1e-6
NEG = -1e30
LOG2E = math.log2(math.e)

MLA_HEADS = 16
MLA_Q_LORA = 512
MLA_KV_LORA = 256
MLA_NOPE = 64
MLA_ROPE = 32
MLA_V = 64
MLA_ROPE_THETA = 10000.0
MLA_SLOT = 128
MLA_HALF = MLA_ROPE // 2


def _mla_head_lanes(lane, head):
    l = lane % MLA_SLOT
    nope = jnp.logical_and(lane // MLA_SLOT == head, l < MLA_NOPE)
    rope = jnp.logical_and(l >= MLA_NOPE + head * MLA_HALF, l < MLA_NOPE + (head + 1) * MLA_HALF)
    return jnp.logical_or(nope, rope)

DIL_PATTERN = ((128, 1), (512, 4), (2048, 16))
DIL_GROUPS = 3
DIL_HEADS = 4
DIL_HEAD_DIM = 64
DIL_W = DIL_HEADS * DIL_HEAD_DIM
DIL_BLK = 128
DIL_UNROLL = 16
ROPE_THETA = 500000.0
PARTIAL_ROT = DIL_HEAD_DIM // 4
DIL_HALF = PARTIAL_ROT // 2
DIL_ROT_LANES = DIL_HEADS * DIL_HALF
DIL_REST = (DIL_HEAD_DIM - PARTIAL_ROT) // 2


def _dil_qk_head_of_lane(lane):
    l = lane % 128
    plain = l - DIL_ROT_LANES
    plain_head = sum((plain >= h * DIL_REST).astype(jnp.int32) for h in range(1, DIL_HEADS))
    return jnp.where(l < DIL_ROT_LANES, l // DIL_HALF, plain_head)

N_GROUPS = 8
EXPERTS_PER_GROUP = 8
N_EXPERTS = 64
TOP_K = 2
EXPERT_FF = 256
PLE_DIM = 256

IN_TN = 768
MAIN_COLS = 3072
IN_COLS_PAD = MAIN_COLS + DIL_GROUPS * 3 * DIL_W
N_MAIN_TILES = MAIN_COLS // IN_TN

FFN_BM = 256
FFN_RANGES = 2
ROW_TILE = D_MODEL // 256
U32 = jnp.uint32
VMEM_LIMIT = 48 * 1024 * 1024
TM_IN, TM_QKV, TQ, TM_POST, TM_ROUTE, TM_COMB = 512, 512, 512, 512, 1024, 512

SC_CORES = 2
SC_SUBCORES = 16
SC_LANES = 16
SC_CHUNK = 64


def _cparams(sem, bounds_checks=True):
    return pltpu.CompilerParams(dimension_semantics=sem, vmem_limit_bytes=VMEM_LIMIT,
                                disable_bounds_checks=not bounds_checks)


def _layer_spec(shape, layer, col=None):
    def index(*grid_idx):
        return (layer,) + (0,) * (len(shape) - 1) + ((col(*grid_idx),) if col else (0,))

    return pl.BlockSpec((None,) + tuple(shape), index)


def _rms(x, g):
    return x * lax.rsqrt(jnp.mean(x * x, axis=-1, keepdims=True) + RMS_EPS) * g


def _in_kernel(x_ref, g_ref, w_ref, tab_ref, main_ref, d0_ref, d1_ref, d2_ref, xn_ref, acc_ref):
    j = pl.program_id(1)

    @pl.when(j == 0)
    def _():
        xn_ref[...] = _rms(x_ref[...], g_ref[...]).astype(BF16)

    acc = jnp.dot(xn_ref[...], w_ref[...], preferred_element_type=F32)

    @pl.when(j < N_MAIN_TILES)
    def _():
        main_ref[...] = acc.astype(BF16)

    def dil_tile(out_ref, dil):
        chunk = lambda c: acc[:, c * 128:(c + 1) * 128]
        for base, t0 in ((0, 0), (2, 3)):
            cos, sin_n, sin_p = [tab_ref[:, (t0 + t) * 128:(t0 + t + 1) * 128] for t in range(3)]
            lo, hi = chunk(base), chunk(base + 1)
            acc_ref[base] = lo * cos + hi * sin_n
            acc_ref[base + 1] = hi * cos + lo * sin_p
        for c in range(4, 6):
            acc_ref[c] = chunk(c)
        rows = acc_ref.shape[1] // dil
        for r in range(dil):
            for c in range(6):
                out_ref[0, r, :, c * 128:(c + 1) * 128] = acc_ref[c, pl.ds(r, rows, stride=dil), :].astype(BF16)

    for gi, (_, dil) in enumerate(DIL_PATTERN):
        pl.when(j == N_MAIN_TILES + gi)(functools.partial(dil_tile, (d0_ref, d1_ref, d2_ref)[gi], dil))


def _in_proj(x2d, g, w, tab, B, S, tm, layer):
    T = x2d.shape[0]
    nt = S // tm
    dil_shapes = [jax.ShapeDtypeStruct((B, d, S // d, 3 * DIL_W), BF16) for _, d in DIL_PATTERN]
    dil_specs = [pl.BlockSpec((1, d, tm // d, 3 * DIL_W), lambda i, j, nt=nt: (i // nt, 0, i % nt, 0))
                 for _, d in DIL_PATTERN]
    return pl.pallas_call(
        _in_kernel,
        grid=(T // tm, IN_COLS_PAD // IN_TN),
        in_specs=[
            pl.BlockSpec((tm, D_MODEL), lambda i, j: (i, 0)),
            _layer_spec((1, D_MODEL), layer),
            _layer_spec((D_MODEL, IN_TN), layer, col=lambda i, j: j),
            pl.BlockSpec((tm, 6 * 128), lambda i, j: (i, 0)),
        ],
        out_specs=[pl.BlockSpec((tm, IN_TN), lambda i, j: (i, jnp.minimum(j, N_MAIN_TILES - 1)))] + dil_specs,
        out_shape=[jax.ShapeDtypeStruct((T, MAIN_COLS), BF16)] + dil_shapes,
        scratch_shapes=[pltpu.VMEM((tm, D_MODEL), BF16), pltpu.VMEM((IN_TN // 128, tm, 128), F32)],
        compiler_params=_cparams(("parallel", "arbitrary")),
        name="in_proj",
    )(x2d, g, w, tab)


def _qkv_kernel(cq_ref, ckv_ref, kpe_ref, gq_ref, gkv_ref, wq_ref, wk_ref, wv_ref, tab_ref,
                q_ref, k_ref, v_ref):
    qn = _rms(cq_ref[...].astype(F32), gq_ref[...]).astype(BF16)
    kvn = _rms(ckv_ref[...].astype(F32), gkv_ref[...]).astype(BF16)
    qacc = jnp.dot(qn, wq_ref[...], preferred_element_type=F32)
    kacc = jnp.dot(kvn, wk_ref[...], preferred_element_type=F32)
    vacc = jnp.dot(kvn, wv_ref[...], preferred_element_type=F32)
    lane_v = lax.broadcasted_iota(jnp.int32, vacc.shape, 1)
    v_ref[...] = jnp.where(lane_v % MLA_SLOT < MLA_V, vacc, 1.0).astype(BF16)
    cq, snq, spq, ck, snk, spk = [tab_ref[:, t * 128:(t + 1) * 128] for t in range(6)]
    kpe_lo = kpe_ref[:, 0:128].astype(F32)
    kpe_hi = kpe_ref[:, 128:256].astype(F32)
    krot = (kpe_lo * ck + kpe_hi * snk, kpe_hi * ck + kpe_lo * spk)
    for pair in range(MLA_HEADS // 2):
        s0 = slice(2 * pair * MLA_SLOT, (2 * pair + 1) * MLA_SLOT)
        s1 = slice((2 * pair + 1) * MLA_SLOT, (2 * pair + 2) * MLA_SLOT)
        q0, q1 = qacc[:, s0], qacc[:, s1]
        q_ref[:, s0] = (q0 * cq + q1 * snq).astype(BF16)
        q_ref[:, s1] = (q1 * cq + q0 * spq).astype(BF16)
        k_ref[:, s0] = (kacc[:, s0] + krot[0]).astype(BF16)
        k_ref[:, s1] = (kacc[:, s1] + krot[1]).astype(BF16)


def _mla_qkv(main, gq, gkv, wq, wk, wv, tab, tm, layer):
    T = main.shape[0]
    return pl.pallas_call(
        _qkv_kernel,
        grid=(T // tm,),
        in_specs=[
            pl.BlockSpec((tm, MLA_Q_LORA), lambda i: (i, 2048 // MLA_Q_LORA)),
            pl.BlockSpec((tm, MLA_KV_LORA), lambda i: (i, 2560 // MLA_KV_LORA)),
            pl.BlockSpec((tm, 256), lambda i: (i, 2816 // 256)),
            _layer_spec((1, MLA_Q_LORA), layer),
            _layer_spec((1, MLA_KV_LORA), layer),
            _layer_spec((MLA_Q_LORA, MLA_HEADS * MLA_SLOT), layer),
            _layer_spec((MLA_KV_LORA, MLA_HEADS * MLA_SLOT), layer),
            _layer_spec((MLA_KV_LORA, MLA_HEADS * MLA_SLOT), layer),
            pl.BlockSpec((tm, 6 * 128), lambda i: (i, 0)),
        ],
        out_specs=[
            pl.BlockSpec((tm, MLA_HEADS * MLA_SLOT), lambda i: (i, 0)),
            pl.BlockSpec((tm, MLA_HEADS * MLA_SLOT), lambda i: (i, 0)),
            pl.BlockSpec((tm, MLA_HEADS * MLA_SLOT), lambda i: (i, 0)),
        ],
        out_shape=[
            jax.ShapeDtypeStruct((T, MLA_HEADS * MLA_SLOT), BF16),
            jax.ShapeDtypeStruct((T, MLA_HEADS * MLA_SLOT), BF16),
            jax.ShapeDtypeStruct((T, MLA_HEADS * MLA_SLOT), BF16),
        ],
        compiler_params=_cparams(("parallel",)),
        name="mla_qkv",
    )(main, main, main, gq, gkv, wq, wk, wv, tab)


def _mla_attn_kernel(qa_ref, qb_ref, k_ref, v_ref, o_ref, m_ref, acc_ref, *, tq):
    p_id = pl.program_id(2)
    nq = k_ref.shape[1] // tq
    lane = lax.broadcasted_iota(jnp.int32, (tq, 2 * MLA_SLOT), 1)
    qh = []
    for q_ref in (qa_ref, qb_ref):
        qp = q_ref[0]
        zero = jnp.zeros_like(qp)
        qh.append([jnp.where(_mla_head_lanes(lane, h), qp, zero) for h in range(2)])
    hq = tq // 2
    dn = (((1,), (1,)), ((), ()))

    def step(blk, start, width, r0, mask, first=False):
        kb = k_ref[0, pl.ds(start, width), :]
        vb = v_ref[0, pl.ds(start, width), :]
        rows = tq - r0
        s_pair = lax.dot_general(jnp.concatenate([qh[blk][0][r0:], qh[blk][1][r0:]], axis=0), kb, dn,
                                 preferred_element_type=F32)
        for h in range(2):
            s = s_pair[h * rows:(h + 1) * rows]
            if mask is not None:
                s = jnp.where(mask, s, NEG)
            m_cur = jnp.max(s, axis=-1, keepdims=True)
            if first:
                m_new = jnp.broadcast_to(m_cur, (tq - r0, 128))
            else:
                m_prev = m_ref[blk, h, r0:, :]
                m_new = jnp.maximum(m_prev, m_cur)
            p = jnp.exp2(s - jnp.concatenate([m_new] * (width // 128), axis=-1))
            pv = jnp.dot(p.astype(BF16), vb[:, h * MLA_SLOT:(h + 1) * MLA_SLOT], preferred_element_type=F32)
            if first:
                acc_ref[blk, h, r0:, :] = pv
            else:
                acc_ref[blk, h, r0:, :] = jnp.exp2(m_prev - m_new) * acc_ref[blk, h, r0:, :] + pv
            m_ref[blk, h, r0:, :] = m_new

    def causal(rows):
        return lax.broadcasted_iota(jnp.int32, (rows, hq), 1) <= lax.broadcasted_iota(jnp.int32, (rows, hq), 0)

    def q_blocks(n_a):
        n_b = nq - 1 - n_a
        for blk, n_full in ((1, n_b), (0, n_a)):
            for j in range(n_full):
                step(blk, j * tq, tq, 0, None, first=j == 0)
            step(blk, n_full * tq, hq, 0, causal(tq), first=n_full == 0)
            step(blk, n_full * tq + hq, hq, hq, causal(hq))

    for n_a in range(nq // 2):
        pl.when(p_id == n_a)(functools.partial(q_blocks, n_a))
    lane_o = lax.broadcasted_iota(jnp.int32, (tq, 2 * MLA_V), 1)
    for blk in range(2):
        outs = []
        for h in range(2):
            a = acc_ref[blk, h]
            outs.append(a / pltpu.roll(a, MLA_V, 1))
        o_ref[0, blk, 0] = jnp.where(lane_o < MLA_V, outs[0], pltpu.roll(outs[1], MLA_V, 1)).astype(BF16)


def _mla_attn(q, k, v, B, S, tq):
    nq = S // tq
    q = q.reshape(B, S, MLA_HEADS * MLA_SLOT)
    k = k.reshape(B, S, MLA_HEADS * MLA_SLOT)
    v = v.reshape(B, S, MLA_HEADS * MLA_SLOT)
    return pl.pallas_call(
        functools.partial(_mla_attn_kernel, tq=tq),
        grid=(B, MLA_HEADS // 2, nq // 2),
        in_specs=[
            pl.BlockSpec((1, tq, 2 * MLA_SLOT), lambda b, h, p: (b, p, h)),
            pl.BlockSpec((1, tq, 2 * MLA_SLOT), lambda b, h, p: (b, nq - 1 - p, h)),
            pl.BlockSpec((1, S, 2 * MLA_SLOT), lambda b, h, p: (b, 0, h)),
            pl.BlockSpec((1, S, 2 * MLA_SLOT), lambda b, h, p: (b, 0, h)),
        ],
        out_specs=pl.BlockSpec((1, 2, 1, tq, 2 * MLA_V), lambda b, h, p: (b, 0, p, 0, h)),
        out_shape=jax.ShapeDtypeStruct((B, 2, nq // 2, tq, MLA_HEADS * MLA_V), BF16),
        scratch_shapes=[pltpu.VMEM((2, 2, tq, 128), F32), pltpu.VMEM((2, 2, tq, MLA_SLOT), F32)],
        compiler_params=_cparams(("parallel", "parallel", "arbitrary")),
        name="mla_attn",
    )(q, q, k, v)


def _dil_kernel(qkv_ref, o_ref, lse_ref, bias_ref, *, nb):
    dil = qkv_ref.shape[1]
    width = 2 * DIL_BLK if nb > 1 else DIL_BLK
    row = lax.broadcasted_iota(jnp.int32, (DIL_BLK, width), 0)
    col = lax.broadcasted_iota(jnp.int32, (DIL_BLK, width), 1)
    bias_ref[0] = jnp.where(col <= row, 0.0, NEG)
    if nb > 1:
        later = jnp.logical_or(jnp.logical_and(col >= DIL_BLK, col - DIL_BLK <= row),
                               jnp.logical_and(col < DIL_BLK, col >= row))
        bias_ref[1] = jnp.where(later, 0.0, NEG)
    lane = lax.broadcasted_iota(jnp.int32, (DIL_BLK, DIL_W), 1)
    head_of_lane = lane // DIL_HEAD_DIM
    qk_head_of_lane = _dil_qk_head_of_lane(lane)
    dn = (((1,), (1,)), ((), ()))

    def by_head(parts):
        out = parts[DIL_HEADS - 1]
        for h in range(DIL_HEADS - 2, -1, -1):
            out = jnp.where(head_of_lane == h, parts[h], out)
        return out

    def unit(u):
        r = u // nb
        n = u % nb
        q0 = pl.multiple_of(n * DIL_BLK, DIL_BLK)
        q = qkv_ref[0, r, pl.ds(q0, DIL_BLK), 0:DIL_W]
        if nb > 1:
            w0 = pl.multiple_of(jnp.maximum(n - 1, 0) * DIL_BLK, DIL_BLK)
            bias = bias_ref[jnp.minimum(n, 1)]
        else:
            w0 = 0
            bias = bias_ref[0]
        kw = qkv_ref[0, r, pl.ds(w0, width), DIL_W:2 * DIL_W]
        vw = qkv_ref[0, r, pl.ds(w0, width), 2 * DIL_W:3 * DIL_W]
        zero = jnp.zeros_like(q)
        qs = jnp.concatenate([jnp.where(qk_head_of_lane == h, q, zero) for h in range(DIL_HEADS)], axis=0)
        s = lax.dot_general(qs, kw, dn, preferred_element_type=F32)
        s = (s.reshape(DIL_HEADS, DIL_BLK, width) + bias[None]).reshape(DIL_HEADS * DIL_BLK, width)
        m = jnp.max(s, axis=-1, keepdims=True)
        e = jnp.exp2(s - m)
        den = jnp.sum(e, axis=-1, keepdims=True)
        pv = jnp.dot(e.astype(BF16), vw, preferred_element_type=F32)
        lse = m + jnp.log2(den)
        blk = lambda t, h: t[h * DIL_BLK:(h + 1) * DIL_BLK]
        o = by_head([blk(pv, h) for h in range(DIL_HEADS)]) / by_head(
            [jnp.broadcast_to(blk(den, h), (DIL_BLK, DIL_W)) for h in range(DIL_HEADS)])
        o_ref[0, r, pl.ds(q0, DIL_BLK), :] = o.astype(BF16)
        lse_ref[0, r, pl.ds(q0, DIL_BLK), :] = by_head(
            [jnp.broadcast_to(blk(lse, h), (DIL_BLK, DIL_W)) for h in range(DIL_HEADS)])

    def body(t, carry):
        for u in range(DIL_UNROLL):
            unit(DIL_UNROLL * t + u)
        return carry

    lax.fori_loop(0, dil * nb // DIL_UNROLL, body, 0)


def _dil_attn(qkv, B, dil, L):
    nb = L // DIL_BLK
    return pl.pallas_call(
        functools.partial(_dil_kernel, nb=nb),
        grid=(B,),
        in_specs=[pl.BlockSpec((1, dil, L, 3 * DIL_W), lambda b: (b, 0, 0, 0))],
        out_specs=[
            pl.BlockSpec((1, dil, L, DIL_W), lambda b: (b, 0, 0, 0)),
            pl.BlockSpec((1, dil, L, DIL_W), lambda b: (b, 0, 0, 0)),
        ],
        out_shape=[
            jax.ShapeDtypeStruct((B, dil, L, DIL_W), BF16),
            jax.ShapeDtypeStruct((B, dil, L, DIL_W), F32),
        ],
        scratch_shapes=[pltpu.VMEM((2, DIL_BLK, 2 * DIL_BLK if nb > 1 else DIL_BLK), F32)],
        compiler_params=_cparams(("parallel",)),
        name=f"dil_attn_d{dil}",
    )(qkv)


def _post_kernel(x_ref, oa_ref, og0_ref, og1_ref, og2_ref, ls0_ref, ls1_ref, ls2_ref, gates_ref,
                 wa_ref, wb_ref, wo_ref, gffn_ref, wr_ref, br_ref,
                 x1_ref, h2_ref, lg_ref, ob_ref, *, tm):
    ls =[r[0] for r in (ls0_ref, ls1_ref, ls2_ref)]
    og = [r[0] for r in (og0_ref, og1_ref, og2_ref)]
    for gi, (_, dil) in enumerate(DIL_PATTERN):
        rows = tm // dil
        for r in range(dil):
            for c in range(2):
                sl = slice(c * 128, (c + 1) * 128)
                ob_ref[2 * gi + c, pl.ds(r, rows, stride=dil), :] = og[gi][r][:, sl].astype(F32)
                ob_ref[6 + 2 * gi + c, pl.ds(r, rows, stride=dil), :] = ls[gi][r][:, sl]

    def tok_major(k):
        return jnp.concatenate([ob_ref[2 * k], ob_ref[2 * k + 1]], axis=-1)

    l0, l1, l2 = tok_major(3), tok_major(4), tok_major(5)
    mx = jnp.maximum(jnp.maximum(l0, l1), l2)
    w0, w1, w2 = jnp.exp2(l0 - mx), jnp.exp2(l1 - mx), jnp.exp2(l2 - mx)
    ob = (w0 * tok_major(0) + w1 * tok_major(1) + w2 * tok_major(2)) / (w0 + w1 + w2)
    ya = jnp.dot(oa_ref[0, 0, 0], wa_ref[...], preferred_element_type=F32)
    yb = jnp.dot(ob.astype(BF16), wb_ref[...], preferred_element_type=F32)
    merged = (jax.nn.sigmoid(gates_ref[:, 0:D_MODEL].astype(F32)) * ya
              + jax.nn.sigmoid(gates_ref[:, D_MODEL:2 * D_MODEL].astype(F32)) * yb)
    x1 = x_ref[...] + jnp.dot(merged.astype(BF16), wo_ref[...], preferred_element_type=F32)
    x1_ref[...] = x1
    h2 = _rms(x1, gffn_ref[...])
    _rows_to_tiles(h2_ref, h2)
    lg_ref[...] = jnp.dot(h2.astype(BF16), wr_ref[...], preferred_element_type=F32) + br_ref[...]


def _post(x2d, oa, ogs, lss, main, wa, wb, wo, gffn, wr, br, B, S, tm, layer):
    T = x2d.shape[0]
    nt = S // tm
    res_specs = [pl.BlockSpec((1, d, tm // d, DIL_W), lambda i, nt=nt: (i // nt, 0, i % nt, 0))
                 for _, d in DIL_PATTERN]
    assert oa.shape[3] == tm and oa.shape[2] * 2 == nt

    def oa_index(i):
        it = i % nt
        late = it >= nt // 2
        return (i // nt, late.astype(jnp.int32), jnp.where(late, nt - 1 - it, it), 0, 0)

    return pl.pallas_call(
        functools.partial(_post_kernel, tm=tm),
        grid=(T // tm,),
        in_specs=[pl.BlockSpec((tm, D_MODEL), lambda i: (i, 0)),
                  pl.BlockSpec((1, 1, 1, tm, D_MODEL), oa_index)]
                 + res_specs + res_specs
                 + [pl.BlockSpec((tm, 2 * D_MODEL), lambda i: (i, 0)),
                    _layer_spec((D_MODEL, D_MODEL), layer),
                    _layer_spec((DIL_W, D_MODEL), layer),
                    _layer_spec((D_MODEL, D_MODEL), layer),
                    _layer_spec((1, D_MODEL), layer),
                    _layer_spec((D_MODEL, 128), layer),
                    _layer_spec((1, 128), layer)],
        out_specs=[pl.BlockSpec((tm, D_MODEL), lambda i: (i, 0)),
                   pl.BlockSpec((tm * ROW_TILE, 128), lambda i: (i, 0)),
                   pl.BlockSpec((tm, 128), lambda i: (i, 0))],
        out_shape=[jax.ShapeDtypeStruct((T, D_MODEL), F32),
                   jax.ShapeDtypeStruct((T * ROW_TILE, 128), U32),
                   jax.ShapeDtypeStruct((T, 128), F32)],
        scratch_shapes=[pltpu.VMEM((12, tm, 128), F32)],
        compiler_params=_cparams(("parallel",)),
        name="post_attn",
    )(x2d, oa, *ogs, *lss, main, wa, wb, wo, gffn, wr, br)


def _route_kernel(lg_ref, ri_ref, rw_ref, cnt_ref, carry_ref, lower_ref, *, tm):
    i = pl.program_id(0)

    @pl.when(i == 0)
    def _():
        carry_ref[...] = jnp.zeros(carry_ref.shape, F32)
        r_i = lax.broadcasted_iota(jnp.int32, (tm, tm), 0)
        c_i = lax.broadcasted_iota(jnp.int32, (tm, tm), 1)
        lower_ref[...] = jnp.where(c_i < r_i, 1.0, 0.0).astype(BF16)

    lg = lg_ref[...]
    lane = lax.broadcasted_iota(jnp.int32, lg.shape, 1)
    lane_f = lane.astype(F32)
    ninf = jnp.float32(-jnp.inf)

    def first_max(vals):
        vmax = jnp.max(vals, axis=-1, keepdims=True)
        idx = jnp.min(jnp.where(vals == vmax, lane_f, 128.0), axis=-1, keepdims=True)
        return vmax, idx.astype(jnp.int32)

    gl = jnp.where(lane < N_GROUPS, lg, ninf)
    gmax, g_sel = first_max(gl)
    p_g = 1.0 / jnp.sum(jnp.exp(gl - gmax), axis=-1, keepdims=True)
    lo = N_GROUPS + g_sel * EXPERTS_PER_GROUP
    el = jnp.where(jnp.logical_and(lane >= lo, lane < lo + EXPERTS_PER_GROUP), lg, ninf)
    v0, i0 = first_max(el)
    v1, i1 = first_max(jnp.where(lane == i0, ninf, el))
    t = jnp.exp(v1 - v0)
    w0 = p_g / (1.0 + t)
    w1 = p_g * t / (1.0 + t)
    e0 = i0 - N_GROUPS
    e1 = i1 - N_GROUPS
    hit0 = lane == e0
    hit1 = lane == e1
    oh = jnp.where(jnp.logical_or(hit0, hit1), 1.0, 0.0).astype(F32)
    excl = jnp.dot(lower_ref[...], oh.astype(BF16), preferred_element_type=F32) + carry_ref[...]
    r0 = jnp.sum(jnp.where(hit0, excl, 0.0), axis=-1, keepdims=True).astype(jnp.int32)
    r1 = jnp.sum(jnp.where(hit1, excl, 0.0), axis=-1, keepdims=True).astype(jnp.int32)
    carry_ref[...] = carry_ref[...] + jnp.sum(oh, axis=0, keepdims=True)
    zi = jnp.zeros(lg.shape, jnp.int32)
    ri_ref[...] = jnp.where(lane == 0, e0, jnp.where(lane == 1, e1, jnp.where(lane == 2, r0, jnp.where(lane == 3, r1, zi))))
    rw_ref[...] = jnp.where(lane == 0, w0, jnp.where(lane == 1, w1, jnp.zeros(lg.shape, F32)))
    cnt_ref[...] = carry_ref[...]


def _route(lg, tm):
    T = lg.shape[0]
    return pl.pallas_call(
        functools.partial(_route_kernel, tm=tm),
        grid=(T // tm,),
        in_specs=[pl.BlockSpec((tm, 128), lambda i: (i, 0))],
        out_specs=[pl.BlockSpec((tm, 128), lambda i: (i, 0)),
                   pl.BlockSpec((tm, 128), lambda i: (i, 0)),
                   pl.BlockSpec((1, 128), lambda i: (0, 0))],
        out_shape=[jax.ShapeDtypeStruct((T, 128), jnp.int32),
                   jax.ShapeDtypeStruct((T, 128), F32),
                   jax.ShapeDtypeStruct((1, 128), F32)],
        scratch_shapes=[pltpu.VMEM((1, 128), F32), pltpu.VMEM((tm, tm), BF16)],
        compiler_params=_cparams(("arbitrary",)),
        name="route",
    )(lg)


def _sc_gather_rows(table, idx, n_out=None, row0=0):
    inverse = n_out is not None
    n_src = idx.shape[0]
    n = n_out if inverse else n_src
    n_workers = SC_CORES * SC_SUBCORES
    per_w = n // n_workers
    n_chunks = per_w // SC_CHUNK
    assert per_w * n_workers == n and n_chunks * SC_CHUNK == per_w and n_chunks % 2 == 0
    mesh = plsc.VectorSubcoreMesh(core_axis_name="c", subcore_axis_name="s",
                                  num_cores=SC_CORES, num_subcores=SC_SUBCORES)

    @functools.partial(
        pl.kernel, mesh=mesh,
        out_type=jax.ShapeDtypeStruct((n,) + table.shape[1:], table.dtype),
        scratch_types=[pltpu.VMEM((per_w,), jnp.int32),
                       pltpu.VMEM((SC_CHUNK,) + table.shape[1:], table.dtype),
                       pltpu.VMEM((SC_CHUNK,) + table.shape[1:], table.dtype),
                       pltpu.SemaphoreType.DMA, pltpu.SemaphoreType.DMA,
                       pltpu.VMEM((n_src if inverse else SC_LANES,), jnp.int32)],
        compiler_params=pltpu.CompilerParams(use_tc_tiling_on_sc=True, needs_layout_passes=not inverse),
        name="sc_dispatch_rows" if inverse else "sc_gather_rows",
    )
    def gather(table_hbm, idx_hbm, out_hbm, idx_v, rows_a, rows_b, sem_a, sem_b, map_v):
        wid = lax.axis_index("s") * SC_CORES + lax.axis_index("c")
        base = wid * per_w
        if inverse:
            pltpu.sync_copy(idx_hbm, map_v)
            lanes = lax.iota(jnp.int32, SC_LANES)

            @pl.loop(0, per_w // SC_LANES)
            def _(j):
                idx_v[pl.ds(j * SC_LANES, SC_LANES)] = lax.rem(row0 + base + j * SC_LANES + lanes, table.shape[0])

            @pl.loop(0, n_src // SC_LANES)
            def _(a):
                local = map_v[pl.ds(a * SC_LANES, SC_LANES)] - (row0 + base)
                mine = jnp.logical_and(local >= 0, local < per_w)
                plsc.store_scatter(idx_v, [local], lax.div(a * SC_LANES + lanes, TOP_K), mask=mine)
        else:
            pltpu.sync_copy(idx_hbm.at[pl.ds(base, per_w)], idx_v)

        def fetch(chunk, rows_v, sem):
            return pltpu.make_async_copy(table_hbm.at[idx_v.at[pl.ds(chunk * SC_CHUNK, SC_CHUNK)]], rows_v, sem)

        def flush(chunk, rows_v):
            pltpu.sync_copy(rows_v, out_hbm.at[pl.ds(base + chunk * SC_CHUNK, SC_CHUNK)])

        fetch(0, rows_a, sem_a).start()

        @pl.loop(0, n_chunks, step=2)
        def _(c):
            fetch(c + 1, rows_b, sem_b).start()
            fetch(c, rows_a, sem_a).wait()
            flush(c, rows_a)

            @pl.when(c + 2 < n_chunks)
            def _():
                fetch(c + 2, rows_a, sem_a).start()

            fetch(c + 1, rows_b, sem_b).wait()
            flush(c + 1, rows_b)

    return gather(table, idx)


def _rows_from_tiles(ref, first, n_rows, stride=ROW_TILE):
    words = [ref[pl.ds(first + c, n_rows, stride=stride), :] for c in range(ROW_TILE)]
    lo = [pltpu.bitcast(w << 16, F32) for w in words]
    hi = [pltpu.bitcast(w & U32(0xFFFF0000), F32) for w in words]
    return jnp.concatenate(lo + hi, axis=-1)


def _rows_to_tiles(ref, val):
    half = D_MODEL // 2

    def bits(x):
        return pltpu.bitcast(x.astype(BF16).astype(F32), U32)

    for c in range(ROW_TILE):
        lo = bits(val[:, c * 128:(c + 1) * 128])
        hi = bits(val[:, half + c * 128:half + (c + 1) * 128])
        ref[pl.ds(c, val.shape[0], stride=ROW_TILE), :] = hi | (lo >> 16)


def _ffn_kernel(be_ref, nx_ref, par_ref, nvb_ref, x_ref, w1_hbm, w3_hbm, w2_hbm, prev_ref, y_ref,
                w1f_ref, w3f_ref, w2f_ref, w1b_ref, w3b_ref, w2b_ref, sem, *, layer):
    b = pl.program_id(0)
    nvb = nvb_ref[0]
    expert = be_ref[b]
    slot = par_ref[b]
    new_expert = jnp.logical_or(b == 0, expert != be_ref[jnp.maximum(b - 1, 0)])

    def fetch(e, sl):
        return [pltpu.make_async_copy(w_hbm.at[layer, e], wf_ref.at[sl], sem.at[sl])
                for w_hbm, wf_ref in ((w1_hbm, w1f_ref), (w3_hbm, w3f_ref), (w2_hbm, w2f_ref))]

    @pl.when(jnp.logical_and(b == 0, nvb > 0))
    def _():
        for cp in fetch(expert, slot):
            cp.start()

    @pl.when(jnp.logical_and(b < nvb, new_expert))
    def _():
        for cp in fetch(expert, slot):
            cp.wait()

        @pl.when(nx_ref[b] >= 0)
        def _():
            for cp in fetch(nx_ref[b], 1 - slot):
                cp.start()

        w1b_ref[...] = w1f_ref[slot].astype(BF16)
        w3b_ref[...] = w3f_ref[slot].astype(BF16)
        w2b_ref[...] = w2f_ref[slot].astype(BF16)

    @pl.when(b < nvb)
    def _():
        xb = _rows_from_tiles(x_ref, 0, FFN_BM).astype(BF16)
        h1 = jnp.dot(xb, w1b_ref[...], preferred_element_type=F32)
        h3 = jnp.dot(xb, w3b_ref[...], preferred_element_type=F32)
        a = (jax.nn.silu(h1) * h3).astype(BF16)
        _rows_to_tiles(y_ref, jnp.dot(a, w2b_ref[...], preferred_element_type=F32))

    @pl.when(b >= nvb)
    def _():
        y_ref[...] = jnp.zeros(y_ref.shape, U32)


def _expert_ffn(block_e, next_e, parity, nvb, xs, prev, first, n_blocks, w1, w3, w2, layer):
    nb = block_e.shape[0]
    if prev is None:
        prev = xs

    def used(b, *prefetch):
        return jnp.minimum(b, jnp.maximum(prefetch[-1][0] - 1, 0))

    return pl.pallas_call(
        functools.partial(_ffn_kernel, layer=layer),
        grid_spec=pltpu.PrefetchScalarGridSpec(
            num_scalar_prefetch=4,
            grid=(nb,),
            in_specs=[
                pl.BlockSpec((FFN_BM * ROW_TILE, 128), lambda b, *prefetch: (used(b, *prefetch), 0)),
                pl.BlockSpec(memory_space=pl.ANY),
                pl.BlockSpec(memory_space=pl.ANY),
                pl.BlockSpec(memory_space=pl.ANY),
                pl.BlockSpec(memory_space=pl.ANY),
            ],
            out_specs=pl.BlockSpec((FFN_BM * ROW_TILE, 128), lambda b, *prefetch: (first + b, 0)),
            scratch_shapes=[pltpu.VMEM((2, D_MODEL, EXPERT_FF), F32), pltpu.VMEM((2, D_MODEL, EXPERT_FF), F32),
                            pltpu.VMEM((2, EXPERT_FF, D_MODEL), F32),
                            pltpu.VMEM((D_MODEL, EXPERT_FF), BF16), pltpu.VMEM((D_MODEL, EXPERT_FF), BF16),
                            pltpu.VMEM((EXPERT_FF, D_MODEL), BF16),
                            pltpu.SemaphoreType.DMA((2,))],
        ),
        out_shape=jax.ShapeDtypeStruct((n_blocks * FFN_BM * ROW_TILE, 128), U32),
        input_output_aliases={} if first == 0 else {8: 0},
        compiler_params=_cparams(("arbitrary",)),
        name="expert_ffn",
    )(block_e, next_e, parity, nvb, xs, w1, w3, w2, prev)


def _comb_kernel(yg_ref, x1_ref, rw_ref, p_ref, gple_ref, wpg_ref, wpp_ref, gout_ref, prev_ref, o_ref, *, tm, final):
    w = rw_ref[...]
    y0 = _rows_from_tiles(yg_ref, 0, tm, stride=TOP_K * ROW_TILE)
    y1 = _rows_from_tiles(yg_ref, ROW_TILE, tm, stride=TOP_K * ROW_TILE)
    x2 = x1_ref[...] + (y0 * w[:, 0:1] + y1 * w[:, 1:2])
    e = jnp.dot(p_ref[...].astype(BF16), wpp_ref[...], preferred_element_type=F32)
    gate = jax.nn.sigmoid(jnp.dot(_rms(x2, gple_ref[...]).astype(BF16), wpg_ref[...], preferred_element_type=F32))
    x3 = x2 + gate * e
    o_ref[...] = _rms(x3, gout_ref[...]) if final else x3


def _combine_ple(yg, prev, first, x1, rw, p2d, gple, wpg, wpp, gout, tm, layer):
    T = x1.shape[0]
    nt = T // tm
    off = first // tm
    if prev is None:
        prev = x1
    return pl.pallas_call(
        functools.partial(_comb_kernel, tm=tm, final=layer == DEPTH - 1),
        grid=(yg.shape[0] // (tm * TOP_K * ROW_TILE),),
        in_specs=[
            pl.BlockSpec((tm * TOP_K * ROW_TILE, 128), lambda i: (i, 0)),
            pl.BlockSpec((tm, D_MODEL), lambda i: (off + i, 0)),
            pl.BlockSpec((tm, 128), lambda i: (off + i, 0)),
            pl.BlockSpec((tm, PLE_DIM), lambda i: (layer * nt + off + i, 0)),
            _layer_spec((1, D_MODEL), layer),
            _layer_spec((D_MODEL, D_MODEL), layer),
            _layer_spec((PLE_DIM, D_MODEL), layer),
            pl.BlockSpec((1, D_MODEL), lambda i: (0, 0)),
            pl.BlockSpec(memory_space=pl.ANY),
        ],
        out_specs=pl.BlockSpec((tm, D_MODEL), lambda i: (off + i, 0)),
        out_shape=jax.ShapeDtypeStruct((T, D_MODEL), F32),
        input_output_aliases={} if first == 0 else {8: 0},
        compiler_params=_cparams(("parallel",)),
        name="combine_ple",
    )(yg, x1, rw, p2d, gple, wpg, wpp, gout, prev)


def _rope_tables(pos, theta, rot_dim, offset, span, scale):
    half = rot_dim // 2
    inv = jnp.float32(theta) ** (-jnp.arange(half, dtype=F32) * 2.0 / rot_dim)
    ang = pos.astype(F32)[:, None] * inv
    rel = np.arange(128) - offset
    rot = (rel >= 0) & (rel < span)
    spread = np.zeros((half, 128), np.float32)
    spread[rel[rot] % half, np.nonzero(rot)[0]] = 1.0
    to_lanes = lambda t: jnp.dot(t, spread, precision=lax.Precision.HIGHEST)
    cos, sin = to_lanes(jnp.cos(ang)) + (~rot).astype(np.float32), to_lanes(jnp.sin(ang))
    return jnp.concatenate([cos, -sin, sin], axis=1) * jnp.float32(scale)


def _prep_w_in(w):
    c_q = w[..., 0:512]
    c_kv = w[..., 512:768]
    k_pe = w[..., 768:800]
    dil = w[..., 800:800 + 2304].reshape(w.shape[:-1] + (DIL_GROUPS, 3, DIL_HEADS, DIL_HEAD_DIM))

    def qk_layout(t):
        flat = lambda u: u.reshape(u.shape[:-2] + (-1,))
        return jnp.concatenate([flat(t[..., :DIL_HALF]), flat(t[..., PARTIAL_ROT:PARTIAL_ROT + DIL_REST]),
                                flat(t[..., DIL_HALF:PARTIAL_ROT]), flat(t[..., PARTIAL_ROT + DIL_REST:])], axis=-1)

    dil = jnp.concatenate([jnp.concatenate([qk_layout(dil[..., g, 0, :, :]), qk_layout(dil[..., g, 1, :, :]),
                                            dil[..., g, 2, :, :].reshape(w.shape[:-1] + (DIL_W,))], axis=-1)
                           for g in range(DIL_GROUPS)], axis=-1)
    gates = w[..., 3104:5152]
    zeros = lambda n: jnp.zeros(w.shape[:-1] + (n,), w.dtype)
    lo, hi = k_pe[..., :MLA_HALF], k_pe[..., MLA_HALF:]
    pad = MLA_SLOT - MLA_NOPE - MLA_ROPE
    kslot = jnp.concatenate([zeros(MLA_NOPE), lo, lo, zeros(pad), zeros(MLA_NOPE), hi, hi, zeros(pad)], axis=-1)
    return jnp.concatenate([gates, c_q, c_kv, kslot, dil], axis=-1).astype(BF16)


def _pad_heads(w, width):
    w = jnp.pad(w, ((0, 0), (0, 0), (0, 0), (0, MLA_SLOT - width)))
    return w.reshape(w.shape[0], w.shape[1], MLA_HEADS * MLA_SLOT).astype(BF16)


def _prep_w_q(w):
    w = w.reshape(DEPTH, MLA_Q_LORA, MLA_HEADS // 2, 2, MLA_NOPE + MLA_ROPE)
    nope = w[..., :MLA_NOPE]
    lo = w[..., MLA_NOPE:MLA_NOPE + MLA_HALF]
    hi = w[..., MLA_NOPE + MLA_HALF:]
    both = lambda t: t.reshape(t.shape[:-2] + (2 * MLA_HALF,))
    zeros = jnp.zeros(w.shape[:3] + (MLA_SLOT - MLA_NOPE - MLA_ROPE,), w.dtype)
    slot0 = jnp.concatenate([nope[..., 0, :], both(lo), zeros], axis=-1)
    slot1 = jnp.concatenate([nope[..., 1, :], both(hi), zeros], axis=-1)
    return jnp.stack([slot0, slot1], axis=-2).reshape(DEPTH, MLA_Q_LORA, MLA_HEADS * MLA_SLOT).astype(BF16)


def _prep_w_kv(w):
    w = w.reshape(DEPTH, MLA_KV_LORA, MLA_HEADS, MLA_NOPE + MLA_V)
    return _pad_heads(w[..., :MLA_NOPE], MLA_NOPE), _pad_heads(w[..., MLA_NOPE:], MLA_V)


def _dest_kernel(ri_ref, ps_ref, o_ref):
    ri = ri_ref[...].astype(F32)
    lane = lax.broadcasted_iota(jnp.int32, ri.shape, 1)
    ps = ps_ref[...]

    def col(k):
        return jnp.sum(jnp.where(lane == k, ri, 0.0), axis=-1, keepdims=True)

    def dest(k):
        start = jnp.sum(jnp.where(lane == col(k).astype(jnp.int32), ps, 0.0), axis=-1, keepdims=True)
        return (start + col(2 + k)).astype(jnp.int32)

    o_ref[...] = jnp.where(lane == 0, dest(0), jnp.where(lane == 1, dest(1), 0))


def _dest_rows(ri, pstarts, tm):
    T = ri.shape[0]
    ps = jnp.zeros((1, 128), F32).at[0, :N_EXPERTS].set(pstarts.astype(F32))
    return pl.pallas_call(
        _dest_kernel,
        grid=(T // tm,),
        in_specs=[pl.BlockSpec((tm, 128), lambda i: (i, 0)), pl.BlockSpec((1, 128), lambda i: (0, 0))],
        out_specs=pl.BlockSpec((tm, 128), lambda i: (i, 0)),
        out_shape=jax.ShapeDtypeStruct((T, 128), jnp.int32),
        compiler_params=_cparams(("parallel",)),
        name="dest_rows",
    )(ri, ps)


def _dispatch_plan(ri, cnt, T, tm):
    counts = cnt[0, :N_EXPERTS].astype(jnp.int32)
    pcounts = (counts + FFN_BM - 1) // FFN_BM * FFN_BM
    pends = jnp.cumsum(pcounts)
    pstarts = pends - pcounts
    dest = _dest_rows(ri, pstarts, tm)[:, 0:TOP_K]
    n_blocks = (T * TOP_K) // FFN_BM + N_EXPERTS
    first_row = jnp.arange(n_blocks, dtype=jnp.int32) * FFN_BM
    block_e = jnp.minimum(jnp.sum((pends[None, :] <= first_row[:, None]).astype(jnp.int32), axis=1), N_EXPERTS - 1)
    nvb = pends[-1] // FFN_BM
    after = pends // FFN_BM
    run_of_expert = jnp.cumsum((pcounts > 0).astype(jnp.int32)) - 1
    parity = (run_of_expert[block_e] % 2).astype(jnp.int32)
    per_range = n_blocks // FFN_RANGES
    ffn_plans = []
    for r in range(FFN_RANGES):
        lo, hi = r * per_range, (r + 1) * per_range
        end = jnp.minimum(nvb, hi)
        next_of_expert = jnp.where(after < end, block_e[jnp.minimum(after, n_blocks - 1)], -1)
        ffn_plans.append((block_e[lo:hi], next_of_expert[block_e[lo:hi]].astype(jnp.int32), parity[lo:hi],
                          jnp.clip(nvb - lo, 0, per_range).astype(jnp.int32).reshape(1)))
    return dest.reshape(T * TOP_K), n_blocks, ffn_plans


def kernel(x, p, positions, g_mix, w_in, g_q_lat, w_q_up, g_kv_lat, w_kv_up, w_branch_a, w_branch_b, w_out, g_ffn, w_router_grp, b_router_grp, w_router_exp, b_router_exp, w_exp_gate, w_exp_up, w_exp_down, g_ple, w_ple_gate, w_ple_proj, g_final):
    B, S, D = x.shape
    T = B * S
    pos = positions.reshape(T)
    dil_scale = DIL_HEAD_DIM ** -0.5 * LOG2E
    mla_scale = (MLA_NOPE + MLA_ROPE) ** -0.5 * LOG2E
    tab_dil = jnp.concatenate([_rope_tables(pos, ROPE_THETA, PARTIAL_ROT, 0, DIL_ROT_LANES, sc)
                               for sc in (dil_scale, 1.0)], axis=1)
    tab_mla = jnp.concatenate([_rope_tables(pos, MLA_ROPE_THETA, MLA_ROPE, MLA_NOPE, MLA_ROPE, sc)
                               for sc in (mla_scale, 1.0)], axis=1)

    wq = _prep_w_q(w_q_up)
    wk, wv = _prep_w_kv(w_kv_up)
    wr = jnp.concatenate([w_router_grp, w_router_exp, jnp.zeros((DEPTH, D, 128 - N_GROUPS - N_EXPERTS), F32)],
                         axis=-1).astype(BF16)
    br = jnp.concatenate([b_router_grp, b_router_exp.reshape(DEPTH, N_EXPERTS),
                          jnp.zeros((DEPTH, 128 - N_GROUPS - N_EXPERTS), F32)], axis=-1).reshape(DEPTH, 1, 128)
    wa, wb, wo = w_branch_a.astype(BF16), w_branch_b.astype(BF16), w_out.astype(BF16)
    wpg, wpp = w_ple_gate.astype(BF16), w_ple_proj.astype(BF16)
    gains = lambda g: g.reshape(DEPTH, 1, -1)

    xc = x.reshape(T, D)
    for i in range(DEPTH):
        main, d0, d1, d2 = _in_proj(xc, gains(g_mix)[i:i + 1], _prep_w_in(w_in[i:i + 1]), tab_dil, B, S, TM_IN, 0)
        q, k, v = _mla_qkv(main, gains(g_q_lat), gains(g_kv_lat), wq, wk, wv, tab_mla, TM_QKV, i)
        o_a = _mla_attn(q, k, v, B, S, TQ)
        ogs, lss = [], []
        for (_, dil), qkv in zip(DIL_PATTERN, (d0, d1, d2)):
            og, ls = _dil_attn(qkv, B, dil, S // dil)
            ogs.append(og)
            lss.append(ls)
        x1, h2, lg = _post(xc, o_a, ogs, lss, main, wa, wb, wo, gains(g_ffn), wr, br, B, S, TM_POST, i)
        ri, rw, cnt = _route(lg, TM_ROUTE)
        dest, n_blocks, ffn_plans = _dispatch_plan(ri, cnt, T, TM_ROUTE)
        range_blocks = n_blocks // FFN_RANGES
        ys = None
        for r, plan in enumerate(ffn_plans):
            xs = _sc_gather_rows(h2.reshape(T, ROW_TILE, 128), dest, n_out=range_blocks * FFN_BM,
                                 row0=r * range_blocks * FFN_BM)
            ys = _expert_ffn(*plan, xs.reshape(-1, 128), ys, r * range_blocks, n_blocks,
                             w_exp_gate, w_exp_up, w_exp_down, i)
        xc = None
        for first in (0, T // 2):
            yg = _sc_gather_rows(ys.reshape(-1, ROW_TILE, 128), dest[first * TOP_K:(first + T // 2) * TOP_K])
            xc = _combine_ple(yg.reshape(-1, 128), xc, first, x1, rw, p.reshape(DEPTH * T, PLE_DIM), gains(g_ple),
                              wpg, wpp, g_final.reshape(1, D), TM_COMB, i)
    return xc.reshape(B, S, D)
```

```python
import functools
import math

import jax
import jax.numpy as jnp
import numpy as np
from jax import lax
from jax.experimental import pallas as pl
from jax.experimental.pallas import tpu as pltpu
from jax.experimental.pallas import tpu_sc as plsc

F32 = jnp.float32
BF16 = jnp.bfloat16

D_MODEL = 1024
DEPTH = 4
RMS_EPS = 1e-6
NEG = -1e30
LOG2E = math.log2(math.e)

MLA_HEADS = 16
MLA_Q_LORA = 512
MLA_KV_LORA = 256
MLA_NOPE = 64
MLA_ROPE = 32
MLA_V = 64
MLA_ROPE_THETA = 10000.0
MLA_SLOT = 128
MLA_HALF = MLA_ROPE // 2


def _mla_head_lanes(lane, head):
    l = lane % MLA_SLOT
    nope = jnp.logical_and(lane // MLA_SLOT == head, l < MLA_NOPE)
    rope = jnp.logical_and(l >= MLA_NOPE + head * MLA_HALF, l < MLA_NOPE + (head + 1) * MLA_HALF)
    return jnp.logical_or(nope, rope)

DIL_PATTERN = ((128, 1), (512, 4), (2048, 16))
DIL_GROUPS = 3
DIL_HEADS = 4
DIL_HEAD_DIM = 64
DIL_W = DIL_HEADS * DIL_HEAD_DIM
DIL_BLK = 128
DIL_UNROLL = 16
ROPE_THETA = 500000.0
PARTIAL_ROT = DIL_HEAD_DIM // 4
DIL_HALF = PARTIAL_ROT // 2
DIL_ROT_LANES = DIL_HEADS * DIL_HALF
DIL_REST = (DIL_HEAD_DIM - PARTIAL_ROT) // 2


def _dil_qk_head_of_lane(lane):
    l = lane % 128
    plain = l - DIL_ROT_LANES
    plain_head = sum((plain >= h * DIL_REST).astype(jnp.int32) for h in range(1, DIL_HEADS))
    return jnp.where(l < DIL_ROT_LANES, l // DIL_HALF, plain_head)

N_GROUPS = 8
EXPERTS_PER_GROUP = 8
N_EXPERTS = 64
TOP_K = 2
EXPERT_FF = 256
PLE_DIM = 256

IN_TN = 768
MAIN_COLS = 3072
IN_COLS_PAD = MAIN_COLS + DIL_GROUPS * 3 * DIL_W
N_MAIN_TILES = MAIN_COLS // IN_TN

FFN_BM = 256
FFN_RANGES = 2
ROW_TILE = D_MODEL // 256
U32 = jnp.uint32
VMEM_LIMIT = 48 * 1024 * 1024
TM_IN, TM_QKV, TQ, TM_POST, TM_ROUTE, TM_COMB = 512, 512, 512, 512, 1024, 512

SC_CORES = 2
SC_SUBCORES = 16
SC_LANES = 16
SC_CHUNK = 64


def _cparams(sem):
    return pltpu.CompilerParams(dimension_semantics=sem, vmem_limit_bytes=VMEM_LIMIT)


def _layer_spec(shape, layer, col=None):
    def index(*grid_idx):
        return (layer,) + (0,) * (len(shape) - 1) + ((col(*grid_idx),) if col else (0,))

    return pl.BlockSpec((None,) + tuple(shape), index)


def _rms(x, g):
    return x * lax.rsqrt(jnp.mean(x * x, axis=-1, keepdims=True) + RMS_EPS) * g


def _in_kernel(x_ref, g_ref, w_ref, tab_ref, main_ref, d0_ref, d1_ref, d2_ref, xn_ref, acc_ref):
    j = pl.program_id(1)

    @pl.when(j == 0)
    def _():
        xn_ref[...] = _rms(x_ref[...], g_ref[...]).astype(BF16)

    acc = jnp.dot(xn_ref[...], w_ref[...], preferred_element_type=F32)

    @pl.when(j < N_MAIN_TILES)
    def _():
        main_ref[...] = acc.astype(BF16)

    def dil_tile(out_ref, dil):
        chunk = lambda c: acc[:, c * 128:(c + 1) * 128]
        for base, t0 in ((0, 0), (2, 3)):
            cos, sin_n, sin_p = [tab_ref[:, (t0 + t) * 128:(t0 + t + 1) * 128] for t in range(3)]
            lo, hi = chunk(base), chunk(base + 1)
            acc_ref[base] = lo * cos + hi * sin_n
            acc_ref[base + 1] = hi * cos + lo * sin_p
        for c in range(4, 6):
            acc_ref[c] = chunk(c)
        rows = acc_ref.shape[1] // dil
        for r in range(dil):
            for c in range(6):
                out_ref[0, r, :, c * 128:(c + 1) * 128] = acc_ref[c, pl.ds(r, rows, stride=dil), :].astype(BF16)

    for gi, (_, dil) in enumerate(DIL_PATTERN):
        pl.when(j == N_MAIN_TILES + gi)(functools.partial(dil_tile, (d0_ref, d1_ref, d2_ref)[gi], dil))


def _in_proj(x2d, g, w, tab, B, S, tm, layer):
    T = x2d.shape[0]
    nt = S // tm
    dil_shapes = [jax.ShapeDtypeStruct((B, d, S // d, 3 * DIL_W), BF16) for _, d in DIL_PATTERN]
    dil_specs = [pl.BlockSpec((1, d, tm // d, 3 * DIL_W), lambda i, j, nt=nt: (i // nt, 0, i % nt, 0))
                 for _, d in DIL_PATTERN]
    return pl.pallas_call(
        _in_kernel,
        grid=(T // tm, IN_COLS_PAD // IN_TN),
        in_specs=[
            pl.BlockSpec((tm, D_MODEL), lambda i, j: (i, 0)),
            _layer_spec((1, D_MODEL), layer),
            _layer_spec((D_MODEL, IN_TN), layer, col=lambda i, j: j),
            pl.BlockSpec((tm, 6 * 128), lambda i, j: (i, 0)),
        ],
        out_specs=[pl.BlockSpec((tm, IN_TN), lambda i, j: (i, jnp.minimum(j, N_MAIN_TILES - 1)))] + dil_specs,
        out_shape=[jax.ShapeDtypeStruct((T, MAIN_COLS), BF16)] + dil_shapes,
        scratch_shapes=[pltpu.VMEM((tm, D_MODEL), BF16), pltpu.VMEM((IN_TN // 128, tm, 128), F32)],
        compiler_params=_cparams(("parallel", "arbitrary")),
        name="in_proj",
    )(x2d, g, w, tab)


def _qkv_kernel(cq_ref, ckv_ref, kpe_ref, gq_ref, gkv_ref, wq_ref, wk_ref, wv_ref, tab_ref,
                q_ref, k_ref, v_ref):
    qn = _rms(cq_ref[...].astype(F32), gq_ref[...]).astype(BF16)
    kvn = _rms(ckv_ref[...].astype(F32), gkv_ref[...]).astype(BF16)
    qacc = jnp.dot(qn, wq_ref[...], preferred_element_type=F32)
    kacc = jnp.dot(kvn, wk_ref[...], preferred_element_type=F32)
    vacc = jnp.dot(kvn, wv_ref[...], preferred_element_type=F32)
    lane_v = lax.broadcasted_iota(jnp.int32, vacc.shape, 1)
    v_ref[...] = jnp.where(lane_v % MLA_SLOT < MLA_V, vacc, 1.0).astype(BF16)
    cq, snq, spq, ck, snk, spk = [tab_ref[:, t * 128:(t + 1) * 128] for t in range(6)]
    kpe_lo = kpe_ref[:, 0:128].astype(F32)
    kpe_hi = kpe_ref[:, 128:256].astype(F32)
    krot = (kpe_lo * ck + kpe_hi * snk, kpe_hi * ck + kpe_lo * spk)
    for pair in range(MLA_HEADS // 2):
        s0 = slice(2 * pair * MLA_SLOT, (2 * pair + 1) * MLA_SLOT)
        s1 = slice((2 * pair + 1) * MLA_SLOT, (2 * pair + 2) * MLA_SLOT)
        q0, q1 = qacc[:, s0], qacc[:, s1]
        q_ref[:, s0] = (q0 * cq + q1 * snq).astype(BF16)
        q_ref[:, s1] = (q1 * cq + q0 * spq).astype(BF16)
        k_ref[:, s0] = (kacc[:, s0] + krot[0]).astype(BF16)
        k_ref[:, s1] = (kacc[:, s1] + krot[1]).astype(BF16)


def _mla_qkv(main, gq, gkv, wq, wk, wv, tab, tm, layer):
    T = main.shape[0]
    return pl.pallas_call(
        _qkv_kernel,
        grid=(T // tm,),
        in_specs=[
            pl.BlockSpec((tm, MLA_Q_LORA), lambda i: (i, 2048 // MLA_Q_LORA)),
            pl.BlockSpec((tm, MLA_KV_LORA), lambda i: (i, 2560 // MLA_KV_LORA)),
            pl.BlockSpec((tm, 256), lambda i: (i, 2816 // 256)),
            _layer_spec((1, MLA_Q_LORA), layer),
            _layer_spec((1, MLA_KV_LORA), layer),
            _layer_spec((MLA_Q_LORA, MLA_HEADS * MLA_SLOT), layer),
            _layer_spec((MLA_KV_LORA, MLA_HEADS * MLA_SLOT), layer),
            _layer_spec((MLA_KV_LORA, MLA_HEADS * MLA_SLOT), layer),
            pl.BlockSpec((tm, 6 * 128), lambda i: (i, 0)),
        ],
        out_specs=[
            pl.BlockSpec((tm, MLA_HEADS * MLA_SLOT), lambda i: (i, 0)),
            pl.BlockSpec((tm, MLA_HEADS * MLA_SLOT), lambda i: (i, 0)),
            pl.BlockSpec((tm, MLA_HEADS * MLA_SLOT), lambda i: (i, 0)),
        ],
        out_shape=[
            jax.ShapeDtypeStruct((T, MLA_HEADS * MLA_SLOT), BF16),
            jax.ShapeDtypeStruct((T, MLA_HEADS * MLA_SLOT), BF16),
            jax.ShapeDtypeStruct((T, MLA_HEADS * MLA_SLOT), BF16),
        ],
        compiler_params=_cparams(("parallel",)),
        name="mla_qkv",
    )(main, main, main, gq, gkv, wq, wk, wv, tab)


def _mla_attn_kernel(qa_ref, qb_ref, k_ref, v_ref, o_ref, m_ref, acc_ref, *, tq):
    p_id = pl.program_id(2)
    nq = k_ref.shape[1] // tq
    lane = lax.broadcasted_iota(jnp.int32, (tq, 2 * MLA_SLOT), 1)
    qh = []
    for q_ref in (qa_ref, qb_ref):
        qp = q_ref[0]
        zero = jnp.zeros_like(qp)
        qh.append([jnp.where(_mla_head_lanes(lane, h), qp, zero) for h in range(2)])
    hq = tq // 2
    dn = (((1,), (1,)), ((), ()))

    def step(blk, start, width, r0, mask, first=False):
        kb = k_ref[0, pl.ds(start, width), :]
        vb = v_ref[0, pl.ds(start, width), :]
        rows = tq - r0
        s_pair = lax.dot_general(jnp.concatenate([qh[blk][0][r0:], qh[blk][1][r0:]], axis=0), kb, dn,
                                 preferred_element_type=F32)
        for h in range(2):
            s = s_pair[h * rows:(h + 1) * rows]
            if mask is not None:
                s = jnp.where(mask, s, NEG)
            m_cur = jnp.max(s, axis=-1, keepdims=True)
            if first:
                m_new = jnp.broadcast_to(m_cur, (tq - r0, 128))
            else:
                m_prev = m_ref[blk, h, r0:, :]
                m_new = jnp.maximum(m_prev, m_cur)
            p = jnp.exp2(s - jnp.concatenate([m_new] * (width // 128), axis=-1))
            pv = jnp.dot(p.astype(BF16), vb[:, h * MLA_SLOT:(h + 1) * MLA_SLOT], preferred_element_type=F32)
            if first:
                acc_ref[blk, h, r0:, :] = pv
            else:
                acc_ref[blk, h, r0:, :] = jnp.exp2(m_prev - m_new) * acc_ref[blk, h, r0:, :] + pv
            m_ref[blk, h, r0:, :] = m_new

    def causal(rows):
        return lax.broadcasted_iota(jnp.int32, (rows, hq), 1) <= lax.broadcasted_iota(jnp.int32, (rows, hq), 0)

    def q_blocks(n_a):
        n_b = nq - 1 - n_a
        for blk, n_full in ((1, n_b), (0, n_a)):
            for j in range(n_full):
                step(blk, j * tq, tq, 0, None, first=j == 0)
            step(blk, n_full * tq, hq, 0, causal(tq), first=n_full == 0)
            step(blk, n_full * tq + hq, hq, hq, causal(hq))

    for n_a in range(nq // 2):
        pl.when(p_id == n_a)(functools.partial(q_blocks, n_a))
    lane_o = lax.broadcasted_iota(jnp.int32, (tq, 2 * MLA_V), 1)
    for blk in range(2):
        outs = []
        for h in range(2):
            a = acc_ref[blk, h]
            outs.append(a / pltpu.roll(a, MLA_V, 1))
        o_ref[0, blk, 0] = jnp.where(lane_o < MLA_V, outs[0], pltpu.roll(outs[1], MLA_V, 1)).astype(BF16)


def _mla_attn(q, k, v, B, S, tq):
    nq = S // tq
    q = q.reshape(B, S, MLA_HEADS * MLA_SLOT)
    k = k.reshape(B, S, MLA_HEADS * MLA_SLOT)
    v = v.reshape(B, S, MLA_HEADS * MLA_SLOT)
    return pl.pallas_call(
        functools.partial(_mla_attn_kernel, tq=tq),
        grid=(B, MLA_HEADS // 2, nq // 2),
        in_specs=[
            pl.BlockSpec((1, tq, 2 * MLA_SLOT), lambda b, h, p: (b, p, h)),
            pl.BlockSpec((1, tq, 2 * MLA_SLOT), lambda b, h, p: (b, nq - 1 - p, h)),
            pl.BlockSpec((1, S, 2 * MLA_SLOT), lambda b, h, p: (b, 0, h)),
            pl.BlockSpec((1, S, 2 * MLA_SLOT), lambda b, h, p: (b, 0, h)),
        ],
        out_specs=pl.BlockSpec((1, 2, 1, tq, 2 * MLA_V), lambda b, h, p: (b, 0, p, 0, h)),
        out_shape=jax.ShapeDtypeStruct((B, 2, nq // 2, tq, MLA_HEADS * MLA_V), BF16),
        scratch_shapes=[pltpu.VMEM((2, 2, tq, 128), F32), pltpu.VMEM((2, 2, tq, MLA_SLOT), F32)],
        compiler_params=_cparams(("parallel", "parallel", "arbitrary")),
        name="mla_attn",
    )(q, q, k, v)


def _dil_kernel(qkv_ref, o_ref, lse_ref, bias_ref, *, nb):
    dil = qkv_ref.shape[1]
    width = 2 * DIL_BLK if nb > 1 else DIL_BLK
    row = lax.broadcasted_iota(jnp.int32, (DIL_BLK, width), 0)
    col = lax.broadcasted_iota(jnp.int32, (DIL_BLK, width), 1)
    bias_ref[0] = jnp.where(col <= row, 0.0, NEG)
    if nb > 1:
        later = jnp.logical_or(jnp.logical_and(col >= DIL_BLK, col - DIL_BLK <= row),
                               jnp.logical_and(col < DIL_BLK, col >= row))
        bias_ref[1] = jnp.where(later, 0.0, NEG)
    lane = lax.broadcasted_iota(jnp.int32, (DIL_BLK, DIL_W), 1)
    head_of_lane = lane // DIL_HEAD_DIM
    qk_head_of_lane = _dil_qk_head_of_lane(lane)
    dn = (((1,), (1,)), ((), ()))

    def by_head(parts):
        out = parts[DIL_HEADS - 1]
        for h in range(DIL_HEADS - 2, -1, -1):
            out = jnp.where(head_of_lane == h, parts[h], out)
        return out

    def unit(u):
        r = u // nb
        n = u % nb
        q0 = pl.multiple_of(n * DIL_BLK, DIL_BLK)
        q = qkv_ref[0, r, pl.ds(q0, DIL_BLK), 0:DIL_W]
        if nb > 1:
            w0 = pl.multiple_of(jnp.maximum(n - 1, 0) * DIL_BLK, DIL_BLK)
            bias = bias_ref[jnp.minimum(n, 1)]
        else:
            w0 = 0
            bias = bias_ref[0]
        kw = qkv_ref[0, r, pl.ds(w0, width), DIL_W:2 * DIL_W]
        vw = qkv_ref[0, r, pl.ds(w0, width), 2 * DIL_W:3 * DIL_W]
        zero = jnp.zeros_like(q)
        qs = jnp.concatenate([jnp.where(qk_head_of_lane == h, q, zero) for h in range(DIL_HEADS)], axis=0)
        s = lax.dot_general(qs, kw, dn, preferred_element_type=F32)
        s = (s.reshape(DIL_HEADS, DIL_BLK, width) + bias[None]).reshape(DIL_HEADS * DIL_BLK, width)
        m = jnp.max(s, axis=-1, keepdims=True)
        e = jnp.exp2(s - m)
        den = jnp.sum(e, axis=-1, keepdims=True)
        pv = jnp.dot(e.astype(BF16), vw, preferred_element_type=F32)
        lse = m + jnp.log2(den)
        blk = lambda t, h: t[h * DIL_BLK:(h + 1) * DIL_BLK]
        o = by_head([blk(pv, h) for h in range(DIL_HEADS)]) / by_head(
            [jnp.broadcast_to(blk(den, h), (DIL_BLK, DIL_W)) for h in range(DIL_HEADS)])
        o_ref[0, r, pl.ds(q0, DIL_BLK), :] = o.astype(BF16)
        lse_ref[0, r, pl.ds(q0, DIL_BLK), :] = by_head(
            [jnp.broadcast_to(blk(lse, h), (DIL_BLK, DIL_W)) for h in range(DIL_HEADS)])

    def body(t, carry):
        for u in range(DIL_UNROLL):
            unit(DIL_UNROLL * t + u)
        return carry

    lax.fori_loop(0, dil * nb // DIL_UNROLL, body, 0)


def _dil_attn(qkv, B, dil, L):
    nb = L // DIL_BLK
    return pl.pallas_call(
        functools.partial(_dil_kernel, nb=nb),
        grid=(B,),
        in_specs=[pl.BlockSpec((1, dil, L, 3 * DIL_W), lambda b: (b, 0, 0, 0))],
        out_specs=[
            pl.BlockSpec((1, dil, L, DIL_W), lambda b: (b, 0, 0, 0)),
            pl.BlockSpec((1, dil, L, DIL_W), lambda b: (b, 0, 0, 0)),
        ],
        out_shape=[
            jax.ShapeDtypeStruct((B, dil, L, DIL_W), BF16),
            jax.ShapeDtypeStruct((B, dil, L, DIL_W), F32),
        ],
        scratch_shapes=[pltpu.VMEM((2, DIL_BLK, 2 * DIL_BLK if nb > 1 else DIL_BLK), F32)],
        compiler_params=_cparams(("parallel",)),
        name=f"dil_attn_d{dil}",
    )(qkv)


def _post_kernel(x_ref, oa_ref, og0_ref, og1_ref, og2_ref, ls0_ref, ls1_ref, ls2_ref, gates_ref,
                 wa_ref, wb_ref, wo_ref, gffn_ref, wr_ref, br_ref,
                 x1_ref, h2_ref, lg_ref, ob_ref, *, tm):
    ls =[r[0] for r in (ls0_ref, ls1_ref, ls2_ref)]
    og = [r[0] for r in (og0_ref, og1_ref, og2_ref)]
    for gi, (_, dil) in enumerate(DIL_PATTERN):
        rows = tm // dil
        for r in range(dil):
            for c in range(2):
                sl = slice(c * 128, (c + 1) * 128)
                ob_ref[2 * gi + c, pl.ds(r, rows, stride=dil), :] = og[gi][r][:, sl].astype(F32)
                ob_ref[6 + 2 * gi + c, pl.ds(r, rows, stride=dil), :] = ls[gi][r][:, sl]

    def tok_major(k):
        return jnp.concatenate([ob_ref[2 * k], ob_ref[2 * k + 1]], axis=-1)

    l0, l1, l2 = tok_major(3), tok_major(4), tok_major(5)
    mx = jnp.maximum(jnp.maximum(l0, l1), l2)
    w0, w1, w2 = jnp.exp2(l0 - mx), jnp.exp2(l1 - mx), jnp.exp2(l2 - mx)
    ob = (w0 * tok_major(0) + w1 * tok_major(1) + w2 * tok_major(2)) / (w0 + w1 + w2)
    ya = jnp.dot(oa_ref[0, 0, 0], wa_ref[...], preferred_element_type=F32)
    yb = jnp.dot(ob.astype(BF16), wb_ref[...], preferred_element_type=F32)
    merged = (jax.nn.sigmoid(gates_ref[:, 0:D_MODEL].astype(F32)) * ya
              + jax.nn.sigmoid(gates_ref[:, D_MODEL:2 * D_MODEL].astype(F32)) * yb)
    x1 = x_ref[...] + jnp.dot(merged.astype(BF16), wo_ref[...], preferred_element_type=F32)
    x1_ref[...] = x1
    h2 = _rms(x1, gffn_ref[...])
    _rows_to_tiles(h2_ref, h2)
    lg_ref[...] = jnp.dot(h2.astype(BF16), wr_ref[...], preferred_element_type=F32) + br_ref[...]


def _post(x2d, oa, ogs, lss, main, wa, wb, wo, gffn, wr, br, B, S, tm, layer):
    T = x2d.shape[0]
    nt = S // tm
    res_specs = [pl.BlockSpec((1, d, tm // d, DIL_W), lambda i, nt=nt: (i // nt, 0, i % nt, 0))
                 for _, d in DIL_PATTERN]
    assert oa.shape[3] == tm and oa.shape[2] * 2 == nt

    def oa_index(i):
        it = i % nt
        late = it >= nt // 2
        return (i // nt, late.astype(jnp.int32), jnp.where(late, nt - 1 - it, it), 0, 0)

    return pl.pallas_call(
        functools.partial(_post_kernel, tm=tm),
        grid=(T // tm,),
        in_specs=[pl.BlockSpec((tm, D_MODEL), lambda i: (i, 0)),
                  pl.BlockSpec((1, 1, 1, tm, D_MODEL), oa_index)]
                 + res_specs + res_specs
                 + [pl.BlockSpec((tm, 2 * D_MODEL), lambda i: (i, 0)),
                    _layer_spec((D_MODEL, D_MODEL), layer),
                    _layer_spec((DIL_W, D_MODEL), layer),
                    _layer_spec((D_MODEL, D_MODEL), layer),
                    _layer_spec((1, D_MODEL), layer),
                    _layer_spec((D_MODEL, 128), layer),
                    _layer_spec((1, 128), layer)],
        out_specs=[pl.BlockSpec((tm, D_MODEL), lambda i: (i, 0)),
                   pl.BlockSpec((tm * ROW_TILE, 128), lambda i: (i, 0)),
                   pl.BlockSpec((tm, 128), lambda i: (i, 0))],
        out_shape=[jax.ShapeDtypeStruct((T, D_MODEL), F32),
                   jax.ShapeDtypeStruct((T * ROW_TILE, 128), U32),
                   jax.ShapeDtypeStruct((T, 128), F32)],
        scratch_shapes=[pltpu.VMEM((12, tm, 128), F32)],
        compiler_params=_cparams(("parallel",)),
        name="post_attn",
    )(x2d, oa, *ogs, *lss, main, wa, wb, wo, gffn, wr, br)


def _route_kernel(lg_ref, ri_ref, rw_ref, cnt_ref, carry_ref, lower_ref, *, tm):
    i = pl.program_id(0)

    @pl.when(i == 0)
    def _():
        carry_ref[...] = jnp.zeros(carry_ref.shape, F32)
        r_i = lax.broadcasted_iota(jnp.int32, (tm, tm), 0)
        c_i = lax.broadcasted_iota(jnp.int32, (tm, tm), 1)
        lower_ref[...] = jnp.where(c_i < r_i, 1.0, 0.0).astype(BF16)

    lg = lg_ref[...]
    lane = lax.broadcasted_iota(jnp.int32, lg.shape, 1)
    lane_f = lane.astype(F32)
    ninf = jnp.float32(-jnp.inf)

    def first_max(vals):
        vmax = jnp.max(vals, axis=-1, keepdims=True)
        idx = jnp.min(jnp.where(vals == vmax, lane_f, 128.0), axis=-1, keepdims=True)
        return vmax, idx.astype(jnp.int32)

    gl = jnp.where(lane < N_GROUPS, lg, ninf)
    gmax, g_sel = first_max(gl)
    p_g = 1.0 / jnp.sum(jnp.exp(gl - gmax), axis=-1, keepdims=True)
    lo = N_GROUPS + g_sel * EXPERTS_PER_GROUP
    el = jnp.where(jnp.logical_and(lane >= lo, lane < lo + EXPERTS_PER_GROUP), lg, ninf)
    v0, i0 = first_max(el)
    v1, i1 = first_max(jnp.where(lane == i0, ninf, el))
    t = jnp.exp(v1 - v0)
    w0 = p_g / (1.0 + t)
    w1 = p_g * t / (1.0 + t)
    e0 = i0 - N_GROUPS
    e1 = i1 - N_GROUPS
    hit0 = lane == e0
    hit1 = lane == e1
    oh = jnp.where(jnp.logical_or(hit0, hit1), 1.0, 0.0).astype(F32)
    excl = jnp.dot(lower_ref[...], oh.astype(BF16), preferred_element_type=F32) + carry_ref[...]
    r0 = jnp.sum(jnp.where(hit0, excl, 0.0), axis=-1, keepdims=True).astype(jnp.int32)
    r1 = jnp.sum(jnp.where(hit1, excl, 0.0), axis=-1, keepdims=True).astype(jnp.int32)
    carry_ref[...] = carry_ref[...] + jnp.sum(oh, axis=0, keepdims=True)
    zi = jnp.zeros(lg.shape, jnp.int32)
    ri_ref[...] = jnp.where(lane == 0, e0, jnp.where(lane == 1, e1, jnp.where(lane == 2, r0, jnp.where(lane == 3, r1, zi))))
    rw_ref[...] = jnp.where(lane == 0, w0, jnp.where(lane == 1, w1, jnp.zeros(lg.shape, F32)))
    cnt_ref[...] = carry_ref[...]


def _route(lg, tm):
    T = lg.shape[0]
    return pl.pallas_call(
        functools.partial(_route_kernel, tm=tm),
        grid=(T // tm,),
        in_specs=[pl.BlockSpec((tm, 128), lambda i: (i, 0))],
        out_specs=[pl.BlockSpec((tm, 128), lambda i: (i, 0)),
                   pl.BlockSpec((tm, 128), lambda i: (i, 0)),
                   pl.BlockSpec((1, 128), lambda i: (0, 0))],
        out_shape=[jax.ShapeDtypeStruct((T, 128), jnp.int32),
                   jax.ShapeDtypeStruct((T, 128), F32),
                   jax.ShapeDtypeStruct((1, 128), F32)],
        scratch_shapes=[pltpu.VMEM((1, 128), F32), pltpu.VMEM((tm, tm), BF16)],
        compiler_params=_cparams(("arbitrary",)),
        name="route",
    )(lg)


def _sc_gather_rows(table, idx, n_out=None, row0=0):
    inverse = n_out is not None
    n_src = idx.shape[0]
    n = n_out if inverse else n_src
    n_workers = SC_CORES * SC_SUBCORES
    per_w = n // n_workers
    n_chunks = per_w // SC_CHUNK
    assert per_w * n_workers == n and n_chunks * SC_CHUNK == per_w and n_chunks % 2 == 0
    mesh = plsc.VectorSubcoreMesh(core_axis_name="c", subcore_axis_name="s",
                                  num_cores=SC_CORES, num_subcores=SC_SUBCORES)

    @functools.partial(
        pl.kernel, mesh=mesh,
        out_type=jax.ShapeDtypeStruct((n,) + table.shape[1:], table.dtype),
        scratch_types=[pltpu.VMEM((per_w,), jnp.int32),
                       pltpu.VMEM((SC_CHUNK,) + table.shape[1:], table.dtype),
                       pltpu.VMEM((SC_CHUNK,) + table.shape[1:], table.dtype),
                       pltpu.SemaphoreType.DMA, pltpu.SemaphoreType.DMA,
                       pltpu.VMEM((n_src if inverse else SC_LANES,), jnp.int32)],
        compiler_params=pltpu.CompilerParams(use_tc_tiling_on_sc=True, needs_layout_passes=not inverse),
        name="sc_dispatch_rows" if inverse else "sc_gather_rows",
    )
    def gather(table_hbm, idx_hbm, out_hbm, idx_v, rows_a, rows_b, sem_a, sem_b, map_v):
        wid = lax.axis_index("s") * SC_CORES + lax.axis_index("c")
        base = wid * per_w
        if inverse:
            pltpu.sync_copy(idx_hbm, map_v)
            lanes = lax.iota(jnp.int32, SC_LANES)

            @pl.loop(0, per_w // SC_LANES)
            def _(j):
                idx_v[pl.ds(j * SC_LANES, SC_LANES)] = lax.rem(row0 + base + j * SC_LANES + lanes, table.shape[0])

            @pl.loop(0, n_src // SC_LANES)
            def _(a):
                local = map_v[pl.ds(a * SC_LANES, SC_LANES)] - (row0 + base)
                mine = jnp.logical_and(local >= 0, local < per_w)
                plsc.store_scatter(idx_v, [local], lax.div(a * SC_LANES + lanes, TOP_K), mask=mine)
        else:
            pltpu.sync_copy(idx_hbm.at[pl.ds(base, per_w)], idx_v)

        def fetch(chunk, rows_v, sem):
            return pltpu.make_async_copy(table_hbm.at[idx_v.at[pl.ds(chunk * SC_CHUNK, SC_CHUNK)]], rows_v, sem)

        def flush(chunk, rows_v):
            pltpu.sync_copy(rows_v, out_hbm.at[pl.ds(base + chunk * SC_CHUNK, SC_CHUNK)])

        fetch(0, rows_a, sem_a).start()

        @pl.loop(0, n_chunks, step=2)
        def _(c):
            fetch(c + 1, rows_b, sem_b).start()
            fetch(c, rows_a, sem_a).wait()
            flush(c, rows_a)

            @pl.when(c + 2 < n_chunks)
            def _():
                fetch(c + 2, rows_a, sem_a).start()

            fetch(c + 1, rows_b, sem_b).wait()
            flush(c + 1, rows_b)

    return gather(table, idx)


def _rows_from_tiles(ref, first, n_rows, stride=ROW_TILE):
    words = [ref[pl.ds(first + c, n_rows, stride=stride), :] for c in range(ROW_TILE)]
    lo = [pltpu.bitcast(w << 16, F32) for w in words]
    hi = [pltpu.bitcast(w & U32(0xFFFF0000), F32) for w in words]
    return jnp.concatenate(lo + hi, axis=-1)


def _rows_to_tiles(ref, val):
    half = D_MODEL // 2

    def bits(x):
        return pltpu.bitcast(x.astype(BF16).astype(F32), U32)

    for c in range(ROW_TILE):
        lo = bits(val[:, c * 128:(c + 1) * 128])
        hi = bits(val[:, half + c * 128:half + (c + 1) * 128])
        ref[pl.ds(c, val.shape[0], stride=ROW_TILE), :] = hi | (lo >> 16)


def _ffn_kernel(be_ref, nx_ref, par_ref, nvb_ref, x_ref, w1_hbm, w3_hbm, w2_hbm, prev_ref, y_ref,
                w1f_ref, w3f_ref, w2f_ref, w1b_ref, w3b_ref, w2b_ref, sem, *, layer):
    b = pl.program_id(0)
    nvb = nvb_ref[0]
    expert = be_ref[b]
    slot = par_ref[b]
    new_expert = jnp.logical_or(b == 0, expert != be_ref[jnp.maximum(b - 1, 0)])

    def fetch(e, sl):
        return [pltpu.make_async_copy(w_hbm.at[layer, e], wf_ref.at[sl], sem.at[sl])
                for w_hbm, wf_ref in ((w1_hbm, w1f_ref), (w3_hbm, w3f_ref), (w2_hbm, w2f_ref))]

    @pl.when(jnp.logical_and(b == 0, nvb > 0))
    def _():
        for cp in fetch(expert, slot):
            cp.start()

    @pl.when(jnp.logical_and(b < nvb, new_expert))
    def _():
        for cp in fetch(expert, slot):
            cp.wait()

        @pl.when(nx_ref[b] >= 0)
        def _():
            for cp in fetch(nx_ref[b], 1 - slot):
                cp.start()

        w1b_ref[...] = w1f_ref[slot].astype(BF16)
        w3b_ref[...] = w3f_ref[slot].astype(BF16)
        w2b_ref[...] = w2f_ref[slot].astype(BF16)

    @pl.when(b < nvb)
    def _():
        xb = _rows_from_tiles(x_ref, 0, FFN_BM).astype(BF16)
        h1 = jnp.dot(xb, w1b_ref[...], preferred_element_type=F32)
        h3 = jnp.dot(xb, w3b_ref[...], preferred_element_type=F32)
        a = (jax.nn.silu(h1) * h3).astype(BF16)
        _rows_to_tiles(y_ref, jnp.dot(a, w2b_ref[...], preferred_element_type=F32))

    @pl.when(b >= nvb)
    def _():
        y_ref[...] = jnp.zeros(y_ref.shape, U32)


def _expert_ffn(block_e, next_e, parity, nvb, xs, prev, first, n_blocks, w1, w3, w2, layer):
    nb = block_e.shape[0]
    if prev is None:
        prev = xs

    def used(b, *prefetch):
        return jnp.minimum(b, jnp.maximum(prefetch[-1][0] - 1, 0))

    return pl.pallas_call(
        functools.partial(_ffn_kernel, layer=layer),
        grid_spec=pltpu.PrefetchScalarGridSpec(
            num_scalar_prefetch=4,
            grid=(nb,),
            in_specs=[
                pl.BlockSpec((FFN_BM * ROW_TILE, 128), lambda b, *prefetch: (used(b, *prefetch), 0)),
                pl.BlockSpec(memory_space=pl.ANY),
                pl.BlockSpec(memory_space=pl.ANY),
                pl.BlockSpec(memory_space=pl.ANY),
                pl.BlockSpec(memory_space=pl.ANY),
            ],
            out_specs=pl.BlockSpec((FFN_BM * ROW_TILE, 128), lambda b, *prefetch: (first + b, 0)),
            scratch_shapes=[pltpu.VMEM((2, D_MODEL, EXPERT_FF), F32), pltpu.VMEM((2, D_MODEL, EXPERT_FF), F32),
                            pltpu.VMEM((2, EXPERT_FF, D_MODEL), F32),
                            pltpu.VMEM((D_MODEL, EXPERT_FF), BF16), pltpu.VMEM((D_MODEL, EXPERT_FF), BF16),
                            pltpu.VMEM((EXPERT_FF, D_MODEL), BF16),
                            pltpu.SemaphoreType.DMA((2,))],
        ),
        out_shape=jax.ShapeDtypeStruct((n_blocks * FFN_BM * ROW_TILE, 128), U32),
        input_output_aliases={} if first == 0 else {8: 0},
        compiler_params=_cparams(("arbitrary",)),
        name="expert_ffn",
    )(block_e, next_e, parity, nvb, xs, w1, w3, w2, prev)


def _comb_kernel(yg_ref, x1_ref, rw_ref, p_ref, gple_ref, wpg_ref, wpp_ref, gout_ref, prev_ref, o_ref, *, tm, final):
    w = rw_ref[...]
    y0 = _rows_from_tiles(yg_ref, 0, tm, stride=TOP_K * ROW_TILE)
    y1 = _rows_from_tiles(yg_ref, ROW_TILE, tm, stride=TOP_K * ROW_TILE)
    x2 = x1_ref[...] + (y0 * w[:, 0:1] + y1 * w[:, 1:2])
    e = jnp.dot(p_ref[...].astype(BF16), wpp_ref[...], preferred_element_type=F32)
    gate = jax.nn.sigmoid(jnp.dot(_rms(x2, gple_ref[...]).astype(BF16), wpg_ref[...], preferred_element_type=F32))
    x3 = x2 + gate * e
    o_ref[...] = _rms(x3, gout_ref[...]) if final else x3


def _combine_ple(yg, prev, first, x1, rw, p2d, gple, wpg, wpp, gout, tm, layer):
    T = x1.shape[0]
    nt = T // tm
    off = first // tm
    if prev is None:
        prev = x1
    return pl.pallas_call(
        functools.partial(_comb_kernel, tm=tm, final=layer == DEPTH - 1),
        grid=(yg.shape[0] // (tm * TOP_K * ROW_TILE),),
        in_specs=[
            pl.BlockSpec((tm * TOP_K * ROW_TILE, 128), lambda i: (i, 0)),
            pl.BlockSpec((tm, D_MODEL), lambda i: (off + i, 0)),
            pl.BlockSpec((tm, 128), lambda i: (off + i, 0)),
            pl.BlockSpec((tm, PLE_DIM), lambda i: (layer * nt + off + i, 0)),
            _layer_spec((1, D_MODEL), layer),
            _layer_spec((D_MODEL, D_MODEL), layer),
            _layer_spec((PLE_DIM, D_MODEL), layer),
            pl.BlockSpec((1, D_MODEL), lambda i: (0, 0)),
            pl.BlockSpec(memory_space=pl.ANY),
        ],
        out_specs=pl.BlockSpec((tm, D_MODEL), lambda i: (off + i, 0)),
        out_shape=jax.ShapeDtypeStruct((T, D_MODEL), F32),
        input_output_aliases={} if first == 0 else {8: 0},
        compiler_params=_cparams(("parallel",)),
        name="combine_ple",
    )(yg, x1, rw, p2d, gple, wpg, wpp, gout, prev)


def _rope_tables(pos, theta, rot_dim, offset, span, scale):
    half = rot_dim // 2
    inv = jnp.float32(theta) ** (-jnp.arange(half, dtype=F32) * 2.0 / rot_dim)
    ang = pos.astype(F32)[:, None] * inv
    rel = np.arange(128) - offset
    rot = (rel >= 0) & (rel < span)
    spread = np.zeros((half, 128), np.float32)
    spread[rel[rot] % half, np.nonzero(rot)[0]] = 1.0
    to_lanes = lambda t: jnp.dot(t, spread, precision=lax.Precision.HIGHEST)
    cos, sin = to_lanes(jnp.cos(ang)) + (~rot).astype(np.float32), to_lanes(jnp.sin(ang))
    return jnp.concatenate([cos, -sin, sin], axis=1) * jnp.float32(scale)


def _prep_w_in(w):
    c_q = w[..., 0:512]
    c_kv = w[..., 512:768]
    k_pe = w[..., 768:800]
    dil = w[..., 800:800 + 2304].reshape(w.shape[:-1] + (DIL_GROUPS, 3, DIL_HEADS, DIL_HEAD_DIM))

    def qk_layout(t):
        flat = lambda u: u.reshape(u.shape[:-2] + (-1,))
        return jnp.concatenate([flat(t[..., :DIL_HALF]), flat(t[..., PARTIAL_ROT:PARTIAL_ROT + DIL_REST]),
                                flat(t[..., DIL_HALF:PARTIAL_ROT]), flat(t[..., PARTIAL_ROT + DIL_REST:])], axis=-1)

    dil = jnp.concatenate([jnp.concatenate([qk_layout(dil[..., g, 0, :, :]), qk_layout(dil[..., g, 1, :, :]),
                                            dil[..., g, 2, :, :].reshape(w.shape[:-1] + (DIL_W,))], axis=-1)
                           for g in range(DIL_GROUPS)], axis=-1)
    gates = w[..., 3104:5152]
    zeros = lambda n: jnp.zeros(w.shape[:-1] + (n,), w.dtype)
    lo, hi = k_pe[..., :MLA_HALF], k_pe[..., MLA_HALF:]
    pad = MLA_SLOT - MLA_NOPE - MLA_ROPE
    kslot = jnp.concatenate([zeros(MLA_NOPE), lo, lo, zeros(pad), zeros(MLA_NOPE), hi, hi, zeros(pad)], axis=-1)
    return jnp.concatenate([gates, c_q, c_kv, kslot, dil], axis=-1).astype(BF16)


def _pad_heads(w, width):
    w = jnp.pad(w, ((0, 0), (0, 0), (0, 0), (0, MLA_SLOT - width)))
    return w.reshape(w.shape[0], w.shape[1], MLA_HEADS * MLA_SLOT).astype(BF16)


def _prep_w_q(w):
    w = w.reshape(DEPTH, MLA_Q_LORA, MLA_HEADS // 2, 2, MLA_NOPE + MLA_ROPE)
    nope = w[..., :MLA_NOPE]
    lo = w[..., MLA_NOPE:MLA_NOPE + MLA_HALF]
    hi = w[..., MLA_NOPE + MLA_HALF:]
    both = lambda t: t.reshape(t.shape[:-2] + (2 * MLA_HALF,))
    zeros = jnp.zeros(w.shape[:3] + (MLA_SLOT - MLA_NOPE - MLA_ROPE,), w.dtype)
    slot0 = jnp.concatenate([nope[..., 0, :], both(lo), zeros], axis=-1)
    slot1 = jnp.concatenate([nope[..., 1, :], both(hi), zeros], axis=-1)
    return jnp.stack([slot0, slot1], axis=-2).reshape(DEPTH, MLA_Q_LORA, MLA_HEADS * MLA_SLOT).astype(BF16)


def _prep_w_kv(w):
    w = w.reshape(DEPTH, MLA_KV_LORA, MLA_HEADS, MLA_NOPE + MLA_V)
    return _pad_heads(w[..., :MLA_NOPE], MLA_NOPE), _pad_heads(w[..., MLA_NOPE:], MLA_V)


def _dest_kernel(ri_ref, ps_ref, o_ref):
    ri = ri_ref[...].astype(F32)
    lane = lax.broadcasted_iota(jnp.int32, ri.shape, 1)
    ps = ps_ref[...]

    def col(k):
        return jnp.sum(jnp.where(lane == k, ri, 0.0), axis=-1, keepdims=True)

    def dest(k):
        start = jnp.sum(jnp.where(lane == col(k).astype(jnp.int32), ps, 0.0), axis=-1, keepdims=True)
        return (start + col(2 + k)).astype(jnp.int32)

    o_ref[...] = jnp.where(lane == 0, dest(0), jnp.where(lane == 1, dest(1), 0))


def _dest_rows(ri, pstarts, tm):
    T = ri.shape[0]
    ps = jnp.zeros((1, 128), F32).at[0, :N_EXPERTS].set(pstarts.astype(F32))
    return pl.pallas_call(
        _dest_kernel,
        grid=(T // tm,),
        in_specs=[pl.BlockSpec((tm, 128), lambda i: (i, 0)), pl.BlockSpec((1, 128), lambda i: (0, 0))],
        out_specs=pl.BlockSpec((tm, 128), lambda i: (i, 0)),
        out_shape=jax.ShapeDtypeStruct((T, 128), jnp.int32),
        compiler_params=_cparams(("parallel",)),
        name="dest_rows",
    )(ri, ps)


def _dispatch_plan(ri, cnt, T, tm):
    counts = cnt[0, :N_EXPERTS].astype(jnp.int32)
    pcounts = (counts + FFN_BM - 1) // FFN_BM * FFN_BM
    pends = jnp.cumsum(pcounts)
    pstarts = pends - pcounts
    dest = _dest_rows(ri, pstarts, tm)[:, 0:TOP_K]
    n_blocks = (T * TOP_K) // FFN_BM + N_EXPERTS
    first_row = jnp.arange(n_blocks, dtype=jnp.int32) * FFN_BM
    block_e = jnp.minimum(jnp.sum((pends[None, :] <= first_row[:, None]).astype(jnp.int32), axis=1), N_EXPERTS - 1)
    nvb = pends[-1] // FFN_BM
    after = pends // FFN_BM
    run_of_expert = jnp.cumsum((pcounts > 0).astype(jnp.int32)) - 1
    parity = (run_of_expert[block_e] % 2).astype(jnp.int32)
    per_range = n_blocks // FFN_RANGES
    ffn_plans = []
    for r in range(FFN_RANGES):
        lo, hi = r * per_range, (r + 1) * per_range
        end = jnp.minimum(nvb, hi)
        next_of_expert = jnp.where(after < end, block_e[jnp.minimum(after, n_blocks - 1)], -1)
        ffn_plans.append((block_e[lo:hi], next_of_expert[block_e[lo:hi]].astype(jnp.int32), parity[lo:hi],
                          jnp.clip(nvb - lo, 0, per_range).astype(jnp.int32).reshape(1)))
    return dest.reshape(T * TOP_K), n_blocks, ffn_plans


def kernel(x, p, positions, g_mix, w_in, g_q_lat, w_q_up, g_kv_lat, w_kv_up, w_branch_a, w_branch_b, w_out, g_ffn, w_router_grp, b_router_grp, w_router_exp, b_router_exp, w_exp_gate, w_exp_up, w_exp_down, g_ple, w_ple_gate, w_ple_proj, g_final):
    B, S, D = x.shape
    T = B * S
    pos = positions.reshape(T)
    dil_scale = DIL_HEAD_DIM ** -0.5 * LOG2E
    mla_scale = (MLA_NOPE + MLA_ROPE) ** -0.5 * LOG2E
    tab_dil = jnp.concatenate([_rope_tables(pos, ROPE_THETA, PARTIAL_ROT, 0, DIL_ROT_LANES, sc)
                               for sc in (dil_scale, 1.0)], axis=1)
    tab_mla = jnp.concatenate([_rope_tables(pos, MLA_ROPE_THETA, MLA_ROPE, MLA_NOPE, MLA_ROPE, sc)
                               for sc in (mla_scale, 1.0)], axis=1)

    w_in_p = _prep_w_in(w_in)
    wq = _prep_w_q(w_q_up)
    wk, wv = _prep_w_kv(w_kv_up)
    wr = jnp.concatenate([w_router_grp, w_router_exp, jnp.zeros((DEPTH, D, 128 - N_GROUPS - N_EXPERTS), F32)],
                         axis=-1).astype(BF16)
    br = jnp.concatenate([b_router_grp, b_router_exp.reshape(DEPTH, N_EXPERTS),
                          jnp.zeros((DEPTH, 128 - N_GROUPS - N_EXPERTS), F32)], axis=-1).reshape(DEPTH, 1, 128)
    wa, wb, wo = w_branch_a.astype(BF16), w_branch_b.astype(BF16), w_out.astype(BF16)
    wpg, wpp = w_ple_gate.astype(BF16), w_ple_proj.astype(BF16)
    gains = lambda g: g.reshape(DEPTH, 1, -1)

    xc = x.reshape(T, D)
    for i in range(DEPTH):
        main, d0, d1, d2 = _in_proj(xc, gains(g_mix), w_in_p, tab_dil, B, S, TM_IN, i)
        q, k, v = _mla_qkv(main, gains(g_q_lat), gains(g_kv_lat), wq, wk, wv, tab_mla, TM_QKV, i)
        o_a = _mla_attn(q, k, v, B, S, TQ)
        ogs, lss = [], []
        for (_, dil), qkv in zip(DIL_PATTERN, (d0, d1, d2)):
            og, ls = _dil_attn(qkv, B, dil, S // dil)
            ogs.append(og)
            lss.append(ls)
        x1, h2, lg = _post(xc, o_a, ogs, lss, main, wa, wb, wo, gains(g_ffn), wr, br, B, S, TM_POST, i)
        ri, rw, cnt = _route(lg, TM_ROUTE)
        dest, n_blocks, ffn_plans = _dispatch_plan(ri, cnt, T, TM_ROUTE)
        range_blocks = n_blocks // FFN_RANGES
        ys = None
        for r, plan in enumerate(ffn_plans):
            xs = _sc_gather_rows(h2.reshape(T, ROW_TILE, 128), dest, n_out=range_blocks * FFN_BM,
                                 row0=r * range_blocks * FFN_BM)
            ys = _expert_ffn(*plan, xs.reshape(-1, 128), ys, r * range_blocks, n_blocks,
                             w_exp_gate, w_exp_up, w_exp_down, i)
        xc = None
        for first in (0, T // 2):
            yg = _sc_gather_rows(ys.reshape(-1, ROW_TILE, 128), dest[first * TOP_K:(first + T // 2) * TOP_K])
            xc = _combine_ple(yg.reshape(-1, 128), xc, first, x1, rw, p.reshape(DEPTH * T, PLE_DIM), gains(g_ple),
                              wpg, wpp, g_final.reshape(1, D), TM_COMB, i)
    return xc.reshape(B, S, D)
```

```python
import functools
import math

import jax
import jax.numpy as jnp
import numpy as np
from jax import lax
from jax.experimental import pallas as pl
from jax.experimental.pallas import tpu as pltpu
from jax.experimental.pallas import tpu_sc as plsc

F32 = jnp.float32
BF16 = jnp.bfloat16

D_MODEL = 1024
DEPTH = 4
RMS_EPS = 1e-6
NEG = -1e30
LOG2E = math.log2(math.e)

MLA_HEADS = 16
MLA_Q_LORA = 512
MLA_KV_LORA = 256
MLA_NOPE = 64
MLA_ROPE = 32
MLA_V = 64
MLA_ROPE_THETA = 10000.0
MLA_SLOT = 128
MLA_HALF = MLA_ROPE // 2


def _mla_head_lanes(lane, head):
    l = lane % MLA_SLOT
    nope = jnp.logical_and(lane // MLA_SLOT == head, l < MLA_NOPE)
    rope = jnp.logical_and(l >= MLA_NOPE + head * MLA_HALF, l < MLA_NOPE + (head + 1) * MLA_HALF)
    return jnp.logical_or(nope, rope)

DIL_PATTERN = ((128, 1), (512, 4), (2048, 16))
DIL_GROUPS = 3
DIL_HEADS = 4
DIL_HEAD_DIM = 64
DIL_W = DIL_HEADS * DIL_HEAD_DIM
DIL_BLK = 128
DIL_UNROLL = 16
ROPE_THETA = 500000.0
PARTIAL_ROT = DIL_HEAD_DIM // 4
DIL_HALF = PARTIAL_ROT // 2
DIL_ROT_LANES = DIL_HEADS * DIL_HALF
DIL_REST = (DIL_HEAD_DIM - PARTIAL_ROT) // 2


def _dil_qk_head_of_lane(lane):
    l = lane % 128
    plain = l - DIL_ROT_LANES
    plain_head = sum((plain >= h * DIL_REST).astype(jnp.int32) for h in range(1, DIL_HEADS))
    return jnp.where(l < DIL_ROT_LANES, l // DIL_HALF, plain_head)

N_GROUPS = 8
EXPERTS_PER_GROUP = 8
N_EXPERTS = 64
TOP_K = 2
EXPERT_FF = 256
PLE_DIM = 256

IN_TN = 768
MAIN_COLS = 3072
IN_COLS_PAD = MAIN_COLS + DIL_GROUPS * 3 * DIL_W
N_MAIN_TILES = MAIN_COLS // IN_TN

FFN_BM = 256
FFN_RANGES = 2
ROW_TILE = D_MODEL // 256
U32 = jnp.uint32
VMEM_LIMIT = 48 * 1024 * 1024
TM_IN, TM_QKV, TQ, TM_POST, TM_ROUTE, TM_COMB = 512, 1024, 512, 512, 1024, 512

SC_CORES = 2
SC_SUBCORES = 16
SC_LANES = 16
SC_CHUNK = 64


def _cparams(sem):
    return pltpu.CompilerParams(dimension_semantics=sem, vmem_limit_bytes=VMEM_LIMIT)


def _layer_spec(shape, layer, col=None):
    def index(*grid_idx):
        return (layer,) + (0,) * (len(shape) - 1) + ((col(*grid_idx),) if col else (0,))

    return pl.BlockSpec((None,) + tuple(shape), index)


def _rms(x, g):
    return x * lax.rsqrt(jnp.mean(x * x, axis=-1, keepdims=True) + RMS_EPS) * g


def _in_kernel(x_ref, g_ref, w_ref, tab_ref, main_ref, d0_ref, d1_ref, d2_ref, xn_ref, acc_ref):
    j = pl.program_id(1)

    @pl.when(j == 0)
    def _():
        xn_ref[...] = _rms(x_ref[...], g_ref[...]).astype(BF16)

    acc = jnp.dot(xn_ref[...], w_ref[...], preferred_element_type=F32)

    @pl.when(j < N_MAIN_TILES)
    def _():
        main_ref[...] = acc.astype(BF16)

    def dil_tile(out_ref, dil):
        chunk = lambda c: acc[:, c * 128:(c + 1) * 128]
        for base, t0 in ((0, 0), (2, 3)):
            cos, sin_n, sin_p = [tab_ref[:, (t0 + t) * 128:(t0 + t + 1) * 128] for t in range(3)]
            lo, hi = chunk(base), chunk(base + 1)
            acc_ref[base] = lo * cos + hi * sin_n
            acc_ref[base + 1] = hi * cos + lo * sin_p
        for c in range(4, 6):
            acc_ref[c] = chunk(c)
        rows = acc_ref.shape[1] // dil
        for r in range(dil):
            for c in range(6):
                out_ref[0, r, :, c * 128:(c + 1) * 128] = acc_ref[c, pl.ds(r, rows, stride=dil), :].astype(BF16)

    for gi, (_, dil) in enumerate(DIL_PATTERN):
        pl.when(j == N_MAIN_TILES + gi)(functools.partial(dil_tile, (d0_ref, d1_ref, d2_ref)[gi], dil))


def _in_proj(x2d, g, w, tab, B, S, tm, layer):
    T = x2d.shape[0]
    nt = S // tm
    dil_shapes = [jax.ShapeDtypeStruct((B, d, S // d, 3 * DIL_W), BF16) for _, d in DIL_PATTERN]
    dil_specs = [pl.BlockSpec((1, d, tm // d, 3 * DIL_W), lambda i, j, nt=nt: (i // nt, 0, i % nt, 0))
                 for _, d in DIL_PATTERN]
    return pl.pallas_call(
        _in_kernel,
        grid=(T // tm, IN_COLS_PAD // IN_TN),
        in_specs=[
            pl.BlockSpec((tm, D_MODEL), lambda i, j: (i, 0)),
            _layer_spec((1, D_MODEL), layer),
            _layer_spec((D_MODEL, IN_TN), layer, col=lambda i, j: j),
            pl.BlockSpec((tm, 6 * 128), lambda i, j: (i, 0)),
        ],
        out_specs=[pl.BlockSpec((tm, IN_TN), lambda i, j: (i, jnp.minimum(j, N_MAIN_TILES - 1)))] + dil_specs,
        out_shape=[jax.ShapeDtypeStruct((T, MAIN_COLS), BF16)] + dil_shapes,
        scratch_shapes=[pltpu.VMEM((tm, D_MODEL), BF16), pltpu.VMEM((IN_TN // 128, tm, 128), F32)],
        compiler_params=_cparams(("parallel", "arbitrary")),
        name="in_proj",
    )(x2d, g, w, tab)


def _qkv_kernel(cq_ref, ckv_ref, kpe_ref, gq_ref, gkv_ref, wq_ref, wk_ref, wv_ref, tab_ref,
                q_ref, k_ref, v_ref):
    qn = _rms(cq_ref[...].astype(F32), gq_ref[...]).astype(BF16)
    kvn = _rms(ckv_ref[...].astype(F32), gkv_ref[...]).astype(BF16)
    qacc = jnp.dot(qn, wq_ref[...], preferred_element_type=F32)
    kacc = jnp.dot(kvn, wk_ref[...], preferred_element_type=F32)
    vacc = jnp.dot(kvn, wv_ref[...], preferred_element_type=F32)
    lane_v = lax.broadcasted_iota(jnp.int32, vacc.shape, 1)
    v_ref[...] = jnp.where(lane_v % MLA_SLOT < MLA_V, vacc, 1.0).astype(BF16)
    cq, snq, spq, ck, snk, spk = [tab_ref[:, t * 128:(t + 1) * 128] for t in range(6)]
    kpe_lo = kpe_ref[:, 0:128].astype(F32)
    kpe_hi = kpe_ref[:, 128:256].astype(F32)
    krot = (kpe_lo * ck + kpe_hi * snk, kpe_hi * ck + kpe_lo * spk)
    for pair in range(MLA_HEADS // 2):
        s0 = slice(2 * pair * MLA_SLOT, (2 * pair + 1) * MLA_SLOT)
        s1 = slice((2 * pair + 1) * MLA_SLOT, (2 * pair + 2) * MLA_SLOT)
        q0, q1 = qacc[:, s0], qacc[:, s1]
        q_ref[:, s0] = (q0 * cq + q1 * snq).astype(BF16)
        q_ref[:, s1] = (q1 * cq + q0 * spq).astype(BF16)
        k_ref[:, s0] = (kacc[:, s0] + krot[0]).astype(BF16)
        k_ref[:, s1] = (kacc[:, s1] + krot[1]).astype(BF16)


def _mla_qkv(main, gq, gkv, wq, wk, wv, tab, tm, layer):
    T = main.shape[0]
    return pl.pallas_call(
        _qkv_kernel,
        grid=(T // tm,),
        in_specs=[
            pl.BlockSpec((tm, MLA_Q_LORA), lambda i: (i, 2048 // MLA_Q_LORA)),
            pl.BlockSpec((tm, MLA_KV_LORA), lambda i: (i, 2560 // MLA_KV_LORA)),
            pl.BlockSpec((tm, 256), lambda i: (i, 2816 // 256)),
            _layer_spec((1, MLA_Q_LORA), layer),
            _layer_spec((1, MLA_KV_LORA), layer),
            _layer_spec((MLA_Q_LORA, MLA_HEADS * MLA_SLOT), layer),
            _layer_spec((MLA_KV_LORA, MLA_HEADS * MLA_SLOT), layer),
            _layer_spec((MLA_KV_LORA, MLA_HEADS * MLA_SLOT), layer),
            pl.BlockSpec((tm, 6 * 128), lambda i: (i, 0)),
        ],
        out_specs=[
            pl.BlockSpec((tm, MLA_HEADS * MLA_SLOT), lambda i: (i, 0)),
            pl.BlockSpec((tm, MLA_HEADS * MLA_SLOT), lambda i: (i, 0)),
            pl.BlockSpec((tm, MLA_HEADS * MLA_SLOT), lambda i: (i, 0)),
        ],
        out_shape=[
            jax.ShapeDtypeStruct((T, MLA_HEADS * MLA_SLOT), BF16),
            jax.ShapeDtypeStruct((T, MLA_HEADS * MLA_SLOT), BF16),
            jax.ShapeDtypeStruct((T, MLA_HEADS * MLA_SLOT), BF16),
        ],
        compiler_params=_cparams(("parallel",)),
        name="mla_qkv",
    )(main, main, main, gq, gkv, wq, wk, wv, tab)


def _mla_attn_kernel(qa_ref, qb_ref, k_ref, v_ref, o_ref, m_ref, acc_ref, *, tq):
    p_id = pl.program_id(2)
    nq = k_ref.shape[1] // tq
    lane = lax.broadcasted_iota(jnp.int32, (tq, 2 * MLA_SLOT), 1)
    qh = []
    for q_ref in (qa_ref, qb_ref):
        qp = q_ref[0]
        zero = jnp.zeros_like(qp)
        qh.append([jnp.where(_mla_head_lanes(lane, h), qp, zero) for h in range(2)])
    hq = tq // 2
    dn = (((1,), (1,)), ((), ()))

    def step(blk, start, width, r0, mask, first=False):
        kb = k_ref[0, pl.ds(start, width), :]
        vb = v_ref[0, pl.ds(start, width), :]
        rows = tq - r0
        s_pair = lax.dot_general(jnp.concatenate([qh[blk][0][r0:], qh[blk][1][r0:]], axis=0), kb, dn,
                                 preferred_element_type=F32)
        for h in range(2):
            s = s_pair[h * rows:(h + 1) * rows]
            if mask is not None:
                s = jnp.where(mask, s, NEG)
            m_cur = jnp.max(s, axis=-1, keepdims=True)
            if first:
                m_new = jnp.broadcast_to(m_cur, (tq - r0, 128))
            else:
                m_prev = m_ref[blk, h, r0:, :]
                m_new = jnp.maximum(m_prev, m_cur)
            p = jnp.exp2(s - jnp.concatenate([m_new] * (width // 128), axis=-1))
            pv = jnp.dot(p.astype(BF16), vb[:, h * MLA_SLOT:(h + 1) * MLA_SLOT], preferred_element_type=F32)
            if first:
                acc_ref[blk, h, r0:, :] = pv
            else:
                acc_ref[blk, h, r0:, :] = jnp.exp2(m_prev - m_new) * acc_ref[blk, h, r0:, :] + pv
            m_ref[blk, h, r0:, :] = m_new

    def causal(rows):
        return lax.broadcasted_iota(jnp.int32, (rows, hq), 1) <= lax.broadcasted_iota(jnp.int32, (rows, hq), 0)

    def q_blocks(n_a):
        n_b = nq - 1 - n_a
        for blk, n_full in ((1, n_b), (0, n_a)):
            j = 0
            while j < n_full:
                wide = 2 if j + 1 < n_full else 1
                step(blk, j * tq, wide * tq, 0, None, first=j == 0)
                j += wide
            step(blk, n_full * tq, hq, 0, causal(tq), first=n_full == 0)
            step(blk, n_full * tq + hq, hq, hq, causal(hq))

    for n_a in range(nq // 2):
        pl.when(p_id == n_a)(functools.partial(q_blocks, n_a))
    lane_o = lax.broadcasted_iota(jnp.int32, (tq, 2 * MLA_V), 1)
    for blk in range(2):
        outs = []
        for h in range(2):
            a = acc_ref[blk, h]
            outs.append(a / pltpu.roll(a, MLA_V, 1))
        o_ref[0, blk, 0] = jnp.where(lane_o < MLA_V, outs[0], pltpu.roll(outs[1], MLA_V, 1)).astype(BF16)


def _mla_attn(q, k, v, B, S, tq):
    nq = S // tq
    q = q.reshape(B, S, MLA_HEADS * MLA_SLOT)
    k = k.reshape(B, S, MLA_HEADS * MLA_SLOT)
    v = v.reshape(B, S, MLA_HEADS * MLA_SLOT)
    return pl.pallas_call(
        functools.partial(_mla_attn_kernel, tq=tq),
        grid=(B, MLA_HEADS // 2, nq // 2),
        in_specs=[
            pl.BlockSpec((1, tq, 2 * MLA_SLOT), lambda b, h, p: (b, p, h)),
            pl.BlockSpec((1, tq, 2 * MLA_SLOT), lambda b, h, p: (b, nq - 1 - p, h)),
            pl.BlockSpec((1, S, 2 * MLA_SLOT), lambda b, h, p: (b, 0, h)),
            pl.BlockSpec((1, S, 2 * MLA_SLOT), lambda b, h, p: (b, 0, h)),
        ],
        out_specs=pl.BlockSpec((1, 2, 1, tq, 2 * MLA_V), lambda b, h, p: (b, 0, p, 0, h)),
        out_shape=jax.ShapeDtypeStruct((B, 2, nq // 2, tq, MLA_HEADS * MLA_V), BF16),
        scratch_shapes=[pltpu.VMEM((2, 2, tq, 128), F32), pltpu.VMEM((2, 2, tq, MLA_SLOT), F32)],
        compiler_params=_cparams(("parallel", "parallel", "arbitrary")),
        name="mla_attn",
    )(q, q, k, v)


def _dil_kernel(qkv_ref, o_ref, lse_ref, bias_ref, *, nb):
    dil = qkv_ref.shape[1]
    width = 2 * DIL_BLK if nb > 1 else DIL_BLK
    row = lax.broadcasted_iota(jnp.int32, (DIL_BLK, width), 0)
    col = lax.broadcasted_iota(jnp.int32, (DIL_BLK, width), 1)
    bias_ref[0] = jnp.where(col <= row, 0.0, NEG)
    if nb > 1:
        later = jnp.logical_or(jnp.logical_and(col >= DIL_BLK, col - DIL_BLK <= row),
                               jnp.logical_and(col < DIL_BLK, col >= row))
        bias_ref[1] = jnp.where(later, 0.0, NEG)
    lane = lax.broadcasted_iota(jnp.int32, (DIL_BLK, DIL_W), 1)
    head_of_lane = lane // DIL_HEAD_DIM
    qk_head_of_lane = _dil_qk_head_of_lane(lane)
    dn = (((1,), (1,)), ((), ()))

    def by_head(parts):
        out = parts[DIL_HEADS - 1]
        for h in range(DIL_HEADS - 2, -1, -1):
            out = jnp.where(head_of_lane == h, parts[h], out)
        return out

    def unit(u):
        r = u // nb
        n = u % nb
        q0 = pl.multiple_of(n * DIL_BLK, DIL_BLK)
        q = qkv_ref[0, r, pl.ds(q0, DIL_BLK), 0:DIL_W]
        if nb > 1:
            w0 = pl.multiple_of(jnp.maximum(n - 1, 0) * DIL_BLK, DIL_BLK)
            bias = bias_ref[jnp.minimum(n, 1)]
        else:
            w0 = 0
            bias = bias_ref[0]
        kw = qkv_ref[0, r, pl.ds(w0, width), DIL_W:2 * DIL_W]
        vw = qkv_ref[0, r, pl.ds(w0, width), 2 * DIL_W:3 * DIL_W]
        zero = jnp.zeros_like(q)
        qs = jnp.concatenate([jnp.where(qk_head_of_lane == h, q, zero) for h in range(DIL_HEADS)], axis=0)
        s = lax.dot_general(qs, kw, dn, preferred_element_type=F32)
        s = (s.reshape(DIL_HEADS, DIL_BLK, width) + bias[None]).reshape(DIL_HEADS * DIL_BLK, width)
        m = jnp.max(s, axis=-1, keepdims=True)
        e = jnp.exp2(s - m)
        den = jnp.sum(e, axis=-1, keepdims=True)
        pv = jnp.dot(e.astype(BF16), vw, preferred_element_type=F32)
        lse = m + jnp.log2(den)
        blk = lambda t, h: t[h * DIL_BLK:(h + 1) * DIL_BLK]
        o = by_head([blk(pv, h) for h in range(DIL_HEADS)]) / by_head(
            [jnp.broadcast_to(blk(den, h), (DIL_BLK, DIL_W)) for h in range(DIL_HEADS)])
        o_ref[0, r, pl.ds(q0, DIL_BLK), :] = o.astype(BF16)
        lse_ref[0, r, pl.ds(q0, DIL_BLK), :] = by_head(
            [jnp.broadcast_to(blk(lse, h), (DIL_BLK, DIL_W)) for h in range(DIL_HEADS)])

    def body(t, carry):
        for u in range(DIL_UNROLL):
            unit(DIL_UNROLL * t + u)
        return carry

    lax.fori_loop(0, dil * nb // DIL_UNROLL, body, 0)


def _dil_attn(qkv, B, dil, L):
    nb = L // DIL_BLK
    return pl.pallas_call(
        functools.partial(_dil_kernel, nb=nb),
        grid=(B,),
        in_specs=[pl.BlockSpec((1, dil, L, 3 * DIL_W), lambda b: (b, 0, 0, 0))],
        out_specs=[
            pl.BlockSpec((1, dil, L, DIL_W), lambda b: (b, 0, 0, 0)),
            pl.BlockSpec((1, dil, L, DIL_W), lambda b: (b, 0, 0, 0)),
        ],
        out_shape=[
            jax.ShapeDtypeStruct((B, dil, L, DIL_W), BF16),
            jax.ShapeDtypeStruct((B, dil, L, DIL_W), F32),
        ],
        scratch_shapes=[pltpu.VMEM((2, DIL_BLK, 2 * DIL_BLK if nb > 1 else DIL_BLK), F32)],
        compiler_params=_cparams(("parallel",)),
        name=f"dil_attn_d{dil}",
    )(qkv)


def _post_kernel(x_ref, oa_ref, og0_ref, og1_ref, og2_ref, ls0_ref, ls1_ref, ls2_ref, gates_ref,
                 wa_ref, wb_ref, wo_ref, gffn_ref, wr_ref, br_ref,
                 x1_ref, h2_ref, lg_ref, ob_ref, *, tm):
    ls =[r[0] for r in (ls0_ref, ls1_ref, ls2_ref)]
    og = [r[0] for r in (og0_ref, og1_ref, og2_ref)]
    for gi, (_, dil) in enumerate(DIL_PATTERN):
        rows = tm // dil
        for r in range(dil):
            for c in range(2):
                sl = slice(c * 128, (c + 1) * 128)
                ob_ref[2 * gi + c, pl.ds(r, rows, stride=dil), :] = og[gi][r][:, sl].astype(F32)
                ob_ref[6 + 2 * gi + c, pl.ds(r, rows, stride=dil), :] = ls[gi][r][:, sl]

    def tok_major(k):
        return jnp.concatenate([ob_ref[2 * k], ob_ref[2 * k + 1]], axis=-1)

    l0, l1, l2 = tok_major(3), tok_major(4), tok_major(5)
    mx = jnp.maximum(jnp.maximum(l0, l1), l2)
    w0, w1, w2 = jnp.exp2(l0 - mx), jnp.exp2(l1 - mx), jnp.exp2(l2 - mx)
    ob = (w0 * tok_major(0) + w1 * tok_major(1) + w2 * tok_major(2)) / (w0 + w1 + w2)
    ya = jnp.dot(oa_ref[0, 0, 0], wa_ref[...], preferred_element_type=F32)
    yb = jnp.dot(ob.astype(BF16), wb_ref[...], preferred_element_type=F32)
    merged = (jax.nn.sigmoid(gates_ref[:, 0:D_MODEL].astype(F32)) * ya
              + jax.nn.sigmoid(gates_ref[:, D_MODEL:2 * D_MODEL].astype(F32)) * yb)
    x1 = x_ref[...] + jnp.dot(merged.astype(BF16), wo_ref[...], preferred_element_type=F32)
    x1_ref[...] = x1
    h2 = _rms(x1, gffn_ref[...])
    _rows_to_tiles(h2_ref, h2)
    lg_ref[...] = jnp.dot(h2.astype(BF16), wr_ref[...], preferred_element_type=F32) + br_ref[...]


def _post(x2d, oa, ogs, lss, main, wa, wb, wo, gffn, wr, br, B, S, tm, layer):
    T = x2d.shape[0]
    nt = S // tm
    res_specs = [pl.BlockSpec((1, d, tm // d, DIL_W), lambda i, nt=nt: (i // nt, 0, i % nt, 0))
                 for _, d in DIL_PATTERN]
    assert oa.shape[3] == tm and oa.shape[2] * 2 == nt

    def oa_index(i):
        it = i % nt
        late = it >= nt // 2
        return (i // nt, late.astype(jnp.int32), jnp.where(late, nt - 1 - it, it), 0, 0)

    return pl.pallas_call(
        functools.partial(_post_kernel, tm=tm),
        grid=(T // tm,),
        in_specs=[pl.BlockSpec((tm, D_MODEL), lambda i: (i, 0)),
                  pl.BlockSpec((1, 1, 1, tm, D_MODEL), oa_index)]
                 + res_specs + res_specs
                 + [pl.BlockSpec((tm, 2 * D_MODEL), lambda i: (i, 0)),
                    _layer_spec((D_MODEL, D_MODEL), layer),
                    _layer_spec((DIL_W, D_MODEL), layer),
                    _layer_spec((D_MODEL, D_MODEL), layer),
                    _layer_spec((1, D_MODEL), layer),
                    _layer_spec((D_MODEL, 128), layer),
                    _layer_spec((1, 128), layer)],
        out_specs=[pl.BlockSpec((tm, D_MODEL), lambda i: (i, 0)),
                   pl.BlockSpec((tm * ROW_TILE, 128), lambda i: (i, 0)),
                   pl.BlockSpec((tm, 128), lambda i: (i, 0))],
        out_shape=[jax.ShapeDtypeStruct((T, D_MODEL), F32),
                   jax.ShapeDtypeStruct((T * ROW_TILE, 128), U32),
                   jax.ShapeDtypeStruct((T, 128), F32)],
        scratch_shapes=[pltpu.VMEM((12, tm, 128), F32)],
        compiler_params=_cparams(("parallel",)),
        name="post_attn",
    )(x2d, oa, *ogs, *lss, main, wa, wb, wo, gffn, wr, br)


def _route_kernel(lg_ref, ri_ref, rw_ref, cnt_ref, carry_ref, lower_ref, *, tm):
    i = pl.program_id(0)

    @pl.when(i == 0)
    def _():
        carry_ref[...] = jnp.zeros(carry_ref.shape, F32)
        r_i = lax.broadcasted_iota(jnp.int32, (tm, tm), 0)
        c_i = lax.broadcasted_iota(jnp.int32, (tm, tm), 1)
        lower_ref[...] = jnp.where(c_i < r_i, 1.0, 0.0).astype(BF16)

    lg = lg_ref[...]
    lane = lax.broadcasted_iota(jnp.int32, lg.shape, 1)
    lane_f = lane.astype(F32)
    ninf = jnp.float32(-jnp.inf)

    def first_max(vals):
        vmax = jnp.max(vals, axis=-1, keepdims=True)
        idx = jnp.min(jnp.where(vals == vmax, lane_f, 128.0), axis=-1, keepdims=True)
        return vmax, idx.astype(jnp.int32)

    gl = jnp.where(lane < N_GROUPS, lg, ninf)
    gmax, g_sel = first_max(gl)
    p_g = 1.0 / jnp.sum(jnp.exp(gl - gmax), axis=-1, keepdims=True)
    lo = N_GROUPS + g_sel * EXPERTS_PER_GROUP
    el = jnp.where(jnp.logical_and(lane >= lo, lane < lo + EXPERTS_PER_GROUP), lg, ninf)
    v0, i0 = first_max(el)
    v1, i1 = first_max(jnp.where(lane == i0, ninf, el))
    t = jnp.exp(v1 - v0)
    w0 = p_g / (1.0 + t)
    w1 = p_g * t / (1.0 + t)
    e0 = i0 - N_GROUPS
    e1 = i1 - N_GROUPS
    hit0 = lane == e0
    hit1 = lane == e1
    oh = jnp.where(jnp.logical_or(hit0, hit1), 1.0, 0.0).astype(F32)
    excl = jnp.dot(lower_ref[...], oh.astype(BF16), preferred_element_type=F32) + carry_ref[...]
    r0 = jnp.sum(jnp.where(hit0, excl, 0.0), axis=-1, keepdims=True).astype(jnp.int32)
    r1 = jnp.sum(jnp.where(hit1, excl, 0.0), axis=-1, keepdims=True).astype(jnp.int32)
    carry_ref[...] = carry_ref[...] + jnp.sum(oh, axis=0, keepdims=True)
    zi = jnp.zeros(lg.shape, jnp.int32)
    ri_ref[...] = jnp.where(lane == 0, e0, jnp.where(lane == 1, e1, jnp.where(lane == 2, r0, jnp.where(lane == 3, r1, zi))))
    rw_ref[...] = jnp.where(lane == 0, w0, jnp.where(lane == 1, w1, jnp.zeros(lg.shape, F32)))
    cnt_ref[...] = carry_ref[...]


def _route(lg, tm):
    T = lg.shape[0]
    return pl.pallas_call(
        functools.partial(_route_kernel, tm=tm),
        grid=(T // tm,),
        in_specs=[pl.BlockSpec((tm, 128), lambda i: (i, 0))],
        out_specs=[pl.BlockSpec((tm, 128), lambda i: (i, 0)),
                   pl.BlockSpec((tm, 128), lambda i: (i, 0)),
                   pl.BlockSpec((1, 128), lambda i: (0, 0))],
        out_shape=[jax.ShapeDtypeStruct((T, 128), jnp.int32),
                   jax.ShapeDtypeStruct((T, 128), F32),
                   jax.ShapeDtypeStruct((1, 128), F32)],
        scratch_shapes=[pltpu.VMEM((1, 128), F32), pltpu.VMEM((tm, tm), BF16)],
        compiler_params=_cparams(("arbitrary",)),
        name="route",
    )(lg)


def _sc_gather_rows(table, idx, n_out=None, row0=0):
    inverse = n_out is not None
    n_src = idx.shape[0]
    n = n_out if inverse else n_src
    n_workers = SC_CORES * SC_SUBCORES
    per_w = n // n_workers
    n_chunks = per_w // SC_CHUNK
    assert per_w * n_workers == n and n_chunks * SC_CHUNK == per_w and n_chunks % 2 == 0
    mesh = plsc.VectorSubcoreMesh(core_axis_name="c", subcore_axis_name="s",
                                  num_cores=SC_CORES, num_subcores=SC_SUBCORES)

    @functools.partial(
        pl.kernel, mesh=mesh,
        out_type=jax.ShapeDtypeStruct((n,) + table.shape[1:], table.dtype),
        scratch_types=[pltpu.VMEM((per_w,), jnp.int32),
                       pltpu.VMEM((SC_CHUNK,) + table.shape[1:], table.dtype),
                       pltpu.VMEM((SC_CHUNK,) + table.shape[1:], table.dtype),
                       pltpu.SemaphoreType.DMA, pltpu.SemaphoreType.DMA,
                       pltpu.VMEM((n_src if inverse else SC_LANES,), jnp.int32)],
        compiler_params=pltpu.CompilerParams(use_tc_tiling_on_sc=True, needs_layout_passes=not inverse),
        name="sc_dispatch_rows" if inverse else "sc_gather_rows",
    )
    def gather(table_hbm, idx_hbm, out_hbm, idx_v, rows_a, rows_b, sem_a, sem_b, map_v):
        wid = lax.axis_index("s") * SC_CORES + lax.axis_index("c")
        base = wid * per_w
        if inverse:
            pltpu.sync_copy(idx_hbm, map_v)
            lanes = lax.iota(jnp.int32, SC_LANES)

            @pl.loop(0, per_w // SC_LANES)
            def _(j):
                idx_v[pl.ds(j * SC_LANES, SC_LANES)] = lax.rem(row0 + base + j * SC_LANES + lanes, table.shape[0])

            @pl.loop(0, n_src // SC_LANES)
            def _(a):
                local = map_v[pl.ds(a * SC_LANES, SC_LANES)] - (row0 + base)
                mine = jnp.logical_and(local >= 0, local < per_w)
                plsc.store_scatter(idx_v, [local], lax.div(a * SC_LANES + lanes, TOP_K), mask=mine)
        else:
            pltpu.sync_copy(idx_hbm.at[pl.ds(base, per_w)], idx_v)

        def fetch(chunk, rows_v, sem):
            return pltpu.make_async_copy(table_hbm.at[idx_v.at[pl.ds(chunk * SC_CHUNK, SC_CHUNK)]], rows_v, sem)

        def flush(chunk, rows_v):
            pltpu.sync_copy(rows_v, out_hbm.at[pl.ds(base + chunk * SC_CHUNK, SC_CHUNK)])

        fetch(0, rows_a, sem_a).start()

        @pl.loop(0, n_chunks, step=2)
        def _(c):
            fetch(c + 1, rows_b, sem_b).start()
            fetch(c, rows_a, sem_a).wait()
            flush(c, rows_a)

            @pl.when(c + 2 < n_chunks)
            def _():
                fetch(c + 2, rows_a, sem_a).start()

            fetch(c + 1, rows_b, sem_b).wait()
            flush(c + 1, rows_b)

    return gather(table, idx)


def _rows_from_tiles(ref, first, n_rows, stride=ROW_TILE):
    words = [ref[pl.ds(first + c, n_rows, stride=stride), :] for c in range(ROW_TILE)]
    lo = [pltpu.bitcast(w << 16, F32) for w in words]
    hi = [pltpu.bitcast(w & U32(0xFFFF0000), F32) for w in words]
    return jnp.concatenate(lo + hi, axis=-1)


def _rows_to_tiles(ref, val):
    half = D_MODEL // 2

    def bits(x):
        return pltpu.bitcast(x.astype(BF16).astype(F32), U32)

    for c in range(ROW_TILE):
        lo = bits(val[:, c * 128:(c + 1) * 128])
        hi = bits(val[:, half + c * 128:half + (c + 1) * 128])
        ref[pl.ds(c, val.shape[0], stride=ROW_TILE), :] = hi | (lo >> 16)


def _ffn_kernel(be_ref, nx_ref, par_ref, nvb_ref, x_ref, w1_hbm, w3_hbm, w2_hbm, prev_ref, y_ref,
                w1f_ref, w3f_ref, w2f_ref, w1b_ref, w3b_ref, w2b_ref, sem, *, layer):
    b = pl.program_id(0)
    nvb = nvb_ref[0]
    expert = be_ref[b]
    slot = par_ref[b]
    new_expert = jnp.logical_or(b == 0, expert != be_ref[jnp.maximum(b - 1, 0)])

    def fetch(e, sl):
        return [pltpu.make_async_copy(w_hbm.at[layer, e], wf_ref.at[sl], sem.at[sl])
                for w_hbm, wf_ref in ((w1_hbm, w1f_ref), (w3_hbm, w3f_ref), (w2_hbm, w2f_ref))]

    @pl.when(jnp.logical_and(b == 0, nvb > 0))
    def _():
        for cp in fetch(expert, slot):
            cp.start()

    @pl.when(jnp.logical_and(b < nvb, new_expert))
    def _():
        for cp in fetch(expert, slot):
            cp.wait()

        @pl.when(nx_ref[b] >= 0)
        def _():
            for cp in fetch(nx_ref[b], 1 - slot):
                cp.start()

        w1b_ref[...] = w1f_ref[slot].astype(BF16)
        w3b_ref[...] = w3f_ref[slot].astype(BF16)
        w2b_ref[...] = w2f_ref[slot].astype(BF16)

    @pl.when(b < nvb)
    def _():
        xb = _rows_from_tiles(x_ref, 0, FFN_BM).astype(BF16)
        h1 = jnp.dot(xb, w1b_ref[...], preferred_element_type=F32)
        h3 = jnp.dot(xb, w3b_ref[...], preferred_element_type=F32)
        a = (jax.nn.silu(h1) * h3).astype(BF16)
        _rows_to_tiles(y_ref, jnp.dot(a, w2b_ref[...], preferred_element_type=F32))

    @pl.when(b >= nvb)
    def _():
        y_ref[...] = jnp.zeros(y_ref.shape, U32)


def _expert_ffn(block_e, next_e, parity, nvb, xs, prev, first, n_blocks, w1, w3, w2, layer):
    nb = block_e.shape[0]
    if prev is None:
        prev = xs

    def used(b, *prefetch):
        return jnp.minimum(b, jnp.maximum(prefetch[-1][0] - 1, 0))

    return pl.pallas_call(
        functools.partial(_ffn_kernel, layer=layer),
        grid_spec=pltpu.PrefetchScalarGridSpec(
            num_scalar_prefetch=4,
            grid=(nb,),
            in_specs=[
                pl.BlockSpec((FFN_BM * ROW_TILE, 128), lambda b, *prefetch: (used(b, *prefetch), 0)),
                pl.BlockSpec(memory_space=pl.ANY),
                pl.BlockSpec(memory_space=pl.ANY),
                pl.BlockSpec(memory_space=pl.ANY),
                pl.BlockSpec(memory_space=pl.ANY),
            ],
            out_specs=pl.BlockSpec((FFN_BM * ROW_TILE, 128), lambda b, *prefetch: (first + b, 0)),
            scratch_shapes=[pltpu.VMEM((2, D_MODEL, EXPERT_FF), F32), pltpu.VMEM((2, D_MODEL, EXPERT_FF), F32),
                            pltpu.VMEM((2, EXPERT_FF, D_MODEL), F32),
                            pltpu.VMEM((D_MODEL, EXPERT_FF), BF16), pltpu.VMEM((D_MODEL, EXPERT_FF), BF16),
                            pltpu.VMEM((EXPERT_FF, D_MODEL), BF16),
                            pltpu.SemaphoreType.DMA((2,))],
        ),
        out_shape=jax.ShapeDtypeStruct((n_blocks * FFN_BM * ROW_TILE, 128), U32),
        input_output_aliases={} if first == 0 else {8: 0},
        compiler_params=_cparams(("arbitrary",)),
        name="expert_ffn",
    )(block_e, next_e, parity, nvb, xs, w1, w3, w2, prev)


def _comb_kernel(yg_ref, x1_ref, rw_ref, p_ref, gple_ref, wpg_ref, wpp_ref, gout_ref, prev_ref, o_ref, *, tm, final):
    w = rw_ref[...]
    y0 = _rows_from_tiles(yg_ref, 0, tm, stride=TOP_K * ROW_TILE)
    y1 = _rows_from_tiles(yg_ref, ROW_TILE, tm, stride=TOP_K * ROW_TILE)
    x2 = x1_ref[...] + (y0 * w[:, 0:1] + y1 * w[:, 1:2])
    e = jnp.dot(p_ref[...].astype(BF16), wpp_ref[...], preferred_element_type=F32)
    gate = jax.nn.sigmoid(jnp.dot(_rms(x2, gple_ref[...]).astype(BF16), wpg_ref[...], preferred_element_type=F32))
    x3 = x2 + gate * e
    o_ref[...] = _rms(x3, gout_ref[...]) if final else x3


def _combine_ple(yg, prev, first, x1, rw, p2d, gple, wpg, wpp, gout, tm, layer):
    T = x1.shape[0]
    nt = T // tm
    off = first // tm
    if prev is None:
        prev = x1
    return pl.pallas_call(
        functools.partial(_comb_kernel, tm=tm, final=layer == DEPTH - 1),
        grid=(yg.shape[0] // (tm * TOP_K * ROW_TILE),),
        in_specs=[
            pl.BlockSpec((tm * TOP_K * ROW_TILE, 128), lambda i: (i, 0)),
            pl.BlockSpec((tm, D_MODEL), lambda i: (off + i, 0)),
            pl.BlockSpec((tm, 128), lambda i: (off + i, 0)),
            pl.BlockSpec((tm, PLE_DIM), lambda i: (layer * nt + off + i, 0)),
            _layer_spec((1, D_MODEL), layer),
            _layer_spec((D_MODEL, D_MODEL), layer),
            _layer_spec((PLE_DIM, D_MODEL), layer),
            pl.BlockSpec((1, D_MODEL), lambda i: (0, 0)),
            pl.BlockSpec(memory_space=pl.ANY),
        ],
        out_specs=pl.BlockSpec((tm, D_MODEL), lambda i: (off + i, 0)),
        out_shape=jax.ShapeDtypeStruct((T, D_MODEL), F32),
        input_output_aliases={} if first == 0 else {8: 0},
        compiler_params=_cparams(("parallel",)),
        name="combine_ple",
    )(yg, x1, rw, p2d, gple, wpg, wpp, gout, prev)


def _rope_tables(pos, theta, rot_dim, offset, span, scale):
    half = rot_dim // 2
    inv = jnp.float32(theta) ** (-jnp.arange(half, dtype=F32) * 2.0 / rot_dim)
    ang = pos.astype(F32)[:, None] * inv
    rel = np.arange(128) - offset
    rot = (rel >= 0) & (rel < span)
    spread = np.zeros((half, 128), np.float32)
    spread[rel[rot] % half, np.nonzero(rot)[0]] = 1.0
    to_lanes = lambda t: jnp.dot(t, spread, precision=lax.Precision.HIGHEST)
    cos, sin = to_lanes(jnp.cos(ang)) + (~rot).astype(np.float32), to_lanes(jnp.sin(ang))
    return jnp.concatenate([cos, -sin, sin], axis=1) * jnp.float32(scale)


def _prep_w_in(w):
    c_q = w[..., 0:512]
    c_kv = w[..., 512:768]
    k_pe = w[..., 768:800]
    dil = w[..., 800:800 + 2304].reshape(w.shape[:-1] + (DIL_GROUPS, 3, DIL_HEADS, DIL_HEAD_DIM))

    def qk_layout(t):
        flat = lambda u: u.reshape(u.shape[:-2] + (-1,))
        return jnp.concatenate([flat(t[..., :DIL_HALF]), flat(t[..., PARTIAL_ROT:PARTIAL_ROT + DIL_REST]),
                                flat(t[..., DIL_HALF:PARTIAL_ROT]), flat(t[..., PARTIAL_ROT + DIL_REST:])], axis=-1)

    dil = jnp.concatenate([jnp.concatenate([qk_layout(dil[..., g, 0, :, :]), qk_layout(dil[..., g, 1, :, :]),
                                            dil[..., g, 2, :, :].reshape(w.shape[:-1] + (DIL_W,))], axis=-1)
                           for g in range(DIL_GROUPS)], axis=-1)
    gates = w[..., 3104:5152]
    zeros = lambda n: jnp.zeros(w.shape[:-1] + (n,), w.dtype)
    lo, hi = k_pe[..., :MLA_HALF], k_pe[..., MLA_HALF:]
    pad = MLA_SLOT - MLA_NOPE - MLA_ROPE
    kslot = jnp.concatenate([zeros(MLA_NOPE), lo, lo, zeros(pad), zeros(MLA_NOPE), hi, hi, zeros(pad)], axis=-1)
    return jnp.concatenate([gates, c_q, c_kv, kslot, dil], axis=-1).astype(BF16)


def _pad_heads(w, width):
    w = jnp.pad(w, ((0, 0), (0, 0), (0, 0), (0, MLA_SLOT - width)))
    return w.reshape(w.shape[0], w.shape[1], MLA_HEADS * MLA_SLOT).astype(BF16)


def _prep_w_q(w):
    w = w.reshape(DEPTH, MLA_Q_LORA, MLA_HEADS // 2, 2, MLA_NOPE + MLA_ROPE)
    nope = w[..., :MLA_NOPE]
    lo = w[..., MLA_NOPE:MLA_NOPE + MLA_HALF]
    hi = w[..., MLA_NOPE + MLA_HALF:]
    both = lambda t: t.reshape(t.shape[:-2] + (2 * MLA_HALF,))
    zeros = jnp.zeros(w.shape[:3] + (MLA_SLOT - MLA_NOPE - MLA_ROPE,), w.dtype)
    slot0 = jnp.concatenate([nope[..., 0, :], both(lo), zeros], axis=-1)
    slot1 = jnp.concatenate([nope[..., 1, :], both(hi), zeros], axis=-1)
    return jnp.stack([slot0, slot1], axis=-2).reshape(DEPTH, MLA_Q_LORA, MLA_HEADS * MLA_SLOT).astype(BF16)


def _prep_w_kv(w):
    w = w.reshape(DEPTH, MLA_KV_LORA, MLA_HEADS, MLA_NOPE + MLA_V)
    return _pad_heads(w[..., :MLA_NOPE], MLA_NOPE), _pad_heads(w[..., MLA_NOPE:], MLA_V)


def _dest_kernel(ri_ref, ps_ref, o_ref):
    ri = ri_ref[...].astype(F32)
    lane = lax.broadcasted_iota(jnp.int32, ri.shape, 1)
    ps = ps_ref[...]

    def col(k):
        return jnp.sum(jnp.where(lane == k, ri, 0.0), axis=-1, keepdims=True)

    def dest(k):
        start = jnp.sum(jnp.where(lane == col(k).astype(jnp.int32), ps, 0.0), axis=-1, keepdims=True)
        return (start + col(2 + k)).astype(jnp.int32)

    o_ref[...] = jnp.where(lane == 0, dest(0), jnp.where(lane == 1, dest(1), 0))


def _dest_rows(ri, pstarts, tm):
    T = ri.shape[0]
    ps = jnp.zeros((1, 128), F32).at[0, :N_EXPERTS].set(pstarts.astype(F32))
    return pl.pallas_call(
        _dest_kernel,
        grid=(T // tm,),
        in_specs=[pl.BlockSpec((tm, 128), lambda i: (i, 0)), pl.BlockSpec((1, 128), lambda i: (0, 0))],
        out_specs=pl.BlockSpec((tm, 128), lambda i: (i, 0)),
        out_shape=jax.ShapeDtypeStruct((T, 128), jnp.int32),
        compiler_params=_cparams(("parallel",)),
        name="dest_rows",
    )(ri, ps)


def _dispatch_plan(ri, cnt, T, tm):
    counts = cnt[0, :N_EXPERTS].astype(jnp.int32)
    pcounts = (counts + FFN_BM - 1) // FFN_BM * FFN_BM
    pends = jnp.cumsum(pcounts)
    pstarts = pends - pcounts
    dest = _dest_rows(ri, pstarts, tm)[:, 0:TOP_K]
    n_blocks = (T * TOP_K) // FFN_BM + N_EXPERTS
    first_row = jnp.arange(n_blocks, dtype=jnp.int32) * FFN_BM
    block_e = jnp.minimum(jnp.sum((pends[None, :] <= first_row[:, None]).astype(jnp.int32), axis=1), N_EXPERTS - 1)
    nvb = pends[-1] // FFN_BM
    after = pends // FFN_BM
    run_of_expert = jnp.cumsum((pcounts > 0).astype(jnp.int32)) - 1
    parity = (run_of_expert[block_e] % 2).astype(jnp.int32)
    per_range = n_blocks // FFN_RANGES
    ffn_plans = []
    for r in range(FFN_RANGES):
        lo, hi = r * per_range, (r + 1) * per_range
        end = jnp.minimum(nvb, hi)
        next_of_expert = jnp.where(after < end, block_e[jnp.minimum(after, n_blocks - 1)], -1)
        ffn_plans.append((block_e[lo:hi], next_of_expert[block_e[lo:hi]].astype(jnp.int32), parity[lo:hi],
                          jnp.clip(nvb - lo, 0, per_range).astype(jnp.int32).reshape(1)))
    return dest.reshape(T * TOP_K), n_blocks, ffn_plans


def kernel(x, p, positions, g_mix, w_in, g_q_lat, w_q_up, g_kv_lat, w_kv_up, w_branch_a, w_branch_b, w_out, g_ffn, w_router_grp, b_router_grp, w_router_exp, b_router_exp, w_exp_gate, w_exp_up, w_exp_down, g_ple, w_ple_gate, w_ple_proj, g_final):
    B, S, D = x.shape
    T = B * S
    pos = positions.reshape(T)
    dil_scale = DIL_HEAD_DIM ** -0.5 * LOG2E
    mla_scale = (MLA_NOPE + MLA_ROPE) ** -0.5 * LOG2E
    tab_dil = jnp.concatenate([_rope_tables(pos, ROPE_THETA, PARTIAL_ROT, 0, DIL_ROT_LANES, sc)
                               for sc in (dil_scale, 1.0)], axis=1)
    tab_mla = jnp.concatenate([_rope_tables(pos, MLA_ROPE_THETA, MLA_ROPE, MLA_NOPE, MLA_ROPE, sc)
                               for sc in (mla_scale, 1.0)], axis=1)

    w_in_p = _prep_w_in(w_in)
    wq = _prep_w_q(w_q_up)
    wk, wv = _prep_w_kv(w_kv_up)
    wr = jnp.concatenate([w_router_grp, w_router_exp, jnp.zeros((DEPTH, D, 128 - N_GROUPS - N_EXPERTS), F32)],
                         axis=-1).astype(BF16)
    br = jnp.concatenate([b_router_grp, b_router_exp.reshape(DEPTH, N_EXPERTS),
                          jnp.zeros((DEPTH, 128 - N_GROUPS - N_EXPERTS), F32)], axis=-1).reshape(DEPTH, 1, 128)
    wa, wb, wo = w_branch_a.astype(BF16), w_branch_b.astype(BF16), w_out.astype(BF16)
    wpg, wpp = w_ple_gate.astype(BF16), w_ple_proj.astype(BF16)
    gains = lambda g: g.reshape(DEPTH, 1, -1)

    xc = x.reshape(T, D)
    for i in range(DEPTH):
        main, d0, d1, d2 = _in_proj(xc, gains(g_mix), w_in_p, tab_dil, B, S, TM_IN, i)
        q, k, v = _mla_qkv(main, gains(g_q_lat), gains(g_kv_lat), wq, wk, wv, tab_mla, TM_QKV, i)
        o_a = _mla_attn(q, k, v, B, S, TQ)
        ogs, lss = [], []
        for (_, dil), qkv in zip(DIL_PATTERN, (d0, d1, d2)):
            og, ls = _dil_attn(qkv, B, dil, S // dil)
            ogs.append(og)
            lss.append(ls)
        x1, h2, lg = _post(xc, o_a, ogs, lss, main, wa, wb, wo, gains(g_ffn), wr, br, B, S, TM_POST, i)
        ri, rw, cnt = _route(lg, TM_ROUTE)
        dest, n_blocks, ffn_plans = _dispatch_plan(ri, cnt, T, TM_ROUTE)
        range_blocks = n_blocks // FFN_RANGES
        ys = None
        for r, plan in enumerate(ffn_plans):
            xs = _sc_gather_rows(h2.reshape(T, ROW_TILE, 128), dest, n_out=range_blocks * FFN_BM,
                                 row0=r * range_blocks * FFN_BM)
            ys = _expert_ffn(*plan, xs.reshape(-1, 128), ys, r * range_blocks, n_blocks,
                             w_exp_gate, w_exp_up, w_exp_down, i)
        xc = None
        for first in (0, T // 2):
            yg = _sc_gather_rows(ys.reshape(-1, ROW_TILE, 128), dest[first * TOP_K:(first + T // 2) * TOP_K])
            xc = _combine_ple(yg.reshape(-1, 128), xc, first, x1, rw, p.reshape(DEPTH * T, PLE_DIM), gains(g_ple),
                              wpg, wpp, g_final.reshape(1, D), TM_COMB, i)
    return xc.reshape(B, S, D)
```

```python
import functools
import math

import jax
import jax.numpy as jnp
import numpy as np
from jax import lax
from jax.experimental import pallas as pl
from jax.experimental.pallas import tpu as pltpu
from jax.experimental.pallas import tpu_sc as plsc

F32 = jnp.float32
BF16 = jnp.bfloat16

D_MODEL = 1024
DEPTH = 4
RMS_EPS = 1e-6
NEG = -1e30
LOG2E = math.log2(math.e)

MLA_HEADS = 16
MLA_Q_LORA = 512
MLA_KV_LORA = 256
MLA_NOPE = 64
MLA_ROPE = 32
MLA_V = 64
MLA_ROPE_THETA = 10000.0
MLA_SLOT = 128
MLA_HALF = MLA_ROPE // 2


def _mla_head_lanes(lane, head):
    l = lane % MLA_SLOT
    nope = jnp.logical_and(lane // MLA_SLOT == head, l < MLA_NOPE)
    rope = jnp.logical_and(l >= MLA_NOPE + head * MLA_HALF, l < MLA_NOPE + (head + 1) * MLA_HALF)
    return jnp.logical_or(nope, rope)

DIL_PATTERN = ((128, 1), (512, 4), (2048, 16))
DIL_GROUPS = 3
DIL_HEADS = 4
DIL_HEAD_DIM = 64
DIL_W = DIL_HEADS * DIL_HEAD_DIM
DIL_BLK = 128
DIL_UNROLL = 16
ROPE_THETA = 500000.0
PARTIAL_ROT = DIL_HEAD_DIM // 4
DIL_HALF = PARTIAL_ROT // 2
DIL_ROT_LANES = DIL_HEADS * DIL_HALF
DIL_REST = (DIL_HEAD_DIM - PARTIAL_ROT) // 2


def _dil_qk_head_of_lane(lane):
    l = lane % 128
    plain = l - DIL_ROT_LANES
    plain_head = sum((plain >= h * DIL_REST).astype(jnp.int32) for h in range(1, DIL_HEADS))
    return jnp.where(l < DIL_ROT_LANES, l // DIL_HALF, plain_head)

N_GROUPS = 8
EXPERTS_PER_GROUP = 8
N_EXPERTS = 64
TOP_K = 2
EXPERT_FF = 256
PLE_DIM = 256

IN_TN = 768
MAIN_COLS = 3072
IN_COLS_PAD = MAIN_COLS + DIL_GROUPS * 3 * DIL_W
N_MAIN_TILES = MAIN_COLS // IN_TN

FFN_BM = 256
FFN_RANGES = 2
ROW_TILE = D_MODEL // 256
U32 = jnp.uint32
VMEM_LIMIT = 48 * 1024 * 1024
TM_IN, TM_QKV, TQ, TM_POST, TM_ROUTE, TM_COMB = 1024, 1024, 512, 512, 1024, 1024

SC_CORES = 2
SC_SUBCORES = 16
SC_LANES = 16
SC_CHUNK = 64


def _cparams(sem):
    return pltpu.CompilerParams(dimension_semantics=sem, vmem_limit_bytes=VMEM_LIMIT)


def _layer_spec(shape, layer, col=None):
    def index(*grid_idx):
        return (layer,) + (0,) * (len(shape) - 1) + ((col(*grid_idx),) if col else (0,))

    return pl.BlockSpec((None,) + tuple(shape), index)


def _rms(x, g):
    return x * lax.rsqrt(jnp.mean(x * x, axis=-1, keepdims=True) + RMS_EPS) * g


def _in_kernel(x_ref, g_ref, w_ref, tab_ref, main_ref, d0_ref, d1_ref, d2_ref, xn_ref, acc_ref):
    j = pl.program_id(1)

    @pl.when(j == 0)
    def _():
        xn_ref[...] = _rms(x_ref[...], g_ref[...]).astype(BF16)

    acc = jnp.dot(xn_ref[...], w_ref[...], preferred_element_type=F32)

    @pl.when(j < N_MAIN_TILES)
    def _():
        main_ref[...] = acc.astype(BF16)

    def dil_tile(out_ref, dil):
        chunk = lambda c: acc[:, c * 128:(c + 1) * 128]
        for base, t0 in ((0, 0), (2, 3)):
            cos, sin_n, sin_p = [tab_ref[:, (t0 + t) * 128:(t0 + t + 1) * 128] for t in range(3)]
            lo, hi = chunk(base), chunk(base + 1)
            acc_ref[base] = lo * cos + hi * sin_n
            acc_ref[base + 1] = hi * cos + lo * sin_p
        for c in range(4, 6):
            acc_ref[c] = chunk(c)
        rows = acc_ref.shape[1] // dil
        for r in range(dil):
            for c in range(6):
                out_ref[0, r, :, c * 128:(c + 1) * 128] = acc_ref[c, pl.ds(r, rows, stride=dil), :].astype(BF16)

    for gi, (_, dil) in enumerate(DIL_PATTERN):
        pl.when(j == N_MAIN_TILES + gi)(functools.partial(dil_tile, (d0_ref, d1_ref, d2_ref)[gi], dil))


def _in_proj(x2d, g, w, tab, B, S, tm, layer):
    T = x2d.shape[0]
    nt = S // tm
    dil_shapes = [jax.ShapeDtypeStruct((B, d, S // d, 3 * DIL_W), BF16) for _, d in DIL_PATTERN]
    dil_specs = [pl.BlockSpec((1, d, tm // d, 3 * DIL_W), lambda i, j, nt=nt: (i // nt, 0, i % nt, 0))
                 for _, d in DIL_PATTERN]
    return pl.pallas_call(
        _in_kernel,
        grid=(T // tm, IN_COLS_PAD // IN_TN),
        in_specs=[
            pl.BlockSpec((tm, D_MODEL), lambda i, j: (i, 0)),
            _layer_spec((1, D_MODEL), layer),
            _layer_spec((D_MODEL, IN_TN), layer, col=lambda i, j: j),
            pl.BlockSpec((tm, 6 * 128), lambda i, j: (i, 0)),
        ],
        out_specs=[pl.BlockSpec((tm, IN_TN), lambda i, j: (i, jnp.minimum(j, N_MAIN_TILES - 1)))] + dil_specs,
        out_shape=[jax.ShapeDtypeStruct((T, MAIN_COLS), BF16)] + dil_shapes,
        scratch_shapes=[pltpu.VMEM((tm, D_MODEL), BF16), pltpu.VMEM((IN_TN // 128, tm, 128), F32)],
        compiler_params=_cparams(("parallel", "arbitrary")),
        name="in_proj",
    )(x2d, g, w, tab)


def _qkv_kernel(cq_ref, ckv_ref, kpe_ref, gq_ref, gkv_ref, wq_ref, wk_ref, wv_ref, tab_ref,
                q_ref, k_ref, v_ref):
    qn = _rms(cq_ref[...].astype(F32), gq_ref[...]).astype(BF16)
    kvn = _rms(ckv_ref[...].astype(F32), gkv_ref[...]).astype(BF16)
    qacc = jnp.dot(qn, wq_ref[...], preferred_element_type=F32)
    kacc = jnp.dot(kvn, wk_ref[...], preferred_element_type=F32)
    vacc = jnp.dot(kvn, wv_ref[...], preferred_element_type=F32)
    lane_v = lax.broadcasted_iota(jnp.int32, vacc.shape, 1)
    v_ref[...] = jnp.where(lane_v % MLA_SLOT < MLA_V, vacc, 1.0).astype(BF16)
    cq, snq, spq, ck, snk, spk = [tab_ref[:, t * 128:(t + 1) * 128] for t in range(6)]
    kpe_lo = kpe_ref[:, 0:128].astype(F32)
    kpe_hi = kpe_ref[:, 128:256].astype(F32)
    krot = (kpe_lo * ck + kpe_hi * snk, kpe_hi * ck + kpe_lo * spk)
    for pair in range(MLA_HEADS // 2):
        s0 = slice(2 * pair * MLA_SLOT, (2 * pair + 1) * MLA_SLOT)
        s1 = slice((2 * pair + 1) * MLA_SLOT, (2 * pair + 2) * MLA_SLOT)
        q0, q1 = qacc[:, s0], qacc[:, s1]
        q_ref[:, s0] = (q0 * cq + q1 * snq).astype(BF16)
        q_ref[:, s1] = (q1 * cq + q0 * spq).astype(BF16)
        k_ref[:, s0] = (kacc[:, s0] + krot[0]).astype(BF16)
        k_ref[:, s1] = (kacc[:, s1] + krot[1]).astype(BF16)


def _mla_qkv(main, gq, gkv, wq, wk, wv, tab, tm, layer):
    T = main.shape[0]
    return pl.pallas_call(
        _qkv_kernel,
        grid=(T // tm,),
        in_specs=[
            pl.BlockSpec((tm, MLA_Q_LORA), lambda i: (i, 2048 // MLA_Q_LORA)),
            pl.BlockSpec((tm, MLA_KV_LORA), lambda i: (i, 2560 // MLA_KV_LORA)),
            pl.BlockSpec((tm, 256), lambda i: (i, 2816 // 256)),
            _layer_spec((1, MLA_Q_LORA), layer),
            _layer_spec((1, MLA_KV_LORA), layer),
            _layer_spec((MLA_Q_LORA, MLA_HEADS * MLA_SLOT), layer),
            _layer_spec((MLA_KV_LORA, MLA_HEADS * MLA_SLOT), layer),
            _layer_spec((MLA_KV_LORA, MLA_HEADS * MLA_SLOT), layer),
            pl.BlockSpec((tm, 6 * 128), lambda i: (i, 0)),
        ],
        out_specs=[
            pl.BlockSpec((tm, MLA_HEADS * MLA_SLOT), lambda i: (i, 0)),
            pl.BlockSpec((tm, MLA_HEADS * MLA_SLOT), lambda i: (i, 0)),
            pl.BlockSpec((tm, MLA_HEADS * MLA_SLOT), lambda i: (i, 0)),
        ],
        out_shape=[
            jax.ShapeDtypeStruct((T, MLA_HEADS * MLA_SLOT), BF16),
            jax.ShapeDtypeStruct((T, MLA_HEADS * MLA_SLOT), BF16),
            jax.ShapeDtypeStruct((T, MLA_HEADS * MLA_SLOT), BF16),
        ],
        compiler_params=_cparams(("parallel",)),
        name="mla_qkv",
    )(main, main, main, gq, gkv, wq, wk, wv, tab)


def _mla_attn_kernel(qa_ref, qb_ref, k_ref, v_ref, o_ref, m_ref, acc_ref, *, tq):
    p_id = pl.program_id(2)
    nq = k_ref.shape[1] // tq
    lane = lax.broadcasted_iota(jnp.int32, (tq, 2 * MLA_SLOT), 1)
    qh = []
    for q_ref in (qa_ref, qb_ref):
        qp = q_ref[0]
        zero = jnp.zeros_like(qp)
        qh.append([jnp.where(_mla_head_lanes(lane, h), qp, zero) for h in range(2)])
    hq = tq // 2
    dn = (((1,), (1,)), ((), ()))

    def step(blk, start, width, r0, mask, first=False):
        kb = k_ref[0, pl.ds(start, width), :]
        vb = v_ref[0, pl.ds(start, width), :]
        rows = tq - r0
        s_pair = lax.dot_general(jnp.concatenate([qh[blk][0][r0:], qh[blk][1][r0:]], axis=0), kb, dn,
                                 preferred_element_type=F32)
        for h in range(2):
            s = s_pair[h * rows:(h + 1) * rows]
            if mask is not None:
                s = jnp.where(mask, s, NEG)
            m_cur = jnp.max(s, axis=-1, keepdims=True)
            if first:
                m_new = jnp.broadcast_to(m_cur, (tq - r0, 128))
            else:
                m_prev = m_ref[blk, h, r0:, :]
                m_new = jnp.maximum(m_prev, m_cur)
            p = jnp.exp2(s - jnp.concatenate([m_new] * (width // 128), axis=-1))
            pv = jnp.dot(p.astype(BF16), vb[:, h * MLA_SLOT:(h + 1) * MLA_SLOT], preferred_element_type=F32)
            if first:
                acc_ref[blk, h, r0:, :] = pv
            else:
                acc_ref[blk, h, r0:, :] = jnp.exp2(m_prev - m_new) * acc_ref[blk, h, r0:, :] + pv
            m_ref[blk, h, r0:, :] = m_new

    def causal(rows):
        return lax.broadcasted_iota(jnp.int32, (rows, hq), 1) <= lax.broadcasted_iota(jnp.int32, (rows, hq), 0)

    def q_blocks(n_a):
        n_b = nq - 1 - n_a
        for blk, n_full in ((1, n_b), (0, n_a)):
            j = 0
            while j < n_full:
                wide = 2 if j + 1 < n_full else 1
                step(blk, j * tq, wide * tq, 0, None, first=j == 0)
                j += wide
            step(blk, n_full * tq, hq, 0, causal(tq), first=n_full == 0)
            step(blk, n_full * tq + hq, hq, hq, causal(hq))

    for n_a in range(nq // 2):
        pl.when(p_id == n_a)(functools.partial(q_blocks, n_a))
    lane_o = lax.broadcasted_iota(jnp.int32, (tq, 2 * MLA_V), 1)
    for blk in range(2):
        outs = []
        for h in range(2):
            a = acc_ref[blk, h]
            outs.append(a / pltpu.roll(a, MLA_V, 1))
        o_ref[0, blk, 0] = jnp.where(lane_o < MLA_V, outs[0], pltpu.roll(outs[1], MLA_V, 1)).astype(BF16)


def _mla_attn(q, k, v, B, S, tq):
    nq = S // tq
    q = q.reshape(B, S, MLA_HEADS * MLA_SLOT)
    k = k.reshape(B, S, MLA_HEADS * MLA_SLOT)
    v = v.reshape(B, S, MLA_HEADS * MLA_SLOT)
    return pl.pallas_call(
        functools.partial(_mla_attn_kernel, tq=tq),
        grid=(B, MLA_HEADS // 2, nq // 2),
        in_specs=[
            pl.BlockSpec((1, tq, 2 * MLA_SLOT), lambda b, h, p: (b, p, h)),
            pl.BlockSpec((1, tq, 2 * MLA_SLOT), lambda b, h, p: (b, nq - 1 - p, h)),
            pl.BlockSpec((1, S, 2 * MLA_SLOT), lambda b, h, p: (b, 0, h)),
            pl.BlockSpec((1, S, 2 * MLA_SLOT), lambda b, h, p: (b, 0, h)),
        ],
        out_specs=pl.BlockSpec((1, 2, 1, tq, 2 * MLA_V), lambda b, h, p: (b, 0, p, 0, h)),
        out_shape=jax.ShapeDtypeStruct((B, 2, nq // 2, tq, MLA_HEADS * MLA_V), BF16),
        scratch_shapes=[pltpu.VMEM((2, 2, tq, 128), F32), pltpu.VMEM((2, 2, tq, MLA_SLOT), F32)],
        compiler_params=_cparams(("parallel", "parallel", "arbitrary")),
        name="mla_attn",
    )(q, q, k, v)


def _dil_kernel(qkv_ref, o_ref, lse_ref, bias_ref, *, nb):
    dil = qkv_ref.shape[1]
    width = 2 * DIL_BLK if nb > 1 else DIL_BLK
    row = lax.broadcasted_iota(jnp.int32, (DIL_BLK, width), 0)
    col = lax.broadcasted_iota(jnp.int32, (DIL_BLK, width), 1)
    bias_ref[0] = jnp.where(col <= row, 0.0, NEG)
    if nb > 1:
        later = jnp.logical_or(jnp.logical_and(col >= DIL_BLK, col - DIL_BLK <= row),
                               jnp.logical_and(col < DIL_BLK, col >= row))
        bias_ref[1] = jnp.where(later, 0.0, NEG)
    lane = lax.broadcasted_iota(jnp.int32, (DIL_BLK, DIL_W), 1)
    head_of_lane = lane // DIL_HEAD_DIM
    qk_head_of_lane = _dil_qk_head_of_lane(lane)
    dn = (((1,), (1,)), ((), ()))

    def by_head(parts):
        out = parts[DIL_HEADS - 1]
        for h in range(DIL_HEADS - 2, -1, -1):
            out = jnp.where(head_of_lane == h, parts[h], out)
        return out

    def unit(u):
        r = u // nb
        n = u % nb
        q0 = pl.multiple_of(n * DIL_BLK, DIL_BLK)
        q = qkv_ref[0, r, pl.ds(q0, DIL_BLK), 0:DIL_W]
        if nb > 1:
            w0 = pl.multiple_of(jnp.maximum(n - 1, 0) * DIL_BLK, DIL_BLK)
            bias = bias_ref[jnp.minimum(n, 1)]
        else:
            w0 = 0
            bias = bias_ref[0]
        kw = qkv_ref[0, r, pl.ds(w0, width), DIL_W:2 * DIL_W]
        vw = qkv_ref[0, r, pl.ds(w0, width), 2 * DIL_W:3 * DIL_W]
        zero = jnp.zeros_like(q)
        qs = jnp.concatenate([jnp.where(qk_head_of_lane == h, q, zero) for h in range(DIL_HEADS)], axis=0)
        s = lax.dot_general(qs, kw, dn, preferred_element_type=F32)
        s = (s.reshape(DIL_HEADS, DIL_BLK, width) + bias[None]).reshape(DIL_HEADS * DIL_BLK, width)
        m = jnp.max(s, axis=-1, keepdims=True)
        e = jnp.exp2(s - m)
        den = jnp.sum(e, axis=-1, keepdims=True)
        pv = jnp.dot(e.astype(BF16), vw, preferred_element_type=F32)
        lse = m + jnp.log2(den)
        blk = lambda t, h: t[h * DIL_BLK:(h + 1) * DIL_BLK]
        o = by_head([blk(pv, h) for h in range(DIL_HEADS)]) / by_head(
            [jnp.broadcast_to(blk(den, h), (DIL_BLK, DIL_W)) for h in range(DIL_HEADS)])
        o_ref[0, r, pl.ds(q0, DIL_BLK), :] = o.astype(BF16)
        lse_ref[0, r, pl.ds(q0, DIL_BLK), :] = by_head(
            [jnp.broadcast_to(blk(lse, h), (DIL_BLK, DIL_W)) for h in range(DIL_HEADS)])

    def body(t, carry):
        for u in range(DIL_UNROLL):
            unit(DIL_UNROLL * t + u)
        return carry

    lax.fori_loop(0, dil * nb // DIL_UNROLL, body, 0)


def _dil_attn(qkv, B, dil, L):
    nb = L // DIL_BLK
    return pl.pallas_call(
        functools.partial(_dil_kernel, nb=nb),
        grid=(B,),
        in_specs=[pl.BlockSpec((1, dil, L, 3 * DIL_W), lambda b: (b, 0, 0, 0))],
        out_specs=[
            pl.BlockSpec((1, dil, L, DIL_W), lambda b: (b, 0, 0, 0)),
            pl.BlockSpec((1, dil, L, DIL_W), lambda b: (b, 0, 0, 0)),
        ],
        out_shape=[
            jax.ShapeDtypeStruct((B, dil, L, DIL_W), BF16),
            jax.ShapeDtypeStruct((B, dil, L, DIL_W), F32),
        ],
        scratch_shapes=[pltpu.VMEM((2, DIL_BLK, 2 * DIL_BLK if nb > 1 else DIL_BLK), F32)],
        compiler_params=_cparams(("parallel",)),
        name=f"dil_attn_d{dil}",
    )(qkv)


def _post_kernel(x_ref, oa_ref, og0_ref, og1_ref, og2_ref, ls0_ref, ls1_ref, ls2_ref, gates_ref,
                 wa_ref, wb_ref, wo_ref, gffn_ref, wr_ref, br_ref,
                 x1_ref, h2_ref, lg_ref, ob_ref, *, tm):
    ls =[r[0] for r in (ls0_ref, ls1_ref, ls2_ref)]
    og = [r[0] for r in (og0_ref, og1_ref, og2_ref)]
    for gi, (_, dil) in enumerate(DIL_PATTERN):
        rows = tm // dil
        for r in range(dil):
            for c in range(2):
                sl = slice(c * 128, (c + 1) * 128)
                ob_ref[2 * gi + c, pl.ds(r, rows, stride=dil), :] = og[gi][r][:, sl].astype(F32)
                ob_ref[6 + 2 * gi + c, pl.ds(r, rows, stride=dil), :] = ls[gi][r][:, sl]

    def tok_major(k):
        return jnp.concatenate([ob_ref[2 * k], ob_ref[2 * k + 1]], axis=-1)

    l0, l1, l2 = tok_major(3), tok_major(4), tok_major(5)
    mx = jnp.maximum(jnp.maximum(l0, l1), l2)
    w0, w1, w2 = jnp.exp2(l0 - mx), jnp.exp2(l1 - mx), jnp.exp2(l2 - mx)
    ob = (w0 * tok_major(0) + w1 * tok_major(1) + w2 * tok_major(2)) / (w0 + w1 + w2)
    ya = jnp.dot(oa_ref[0, 0, 0], wa_ref[...], preferred_element_type=F32)
    yb = jnp.dot(ob.astype(BF16), wb_ref[...], preferred_element_type=F32)
    merged = (jax.nn.sigmoid(gates_ref[:, 0:D_MODEL].astype(F32)) * ya
              + jax.nn.sigmoid(gates_ref[:, D_MODEL:2 * D_MODEL].astype(F32)) * yb)
    x1 = x_ref[...] + jnp.dot(merged.astype(BF16), wo_ref[...], preferred_element_type=F32)
    x1_ref[...] = x1
    h2 = _rms(x1, gffn_ref[...])
    _rows_to_tiles(h2_ref, h2)
    lg_ref[...] = jnp.dot(h2.astype(BF16), wr_ref[...], preferred_element_type=F32) + br_ref[...]


def _post(x2d, oa, ogs, lss, main, wa, wb, wo, gffn, wr, br, B, S, tm, layer):
    T = x2d.shape[0]
    nt = S // tm
    res_specs = [pl.BlockSpec((1, d, tm // d, DIL_W), lambda i, nt=nt: (i // nt, 0, i % nt, 0))
                 for _, d in DIL_PATTERN]
    assert oa.shape[3] == tm and oa.shape[2] * 2 == nt

    def oa_index(i):
        it = i % nt
        late = it >= nt // 2
        return (i // nt, late.astype(jnp.int32), jnp.where(late, nt - 1 - it, it), 0, 0)

    return pl.pallas_call(
        functools.partial(_post_kernel, tm=tm),
        grid=(T // tm,),
        in_specs=[pl.BlockSpec((tm, D_MODEL), lambda i: (i, 0)),
                  pl.BlockSpec((1, 1, 1, tm, D_MODEL), oa_index)]
                 + res_specs + res_specs
                 + [pl.BlockSpec((tm, 2 * D_MODEL), lambda i: (i, 0)),
                    _layer_spec((D_MODEL, D_MODEL), layer),
                    _layer_spec((DIL_W, D_MODEL), layer),
                    _layer_spec((D_MODEL, D_MODEL), layer),
                    _layer_spec((1, D_MODEL), layer),
                    _layer_spec((D_MODEL, 128), layer),
                    _layer_spec((1, 128), layer)],
        out_specs=[pl.BlockSpec((tm, D_MODEL), lambda i: (i, 0)),
                   pl.BlockSpec((tm * ROW_TILE, 128), lambda i: (i, 0)),
                   pl.BlockSpec((tm, 128), lambda i: (i, 0))],
        out_shape=[jax.ShapeDtypeStruct((T, D_MODEL), F32),
                   jax.ShapeDtypeStruct((T * ROW_TILE, 128), U32),
                   jax.ShapeDtypeStruct((T, 128), F32)],
        scratch_shapes=[pltpu.VMEM((12, tm, 128), F32)],
        compiler_params=_cparams(("parallel",)),
        name="post_attn",
    )(x2d, oa, *ogs, *lss, main, wa, wb, wo, gffn, wr, br)


def _route_kernel(lg_ref, ri_ref, rw_ref, cnt_ref, carry_ref, lower_ref, *, tm):
    i = pl.program_id(0)

    @pl.when(i == 0)
    def _():
        carry_ref[...] = jnp.zeros(carry_ref.shape, F32)
        r_i = lax.broadcasted_iota(jnp.int32, (tm, tm), 0)
        c_i = lax.broadcasted_iota(jnp.int32, (tm, tm), 1)
        lower_ref[...] = jnp.where(c_i < r_i, 1.0, 0.0).astype(BF16)

    lg = lg_ref[...]
    lane = lax.broadcasted_iota(jnp.int32, lg.shape, 1)
    lane_f = lane.astype(F32)
    ninf = jnp.float32(-jnp.inf)

    def first_max(vals):
        vmax = jnp.max(vals, axis=-1, keepdims=True)
        idx = jnp.min(jnp.where(vals == vmax, lane_f, 128.0), axis=-1, keepdims=True)
        return vmax, idx.astype(jnp.int32)

    gl = jnp.where(lane < N_GROUPS, lg, ninf)
    gmax, g_sel = first_max(gl)
    p_g = 1.0 / jnp.sum(jnp.exp(gl - gmax), axis=-1, keepdims=True)
    lo = N_GROUPS + g_sel * EXPERTS_PER_GROUP
    el = jnp.where(jnp.logical_and(lane >= lo, lane < lo + EXPERTS_PER_GROUP), lg, ninf)
    v0, i0 = first_max(el)
    v1, i1 = first_max(jnp.where(lane == i0, ninf, el))
    t = jnp.exp(v1 - v0)
    w0 = p_g / (1.0 + t)
    w1 = p_g * t / (1.0 + t)
    e0 = i0 - N_GROUPS
    e1 = i1 - N_GROUPS
    hit0 = lane == e0
    hit1 = lane == e1
    oh = jnp.where(jnp.logical_or(hit0, hit1), 1.0, 0.0).astype(F32)
    excl = jnp.dot(lower_ref[...], oh.astype(BF16), preferred_element_type=F32) + carry_ref[...]
    r0 = jnp.sum(jnp.where(hit0, excl, 0.0), axis=-1, keepdims=True).astype(jnp.int32)
    r1 = jnp.sum(jnp.where(hit1, excl, 0.0), axis=-1, keepdims=True).astype(jnp.int32)
    carry_ref[...] = carry_ref[...] + jnp.sum(oh, axis=0, keepdims=True)
    zi = jnp.zeros(lg.shape, jnp.int32)
    ri_ref[...] = jnp.where(lane == 0, e0, jnp.where(lane == 1, e1, jnp.where(lane == 2, r0, jnp.where(lane == 3, r1, zi))))
    rw_ref[...] = jnp.where(lane == 0, w0, jnp.where(lane == 1, w1, jnp.zeros(lg.shape, F32)))
    cnt_ref[...] = carry_ref[...]


def _route(lg, tm):
    T = lg.shape[0]
    return pl.pallas_call(
        functools.partial(_route_kernel, tm=tm),
        grid=(T // tm,),
        in_specs=[pl.BlockSpec((tm, 128), lambda i: (i, 0))],
        out_specs=[pl.BlockSpec((tm, 128), lambda i: (i, 0)),
                   pl.BlockSpec((tm, 128), lambda i: (i, 0)),
                   pl.BlockSpec((1, 128), lambda i: (0, 0))],
        out_shape=[jax.ShapeDtypeStruct((T, 128), jnp.int32),
                   jax.ShapeDtypeStruct((T, 128), F32),
                   jax.ShapeDtypeStruct((1, 128), F32)],
        scratch_shapes=[pltpu.VMEM((1, 128), F32), pltpu.VMEM((tm, tm), BF16)],
        compiler_params=_cparams(("arbitrary",)),
        name="route",
    )(lg)


def _sc_gather_rows(table, idx, n_out=None, row0=0):
    inverse = n_out is not None
    n_src = idx.shape[0]
    n = n_out if inverse else n_src
    n_workers = SC_CORES * SC_SUBCORES
    per_w = n // n_workers
    n_chunks = per_w // SC_CHUNK
    assert per_w * n_workers == n and n_chunks * SC_CHUNK == per_w and n_chunks % 2 == 0
    mesh = plsc.VectorSubcoreMesh(core_axis_name="c", subcore_axis_name="s",
                                  num_cores=SC_CORES, num_subcores=SC_SUBCORES)

    @functools.partial(
        pl.kernel, mesh=mesh,
        out_type=jax.ShapeDtypeStruct((n,) + table.shape[1:], table.dtype),
        scratch_types=[pltpu.VMEM((per_w,), jnp.int32),
                       pltpu.VMEM((SC_CHUNK,) + table.shape[1:], table.dtype),
                       pltpu.VMEM((SC_CHUNK,) + table.shape[1:], table.dtype),
                       pltpu.SemaphoreType.DMA, pltpu.SemaphoreType.DMA,
                       pltpu.VMEM((n_src if inverse else SC_LANES,), jnp.int32)],
        compiler_params=pltpu.CompilerParams(use_tc_tiling_on_sc=True, needs_layout_passes=not inverse),
        name="sc_dispatch_rows" if inverse else "sc_gather_rows",
    )
    def gather(table_hbm, idx_hbm, out_hbm, idx_v, rows_a, rows_b, sem_a, sem_b, map_v):
        wid = lax.axis_index("s") * SC_CORES + lax.axis_index("c")
        base = wid * per_w
        if inverse:
            pltpu.sync_copy(idx_hbm, map_v)
            lanes = lax.iota(jnp.int32, SC_LANES)

            @pl.loop(0, per_w // SC_LANES)
            def _(j):
                idx_v[pl.ds(j * SC_LANES, SC_LANES)] = lax.rem(row0 + base + j * SC_LANES + lanes, table.shape[0])

            @pl.loop(0, n_src // SC_LANES)
            def _(a):
                local = map_v[pl.ds(a * SC_LANES, SC_LANES)] - (row0 + base)
                mine = jnp.logical_and(local >= 0, local < per_w)
                plsc.store_scatter(idx_v, [local], lax.div(a * SC_LANES + lanes, TOP_K), mask=mine)
        else:
            pltpu.sync_copy(idx_hbm.at[pl.ds(base, per_w)], idx_v)

        def fetch(chunk, rows_v, sem):
            return pltpu.make_async_copy(table_hbm.at[idx_v.at[pl.ds(chunk * SC_CHUNK, SC_CHUNK)]], rows_v, sem)

        def flush(chunk, rows_v):
            pltpu.sync_copy(rows_v, out_hbm.at[pl.ds(base + chunk * SC_CHUNK, SC_CHUNK)])

        fetch(0, rows_a, sem_a).start()

        @pl.loop(0, n_chunks, step=2)
        def _(c):
            fetch(c + 1, rows_b, sem_b).start()
            fetch(c, rows_a, sem_a).wait()
            flush(c, rows_a)

            @pl.when(c + 2 < n_chunks)
            def _():
                fetch(c + 2, rows_a, sem_a).start()

            fetch(c + 1, rows_b, sem_b).wait()
            flush(c + 1, rows_b)

    return gather(table, idx)


def _rows_from_tiles(ref, first, n_rows, stride=ROW_TILE):
    words = [ref[pl.ds(first + c, n_rows, stride=stride), :] for c in range(ROW_TILE)]
    lo = [pltpu.bitcast(w << 16, F32) for w in words]
    hi = [pltpu.bitcast(w & U32(0xFFFF0000), F32) for w in words]
    return jnp.concatenate(lo + hi, axis=-1)


def _rows_to_tiles(ref, val):
    half = D_MODEL // 2

    def bits(x):
        return pltpu.bitcast(x.astype(BF16).astype(F32), U32)

    for c in range(ROW_TILE):
        lo = bits(val[:, c * 128:(c + 1) * 128])
        hi = bits(val[:, half + c * 128:half + (c + 1) * 128])
        ref[pl.ds(c, val.shape[0], stride=ROW_TILE), :] = hi | (lo >> 16)


def _ffn_kernel(be_ref, nx_ref, par_ref, nvb_ref, x_ref, w1_hbm, w3_hbm, w2_hbm, prev_ref, y_ref,
                w1f_ref, w3f_ref, w2f_ref, w1b_ref, w3b_ref, w2b_ref, sem, *, layer):
    b = pl.program_id(0)
    nvb = nvb_ref[0]
    expert = be_ref[b]
    slot = par_ref[b]
    new_expert = jnp.logical_or(b == 0, expert != be_ref[jnp.maximum(b - 1, 0)])

    def fetch(e, sl):
        return [pltpu.make_async_copy(w_hbm.at[layer, e], wf_ref.at[sl], sem.at[sl])
                for w_hbm, wf_ref in ((w1_hbm, w1f_ref), (w3_hbm, w3f_ref), (w2_hbm, w2f_ref))]

    @pl.when(jnp.logical_and(b == 0, nvb > 0))
    def _():
        for cp in fetch(expert, slot):
            cp.start()

    @pl.when(jnp.logical_and(b < nvb, new_expert))
    def _():
        for cp in fetch(expert, slot):
            cp.wait()

        @pl.when(nx_ref[b] >= 0)
        def _():
            for cp in fetch(nx_ref[b], 1 - slot):
                cp.start()

        w1b_ref[...] = w1f_ref[slot].astype(BF16)
        w3b_ref[...] = w3f_ref[slot].astype(BF16)
        w2b_ref[...] = w2f_ref[slot].astype(BF16)

    @pl.when(b < nvb)
    def _():
        xb = _rows_from_tiles(x_ref, 0, FFN_BM).astype(BF16)
        h1 = jnp.dot(xb, w1b_ref[...], preferred_element_type=F32)
        h3 = jnp.dot(xb, w3b_ref[...], preferred_element_type=F32)
        a = (jax.nn.silu(h1) * h3).astype(BF16)
        _rows_to_tiles(y_ref, jnp.dot(a, w2b_ref[...], preferred_element_type=F32))

    @pl.when(b >= nvb)
    def _():
        y_ref[...] = jnp.zeros(y_ref.shape, U32)


def _expert_ffn(block_e, next_e, parity, nvb, xs, prev, first, n_blocks, w1, w3, w2, layer):
    nb = block_e.shape[0]
    if prev is None:
        prev = xs

    def used(b, *prefetch):
        return jnp.minimum(b, jnp.maximum(prefetch[-1][0] - 1, 0))

    return pl.pallas_call(
        functools.partial(_ffn_kernel, layer=layer),
        grid_spec=pltpu.PrefetchScalarGridSpec(
            num_scalar_prefetch=4,
            grid=(nb,),
            in_specs=[
                pl.BlockSpec((FFN_BM * ROW_TILE, 128), lambda b, *prefetch: (used(b, *prefetch), 0)),
                pl.BlockSpec(memory_space=pl.ANY),
                pl.BlockSpec(memory_space=pl.ANY),
                pl.BlockSpec(memory_space=pl.ANY),
                pl.BlockSpec(memory_space=pl.ANY),
            ],
            out_specs=pl.BlockSpec((FFN_BM * ROW_TILE, 128), lambda b, *prefetch: (first + b, 0)),
            scratch_shapes=[pltpu.VMEM((2, D_MODEL, EXPERT_FF), F32), pltpu.VMEM((2, D_MODEL, EXPERT_FF), F32),
                            pltpu.VMEM((2, EXPERT_FF, D_MODEL), F32),
                            pltpu.VMEM((D_MODEL, EXPERT_FF), BF16), pltpu.VMEM((D_MODEL, EXPERT_FF), BF16),
                            pltpu.VMEM((EXPERT_FF, D_MODEL), BF16),
                            pltpu.SemaphoreType.DMA((2,))],
        ),
        out_shape=jax.ShapeDtypeStruct((n_blocks * FFN_BM * ROW_TILE, 128), U32),
        input_output_aliases={} if first == 0 else {8: 0},
        compiler_params=_cparams(("arbitrary",)),
        name="expert_ffn",
    )(block_e, next_e, parity, nvb, xs, w1, w3, w2, prev)


def _comb_kernel(yg_ref, x1_ref, rw_ref, p_ref, gple_ref, wpg_ref, wpp_ref, gout_ref, prev_ref, o_ref, *, tm, final):
    w = rw_ref[...]
    y0 = _rows_from_tiles(yg_ref, 0, tm, stride=TOP_K * ROW_TILE)
    y1 = _rows_from_tiles(yg_ref, ROW_TILE, tm, stride=TOP_K * ROW_TILE)
    x2 = x1_ref[...] + (y0 * w[:, 0:1] + y1 * w[:, 1:2])
    e = jnp.dot(p_ref[...].astype(BF16), wpp_ref[...], preferred_element_type=F32)
    gate = jax.nn.sigmoid(jnp.dot(_rms(x2, gple_ref[...]).astype(BF16), wpg_ref[...], preferred_element_type=F32))
    x3 = x2 + gate * e
    o_ref[...] = _rms(x3, gout_ref[...]) if final else x3


def _combine_ple(yg, prev, first, x1, rw, p2d, gple, wpg, wpp, gout, tm, layer):
    T = x1.shape[0]
    nt = T // tm
    off = first // tm
    if prev is None:
        prev = x1
    return pl.pallas_call(
        functools.partial(_comb_kernel, tm=tm, final=layer == DEPTH - 1),
        grid=(yg.shape[0] // (tm * TOP_K * ROW_TILE),),
        in_specs=[
            pl.BlockSpec((tm * TOP_K * ROW_TILE, 128), lambda i: (i, 0)),
            pl.BlockSpec((tm, D_MODEL), lambda i: (off + i, 0)),
            pl.BlockSpec((tm, 128), lambda i: (off + i, 0)),
            pl.BlockSpec((tm, PLE_DIM), lambda i: (layer * nt + off + i, 0)),
            _layer_spec((1, D_MODEL), layer),
            _layer_spec((D_MODEL, D_MODEL), layer),
            _layer_spec((PLE_DIM, D_MODEL), layer),
            pl.BlockSpec((1, D_MODEL), lambda i: (0, 0)),
            pl.BlockSpec(memory_space=pl.ANY),
        ],
        out_specs=pl.BlockSpec((tm, D_MODEL), lambda i: (off + i, 0)),
        out_shape=jax.ShapeDtypeStruct((T, D_MODEL), F32),
        input_output_aliases={} if first == 0 else {8: 0},
        compiler_params=_cparams(("parallel",)),
        name="combine_ple",
    )(yg, x1, rw, p2d, gple, wpg, wpp, gout, prev)


def _rope_tables(pos, theta, rot_dim, offset, span, scale):
    half = rot_dim // 2
    inv = jnp.float32(theta) ** (-jnp.arange(half, dtype=F32) * 2.0 / rot_dim)
    ang = pos.astype(F32)[:, None] * inv
    rel = np.arange(128) - offset
    rot = (rel >= 0) & (rel < span)
    spread = np.zeros((half, 128), np.float32)
    spread[rel[rot] % half, np.nonzero(rot)[0]] = 1.0
    to_lanes = lambda t: jnp.dot(t, spread, precision=lax.Precision.HIGHEST)
    cos, sin = to_lanes(jnp.cos(ang)) + (~rot).astype(np.float32), to_lanes(jnp.sin(ang))
    return jnp.concatenate([cos, -sin, sin], axis=1) * jnp.float32(scale)


def _prep_w_in(w):
    c_q = w[..., 0:512]
    c_kv = w[..., 512:768]
    k_pe = w[..., 768:800]
    dil = w[..., 800:800 + 2304].reshape(w.shape[:-1] + (DIL_GROUPS, 3, DIL_HEADS, DIL_HEAD_DIM))

    def qk_layout(t):
        flat = lambda u: u.reshape(u.shape[:-2] + (-1,))
        return jnp.concatenate([flat(t[..., :DIL_HALF]), flat(t[..., PARTIAL_ROT:PARTIAL_ROT + DIL_REST]),
                                flat(t[..., DIL_HALF:PARTIAL_ROT]), flat(t[..., PARTIAL_ROT + DIL_REST:])], axis=-1)

    dil = jnp.concatenate([jnp.concatenate([qk_layout(dil[..., g, 0, :, :]), qk_layout(dil[..., g, 1, :, :]),
                                            dil[..., g, 2, :, :].reshape(w.shape[:-1] + (DIL_W,))], axis=-1)
                           for g in range(DIL_GROUPS)], axis=-1)
    gates = w[..., 3104:5152]
    zeros = lambda n: jnp.zeros(w.shape[:-1] + (n,), w.dtype)
    lo, hi = k_pe[..., :MLA_HALF], k_pe[..., MLA_HALF:]
    pad = MLA_SLOT - MLA_NOPE - MLA_ROPE
    kslot = jnp.concatenate([zeros(MLA_NOPE), lo, lo, zeros(pad), zeros(MLA_NOPE), hi, hi, zeros(pad)], axis=-1)
    return jnp.concatenate([gates, c_q, c_kv, kslot, dil], axis=-1).astype(BF16)


def _pad_heads(w, width):
    w = jnp.pad(w, ((0, 0), (0, 0), (0, 0), (0, MLA_SLOT - width)))
    return w.reshape(w.shape[0], w.shape[1], MLA_HEADS * MLA_SLOT).astype(BF16)


def _prep_w_q(w):
    w = w.reshape(DEPTH, MLA_Q_LORA, MLA_HEADS // 2, 2, MLA_NOPE + MLA_ROPE)
    nope = w[..., :MLA_NOPE]
    lo = w[..., MLA_NOPE:MLA_NOPE + MLA_HALF]
    hi = w[..., MLA_NOPE + MLA_HALF:]
    both = lambda t: t.reshape(t.shape[:-2] + (2 * MLA_HALF,))
    zeros = jnp.zeros(w.shape[:3] + (MLA_SLOT - MLA_NOPE - MLA_ROPE,), w.dtype)
    slot0 = jnp.concatenate([nope[..., 0, :], both(lo), zeros], axis=-1)
    slot1 = jnp.concatenate([nope[..., 1, :], both(hi), zeros], axis=-1)
    return jnp.stack([slot0, slot1], axis=-2).reshape(DEPTH, MLA_Q_LORA, MLA_HEADS * MLA_SLOT).astype(BF16)


def _prep_w_kv(w):
    w = w.reshape(DEPTH, MLA_KV_LORA, MLA_HEADS, MLA_NOPE + MLA_V)
    return _pad_heads(w[..., :MLA_NOPE], MLA_NOPE), _pad_heads(w[..., MLA_NOPE:], MLA_V)


def _dest_kernel(ri_ref, ps_ref, o_ref):
    ri = ri_ref[...].astype(F32)
    lane = lax.broadcasted_iota(jnp.int32, ri.shape, 1)
    ps = ps_ref[...]

    def col(k):
        return jnp.sum(jnp.where(lane == k, ri, 0.0), axis=-1, keepdims=True)

    def dest(k):
        start = jnp.sum(jnp.where(lane == col(k).astype(jnp.int32), ps, 0.0), axis=-1, keepdims=True)
        return (start + col(2 + k)).astype(jnp.int32)

    o_ref[...] = jnp.where(lane == 0, dest(0), jnp.where(lane == 1, dest(1), 0))


def _dest_rows(ri, pstarts, tm):
    T = ri.shape[0]
    ps = jnp.zeros((1, 128), F32).at[0, :N_EXPERTS].set(pstarts.astype(F32))
    return pl.pallas_call(
        _dest_kernel,
        grid=(T // tm,),
        in_specs=[pl.BlockSpec((tm, 128), lambda i: (i, 0)), pl.BlockSpec((1, 128), lambda i: (0, 0))],
        out_specs=pl.BlockSpec((tm, 128), lambda i: (i, 0)),
        out_shape=jax.ShapeDtypeStruct((T, 128), jnp.int32),
        compiler_params=_cparams(("parallel",)),
        name="dest_rows",
    )(ri, ps)


def _dispatch_plan(ri, cnt, T, tm):
    counts = cnt[0, :N_EXPERTS].astype(jnp.int32)
    pcounts = (counts + FFN_BM - 1) // FFN_BM * FFN_BM
    pends = jnp.cumsum(pcounts)
    pstarts = pends - pcounts
    dest = _dest_rows(ri, pstarts, tm)[:, 0:TOP_K]
    n_blocks = (T * TOP_K) // FFN_BM + N_EXPERTS
    first_row = jnp.arange(n_blocks, dtype=jnp.int32) * FFN_BM
    block_e = jnp.minimum(jnp.sum((pends[None, :] <= first_row[:, None]).astype(jnp.int32), axis=1), N_EXPERTS - 1)
    nvb = pends[-1] // FFN_BM
    after = pends // FFN_BM
    run_of_expert = jnp.cumsum((pcounts > 0).astype(jnp.int32)) - 1
    parity = (run_of_expert[block_e] % 2).astype(jnp.int32)
    per_range = n_blocks // FFN_RANGES
    ffn_plans = []
    for r in range(FFN_RANGES):
        lo, hi = r * per_range, (r + 1) * per_range
        end = jnp.minimum(nvb, hi)
        next_of_expert = jnp.where(after < end, block_e[jnp.minimum(after, n_blocks - 1)], -1)
        ffn_plans.append((block_e[lo:hi], next_of_expert[block_e[lo:hi]].astype(jnp.int32), parity[lo:hi],
                          jnp.clip(nvb - lo, 0, per_range).astype(jnp.int32).reshape(1)))
    return dest.reshape(T * TOP_K), n_blocks, ffn_plans


def kernel(x, p, positions, g_mix, w_in, g_q_lat, w_q_up, g_kv_lat, w_kv_up, w_branch_a, w_branch_b, w_out, g_ffn, w_router_grp, b_router_grp, w_router_exp, b_router_exp, w_exp_gate, w_exp_up, w_exp_down, g_ple, w_ple_gate, w_ple_proj, g_final):
    B, S, D = x.shape
    T = B * S
    pos = positions.reshape(T)
    dil_scale = DIL_HEAD_DIM ** -0.5 * LOG2E
    mla_scale = (MLA_NOPE + MLA_ROPE) ** -0.5 * LOG2E
    tab_dil = jnp.concatenate([_rope_tables(pos, ROPE_THETA, PARTIAL_ROT, 0, DIL_ROT_LANES, sc)
                               for sc in (dil_scale, 1.0)], axis=1)
    tab_mla = jnp.concatenate([_rope_tables(pos, MLA_ROPE_THETA, MLA_ROPE, MLA_NOPE, MLA_ROPE, sc)
                               for sc in (mla_scale, 1.0)], axis=1)

    w_in_p = _prep_w_in(w_in)
    wq = _prep_w_q(w_q_up)
    wk, wv = _prep_w_kv(w_kv_up)
    wr = jnp.concatenate([w_router_grp, w_router_exp, jnp.zeros((DEPTH, D, 128 - N_GROUPS - N_EXPERTS), F32)],
                         axis=-1).astype(BF16)
    br = jnp.concatenate([b_router_grp, b_router_exp.reshape(DEPTH, N_EXPERTS),
                          jnp.zeros((DEPTH, 128 - N_GROUPS - N_EXPERTS), F32)], axis=-1).reshape(DEPTH, 1, 128)
    wa, wb, wo = w_branch_a.astype(BF16), w_branch_b.astype(BF16), w_out.astype(BF16)
    wpg, wpp = w_ple_gate.astype(BF16), w_ple_proj.astype(BF16)
    gains = lambda g: g.reshape(DEPTH, 1, -1)

    xc = x.reshape(T, D)
    for i in range(DEPTH):
        main, d0, d1, d2 = _in_proj(xc, gains(g_mix), w_in_p, tab_dil, B, S, TM_IN, i)
        q, k, v = _mla_qkv(main, gains(g_q_lat), gains(g_kv_lat), wq, wk, wv, tab_mla, TM_QKV, i)
        o_a = _mla_attn(q, k, v, B, S, TQ)
        ogs, lss = [], []
        for (_, dil), qkv in zip(DIL_PATTERN, (d0, d1, d2)):
            og, ls = _dil_attn(qkv, B, dil, S // dil)
            ogs.append(og)
            lss.append(ls)
        x1, h2, lg = _post(xc, o_a, ogs, lss, main, wa, wb, wo, gains(g_ffn), wr, br, B, S, TM_POST, i)
        ri, rw, cnt = _route(lg, TM_ROUTE)
        dest, n_blocks, ffn_plans = _dispatch_plan(ri, cnt, T, TM_ROUTE)
        range_blocks = n_blocks // FFN_RANGES
        ys = None
        for r, plan in enumerate(ffn_plans):
            xs = _sc_gather_rows(h2.reshape(T, ROW_TILE, 128), dest, n_out=range_blocks * FFN_BM,
                                 row0=r * range_blocks * FFN_BM)
            ys = _expert_ffn(*plan, xs.reshape(-1, 128), ys, r * range_blocks, n_blocks,
                             w_exp_gate, w_exp_up, w_exp_down, i)
        xc = None
        for first in (0, T // 2):
            yg = _sc_gather_rows(ys.reshape(-1, ROW_TILE, 128), dest[first * TOP_K:(first + T // 2) * TOP_K])
            xc = _combine_ple(yg.reshape(-1, 128), xc, first, x1, rw, p.reshape(DEPTH * T, PLE_DIM), gains(g_ple),
                              wpg, wpp, g_final.reshape(1, D), TM_COMB, i)
    return xc.reshape(B, S, D)
```

```python
import functools
import math

import jax
import jax.numpy as jnp
import numpy as np
from jax import lax
from jax.experimental import pallas as pl
from jax.experimental.pallas import tpu as pltpu
from jax.experimental.pallas import tpu_sc as plsc

F32 = jnp.float32
BF16 = jnp.bfloat16

D_MODEL = 1024
DEPTH = 4
RMS_EPS = 1e-6
NEG = -1e30
LOG2E = math.log2(math.e)

MLA_HEADS = 16
MLA_Q_LORA = 512
MLA_KV_LORA = 256
MLA_NOPE = 64
MLA_ROPE = 32
MLA_V = 64
MLA_ROPE_THETA = 10000.0
MLA_SLOT = 128
MLA_HALF = MLA_ROPE // 2


def _mla_head_lanes(lane, head):
    l = lane % MLA_SLOT
    nope = jnp.logical_and(lane // MLA_SLOT == head, l < MLA_NOPE)
    rope = jnp.logical_and(l >= MLA_NOPE + head * MLA_HALF, l < MLA_NOPE + (head + 1) * MLA_HALF)
    return jnp.logical_or(nope, rope)

DIL_PATTERN = ((128, 1), (512, 4), (2048, 16))
DIL_GROUPS = 3
DIL_HEADS = 4
DIL_HEAD_DIM = 64
DIL_W = DIL_HEADS * DIL_HEAD_DIM
DIL_BLK = 128
DIL_UNROLL = 16
ROPE_THETA = 500000.0
PARTIAL_ROT = DIL_HEAD_DIM // 4
DIL_HALF = PARTIAL_ROT // 2
DIL_ROT_LANES = DIL_HEADS * DIL_HALF
DIL_REST = (DIL_HEAD_DIM - PARTIAL_ROT) // 2


def _dil_qk_head_of_lane(lane):
    l = lane % 128
    plain = l - DIL_ROT_LANES
    plain_head = sum((plain >= h * DIL_REST).astype(jnp.int32) for h in range(1, DIL_HEADS))
    return jnp.where(l < DIL_ROT_LANES, l // DIL_HALF, plain_head)

N_GROUPS = 8
EXPERTS_PER_GROUP = 8
N_EXPERTS = 64
TOP_K = 2
EXPERT_FF = 256
PLE_DIM = 256

IN_TN = 768
MAIN_COLS = 3072
IN_COLS_PAD = MAIN_COLS + DIL_GROUPS * 3 * DIL_W
N_MAIN_TILES = MAIN_COLS // IN_TN

FFN_BM = 256
FFN_RANGES = 2
ROW_TILE = D_MODEL // 256
U32 = jnp.uint32
VMEM_LIMIT = 48 * 1024 * 1024
IN_VMEM_LIMIT = 58 * 1024 * 1024
TM_IN, TM_QKV, TQ, TM_POST, TM_ROUTE, TM_COMB = 1024, 1024, 512, 512, 1024, 1024

SC_CORES = 2
SC_SUBCORES = 16
SC_LANES = 16
SC_CHUNK = 64


def _cparams(sem):
    return pltpu.CompilerParams(dimension_semantics=sem, vmem_limit_bytes=VMEM_LIMIT)


def _layer_spec(shape, layer, col=None):
    def index(*grid_idx):
        return (layer,) + (0,) * (len(shape) - 1) + ((col(*grid_idx),) if col else (0,))

    return pl.BlockSpec((None,) + tuple(shape), index)


def _rms(x, g):
    return x * lax.rsqrt(jnp.mean(x * x, axis=-1, keepdims=True) + RMS_EPS) * g


def _in_kernel(x_ref, g_ref, w_ref, tab_ref, main_ref, d0_ref, d1_ref, d2_ref, acc_ref):
    xn = _rms(x_ref[...], g_ref[...]).astype(BF16)

    def dil_tile(acc, out_ref, dil):
        chunk = lambda c: acc[:, c * 128:(c + 1) * 128]
        for base, t0 in ((0, 0), (2, 3)):
            cos, sin_n, sin_p = [tab_ref[:, (t0 + t) * 128:(t0 + t + 1) * 128] for t in range(3)]
            lo, hi = chunk(base), chunk(base + 1)
            acc_ref[base] = lo * cos + hi * sin_n
            acc_ref[base + 1] = hi * cos + lo * sin_p
        for c in range(4, 6):
            acc_ref[c] = chunk(c)
        rows = acc_ref.shape[1] // dil
        for r in range(dil):
            for c in range(6):
                out_ref[0, r, :, c * 128:(c + 1) * 128] = acc_ref[c, pl.ds(r, rows, stride=dil), :].astype(BF16)

    for j in range(IN_COLS_PAD // IN_TN):
        acc = jnp.dot(xn, w_ref[:, j * IN_TN:(j + 1) * IN_TN], preferred_element_type=F32)
        if j < N_MAIN_TILES:
            main_ref[:, j * IN_TN:(j + 1) * IN_TN] = acc.astype(BF16)
        else:
            gi = j - N_MAIN_TILES
            dil_tile(acc, (d0_ref, d1_ref, d2_ref)[gi], DIL_PATTERN[gi][1])


def _in_proj(x2d, g, w, tab, B, S, tm, layer):
    T = x2d.shape[0]
    nt = S // tm
    dil_shapes = [jax.ShapeDtypeStruct((B, d, S // d, 3 * DIL_W), BF16) for _, d in DIL_PATTERN]
    dil_specs = [pl.BlockSpec((1, d, tm // d, 3 * DIL_W), lambda i, nt=nt: (i // nt, 0, i % nt, 0))
                 for _, d in DIL_PATTERN]
    return pl.pallas_call(
        _in_kernel,
        grid=(T // tm,),
        in_specs=[
            pl.BlockSpec((tm, D_MODEL), lambda i: (i, 0)),
            _layer_spec((1, D_MODEL), layer),
            pl.BlockSpec((None, D_MODEL, IN_COLS_PAD), lambda i: (layer, 0, 0), pipeline_mode=pl.Buffered(1)),
            pl.BlockSpec((tm, 6 * 128), lambda i: (i, 0)),
        ],
        out_specs=[pl.BlockSpec((tm, MAIN_COLS), lambda i: (i, 0))] + dil_specs,
        out_shape=[jax.ShapeDtypeStruct((T, MAIN_COLS), BF16)] + dil_shapes,
        scratch_shapes=[pltpu.VMEM((IN_TN // 128, tm, 128), F32)],
        compiler_params=pltpu.CompilerParams(dimension_semantics=("parallel",), vmem_limit_bytes=IN_VMEM_LIMIT),
        name="in_proj",
    )(x2d, g, w, tab)


def _qkv_kernel(cq_ref, ckv_ref, kpe_ref, gq_ref, gkv_ref, wq_ref, wk_ref, wv_ref, tab_ref,
                q_ref, k_ref, v_ref):
    qn = _rms(cq_ref[...].astype(F32), gq_ref[...]).astype(BF16)
    kvn = _rms(ckv_ref[...].astype(F32), gkv_ref[...]).astype(BF16)
    qacc = jnp.dot(qn, wq_ref[...], preferred_element_type=F32)
    kacc = jnp.dot(kvn, wk_ref[...], preferred_element_type=F32)
    vacc = jnp.dot(kvn, wv_ref[...], preferred_element_type=F32)
    lane_v = lax.broadcasted_iota(jnp.int32, vacc.shape, 1)
    v_ref[...] = jnp.where(lane_v % MLA_SLOT < MLA_V, vacc, 1.0).astype(BF16)
    cq, snq, spq, ck, snk, spk = [tab_ref[:, t * 128:(t + 1) * 128] for t in range(6)]
    kpe_lo = kpe_ref[:, 0:128].astype(F32)
    kpe_hi = kpe_ref[:, 128:256].astype(F32)
    krot = (kpe_lo * ck + kpe_hi * snk, kpe_hi * ck + kpe_lo * spk)
    for pair in range(MLA_HEADS // 2):
        s0 = slice(2 * pair * MLA_SLOT, (2 * pair + 1) * MLA_SLOT)
        s1 = slice((2 * pair + 1) * MLA_SLOT, (2 * pair + 2) * MLA_SLOT)
        q0, q1 = qacc[:, s0], qacc[:, s1]
        q_ref[:, s0] = (q0 * cq + q1 * snq).astype(BF16)
        q_ref[:, s1] = (q1 * cq + q0 * spq).astype(BF16)
        k_ref[:, s0] = (kacc[:, s0] + krot[0]).astype(BF16)
        k_ref[:, s1] = (kacc[:, s1] + krot[1]).astype(BF16)


def _mla_qkv(main, gq, gkv, wq, wk, wv, tab, tm, layer):
    T = main.shape[0]
    return pl.pallas_call(
        _qkv_kernel,
        grid=(T // tm,),
        in_specs=[
            pl.BlockSpec((tm, MLA_Q_LORA), lambda i: (i, 2048 // MLA_Q_LORA)),
            pl.BlockSpec((tm, MLA_KV_LORA), lambda i: (i, 2560 // MLA_KV_LORA)),
            pl.BlockSpec((tm, 256), lambda i: (i, 2816 // 256)),
            _layer_spec((1, MLA_Q_LORA), layer),
            _layer_spec((1, MLA_KV_LORA), layer),
            _layer_spec((MLA_Q_LORA, MLA_HEADS * MLA_SLOT), layer),
            _layer_spec((MLA_KV_LORA, MLA_HEADS * MLA_SLOT), layer),
            _layer_spec((MLA_KV_LORA, MLA_HEADS * MLA_SLOT), layer),
            pl.BlockSpec((tm, 6 * 128), lambda i: (i, 0)),
        ],
        out_specs=[
            pl.BlockSpec((tm, MLA_HEADS * MLA_SLOT), lambda i: (i, 0)),
            pl.BlockSpec((tm, MLA_HEADS * MLA_SLOT), lambda i: (i, 0)),
            pl.BlockSpec((tm, MLA_HEADS * MLA_SLOT), lambda i: (i, 0)),
        ],
        out_shape=[
            jax.ShapeDtypeStruct((T, MLA_HEADS * MLA_SLOT), BF16),
            jax.ShapeDtypeStruct((T, MLA_HEADS * MLA_SLOT), BF16),
            jax.ShapeDtypeStruct((T, MLA_HEADS * MLA_SLOT), BF16),
        ],
        compiler_params=_cparams(("parallel",)),
        name="mla_qkv",
    )(main, main, main, gq, gkv, wq, wk, wv, tab)


def _mla_attn_kernel(qa_ref, qb_ref, k_ref, v_ref, o_ref, m_ref, acc_ref, *, tq):
    p_id = pl.program_id(2)
    nq = k_ref.shape[1] // tq
    lane = lax.broadcasted_iota(jnp.int32, (tq, 2 * MLA_SLOT), 1)
    qh = []
    for q_ref in (qa_ref, qb_ref):
        qp = q_ref[0]
        zero = jnp.zeros_like(qp)
        qh.append([jnp.where(_mla_head_lanes(lane, h), qp, zero) for h in range(2)])
    hq = tq // 2
    dn = (((1,), (1,)), ((), ()))

    def step(blk, start, width, r0, mask, first=False):
        kb = k_ref[0, pl.ds(start, width), :]
        vb = v_ref[0, pl.ds(start, width), :]
        rows = tq - r0
        s_pair = lax.dot_general(jnp.concatenate([qh[blk][0][r0:], qh[blk][1][r0:]], axis=0), kb, dn,
                                 preferred_element_type=F32)
        for h in range(2):
            s = s_pair[h * rows:(h + 1) * rows]
            if mask is not None:
                s = jnp.where(mask, s, NEG)
            m_cur = jnp.max(s, axis=-1, keepdims=True)
            if first:
                m_new = jnp.broadcast_to(m_cur, (tq - r0, 128))
            else:
                m_prev = m_ref[blk, h, r0:, :]
                m_new = jnp.maximum(m_prev, m_cur)
            p = jnp.exp2(s - jnp.concatenate([m_new] * (width // 128), axis=-1))
            pv = jnp.dot(p.astype(BF16), vb[:, h * MLA_SLOT:(h + 1) * MLA_SLOT], preferred_element_type=F32)
            if first:
                acc_ref[blk, h, r0:, :] = pv
            else:
                acc_ref[blk, h, r0:, :] = jnp.exp2(m_prev - m_new) * acc_ref[blk, h, r0:, :] + pv
            m_ref[blk, h, r0:, :] = m_new

    def causal(rows):
        return lax.broadcasted_iota(jnp.int32, (rows, hq), 1) <= lax.broadcasted_iota(jnp.int32, (rows, hq), 0)

    def q_blocks(n_a):
        n_b = nq - 1 - n_a
        for blk, n_full in ((1, n_b), (0, n_a)):
            j = 0
            while j < n_full:
                wide = 2 if j + 1 < n_full else 1
                step(blk, j * tq, wide * tq, 0, None, first=j == 0)
                j += wide
            step(blk, n_full * tq, hq, 0, causal(tq), first=n_full == 0)
            step(blk, n_full * tq + hq, hq, hq, causal(hq))

    for n_a in range(nq // 2):
        pl.when(p_id == n_a)(functools.partial(q_blocks, n_a))
    lane_o = lax.broadcasted_iota(jnp.int32, (tq, 2 * MLA_V), 1)
    for blk in range(2):
        outs = []
        for h in range(2):
            a = acc_ref[blk, h]
            outs.append(a / pltpu.roll(a, MLA_V, 1))
        o_ref[0, blk, 0] = jnp.where(lane_o < MLA_V, outs[0], pltpu.roll(outs[1], MLA_V, 1)).astype(BF16)


def _mla_attn(q, k, v, B, S, tq):
    nq = S // tq
    q = q.reshape(B, S, MLA_HEADS * MLA_SLOT)
    k = k.reshape(B, S, MLA_HEADS * MLA_SLOT)
    v = v.reshape(B, S, MLA_HEADS * MLA_SLOT)
    return pl.pallas_call(
        functools.partial(_mla_attn_kernel, tq=tq),
        grid=(B, MLA_HEADS // 2, nq // 2),
        in_specs=[
            pl.BlockSpec((1, tq, 2 * MLA_SLOT), lambda b, h, p: (b, p, h)),
            pl.BlockSpec((1, tq, 2 * MLA_SLOT), lambda b, h, p: (b, nq - 1 - p, h)),
            pl.BlockSpec((1, S, 2 * MLA_SLOT), lambda b, h, p: (b, 0, h)),
            pl.BlockSpec((1, S, 2 * MLA_SLOT), lambda b, h, p: (b, 0, h)),
        ],
        out_specs=pl.BlockSpec((1, 2, 1, tq, 2 * MLA_V), lambda b, h, p: (b, 0, p, 0, h)),
        out_shape=jax.ShapeDtypeStruct((B, 2, nq // 2, tq, MLA_HEADS * MLA_V), BF16),
        scratch_shapes=[pltpu.VMEM((2, 2, tq, 128), F32), pltpu.VMEM((2, 2, tq, MLA_SLOT), F32)],
        compiler_params=_cparams(("parallel", "parallel", "arbitrary")),
        name="mla_attn",
    )(q, q, k, v)


def _dil_kernel(qkv_ref, o_ref, lse_ref, bias_ref, *, nb):
    dil = qkv_ref.shape[1]
    width = 2 * DIL_BLK if nb > 1 else DIL_BLK
    row = lax.broadcasted_iota(jnp.int32, (DIL_BLK, width), 0)
    col = lax.broadcasted_iota(jnp.int32, (DIL_BLK, width), 1)
    bias_ref[0] = jnp.where(col <= row, 0.0, NEG)
    if nb > 1:
        later = jnp.logical_or(jnp.logical_and(col >= DIL_BLK, col - DIL_BLK <= row),
                               jnp.logical_and(col < DIL_BLK, col >= row))
        bias_ref[1] = jnp.where(later, 0.0, NEG)
    lane = lax.broadcasted_iota(jnp.int32, (DIL_BLK, DIL_W), 1)
    head_of_lane = lane // DIL_HEAD_DIM
    qk_head_of_lane = _dil_qk_head_of_lane(lane)
    dn = (((1,), (1,)), ((), ()))

    def by_head(parts):
        out = parts[DIL_HEADS - 1]
        for h in range(DIL_HEADS - 2, -1, -1):
            out = jnp.where(head_of_lane == h, parts[h], out)
        return out

    def unit(u):
        r = u // nb
        n = u % nb
        q0 = pl.multiple_of(n * DIL_BLK, DIL_BLK)
        q = qkv_ref[0, r, pl.ds(q0, DIL_BLK), 0:DIL_W]
        if nb > 1:
            w0 = pl.multiple_of(jnp.maximum(n - 1, 0) * DIL_BLK, DIL_BLK)
            bias = bias_ref[jnp.minimum(n, 1)]
        else:
            w0 = 0
            bias = bias_ref[0]
        kw = qkv_ref[0, r, pl.ds(w0, width), DIL_W:2 * DIL_W]
        vw = qkv_ref[0, r, pl.ds(w0, width), 2 * DIL_W:3 * DIL_W]
        zero = jnp.zeros_like(q)
        qs = jnp.concatenate([jnp.where(qk_head_of_lane == h, q, zero) for h in range(DIL_HEADS)], axis=0)
        s = lax.dot_general(qs, kw, dn, preferred_element_type=F32)
        s = (s.reshape(DIL_HEADS, DIL_BLK, width) + bias[None]).reshape(DIL_HEADS * DIL_BLK, width)
        m = jnp.max(s, axis=-1, keepdims=True)
        e = jnp.exp2(s - m)
        den = jnp.sum(e, axis=-1, keepdims=True)
        pv = jnp.dot(e.astype(BF16), vw, preferred_element_type=F32)
        lse = m + jnp.log2(den)
        blk = lambda t, h: t[h * DIL_BLK:(h + 1) * DIL_BLK]
        o = by_head([blk(pv, h) for h in range(DIL_HEADS)]) / by_head(
            [jnp.broadcast_to(blk(den, h), (DIL_BLK, DIL_W)) for h in range(DIL_HEADS)])
        o_ref[0, r, pl.ds(q0, DIL_BLK), :] = o.astype(BF16)
        lse_ref[0, r, pl.ds(q0, DIL_BLK), :] = by_head(
            [jnp.broadcast_to(blk(lse, h), (DIL_BLK, DIL_W)) for h in range(DIL_HEADS)])

    def body(t, carry):
        for u in range(DIL_UNROLL):
            unit(DIL_UNROLL * t + u)
        return carry

    lax.fori_loop(0, dil * nb // DIL_UNROLL, body, 0)


def _dil_attn(qkv, B, dil, L):
    nb = L // DIL_BLK
    return pl.pallas_call(
        functools.partial(_dil_kernel, nb=nb),
        grid=(B,),
        in_specs=[pl.BlockSpec((1, dil, L, 3 * DIL_W), lambda b: (b, 0, 0, 0))],
        out_specs=[
            pl.BlockSpec((1, dil, L, DIL_W), lambda b: (b, 0, 0, 0)),
            pl.BlockSpec((1, dil, L, DIL_W), lambda b: (b, 0, 0, 0)),
        ],
        out_shape=[
            jax.ShapeDtypeStruct((B, dil, L, DIL_W), BF16),
            jax.ShapeDtypeStruct((B, dil, L, DIL_W), F32),
        ],
        scratch_shapes=[pltpu.VMEM((2, DIL_BLK, 2 * DIL_BLK if nb > 1 else DIL_BLK), F32)],
        compiler_params=_cparams(("parallel",)),
        name=f"dil_attn_d{dil}",
    )(qkv)


def _post_kernel(x_ref, oa_ref, og0_ref, og1_ref, og2_ref, ls0_ref, ls1_ref, ls2_ref, gates_ref,
                 wa_ref, wb_ref, wo_ref, gffn_ref, wr_ref, br_ref,
                 x1_ref, h2_ref, lg_ref, ob_ref, *, tm):
    ls =[r[0] for r in (ls0_ref, ls1_ref, ls2_ref)]
    og = [r[0] for r in (og0_ref, og1_ref, og2_ref)]
    for gi, (_, dil) in enumerate(DIL_PATTERN):
        rows = tm // dil
        for r in range(dil):
            for c in range(2):
                sl = slice(c * 128, (c + 1) * 128)
                ob_ref[2 * gi + c, pl.ds(r, rows, stride=dil), :] = og[gi][r][:, sl].astype(F32)
                ob_ref[6 + 2 * gi + c, pl.ds(r, rows, stride=dil), :] = ls[gi][r][:, sl]

    def tok_major(k):
        return jnp.concatenate([ob_ref[2 * k], ob_ref[2 * k + 1]], axis=-1)

    l0, l1, l2 = tok_major(3), tok_major(4), tok_major(5)
    mx = jnp.maximum(jnp.maximum(l0, l1), l2)
    w0, w1, w2 = jnp.exp2(l0 - mx), jnp.exp2(l1 - mx), jnp.exp2(l2 - mx)
    ob = (w0 * tok_major(0) + w1 * tok_major(1) + w2 * tok_major(2)) / (w0 + w1 + w2)
    ya = jnp.dot(oa_ref[0, 0, 0], wa_ref[...], preferred_element_type=F32)
    yb = jnp.dot(ob.astype(BF16), wb_ref[...], preferred_element_type=F32)
    merged = (jax.nn.sigmoid(gates_ref[:, 0:D_MODEL].astype(F32)) * ya
              + jax.nn.sigmoid(gates_ref[:, D_MODEL:2 * D_MODEL].astype(F32)) * yb)
    x1 = x_ref[...] + jnp.dot(merged.astype(BF16), wo_ref[...], preferred_element_type=F32)
    x1_ref[...] = x1
    h2 = _rms(x1, gffn_ref[...])
    _rows_to_tiles(h2_ref, h2)
    lg_ref[...] = jnp.dot(h2.astype(BF16), wr_ref[...], preferred_element_type=F32) + br_ref[...]


def _post(x2d, oa, ogs, lss, main, wa, wb, wo, gffn, wr, br, B, S, tm, layer):
    T = x2d.shape[0]
    nt = S // tm
    res_specs = [pl.BlockSpec((1, d, tm // d, DIL_W), lambda i, nt=nt: (i // nt, 0, i % nt, 0))
                 for _, d in DIL_PATTERN]
    assert oa.shape[3] == tm and oa.shape[2] * 2 == nt

    def oa_index(i):
        it = i % nt
        late = it >= nt // 2
        return (i // nt, late.astype(jnp.int32), jnp.where(late, nt - 1 - it, it), 0, 0)

    return pl.pallas_call(
        functools.partial(_post_kernel, tm=tm),
        grid=(T // tm,),
        in_specs=[pl.BlockSpec((tm, D_MODEL), lambda i: (i, 0)),
                  pl.BlockSpec((1, 1, 1, tm, D_MODEL), oa_index)]
                 + res_specs + res_specs
                 + [pl.BlockSpec((tm, 2 * D_MODEL), lambda i: (i, 0)),
                    _layer_spec((D_MODEL, D_MODEL), layer),
                    _layer_spec((DIL_W, D_MODEL), layer),
                    _layer_spec((D_MODEL, D_MODEL), layer),
                    _layer_spec((1, D_MODEL), layer),
                    _layer_spec((D_MODEL, 128), layer),
                    _layer_spec((1, 128), layer)],
        out_specs=[pl.BlockSpec((tm, D_MODEL), lambda i: (i, 0)),
                   pl.BlockSpec((tm * ROW_TILE, 128), lambda i: (i, 0)),
                   pl.BlockSpec((tm, 128), lambda i: (i, 0))],
        out_shape=[jax.ShapeDtypeStruct((T, D_MODEL), F32),
                   jax.ShapeDtypeStruct((T * ROW_TILE, 128), U32),
                   jax.ShapeDtypeStruct((T, 128), F32)],
        scratch_shapes=[pltpu.VMEM((12, tm, 128), F32)],
        compiler_params=_cparams(("parallel",)),
        name="post_attn",
    )(x2d, oa, *ogs, *lss, main, wa, wb, wo, gffn, wr, br)


def _route_kernel(lg_ref, ri_ref, rw_ref, cnt_ref, carry_ref, lower_ref, *, tm):
    i = pl.program_id(0)

    @pl.when(i == 0)
    def _():
        carry_ref[...] = jnp.zeros(carry_ref.shape, F32)
        r_i = lax.broadcasted_iota(jnp.int32, (tm, tm), 0)
        c_i = lax.broadcasted_iota(jnp.int32, (tm, tm), 1)
        lower_ref[...] = jnp.where(c_i < r_i, 1.0, 0.0).astype(BF16)

    lg = lg_ref[...]
    lane = lax.broadcasted_iota(jnp.int32, lg.shape, 1)
    lane_f = lane.astype(F32)
    ninf = jnp.float32(-jnp.inf)

    def first_max(vals):
        vmax = jnp.max(vals, axis=-1, keepdims=True)
        idx = jnp.min(jnp.where(vals == vmax, lane_f, 128.0), axis=-1, keepdims=True)
        return vmax, idx.astype(jnp.int32)

    gl = jnp.where(lane < N_GROUPS, lg, ninf)
    gmax, g_sel = first_max(gl)
    p_g = 1.0 / jnp.sum(jnp.exp(gl - gmax), axis=-1, keepdims=True)
    lo = N_GROUPS + g_sel * EXPERTS_PER_GROUP
    el = jnp.where(jnp.logical_and(lane >= lo, lane < lo + EXPERTS_PER_GROUP), lg, ninf)
    v0, i0 = first_max(el)
    v1, i1 = first_max(jnp.where(lane == i0, ninf, el))
    t = jnp.exp(v1 - v0)
    w0 = p_g / (1.0 + t)
    w1 = p_g * t / (1.0 + t)
    e0 = i0 - N_GROUPS
    e1 = i1 - N_GROUPS
    hit0 = lane == e0
    hit1 = lane == e1
    oh = jnp.where(jnp.logical_or(hit0, hit1), 1.0, 0.0).astype(F32)
    excl = jnp.dot(lower_ref[...], oh.astype(BF16), preferred_element_type=F32) + carry_ref[...]
    r0 = jnp.sum(jnp.where(hit0, excl, 0.0), axis=-1, keepdims=True).astype(jnp.int32)
    r1 = jnp.sum(jnp.where(hit1, excl, 0.0), axis=-1, keepdims=True).astype(jnp.int32)
    carry_ref[...] = carry_ref[...] + jnp.sum(oh, axis=0, keepdims=True)
    zi = jnp.zeros(lg.shape, jnp.int32)
    ri_ref[...] = jnp.where(lane == 0, e0, jnp.where(lane == 1, e1, jnp.where(lane == 2, r0, jnp.where(lane == 3, r1, zi))))
    rw_ref[...] = jnp.where(lane == 0, w0, jnp.where(lane == 1, w1, jnp.zeros(lg.shape, F32)))
    cnt_ref[...] = carry_ref[...]


def _route(lg, tm):
    T = lg.shape[0]
    return pl.pallas_call(
        functools.partial(_route_kernel, tm=tm),
        grid=(T // tm,),
        in_specs=[pl.BlockSpec((tm, 128), lambda i: (i, 0))],
        out_specs=[pl.BlockSpec((tm, 128), lambda i: (i, 0)),
                   pl.BlockSpec((tm, 128), lambda i: (i, 0)),
                   pl.BlockSpec((1, 128), lambda i: (0, 0))],
        out_shape=[jax.ShapeDtypeStruct((T, 128), jnp.int32),
                   jax.ShapeDtypeStruct((T, 128), F32),
                   jax.ShapeDtypeStruct((1, 128), F32)],
        scratch_shapes=[pltpu.VMEM((1, 128), F32), pltpu.VMEM((tm, tm), BF16)],
        compiler_params=_cparams(("arbitrary",)),
        name="route",
    )(lg)


def _sc_gather_rows(table, idx, n_out=None, row0=0):
    inverse = n_out is not None
    n_src = idx.shape[0]
    n = n_out if inverse else n_src
    n_workers = SC_CORES * SC_SUBCORES
    per_w = n // n_workers
    n_chunks = per_w // SC_CHUNK
    assert per_w * n_workers == n and n_chunks * SC_CHUNK == per_w and n_chunks % 2 == 0
    mesh = plsc.VectorSubcoreMesh(core_axis_name="c", subcore_axis_name="s",
                                  num_cores=SC_CORES, num_subcores=SC_SUBCORES)

    @functools.partial(
        pl.kernel, mesh=mesh,
        out_type=jax.ShapeDtypeStruct((n,) + table.shape[1:], table.dtype),
        scratch_types=[pltpu.VMEM((per_w,), jnp.int32),
                       pltpu.VMEM((SC_CHUNK,) + table.shape[1:], table.dtype),
                       pltpu.VMEM((SC_CHUNK,) + table.shape[1:], table.dtype),
                       pltpu.SemaphoreType.DMA, pltpu.SemaphoreType.DMA,
                       pltpu.VMEM((n_src if inverse else SC_LANES,), jnp.int32)],
        compiler_params=pltpu.CompilerParams(use_tc_tiling_on_sc=True, needs_layout_passes=not inverse),
        name="sc_dispatch_rows" if inverse else "sc_gather_rows",
    )
    def gather(table_hbm, idx_hbm, out_hbm, idx_v, rows_a, rows_b, sem_a, sem_b, map_v):
        wid = lax.axis_index("s") * SC_CORES + lax.axis_index("c")
        base = wid * per_w
        if inverse:
            pltpu.sync_copy(idx_hbm, map_v)
            lanes = lax.iota(jnp.int32, SC_LANES)

            @pl.loop(0, per_w // SC_LANES)
            def _(j):
                idx_v[pl.ds(j * SC_LANES, SC_LANES)] = lax.rem(row0 + base + j * SC_LANES + lanes, table.shape[0])

            @pl.loop(0, n_src // SC_LANES)
            def _(a):
                local = map_v[pl.ds(a * SC_LANES, SC_LANES)] - (row0 + base)
                mine = jnp.logical_and(local >= 0, local < per_w)
                plsc.store_scatter(idx_v, [local], lax.div(a * SC_LANES + lanes, TOP_K), mask=mine)
        else:
            pltpu.sync_copy(idx_hbm.at[pl.ds(base, per_w)], idx_v)

        def fetch(chunk, rows_v, sem):
            return pltpu.make_async_copy(table_hbm.at[idx_v.at[pl.ds(chunk * SC_CHUNK, SC_CHUNK)]], rows_v, sem)

        def flush(chunk, rows_v):
            pltpu.sync_copy(rows_v, out_hbm.at[pl.ds(base + chunk * SC_CHUNK, SC_CHUNK)])

        fetch(0, rows_a, sem_a).start()

        @pl.loop(0, n_chunks, step=2)
        def _(c):
            fetch(c + 1, rows_b, sem_b).start()
            fetch(c, rows_a, sem_a).wait()
            flush(c, rows_a)

            @pl.when(c + 2 < n_chunks)
            def _():
                fetch(c + 2, rows_a, sem_a).start()

            fetch(c + 1, rows_b, sem_b).wait()
            flush(c + 1, rows_b)

    return gather(table, idx)


def _rows_from_tiles(ref, first, n_rows, stride=ROW_TILE):
    words = [ref[pl.ds(first + c, n_rows, stride=stride), :] for c in range(ROW_TILE)]
    lo = [pltpu.bitcast(w << 16, F32) for w in words]
    hi = [pltpu.bitcast(w & U32(0xFFFF0000), F32) for w in words]
    return jnp.concatenate(lo + hi, axis=-1)


def _rows_to_tiles(ref, val):
    half = D_MODEL // 2

    def bits(x):
        return pltpu.bitcast(x.astype(BF16).astype(F32), U32)

    for c in range(ROW_TILE):
        lo = bits(val[:, c * 128:(c + 1) * 128])
        hi = bits(val[:, half + c * 128:half + (c + 1) * 128])
        ref[pl.ds(c, val.shape[0], stride=ROW_TILE), :] = hi | (lo >> 16)


def _ffn_kernel(be_ref, nx_ref, par_ref, nvb_ref, x_ref, w1_hbm, w3_hbm, w2_hbm, prev_ref, y_ref,
                w1f_ref, w3f_ref, w2f_ref, w1b_ref, w3b_ref, w2b_ref, sem, *, layer):
    b = pl.program_id(0)
    nvb = nvb_ref[0]
    expert = be_ref[b]
    slot = par_ref[b]
    new_expert = jnp.logical_or(b == 0, expert != be_ref[jnp.maximum(b - 1, 0)])

    def fetch(e, sl):
        return [pltpu.make_async_copy(w_hbm.at[layer, e], wf_ref.at[sl], sem.at[sl])
                for w_hbm, wf_ref in ((w1_hbm, w1f_ref), (w3_hbm, w3f_ref), (w2_hbm, w2f_ref))]

    @pl.when(jnp.logical_and(b == 0, nvb > 0))
    def _():
        for cp in fetch(expert, slot):
            cp.start()

    @pl.when(jnp.logical_and(b < nvb, new_expert))
    def _():
        for cp in fetch(expert, slot):
            cp.wait()

        @pl.when(nx_ref[b] >= 0)
        def _():
            for cp in fetch(nx_ref[b], 1 - slot):
                cp.start()

        w1b_ref[...] = w1f_ref[slot].astype(BF16)
        w3b_ref[...] = w3f_ref[slot].astype(BF16)
        w2b_ref[...] = w2f_ref[slot].astype(BF16)

    @pl.when(b < nvb)
    def _():
        xb = _rows_from_tiles(x_ref, 0, FFN_BM).astype(BF16)
        h1 = jnp.dot(xb, w1b_ref[...], preferred_element_type=F32)
        h3 = jnp.dot(xb, w3b_ref[...], preferred_element_type=F32)
        a = (jax.nn.silu(h1) * h3).astype(BF16)
        _rows_to_tiles(y_ref, jnp.dot(a, w2b_ref[...], preferred_element_type=F32))

    @pl.when(b >= nvb)
    def _():
        y_ref[...] = jnp.zeros(y_ref.shape, U32)


def _expert_ffn(block_e, next_e, parity, nvb, xs, prev, first, n_blocks, w1, w3, w2, layer):
    nb = block_e.shape[0]
    if prev is None:
        prev = xs

    def used(b, *prefetch):
        return jnp.minimum(b, jnp.maximum(prefetch[-1][0] - 1, 0))

    return pl.pallas_call(
        functools.partial(_ffn_kernel, layer=layer),
        grid_spec=pltpu.PrefetchScalarGridSpec(
            num_scalar_prefetch=4,
            grid=(nb,),
            in_specs=[
                pl.BlockSpec((FFN_BM * ROW_TILE, 128), lambda b, *prefetch: (used(b, *prefetch), 0)),
                pl.BlockSpec(memory_space=pl.ANY),
                pl.BlockSpec(memory_space=pl.ANY),
                pl.BlockSpec(memory_space=pl.ANY),
                pl.BlockSpec(memory_space=pl.ANY),
            ],
            out_specs=pl.BlockSpec((FFN_BM * ROW_TILE, 128), lambda b, *prefetch: (first + b, 0)),
            scratch_shapes=[pltpu.VMEM((2, D_MODEL, EXPERT_FF), F32), pltpu.VMEM((2, D_MODEL, EXPERT_FF), F32),
                            pltpu.VMEM((2, EXPERT_FF, D_MODEL), F32),
                            pltpu.VMEM((D_MODEL, EXPERT_FF), BF16), pltpu.VMEM((D_MODEL, EXPERT_FF), BF16),
                            pltpu.VMEM((EXPERT_FF, D_MODEL), BF16),
                            pltpu.SemaphoreType.DMA((2,))],
        ),
        out_shape=jax.ShapeDtypeStruct((n_blocks * FFN_BM * ROW_TILE, 128), U32),
        input_output_aliases={} if first == 0 else {8: 0},
        compiler_params=_cparams(("arbitrary",)),
        name="expert_ffn",
    )(block_e, next_e, parity, nvb, xs, w1, w3, w2, prev)


def _comb_kernel(yg_ref, x1_ref, rw_ref, p_ref, gple_ref, wpg_ref, wpp_ref, gout_ref, prev_ref, o_ref, *, tm, final):
    w = rw_ref[...]
    y0 = _rows_from_tiles(yg_ref, 0, tm, stride=TOP_K * ROW_TILE)
    y1 = _rows_from_tiles(yg_ref, ROW_TILE, tm, stride=TOP_K * ROW_TILE)
    x2 = x1_ref[...] + (y0 * w[:, 0:1] + y1 * w[:, 1:2])
    e = jnp.dot(p_ref[...].astype(BF16), wpp_ref[...], preferred_element_type=F32)
    gate = jax.nn.sigmoid(jnp.dot(_rms(x2, gple_ref[...]).astype(BF16), wpg_ref[...], preferred_element_type=F32))
    x3 = x2 + gate * e
    o_ref[...] = _rms(x3, gout_ref[...]) if final else x3


def _combine_ple(yg, prev, first, x1, rw, p2d, gple, wpg, wpp, gout, tm, layer):
    T = x1.shape[0]
    nt = T // tm
    off = first // tm
    if prev is None:
        prev = x1
    return pl.pallas_call(
        functools.partial(_comb_kernel, tm=tm, final=layer == DEPTH - 1),
        grid=(yg.shape[0] // (tm * TOP_K * ROW_TILE),),
        in_specs=[
            pl.BlockSpec((tm * TOP_K * ROW_TILE, 128), lambda i: (i, 0)),
            pl.BlockSpec((tm, D_MODEL), lambda i: (off + i, 0)),
            pl.BlockSpec((tm, 128), lambda i: (off + i, 0)),
            pl.BlockSpec((tm, PLE_DIM), lambda i: (layer * nt + off + i, 0)),
            _layer_spec((1, D_MODEL), layer),
            _layer_spec((D_MODEL, D_MODEL), layer),
            _layer_spec((PLE_DIM, D_MODEL), layer),
            pl.BlockSpec((1, D_MODEL), lambda i: (0, 0)),
            pl.BlockSpec(memory_space=pl.ANY),
        ],
        out_specs=pl.BlockSpec((tm, D_MODEL), lambda i: (off + i, 0)),
        out_shape=jax.ShapeDtypeStruct((T, D_MODEL), F32),
        input_output_aliases={} if first == 0 else {8: 0},
        compiler_params=_cparams(("parallel",)),
        name="combine_ple",
    )(yg, x1, rw, p2d, gple, wpg, wpp, gout, prev)


def _rope_tables(pos, theta, rot_dim, offset, span, scale):
    half = rot_dim // 2
    inv = jnp.float32(theta) ** (-jnp.arange(half, dtype=F32) * 2.0 / rot_dim)
    ang = pos.astype(F32)[:, None] * inv
    rel = np.arange(128) - offset
    rot = (rel >= 0) & (rel < span)
    spread = np.zeros((half, 128), np.float32)
    spread[rel[rot] % half, np.nonzero(rot)[0]] = 1.0
    to_lanes = lambda t: jnp.dot(t, spread, precision=lax.Precision.HIGHEST)
    cos, sin = to_lanes(jnp.cos(ang)) + (~rot).astype(np.float32), to_lanes(jnp.sin(ang))
    return jnp.concatenate([cos, -sin, sin], axis=1) * jnp.float32(scale)


def _prep_w_in(w):
    c_q = w[..., 0:512]
    c_kv = w[..., 512:768]
    k_pe = w[..., 768:800]
    dil = w[..., 800:800 + 2304].reshape(w.shape[:-1] + (DIL_GROUPS, 3, DIL_HEADS, DIL_HEAD_DIM))

    def qk_layout(t):
        flat = lambda u: u.reshape(u.shape[:-2] + (-1,))
        return jnp.concatenate([flat(t[..., :DIL_HALF]), flat(t[..., PARTIAL_ROT:PARTIAL_ROT + DIL_REST]),
                                flat(t[..., DIL_HALF:PARTIAL_ROT]), flat(t[..., PARTIAL_ROT + DIL_REST:])], axis=-1)

    dil = jnp.concatenate([jnp.concatenate([qk_layout(dil[..., g, 0, :, :]), qk_layout(dil[..., g, 1, :, :]),
                                            dil[..., g, 2, :, :].reshape(w.shape[:-1] + (DIL_W,))], axis=-1)
                           for g in range(DIL_GROUPS)], axis=-1)
    gates = w[..., 3104:5152]
    zeros = lambda n: jnp.zeros(w.shape[:-1] + (n,), w.dtype)
    lo, hi = k_pe[..., :MLA_HALF], k_pe[..., MLA_HALF:]
    pad = MLA_SLOT - MLA_NOPE - MLA_ROPE
    kslot = jnp.concatenate([zeros(MLA_NOPE), lo, lo, zeros(pad), zeros(MLA_NOPE), hi, hi, zeros(pad)], axis=-1)
    return jnp.concatenate([gates, c_q, c_kv, kslot, dil], axis=-1).astype(BF16)


def _pad_heads(w, width):
    w = jnp.pad(w, ((0, 0), (0, 0), (0, 0), (0, MLA_SLOT - width)))
    return w.reshape(w.shape[0], w.shape[1], MLA_HEADS * MLA_SLOT).astype(BF16)


def _prep_w_q(w):
    w = w.reshape(DEPTH, MLA_Q_LORA, MLA_HEADS // 2, 2, MLA_NOPE + MLA_ROPE)
    nope = w[..., :MLA_NOPE]
    lo = w[..., MLA_NOPE:MLA_NOPE + MLA_HALF]
    hi = w[..., MLA_NOPE + MLA_HALF:]
    both = lambda t: t.reshape(t.shape[:-2] + (2 * MLA_HALF,))
    zeros = jnp.zeros(w.shape[:3] + (MLA_SLOT - MLA_NOPE - MLA_ROPE,), w.dtype)
    slot0 = jnp.concatenate([nope[..., 0, :], both(lo), zeros], axis=-1)
    slot1 = jnp.concatenate([nope[..., 1, :], both(hi), zeros], axis=-1)
    return jnp.stack([slot0, slot1], axis=-2).reshape(DEPTH, MLA_Q_LORA, MLA_HEADS * MLA_SLOT).astype(BF16)


def _prep_w_kv(w):
    w = w.reshape(DEPTH, MLA_KV_LORA, MLA_HEADS, MLA_NOPE + MLA_V)
    return _pad_heads(w[..., :MLA_NOPE], MLA_NOPE), _pad_heads(w[..., MLA_NOPE:], MLA_V)


def _dest_kernel(ri_ref, ps_ref, o_ref):
    ri = ri_ref[...].astype(F32)
    lane = lax.broadcasted_iota(jnp.int32, ri.shape, 1)
    ps = ps_ref[...]

    def col(k):
        return jnp.sum(jnp.where(lane == k, ri, 0.0), axis=-1, keepdims=True)

    def dest(k):
        start = jnp.sum(jnp.where(lane == col(k).astype(jnp.int32), ps, 0.0), axis=-1, keepdims=True)
        return (start + col(2 + k)).astype(jnp.int32)

    o_ref[...] = jnp.where(lane == 0, dest(0), jnp.where(lane == 1, dest(1), 0))


def _dest_rows(ri, pstarts, tm):
    T = ri.shape[0]
    ps = jnp.zeros((1, 128), F32).at[0, :N_EXPERTS].set(pstarts.astype(F32))
    return pl.pallas_call(
        _dest_kernel,
        grid=(T // tm,),
        in_specs=[pl.BlockSpec((tm, 128), lambda i: (i, 0)), pl.BlockSpec((1, 128), lambda i: (0, 0))],
        out_specs=pl.BlockSpec((tm, 128), lambda i: (i, 0)),
        out_shape=jax.ShapeDtypeStruct((T, 128), jnp.int32),
        compiler_params=_cparams(("parallel",)),
        name="dest_rows",
    )(ri, ps)


def _dispatch_plan(ri, cnt, T, tm):
    counts = cnt[0, :N_EXPERTS].astype(jnp.int32)
    pcounts = (counts + FFN_BM - 1) // FFN_BM * FFN_BM
    pends = jnp.cumsum(pcounts)
    pstarts = pends - pcounts
    dest = _dest_rows(ri, pstarts, tm)[:, 0:TOP_K]
    n_blocks = (T * TOP_K) // FFN_BM + N_EXPERTS
    first_row = jnp.arange(n_blocks, dtype=jnp.int32) * FFN_BM
    block_e = jnp.minimum(jnp.sum((pends[None, :] <= first_row[:, None]).astype(jnp.int32), axis=1), N_EXPERTS - 1)
    nvb = pends[-1] // FFN_BM
    after = pends // FFN_BM
    run_of_expert = jnp.cumsum((pcounts > 0).astype(jnp.int32)) - 1
    parity = (run_of_expert[block_e] % 2).astype(jnp.int32)
    per_range = n_blocks // FFN_RANGES
    ffn_plans = []
    for r in range(FFN_RANGES):
        lo, hi = r * per_range, (r + 1) * per_range
        end = jnp.minimum(nvb, hi)
        next_of_expert = jnp.where(after < end, block_e[jnp.minimum(after, n_blocks - 1)], -1)
        ffn_plans.append((block_e[lo:hi], next_of_expert[block_e[lo:hi]].astype(jnp.int32), parity[lo:hi],
                          jnp.clip(nvb - lo, 0, per_range).astype(jnp.int32).reshape(1)))
    return dest.reshape(T * TOP_K), n_blocks, ffn_plans


def kernel(x, p, positions, g_mix, w_in, g_q_lat, w_q_up, g_kv_lat, w_kv_up, w_branch_a, w_branch_b, w_out, g_ffn, w_router_grp, b_router_grp, w_router_exp, b_router_exp, w_exp_gate, w_exp_up, w_exp_down, g_ple, w_ple_gate, w_ple_proj, g_final):
    B, S, D = x.shape
    T = B * S
    pos = positions.reshape(T)
    dil_scale = DIL_HEAD_DIM ** -0.5 * LOG2E
    mla_scale = (MLA_NOPE + MLA_ROPE) ** -0.5 * LOG2E
    tab_dil = jnp.concatenate([_rope_tables(pos, ROPE_THETA, PARTIAL_ROT, 0, DIL_ROT_LANES, sc)
                               for sc in (dil_scale, 1.0)], axis=1)
    tab_mla = jnp.concatenate([_rope_tables(pos, MLA_ROPE_THETA, MLA_ROPE, MLA_NOPE, MLA_ROPE, sc)
                               for sc in (mla_scale, 1.0)], axis=1)

    w_in_p = _prep_w_in(w_in)
    wq = _prep_w_q(w_q_up)
    wk, wv = _prep_w_kv(w_kv_up)
    wr = jnp.concatenate([w_router_grp, w_router_exp, jnp.zeros((DEPTH, D, 128 - N_GROUPS - N_EXPERTS), F32)],
                         axis=-1).astype(BF16)
    br = jnp.concatenate([b_router_grp, b_router_exp.reshape(DEPTH, N_EXPERTS),
                          jnp.zeros((DEPTH, 128 - N_GROUPS - N_EXPERTS), F32)], axis=-1).reshape(DEPTH, 1, 128)
    wa, wb, wo = w_branch_a.astype(BF16), w_branch_b.astype(BF16), w_out.astype(BF16)
    wpg, wpp = w_ple_gate.astype(BF16), w_ple_proj.astype(BF16)
    gains = lambda g: g.reshape(DEPTH, 1, -1)

    xc = x.reshape(T, D)
    for i in range(DEPTH):
        main, d0, d1, d2 = _in_proj(xc, gains(g_mix), w_in_p, tab_dil, B, S, TM_IN, i)
        q, k, v = _mla_qkv(main, gains(g_q_lat), gains(g_kv_lat), wq, wk, wv, tab_mla, TM_QKV, i)
        o_a = _mla_attn(q, k, v, B, S, TQ)
        ogs, lss = [], []
        for (_, dil), qkv in zip(DIL_PATTERN, (d0, d1, d2)):
            og, ls = _dil_attn(qkv, B, dil, S // dil)
            ogs.append(og)
            lss.append(ls)
        x1, h2, lg = _post(xc, o_a, ogs, lss, main, wa, wb, wo, gains(g_ffn), wr, br, B, S, TM_POST, i)
        ri, rw, cnt = _route(lg, TM_ROUTE)
        dest, n_blocks, ffn_plans = _dispatch_plan(ri, cnt, T, TM_ROUTE)
        range_blocks = n_blocks // FFN_RANGES
        ys = None
        for r, plan in enumerate(ffn_plans):
            xs = _sc_gather_rows(h2.reshape(T, ROW_TILE, 128), dest, n_out=range_blocks * FFN_BM,
                                 row0=r * range_blocks * FFN_BM)
            ys = _expert_ffn(*plan, xs.reshape(-1, 128), ys, r * range_blocks, n_blocks,
                             w_exp_gate, w_exp_up, w_exp_down, i)
        xc = None
        for first in (0, T // 2):
            yg = _sc_gather_rows(ys.reshape(-1, ROW_TILE, 128), dest[first * TOP_K:(first + T // 2) * TOP_K])
            xc = _combine_ple(yg.reshape(-1, 128), xc, first, x1, rw, p.reshape(DEPTH * T, PLE_DIM), gains(g_ple),
                              wpg, wpp, g_final.reshape(1, D), TM_COMB, i)
    return xc.reshape(B, S, D)
```

```python
import functools
import math

import jax
import jax.numpy as jnp
import numpy as np
from jax import lax
from jax.experimental import pallas as pl
from jax.experimental.pallas import tpu as pltpu
from jax.experimental.pallas import tpu_sc as plsc

F32 = jnp.float32
BF16 = jnp.bfloat16

D_MODEL = 1024
DEPTH = 4
RMS_EPS = 1e-6
NEG = -1e30
LOG2E = math.log2(math.e)

MLA_HEADS = 16
MLA_Q_LORA = 512
MLA_KV_LORA = 256
MLA_NOPE = 64
MLA_ROPE = 32
MLA_V = 64
MLA_ROPE_THETA = 10000.0
MLA_SLOT = 128
MLA_HALF = MLA_ROPE // 2


def _mla_head_lanes(lane, head):
    l = lane % MLA_SLOT
    nope = jnp.logical_and(lane // MLA_SLOT == head, l < MLA_NOPE)
    rope = jnp.logical_and(l >= MLA_NOPE + head * MLA_HALF, l < MLA_NOPE + (head + 1) * MLA_HALF)
    return jnp.logical_or(nope, rope)

DIL_PATTERN = ((128, 1), (512, 4), (2048, 16))
DIL_GROUPS = 3
DIL_HEADS = 4
DIL_HEAD_DIM = 64
DIL_W = DIL_HEADS * DIL_HEAD_DIM
DIL_BLK = 128
DIL_UNROLL = 16
ROPE_THETA = 500000.0
PARTIAL_ROT = DIL_HEAD_DIM // 4
DIL_HALF = PARTIAL_ROT // 2
DIL_ROT_LANES = DIL_HEADS * DIL_HALF
DIL_REST = (DIL_HEAD_DIM - PARTIAL_ROT) // 2


def _dil_qk_head_of_lane(lane):
    l = lane % 128
    plain = l - DIL_ROT_LANES
    plain_head = sum((plain >= h * DIL_REST).astype(jnp.int32) for h in range(1, DIL_HEADS))
    return jnp.where(l < DIL_ROT_LANES, l // DIL_HALF, plain_head)

N_GROUPS = 8
EXPERTS_PER_GROUP = 8
N_EXPERTS = 64
TOP_K = 2
EXPERT_FF = 256
PLE_DIM = 256

IN_TN = 768
MAIN_COLS = 3072
IN_COLS_PAD = MAIN_COLS + DIL_GROUPS * 3 * DIL_W
N_MAIN_TILES = MAIN_COLS // IN_TN

FFN_BM = 256
FFN_RANGES = 2
ROW_TILE = D_MODEL // 256
U32 = jnp.uint32
VMEM_LIMIT = 48 * 1024 * 1024
IN_VMEM_LIMIT = 58 * 1024 * 1024
TM_IN, TM_QKV, TQ, TM_POST, TM_ROUTE, TM_COMB = 1024, 1024, 512, 512, 1024, 1024

SC_CORES = 2
SC_SUBCORES = 16
SC_LANES = 16
SC_CHUNK = 64


def _cparams(sem):
    return pltpu.CompilerParams(dimension_semantics=sem, vmem_limit_bytes=VMEM_LIMIT)


def _layer_spec(shape, layer, col=None):
    def index(*grid_idx):
        return (layer,) + (0,) * (len(shape) - 1) + ((col(*grid_idx),) if col else (0,))

    return pl.BlockSpec((None,) + tuple(shape), index)


def _rms(x, g):
    return x * lax.rsqrt(jnp.mean(x * x, axis=-1, keepdims=True) + RMS_EPS) * g


def _in_kernel(x_ref, g_ref, w_ref, tab_ref, main_ref, d0_ref, d1_ref, d2_ref, acc_ref):
    xn = _rms(x_ref[...], g_ref[...]).astype(BF16)

    def dil_tile(acc, out_ref, dil):
        chunk = lambda c: acc[:, c * 128:(c + 1) * 128]
        for base, t0 in ((0, 0), (2, 3)):
            cos, sin_n, sin_p = [tab_ref[:, (t0 + t) * 128:(t0 + t + 1) * 128] for t in range(3)]
            lo, hi = chunk(base), chunk(base + 1)
            acc_ref[base] = lo * cos + hi * sin_n
            acc_ref[base + 1] = hi * cos + lo * sin_p
        for c in range(4, 6):
            acc_ref[c] = chunk(c)
        rows = acc_ref.shape[1] // dil
        for r in range(dil):
            for c in range(6):
                out_ref[0, r, :, c * 128:(c + 1) * 128] = acc_ref[c, pl.ds(r, rows, stride=dil), :].astype(BF16)

    for j in range(IN_COLS_PAD // IN_TN):
        acc = jnp.dot(xn, w_ref[:, j * IN_TN:(j + 1) * IN_TN], preferred_element_type=F32)
        if j < N_MAIN_TILES:
            main_ref[:, j * IN_TN:(j + 1) * IN_TN] = acc.astype(BF16)
        else:
            gi = j - N_MAIN_TILES
            dil_tile(acc, (d0_ref, d1_ref, d2_ref)[gi], DIL_PATTERN[gi][1])


def _in_proj(x2d, g, w, tab, B, S, tm, layer):
    T = x2d.shape[0]
    nt = S // tm
    dil_shapes = [jax.ShapeDtypeStruct((B, d, S // d, 3 * DIL_W), BF16) for _, d in DIL_PATTERN]
    dil_specs = [pl.BlockSpec((1, d, tm // d, 3 * DIL_W), lambda i, nt=nt: (i // nt, 0, i % nt, 0))
                 for _, d in DIL_PATTERN]
    return pl.pallas_call(
        _in_kernel,
        grid=(T // tm,),
        in_specs=[
            pl.BlockSpec((tm, D_MODEL), lambda i: (i, 0)),
            _layer_spec((1, D_MODEL), layer),
            pl.BlockSpec((None, D_MODEL, IN_COLS_PAD), lambda i: (layer, 0, 0), pipeline_mode=pl.Buffered(1)),
            pl.BlockSpec((tm, 6 * 128), lambda i: (i, 0)),
        ],
        out_specs=[pl.BlockSpec((tm, MAIN_COLS), lambda i: (i, 0))] + dil_specs,
        out_shape=[jax.ShapeDtypeStruct((T, MAIN_COLS), BF16)] + dil_shapes,
        scratch_shapes=[pltpu.VMEM((IN_TN // 128, tm, 128), F32)],
        compiler_params=pltpu.CompilerParams(dimension_semantics=("parallel",), vmem_limit_bytes=IN_VMEM_LIMIT),
        name="in_proj",
    )(x2d, g, w, tab)


def _qkv_kernel(cq_ref, ckv_ref, kpe_ref, gq_ref, gkv_ref, wq_ref, wk_ref, wv_ref, tab_ref,
                q_ref, k_ref, v_ref):
    qn = _rms(cq_ref[...].astype(F32), gq_ref[...]).astype(BF16)
    kvn = _rms(ckv_ref[...].astype(F32), gkv_ref[...]).astype(BF16)
    qacc = jnp.dot(qn, wq_ref[...], preferred_element_type=F32)
    kacc = jnp.dot(kvn, wk_ref[...], preferred_element_type=F32)
    vacc = jnp.dot(kvn, wv_ref[...], preferred_element_type=F32)
    lane_v = lax.broadcasted_iota(jnp.int32, vacc.shape, 1)
    v_ref[...] = jnp.where(lane_v % MLA_SLOT < MLA_V, vacc, 1.0).astype(BF16)
    cq, snq, spq, ck, snk, spk = [tab_ref[:, t * 128:(t + 1) * 128] for t in range(6)]
    kpe_lo = kpe_ref[:, 0:128].astype(F32)
    kpe_hi = kpe_ref[:, 128:256].astype(F32)
    krot = (kpe_lo * ck + kpe_hi * snk, kpe_hi * ck + kpe_lo * spk)
    for pair in range(MLA_HEADS // 2):
        s0 = slice(2 * pair * MLA_SLOT, (2 * pair + 1) * MLA_SLOT)
        s1 = slice((2 * pair + 1) * MLA_SLOT, (2 * pair + 2) * MLA_SLOT)
        q0, q1 = qacc[:, s0], qacc[:, s1]
        q_ref[:, s0] = (q0 * cq + q1 * snq).astype(BF16)
        q_ref[:, s1] = (q1 * cq + q0 * spq).astype(BF16)
        k_ref[:, s0] = (kacc[:, s0] + krot[0]).astype(BF16)
        k_ref[:, s1] = (kacc[:, s1] + krot[1]).astype(BF16)


def _mla_qkv(main, gq, gkv, wq, wk, wv, tab, tm, layer):
    T = main.shape[0]
    return pl.pallas_call(
        _qkv_kernel,
        grid=(T // tm,),
        in_specs=[
            pl.BlockSpec((tm, MLA_Q_LORA), lambda i: (i, 2048 // MLA_Q_LORA)),
            pl.BlockSpec((tm, MLA_KV_LORA), lambda i: (i, 2560 // MLA_KV_LORA)),
            pl.BlockSpec((tm, 256), lambda i: (i, 2816 // 256)),
            _layer_spec((1, MLA_Q_LORA), layer),
            _layer_spec((1, MLA_KV_LORA), layer),
            _layer_spec((MLA_Q_LORA, MLA_HEADS * MLA_SLOT), layer),
            _layer_spec((MLA_KV_LORA, MLA_HEADS * MLA_SLOT), layer),
            _layer_spec((MLA_KV_LORA, MLA_HEADS * MLA_SLOT), layer),
            pl.BlockSpec((tm, 6 * 128), lambda i: (i, 0)),
        ],
        out_specs=[
            pl.BlockSpec((tm, MLA_HEADS * MLA_SLOT), lambda i: (i, 0)),
            pl.BlockSpec((tm, MLA_HEADS * MLA_SLOT), lambda i: (i, 0)),
            pl.BlockSpec((tm, MLA_HEADS * MLA_SLOT), lambda i: (i, 0)),
        ],
        out_shape=[
            jax.ShapeDtypeStruct((T, MLA_HEADS * MLA_SLOT), BF16),
            jax.ShapeDtypeStruct((T, MLA_HEADS * MLA_SLOT), BF16),
            jax.ShapeDtypeStruct((T, MLA_HEADS * MLA_SLOT), BF16),
        ],
        compiler_params=_cparams(("parallel",)),
        name="mla_qkv",
    )(main, main, main, gq, gkv, wq, wk, wv, tab)


def _mla_attn_kernel(qa_ref, qb_ref, k_ref, v_ref, o_ref, m_ref, acc_ref, *, tq):
    p_id = pl.program_id(2)
    nq = k_ref.shape[1] // tq
    lane = lax.broadcasted_iota(jnp.int32, (tq, 2 * MLA_SLOT), 1)
    qh = []
    for q_ref in (qa_ref, qb_ref):
        qp = q_ref[0]
        zero = jnp.zeros_like(qp)
        qh.append([jnp.where(_mla_head_lanes(lane, h), qp, zero) for h in range(2)])
    hq = tq // 2
    dn = (((1,), (1,)), ((), ()))

    def step(blk, start, width, r0, mask, first=False):
        kb = k_ref[0, pl.ds(start, width), :]
        vb = v_ref[0, pl.ds(start, width), :]
        rows = tq - r0
        s_pair = lax.dot_general(jnp.concatenate([qh[blk][0][r0:], qh[blk][1][r0:]], axis=0), kb, dn,
                                 preferred_element_type=F32)
        for h in range(2):
            s = s_pair[h * rows:(h + 1) * rows]
            if mask is not None:
                s = jnp.where(mask, s, NEG)
            m_cur = jnp.max(s, axis=-1, keepdims=True)
            if first:
                m_new = jnp.broadcast_to(m_cur, (tq - r0, 128))
            else:
                m_prev = m_ref[blk, h, r0:, :]
                m_new = jnp.maximum(m_prev, m_cur)
            p = jnp.exp2(s - jnp.concatenate([m_new] * (width // 128), axis=-1))
            pv = jnp.dot(p.astype(BF16), vb[:, h * MLA_SLOT:(h + 1) * MLA_SLOT], preferred_element_type=F32)
            if first:
                acc_ref[blk, h, r0:, :] = pv
            else:
                acc_ref[blk, h, r0:, :] = jnp.exp2(m_prev - m_new) * acc_ref[blk, h, r0:, :] + pv
            m_ref[blk, h, r0:, :] = m_new

    def causal(rows):
        return lax.broadcasted_iota(jnp.int32, (rows, hq), 1) <= lax.broadcasted_iota(jnp.int32, (rows, hq), 0)

    def q_blocks(n_a):
        n_b = nq - 1 - n_a
        for blk, n_full in ((1, n_b), (0, n_a)):
            j = 0
            while j < n_full:
                wide = 2 if j + 1 < n_full else 1
                step(blk, j * tq, wide * tq, 0, None, first=j == 0)
                j += wide
            step(blk, n_full * tq, hq, 0, causal(tq), first=n_full == 0)
            step(blk, n_full * tq + hq, hq, hq, causal(hq))

    for n_a in range(nq // 2):
        pl.when(p_id == n_a)(functools.partial(q_blocks, n_a))
    lane_o = lax.broadcasted_iota(jnp.int32, (tq, 2 * MLA_V), 1)
    for blk in range(2):
        outs = []
        for h in range(2):
            a = acc_ref[blk, h]
            outs.append(a / pltpu.roll(a, MLA_V, 1))
        o_ref[0, blk, 0] = jnp.where(lane_o < MLA_V, outs[0], pltpu.roll(outs[1], MLA_V, 1)).astype(BF16)


def _mla_attn(q, k, v, B, S, tq):
    nq = S // tq
    q = q.reshape(B, S, MLA_HEADS * MLA_SLOT)
    k = k.reshape(B, S, MLA_HEADS * MLA_SLOT)
    v = v.reshape(B, S, MLA_HEADS * MLA_SLOT)
    return pl.pallas_call(
        functools.partial(_mla_attn_kernel, tq=tq),
        grid=(B, MLA_HEADS // 2, nq // 2),
        in_specs=[
            pl.BlockSpec((1, tq, 2 * MLA_SLOT), lambda b, h, p: (b, p, h)),
            pl.BlockSpec((1, tq, 2 * MLA_SLOT), lambda b, h, p: (b, nq - 1 - p, h)),
            pl.BlockSpec((1, S, 2 * MLA_SLOT), lambda b, h, p: (b, 0, h)),
            pl.BlockSpec((1, S, 2 * MLA_SLOT), lambda b, h, p: (b, 0, h)),
        ],
        out_specs=pl.BlockSpec((1, 2, 1, tq, 2 * MLA_V), lambda b, h, p: (b, 0, p, 0, h)),
        out_shape=jax.ShapeDtypeStruct((B, 2, nq // 2, tq, MLA_HEADS * MLA_V), BF16),
        scratch_shapes=[pltpu.VMEM((2, 2, tq, 128), F32), pltpu.VMEM((2, 2, tq, MLA_SLOT), F32)],
        compiler_params=_cparams(("parallel", "parallel", "arbitrary")),
        name="mla_attn",
    )(q, q, k, v)


def _dil_kernel(qkv_ref, o_ref, lse_ref, bias_ref, *, nb):
    dil = qkv_ref.shape[1]
    width = 2 * DIL_BLK if nb > 1 else DIL_BLK
    row = lax.broadcasted_iota(jnp.int32, (DIL_BLK, width), 0)
    col = lax.broadcasted_iota(jnp.int32, (DIL_BLK, width), 1)
    bias_ref[0] = jnp.where(col <= row, 0.0, NEG)
    if nb > 1:
        later = jnp.logical_or(jnp.logical_and(col >= DIL_BLK, col - DIL_BLK <= row),
                               jnp.logical_and(col < DIL_BLK, col >= row))
        bias_ref[1] = jnp.where(later, 0.0, NEG)
    lane = lax.broadcasted_iota(jnp.int32, (DIL_BLK, DIL_W), 1)
    head_of_lane = lane // DIL_HEAD_DIM
    qk_head_of_lane = _dil_qk_head_of_lane(lane)
    dn = (((1,), (1,)), ((), ()))

    def by_head(parts):
        out = parts[DIL_HEADS - 1]
        for h in range(DIL_HEADS - 2, -1, -1):
            out = jnp.where(head_of_lane == h, parts[h], out)
        return out

    def unit(u):
        r = u // nb
        n = u % nb
        q0 = pl.multiple_of(n * DIL_BLK, DIL_BLK)
        q = qkv_ref[0, r, pl.ds(q0, DIL_BLK), 0:DIL_W]
        if nb > 1:
            w0 = pl.multiple_of(jnp.maximum(n - 1, 0) * DIL_BLK, DIL_BLK)
            bias = bias_ref[jnp.minimum(n, 1)]
        else:
            w0 = 0
            bias = bias_ref[0]
        kw = qkv_ref[0, r, pl.ds(w0, width), DIL_W:2 * DIL_W]
        vw = qkv_ref[0, r, pl.ds(w0, width), 2 * DIL_W:3 * DIL_W]
        zero = jnp.zeros_like(q)
        qs = jnp.concatenate([jnp.where(qk_head_of_lane == h, q, zero) for h in range(DIL_HEADS)], axis=0)
        s = lax.dot_general(qs, kw, dn, preferred_element_type=F32)
        s = (s.reshape(DIL_HEADS, DIL_BLK, width) + bias[None]).reshape(DIL_HEADS * DIL_BLK, width)
        m = jnp.max(s, axis=-1, keepdims=True)
        e = jnp.exp2(s - m)
        den = jnp.sum(e, axis=-1, keepdims=True)
        pv = jnp.dot(e.astype(BF16), vw, preferred_element_type=F32)
        lse = m + jnp.log2(den)
        blk = lambda t, h: t[h * DIL_BLK:(h + 1) * DIL_BLK]
        o = by_head([blk(pv, h) for h in range(DIL_HEADS)]) / by_head(
            [jnp.broadcast_to(blk(den, h), (DIL_BLK, DIL_W)) for h in range(DIL_HEADS)])
        o_ref[0, r, pl.ds(q0, DIL_BLK), :] = o.astype(BF16)
        lse_ref[0, r, pl.ds(q0, DIL_BLK), :] = by_head(
            [jnp.broadcast_to(blk(lse, h), (DIL_BLK, DIL_W)) for h in range(DIL_HEADS)])

    def body(t, carry):
        for u in range(DIL_UNROLL):
            unit(DIL_UNROLL * t + u)
        return carry

    lax.fori_loop(0, dil * nb // DIL_UNROLL, body, 0)


def _dil_attn(qkv, B, dil, L):
    nb = L // DIL_BLK
    return pl.pallas_call(
        functools.partial(_dil_kernel, nb=nb),
        grid=(B,),
        in_specs=[pl.BlockSpec((1, dil, L, 3 * DIL_W), lambda b: (b, 0, 0, 0))],
        out_specs=[
            pl.BlockSpec((1, dil, L, DIL_W), lambda b: (b, 0, 0, 0)),
            pl.BlockSpec((1, dil, L, DIL_W), lambda b: (b, 0, 0, 0)),
        ],
        out_shape=[
            jax.ShapeDtypeStruct((B, dil, L, DIL_W), BF16),
            jax.ShapeDtypeStruct((B, dil, L, DIL_W), F32),
        ],
        scratch_shapes=[pltpu.VMEM((2, DIL_BLK, 2 * DIL_BLK if nb > 1 else DIL_BLK), F32)],
        compiler_params=_cparams(("parallel",)),
        name=f"dil_attn_d{dil}",
    )(qkv)


def _post_kernel(x_ref, oa_ref, og0_ref, og1_ref, og2_ref, ls0_ref, ls1_ref, ls2_ref, gates_ref,
                 wa_ref, wb_ref, wo_ref, gffn_ref, wr_ref, br_ref,
                 x1_ref, h2_ref, lg_ref, ob_ref, *, tm):
    ls =[r[0] for r in (ls0_ref, ls1_ref, ls2_ref)]
    og = [r[0] for r in (og0_ref, og1_ref, og2_ref)]
    for gi, (_, dil) in enumerate(DIL_PATTERN):
        rows = tm // dil
        for r in range(dil):
            for c in range(2):
                sl = slice(c * 128, (c + 1) * 128)
                ob_ref[2 * gi + c, pl.ds(r, rows, stride=dil), :] = og[gi][r][:, sl].astype(F32)
                ob_ref[6 + 2 * gi + c, pl.ds(r, rows, stride=dil), :] = ls[gi][r][:, sl]

    def tok_major(k):
        return jnp.concatenate([ob_ref[2 * k], ob_ref[2 * k + 1]], axis=-1)

    l0, l1, l2 = tok_major(3), tok_major(4), tok_major(5)
    mx = jnp.maximum(jnp.maximum(l0, l1), l2)
    w0, w1, w2 = jnp.exp2(l0 - mx), jnp.exp2(l1 - mx), jnp.exp2(l2 - mx)
    ob = (w0 * tok_major(0) + w1 * tok_major(1) + w2 * tok_major(2)) / (w0 + w1 + w2)
    ya = jnp.dot(oa_ref[0, 0, 0], wa_ref[...], preferred_element_type=F32)
    yb = jnp.dot(ob.astype(BF16), wb_ref[...], preferred_element_type=F32)
    merged = (jax.nn.sigmoid(gates_ref[:, 0:D_MODEL].astype(F32)) * ya
              + jax.nn.sigmoid(gates_ref[:, D_MODEL:2 * D_MODEL].astype(F32)) * yb)
    x1 = x_ref[...] + jnp.dot(merged.astype(BF16), wo_ref[...], preferred_element_type=F32)
    x1_ref[...] = x1
    h2 = _rms(x1, gffn_ref[...])
    _rows_to_tiles(h2_ref, h2)
    lg_ref[...] = jnp.dot(h2.astype(BF16), wr_ref[...], preferred_element_type=F32) + br_ref[...]


def _post(x2d, oa, ogs, lss, main, wa, wb, wo, gffn, wr, br, B, S, tm, layer):
    T = x2d.shape[0]
    nt = S // tm
    res_specs = [pl.BlockSpec((1, d, tm // d, DIL_W), lambda i, nt=nt: (i // nt, 0, i % nt, 0))
                 for _, d in DIL_PATTERN]
    assert oa.shape[3] == tm and oa.shape[2] * 2 == nt

    def oa_index(i):
        it = i % nt
        late = it >= nt // 2
        return (i // nt, late.astype(jnp.int32), jnp.where(late, nt - 1 - it, it), 0, 0)

    return pl.pallas_call(
        functools.partial(_post_kernel, tm=tm),
        grid=(T // tm,),
        in_specs=[pl.BlockSpec((tm, D_MODEL), lambda i: (i, 0)),
                  pl.BlockSpec((1, 1, 1, tm, D_MODEL), oa_index)]
                 + res_specs + res_specs
                 + [pl.BlockSpec((tm, 2 * D_MODEL), lambda i: (i, 0)),
                    _layer_spec((D_MODEL, D_MODEL), layer),
                    _layer_spec((DIL_W, D_MODEL), layer),
                    _layer_spec((D_MODEL, D_MODEL), layer),
                    _layer_spec((1, D_MODEL), layer),
                    _layer_spec((D_MODEL, 128), layer),
                    _layer_spec((1, 128), layer)],
        out_specs=[pl.BlockSpec((tm, D_MODEL), lambda i: (i, 0)),
                   pl.BlockSpec((tm * ROW_TILE, 128), lambda i: (i, 0)),
                   pl.BlockSpec((tm, 128), lambda i: (i, 0))],
        out_shape=[jax.ShapeDtypeStruct((T, D_MODEL), F32),
                   jax.ShapeDtypeStruct((T * ROW_TILE, 128), U32),
                   jax.ShapeDtypeStruct((T, 128), F32)],
        scratch_shapes=[pltpu.VMEM((12, tm, 128), F32)],
        compiler_params=_cparams(("parallel",)),
        name="post_attn",
    )(x2d, oa, *ogs, *lss, main, wa, wb, wo, gffn, wr, br)


def _route_kernel(lg_ref, ri_ref, rw_ref, cnt_ref, carry_ref, lower_ref, *, tm):
    i = pl.program_id(0)

    @pl.when(i == 0)
    def _():
        carry_ref[...] = jnp.zeros(carry_ref.shape, F32)
        r_i = lax.broadcasted_iota(jnp.int32, (tm, tm), 0)
        c_i = lax.broadcasted_iota(jnp.int32, (tm, tm), 1)
        lower_ref[...] = jnp.where(c_i < r_i, 1.0, 0.0).astype(BF16)

    lg = lg_ref[...]
    lane = lax.broadcasted_iota(jnp.int32, lg.shape, 1)
    lane_f = lane.astype(F32)
    ninf = jnp.float32(-jnp.inf)

    def first_max(vals):
        vmax = jnp.max(vals, axis=-1, keepdims=True)
        idx = jnp.min(jnp.where(vals == vmax, lane_f, 128.0), axis=-1, keepdims=True)
        return vmax, idx.astype(jnp.int32)

    gl = jnp.where(lane < N_GROUPS, lg, ninf)
    gmax, g_sel = first_max(gl)
    p_g = 1.0 / jnp.sum(jnp.exp(gl - gmax), axis=-1, keepdims=True)
    lo = N_GROUPS + g_sel * EXPERTS_PER_GROUP
    el = jnp.where(jnp.logical_and(lane >= lo, lane < lo + EXPERTS_PER_GROUP), lg, ninf)
    v0, i0 = first_max(el)
    v1, i1 = first_max(jnp.where(lane == i0, ninf, el))
    t = jnp.exp(v1 - v0)
    w0 = p_g / (1.0 + t)
    w1 = p_g * t / (1.0 + t)
    e0 = i0 - N_GROUPS
    e1 = i1 - N_GROUPS
    hit0 = lane == e0
    hit1 = lane == e1
    oh = jnp.where(jnp.logical_or(hit0, hit1), 1.0, 0.0).astype(F32)
    excl = jnp.dot(lower_ref[...], oh.astype(BF16), preferred_element_type=F32) + carry_ref[...]
    r0 = jnp.sum(jnp.where(hit0, excl, 0.0), axis=-1, keepdims=True).astype(jnp.int32)
    r1 = jnp.sum(jnp.where(hit1, excl, 0.0), axis=-1, keepdims=True).astype(jnp.int32)
    carry_ref[...] = carry_ref[...] + jnp.sum(oh, axis=0, keepdims=True)
    zi = jnp.zeros(lg.shape, jnp.int32)
    ri_ref[...] = jnp.where(lane == 0, e0, jnp.where(lane == 1, e1, jnp.where(lane == 2, r0, jnp.where(lane == 3, r1, zi))))
    rw_ref[...] = jnp.where(lane == 0, w0, jnp.where(lane == 1, w1, jnp.zeros(lg.shape, F32)))
    cnt_ref[...] = carry_ref[...]


def _route(lg, tm):
    T = lg.shape[0]
    return pl.pallas_call(
        functools.partial(_route_kernel, tm=tm),
        grid=(T // tm,),
        in_specs=[pl.BlockSpec((tm, 128), lambda i: (i, 0))],
        out_specs=[pl.BlockSpec((tm, 128), lambda i: (i, 0)),
                   pl.BlockSpec((tm, 128), lambda i: (i, 0)),
                   pl.BlockSpec((1, 128), lambda i: (0, 0))],
        out_shape=[jax.ShapeDtypeStruct((T, 128), jnp.int32),
                   jax.ShapeDtypeStruct((T, 128), F32),
                   jax.ShapeDtypeStruct((1, 128), F32)],
        scratch_shapes=[pltpu.VMEM((1, 128), F32), pltpu.VMEM((tm, tm), BF16)],
        compiler_params=_cparams(("arbitrary",)),
        name="route",
    )(lg)


def _sc_gather_rows(table, idx, n_out=None, row0=0):
    inverse = n_out is not None
    n_src = idx.shape[0]
    n = n_out if inverse else n_src
    n_workers = SC_CORES * SC_SUBCORES
    per_w = n // n_workers
    n_chunks = per_w // SC_CHUNK
    assert per_w * n_workers == n and n_chunks * SC_CHUNK == per_w and n_chunks % 2 == 0
    mesh = plsc.VectorSubcoreMesh(core_axis_name="c", subcore_axis_name="s",
                                  num_cores=SC_CORES, num_subcores=SC_SUBCORES)

    @functools.partial(
        pl.kernel, mesh=mesh,
        out_type=jax.ShapeDtypeStruct((n,) + table.shape[1:], table.dtype),
        scratch_types=[pltpu.VMEM((per_w,), jnp.int32),
                       pltpu.VMEM((SC_CHUNK,) + table.shape[1:], table.dtype),
                       pltpu.VMEM((SC_CHUNK,) + table.shape[1:], table.dtype),
                       pltpu.SemaphoreType.DMA, pltpu.SemaphoreType.DMA,
                       pltpu.VMEM((n_src if inverse else SC_LANES,), jnp.int32)],
        compiler_params=pltpu.CompilerParams(use_tc_tiling_on_sc=True, needs_layout_passes=not inverse),
        name="sc_dispatch_rows" if inverse else "sc_gather_rows",
    )
    def gather(table_hbm, idx_hbm, out_hbm, idx_v, rows_a, rows_b, sem_a, sem_b, map_v):
        wid = lax.axis_index("s") * SC_CORES + lax.axis_index("c")
        base = wid * per_w
        if inverse:
            pltpu.sync_copy(idx_hbm, map_v)
            lanes = lax.iota(jnp.int32, SC_LANES)

            @pl.loop(0, per_w // SC_LANES)
            def _(j):
                idx_v[pl.ds(j * SC_LANES, SC_LANES)] = lax.rem(row0 + base + j * SC_LANES + lanes, table.shape[0])

            @pl.loop(0, n_src // SC_LANES)
            def _(a):
                local = map_v[pl.ds(a * SC_LANES, SC_LANES)] - (row0 + base)
                mine = jnp.logical_and(local >= 0, local < per_w)
                plsc.store_scatter(idx_v, [local], lax.div(a * SC_LANES + lanes, TOP_K), mask=mine)
        else:
            pltpu.sync_copy(idx_hbm.at[pl.ds(base, per_w)], idx_v)

        def fetch(chunk, rows_v, sem):
            return pltpu.make_async_copy(table_hbm.at[idx_v.at[pl.ds(chunk * SC_CHUNK, SC_CHUNK)]], rows_v, sem)

        def flush(chunk, rows_v):
            pltpu.sync_copy(rows_v, out_hbm.at[pl.ds(base + chunk * SC_CHUNK, SC_CHUNK)])

        fetch(0, rows_a, sem_a).start()

        @pl.loop(0, n_chunks, step=2)
        def _(c):
            fetch(c + 1, rows_b, sem_b).start()
            fetch(c, rows_a, sem_a).wait()
            flush(c, rows_a)

            @pl.when(c + 2 < n_chunks)
            def _():
                fetch(c + 2, rows_a, sem_a).start()

            fetch(c + 1, rows_b, sem_b).wait()
            flush(c + 1, rows_b)

    return gather(table, idx)


def _rows_from_tiles(ref, first, n_rows, stride=ROW_TILE):
    words = [ref[pl.ds(first + c, n_rows, stride=stride), :] for c in range(ROW_TILE)]
    lo = [pltpu.bitcast(w << 16, F32) for w in words]
    hi = [pltpu.bitcast(w & U32(0xFFFF0000), F32) for w in words]
    return jnp.concatenate(lo + hi, axis=-1)


def _rows_to_tiles(ref, val):
    half = D_MODEL // 2

    def bits(x):
        return pltpu.bitcast(x.astype(BF16).astype(F32), U32)

    for c in range(ROW_TILE):
        lo = bits(val[:, c * 128:(c + 1) * 128])
        hi = bits(val[:, half + c * 128:half + (c + 1) * 128])
        ref[pl.ds(c, val.shape[0], stride=ROW_TILE), :] = hi | (lo >> 16)


def _ffn_kernel(be_ref, nx_ref, par_ref, nvb_ref, x_ref, w1_hbm, w3_hbm, w2_hbm, prev_ref, y_ref,
                w1f_ref, w3f_ref, w2f_ref, w1b_ref, w3b_ref, w2b_ref, sem, *, layer):
    b = pl.program_id(0)
    nvb = nvb_ref[0]
    expert = be_ref[b]
    slot = par_ref[b]
    new_expert = jnp.logical_or(b == 0, expert != be_ref[jnp.maximum(b - 1, 0)])

    def fetch(e, sl):
        return [pltpu.make_async_copy(w_hbm.at[layer, e], wf_ref.at[sl], sem.at[sl])
                for w_hbm, wf_ref in ((w1_hbm, w1f_ref), (w3_hbm, w3f_ref), (w2_hbm, w2f_ref))]

    @pl.when(jnp.logical_and(b == 0, nvb > 0))
    def _():
        for cp in fetch(expert, slot):
            cp.start()

    @pl.when(jnp.logical_and(b < nvb, new_expert))
    def _():
        for cp in fetch(expert, slot):
            cp.wait()

        @pl.when(nx_ref[b] >= 0)
        def _():
            for cp in fetch(nx_ref[b], 1 - slot):
                cp.start()

        w1b_ref[...] = w1f_ref[slot].astype(BF16)
        w3b_ref[...] = w3f_ref[slot].astype(BF16)
        w2b_ref[...] = w2f_ref[slot].astype(BF16)

    @pl.when(b < nvb)
    def _():
        xb = _rows_from_tiles(x_ref, 0, FFN_BM).astype(BF16)
        h1 = jnp.dot(xb, w1b_ref[...], preferred_element_type=F32)
        h3 = jnp.dot(xb, w3b_ref[...], preferred_element_type=F32)
        a = (jax.nn.silu(h1) * h3).astype(BF16)
        _rows_to_tiles(y_ref, jnp.dot(a, w2b_ref[...], preferred_element_type=F32))

    @pl.when(b >= nvb)
    def _():
        y_ref[...] = jnp.zeros(y_ref.shape, U32)


def _expert_ffn(block_e, next_e, parity, nvb, xs, prev, first, n_blocks, w1, w3, w2, layer):
    nb = block_e.shape[0]
    if prev is None:
        prev = xs

    def used(b, *prefetch):
        return jnp.minimum(b, jnp.maximum(prefetch[-1][0] - 1, 0))

    return pl.pallas_call(
        functools.partial(_ffn_kernel, layer=layer),
        grid_spec=pltpu.PrefetchScalarGridSpec(
            num_scalar_prefetch=4,
            grid=(nb,),
            in_specs=[
                pl.BlockSpec((FFN_BM * ROW_TILE, 128), lambda b, *prefetch: (used(b, *prefetch), 0)),
                pl.BlockSpec(memory_space=pl.ANY),
                pl.BlockSpec(memory_space=pl.ANY),
                pl.BlockSpec(memory_space=pl.ANY),
                pl.BlockSpec(memory_space=pl.ANY),
            ],
            out_specs=pl.BlockSpec((FFN_BM * ROW_TILE, 128), lambda b, *prefetch: (first + b, 0)),
            scratch_shapes=[pltpu.VMEM((2, D_MODEL, EXPERT_FF), F32), pltpu.VMEM((2, D_MODEL, EXPERT_FF), F32),
                            pltpu.VMEM((2, EXPERT_FF, D_MODEL), F32),
                            pltpu.VMEM((D_MODEL, EXPERT_FF), BF16), pltpu.VMEM((D_MODEL, EXPERT_FF), BF16),
                            pltpu.VMEM((EXPERT_FF, D_MODEL), BF16),
                            pltpu.SemaphoreType.DMA((2,))],
        ),
        out_shape=jax.ShapeDtypeStruct((n_blocks * FFN_BM * ROW_TILE, 128), U32),
        input_output_aliases={} if first == 0 else {8: 0},
        compiler_params=_cparams(("arbitrary",)),
        name="expert_ffn",
    )(block_e, next_e, parity, nvb, xs, w1, w3, w2, prev)


def _comb_kernel(yg_ref, x1_ref, rw_ref, p_ref, gple_ref, wpg_ref, wpp_ref, gout_ref, prev_ref, o_ref, *, tm, final):
    w = rw_ref[...]
    y0 = _rows_from_tiles(yg_ref, 0, tm, stride=TOP_K * ROW_TILE)
    y1 = _rows_from_tiles(yg_ref, ROW_TILE, tm, stride=TOP_K * ROW_TILE)
    x2 = x1_ref[...] + (y0 * w[:, 0:1] + y1 * w[:, 1:2])
    e = jnp.dot(p_ref[...].astype(BF16), wpp_ref[...], preferred_element_type=F32)
    gate = jax.nn.sigmoid(jnp.dot(_rms(x2, gple_ref[...]).astype(BF16), wpg_ref[...], preferred_element_type=F32))
    x3 = x2 + gate * e
    o_ref[...] = _rms(x3, gout_ref[...]) if final else x3


def _combine_ple(yg, prev, first, x1, rw, p2d, gple, wpg, wpp, gout, tm, layer):
    T = x1.shape[0]
    nt = T // tm
    off = first // tm
    if prev is None:
        prev = x1
    return pl.pallas_call(
        functools.partial(_comb_kernel, tm=tm, final=layer == DEPTH - 1),
        grid=(yg.shape[0] // (tm * TOP_K * ROW_TILE),),
        in_specs=[
            pl.BlockSpec((tm * TOP_K * ROW_TILE, 128), lambda i: (i, 0)),
            pl.BlockSpec((tm, D_MODEL), lambda i: (off + i, 0)),
            pl.BlockSpec((tm, 128), lambda i: (off + i, 0)),
            pl.BlockSpec((tm, PLE_DIM), lambda i: (layer * nt + off + i, 0)),
            _layer_spec((1, D_MODEL), layer),
            _layer_spec((D_MODEL, D_MODEL), layer),
            _layer_spec((PLE_DIM, D_MODEL), layer),
            pl.BlockSpec((1, D_MODEL), lambda i: (0, 0)),
            pl.BlockSpec(memory_space=pl.ANY),
        ],
        out_specs=pl.BlockSpec((tm, D_MODEL), lambda i: (off + i, 0)),
        out_shape=jax.ShapeDtypeStruct((T, D_MODEL), F32),
        input_output_aliases={} if first == 0 else {8: 0},
        compiler_params=_cparams(("parallel",)),
        name="combine_ple",
    )(yg, x1, rw, p2d, gple, wpg, wpp, gout, prev)


def _rope_tables(pos, theta, rot_dim, offset, span, scale):
    half = rot_dim // 2
    inv = jnp.float32(theta) ** (-jnp.arange(half, dtype=F32) * 2.0 / rot_dim)
    ang = pos.astype(F32)[:, None] * inv
    rel = np.arange(128) - offset
    rot = (rel >= 0) & (rel < span)
    spread = np.zeros((half, 128), np.float32)
    spread[rel[rot] % half, np.nonzero(rot)[0]] = 1.0
    to_lanes = lambda t: jnp.dot(t, spread, precision=lax.Precision.HIGHEST)
    cos, sin = to_lanes(jnp.cos(ang)) + (~rot).astype(np.float32), to_lanes(jnp.sin(ang))
    return jnp.concatenate([cos, -sin, sin], axis=1) * jnp.float32(scale)


def _w_in_runs(n_cols):
    src = _w_in_columns(np.arange(1, n_cols + 1, dtype=np.int64), np) - 1
    runs, start = [], 0
    for c in range(1, len(src) + 1):
        prev = src[c - 1]
        if c == len(src) or (src[c] != prev + 1 if prev >= 0 else src[c] >= 0):
            runs.append((start, int(src[start]), c - start))
            start = c
    out = []
    for dst, s, n in runs:
        for o in range(0, n, 512):
            out.append((dst + o, s + o if s >= 0 else -1, min(512, n - o)))
    return out


def _w_layout_kernel(w_ref, o_ref, f_ref, *, runs):
    rows = f_ref.shape[0]
    for dst, src, n in runs:
        if src < 0:
            f_ref[:, dst:dst + n] = jnp.zeros((rows, n), F32)
        else:
            f_ref[:, dst:dst + n] = w_ref[:, src:src + n]
    o_ref[...] = f_ref[...].astype(BF16)


def _prep_w_in(w, tr=256):
    n_cols = w.shape[-1]
    return pl.pallas_call(
        functools.partial(_w_layout_kernel, runs=_w_in_runs(n_cols)),
        grid=(DEPTH, D_MODEL // tr),
        in_specs=[pl.BlockSpec((None, tr, n_cols), lambda l, i: (l, i, 0))],
        out_specs=pl.BlockSpec((None, tr, IN_COLS_PAD), lambda l, i: (l, i, 0)),
        out_shape=jax.ShapeDtypeStruct((DEPTH, D_MODEL, IN_COLS_PAD), BF16),
        scratch_shapes=[pltpu.VMEM((tr, IN_COLS_PAD), F32)],
        compiler_params=_cparams(("parallel", "parallel")),
        name="w_in_layout",
    )(w)


def _w_in_columns(w, jnp):
    c_q = w[..., 0:512]
    c_kv = w[..., 512:768]
    k_pe = w[..., 768:800]
    dil = w[..., 800:800 + 2304].reshape(w.shape[:-1] + (DIL_GROUPS, 3, DIL_HEADS, DIL_HEAD_DIM))

    def qk_layout(t):
        flat = lambda u: u.reshape(u.shape[:-2] + (-1,))
        return jnp.concatenate([flat(t[..., :DIL_HALF]), flat(t[..., PARTIAL_ROT:PARTIAL_ROT + DIL_REST]),
                                flat(t[..., DIL_HALF:PARTIAL_ROT]), flat(t[..., PARTIAL_ROT + DIL_REST:])], axis=-1)

    dil = jnp.concatenate([jnp.concatenate([qk_layout(dil[..., g, 0, :, :]), qk_layout(dil[..., g, 1, :, :]),
                                            dil[..., g, 2, :, :].reshape(w.shape[:-1] + (DIL_W,))], axis=-1)
                           for g in range(DIL_GROUPS)], axis=-1)
    gates = w[..., 3104:5152]
    zeros = lambda n: jnp.zeros(w.shape[:-1] + (n,), w.dtype)
    lo, hi = k_pe[..., :MLA_HALF], k_pe[..., MLA_HALF:]
    pad = MLA_SLOT - MLA_NOPE - MLA_ROPE
    kslot = jnp.concatenate([zeros(MLA_NOPE), lo, lo, zeros(pad), zeros(MLA_NOPE), hi, hi, zeros(pad)], axis=-1)
    return jnp.concatenate([gates, c_q, c_kv, kslot, dil], axis=-1)


def _pad_heads(w, width):
    w = jnp.pad(w, ((0, 0), (0, 0), (0, 0), (0, MLA_SLOT - width)))
    return w.reshape(w.shape[0], w.shape[1], MLA_HEADS * MLA_SLOT).astype(BF16)


def _prep_w_q(w):
    w = w.reshape(DEPTH, MLA_Q_LORA, MLA_HEADS // 2, 2, MLA_NOPE + MLA_ROPE)
    nope = w[..., :MLA_NOPE]
    lo = w[..., MLA_NOPE:MLA_NOPE + MLA_HALF]
    hi = w[..., MLA_NOPE + MLA_HALF:]
    both = lambda t: t.reshape(t.shape[:-2] + (2 * MLA_HALF,))
    zeros = jnp.zeros(w.shape[:3] + (MLA_SLOT - MLA_NOPE - MLA_ROPE,), w.dtype)
    slot0 = jnp.concatenate([nope[..., 0, :], both(lo), zeros], axis=-1)
    slot1 = jnp.concatenate([nope[..., 1, :], both(hi), zeros], axis=-1)
    return jnp.stack([slot0, slot1], axis=-2).reshape(DEPTH, MLA_Q_LORA, MLA_HEADS * MLA_SLOT).astype(BF16)


def _prep_w_kv(w):
    w = w.reshape(DEPTH, MLA_KV_LORA, MLA_HEADS, MLA_NOPE + MLA_V)
    return _pad_heads(w[..., :MLA_NOPE], MLA_NOPE), _pad_heads(w[..., MLA_NOPE:], MLA_V)


def _dest_kernel(ri_ref, ps_ref, o_ref):
    ri = ri_ref[...].astype(F32)
    lane = lax.broadcasted_iota(jnp.int32, ri.shape, 1)
    ps = ps_ref[...]

    def col(k):
        return jnp.sum(jnp.where(lane == k, ri, 0.0), axis=-1, keepdims=True)

    def dest(k):
        start = jnp.sum(jnp.where(lane == col(k).astype(jnp.int32), ps, 0.0), axis=-1, keepdims=True)
        return (start + col(2 + k)).astype(jnp.int32)

    o_ref[...] = jnp.where(lane == 0, dest(0), jnp.where(lane == 1, dest(1), 0))


def _dest_rows(ri, pstarts, tm):
    T = ri.shape[0]
    ps = jnp.zeros((1, 128), F32).at[0, :N_EXPERTS].set(pstarts.astype(F32))
    return pl.pallas_call(
        _dest_kernel,
        grid=(T // tm,),
        in_specs=[pl.BlockSpec((tm, 128), lambda i: (i, 0)), pl.BlockSpec((1, 128), lambda i: (0, 0))],
        out_specs=pl.BlockSpec((tm, 128), lambda i: (i, 0)),
        out_shape=jax.ShapeDtypeStruct((T, 128), jnp.int32),
        compiler_params=_cparams(("parallel",)),
        name="dest_rows",
    )(ri, ps)


def _dispatch_plan(ri, cnt, T, tm):
    counts = cnt[0, :N_EXPERTS].astype(jnp.int32)
    pcounts = (counts + FFN_BM - 1) // FFN_BM * FFN_BM
    pends = jnp.cumsum(pcounts)
    pstarts = pends - pcounts
    dest = _dest_rows(ri, pstarts, tm)[:, 0:TOP_K]
    n_blocks = (T * TOP_K) // FFN_BM + N_EXPERTS
    first_row = jnp.arange(n_blocks, dtype=jnp.int32) * FFN_BM
    block_e = jnp.minimum(jnp.sum((pends[None, :] <= first_row[:, None]).astype(jnp.int32), axis=1), N_EXPERTS - 1)
    nvb = pends[-1] // FFN_BM
    after = pends // FFN_BM
    run_of_expert = jnp.cumsum((pcounts > 0).astype(jnp.int32)) - 1
    parity = (run_of_expert[block_e] % 2).astype(jnp.int32)
    per_range = n_blocks // FFN_RANGES
    ffn_plans = []
    for r in range(FFN_RANGES):
        lo, hi = r * per_range, (r + 1) * per_range
        end = jnp.minimum(nvb, hi)
        next_of_expert = jnp.where(after < end, block_e[jnp.minimum(after, n_blocks - 1)], -1)
        ffn_plans.append((block_e[lo:hi], next_of_expert[block_e[lo:hi]].astype(jnp.int32), parity[lo:hi],
                          jnp.clip(nvb - lo, 0, per_range).astype(jnp.int32).reshape(1)))
    return dest.reshape(T * TOP_K), n_blocks, ffn_plans


def kernel(x, p, positions, g_mix, w_in, g_q_lat, w_q_up, g_kv_lat, w_kv_up, w_branch_a, w_branch_b, w_out, g_ffn, w_router_grp, b_router_grp, w_router_exp, b_router_exp, w_exp_gate, w_exp_up, w_exp_down, g_ple, w_ple_gate, w_ple_proj, g_final):
    B, S, D = x.shape
    T = B * S
    pos = positions.reshape(T)
    dil_scale = DIL_HEAD_DIM ** -0.5 * LOG2E
    mla_scale = (MLA_NOPE + MLA_ROPE) ** -0.5 * LOG2E
    tab_dil = jnp.concatenate([_rope_tables(pos, ROPE_THETA, PARTIAL_ROT, 0, DIL_ROT_LANES, sc)
                               for sc in (dil_scale, 1.0)], axis=1)
    tab_mla = jnp.concatenate([_rope_tables(pos, MLA_ROPE_THETA, MLA_ROPE, MLA_NOPE, MLA_ROPE, sc)
                               for sc in (mla_scale, 1.0)], axis=1)

    w_in_p = _prep_w_in(w_in)
    wq = _prep_w_q(w_q_up)
    wk, wv = _prep_w_kv(w_kv_up)
    wr = jnp.concatenate([w_router_grp, w_router_exp, jnp.zeros((DEPTH, D, 128 - N_GROUPS - N_EXPERTS), F32)],
                         axis=-1).astype(BF16)
    br = jnp.concatenate([b_router_grp, b_router_exp.reshape(DEPTH, N_EXPERTS),
                          jnp.zeros((DEPTH, 128 - N_GROUPS - N_EXPERTS), F32)], axis=-1).reshape(DEPTH, 1, 128)
    wa, wb, wo = w_branch_a.astype(BF16), w_branch_b.astype(BF16), w_out.astype(BF16)
    wpg, wpp = w_ple_gate.astype(BF16), w_ple_proj.astype(BF16)
    gains = lambda g: g.reshape(DEPTH, 1, -1)

    xc = x.reshape(T, D)
    for i in range(DEPTH):
        main, d0, d1, d2 = _in_proj(xc, gains(g_mix), w_in_p, tab_dil, B, S, TM_IN, i)
        q, k, v = _mla_qkv(main, gains(g_q_lat), gains(g_kv_lat), wq, wk, wv, tab_mla, TM_QKV, i)
        o_a = _mla_attn(q, k, v, B, S, TQ)
        ogs, lss = [], []
        for (_, dil), qkv in zip(DIL_PATTERN, (d0, d1, d2)):
            og, ls = _dil_attn(qkv, B, dil, S // dil)
            ogs.append(og)
            lss.append(ls)
        x1, h2, lg = _post(xc, o_a, ogs, lss, main, wa, wb, wo, gains(g_ffn), wr, br, B, S, TM_POST, i)
        ri, rw, cnt = _route(lg, TM_ROUTE)
        dest, n_blocks, ffn_plans = _dispatch_plan(ri, cnt, T, TM_ROUTE)
        range_blocks = n_blocks // FFN_RANGES
        ys = None
        for r, plan in enumerate(ffn_plans):
            xs = _sc_gather_rows(h2.reshape(T, ROW_TILE, 128), dest, n_out=range_blocks * FFN_BM,
                                 row0=r * range_blocks * FFN_BM)
            ys = _expert_ffn(*plan, xs.reshape(-1, 128), ys, r * range_blocks, n_blocks,
                             w_exp_gate, w_exp_up, w_exp_down, i)
        xc = None
        for first in (0, T // 2):
            yg = _sc_gather_rows(ys.reshape(-1, ROW_TILE, 128), dest[first * TOP_K:(first + T // 2) * TOP_K])
            xc = _combine_ple(yg.reshape(-1, 128), xc, first, x1, rw, p.reshape(DEPTH * T, PLE_DIM), gains(g_ple),
                              wpg, wpp, g_final.reshape(1, D), TM_COMB, i)
    return xc.reshape(B, S, D)
```
